```python
import math
import numpy as np
import jax
import jax.numpy as jnp
from jax import lax

D_MODEL = 1024
BATCH = 16
SEQ = 2048
DEPTH = 2

EXPAND = 2
MIX_WIDTH = EXPAND * D_MODEL
HG_WIDTH = MIX_WIDTH // 2
HG_HEAD_DIM = 128
HG_HEADS = HG_WIDTH // HG_HEAD_DIM
HG_CHUNK = 64
ATT_WIDTH = MIX_WIDTH - HG_WIDTH
ATT_HEAD_DIM = 64
ATT_HEADS = ATT_WIDTH // ATT_HEAD_DIM
ATT_KV_HEADS = max(1, ATT_HEADS // 8)
ATT_GROUP = ATT_HEADS // ATT_KV_HEADS
KV_WIDTH = ATT_KV_HEADS * ATT_HEAD_DIM
WINDOW = 128
ATT_BLOCK = 128
ATT_SCALE = 1.0 / math.sqrt(ATT_HEAD_DIM)
ROPE_THETA = 10000.0
NORM_EPS = 1e-6
NEG_INF = -1e30
LB_FLOOR = 1e-20

SPLIT_SIZES = (HG_WIDTH, HG_WIDTH, HG_WIDTH, HG_WIDTH, ATT_WIDTH, KV_WIDTH, KV_WIDTH, ATT_WIDTH)
IN_WIDTH = int(sum(SPLIT_SIZES))
SPLIT_POINTS = tuple(int(v) for v in np.cumsum(SPLIT_SIZES)[:-1])

kernel_name = "hymba_hgrn2_swa_sink_hybrid"


def rms_norm(x, g):
    xf = x.astype(jnp.float32)
    y = xf * lax.rsqrt(jnp.mean(xf * xf, axis=-1, keepdims=True) + NORM_EPS)
    return (y * g.astype(jnp.float32)).astype(x.dtype)


def rotary(x, pos):
    half = x.shape[-1] // 2
    inv_freq = ROPE_THETA ** (-jnp.arange(half, dtype=jnp.float32) / half)
    ang = pos.astype(jnp.float32)[:, None] * inv_freq[None, :]
    cos = jnp.cos(ang)[None, :, None, :]
    sin = jnp.sin(ang)[None, :, None, :]
    xf = x.astype(jnp.float32)
    x1, x2 = xf[..., :half], xf[..., half:]
    return jnp.concatenate([x1 * cos - x2 * sin, x2 * cos + x1 * sin], axis=-1).astype(x.dtype)


def hgrn2(q, f_logit, i, lb):
    B, S, _ = q.shape
    C, H, D = HG_CHUNK, HG_HEADS, HG_HEAD_DIM
    nC = S // C
    lb = lb.astype(jnp.float32)
    qf = jax.nn.silu(q.astype(jnp.float32))
    logf = jnp.logaddexp(jnp.log(jnp.maximum(lb, LB_FLOOR)),
                         jnp.log1p(-lb) + jax.nn.log_sigmoid(f_logit.astype(jnp.float32)))
    k = -jnp.expm1(logf)

    def chunks(t):
        return t.reshape(B, nC, C, H, D).transpose(1, 0, 3, 2, 4)

    qc, kc, vc, gc = chunks(qf), chunks(k), chunks(i.astype(jnp.float32)), chunks(logf)
    bc = jnp.cumsum(gc, axis=3)
    causal = jnp.tril(jnp.ones((C, C), dtype=bool))[None, None, :, :, None]

    def step(state, xs):
        qch, kch, vch, b = xs
        b_last = b[:, :, C - 1:C, :]
        diff = b[:, :, :, None, :] - b[:, :, None, :, :]
        decay = jnp.where(causal, jnp.exp(jnp.where(causal, diff, 0.0)), 0.0)
        scores = jnp.einsum('bhtd,bhsd,bhtsd->bhts', qch, kch, decay)
        o = jnp.einsum('bhts,bhsv->bhtv', scores, vch)
        o = o + jnp.einsum('bhtd,bhdv->bhtv', qch * jnp.exp(b), state)
        state = jnp.exp(b_last[:, :, 0, :])[..., None] * state + \
            jnp.einsum('bhsd,bhsv->bhdv', kch * jnp.exp(b_last - b), vch)
        return state, o

    s0 = jnp.zeros((B, H, D, D), dtype=jnp.float32)
    _, o = lax.scan(step, s0, (qc, kc, vc, bc))
    return o.transpose(1, 0, 3, 2, 4).reshape(B, S, H, D)


def sliding_window_attention(q, k, v, sinks):
    B, S = q.shape[0], q.shape[1]
    L, KV, G, D = ATT_BLOCK, ATT_KV_HEADS, ATT_GROUP, ATT_HEAD_DIM
    nB = S // L
    qb = q.reshape(B, nB, L, KV, G, D)
    kb = k.reshape(B, nB, L, KV, D)
    vb = v.reshape(B, nB, L, KV, D)
    k_prev = jnp.concatenate([jnp.zeros_like(kb[:, :1]), kb[:, :-1]], axis=1)
    v_prev = jnp.concatenate([jnp.zeros_like(vb[:, :1]), vb[:, :-1]], axis=1)
    kw = jnp.concatenate([k_prev, kb], axis=2)
    vw = jnp.concatenate([v_prev, vb], axis=2)
    s = jnp.einsum('bnqkgd,bnskd->bnkgqs', qb, kw).astype(jnp.float32) * ATT_SCALE
    qpos = jnp.arange(L)[:, None] + L
    kpos = jnp.arange(2 * L)[None, :]
    diff = qpos - kpos
    band = (diff >= 0) & (diff < WINDOW)
    key_exists = (jnp.arange(nB)[:, None] * L - L + jnp.arange(2 * L)[None, :]) >= 0
    mask = band[None, :, :] & key_exists[:, None, :]
    s = jnp.where(mask[None, :, None, None, :, :], s, NEG_INF)
    sink = jnp.broadcast_to(sinks.astype(jnp.float32).reshape(KV, G)[None, None, :, :, None, None],
                            s.shape[:-1] + (1,))
    p = jax.nn.softmax(jnp.concatenate([s, sink], axis=-1), axis=-1)[..., :-1]
    o = jnp.einsum('bnkgqs,bnskd->bnqkgd', p.astype(v.dtype), vw)
    return o.reshape(B, S, KV * G * D)


def hybrid_layer(x, w_in, w_out, g_pre, g_post, lb, g_head, sinks):
    B, S, _ = x.shape
    h = rms_norm(x, g_pre)
    proj = jnp.einsum('bsd,de->bse', h, w_in)
    q_h, f_h, i_h, z_h, q_a, k_a, v_a, z_a = jnp.split(proj, SPLIT_POINTS, axis=-1)

    o_h = hgrn2(q_h, f_h, i_h, lb)
    o_h = rms_norm(o_h, g_head).reshape(B, S, HG_WIDTH).astype(z_h.dtype) * jax.nn.silu(z_h)

    pos = jnp.arange(S)
    q_a = rotary(q_a.reshape(B, S, ATT_HEADS, ATT_HEAD_DIM), pos)
    k_a = rotary(k_a.reshape(B, S, ATT_KV_HEADS, ATT_HEAD_DIM), pos)
    v_a = v_a.reshape(B, S, ATT_KV_HEADS, ATT_HEAD_DIM)
    o_a = sliding_window_attention(q_a, k_a, v_a, sinks).astype(z_a.dtype) * jax.nn.silu(z_a)

    y = jnp.einsum('bse,ed->bsd', jnp.concatenate([o_h, o_a], axis=-1), w_out)
    return x + rms_norm(y, g_post)


def _fwd_setup_inputs(seed: int = 0) -> dict:
    key = jax.random.key(seed)
    ks = jax.random.split(key, 9)
    x = jax.random.normal(ks[0], (BATCH, SEQ, D_MODEL), dtype=jnp.float32)
    w_in = jax.random.normal(ks[1], (DEPTH, D_MODEL, IN_WIDTH), dtype=jnp.float32) * D_MODEL ** -0.5
    w_out = jax.random.normal(ks[2], (DEPTH, MIX_WIDTH, D_MODEL), dtype=jnp.float32) * MIX_WIDTH ** -0.5
    g_pre = 1.0 + 0.05 * jax.random.normal(ks[3], (DEPTH, D_MODEL), dtype=jnp.float32)
    g_post = 1.0 + 0.05 * jax.random.normal(ks[4], (DEPTH, D_MODEL), dtype=jnp.float32)
    lb_param = 0.1 * jax.random.normal(ks[5], (DEPTH, HG_WIDTH), dtype=jnp.float32)
    g_head = 1.0 + 0.05 * jax.random.normal(ks[6], (DEPTH, HG_HEAD_DIM), dtype=jnp.float32)
    sinks = jax.random.normal(ks[7], (DEPTH, ATT_HEADS), dtype=jnp.float32)
    return {"x": x, "w_in": w_in, "w_out": w_out, "g_pre": g_pre, "g_post": g_post,
            "lb_param": lb_param, "g_head": g_head, "sinks": sinks}


def _fwd_reference(x, w_in, w_out, g_pre, g_post, lb_param, g_head, sinks):
    p = jax.nn.softmax(lb_param.astype(jnp.float32), axis=0)
    lower_bounds = jnp.cumsum(p, axis=0) - p[0:1]
    for l in range(DEPTH):
        x = hybrid_layer(x, w_in[l], w_out[l], g_pre[l], g_post[l], lower_bounds[l], g_head[l], sinks[l])
    return x


import jax as _jax
import jax.numpy as _jnp

TWIN_FORMAT = 'train_step'
FWD_PARAMS = ['x', 'w_in', 'w_out', 'g_pre', 'g_post', 'lb_param', 'g_head', 'sinks']
TWIN_WEIGHTS = ['w_in', 'w_out', 'g_pre', 'g_post', 'lb_param', 'g_head', 'sinks']
TWIN_DIFF_INPUT = 'x'
TWIN_INPUTS = ['x', 'w_in', 'w_out', 'g_pre', 'g_post', 'lb_param', 'g_head', 'sinks', 'loss_target', 'm_w_in', 'm_w_out', 'm_g_pre', 'm_g_post', 'm_lb_param', 'm_g_head', 'm_sinks', 'v_w_in', 'v_w_out', 'v_g_pre', 'v_g_post', 'v_lb_param', 'v_g_head', 'v_sinks']
TWIN_OUTPUTS = ['loss', 'grad_x', 'grad_w_in', 'grad_w_out', 'grad_g_pre', 'grad_g_post', 'grad_lb_param', 'grad_g_head', 'grad_sinks', 'delta_w_in', 'delta_w_out', 'delta_g_pre', 'delta_g_post', 'delta_lb_param', 'delta_g_head', 'delta_sinks', 'new_m_w_in', 'new_m_w_out', 'new_m_g_pre', 'new_m_g_post', 'new_m_lb_param', 'new_m_g_head', 'new_m_sinks', 'new_v_w_in', 'new_v_w_out', 'new_v_g_pre', 'new_v_g_post', 'new_v_lb_param', 'new_v_g_head', 'new_v_sinks']
TWIN_LEAF_KINDS = {'loss': 'loss', 'grad_x': 'grad_x', 'grad_w_in': 'grad_w', 'grad_w_out': 'grad_w', 'grad_g_pre': 'grad_w', 'grad_g_post': 'grad_w', 'grad_lb_param': 'grad_w', 'grad_g_head': 'grad_w', 'grad_sinks': 'grad_w', 'delta_w_in': 'delta_w', 'delta_w_out': 'delta_w', 'delta_g_pre': 'delta_w', 'delta_g_post': 'delta_w', 'delta_lb_param': 'delta_w', 'delta_g_head': 'delta_w', 'delta_sinks': 'delta_w', 'new_m_w_in': 'new_m', 'new_m_w_out': 'new_m', 'new_m_g_pre': 'new_m', 'new_m_g_post': 'new_m', 'new_m_lb_param': 'new_m', 'new_m_g_head': 'new_m', 'new_m_sinks': 'new_m', 'new_v_w_in': 'new_v', 'new_v_w_out': 'new_v', 'new_v_g_pre': 'new_v', 'new_v_g_post': 'new_v', 'new_v_lb_param': 'new_v', 'new_v_g_head': 'new_v', 'new_v_sinks': 'new_v'}


def _forward(args):
    return _fwd_reference(*[args[k] for k in FWD_PARAMS])


def _output_shape():
    out = _jax.eval_shape(lambda: _forward(_fwd_setup_inputs(0)))
    return out.shape, out.dtype

N_MICROBATCH = 1
ADAM_LR = 0.001
ADAM_B1 = 0.9
ADAM_B2 = 0.999
ADAM_EPS = 1e-08
ADAM_WD = 0.01
ADAM_STEP = 10
PER_EXAMPLE_BATCH_AXIS = {'x': 0, 'loss_target': 0}
SHARED_INPUTS = []
_WEIGHT_DTYPES = {'w_in': _jnp.float32, 'w_out': _jnp.float32, 'g_pre': _jnp.float32, 'g_post': _jnp.float32, 'lb_param': _jnp.float32, 'g_head': _jnp.float32, 'sinks': _jnp.float32}
MOMENT_SCALE = {'w_in': 2.344485e-01, 'w_out': 4.377171e-01, 'g_pre': 6.409858e-01, 'g_post': 3.200629e+01, 'lb_param': 2.700329e-02, 'g_head': 1.266625e+00, 'sinks': 6.325545e-02}


def _to_microbatches(a, axis):
    t = _jnp.moveaxis(a, axis, 0)
    t = t.reshape((N_MICROBATCH, t.shape[0] // N_MICROBATCH) + t.shape[1:])
    return _jnp.moveaxis(t, 1, axis + 1)


def setup_inputs(seed: int = 0) -> dict:
    inp = _fwd_setup_inputs(seed)
    key = _jax.random.fold_in(_jax.random.key(seed), 7919)
    shape, _ = _output_shape()
    out = dict(inp)
    out["loss_target"] = _jax.random.normal(_jax.random.fold_in(key, 0), shape, _jnp.float32)
    for i, name in enumerate(TWIN_WEIGHTS):
        w = inp[name].astype(_jnp.float32)
        if MOMENT_SCALE is None:
            s = _jnp.sqrt(_jnp.mean(_jnp.square(w)) + 1e-30)
        else:
            s = MOMENT_SCALE[name]
        km, kv = _jax.random.split(_jax.random.fold_in(key, i + 1))
        out[name] = w
        out["m_" + name] = s * _jax.random.normal(km, w.shape, _jnp.float32)
        out["v_" + name] = (s * s) * _jax.random.uniform(kv, w.shape, _jnp.float32, 0.5, 1.5)
    if N_MICROBATCH > 1:
        for name, axis in PER_EXAMPLE_BATCH_AXIS.items():
            out[name] = _to_microbatches(out[name], axis)
    return {'x': out['x'], 'w_in': out['w_in'], 'w_out': out['w_out'], 'g_pre': out['g_pre'], 'g_post': out['g_post'], 'lb_param': out['lb_param'], 'g_head': out['g_head'], 'sinks': out['sinks'], 'loss_target': out['loss_target'], 'm_w_in': out['m_w_in'], 'm_w_out': out['m_w_out'], 'm_g_pre': out['m_g_pre'], 'm_g_post': out['m_g_post'], 'm_lb_param': out['m_lb_param'], 'm_g_head': out['m_g_head'], 'm_sinks': out['m_sinks'], 'v_w_in': out['v_w_in'], 'v_w_out': out['v_w_out'], 'v_g_pre': out['v_g_pre'], 'v_g_post': out['v_g_post'], 'v_lb_param': out['v_lb_param'], 'v_g_head': out['v_g_head'], 'v_sinks': out['v_sinks']}


def _loss(weights, diff, rest, loss_target):
    with _jax.named_scope("forward"):
        args = {**rest, TWIN_DIFF_INPUT: diff, **{k: w.astype(_WEIGHT_DTYPES[k]) for k, w in weights.items()}}
        y = _forward(args)
    with _jax.named_scope("loss_head"):
        err = _jnp.square(y.astype(_jnp.float32) - loss_target)
        return 0.5 * _jnp.sum(_jnp.mean(err, axis=-1)) if err.ndim else 0.5 * err


def _adamw(w, g, m, v):
    m = ADAM_B1 * m + (1.0 - ADAM_B1) * g
    v = ADAM_B2 * v + (1.0 - ADAM_B2) * _jnp.square(g)
    m_hat = m / (1.0 - ADAM_B1 ** ADAM_STEP)
    v_hat = v / (1.0 - ADAM_B2 ** ADAM_STEP)
    delta = -ADAM_LR * (m_hat / (_jnp.sqrt(v_hat) + ADAM_EPS) + ADAM_WD * w)
    return delta, m, v


def reference(x, w_in, w_out, g_pre, g_post, lb_param, g_head, sinks, loss_target, m_w_in, m_w_out, m_g_pre, m_g_post, m_lb_param, m_g_head, m_sinks, v_w_in, v_w_out, v_g_pre, v_g_post, v_lb_param, v_g_head, v_sinks):
    given = dict(x=x, w_in=w_in, w_out=w_out, g_pre=g_pre, g_post=g_post, lb_param=lb_param, g_head=g_head, sinks=sinks, loss_target=loss_target, m_w_in=m_w_in, m_w_out=m_w_out, m_g_pre=m_g_pre, m_g_post=m_g_post, m_lb_param=m_lb_param, m_g_head=m_g_head, m_sinks=m_sinks, v_w_in=v_w_in, v_w_out=v_w_out, v_g_pre=v_g_pre, v_g_post=v_g_post, v_lb_param=v_lb_param, v_g_head=v_g_head, v_sinks=v_sinks)
    weights = {n: given[n] for n in TWIN_WEIGHTS}
    shared = {n: given[n] for n in SHARED_INPUTS}
    per_example = {n: given[n] for n in ['x']}
    grad_fn = _jax.value_and_grad(_loss, argnums=(0, 1))

    def one_microbatch(ex, loss_target):
        ex = dict(ex)
        diff = ex.pop(TWIN_DIFF_INPUT)
        return grad_fn(weights, diff, {**shared, **ex}, loss_target)

    if N_MICROBATCH == 1:
        loss, (grad_w, grad_x) = one_microbatch(per_example, given["loss_target"])
    else:
        def body(carry, xs):
            loss_sum, grad_sum = carry
            l_k, (gw_k, gx_k) = one_microbatch(xs[0], xs[1])
            with _jax.named_scope("update"):
                return (loss_sum + l_k, _jax.tree.map(_jnp.add, grad_sum, gw_k)), gx_k

        init = (_jnp.zeros((), _jnp.float32), _jax.tree.map(_jnp.zeros_like, weights))
        (loss, grad_w), grad_x = _jax.lax.scan(body, init, (per_example, given["loss_target"]))
    with _jax.named_scope("update"):
        delta_w, new_m, new_v = {}, {}, {}
        for n in TWIN_WEIGHTS:
            delta_w[n], new_m[n], new_v[n] = _adamw(weights[n], grad_w[n], given["m_" + n], given["v_" + n])
    return (loss, grad_x, *[grad_w[n] for n in TWIN_WEIGHTS], *[delta_w[n] for n in TWIN_WEIGHTS],
            *[new_m[n] for n in TWIN_WEIGHTS], *[new_v[n] for n in TWIN_WEIGHTS])
```

```python
import functools
import math

import jax
import jax.numpy as jnp
import numpy as np
from jax import lax
from jax.experimental import pallas as pl
from jax.experimental.pallas import tpu as pltpu

f32 = jnp.float32
bf16 = jnp.bfloat16

D_MODEL = 1024
DEPTH = 2
HG_WIDTH = 1024
HG_HEAD_DIM = 128
HG_HEADS = 8
CHUNK = 64
SUB = 16
ATT_WIDTH = 1024
ATT_HEAD_DIM = 64
ATT_HEADS = 16
ATT_GROUP = 8
KV_WIDTH = 128
ATT_BLOCK = 128
ATT_SCALE = 1.0 / math.sqrt(ATT_HEAD_DIM)
ROPE_THETA = 10000.0
IN_WIDTH = 6400
MIX_WIDTH = 2048
NORM_EPS = 1e-6
NEG_INF = -1e30
LB_FLOOR = 1e-20
LANES = 128
VMEM_LIMIT = 48 * 1024 * 1024

ADAM_LR = 0.001
ADAM_B1 = 0.9
ADAM_B2 = 0.999
ADAM_EPS = 1e-08
ADAM_WD = 0.01
ADAM_STEP = 10

COL_PERM = np.concatenate([np.arange(0, 5120), np.arange(5376, 6400), np.arange(5120, 5376)])
COL_INV = np.argsort(COL_PERM)
QA_BLK, ZA_BLK, KV_BLK = 4, 5, 24

NT = (((1,), (1,)), ((), ()))
TN = (((0,), (0,)), ((), ()))


def _dot(a, b, dims=None, precision=None):
    if dims is None:
        return jnp.dot(a, b, preferred_element_type=f32, precision=precision)
    return lax.dot_general(a, b, dims, preferred_element_type=f32, precision=precision)


def _sigmoid(x):
    return 1.0 / (1.0 + jnp.exp(-x))


def _params(*sem):
    return pltpu.CompilerParams(dimension_semantics=sem, vmem_limit_bytes=VMEM_LIMIT)


def _in_proj(x, g, w, *, tm=512, tn=1280):
    T = x.shape[0]
    tm = min(tm, T)

    def body(x_ref, g_ref, w_ref, p_ref, h_ref, hs):
        @pl.when(pl.program_id(1) == 0)
        def _():
            xv = x_ref[...]
            r = lax.rsqrt(jnp.mean(xv * xv, axis=-1, keepdims=True) + NORM_EPS)
            hv = (xv * r * g_ref[...]).astype(bf16)
            hs[...] = hv
            h_ref[...] = hv
        p_ref[...] = _dot(hs[...], w_ref[...])

    return pl.pallas_call(
        body, name="in_proj", grid=(T // tm, IN_WIDTH // tn),
        in_specs=[pl.BlockSpec((tm, D_MODEL), lambda i, j: (i, 0)),
                  pl.BlockSpec((1, D_MODEL), lambda i, j: (0, 0)),
                  pl.BlockSpec((D_MODEL, tn), lambda i, j: (0, j))],
        out_specs=[pl.BlockSpec((tm, tn), lambda i, j: (i, j)),
                   pl.BlockSpec((tm, D_MODEL), lambda i, j: (i, 0))],
        out_shape=[jax.ShapeDtypeStruct((T, IN_WIDTH), f32), jax.ShapeDtypeStruct((T, D_MODEL), bf16)],
        scratch_shapes=[pltpu.VMEM((tm, D_MODEL), bf16)],
        compiler_params=_params("parallel", "arbitrary"),
    )(x, g, w)


def _out_proj(ch, ca, wo, x, g, *, tm=512):
    T = x.shape[0]
    tm = min(tm, T)
    half = MIX_WIDTH // 2

    def body(ch_ref, ca_ref, wo_ref, x_ref, g_ref, xn_ref, y_ref):
        y = _dot(ch_ref[...], wo_ref[0:half, :]) + _dot(ca_ref[...], wo_ref[half:MIX_WIDTH, :])
        r = lax.rsqrt(jnp.mean(y * y, axis=-1, keepdims=True) + NORM_EPS)
        y_ref[...] = y
        xn_ref[...] = x_ref[...] + y * r * g_ref[...]

    row = lambda i: (i, 0)
    fixed = lambda i: (0, 0)
    return pl.pallas_call(
        body, name="out_proj", grid=(T // tm,),
        in_specs=[pl.BlockSpec((tm, half), row), pl.BlockSpec((tm, half), row),
                  pl.BlockSpec((MIX_WIDTH, D_MODEL), fixed), pl.BlockSpec((tm, D_MODEL), row),
                  pl.BlockSpec((1, D_MODEL), fixed)],
        out_specs=[pl.BlockSpec((tm, D_MODEL), row), pl.BlockSpec((tm, D_MODEL), row)],
        out_shape=[jax.ShapeDtypeStruct((T, D_MODEL), f32)] * 2,
        compiler_params=_params("parallel"),
    )(ch, ca, wo, x, g)


def _loss_head(y, target, *, tm=512):
    T = y.shape[0]
    tm = min(tm, T)

    def body(y_ref, t_ref, d_ref, l_ref):
        @pl.when(pl.program_id(0) == 0)
        def _():
            l_ref[...] = jnp.zeros_like(l_ref)
        err = y_ref[...] - t_ref[...]
        d_ref[...] = err * (1.0 / D_MODEL)
        l_ref[...] += jnp.sum(err * err) * (0.5 / D_MODEL)

    row = lambda i: (i, 0)
    return pl.pallas_call(
        body, name="loss_head", grid=(T // tm,),
        in_specs=[pl.BlockSpec((tm, D_MODEL), row), pl.BlockSpec((tm, D_MODEL), row)],
        out_specs=[pl.BlockSpec((tm, D_MODEL), row), pl.BlockSpec((8, LANES), lambda i: (0, 0))],
        out_shape=[jax.ShapeDtypeStruct((T, D_MODEL), f32), jax.ShapeDtypeStruct((8, LANES), f32)],
        compiler_params=_params("arbitrary"),
    )(y, target)


def _out_proj_bwd(dxn, y, g, wo, ch, ca, *, tm=256):
    T = y.shape[0]
    tm = min(tm, T)
    half = MIX_WIDTH // 2

    def body(dx_ref, y_ref, g_ref, wo_ref, ch_ref, ca_ref, dch_ref, dca_ref, dwo_ref, dg_ref):
        @pl.when(pl.program_id(0) == 0)
        def _():
            dwo_ref[...] = jnp.zeros_like(dwo_ref)
            dg_ref[...] = jnp.zeros_like(dg_ref)
        y = y_ref[...]
        dx = dx_ref[...]
        r = lax.rsqrt(jnp.mean(y * y, axis=-1, keepdims=True) + NORM_EPS)
        gy = dx * g_ref[...]
        dy = r * gy - y * (r * r * r) * jnp.mean(gy * y, axis=-1, keepdims=True)
        dg_ref[...] += jnp.sum(dx * y * r, axis=0, keepdims=True)
        dyb = dy.astype(bf16)
        dch_ref[...] = _dot(dyb, wo_ref[0:half, :], NT)
        dca_ref[...] = _dot(dyb, wo_ref[half:MIX_WIDTH, :], NT)
        dwo_ref[0:half, :] += _dot(ch_ref[...], dyb, TN)
        dwo_ref[half:MIX_WIDTH, :] += _dot(ca_ref[...], dyb, TN)

    row = lambda i: (i, 0)
    fixed = lambda i: (0, 0)
    return pl.pallas_call(
        body, name="out_proj_bwd", grid=(T // tm,),
        in_specs=[pl.BlockSpec((tm, D_MODEL), row), pl.BlockSpec((tm, D_MODEL), row),
                  pl.BlockSpec((1, D_MODEL), fixed), pl.BlockSpec((MIX_WIDTH, D_MODEL), fixed),
                  pl.BlockSpec((tm, half), row), pl.BlockSpec((tm, half), row)],
        out_specs=[pl.BlockSpec((tm, half), row), pl.BlockSpec((tm, half), row),
                   pl.BlockSpec((MIX_WIDTH, D_MODEL), fixed), pl.BlockSpec((1, D_MODEL), fixed)],
        out_shape=[jax.ShapeDtypeStruct((T, half), f32), jax.ShapeDtypeStruct((T, half), f32),
                   jax.ShapeDtypeStruct((MIX_WIDTH, D_MODEL), f32), jax.ShapeDtypeStruct((1, D_MODEL), f32)],
        compiler_params=_params("arbitrary"),
    )(dxn, y, g, wo, ch, ca)


def _in_proj_bwd(dproj, w, x, g, dxn, *, tm=512, tk=1280):
    T = x.shape[0]
    tm = min(tm, T)
    nk = IN_WIDTH // tk

    def body(dp_ref, w_ref, x_ref, g_ref, dxn_ref, dx_ref, dg_ref, acc):
        i, k = pl.program_id(0), pl.program_id(1)

        @pl.when((i == 0) & (k == 0))
        def _():
            dg_ref[...] = jnp.zeros_like(dg_ref)

        @pl.when(k == 0)
        def _():
            acc[...] = jnp.zeros_like(acc)
        acc[...] += _dot(dp_ref[...], w_ref[...], NT)

        @pl.when(k == nk - 1)
        def _():
            dh = acc[...]
            xv = x_ref[...]
            r = lax.rsqrt(jnp.mean(xv * xv, axis=-1, keepdims=True) + NORM_EPS)
            gy = dh * g_ref[...]
            dx_ref[...] = dxn_ref[...] + r * gy - xv * (r * r * r) * jnp.mean(gy * xv, axis=-1, keepdims=True)
            dg_ref[...] += jnp.sum(dh * xv * r, axis=0, keepdims=True)

    return pl.pallas_call(
        body, name="in_proj_bwd", grid=(T // tm, nk),
        in_specs=[pl.BlockSpec((tm, tk), lambda i, k: (i, k)), pl.BlockSpec((D_MODEL, tk), lambda i, k: (0, k)),
                  pl.BlockSpec((tm, D_MODEL), lambda i, k: (i, 0)), pl.BlockSpec((1, D_MODEL), lambda i, k: (0, 0)),
                  pl.BlockSpec((tm, D_MODEL), lambda i, k: (i, 0))],
        out_specs=[pl.BlockSpec((tm, D_MODEL), lambda i, k: (i, 0)), pl.BlockSpec((1, D_MODEL), lambda i, k: (0, 0))],
        out_shape=[jax.ShapeDtypeStruct((T, D_MODEL), f32), jax.ShapeDtypeStruct((1, D_MODEL), f32)],
        scratch_shapes=[pltpu.VMEM((tm, D_MODEL), f32)],
        compiler_params=_params("arbitrary", "arbitrary"),
    )(dproj, w, x, g, dxn)


def _grad_w_in(h, dproj, *, tn=640, tk=1024):
    T = h.shape[0]
    tk = min(tk, T)

    def body(h_ref, dp_ref, o_ref):
        @pl.when(pl.program_id(1) == 0)
        def _():
            o_ref[...] = jnp.zeros_like(o_ref)
        o_ref[...] += _dot(h_ref[...], dp_ref[...], TN)

    return pl.pallas_call(
        body, name="grad_w_in", grid=(IN_WIDTH // tn, T // tk),
        in_specs=[pl.BlockSpec((tk, D_MODEL), lambda j, k: (k, 0)), pl.BlockSpec((tk, tn), lambda j, k: (k, j))],
        out_specs=pl.BlockSpec((D_MODEL, tn), lambda j, k: (0, j)),
        out_shape=jax.ShapeDtypeStruct((D_MODEL, IN_WIDTH), f32),
        compiler_params=_params("parallel", "arbitrary"),
    )(h, dproj)


def _lower_bound(lbp, layer):
    m = jnp.max(lbp, axis=0, keepdims=True)
    e = jnp.exp(lbp - m)
    p = e / jnp.sum(e, axis=0, keepdims=True)
    acc = p[0:1]
    for i in range(1, layer + 1):
        acc = acc + p[i:i + 1]
    return acc - p[0:1]


def _gate_parts(qr, fr, lb, lbf):
    sq = _sigmoid(qr)
    e = jnp.exp(-jnp.abs(fr))
    inv = 1.0 / (1.0 + e)
    pos = fr >= 0
    sg = jnp.where(pos, inv, e * inv)
    nsg = jnp.where(pos, e * inv, inv)
    fg = lbf + (1.0 - lb) * sg
    return qr * sq, sq, sg, nsg, fg, jnp.log(fg), (1.0 - lb) * nsg


def _anchor_masks():
    t = lax.broadcasted_iota(jnp.int32, (CHUNK, CHUNK), 0)
    s = lax.broadcasted_iota(jnp.int32, (CHUNK, CHUNK), 1)
    anchors = tuple(range(SUB - 1, CHUNK - 1, SUB))
    return anchors, [(t > a) & (s <= a) & (s > a - SUB) for a in anchors]


def _hgrn_fwd(proj, lb_param, g_head, *, B, S, layer):
    T = B * S
    TB = min(256, S)
    nT, NC = S // TB, TB // CHUNK
    nC = S // CHUNK
    HD = HG_HEAD_DIM

    def body(q_ref, f_ref, i_ref, z_ref, lb_ref, gh_ref, cat_ref, op_ref, st_ref,
             s_scr, b_scr, q_scr, o_scr):
        @pl.when(pl.program_id(2) == 0)
        def _():
            s_scr[...] = jnp.zeros_like(s_scr)
        lb = _lower_bound(lb_ref[...], layer)
        lbf = jnp.maximum(lb, LB_FLOOR)
        gh = gh_ref[...]
        r_i = lax.broadcasted_iota(jnp.int32, (CHUNK, CHUNK), 0)
        c_i = lax.broadcasted_iota(jnp.int32, (CHUNK, CHUNK), 1)
        tril = (r_i >= c_i).astype(f32)
        rows = lax.broadcasted_iota(jnp.int32, (SUB, HD), 0)
        anchors, masks = _anchor_masks()

        def chunk(c, carry):
            rs = pl.ds(pl.multiple_of(c * CHUNK, CHUNK), CHUNK)
            q, _, _, _, _, logf, k = _gate_parts(q_ref[rs, :], f_ref[rs, :], lb, lbf)
            v = i_ref[rs, :]
            b = _dot(tril, logf, precision=lax.Precision.HIGHEST)
            b_scr[...] = b
            q_scr[...] = q
            for blk in range(CHUNK // SUB):
                r0 = blk * SUB
                bb, kb, vb = b[r0:r0 + SUB], k[r0:r0 + SUB], v[r0:r0 + SUB]
                for t in range(SUB):
                    bt = b_scr[r0 + t:r0 + t + 1, :]
                    qt = q_scr[r0 + t:r0 + t + 1, :]
                    e = jnp.exp(jnp.where(rows <= t, bt - bb, NEG_INF))
                    a = jnp.sum(e * kb * qt, axis=1, keepdims=True)
                    o_scr[r0 + t:r0 + t + 1, :] = jnp.sum(a * vb, axis=0, keepdims=True)
            a_off = jnp.zeros((CHUNK, CHUNK), f32)
            for an, mk in zip(anchors, masks):
                beta = b_scr[an:an + 1, :]
                qh = (q * jnp.exp(jnp.minimum(b - beta, 0.0))).astype(bf16)
                kh = (k * jnp.exp(jnp.minimum(beta - b, 0.0))).astype(bf16)
                a_off = a_off + jnp.where(mk, _dot(qh, kh, NT), 0.0)
            st = s_scr[...]
            st_ref[0, 0, c] = st
            vb16 = v.astype(bf16)
            o = o_scr[...] + _dot(a_off.astype(bf16), vb16) + _dot((q * jnp.exp(b)).astype(bf16), st.astype(bf16), NT)
            b_end = b_scr[CHUNK - 1:CHUNK, :]
            kdec = (k * jnp.exp(b_end - b)).astype(bf16)
            s_scr[...] = jnp.exp(b_end) * st + _dot(vb16, kdec, TN)
            rr = lax.rsqrt(jnp.mean(o * o, axis=-1, keepdims=True) + NORM_EPS)
            zr = z_ref[rs, :]
            cat_ref[rs, :] = (o * rr * gh * (zr * _sigmoid(zr))).astype(bf16)
            op_ref[rs, :] = o
            return carry

        lax.fori_loop(0, NC, chunk, 0)

    def col(part):
        return pl.BlockSpec((TB, HD), lambda b, h, n: (b * nT + n, part * HG_HEADS + h))

    out_col = pl.BlockSpec((TB, HD), lambda b, h, n: (b * nT + n, h))
    return pl.pallas_call(
        body, name=f"hgrn_fwd_l{layer}", grid=(B, HG_HEADS, nT),
        in_specs=[col(0), col(1), col(2), col(3),
                  pl.BlockSpec((DEPTH, HD), lambda b, h, n: (0, h)),
                  pl.BlockSpec((1, HD), lambda b, h, n: (0, 0))],
        out_specs=[out_col, out_col,
                   pl.BlockSpec((1, 1, NC, HD, HD), lambda b, h, n: (b, h, n, 0, 0))],
        out_shape=[jax.ShapeDtypeStruct((T, HG_WIDTH), bf16), jax.ShapeDtypeStruct((T, HG_WIDTH), f32),
                   jax.ShapeDtypeStruct((B, HG_HEADS, nC, HD, HD), f32)],
        scratch_shapes=[pltpu.VMEM((HD, HD), f32), pltpu.VMEM((CHUNK, HD), f32),
                        pltpu.VMEM((CHUNK, HD), f32), pltpu.VMEM((CHUNK, HD), f32)],
        compiler_params=_params("parallel", "parallel", "arbitrary"),
    )(proj, proj, proj, proj, lb_param, g_head)


def _hgrn_bwd(proj, lb_param, g_head, o_pre, states, dcat, *, B, S, layer):
    T = B * S
    TB = min(256, S)
    nT, NC = S // TB, TB // CHUNK
    HD = HG_HEAD_DIM

    def body(q_ref, f_ref, i_ref, z_ref, lb_ref, gh_ref, op_ref, st_ref, dc_ref,
             dq_ref, df_ref, di_ref, dz_ref, dlb_ref, dgh_ref,
             ds_scr, b_scr, q_scr, do_scr, dqd_scr, dkd_scr, dvd_scr):
        @pl.when(pl.program_id(2) == 0)
        def _():
            ds_scr[...] = jnp.zeros_like(ds_scr)
            dlb_ref[...] = jnp.zeros_like(dlb_ref)
            dgh_ref[...] = jnp.zeros_like(dgh_ref)
        lb = _lower_bound(lb_ref[...], layer)
        lbf = jnp.maximum(lb, LB_FLOOR)
        ind = (lb > LB_FLOOR).astype(f32)
        gh = gh_ref[...]
        r_i = lax.broadcasted_iota(jnp.int32, (CHUNK, CHUNK), 0)
        c_i = lax.broadcasted_iota(jnp.int32, (CHUNK, CHUNK), 1)
        tril = (r_i >= c_i).astype(f32)
        triu = (c_i >= r_i).astype(f32)
        rows = lax.broadcasted_iota(jnp.int32, (SUB, HD), 0)
        last_row = lax.broadcasted_iota(jnp.int32, (CHUNK, HD), 0) == CHUNK - 1
        anchors, masks = _anchor_masks()

        def chunk(cc, carry):
            c = NC - 1 - cc
            rs = pl.ds(pl.multiple_of(c * CHUNK, CHUNK), CHUNK)
            qr, fr = q_ref[rs, :], f_ref[rs, :]
            q, sq, sg, nsg, fg, logf, k = _gate_parts(qr, fr, lb, lbf)
            v = i_ref[rs, :]
            b = _dot(tril, logf, precision=lax.Precision.HIGHEST)
            o = op_ref[rs, :]
            dc = dc_ref[rs, :]
            zr = z_ref[rs, :]
            sz = _sigmoid(zr)
            rr = lax.rsqrt(jnp.mean(o * o, axis=-1, keepdims=True) + NORM_EPS)
            dz_ref[rs, :] = (dc * (o * rr * gh) * (sz * (1.0 + zr * (1.0 - sz)))).astype(bf16)
            dn = dc * (zr * sz)
            dgh_ref[0, 0] += jnp.sum(dn * o * rr, axis=0, keepdims=True)
            gdn = dn * gh
            d_o = rr * gdn - o * (rr * rr * rr) * jnp.mean(gdn * o, axis=-1, keepdims=True)
            b_scr[...] = b
            q_scr[...] = q
            do_scr[...] = d_o
            dob = d_o.astype(bf16)
            vb16 = v.astype(bf16)
            d_a = _dot(dob, vb16, NT)
            d_q = jnp.zeros((CHUNK, HD), f32)
            d_k = jnp.zeros((CHUNK, HD), f32)
            a_off = jnp.zeros((CHUNK, CHUNK), f32)
            for an, mk in zip(anchors, masks):
                beta = b_scr[an:an + 1, :]
                eq = jnp.exp(jnp.minimum(b - beta, 0.0))
                ek = jnp.exp(jnp.minimum(beta - b, 0.0))
                qh = (q * eq).astype(bf16)
                kh = (k * ek).astype(bf16)
                a_off = a_off + jnp.where(mk, _dot(qh, kh, NT), 0.0)
                d_aa = jnp.where(mk, d_a, 0.0).astype(bf16)
                d_q = d_q + _dot(d_aa, kh) * eq
                d_k = d_k + _dot(d_aa, qh, TN) * ek
            d_v = _dot(a_off.astype(bf16), dob, TN)
            st0 = st_ref[0, 0, c]
            dst1 = ds_scr[...]
            dst1b = dst1.astype(bf16)
            eb = jnp.exp(b)
            b_end = b_scr[CHUNK - 1:CHUNK, :]
            edec = jnp.exp(b_end - b)
            e_end = jnp.exp(b_end)
            kdec = (k * edec).astype(bf16)
            qdec = (q * eb).astype(bf16)
            d_q = d_q + _dot(dob, st0.astype(bf16)) * eb
            d_v = d_v + _dot(kdec, dst1b, NT)
            d_k = d_k + _dot(vb16, dst1b) * edec
            st1 = e_end * st0 + _dot(vb16, kdec, TN)
            rterm = jnp.sum(dst1 * st1, axis=0, keepdims=True)
            ds_scr[...] = e_end * dst1 + _dot(dob, qdec, TN)
            for blk in range(CHUNK // SUB):
                r0 = blk * SUB
                bb, kb, vb = b[r0:r0 + SUB], k[r0:r0 + SUB], v[r0:r0 + SUB]
                dkb = jnp.zeros((SUB, HD), f32)
                dvb = jnp.zeros((SUB, HD), f32)
                for t in range(SUB):
                    bt = b_scr[r0 + t:r0 + t + 1, :]
                    qt = q_scr[r0 + t:r0 + t + 1, :]
                    dot_ = do_scr[r0 + t:r0 + t + 1, :]
                    e = jnp.exp(jnp.where(rows <= t, bt - bb, NEG_INF))
                    ke = e * kb
                    a = jnp.sum(ke * qt, axis=1, keepdims=True)
                    da = jnp.sum(vb * dot_, axis=1, keepdims=True)
                    dqd_scr[r0 + t:r0 + t + 1, :] = jnp.sum(da * ke, axis=0, keepdims=True)
                    dkb = dkb + da * (e * qt)
                    dvb = dvb + a * dot_
                dkd_scr[r0:r0 + SUB, :] = dkb
                dvd_scr[r0:r0 + SUB, :] = dvb
            d_q = d_q + dqd_scr[...]
            d_k = d_k + dkd_scr[...]
            d_v = d_v + dvd_scr[...]
            db = q * d_q - k * d_k + jnp.where(last_row, rterm, 0.0)
            dlt = _dot(triu, db, precision=lax.Precision.HIGHEST) - fg * d_k
            df_ref[rs, :] = (dlt * (1.0 - lb) * sg * nsg / fg).astype(bf16)
            dlb_ref[0] += jnp.sum(dlt * (ind - sg) / fg, axis=0, keepdims=True)
            dq_ref[rs, :] = (d_q * (sq * (1.0 + qr * (1.0 - sq)))).astype(bf16)
            di_ref[rs, :] = d_v.astype(bf16)
            return carry

        lax.fori_loop(0, NC, chunk, 0)

    def col(part):
        return pl.BlockSpec((TB, HD), lambda b, h, n: (b * nT + nT - 1 - n, part * HG_HEADS + h))

    hcol = pl.BlockSpec((TB, HD), lambda b, h, n: (b * nT + nT - 1 - n, h))
    return pl.pallas_call(
        body, name=f"hgrn_bwd_l{layer}", grid=(B, HG_HEADS, nT),
        in_specs=[col(0), col(1), col(2), col(3),
                  pl.BlockSpec((DEPTH, HD), lambda b, h, n: (0, h)),
                  pl.BlockSpec((1, HD), lambda b, h, n: (0, 0)),
                  hcol,
                  pl.BlockSpec((1, 1, NC, HD, HD), lambda b, h, n: (b, h, nT - 1 - n, 0, 0)),
                  hcol],
        out_specs=[hcol, hcol, hcol, hcol,
                   pl.BlockSpec((1, 1, HD), lambda b, h, n: (b, 0, h)),
                   pl.BlockSpec((1, 1, 1, HD), lambda b, h, n: (b, h, 0, 0))],
        out_shape=[jax.ShapeDtypeStruct((T, HG_WIDTH), bf16)] * 4 + [
            jax.ShapeDtypeStruct((B, 1, HG_WIDTH), f32), jax.ShapeDtypeStruct((B, HG_HEADS, 1, HD), f32)],
        scratch_shapes=[pltpu.VMEM((HD, HD), f32)] + [pltpu.VMEM((CHUNK, HD), f32)] * 6,
        compiler_params=_params("parallel", "parallel", "arbitrary"),
    )(proj, proj, proj, proj, lb_param, g_head, o_pre, states, dcat)


def _rope_tables(S):
    half = ATT_HEAD_DIM // 2
    inv_freq = ROPE_THETA ** (-jnp.arange(half, dtype=f32) / half)
    ang = jnp.arange(S, dtype=f32)[:, None] * inv_freq[None, :]
    cos, sin = jnp.cos(ang), jnp.sin(ang)
    return jnp.tile(jnp.concatenate([cos, cos], axis=1), (1, 2)), jnp.tile(jnp.concatenate([-sin, sin], axis=1), (1, 2))


def _swap_halves(x, first_half):
    return jnp.where(first_half, pltpu.roll(x, LANES - ATT_HEAD_DIM // 2, 1), pltpu.roll(x, ATT_HEAD_DIM // 2, 1))


def _rope(x, cos, sin, first_half):
    return x * cos + _swap_halves(x, first_half) * sin


def _rope_bwd(dy, cos, sin, first_half):
    return dy * cos + _swap_halves(dy * sin, first_half)


def _attn_consts(n):
    lane = lax.broadcasted_iota(jnp.int32, (1, LANES), 1)
    low = lane < ATT_HEAD_DIM
    first_half = (lane % ATT_HEAD_DIM) < ATT_HEAD_DIM // 2
    t = lax.broadcasted_iota(jnp.int32, (ATT_BLOCK, 2 * ATT_BLOCK), 0)
    s = lax.broadcasted_iota(jnp.int32, (ATT_BLOCK, 2 * ATT_BLOCK), 1)
    mask = (s > t) & (s <= t + ATT_BLOCK) & ((s >= ATT_BLOCK) | (n > 0))
    return low, first_half, mask


def _dup_kv(x, low):
    rolled = pltpu.roll(x, ATT_HEAD_DIM, 1)
    return [jnp.where(low, x, rolled), jnp.where(low, rolled, x)]


def _attn_head(qm, kd, vd, sink, mask):
    s = jnp.where(mask, _dot(qm, kd, NT) * ATT_SCALE, NEG_INF)
    m = jnp.maximum(jnp.max(s, axis=-1, keepdims=True), sink)
    p = jnp.exp(s - m)
    psink = jnp.exp(sink - m)
    inv = 1.0 / (jnp.sum(p, axis=-1, keepdims=True) + psink)
    pn = p * inv
    return pn, psink * inv, _dot(pn.astype(bf16), vd)


def _swa_fwd(proj, sink_b, cos, sin, *, B, S):
    T = B * S
    L = ATT_BLOCK
    nB = S // L

    def body(q_ref, z_ref, kvc_ref, kvp_ref, sk_ref, cc_ref, sc_ref, cp_ref, sp_ref, cat_ref):
        n = pl.program_id(1)
        low, first_half, mask = _attn_consts(n)
        cc, sc = cc_ref[...], sc_ref[...]
        kc = _rope(kvc_ref[:, 0:LANES], cc, sc, first_half)
        kp = _rope(kvp_ref[:, 0:LANES], cp_ref[...], sp_ref[...], first_half)
        kd = [x.astype(bf16) for x in _dup_kv(jnp.concatenate([kp, kc], axis=0), low)]
        vd = [x.astype(bf16) for x in _dup_kv(jnp.concatenate([kvp_ref[:, LANES:2 * LANES], kvc_ref[:, LANES:2 * LANES]], axis=0), low)]
        for pair in range(ATT_HEADS // 2):
            cols = slice(pair * LANES, (pair + 1) * LANES)
            j = (2 * pair) // ATT_GROUP
            qp = _rope(q_ref[:, cols], cc, sc, first_half)
            outs = []
            for hh in range(2):
                h = 2 * pair + hh
                qm = jnp.where(low if hh == 0 else ~low, qp, 0.0).astype(bf16)
                _, _, o = _attn_head(qm, kd[j], vd[j], sk_ref[h:h + 1, 0:1], mask)
                outs.append(o)
            zp = z_ref[:, cols]
            cat_ref[:, cols] = (jnp.where(low, outs[0], outs[1]) * (zp * _sigmoid(zp))).astype(bf16)

    cur = lambda b, n: (b * nB + n, 0)
    prev = lambda b, n: (b * nB + jnp.maximum(n - 1, 0), 0)
    return pl.pallas_call(
        body, name="swa_fwd", grid=(B, nB),
        in_specs=[pl.BlockSpec((L, ATT_WIDTH), lambda b, n: (b * nB + n, QA_BLK)),
                  pl.BlockSpec((L, ATT_WIDTH), lambda b, n: (b * nB + n, ZA_BLK)),
                  pl.BlockSpec((L, 2 * KV_WIDTH), lambda b, n: (b * nB + n, KV_BLK)),
                  pl.BlockSpec((L, 2 * KV_WIDTH), lambda b, n: (b * nB + jnp.maximum(n - 1, 0), KV_BLK)),
                  pl.BlockSpec((ATT_HEADS, LANES), lambda b, n: (0, 0)),
                  pl.BlockSpec((L, LANES), lambda b, n: (n, 0)), pl.BlockSpec((L, LANES), lambda b, n: (n, 0)),
                  pl.BlockSpec((L, LANES), lambda b, n: (jnp.maximum(n - 1, 0), 0)),
                  pl.BlockSpec((L, LANES), lambda b, n: (jnp.maximum(n - 1, 0), 0))],
        out_specs=pl.BlockSpec((L, ATT_WIDTH), cur),
        out_shape=jax.ShapeDtypeStruct((T, ATT_WIDTH), bf16),
        compiler_params=_params("parallel", "parallel"),
    )(proj, proj, proj, proj, sink_b, cos, sin, cos, sin)


def _swa_bwd(proj, sink_b, cos, sin, dcat, *, B, S):
    T = B * S
    L = ATT_BLOCK
    nB = S // L

    def body(q_ref, z_ref, kvc_ref, kvp_ref, sk_ref, cc_ref, sc_ref, cp_ref, sp_ref, dc_ref,
             dq_ref, dz_ref, dkv_ref, dsk_ref, carry):
        step = pl.program_id(1)
        n = nB - 1 - step

        @pl.when((pl.program_id(0) == 0) & (step == 0))
        def _():
            dsk_ref[...] = jnp.zeros_like(dsk_ref)

        @pl.when(step == 0)
        def _():
            carry[...] = jnp.zeros_like(carry)
        low, first_half, mask = _attn_consts(n)
        cc, sc, cp, sp = cc_ref[...], sc_ref[...], cp_ref[...], sp_ref[...]
        kc = _rope(kvc_ref[:, 0:LANES], cc, sc, first_half)
        kp = _rope(kvp_ref[:, 0:LANES], cp, sp, first_half)
        kd = [x.astype(bf16) for x in _dup_kv(jnp.concatenate([kp, kc], axis=0), low)]
        vd = [x.astype(bf16) for x in _dup_kv(jnp.concatenate([kvp_ref[:, LANES:2 * LANES], kvc_ref[:, LANES:2 * LANES]], axis=0), low)]
        dkd = [jnp.zeros((2 * L, LANES), f32) for _ in range(2)]
        dvd = [jnp.zeros((2 * L, LANES), f32) for _ in range(2)]
        for pair in range(ATT_HEADS // 2):
            cols = slice(pair * LANES, (pair + 1) * LANES)
            j = (2 * pair) // ATT_GROUP
            qp = _rope(q_ref[:, cols], cc, sc, first_half)
            zp = z_ref[:, cols]
            dc = dc_ref[:, cols]
            sz = _sigmoid(zp)
            d_o = dc * (zp * sz)
            res = []
            for hh in range(2):
                sel = low if hh == 0 else ~low
                qm = jnp.where(sel, qp, 0.0).astype(bf16)
                pn, psn, o = _attn_head(qm, kd[j], vd[j], sk_ref[2 * pair + hh:2 * pair + hh + 1, 0:1], mask)
                res.append((sel, qm, pn, psn, o))
            o_pair = jnp.where(low, res[0][4], res[1][4])
            dz_ref[:, cols] = (dc * o_pair * (sz * (1.0 + zp * (1.0 - sz)))).astype(bf16)
            dqs = []
            for hh in range(2):
                h = 2 * pair + hh
                sel, qm, pn, psn, _ = res[hh]
                dom = jnp.where(sel, d_o, 0.0)
                delta = jnp.sum(dom * o_pair, axis=-1, keepdims=True)
                domb = dom.astype(bf16)
                ds = (pn * (_dot(domb, vd[j], NT) - delta) * ATT_SCALE).astype(bf16)
                dsk_ref[h:h + 1, :] += jnp.zeros((1, LANES), f32) - jnp.sum(psn * delta)
                dqs.append(_dot(ds, kd[j]))
                dkd[j] = dkd[j] + _dot(ds, qm, TN)
                dvd[j] = dvd[j] + _dot(pn.astype(bf16), domb, TN)
            dq_ref[:, cols] = _rope_bwd(jnp.where(low, dqs[0], dqs[1]), cc, sc, first_half).astype(bf16)
        dk = [x + pltpu.roll(x, ATT_HEAD_DIM, 1) for x in dkd]
        dv = [x + pltpu.roll(x, ATT_HEAD_DIM, 1) for x in dvd]
        dk = jnp.where(low, dk[0], dk[1])
        dv = jnp.where(low, dv[0], dv[1])
        dkv_ref[:, 0:LANES] = (_rope_bwd(dk[L:2 * L], cc, sc, first_half) + carry[:, 0:LANES]).astype(bf16)
        dkv_ref[:, LANES:2 * LANES] = (dv[L:2 * L] + carry[:, LANES:2 * LANES]).astype(bf16)
        carry[:, 0:LANES] = _rope_bwd(dk[0:L], cp, sp, first_half)
        carry[:, LANES:2 * LANES] = dv[0:L]

    rev = lambda b, s: b * nB + nB - 1 - s
    revp = lambda b, s: b * nB + jnp.maximum(nB - 2 - s, 0)
    wide = lambda blk: pl.BlockSpec((L, ATT_WIDTH), lambda b, s: (rev(b, s), blk))
    tab = pl.BlockSpec((L, LANES), lambda b, s: (nB - 1 - s, 0))
    tabp = pl.BlockSpec((L, LANES), lambda b, s: (jnp.maximum(nB - 2 - s, 0), 0))
    return pl.pallas_call(
        body, name="swa_bwd", grid=(B, nB),
        in_specs=[wide(QA_BLK), wide(ZA_BLK),
                  pl.BlockSpec((L, 2 * KV_WIDTH), lambda b, s: (rev(b, s), KV_BLK)),
                  pl.BlockSpec((L, 2 * KV_WIDTH), lambda b, s: (revp(b, s), KV_BLK)),
                  pl.BlockSpec((ATT_HEADS, LANES), lambda b, s: (0, 0)),
                  tab, tab, tabp, tabp, wide(0)],
        out_specs=[wide(0), wide(0), pl.BlockSpec((L, 2 * KV_WIDTH), lambda b, s: (rev(b, s), 0)),
                   pl.BlockSpec((ATT_HEADS, LANES), lambda b, s: (0, 0))],
        out_shape=[jax.ShapeDtypeStruct((T, ATT_WIDTH), bf16), jax.ShapeDtypeStruct((T, ATT_WIDTH), bf16),
                   jax.ShapeDtypeStruct((T, 2 * KV_WIDTH), bf16), jax.ShapeDtypeStruct((ATT_HEADS, LANES), f32)],
        scratch_shapes=[pltpu.VMEM((L, 2 * KV_WIDTH), f32)],
        compiler_params=_params("arbitrary", "arbitrary"),
    )(proj, proj, proj, proj, sink_b, cos, sin, cos, sin, dcat)


def _local_grads(x, target, w_in_p, w_out_b, g_pre, g_post, lb_param, g_head, sinks, *, B, S):
    cos, sin = _rope_tables(S)
    saved = []
    for l in range(DEPTH):
        proj, h = _in_proj(x, g_pre[l:l + 1], w_in_p[l])
        ch, o_pre, states = _hgrn_fwd(proj, lb_param, g_head[l:l + 1], B=B, S=S, layer=l)
        sink_b = jnp.broadcast_to(sinks[l][:, None], (ATT_HEADS, LANES))
        ca = _swa_fwd(proj, sink_b, cos, sin, B=B, S=S)
        xn, y = _out_proj(ch, ca, w_out_b[l], x, g_post[l:l + 1])
        saved.append((x, proj, h, ch, o_pre, states, sink_b, ca, y))
        x = xn
    dx, loss = _loss_head(x, target)
    gw_in, gw_out, gg_pre, gg_post, g_lb, gg_head, g_sinks = [], [], [], [], [], [], []
    for l in reversed(range(DEPTH)):
        x_in, proj, h, ch, o_pre, states, sink_b, ca, y = saved[l]
        dch, dca, dwo, dgpost = _out_proj_bwd(dx, y, g_post[l:l + 1], w_out_b[l], ch, ca)
        dq, df, di, dz, dlb, dgh = _hgrn_bwd(proj, lb_param, g_head[l:l + 1], o_pre, states, dch, B=B, S=S, layer=l)
        dqa, dza, dkv, dsk = _swa_bwd(proj, sink_b, cos, sin, dca, B=B, S=S)
        dproj = jnp.concatenate([dq, df, di, dz, dqa, dza, dkv], axis=1)
        dx, dgpre = _in_proj_bwd(dproj, w_in_p[l], x_in, g_pre[l:l + 1], dx)
        gw_in.append(_grad_w_in(h, dproj))
        gw_out.append(dwo)
        gg_pre.append(dgpre[0])
        gg_post.append(dgpost[0])
        g_lb.append(jnp.sum(dlb, axis=(0, 1)))
        gg_head.append(jnp.sum(dgh, axis=(0, 1, 2)))
        g_sinks.append(dsk[:, 0])
    rev = lambda xs: jnp.stack(xs[::-1])
    return (loss[0, 0], dx, rev(gw_in), rev(gw_out), rev(gg_pre), rev(gg_post), rev(g_lb), rev(gg_head), rev(g_sinks))


MESH = pl.DeviceIdType.MESH
ANY = pl.BlockSpec(memory_space=pl.ANY)


def _place():
    x, y, c = lax.axis_index("x"), lax.axis_index("y"), lax.axis_index("c")
    return x, y, c, [(1 - x, y), (x, 1 - y), (1 - x, 1 - y)]


def _rcopy(src, dst, send, recv, k, to):
    return pltpu.make_async_remote_copy(src_ref=src, dst_ref=dst, send_sem=send.at[k], recv_sem=recv.at[k],
                                        device_id=to, device_id_type=MESH)


def _gather_shards(shards):
    n = len(shards)

    def body(*refs):
        ins, outs = refs[:n], refs[n:2 * n]
        send, recv, lsem = refs[2 * n:]
        x, y, c, chips = _place()
        me = 2 * x + y
        sib = (x, y, 1 - c)

        def half(ref, a, hc):
            r = shards[a].shape[1] // 2
            return ref.at[:, pl.ds(hc * r, r), :]

        local = [pltpu.make_async_copy(ins[a], outs[a].at[me], lsem.at[a]) for a in range(n)]
        for cp in local:
            cp.start()
        first = []
        for j, chip in enumerate(chips):
            for a in range(n):
                first.append(_rcopy(half(ins[a], a, c), half(outs[a].at[me], a, c), send, recv, j * n + a, (*chip, c)))
        for cp in first:
            cp.start()
        passed = []
        for j, (px, py) in enumerate(chips):
            for a in range(n):
                blk = half(outs[a].at[2 * px + py], a, c)
                _rcopy(blk, blk, send, recv, j * n + a, sib).wait_recv()
                cp = _rcopy(blk, blk, send, recv, (3 + j) * n + a, sib)
                cp.start()
                passed.append(cp)
        for j, (px, py) in enumerate(chips):
            for a in range(n):
                blk = half(outs[a].at[2 * px + py], a, 1 - c)
                _rcopy(blk, blk, send, recv, (3 + j) * n + a, sib).wait_recv()
        for cp in first + passed:
            cp.wait_send()
        for cp in local:
            cp.wait()

    return pl.pallas_call(
        body, name="gather_shards",
        in_specs=[ANY] * n, out_specs=[ANY] * n,
        out_shape=[jax.ShapeDtypeStruct((4,) + s.shape, s.dtype) for s in shards],
        scratch_shapes=[pltpu.SemaphoreType.DMA((6 * n,)), pltpu.SemaphoreType.DMA((6 * n,)),
                        pltpu.SemaphoreType.DMA((n,))],
    )(*shards)


def _pair_exchange(parts):
    n = len(parts)

    def body(*refs):
        ins, outs = refs[:n], refs[n:2 * n]
        send, recv = refs[2 * n:]
        x, y, c, _ = _place()
        cps = []
        for a in range(n):
            r = parts[a].shape[2] // 2
            cps.append(_rcopy(ins[a].at[:, :, pl.ds((1 - c) * r, r), :], outs[a], send, recv, a, (x, y, 1 - c)))
        for cp in cps:
            cp.start()
        for cp in cps:
            cp.wait()

    return pl.pallas_call(
        body, name="pair_exchange", in_specs=[ANY] * n, out_specs=[ANY] * n,
        out_shape=[jax.ShapeDtypeStruct(p.shape[:2] + (p.shape[2] // 2, p.shape[3]), p.dtype) for p in parts],
        scratch_shapes=[pltpu.SemaphoreType.DMA((n,)), pltpu.SemaphoreType.DMA((n,))],
    )(*parts)


def _pair_add(part, got, *, rows=256):
    K, L, R, C = part.shape
    r = R // 2
    rows = min(rows, r)
    nb = r // rows

    def body(c_ref, a_ref, b_ref, o_ref):
        o_ref[...] = (a_ref[...] + b_ref[...]).astype(bf16)

    blk = (1, 1, rows, C)
    return pl.pallas_call(
        body, name="pair_add",
        grid_spec=pltpu.PrefetchScalarGridSpec(
            num_scalar_prefetch=1, grid=(K, L, nb),
            in_specs=[pl.BlockSpec(blk, lambda k, l, i, c: (k, l, c[0] * nb + i, 0)),
                      pl.BlockSpec(blk, lambda k, l, i, c: (k, l, i, 0))],
            out_specs=pl.BlockSpec(blk, lambda k, l, i, c: (k, l, i, 0))),
        out_shape=jax.ShapeDtypeStruct((K, L, r, C), bf16),
        compiler_params=_params("parallel", "parallel", "parallel"),
    )(jnp.reshape(lax.axis_index("c"), (1,)).astype(jnp.int32), part, got)


def _chip_exchange(sums):
    n = len(sums)

    def body(*refs):
        ins, outs = refs[:n], refs[n:2 * n]
        send, recv = refs[2 * n:]
        x, y, c, chips = _place()
        cps = []
        for j, (px, py) in enumerate(chips):
            for a in range(n):
                cps.append(_rcopy(ins[a].at[2 * px + py], outs[a].at[j], send, recv, j * n + a, (px, py, c)))
        for cp in cps:
            cp.start()
        for cp in cps:
            cp.wait()

    return pl.pallas_call(
        body, name="chip_exchange", in_specs=[ANY] * n, out_specs=[ANY] * n,
        out_shape=[jax.ShapeDtypeStruct((3,) + s.shape[1:], s.dtype) for s in sums],
        scratch_shapes=[pltpu.SemaphoreType.DMA((3 * n,)), pltpu.SemaphoreType.DMA((3 * n,))],
    )(*sums)


def _chip_sum(mine, got, *, rows=256):
    K, L, r, C = mine.shape
    rows = min(rows, r)

    def body(k_ref, a_ref, b_ref, o_ref):
        o_ref[0] = (a_ref[0, 0].astype(f32) + b_ref[0, 0].astype(f32)) + (b_ref[1, 0].astype(f32) + b_ref[2, 0].astype(f32))

    chip = 2 * lax.axis_index("x") + lax.axis_index("y")
    return pl.pallas_call(
        body, name="chip_sum",
        grid_spec=pltpu.PrefetchScalarGridSpec(
            num_scalar_prefetch=1, grid=(L, r // rows),
            in_specs=[pl.BlockSpec((1, 1, rows, C), lambda l, i, k: (k[0], l, i, 0)),
                      pl.BlockSpec((3, 1, rows, C), lambda l, i, k: (0, l, i, 0))],
            out_specs=pl.BlockSpec((1, rows, C), lambda l, i, k: (l, i, 0))),
        out_shape=jax.ShapeDtypeStruct((L, r, C), f32),
        compiler_params=_params("parallel", "parallel"),
    )(jnp.reshape(chip, (1,)).astype(jnp.int32), mine, got)


def _pair_share(halves):
    n = len(halves)

    def body(*refs):
        ins, outs = refs[:n], refs[n:2 * n]
        send, recv, lsem = refs[2 * n:]
        x, y, c, _ = _place()
        local, cps = [], []
        for a in range(n):
            r = halves[a].shape[1]
            mine = outs[a].at[:, pl.ds(c * r, r), :]
            local.append(pltpu.make_async_copy(ins[a], mine, lsem.at[a]))
            cps.append(_rcopy(ins[a], mine, send, recv, a, (x, y, 1 - c)))
        for cp in local + cps:
            cp.start()
        for a in range(n):
            r = halves[a].shape[1]
            theirs = outs[a].at[:, pl.ds((1 - c) * r, r), :]
            _rcopy(theirs, theirs, send, recv, a, (x, y, 1 - c)).wait_recv()
        for cp in cps:
            cp.wait_send()
        for cp in local:
            cp.wait()

    return pl.pallas_call(
        body, name="pair_share", in_specs=[ANY] * n, out_specs=[ANY] * n,
        out_shape=[jax.ShapeDtypeStruct((h.shape[0], 2 * h.shape[1], h.shape[2]), h.dtype) for h in halves],
        scratch_shapes=[pltpu.SemaphoreType.DMA((n,)), pltpu.SemaphoreType.DMA((n,)), pltpu.SemaphoreType.DMA((n,))],
    )(*halves)


def _all_sum_small(v):
    def body(v_ref, o_ref, buf, send, recv):
        x, y, c, _ = _place()
        me = 4 * x + 2 * y + c
        buf[me] = v_ref[...]
        cps = []
        for m in range(1, 8):
            to = (x ^ (m >> 2), y ^ ((m >> 1) & 1), c ^ (m & 1))
            cps.append(_rcopy(v_ref, buf.at[me], send, recv, m - 1, to))
        for cp in cps:
            cp.start()
        for cp in cps:
            cp.wait()
        acc = buf[0]
        for d in range(1, 8):
            acc = acc + buf[d]
        o_ref[...] = acc

    vm = pl.BlockSpec(memory_space=pltpu.VMEM)
    return pl.pallas_call(
        body, name="all_sum_small", in_specs=[vm], out_specs=vm,
        out_shape=jax.ShapeDtypeStruct(v.shape, v.dtype),
        scratch_shapes=[pltpu.VMEM((8,) + v.shape, v.dtype), pltpu.SemaphoreType.DMA((7,)), pltpu.SemaphoreType.DMA((7,))],
    )(v)


def _adamw_math(w, g, m, v):
    m = ADAM_B1 * m + (1.0 - ADAM_B1) * g
    v = ADAM_B2 * v + (1.0 - ADAM_B2) * (g * g)
    m_hat = m / (1.0 - ADAM_B1 ** ADAM_STEP)
    v_hat = v / (1.0 - ADAM_B2 ** ADAM_STEP)
    return -ADAM_LR * (m_hat / (jnp.sqrt(v_hat) + ADAM_EPS) + ADAM_WD * w), m, v


def _adamw(w, g, m, v, *, rows=256):
    L, R, C = w.shape
    rows = min(rows, R)

    def body(w_ref, g_ref, m_ref, v_ref, d_ref, mo_ref, vo_ref):
        d_ref[...], mo_ref[...], vo_ref[...] = _adamw_math(w_ref[...], g_ref[...], m_ref[...], v_ref[...])

    blk = pl.BlockSpec((1, rows, C), lambda l, i: (l, i, 0))
    return pl.pallas_call(
        body, name="adamw", grid=(L, R // rows), in_specs=[blk] * 4, out_specs=[blk] * 3,
        out_shape=[jax.ShapeDtypeStruct(w.shape, f32)] * 3,
        compiler_params=_params("parallel", "parallel"),
    )(w, g, m, v)


SMALL_ROWS = 4 * DEPTH


def _pack_small(g_pre, g_post, lb, g_head, sinks, loss=None):
    rows = []
    for l in range(DEPTH):
        tail = [g_head[l], sinks[l]]
        if loss is not None and l == 0:
            tail.append(jnp.reshape(loss, (1,)))
        tail = jnp.concatenate(tail)
        rows += [g_pre[l], g_post[l], lb[l], jnp.pad(tail, (0, D_MODEL - tail.shape[0]))]
    return jnp.stack(rows)


def _unpack_small(p):
    g_pre = jnp.stack([p[4 * l] for l in range(DEPTH)])
    g_post = jnp.stack([p[4 * l + 1] for l in range(DEPTH)])
    lb = jnp.stack([p[4 * l + 2] for l in range(DEPTH)])
    g_head = jnp.stack([p[4 * l + 3, :HG_HEAD_DIM] for l in range(DEPTH)])
    sinks = jnp.stack([p[4 * l + 3, HG_HEAD_DIM:HG_HEAD_DIM + ATT_HEADS] for l in range(DEPTH)])
    return g_pre, g_post, lb, g_head, sinks


def _small_update(gsum, w, m, v):
    def body(g_ref, w_ref, m_ref, v_ref, go_ref, d_ref, mo_ref, vo_ref):
        g = g_ref[...]
        w = w_ref[...]
        lbp = [w[4 * l + 2:4 * l + 3] for l in range(DEPTH)]
        mx = functools.reduce(jnp.maximum, lbp)
        e = [jnp.exp(t - mx) for t in lbp]
        tot = functools.reduce(jnp.add, e)
        p = [t / tot for t in e]
        glb = [g[4 * l + 2:4 * l + 3] for l in range(DEPTH)]
        row = lax.broadcasted_iota(jnp.int32, g.shape, 0)
        for j in range(DEPTH):
            gj = jnp.zeros_like(p[0])
            for l in range(DEPTH):
                for i in range(1, l + 1):
                    gj = gj + glb[l] * p[i] * ((1.0 if i == j else 0.0) - p[j])
            g = jnp.where(row == 4 * j + 2, gj, g)
        go_ref[...] = g
        d_ref[...], mo_ref[...], vo_ref[...] = _adamw_math(w, g, m_ref[...], v_ref[...])

    vm = pl.BlockSpec(memory_space=pltpu.VMEM)
    return pl.pallas_call(
        body, name="small_update", in_specs=[vm] * 4, out_specs=[vm] * 4,
        out_shape=[jax.ShapeDtypeStruct(gsum.shape, f32)] * 4,
    )(gsum, w, m, v)


def _to_internal_cols(w):
    return jnp.concatenate([w[..., 0:5120], w[..., 5376:6400], w[..., 5120:5376]], axis=-1)


def _from_internal_cols(w):
    return jnp.concatenate([w[..., 0:5120], w[..., 6144:6400], w[..., 5120:6144]], axis=-1)


def kernel(x, w_in, w_out, g_pre, g_post, lb_param, g_head, sinks, loss_target, m_w_in, m_w_out, m_g_pre, m_g_post, m_lb_param, m_g_head, m_sinks, v_w_in, v_w_out, v_g_pre, v_g_post, v_lb_param, v_g_head, v_sinks):
    B, S, _ = x.shape
    T = B * S
    L = DEPTH
    wi_all, wo_all = _gather_shards([w_in.astype(bf16), w_out.astype(bf16)])
    w_in_p = _to_internal_cols(jnp.transpose(wi_all, (1, 2, 0, 3)).reshape(L, D_MODEL, IN_WIDTH))
    w_out_b = jnp.transpose(wo_all, (1, 0, 2, 3)).reshape(L, MIX_WIDTH, D_MODEL)

    loss, dx, gwi, gwo, ggpre, ggpost, glb, gghead, gsinks = _local_grads(
        x.reshape(T, D_MODEL), loss_target.reshape(T, D_MODEL), w_in_p, w_out_b,
        g_pre, g_post, lb_param, g_head, sinks, B=B, S=S)

    gwi_s = jnp.transpose(_from_internal_cols(gwi).reshape(L, D_MODEL, 4, IN_WIDTH // 4), (2, 0, 1, 3))
    gwo_s = jnp.transpose(gwo.reshape(L, 4, MIX_WIDTH // 4, D_MODEL), (1, 0, 2, 3))
    parts = [gwi_s, gwo_s]
    got = _pair_exchange(parts)
    sums = [_pair_add(p, r) for p, r in zip(parts, got)]
    recv = _chip_exchange(sums)
    halves = [_chip_sum(s, r) for s, r in zip(sums, recv)]
    grad_w_in, grad_w_out = _pair_share(halves)

    d_w_in, nm_w_in, nv_w_in = _adamw(w_in, grad_w_in, m_w_in, v_w_in)
    d_w_out, nm_w_out, nv_w_out = _adamw(w_out, grad_w_out, m_w_out, v_w_out)

    gsum = _all_sum_small(_pack_small(ggpre, ggpost, glb, gghead, gsinks, loss))
    gs, ds, ms, vs = _small_update(
        gsum, _pack_small(g_pre, g_post, lb_param, g_head, sinks),
        _pack_small(m_g_pre, m_g_post, m_lb_param, m_g_head, m_sinks),
        _pack_small(v_g_pre, v_g_post, v_lb_param, v_g_head, v_sinks))
    loss_all = gsum[3, HG_HEAD_DIM + ATT_HEADS]
    return (loss_all, dx.reshape(B, S, D_MODEL), grad_w_in, grad_w_out, *_unpack_small(gs),
            d_w_in, d_w_out, *_unpack_small(ds), nm_w_in, nm_w_out, *_unpack_small(ms),
            nv_w_in, nv_w_out, *_unpack_small(vs))
```

```python
import functools
import math

import jax
import jax.numpy as jnp
import numpy as np
from jax import lax
from jax.experimental import pallas as pl
from jax.experimental.pallas import tpu as pltpu

f32 = jnp.float32
bf16 = jnp.bfloat16

D_MODEL = 1024
DEPTH = 2
HG_WIDTH = 1024
HG_HEAD_DIM = 128
HG_HEADS = 8
CHUNK = 64
SUB = 16
ATT_WIDTH = 1024
ATT_HEAD_DIM = 64
ATT_HEADS = 16
ATT_GROUP = 8
KV_WIDTH = 128
ATT_BLOCK = 128
ATT_SCALE = 1.0 / math.sqrt(ATT_HEAD_DIM)
ROPE_THETA = 10000.0
IN_WIDTH = 6400
MIX_WIDTH = 2048
NORM_EPS = 1e-6
NEG_INF = -1e30
LB_FLOOR = 1e-20
LANES = 128
VMEM_LIMIT = 48 * 1024 * 1024

ADAM_LR = 0.001
ADAM_B1 = 0.9
ADAM_B2 = 0.999
ADAM_EPS = 1e-08
ADAM_WD = 0.01
ADAM_STEP = 10

COL_PERM = np.concatenate([np.arange(0, 5120), np.arange(5376, 6400), np.arange(5120, 5376)])
COL_INV = np.argsort(COL_PERM)
QA_BLK, ZA_BLK, KV_BLK = 4, 5, 24

NT = (((1,), (1,)), ((), ()))
TN = (((0,), (0,)), ((), ()))


def _dot(a, b, dims=None, precision=None):
    if dims is None:
        return jnp.dot(a, b, preferred_element_type=f32, precision=precision)
    return lax.dot_general(a, b, dims, preferred_element_type=f32, precision=precision)


def _sigmoid(x):
    return 1.0 / (1.0 + jnp.exp(-x))


def _params(*sem):
    return pltpu.CompilerParams(dimension_semantics=sem, vmem_limit_bytes=VMEM_LIMIT)


def _in_proj(x, g, w, *, tm=512, tn=1280):
    T = x.shape[0]
    tm = min(tm, T)

    def body(x_ref, g_ref, w_ref, p_ref, h_ref, hs):
        @pl.when(pl.program_id(1) == 0)
        def _():
            xv = x_ref[...]
            r = lax.rsqrt(jnp.mean(xv * xv, axis=-1, keepdims=True) + NORM_EPS)
            hv = (xv * r * g_ref[...]).astype(bf16)
            hs[...] = hv
            h_ref[...] = hv
        p_ref[...] = _dot(hs[...], w_ref[...])

    return pl.pallas_call(
        body, name="in_proj", grid=(T // tm, IN_WIDTH // tn),
        in_specs=[pl.BlockSpec((tm, D_MODEL), lambda i, j: (i, 0)),
                  pl.BlockSpec((1, D_MODEL), lambda i, j: (0, 0)),
                  pl.BlockSpec((D_MODEL, tn), lambda i, j: (0, j))],
        out_specs=[pl.BlockSpec((tm, tn), lambda i, j: (i, j)),
                   pl.BlockSpec((tm, D_MODEL), lambda i, j: (i, 0))],
        out_shape=[jax.ShapeDtypeStruct((T, IN_WIDTH), f32), jax.ShapeDtypeStruct((T, D_MODEL), bf16)],
        scratch_shapes=[pltpu.VMEM((tm, D_MODEL), bf16)],
        compiler_params=_params("parallel", "arbitrary"),
    )(x, g, w)


def _out_proj(ch, ca, wo, x, g, *, tm=512):
    T = x.shape[0]
    tm = min(tm, T)
    half = MIX_WIDTH // 2

    def body(ch_ref, ca_ref, wo_ref, x_ref, g_ref, xn_ref, y_ref):
        y = _dot(ch_ref[...], wo_ref[0:half, :]) + _dot(ca_ref[...], wo_ref[half:MIX_WIDTH, :])
        r = lax.rsqrt(jnp.mean(y * y, axis=-1, keepdims=True) + NORM_EPS)
        y_ref[...] = y
        xn_ref[...] = x_ref[...] + y * r * g_ref[...]

    row = lambda i: (i, 0)
    fixed = lambda i: (0, 0)
    return pl.pallas_call(
        body, name="out_proj", grid=(T // tm,),
        in_specs=[pl.BlockSpec((tm, half), row), pl.BlockSpec((tm, half), row),
                  pl.BlockSpec((MIX_WIDTH, D_MODEL), fixed), pl.BlockSpec((tm, D_MODEL), row),
                  pl.BlockSpec((1, D_MODEL), fixed)],
        out_specs=[pl.BlockSpec((tm, D_MODEL), row), pl.BlockSpec((tm, D_MODEL), row)],
        out_shape=[jax.ShapeDtypeStruct((T, D_MODEL), f32)] * 2,
        compiler_params=_params("parallel"),
    )(ch, ca, wo, x, g)


def _loss_head(y, target, *, tm=512):
    T = y.shape[0]
    tm = min(tm, T)

    def body(y_ref, t_ref, d_ref, l_ref):
        @pl.when(pl.program_id(0) == 0)
        def _():
            l_ref[...] = jnp.zeros_like(l_ref)
        err = y_ref[...] - t_ref[...]
        d_ref[...] = err * (1.0 / D_MODEL)
        l_ref[...] += jnp.sum(err * err) * (0.5 / D_MODEL)

    row = lambda i: (i, 0)
    return pl.pallas_call(
        body, name="loss_head", grid=(T // tm,),
        in_specs=[pl.BlockSpec((tm, D_MODEL), row), pl.BlockSpec((tm, D_MODEL), row)],
        out_specs=[pl.BlockSpec((tm, D_MODEL), row), pl.BlockSpec((8, LANES), lambda i: (0, 0))],
        out_shape=[jax.ShapeDtypeStruct((T, D_MODEL), f32), jax.ShapeDtypeStruct((8, LANES), f32)],
        compiler_params=_params("arbitrary"),
    )(y, target)


def _out_proj_bwd(dxn, y, g, wo, ch, ca, *, tm=256):
    T = y.shape[0]
    tm = min(tm, T)
    half = MIX_WIDTH // 2

    def body(dx_ref, y_ref, g_ref, wo_ref, ch_ref, ca_ref, dch_ref, dca_ref, dwo_ref, dg_ref):
        @pl.when(pl.program_id(0) == 0)
        def _():
            dwo_ref[...] = jnp.zeros_like(dwo_ref)
            dg_ref[...] = jnp.zeros_like(dg_ref)
        y = y_ref[...]
        dx = dx_ref[...]
        r = lax.rsqrt(jnp.mean(y * y, axis=-1, keepdims=True) + NORM_EPS)
        gy = dx * g_ref[...]
        dy = r * gy - y * (r * r * r) * jnp.mean(gy * y, axis=-1, keepdims=True)
        dg_ref[...] += jnp.sum(dx * y * r, axis=0, keepdims=True)
        dyb = dy.astype(bf16)
        dch_ref[...] = _dot(dyb, wo_ref[0:half, :], NT)
        dca_ref[...] = _dot(dyb, wo_ref[half:MIX_WIDTH, :], NT)
        dwo_ref[0:half, :] += _dot(ch_ref[...], dyb, TN)
        dwo_ref[half:MIX_WIDTH, :] += _dot(ca_ref[...], dyb, TN)

    row = lambda i: (i, 0)
    fixed = lambda i: (0, 0)
    return pl.pallas_call(
        body, name="out_proj_bwd", grid=(T // tm,),
        in_specs=[pl.BlockSpec((tm, D_MODEL), row), pl.BlockSpec((tm, D_MODEL), row),
                  pl.BlockSpec((1, D_MODEL), fixed), pl.BlockSpec((MIX_WIDTH, D_MODEL), fixed),
                  pl.BlockSpec((tm, half), row), pl.BlockSpec((tm, half), row)],
        out_specs=[pl.BlockSpec((tm, half), row), pl.BlockSpec((tm, half), row),
                   pl.BlockSpec((MIX_WIDTH, D_MODEL), fixed), pl.BlockSpec((1, D_MODEL), fixed)],
        out_shape=[jax.ShapeDtypeStruct((T, half), f32), jax.ShapeDtypeStruct((T, half), f32),
                   jax.ShapeDtypeStruct((MIX_WIDTH, D_MODEL), f32), jax.ShapeDtypeStruct((1, D_MODEL), f32)],
        compiler_params=_params("arbitrary"),
    )(dxn, y, g, wo, ch, ca)


def _in_proj_bwd(dproj, w, x, g, dxn, *, tm=512, tk=1280):
    T = x.shape[0]
    tm = min(tm, T)
    nk = IN_WIDTH // tk

    def body(dp_ref, w_ref, x_ref, g_ref, dxn_ref, dx_ref, dg_ref, acc):
        i, k = pl.program_id(0), pl.program_id(1)

        @pl.when((i == 0) & (k == 0))
        def _():
            dg_ref[...] = jnp.zeros_like(dg_ref)

        @pl.when(k == 0)
        def _():
            acc[...] = jnp.zeros_like(acc)
        acc[...] += _dot(dp_ref[...], w_ref[...], NT)

        @pl.when(k == nk - 1)
        def _():
            dh = acc[...]
            xv = x_ref[...]
            r = lax.rsqrt(jnp.mean(xv * xv, axis=-1, keepdims=True) + NORM_EPS)
            gy = dh * g_ref[...]
            dx_ref[...] = dxn_ref[...] + r * gy - xv * (r * r * r) * jnp.mean(gy * xv, axis=-1, keepdims=True)
            dg_ref[...] += jnp.sum(dh * xv * r, axis=0, keepdims=True)

    return pl.pallas_call(
        body, name="in_proj_bwd", grid=(T // tm, nk),
        in_specs=[pl.BlockSpec((tm, tk), lambda i, k: (i, k)), pl.BlockSpec((D_MODEL, tk), lambda i, k: (0, k)),
                  pl.BlockSpec((tm, D_MODEL), lambda i, k: (i, 0)), pl.BlockSpec((1, D_MODEL), lambda i, k: (0, 0)),
                  pl.BlockSpec((tm, D_MODEL), lambda i, k: (i, 0))],
        out_specs=[pl.BlockSpec((tm, D_MODEL), lambda i, k: (i, 0)), pl.BlockSpec((1, D_MODEL), lambda i, k: (0, 0))],
        out_shape=[jax.ShapeDtypeStruct((T, D_MODEL), f32), jax.ShapeDtypeStruct((1, D_MODEL), f32)],
        scratch_shapes=[pltpu.VMEM((tm, D_MODEL), f32)],
        compiler_params=_params("arbitrary", "arbitrary"),
    )(dproj, w, x, g, dxn)


def _grad_w_in(h, dproj, *, tn=640, tk=1024):
    T = h.shape[0]
    tk = min(tk, T)

    def body(h_ref, dp_ref, o_ref):
        @pl.when(pl.program_id(1) == 0)
        def _():
            o_ref[...] = jnp.zeros_like(o_ref)
        o_ref[...] += _dot(h_ref[...], dp_ref[...], TN)

    return pl.pallas_call(
        body, name="grad_w_in", grid=(IN_WIDTH // tn, T // tk),
        in_specs=[pl.BlockSpec((tk, D_MODEL), lambda j, k: (k, 0)), pl.BlockSpec((tk, tn), lambda j, k: (k, j))],
        out_specs=pl.BlockSpec((D_MODEL, tn), lambda j, k: (0, j)),
        out_shape=jax.ShapeDtypeStruct((D_MODEL, IN_WIDTH), f32),
        compiler_params=_params("parallel", "arbitrary"),
    )(h, dproj)


def _lower_bound(lbp, layer):
    m = jnp.max(lbp, axis=0, keepdims=True)
    e = jnp.exp(lbp - m)
    p = e / jnp.sum(e, axis=0, keepdims=True)
    acc = p[0:1]
    for i in range(1, layer + 1):
        acc = acc + p[i:i + 1]
    return acc - p[0:1]


def _gate_parts(qr, fr, lb, lbf):
    sq = _sigmoid(qr)
    e = jnp.exp(-jnp.abs(fr))
    inv = 1.0 / (1.0 + e)
    pos = fr >= 0
    sg = jnp.where(pos, inv, e * inv)
    nsg = jnp.where(pos, e * inv, inv)
    fg = lbf + (1.0 - lb) * sg
    return qr * sq, sq, sg, nsg, fg, jnp.log(fg), (1.0 - lb) * nsg


def _anchor_masks(transposed=False):
    t = lax.broadcasted_iota(jnp.int32, (CHUNK, CHUNK), 1 if transposed else 0)
    s = lax.broadcasted_iota(jnp.int32, (CHUNK, CHUNK), 0 if transposed else 1)
    anchors = tuple(range(SUB - 1, CHUNK - 1, SUB))
    return anchors, [(t > a) & (s <= a) & (s > a - SUB) for a in anchors]


def _seg_sum(seg, x):
    hi = x.astype(bf16)
    return _dot(seg, hi) + _dot(seg, (x - hi.astype(f32)).astype(bf16))


def _hgrn_fwd(proj, lb_param, g_head, *, B, S, layer):
    T = B * S
    TB = min(256, S)
    nT, NC = S // TB, TB // CHUNK
    nC = S // CHUNK
    HD = HG_HEAD_DIM

    def body(q_ref, f_ref, i_ref, z_ref, lb_ref, gh_ref, cat_ref, op_ref, st_ref,
             s_scr, b_scr, k_scr):
        @pl.when(pl.program_id(2) == 0)
        def _():
            s_scr[...] = jnp.zeros_like(s_scr)
        lb = _lower_bound(lb_ref[...], layer)
        lbf = jnp.maximum(lb, LB_FLOOR)
        gh = gh_ref[...]
        r_i = lax.broadcasted_iota(jnp.int32, (CHUNK, CHUNK), 0)
        c_i = lax.broadcasted_iota(jnp.int32, (CHUNK, CHUNK), 1)
        tril = (r_i >= c_i).astype(f32)
        rows8 = lax.broadcasted_iota(jnp.int32, (8, HD), 0)
        lane_c = lax.broadcasted_iota(jnp.int32, (8, CHUNK), 1)
        anchors, masks = _anchor_masks()

        def chunk(c, st):
            rs = slice(c * CHUNK, (c + 1) * CHUNK)
            b_s, k_s = b_scr.at[c], k_scr.at[c]
            q, _, _, _, _, logf, k = _gate_parts(q_ref[rs, :], f_ref[rs, :], lb, lbf)
            v = i_ref[rs, :]
            b = _dot(tril, logf, precision=lax.Precision.HIGHEST)
            b_s[...] = b
            k_s[...] = k
            pieces = []
            for blk in range(CHUNK // SUB):
                r0 = blk * SUB
                bp = [b[r0 + 8 * i:r0 + 8 * i + 8] for i in range(SUB // 8)]
                qp = [q[r0 + 8 * i:r0 + 8 * i + 8] for i in range(SUB // 8)]
                ap = [jnp.zeros((8, CHUNK), f32) for _ in range(SUB // 8)]
                for s in range(SUB):
                    bs = b_s[r0 + s:r0 + s + 1, :]
                    ks = k_s[r0 + s:r0 + s + 1, :]
                    for i in range(s // 8, SUB // 8):
                        diff = bp[i] - bs
                        if i == s // 8:
                            diff = jnp.where(rows8 >= s - 8 * i, diff, NEG_INF)
                        col = jnp.sum(jnp.exp(diff) * qp[i] * ks, axis=1, keepdims=True)
                        ap[i] = jnp.where(lane_c == r0 + s, col, ap[i])
                pieces += ap
            a_all = jnp.concatenate(pieces, axis=0)
            for an, mk in zip(anchors, masks):
                beta = b_s[an:an + 1, :]
                qh = (q * jnp.exp(jnp.minimum(b - beta, 0.0))).astype(bf16)
                kh = (k * jnp.exp(jnp.minimum(beta - b, 0.0))).astype(bf16)
                a_all = a_all + jnp.where(mk, _dot(qh, kh, NT), 0.0)
            st_ref[0, 0, c] = st
            vb16 = v.astype(bf16)
            o = _dot(a_all.astype(bf16), vb16) + _dot((q * jnp.exp(b)).astype(bf16), st.astype(bf16), NT)
            b_end = b_s[CHUNK - 1:CHUNK, :]
            kdec = (k * jnp.exp(b_end - b)).astype(bf16)
            st_next = jnp.exp(b_end) * st + _dot(vb16, kdec, TN)
            rr = lax.rsqrt(jnp.mean(o * o, axis=-1, keepdims=True) + NORM_EPS)
            zr = z_ref[rs, :]
            cat_ref[rs, :] = (o * rr * gh * (zr * _sigmoid(zr))).astype(bf16)
            op_ref[rs, :] = o
            return st_next

        st = s_scr[...]
        for c in range(NC):
            st = chunk(c, st)
        s_scr[...] = st

    def col(part):
        return pl.BlockSpec((TB, HD), lambda b, h, n: (b * nT + n, part * HG_HEADS + h))

    out_col = pl.BlockSpec((TB, HD), lambda b, h, n: (b * nT + n, h))
    return pl.pallas_call(
        body, name=f"hgrn_fwd_l{layer}", grid=(B, HG_HEADS, nT),
        in_specs=[col(0), col(1), col(2), col(3),
                  pl.BlockSpec((DEPTH, HD), lambda b, h, n: (0, h)),
                  pl.BlockSpec((1, HD), lambda b, h, n: (0, 0))],
        out_specs=[out_col, out_col,
                   pl.BlockSpec((1, 1, NC, HD, HD), lambda b, h, n: (b, h, n, 0, 0))],
        out_shape=[jax.ShapeDtypeStruct((T, HG_WIDTH), bf16), jax.ShapeDtypeStruct((T, HG_WIDTH), f32),
                   jax.ShapeDtypeStruct((B, HG_HEADS, nC, HD, HD), f32)],
        scratch_shapes=[pltpu.VMEM((HD, HD), f32), pltpu.VMEM((NC, CHUNK, HD), f32), pltpu.VMEM((NC, CHUNK, HD), f32)],
        compiler_params=_params("parallel", "parallel", "arbitrary"),
    )(proj, proj, proj, proj, lb_param, g_head)


def _hgrn_bwd(proj, lb_param, g_head, o_pre, states, dcat, *, B, S, layer):
    T = B * S
    TB = min(256, S)
    nT, NC = S // TB, TB // CHUNK
    HD = HG_HEAD_DIM

    def body(q_ref, f_ref, i_ref, z_ref, lb_ref, gh_ref, op_ref, st_ref, dc_ref,
             dq_ref, df_ref, di_ref, dz_ref, dlb_ref, dgh_ref,
             ds_scr, b_scr, q_scr, do_scr, wk_scr):
        @pl.when(pl.program_id(2) == 0)
        def _():
            ds_scr[...] = jnp.zeros_like(ds_scr)
            dlb_ref[...] = jnp.zeros_like(dlb_ref)
            dgh_ref[...] = jnp.zeros_like(dgh_ref)
        lb = _lower_bound(lb_ref[...], layer)
        lbf = jnp.maximum(lb, LB_FLOOR)
        ind = (lb > LB_FLOOR).astype(f32)
        gh = gh_ref[...]
        r_i = lax.broadcasted_iota(jnp.int32, (CHUNK, CHUNK), 0)
        c_i = lax.broadcasted_iota(jnp.int32, (CHUNK, CHUNK), 1)
        tril = (r_i >= c_i).astype(f32)
        triu = (c_i >= r_i).astype(f32)
        rows8 = lax.broadcasted_iota(jnp.int32, (8, HD), 0)
        lane_c = lax.broadcasted_iota(jnp.int32, (8, CHUNK), 1)
        last_row = lax.broadcasted_iota(jnp.int32, (CHUNK, HD), 0) == CHUNK - 1
        anchors, masks = _anchor_masks()
        _, masks_t = _anchor_masks(transposed=True)
        seg_t = lax.broadcasted_iota(jnp.int32, (SUB, 8 * SUB), 0)
        seg_r = lax.broadcasted_iota(jnp.int32, (SUB, 8 * SUB), 1) // 8
        seg0 = (seg_r == seg_t).astype(bf16)
        seg1 = (seg_r[:, 0:4 * SUB] + 8 == seg_t[:, 0:4 * SUB]).astype(bf16)

        def chunk(c, dst1):
            rs = slice(c * CHUNK, (c + 1) * CHUNK)
            b_s, q_s, do_s = b_scr.at[c], q_scr.at[c], do_scr.at[c]
            qr, fr = q_ref[rs, :], f_ref[rs, :]
            q, sq, sg, nsg, fg, logf, k = _gate_parts(qr, fr, lb, lbf)
            v = i_ref[rs, :]
            b = _dot(tril, logf, precision=lax.Precision.HIGHEST)
            o = op_ref[rs, :]
            dc = dc_ref[rs, :]
            zr = z_ref[rs, :]
            sz = _sigmoid(zr)
            rr = lax.rsqrt(jnp.mean(o * o, axis=-1, keepdims=True) + NORM_EPS)
            dz_ref[rs, :] = (dc * (o * rr * gh) * (sz * (1.0 + zr * (1.0 - sz)))).astype(bf16)
            dn = dc * (zr * sz)
            dgh_ref[0, 0] += jnp.sum(dn * o * rr, axis=0, keepdims=True)
            gdn = dn * gh
            d_o = rr * gdn - o * (rr * rr * rr) * jnp.mean(gdn * o, axis=-1, keepdims=True)
            b_s[...] = b
            q_s[...] = q
            do_s[...] = d_o
            dob = d_o.astype(bf16)
            vb16 = v.astype(bf16)
            d_a = _dot(dob, vb16, NT)
            d_q = jnp.zeros((CHUNK, HD), f32)
            d_k = jnp.zeros((CHUNK, HD), f32)
            at_all = jnp.zeros((CHUNK, CHUNK), f32)
            for an, mk, mkt in zip(anchors, masks, masks_t):
                beta = b_s[an:an + 1, :]
                eq = jnp.exp(jnp.minimum(b - beta, 0.0))
                ek = jnp.exp(jnp.minimum(beta - b, 0.0))
                qh = (q * eq).astype(bf16)
                kh = (k * ek).astype(bf16)
                at_all = at_all + jnp.where(mkt, _dot(kh, qh, NT), 0.0)
                d_aa = jnp.where(mk, d_a, 0.0).astype(bf16)
                d_q = d_q + _dot(d_aa, kh) * eq
                d_k = d_k + _dot(d_aa, qh, TN) * ek
            st0 = st_ref[0, 0, c]
            dst1b = dst1.astype(bf16)
            eb = jnp.exp(b)
            b_end = b_s[CHUNK - 1:CHUNK, :]
            edec = jnp.exp(b_end - b)
            e_end = jnp.exp(b_end)
            kdec = (k * edec).astype(bf16)
            qdec = (q * eb).astype(bf16)
            d_q = d_q + _dot(dob, st0.astype(bf16)) * eb
            d_v = _dot(kdec, dst1b, NT)
            d_k = d_k + _dot(vb16, dst1b) * edec
            st1 = e_end * st0 + _dot(vb16, kdec, TN)
            rterm = jnp.sum(dst1 * st1, axis=0, keepdims=True)
            dst0 = e_end * dst1 + _dot(dob, qdec, TN)
            dq_blocks, dk_pieces, at_pieces = [], [], []
            for blk in range(CHUNK // SUB):
                r0 = blk * SUB
                wk = wk_scr.at[c * (CHUNK // SUB) + blk]
                bp = [b[r0 + 8 * i:r0 + 8 * i + 8] for i in range(SUB // 8)]
                kp = [k[r0 + 8 * i:r0 + 8 * i + 8] for i in range(SUB // 8)]
                vp = [v[r0 + 8 * i:r0 + 8 * i + 8] for i in range(SUB // 8)]
                dkp = [jnp.zeros((8, HD), f32) for _ in range(SUB // 8)]
                atp = [jnp.zeros((8, CHUNK), f32) for _ in range(SUB // 8)]
                for t in range(SUB):
                    bt = b_s[r0 + t:r0 + t + 1, :]
                    qt = q_s[r0 + t:r0 + t + 1, :]
                    dot_ = do_s[r0 + t:r0 + t + 1, :]
                    for i in range(t // 8 + 1):
                        diff = bt - bp[i]
                        if i == t // 8:
                            diff = jnp.where(rows8 <= t - 8 * i, diff, NEG_INF)
                        e = jnp.exp(diff)
                        a = jnp.sum(e * kp[i] * qt, axis=1, keepdims=True)
                        atp[i] = jnp.where(lane_c == r0 + t, a, atp[i])
                        w = jnp.sum(vp[i] * dot_, axis=1, keepdims=True) * e
                        dkp[i] = dkp[i] + w * qt
                        row = 8 * t if i == 0 else 8 * SUB + 8 * (t - 8)
                        wk[row:row + 8, :] = w * kp[i]
                dq_blocks.append(_seg_sum(seg0, wk[0:8 * SUB, :]) + _seg_sum(seg1, wk[8 * SUB:12 * SUB, :]))
                dk_pieces += dkp
                at_pieces += atp
            d_q = d_q + jnp.concatenate(dq_blocks, axis=0)
            d_k = d_k + jnp.concatenate(dk_pieces, axis=0)
            d_v = d_v + _dot((at_all + jnp.concatenate(at_pieces, axis=0)).astype(bf16), dob)
            db = q * d_q - k * d_k + jnp.where(last_row, rterm, 0.0)
            dlt = _dot(triu, db, precision=lax.Precision.HIGHEST) - fg * d_k
            df_ref[rs, :] = (dlt * (1.0 - lb) * sg * nsg / fg).astype(bf16)
            dlb_ref[0] += jnp.sum(dlt * (ind - sg) / fg, axis=0, keepdims=True)
            dq_ref[rs, :] = (d_q * (sq * (1.0 + qr * (1.0 - sq)))).astype(bf16)
            di_ref[rs, :] = d_v.astype(bf16)
            return dst0

        dst = ds_scr[...]
        for c in reversed(range(NC)):
            dst = chunk(c, dst)
        ds_scr[...] = dst

    def col(part):
        return pl.BlockSpec((TB, HD), lambda b, h, n: (b * nT + nT - 1 - n, part * HG_HEADS + h))

    hcol = pl.BlockSpec((TB, HD), lambda b, h, n: (b * nT + nT - 1 - n, h))
    return pl.pallas_call(
        body, name=f"hgrn_bwd_l{layer}", grid=(B, HG_HEADS, nT),
        in_specs=[col(0), col(1), col(2), col(3),
                  pl.BlockSpec((DEPTH, HD), lambda b, h, n: (0, h)),
                  pl.BlockSpec((1, HD), lambda b, h, n: (0, 0)),
                  hcol,
                  pl.BlockSpec((1, 1, NC, HD, HD), lambda b, h, n: (b, h, nT - 1 - n, 0, 0)),
                  hcol],
        out_specs=[hcol, hcol, hcol, hcol,
                   pl.BlockSpec((1, 1, HD), lambda b, h, n: (b, 0, h)),
                   pl.BlockSpec((1, 1, 1, HD), lambda b, h, n: (b, h, 0, 0))],
        out_shape=[jax.ShapeDtypeStruct((T, HG_WIDTH), bf16)] * 4 + [
            jax.ShapeDtypeStruct((B, 1, HG_WIDTH), f32), jax.ShapeDtypeStruct((B, HG_HEADS, 1, HD), f32)],
        scratch_shapes=[pltpu.VMEM((HD, HD), f32)] + [pltpu.VMEM((NC, CHUNK, HD), f32)] * 3
        + [pltpu.VMEM((NC * CHUNK // SUB, 12 * SUB, HD), f32)],
        compiler_params=_params("parallel", "parallel", "arbitrary"),
    )(proj, proj, proj, proj, lb_param, g_head, o_pre, states, dcat)


def _rope_tables(S):
    half = ATT_HEAD_DIM // 2
    inv_freq = ROPE_THETA ** (-jnp.arange(half, dtype=f32) / half)
    ang = jnp.arange(S, dtype=f32)[:, None] * inv_freq[None, :]
    cos, sin = jnp.cos(ang), jnp.sin(ang)
    return jnp.tile(jnp.concatenate([cos, cos], axis=1), (1, 2)), jnp.tile(jnp.concatenate([-sin, sin], axis=1), (1, 2))


def _swap_halves(x, first_half):
    return jnp.where(first_half, pltpu.roll(x, LANES - ATT_HEAD_DIM // 2, 1), pltpu.roll(x, ATT_HEAD_DIM // 2, 1))


def _rope(x, cos, sin, first_half):
    return x * cos + _swap_halves(x, first_half) * sin


def _rope_bwd(dy, cos, sin, first_half):
    return dy * cos + _swap_halves(dy * sin, first_half)


def _attn_consts(n):
    lane = lax.broadcasted_iota(jnp.int32, (1, LANES), 1)
    low = lane < ATT_HEAD_DIM
    first_half = (lane % ATT_HEAD_DIM) < ATT_HEAD_DIM // 2
    top = lax.broadcasted_iota(jnp.int32, (LANES, 1), 0) < ATT_HEAD_DIM
    s = lax.broadcasted_iota(jnp.int32, (2 * ATT_BLOCK, ATT_BLOCK), 0)
    t = lax.broadcasted_iota(jnp.int32, (2 * ATT_BLOCK, ATT_BLOCK), 1)
    mask = (s > t) & (s <= t + ATT_BLOCK) & ((s >= ATT_BLOCK) | (n > 0))
    return low, first_half, top, mask


def _dup_kv(x, low):
    rolled = pltpu.roll(x, ATT_HEAD_DIM, 1)
    return [jnp.where(low, x, rolled), jnp.where(low, rolled, x)]


def _attn_head(qtm, kd, vdt, sink, mask):
    s = jnp.where(mask, _dot(kd, qtm) * ATT_SCALE, NEG_INF)
    m = jnp.maximum(jnp.max(s, axis=0, keepdims=True), sink)
    p = jnp.exp(s - m)
    psink = jnp.exp(sink - m)
    inv = 1.0 / (jnp.sum(p, axis=0, keepdims=True) + psink)
    pn = p * inv
    return pn, psink * inv, _dot(vdt, pn.astype(bf16))


def _swa_fwd(proj, sink_b, cos, sin, *, B, S):
    T = B * S
    L = ATT_BLOCK
    nB = S // L

    def body(q_ref, z_ref, kvc_ref, kvp_ref, sk_ref, cc_ref, sc_ref, cp_ref, sp_ref, cat_ref):
        n = pl.program_id(1)
        low, first_half, top, mask = _attn_consts(n)
        cc, sc = cc_ref[...], sc_ref[...]
        kc = _rope(kvc_ref[:, 0:LANES], cc, sc, first_half)
        kp = _rope(kvp_ref[:, 0:LANES], cp_ref[...], sp_ref[...], first_half)
        kd = [x.astype(bf16) for x in _dup_kv(jnp.concatenate([kp, kc], axis=0), low)]
        vdt = [x.T.astype(bf16) for x in _dup_kv(jnp.concatenate([kvp_ref[:, LANES:2 * LANES], kvc_ref[:, LANES:2 * LANES]], axis=0), low)]
        for pair in range(ATT_HEADS // 2):
            cols = slice(pair * LANES, (pair + 1) * LANES)
            j = (2 * pair) // ATT_GROUP
            qt = _rope(q_ref[:, cols], cc, sc, first_half).T
            outs = []
            for hh in range(2):
                h = 2 * pair + hh
                qtm = jnp.where(top if hh == 0 else ~top, qt, 0.0).astype(bf16)
                _, _, o = _attn_head(qtm, kd[j], vdt[j], sk_ref[h:h + 1, 0:1], mask)
                outs.append(o)
            zp = z_ref[:, cols]
            cat_ref[:, cols] = (jnp.where(top, outs[0], outs[1]).T * (zp * _sigmoid(zp))).astype(bf16)

    cur = lambda b, n: (b * nB + n, 0)
    prev = lambda b, n: (b * nB + jnp.maximum(n - 1, 0), 0)
    return pl.pallas_call(
        body, name="swa_fwd", grid=(B, nB),
        in_specs=[pl.BlockSpec((L, ATT_WIDTH), lambda b, n: (b * nB + n, QA_BLK)),
                  pl.BlockSpec((L, ATT_WIDTH), lambda b, n: (b * nB + n, ZA_BLK)),
                  pl.BlockSpec((L, 2 * KV_WIDTH), lambda b, n: (b * nB + n, KV_BLK)),
                  pl.BlockSpec((L, 2 * KV_WIDTH), lambda b, n: (b * nB + jnp.maximum(n - 1, 0), KV_BLK)),
                  pl.BlockSpec((ATT_HEADS, LANES), lambda b, n: (0, 0)),
                  pl.BlockSpec((L, LANES), lambda b, n: (n, 0)), pl.BlockSpec((L, LANES), lambda b, n: (n, 0)),
                  pl.BlockSpec((L, LANES), lambda b, n: (jnp.maximum(n - 1, 0), 0)),
                  pl.BlockSpec((L, LANES), lambda b, n: (jnp.maximum(n - 1, 0), 0))],
        out_specs=pl.BlockSpec((L, ATT_WIDTH), cur),
        out_shape=jax.ShapeDtypeStruct((T, ATT_WIDTH), bf16),
        compiler_params=_params("parallel", "parallel"),
    )(proj, proj, proj, proj, sink_b, cos, sin, cos, sin)


def _swa_bwd(proj, sink_b, cos, sin, dcat, *, B, S):
    T = B * S
    L = ATT_BLOCK
    nB = S // L

    def body(q_ref, z_ref, kvc_ref, kvp_ref, sk_ref, cc_ref, sc_ref, cp_ref, sp_ref, dc_ref,
             dq_ref, dz_ref, dkv_ref, dsk_ref, carry, ds_st, pn_st, q_st, do_st):
        step = pl.program_id(1)
        n = nB - 1 - step

        @pl.when((pl.program_id(0) == 0) & (step == 0))
        def _():
            dsk_ref[...] = jnp.zeros_like(dsk_ref)

        @pl.when(step == 0)
        def _():
            carry[...] = jnp.zeros_like(carry)
        low, first_half, top, mask = _attn_consts(n)
        cc, sc, cp, sp = cc_ref[...], sc_ref[...], cp_ref[...], sp_ref[...]
        kc = _rope(kvc_ref[:, 0:LANES], cc, sc, first_half)
        kp = _rope(kvp_ref[:, 0:LANES], cp, sp, first_half)
        kdf = _dup_kv(jnp.concatenate([kp, kc], axis=0), low)
        vdf = _dup_kv(jnp.concatenate([kvp_ref[:, LANES:2 * LANES], kvc_ref[:, LANES:2 * LANES]], axis=0), low)
        kd = [x.astype(bf16) for x in kdf]
        vd = [x.astype(bf16) for x in vdf]
        kdt = [x.T.astype(bf16) for x in kdf]
        vdt = [x.T.astype(bf16) for x in vdf]
        dkd, dvd = [], []
        for pair in range(ATT_HEADS // 2):
            cols = slice(pair * LANES, (pair + 1) * LANES)
            j = (2 * pair) // ATT_GROUP
            qp = _rope(q_ref[:, cols], cc, sc, first_half)
            qt = qp.T
            zp = z_ref[:, cols]
            dc = dc_ref[:, cols]
            sz = _sigmoid(zp)
            d_o = dc * (zp * sz)
            dot_ = d_o.T
            res = []
            for hh in range(2):
                rsel = top if hh == 0 else ~top
                qtm = jnp.where(rsel, qt, 0.0).astype(bf16)
                pn, psn, o = _attn_head(qtm, kd[j], vdt[j], sk_ref[2 * pair + hh:2 * pair + hh + 1, 0:1], mask)
                res.append((rsel, pn, psn, o))
            ot = jnp.where(top, res[0][3], res[1][3])
            dz_ref[:, cols] = (dc * ot.T * (sz * (1.0 + zp * (1.0 - sz)))).astype(bf16)
            dqts = []
            for hh in range(2):
                h = 2 * pair + hh
                rsel, pn, psn, _ = res[hh]
                lsel = low if hh == 0 else ~low
                dotm = jnp.where(rsel, dot_, 0.0)
                delta = jnp.sum(dotm * ot, axis=0, keepdims=True)
                dst = (pn * (_dot(vd[j], dotm.astype(bf16)) - delta) * ATT_SCALE).astype(bf16)
                dsk_ref[h:h + 1, :] += jnp.zeros((1, LANES), f32) - jnp.sum(psn * delta)
                dqts.append(_dot(kdt[j], dst))
                g = h % ATT_GROUP
                ds_st[:, g * LANES:(g + 1) * LANES] = dst
                pn_st[:, g * LANES:(g + 1) * LANES] = pn.astype(bf16)
                q_st[g * LANES:(g + 1) * LANES, :] = jnp.where(lsel, qp, 0.0).astype(bf16)
                do_st[g * LANES:(g + 1) * LANES, :] = jnp.where(lsel, d_o, 0.0).astype(bf16)
            dq_ref[:, cols] = _rope_bwd(jnp.where(top, dqts[0], dqts[1]).T, cc, sc, first_half).astype(bf16)
            if (2 * pair + 2) % ATT_GROUP == 0:
                dkd.append(_dot(ds_st[...], q_st[...]))
                dvd.append(_dot(pn_st[...], do_st[...]))
        dk = [x + pltpu.roll(x, ATT_HEAD_DIM, 1) for x in dkd]
        dv = [x + pltpu.roll(x, ATT_HEAD_DIM, 1) for x in dvd]
        dk = jnp.where(low, dk[0], dk[1])
        dv = jnp.where(low, dv[0], dv[1])
        dkv_ref[:, 0:LANES] = (_rope_bwd(dk[L:2 * L], cc, sc, first_half) + carry[:, 0:LANES]).astype(bf16)
        dkv_ref[:, LANES:2 * LANES] = (dv[L:2 * L] + carry[:, LANES:2 * LANES]).astype(bf16)
        carry[:, 0:LANES] = _rope_bwd(dk[0:L], cp, sp, first_half)
        carry[:, LANES:2 * LANES] = dv[0:L]

    rev = lambda b, s: b * nB + nB - 1 - s
    revp = lambda b, s: b * nB + jnp.maximum(nB - 2 - s, 0)
    wide = lambda blk: pl.BlockSpec((L, ATT_WIDTH), lambda b, s: (rev(b, s), blk))
    tab = pl.BlockSpec((L, LANES), lambda b, s: (nB - 1 - s, 0))
    tabp = pl.BlockSpec((L, LANES), lambda b, s: (jnp.maximum(nB - 2 - s, 0), 0))
    return pl.pallas_call(
        body, name="swa_bwd", grid=(B, nB),
        in_specs=[wide(QA_BLK), wide(ZA_BLK),
                  pl.BlockSpec((L, 2 * KV_WIDTH), lambda b, s: (rev(b, s), KV_BLK)),
                  pl.BlockSpec((L, 2 * KV_WIDTH), lambda b, s: (revp(b, s), KV_BLK)),
                  pl.BlockSpec((ATT_HEADS, LANES), lambda b, s: (0, 0)),
                  tab, tab, tabp, tabp, wide(0)],
        out_specs=[wide(0), wide(0), pl.BlockSpec((L, 2 * KV_WIDTH), lambda b, s: (rev(b, s), 0)),
                   pl.BlockSpec((ATT_HEADS, LANES), lambda b, s: (0, 0))],
        out_shape=[jax.ShapeDtypeStruct((T, ATT_WIDTH), bf16), jax.ShapeDtypeStruct((T, ATT_WIDTH), bf16),
                   jax.ShapeDtypeStruct((T, 2 * KV_WIDTH), bf16), jax.ShapeDtypeStruct((ATT_HEADS, LANES), f32)],
        scratch_shapes=[pltpu.VMEM((L, 2 * KV_WIDTH), f32),
                        pltpu.VMEM((2 * L, ATT_GROUP * LANES), bf16), pltpu.VMEM((2 * L, ATT_GROUP * LANES), bf16),
                        pltpu.VMEM((ATT_GROUP * LANES, LANES), bf16), pltpu.VMEM((ATT_GROUP * LANES, LANES), bf16)],
        compiler_params=_params("arbitrary", "arbitrary"),
    )(proj, proj, proj, proj, sink_b, cos, sin, cos, sin, dcat)


def _local_grads(x, target, w_in_p, w_out_b, g_pre, g_post, lb_param, g_head, sinks, *, B, S):
    cos, sin = _rope_tables(S)
    saved = []
    for l in range(DEPTH):
        proj, h = _in_proj(x, g_pre[l:l + 1], w_in_p[l])
        ch, o_pre, states = _hgrn_fwd(proj, lb_param, g_head[l:l + 1], B=B, S=S, layer=l)
        sink_b = jnp.broadcast_to(sinks[l][:, None], (ATT_HEADS, LANES))
        ca = _swa_fwd(proj, sink_b, cos, sin, B=B, S=S)
        xn, y = _out_proj(ch, ca, w_out_b[l], x, g_post[l:l + 1])
        saved.append((x, proj, h, ch, o_pre, states, sink_b, ca, y))
        x = xn
    dx, loss = _loss_head(x, target)
    gw_in, gw_out, gg_pre, gg_post, g_lb, gg_head, g_sinks = [], [], [], [], [], [], []
    for l in reversed(range(DEPTH)):
        x_in, proj, h, ch, o_pre, states, sink_b, ca, y = saved[l]
        dch, dca, dwo, dgpost = _out_proj_bwd(dx, y, g_post[l:l + 1], w_out_b[l], ch, ca)
        dq, df, di, dz, dlb, dgh = _hgrn_bwd(proj, lb_param, g_head[l:l + 1], o_pre, states, dch, B=B, S=S, layer=l)
        dqa, dza, dkv, dsk = _swa_bwd(proj, sink_b, cos, sin, dca, B=B, S=S)
        dproj = jnp.concatenate([dq, df, di, dz, dqa, dza, dkv], axis=1)
        dx, dgpre = _in_proj_bwd(dproj, w_in_p[l], x_in, g_pre[l:l + 1], dx)
        gw_in.append(_grad_w_in(h, dproj))
        gw_out.append(dwo)
        gg_pre.append(dgpre[0])
        gg_post.append(dgpost[0])
        g_lb.append(jnp.sum(dlb, axis=(0, 1)))
        gg_head.append(jnp.sum(dgh, axis=(0, 1, 2)))
        g_sinks.append(dsk[:, 0])
    rev = lambda xs: jnp.stack(xs[::-1])
    return (loss[0, 0], dx, rev(gw_in), rev(gw_out), rev(gg_pre), rev(gg_post), rev(g_lb), rev(gg_head), rev(g_sinks))


MESH = pl.DeviceIdType.MESH
ANY = pl.BlockSpec(memory_space=pl.ANY)


def _place():
    x, y, c = lax.axis_index("x"), lax.axis_index("y"), lax.axis_index("c")
    return x, y, c, [(1 - x, y), (x, 1 - y), (1 - x, 1 - y)]


def _rcopy(src, dst, send, recv, k, to):
    return pltpu.make_async_remote_copy(src_ref=src, dst_ref=dst, send_sem=send.at[k], recv_sem=recv.at[k],
                                        device_id=to, device_id_type=MESH)


def _gather_shards(shards):
    n = len(shards)

    def body(*refs):
        ins, outs = refs[:n], refs[n:2 * n]
        send, recv, lsem = refs[2 * n:]
        x, y, c, chips = _place()
        me = 2 * x + y
        sib = (x, y, 1 - c)

        def half(ref, a, hc):
            r = shards[a].shape[1] // 2
            return ref.at[:, pl.ds(hc * r, r), :]

        local = [pltpu.make_async_copy(ins[a], outs[a].at[me], lsem.at[a]) for a in range(n)]
        for cp in local:
            cp.start()
        first = []
        for j, chip in enumerate(chips):
            for a in range(n):
                first.append(_rcopy(half(ins[a], a, c), half(outs[a].at[me], a, c), send, recv, j * n + a, (*chip, c)))
        for cp in first:
            cp.start()
        passed = []
        for j, (px, py) in enumerate(chips):
            for a in range(n):
                blk = half(outs[a].at[2 * px + py], a, c)
                _rcopy(blk, blk, send, recv, j * n + a, sib).wait_recv()
                cp = _rcopy(blk, blk, send, recv, (3 + j) * n + a, sib)
                cp.start()
                passed.append(cp)
        for j, (px, py) in enumerate(chips):
            for a in range(n):
                blk = half(outs[a].at[2 * px + py], a, 1 - c)
                _rcopy(blk, blk, send, recv, (3 + j) * n + a, sib).wait_recv()
        for cp in first + passed:
            cp.wait_send()
        for cp in local:
            cp.wait()

    return pl.pallas_call(
        body, name="gather_shards",
        in_specs=[ANY] * n, out_specs=[ANY] * n,
        out_shape=[jax.ShapeDtypeStruct((4,) + s.shape, s.dtype) for s in shards],
        scratch_shapes=[pltpu.SemaphoreType.DMA((6 * n,)), pltpu.SemaphoreType.DMA((6 * n,)),
                        pltpu.SemaphoreType.DMA((n,))],
    )(*shards)


def _pair_exchange(parts):
    n = len(parts)

    def body(*refs):
        ins, outs = refs[:n], refs[n:2 * n]
        send, recv = refs[2 * n:]
        x, y, c, _ = _place()
        cps = []
        for a in range(n):
            r = parts[a].shape[2] // 2
            cps.append(_rcopy(ins[a].at[:, :, pl.ds((1 - c) * r, r), :], outs[a], send, recv, a, (x, y, 1 - c)))
        for cp in cps:
            cp.start()
        for cp in cps:
            cp.wait()

    return pl.pallas_call(
        body, name="pair_exchange", in_specs=[ANY] * n, out_specs=[ANY] * n,
        out_shape=[jax.ShapeDtypeStruct(p.shape[:2] + (p.shape[2] // 2, p.shape[3]), p.dtype) for p in parts],
        scratch_shapes=[pltpu.SemaphoreType.DMA((n,)), pltpu.SemaphoreType.DMA((n,))],
    )(*parts)


def _pair_add(part, got, *, rows=256):
    K, L, R, C = part.shape
    r = R // 2
    rows = min(rows, r)
    nb = r // rows

    def body(c_ref, a_ref, b_ref, o_ref):
        o_ref[...] = (a_ref[...] + b_ref[...]).astype(bf16)

    blk = (1, 1, rows, C)
    return pl.pallas_call(
        body, name="pair_add",
        grid_spec=pltpu.PrefetchScalarGridSpec(
            num_scalar_prefetch=1, grid=(K, L, nb),
            in_specs=[pl.BlockSpec(blk, lambda k, l, i, c: (k, l, c[0] * nb + i, 0)),
                      pl.BlockSpec(blk, lambda k, l, i, c: (k, l, i, 0))],
            out_specs=pl.BlockSpec(blk, lambda k, l, i, c: (k, l, i, 0))),
        out_shape=jax.ShapeDtypeStruct((K, L, r, C), bf16),
        compiler_params=_params("parallel", "parallel", "parallel"),
    )(jnp.reshape(lax.axis_index("c"), (1,)).astype(jnp.int32), part, got)


def _chip_exchange(sums):
    n = len(sums)

    def body(*refs):
        ins, outs = refs[:n], refs[n:2 * n]
        send, recv = refs[2 * n:]
        x, y, c, chips = _place()
        cps = []
        for j, (px, py) in enumerate(chips):
            for a in range(n):
                cps.append(_rcopy(ins[a].at[2 * px + py], outs[a].at[j], send, recv, j * n + a, (px, py, c)))
        for cp in cps:
            cp.start()
        for cp in cps:
            cp.wait()

    return pl.pallas_call(
        body, name="chip_exchange", in_specs=[ANY] * n, out_specs=[ANY] * n,
        out_shape=[jax.ShapeDtypeStruct((3,) + s.shape[1:], s.dtype) for s in sums],
        scratch_shapes=[pltpu.SemaphoreType.DMA((3 * n,)), pltpu.SemaphoreType.DMA((3 * n,))],
    )(*sums)


def _chip_sum(mine, got, *, rows=256):
    K, L, r, C = mine.shape
    rows = min(rows, r)

    def body(k_ref, a_ref, b_ref, o_ref):
        o_ref[0] = (a_ref[0, 0].astype(f32) + b_ref[0, 0].astype(f32)) + (b_ref[1, 0].astype(f32) + b_ref[2, 0].astype(f32))

    chip = 2 * lax.axis_index("x") + lax.axis_index("y")
    return pl.pallas_call(
        body, name="chip_sum",
        grid_spec=pltpu.PrefetchScalarGridSpec(
            num_scalar_prefetch=1, grid=(L, r // rows),
            in_specs=[pl.BlockSpec((1, 1, rows, C), lambda l, i, k: (k[0], l, i, 0)),
                      pl.BlockSpec((3, 1, rows, C), lambda l, i, k: (0, l, i, 0))],
            out_specs=pl.BlockSpec((1, rows, C), lambda l, i, k: (l, i, 0))),
        out_shape=jax.ShapeDtypeStruct((L, r, C), f32),
        compiler_params=_params("parallel", "parallel"),
    )(jnp.reshape(chip, (1,)).astype(jnp.int32), mine, got)


def _pair_share(halves):
    n = len(halves)

    def body(*refs):
        ins, outs = refs[:n], refs[n:2 * n]
        send, recv, lsem = refs[2 * n:]
        x, y, c, _ = _place()
        local, cps = [], []
        for a in range(n):
            r = halves[a].shape[1]
            mine = outs[a].at[:, pl.ds(c * r, r), :]
            local.append(pltpu.make_async_copy(ins[a], mine, lsem.at[a]))
            cps.append(_rcopy(ins[a], mine, send, recv, a, (x, y, 1 - c)))
        for cp in local + cps:
            cp.start()
        for a in range(n):
            r = halves[a].shape[1]
            theirs = outs[a].at[:, pl.ds((1 - c) * r, r), :]
            _rcopy(theirs, theirs, send, recv, a, (x, y, 1 - c)).wait_recv()
        for cp in cps:
            cp.wait_send()
        for cp in local:
            cp.wait()

    return pl.pallas_call(
        body, name="pair_share", in_specs=[ANY] * n, out_specs=[ANY] * n,
        out_shape=[jax.ShapeDtypeStruct((h.shape[0], 2 * h.shape[1], h.shape[2]), h.dtype) for h in halves],
        scratch_shapes=[pltpu.SemaphoreType.DMA((n,)), pltpu.SemaphoreType.DMA((n,)), pltpu.SemaphoreType.DMA((n,))],
    )(*halves)


def _all_sum_small(v):
    def body(v_ref, o_ref, buf, send, recv):
        x, y, c, _ = _place()
        me = 4 * x + 2 * y + c
        buf[me] = v_ref[...]
        cps = []
        for m in range(1, 8):
            to = (x ^ (m >> 2), y ^ ((m >> 1) & 1), c ^ (m & 1))
            cps.append(_rcopy(v_ref, buf.at[me], send, recv, m - 1, to))
        for cp in cps:
            cp.start()
        for cp in cps:
            cp.wait()
        acc = buf[0]
        for d in range(1, 8):
            acc = acc + buf[d]
        o_ref[...] = acc

    vm = pl.BlockSpec(memory_space=pltpu.VMEM)
    return pl.pallas_call(
        body, name="all_sum_small", in_specs=[vm], out_specs=vm,
        out_shape=jax.ShapeDtypeStruct(v.shape, v.dtype),
        scratch_shapes=[pltpu.VMEM((8,) + v.shape, v.dtype), pltpu.SemaphoreType.DMA((7,)), pltpu.SemaphoreType.DMA((7,))],
    )(v)


def _adamw_math(w, g, m, v):
    m = ADAM_B1 * m + (1.0 - ADAM_B1) * g
    v = ADAM_B2 * v + (1.0 - ADAM_B2) * (g * g)
    m_hat = m / (1.0 - ADAM_B1 ** ADAM_STEP)
    v_hat = v / (1.0 - ADAM_B2 ** ADAM_STEP)
    return -ADAM_LR * (m_hat / (jnp.sqrt(v_hat) + ADAM_EPS) + ADAM_WD * w), m, v


def _adamw(w, g, m, v, *, rows=256):
    L, R, C = w.shape
    rows = min(rows, R)

    def body(w_ref, g_ref, m_ref, v_ref, d_ref, mo_ref, vo_ref):
        d_ref[...], mo_ref[...], vo_ref[...] = _adamw_math(w_ref[...], g_ref[...], m_ref[...], v_ref[...])

    blk = pl.BlockSpec((1, rows, C), lambda l, i: (l, i, 0))
    return pl.pallas_call(
        body, name="adamw", grid=(L, R // rows), in_specs=[blk] * 4, out_specs=[blk] * 3,
        out_shape=[jax.ShapeDtypeStruct(w.shape, f32)] * 3,
        compiler_params=_params("parallel", "parallel"),
    )(w, g, m, v)


SMALL_ROWS = 4 * DEPTH


def _pack_small(g_pre, g_post, lb, g_head, sinks, loss=None):
    rows = []
    for l in range(DEPTH):
        tail = [g_head[l], sinks[l]]
        if loss is not None and l == 0:
            tail.append(jnp.reshape(loss, (1,)))
        tail = jnp.concatenate(tail)
        rows += [g_pre[l], g_post[l], lb[l], jnp.pad(tail, (0, D_MODEL - tail.shape[0]))]
    return jnp.stack(rows)


def _unpack_small(p):
    g_pre = jnp.stack([p[4 * l] for l in range(DEPTH)])
    g_post = jnp.stack([p[4 * l + 1] for l in range(DEPTH)])
    lb = jnp.stack([p[4 * l + 2] for l in range(DEPTH)])
    g_head = jnp.stack([p[4 * l + 3, :HG_HEAD_DIM] for l in range(DEPTH)])
    sinks = jnp.stack([p[4 * l + 3, HG_HEAD_DIM:HG_HEAD_DIM + ATT_HEADS] for l in range(DEPTH)])
    return g_pre, g_post, lb, g_head, sinks


def _small_update(gsum, w, m, v):
    def body(g_ref, w_ref, m_ref, v_ref, go_ref, d_ref, mo_ref, vo_ref):
        g = g_ref[...]
        w = w_ref[...]
        lbp = [w[4 * l + 2:4 * l + 3] for l in range(DEPTH)]
        mx = functools.reduce(jnp.maximum, lbp)
        e = [jnp.exp(t - mx) for t in lbp]
        tot = functools.reduce(jnp.add, e)
        p = [t / tot for t in e]
        glb = [g[4 * l + 2:4 * l + 3] for l in range(DEPTH)]
        row = lax.broadcasted_iota(jnp.int32, g.shape, 0)
        for j in range(DEPTH):
            gj = jnp.zeros_like(p[0])
            for l in range(DEPTH):
                for i in range(1, l + 1):
                    gj = gj + glb[l] * p[i] * ((1.0 if i == j else 0.0) - p[j])
            g = jnp.where(row == 4 * j + 2, gj, g)
        go_ref[...] = g
        d_ref[...], mo_ref[...], vo_ref[...] = _adamw_math(w, g, m_ref[...], v_ref[...])

    vm = pl.BlockSpec(memory_space=pltpu.VMEM)
    return pl.pallas_call(
        body, name="small_update", in_specs=[vm] * 4, out_specs=[vm] * 4,
        out_shape=[jax.ShapeDtypeStruct(gsum.shape, f32)] * 4,
    )(gsum, w, m, v)


def _to_internal_cols(w):
    return jnp.concatenate([w[..., 0:5120], w[..., 5376:6400], w[..., 5120:5376]], axis=-1)


def _from_internal_cols(w):
    return jnp.concatenate([w[..., 0:5120], w[..., 6144:6400], w[..., 5120:6144]], axis=-1)


def kernel(x, w_in, w_out, g_pre, g_post, lb_param, g_head, sinks, loss_target, m_w_in, m_w_out, m_g_pre, m_g_post, m_lb_param, m_g_head, m_sinks, v_w_in, v_w_out, v_g_pre, v_g_post, v_lb_param, v_g_head, v_sinks):
    B, S, _ = x.shape
    T = B * S
    L = DEPTH
    wi_all, wo_all = _gather_shards([w_in.astype(bf16), w_out.astype(bf16)])
    w_in_p = _to_internal_cols(jnp.transpose(wi_all, (1, 2, 0, 3)).reshape(L, D_MODEL, IN_WIDTH))
    w_out_b = jnp.transpose(wo_all, (1, 0, 2, 3)).reshape(L, MIX_WIDTH, D_MODEL)

    loss, dx, gwi, gwo, ggpre, ggpost, glb, gghead, gsinks = _local_grads(
        x.reshape(T, D_MODEL), loss_target.reshape(T, D_MODEL), w_in_p, w_out_b,
        g_pre, g_post, lb_param, g_head, sinks, B=B, S=S)

    gwi_s = jnp.transpose(_from_internal_cols(gwi).reshape(L, D_MODEL, 4, IN_WIDTH // 4), (2, 0, 1, 3))
    gwo_s = jnp.transpose(gwo.reshape(L, 4, MIX_WIDTH // 4, D_MODEL), (1, 0, 2, 3))
    parts = [gwi_s, gwo_s]
    got = _pair_exchange(parts)
    sums = [_pair_add(p, r) for p, r in zip(parts, got)]
    recv = _chip_exchange(sums)
    halves = [_chip_sum(s, r) for s, r in zip(sums, recv)]
    grad_w_in, grad_w_out = _pair_share(halves)

    d_w_in, nm_w_in, nv_w_in = _adamw(w_in, grad_w_in, m_w_in, v_w_in)
    d_w_out, nm_w_out, nv_w_out = _adamw(w_out, grad_w_out, m_w_out, v_w_out)

    gsum = _all_sum_small(_pack_small(ggpre, ggpost, glb, gghead, gsinks, loss))
    gs, ds, ms, vs = _small_update(
        gsum, _pack_small(g_pre, g_post, lb_param, g_head, sinks),
        _pack_small(m_g_pre, m_g_post, m_lb_param, m_g_head, m_sinks),
        _pack_small(v_g_pre, v_g_post, v_lb_param, v_g_head, v_sinks))
    loss_all = gsum[3, HG_HEAD_DIM + ATT_HEADS]
    return (loss_all, dx.reshape(B, S, D_MODEL), grad_w_in, grad_w_out, *_unpack_small(gs),
            d_w_in, d_w_out, *_unpack_small(ds), nm_w_in, nm_w_out, *_unpack_small(ms),
            nv_w_in, nv_w_out, *_unpack_small(vs))
```

```python
import functools
import math

import jax
import jax.numpy as jnp
import numpy as np
from jax import lax
from jax.experimental import pallas as pl
from jax.experimental.pallas import tpu as pltpu

f32 = jnp.float32
bf16 = jnp.bfloat16

D_MODEL = 1024
DEPTH = 2
HG_WIDTH = 1024
HG_HEAD_DIM = 128
HG_HEADS = 8
CHUNK = 64
SUB = 16
ATT_WIDTH = 1024
ATT_HEAD_DIM = 64
ATT_HEADS = 16
ATT_GROUP = 8
KV_WIDTH = 128
ATT_BLOCK = 128
ATT_SCALE = 1.0 / math.sqrt(ATT_HEAD_DIM)
ROPE_THETA = 10000.0
IN_WIDTH = 6400
MIX_WIDTH = 2048
NORM_EPS = 1e-6
NEG_INF = -1e30
LB_FLOOR = 1e-20
LANES = 128
VMEM_LIMIT = 48 * 1024 * 1024

ADAM_LR = 0.001
ADAM_B1 = 0.9
ADAM_B2 = 0.999
ADAM_EPS = 1e-08
ADAM_WD = 0.01
ADAM_STEP = 10

COL_PERM = np.concatenate([np.arange(0, 5120), np.arange(5376, 6400), np.arange(5120, 5376)])
COL_INV = np.argsort(COL_PERM)
QA_BLK, ZA_BLK, KV_BLK = 4, 5, 24

NT = (((1,), (1,)), ((), ()))
TN = (((0,), (0,)), ((), ()))


def _dot(a, b, dims=None, precision=None):
    if dims is None:
        return jnp.dot(a, b, preferred_element_type=f32, precision=precision)
    return lax.dot_general(a, b, dims, preferred_element_type=f32, precision=precision)


def _sigmoid(x):
    return 1.0 / (1.0 + jnp.exp(-x))


def _params(*sem):
    return pltpu.CompilerParams(dimension_semantics=sem, vmem_limit_bytes=VMEM_LIMIT)


def _in_proj(x, g, wt, *, tm=512, tn=1280):
    T = x.shape[0]
    tm = min(tm, T)

    def body(x_ref, g_ref, w_ref, p_ref, h_ref, hs):
        @pl.when(pl.program_id(1) == 0)
        def _():
            xv = x_ref[...]
            r = lax.rsqrt(jnp.mean(xv * xv, axis=-1, keepdims=True) + NORM_EPS)
            hv = (xv * r * g_ref[...]).astype(bf16)
            hs[...] = hv
            h_ref[...] = hv
        p_ref[...] = _dot(hs[...], w_ref[...], NT)

    return pl.pallas_call(
        body, name="in_proj", grid=(T // tm, IN_WIDTH // tn),
        in_specs=[pl.BlockSpec((tm, D_MODEL), lambda i, j: (i, 0)),
                  pl.BlockSpec((1, D_MODEL), lambda i, j: (0, 0)),
                  pl.BlockSpec((tn, D_MODEL), lambda i, j: (j, 0))],
        out_specs=[pl.BlockSpec((tm, tn), lambda i, j: (i, j)),
                   pl.BlockSpec((tm, D_MODEL), lambda i, j: (i, 0))],
        out_shape=[jax.ShapeDtypeStruct((T, IN_WIDTH), f32), jax.ShapeDtypeStruct((T, D_MODEL), bf16)],
        scratch_shapes=[pltpu.VMEM((tm, D_MODEL), bf16)],
        compiler_params=_params("parallel", "arbitrary"),
    )(x, g, wt)


def _out_proj(ch, ca, wo, x, g, *, tm=512):
    T = x.shape[0]
    tm = min(tm, T)
    half = MIX_WIDTH // 2

    def body(ch_ref, ca_ref, wo_ref, x_ref, g_ref, xn_ref, y_ref):
        y = _dot(ch_ref[...], wo_ref[0:half, :]) + _dot(ca_ref[...], wo_ref[half:MIX_WIDTH, :])
        r = lax.rsqrt(jnp.mean(y * y, axis=-1, keepdims=True) + NORM_EPS)
        y_ref[...] = y
        xn_ref[...] = x_ref[...] + y * r * g_ref[...]

    row = lambda i: (i, 0)
    fixed = lambda i: (0, 0)
    return pl.pallas_call(
        body, name="out_proj", grid=(T // tm,),
        in_specs=[pl.BlockSpec((tm, half), row), pl.BlockSpec((tm, half), row),
                  pl.BlockSpec((MIX_WIDTH, D_MODEL), fixed), pl.BlockSpec((tm, D_MODEL), row),
                  pl.BlockSpec((1, D_MODEL), fixed)],
        out_specs=[pl.BlockSpec((tm, D_MODEL), row), pl.BlockSpec((tm, D_MODEL), row)],
        out_shape=[jax.ShapeDtypeStruct((T, D_MODEL), f32)] * 2,
        compiler_params=_params("parallel"),
    )(ch, ca, wo, x, g)


def _loss_head(y, target, *, tm=512):
    T = y.shape[0]
    tm = min(tm, T)

    def body(y_ref, t_ref, d_ref, l_ref):
        @pl.when(pl.program_id(0) == 0)
        def _():
            l_ref[...] = jnp.zeros_like(l_ref)
        err = y_ref[...] - t_ref[...]
        d_ref[...] = err * (1.0 / D_MODEL)
        l_ref[...] += jnp.sum(err * err) * (0.5 / D_MODEL)

    row = lambda i: (i, 0)
    return pl.pallas_call(
        body, name="loss_head", grid=(T // tm,),
        in_specs=[pl.BlockSpec((tm, D_MODEL), row), pl.BlockSpec((tm, D_MODEL), row)],
        out_specs=[pl.BlockSpec((tm, D_MODEL), row), pl.BlockSpec((8, LANES), lambda i: (0, 0))],
        out_shape=[jax.ShapeDtypeStruct((T, D_MODEL), f32), jax.ShapeDtypeStruct((8, LANES), f32)],
        compiler_params=_params("arbitrary"),
    )(y, target)


def _out_proj_bwd(dxn, y, g, wo, ch, ca, *, tm=256):
    T = y.shape[0]
    tm = min(tm, T)
    half = MIX_WIDTH // 2

    def body(dx_ref, y_ref, g_ref, wo_ref, ch_ref, ca_ref, dch_ref, dca_ref, dwo_ref, dg_ref):
        @pl.when(pl.program_id(0) == 0)
        def _():
            dwo_ref[...] = jnp.zeros_like(dwo_ref)
            dg_ref[...] = jnp.zeros_like(dg_ref)
        y = y_ref[...]
        dx = dx_ref[...]
        r = lax.rsqrt(jnp.mean(y * y, axis=-1, keepdims=True) + NORM_EPS)
        gy = dx * g_ref[...]
        dy = r * gy - y * (r * r * r) * jnp.mean(gy * y, axis=-1, keepdims=True)
        dg_ref[...] += jnp.sum(dx * y * r, axis=0, keepdims=True)
        dyb = dy.astype(bf16)
        dch_ref[...] = _dot(dyb, wo_ref[0:half, :], NT)
        dca_ref[...] = _dot(dyb, wo_ref[half:MIX_WIDTH, :], NT)
        dwo_ref[0:half, :] += _dot(ch_ref[...], dyb, TN)
        dwo_ref[half:MIX_WIDTH, :] += _dot(ca_ref[...], dyb, TN)

    row = lambda i: (i, 0)
    fixed = lambda i: (0, 0)
    return pl.pallas_call(
        body, name="out_proj_bwd", grid=(T // tm,),
        in_specs=[pl.BlockSpec((tm, D_MODEL), row), pl.BlockSpec((tm, D_MODEL), row),
                  pl.BlockSpec((1, D_MODEL), fixed), pl.BlockSpec((MIX_WIDTH, D_MODEL), fixed),
                  pl.BlockSpec((tm, half), row), pl.BlockSpec((tm, half), row)],
        out_specs=[pl.BlockSpec((tm, half), row), pl.BlockSpec((tm, half), row),
                   pl.BlockSpec((MIX_WIDTH, D_MODEL), fixed), pl.BlockSpec((1, D_MODEL), fixed)],
        out_shape=[jax.ShapeDtypeStruct((T, half), f32), jax.ShapeDtypeStruct((T, half), f32),
                   jax.ShapeDtypeStruct((MIX_WIDTH, D_MODEL), f32), jax.ShapeDtypeStruct((1, D_MODEL), f32)],
        compiler_params=_params("arbitrary"),
    )(dxn, y, g, wo, ch, ca)


def _in_proj_bwd(dproj, wt, x, g, dxn, *, tm=512, tk=1280):
    T = x.shape[0]
    tm = min(tm, T)
    nk = IN_WIDTH // tk

    def body(dp_ref, w_ref, x_ref, g_ref, dxn_ref, dx_ref, dg_ref, acc):
        i, k = pl.program_id(0), pl.program_id(1)

        @pl.when((i == 0) & (k == 0))
        def _():
            dg_ref[...] = jnp.zeros_like(dg_ref)

        @pl.when(k == 0)
        def _():
            acc[...] = jnp.zeros_like(acc)
        acc[...] += _dot(dp_ref[...], w_ref[...])

        @pl.when(k == nk - 1)
        def _():
            dh = acc[...]
            xv = x_ref[...]
            r = lax.rsqrt(jnp.mean(xv * xv, axis=-1, keepdims=True) + NORM_EPS)
            gy = dh * g_ref[...]
            dx_ref[...] = dxn_ref[...] + r * gy - xv * (r * r * r) * jnp.mean(gy * xv, axis=-1, keepdims=True)
            dg_ref[...] += jnp.sum(dh * xv * r, axis=0, keepdims=True)

    return pl.pallas_call(
        body, name="in_proj_bwd", grid=(T // tm, nk),
        in_specs=[pl.BlockSpec((tm, tk), lambda i, k: (i, k)), pl.BlockSpec((tk, D_MODEL), lambda i, k: (k, 0)),
                  pl.BlockSpec((tm, D_MODEL), lambda i, k: (i, 0)), pl.BlockSpec((1, D_MODEL), lambda i, k: (0, 0)),
                  pl.BlockSpec((tm, D_MODEL), lambda i, k: (i, 0))],
        out_specs=[pl.BlockSpec((tm, D_MODEL), lambda i, k: (i, 0)), pl.BlockSpec((1, D_MODEL), lambda i, k: (0, 0))],
        out_shape=[jax.ShapeDtypeStruct((T, D_MODEL), f32), jax.ShapeDtypeStruct((1, D_MODEL), f32)],
        scratch_shapes=[pltpu.VMEM((tm, D_MODEL), f32)],
        compiler_params=_params("arbitrary", "arbitrary"),
    )(dproj, wt, x, g, dxn)


def _grad_w_in(h, dproj, *, tn=640, tk=1024):
    T = h.shape[0]
    tk = min(tk, T)

    def body(h_ref, dp_ref, o_ref):
        @pl.when(pl.program_id(1) == 0)
        def _():
            o_ref[...] = jnp.zeros_like(o_ref)
        o_ref[...] += _dot(dp_ref[...], h_ref[...], TN)

    return pl.pallas_call(
        body, name="grad_w_in", grid=(IN_WIDTH // tn, T // tk),
        in_specs=[pl.BlockSpec((tk, D_MODEL), lambda j, k: (k, 0)), pl.BlockSpec((tk, tn), lambda j, k: (k, j))],
        out_specs=pl.BlockSpec((tn, D_MODEL), lambda j, k: (j, 0)),
        out_shape=jax.ShapeDtypeStruct((IN_WIDTH, D_MODEL), f32),
        compiler_params=_params("parallel", "arbitrary"),
    )(h, dproj)


def _lower_bound(lbp, layer):
    m = jnp.max(lbp, axis=0, keepdims=True)
    e = jnp.exp(lbp - m)
    p = e / jnp.sum(e, axis=0, keepdims=True)
    acc = p[0:1]
    for i in range(1, layer + 1):
        acc = acc + p[i:i + 1]
    return acc - p[0:1]


def _gate_parts(qr, fr, lb, lbf):
    sq = _sigmoid(qr)
    e = jnp.exp(-jnp.abs(fr))
    inv = 1.0 / (1.0 + e)
    pos = fr >= 0
    sg = jnp.where(pos, inv, e * inv)
    nsg = jnp.where(pos, e * inv, inv)
    fg = lbf + (1.0 - lb) * sg
    return qr * sq, sq, sg, nsg, fg, jnp.log(fg), (1.0 - lb) * nsg


def _anchor_masks(transposed=False):
    t = lax.broadcasted_iota(jnp.int32, (CHUNK, CHUNK), 1 if transposed else 0)
    s = lax.broadcasted_iota(jnp.int32, (CHUNK, CHUNK), 0 if transposed else 1)
    anchors = tuple(range(SUB - 1, CHUNK - 1, SUB))
    return anchors, [(t > a) & (s <= a) & (s > a - SUB) for a in anchors]


def _seg_sum(seg, x):
    hi = x.astype(bf16)
    return _dot(seg, hi) + _dot(seg, (x - hi.astype(f32)).astype(bf16))


def _hgrn_fwd(proj, lb_param, g_head, *, B, S, layer):
    T = B * S
    TB = min(256, S)
    nT, NC = S // TB, TB // CHUNK
    nC = S // CHUNK
    HD = HG_HEAD_DIM

    def body(q_ref, f_ref, i_ref, z_ref, lb_ref, gh_ref, cat_ref, op_ref, st_ref,
             s_scr, b_scr, k_scr):
        @pl.when(pl.program_id(2) == 0)
        def _():
            s_scr[...] = jnp.zeros_like(s_scr)
        lb = _lower_bound(lb_ref[...], layer)
        lbf = jnp.maximum(lb, LB_FLOOR)
        gh = gh_ref[...]
        r_i = lax.broadcasted_iota(jnp.int32, (CHUNK, CHUNK), 0)
        c_i = lax.broadcasted_iota(jnp.int32, (CHUNK, CHUNK), 1)
        tril = (r_i >= c_i).astype(f32)
        rows8 = lax.broadcasted_iota(jnp.int32, (8, HD), 0)
        lane_c = lax.broadcasted_iota(jnp.int32, (8, CHUNK), 1)
        anchors, masks = _anchor_masks()

        def chunk(c, st):
            rs = slice(c * CHUNK, (c + 1) * CHUNK)
            b_s, k_s = b_scr.at[c], k_scr.at[c]
            q, _, _, _, _, logf, k = _gate_parts(q_ref[rs, :], f_ref[rs, :], lb, lbf)
            v = i_ref[rs, :]
            b = _dot(tril, logf, precision=lax.Precision.HIGHEST)
            b_s[...] = b
            k_s[...] = k
            pieces = []
            for blk in range(CHUNK // SUB):
                r0 = blk * SUB
                bp = [b[r0 + 8 * i:r0 + 8 * i + 8] for i in range(SUB // 8)]
                qp = [q[r0 + 8 * i:r0 + 8 * i + 8] for i in range(SUB // 8)]
                ap = [jnp.zeros((8, CHUNK), f32) for _ in range(SUB // 8)]
                for s in range(SUB):
                    bs = b_s[r0 + s:r0 + s + 1, :]
                    ks = k_s[r0 + s:r0 + s + 1, :]
                    for i in range(s // 8, SUB // 8):
                        diff = bp[i] - bs
                        if i == s // 8:
                            diff = jnp.where(rows8 >= s - 8 * i, diff, NEG_INF)
                        col = jnp.sum(jnp.exp(diff) * qp[i] * ks, axis=1, keepdims=True)
                        ap[i] = jnp.where(lane_c == r0 + s, col, ap[i])
                pieces += ap
            a_all = jnp.concatenate(pieces, axis=0)
            for an, mk in zip(anchors, masks):
                beta = b_s[an:an + 1, :]
                qh = (q * jnp.exp(jnp.minimum(b - beta, 0.0))).astype(bf16)
                kh = (k * jnp.exp(jnp.minimum(beta - b, 0.0))).astype(bf16)
                a_all = a_all + jnp.where(mk, _dot(qh, kh, NT), 0.0)
            st_ref[0, 0, c] = st
            vb16 = v.astype(bf16)
            o = _dot(a_all.astype(bf16), vb16) + _dot((q * jnp.exp(b)).astype(bf16), st.astype(bf16), NT)
            b_end = b_s[CHUNK - 1:CHUNK, :]
            kdec = (k * jnp.exp(b_end - b)).astype(bf16)
            st_next = jnp.exp(b_end) * st + _dot(vb16, kdec, TN)
            rr = lax.rsqrt(jnp.mean(o * o, axis=-1, keepdims=True) + NORM_EPS)
            zr = z_ref[rs, :]
            cat_ref[rs, :] = (o * rr * gh * (zr * _sigmoid(zr))).astype(bf16)
            op_ref[rs, :] = o
            return st_next

        st = s_scr[...]
        for c in range(NC):
            st = chunk(c, st)
        s_scr[...] = st

    def col(part):
        return pl.BlockSpec((TB, HD), lambda b, h, n: (b * nT + n, part * HG_HEADS + h))

    out_col = pl.BlockSpec((TB, HD), lambda b, h, n: (b * nT + n, h))
    return pl.pallas_call(
        body, name=f"hgrn_fwd_l{layer}", grid=(B, HG_HEADS, nT),
        in_specs=[col(0), col(1), col(2), col(3),
                  pl.BlockSpec((DEPTH, HD), lambda b, h, n: (0, h)),
                  pl.BlockSpec((1, HD), lambda b, h, n: (0, 0))],
        out_specs=[out_col, out_col,
                   pl.BlockSpec((1, 1, NC, HD, HD), lambda b, h, n: (b, h, n, 0, 0))],
        out_shape=[jax.ShapeDtypeStruct((T, HG_WIDTH), bf16), jax.ShapeDtypeStruct((T, HG_WIDTH), f32),
                   jax.ShapeDtypeStruct((B, HG_HEADS, nC, HD, HD), f32)],
        scratch_shapes=[pltpu.VMEM((HD, HD), f32), pltpu.VMEM((NC, CHUNK, HD), f32), pltpu.VMEM((NC, CHUNK, HD), f32)],
        compiler_params=_params("parallel", "parallel", "arbitrary"),
    )(proj, proj, proj, proj, lb_param, g_head)


def _hgrn_bwd(proj, lb_param, g_head, o_pre, states, dcat, *, B, S, layer):
    T = B * S
    TB = min(256, S)
    nT, NC = S // TB, TB // CHUNK
    HD = HG_HEAD_DIM

    def body(q_ref, f_ref, i_ref, z_ref, lb_ref, gh_ref, op_ref, st_ref, dc_ref,
             dq_ref, df_ref, di_ref, dz_ref, dlb_ref, dgh_ref,
             ds_scr, b_scr, q_scr, do_scr, wk_scr):
        @pl.when(pl.program_id(2) == 0)
        def _():
            ds_scr[...] = jnp.zeros_like(ds_scr)
            dlb_ref[...] = jnp.zeros_like(dlb_ref)
            dgh_ref[...] = jnp.zeros_like(dgh_ref)
        lb = _lower_bound(lb_ref[...], layer)
        lbf = jnp.maximum(lb, LB_FLOOR)
        ind = (lb > LB_FLOOR).astype(f32)
        gh = gh_ref[...]
        r_i = lax.broadcasted_iota(jnp.int32, (CHUNK, CHUNK), 0)
        c_i = lax.broadcasted_iota(jnp.int32, (CHUNK, CHUNK), 1)
        tril = (r_i >= c_i).astype(f32)
        triu = (c_i >= r_i).astype(f32)
        rows8 = lax.broadcasted_iota(jnp.int32, (8, HD), 0)
        lane_c = lax.broadcasted_iota(jnp.int32, (8, CHUNK), 1)
        last_row = lax.broadcasted_iota(jnp.int32, (CHUNK, HD), 0) == CHUNK - 1
        anchors, masks = _anchor_masks()
        _, masks_t = _anchor_masks(transposed=True)
        seg_t = lax.broadcasted_iota(jnp.int32, (SUB, 8 * SUB), 0)
        seg_r = lax.broadcasted_iota(jnp.int32, (SUB, 8 * SUB), 1) // 8
        seg0 = (seg_r == seg_t).astype(bf16)
        seg1 = (seg_r[:, 0:4 * SUB] + 8 == seg_t[:, 0:4 * SUB]).astype(bf16)

        def chunk(c, dst1):
            rs = slice(c * CHUNK, (c + 1) * CHUNK)
            b_s, q_s, do_s = b_scr.at[c], q_scr.at[c], do_scr.at[c]
            qr, fr = q_ref[rs, :], f_ref[rs, :]
            q, sq, sg, nsg, fg, logf, k = _gate_parts(qr, fr, lb, lbf)
            v = i_ref[rs, :]
            b = _dot(tril, logf, precision=lax.Precision.HIGHEST)
            o = op_ref[rs, :]
            dc = dc_ref[rs, :]
            zr = z_ref[rs, :]
            sz = _sigmoid(zr)
            rr = lax.rsqrt(jnp.mean(o * o, axis=-1, keepdims=True) + NORM_EPS)
            dz_ref[rs, :] = (dc * (o * rr * gh) * (sz * (1.0 + zr * (1.0 - sz)))).astype(bf16)
            dn = dc * (zr * sz)
            dgh_ref[0, 0] += jnp.sum(dn * o * rr, axis=0, keepdims=True)
            gdn = dn * gh
            d_o = rr * gdn - o * (rr * rr * rr) * jnp.mean(gdn * o, axis=-1, keepdims=True)
            b_s[...] = b
            q_s[...] = q
            do_s[...] = d_o
            dob = d_o.astype(bf16)
            vb16 = v.astype(bf16)
            d_a = _dot(dob, vb16, NT)
            d_q = jnp.zeros((CHUNK, HD), f32)
            d_k = jnp.zeros((CHUNK, HD), f32)
            at_all = jnp.zeros((CHUNK, CHUNK), f32)
            for an, mk, mkt in zip(anchors, masks, masks_t):
                beta = b_s[an:an + 1, :]
                eq = jnp.exp(jnp.minimum(b - beta, 0.0))
                ek = jnp.exp(jnp.minimum(beta - b, 0.0))
                qh = (q * eq).astype(bf16)
                kh = (k * ek).astype(bf16)
                at_all = at_all + jnp.where(mkt, _dot(kh, qh, NT), 0.0)
                d_aa = jnp.where(mk, d_a, 0.0).astype(bf16)
                d_q = d_q + _dot(d_aa, kh) * eq
                d_k = d_k + _dot(d_aa, qh, TN) * ek
            st0 = st_ref[0, 0, c]
            dst1b = dst1.astype(bf16)
            eb = jnp.exp(b)
            b_end = b_s[CHUNK - 1:CHUNK, :]
            edec = jnp.exp(b_end - b)
            e_end = jnp.exp(b_end)
            kdec = (k * edec).astype(bf16)
            qdec = (q * eb).astype(bf16)
            d_q = d_q + _dot(dob, st0.astype(bf16)) * eb
            d_v = _dot(kdec, dst1b, NT)
            d_k = d_k + _dot(vb16, dst1b) * edec
            st1 = e_end * st0 + _dot(vb16, kdec, TN)
            rterm = jnp.sum(dst1 * st1, axis=0, keepdims=True)
            dst0 = e_end * dst1 + _dot(dob, qdec, TN)
            dq_blocks, dk_pieces, at_pieces = [], [], []
            for blk in range(CHUNK // SUB):
                r0 = blk * SUB
                wk = wk_scr.at[c * (CHUNK // SUB) + blk]
                bp = [b[r0 + 8 * i:r0 + 8 * i + 8] for i in range(SUB // 8)]
                kp = [k[r0 + 8 * i:r0 + 8 * i + 8] for i in range(SUB // 8)]
                vp = [v[r0 + 8 * i:r0 + 8 * i + 8] for i in range(SUB // 8)]
                dkp = [jnp.zeros((8, HD), f32) for _ in range(SUB // 8)]
                atp = [jnp.zeros((8, CHUNK), f32) for _ in range(SUB // 8)]
                for t in range(SUB):
                    bt = b_s[r0 + t:r0 + t + 1, :]
                    qt = q_s[r0 + t:r0 + t + 1, :]
                    dot_ = do_s[r0 + t:r0 + t + 1, :]
                    for i in range(t // 8 + 1):
                        diff = bt - bp[i]
                        if i == t // 8:
                            diff = jnp.where(rows8 <= t - 8 * i, diff, NEG_INF)
                        e = jnp.exp(diff)
                        a = jnp.sum(e * kp[i] * qt, axis=1, keepdims=True)
                        atp[i] = jnp.where(lane_c == r0 + t, a, atp[i])
                        w = jnp.sum(vp[i] * dot_, axis=1, keepdims=True) * e
                        dkp[i] = dkp[i] + w * qt
                        row = 8 * t if i == 0 else 8 * SUB + 8 * (t - 8)
                        wk[row:row + 8, :] = w * kp[i]
                dq_blocks.append(_seg_sum(seg0, wk[0:8 * SUB, :]) + _seg_sum(seg1, wk[8 * SUB:12 * SUB, :]))
                dk_pieces += dkp
                at_pieces += atp
            d_q = d_q + jnp.concatenate(dq_blocks, axis=0)
            d_k = d_k + jnp.concatenate(dk_pieces, axis=0)
            d_v = d_v + _dot((at_all + jnp.concatenate(at_pieces, axis=0)).astype(bf16), dob)
            db = q * d_q - k * d_k + jnp.where(last_row, rterm, 0.0)
            dlt = _dot(triu, db, precision=lax.Precision.HIGHEST) - fg * d_k
            df_ref[rs, :] = (dlt * (1.0 - lb) * sg * nsg / fg).astype(bf16)
            dlb_ref[0] += jnp.sum(dlt * (ind - sg) / fg, axis=0, keepdims=True)
            dq_ref[rs, :] = (d_q * (sq * (1.0 + qr * (1.0 - sq)))).astype(bf16)
            di_ref[rs, :] = d_v.astype(bf16)
            return dst0

        dst = ds_scr[...]
        for c in reversed(range(NC)):
            dst = chunk(c, dst)
        ds_scr[...] = dst

    def col(part):
        return pl.BlockSpec((TB, HD), lambda b, h, n: (b * nT + nT - 1 - n, part * HG_HEADS + h))

    hcol = pl.BlockSpec((TB, HD), lambda b, h, n: (b * nT + nT - 1 - n, h))
    return pl.pallas_call(
        body, name=f"hgrn_bwd_l{layer}", grid=(B, HG_HEADS, nT),
        in_specs=[col(0), col(1), col(2), col(3),
                  pl.BlockSpec((DEPTH, HD), lambda b, h, n: (0, h)),
                  pl.BlockSpec((1, HD), lambda b, h, n: (0, 0)),
                  hcol,
                  pl.BlockSpec((1, 1, NC, HD, HD), lambda b, h, n: (b, h, nT - 1 - n, 0, 0)),
                  hcol],
        out_specs=[hcol, hcol, hcol, hcol,
                   pl.BlockSpec((1, 1, HD), lambda b, h, n: (b, 0, h)),
                   pl.BlockSpec((1, 1, 1, HD), lambda b, h, n: (b, h, 0, 0))],
        out_shape=[jax.ShapeDtypeStruct((T, HG_WIDTH), bf16)] * 4 + [
            jax.ShapeDtypeStruct((B, 1, HG_WIDTH), f32), jax.ShapeDtypeStruct((B, HG_HEADS, 1, HD), f32)],
        scratch_shapes=[pltpu.VMEM((HD, HD), f32)] + [pltpu.VMEM((NC, CHUNK, HD), f32)] * 3
        + [pltpu.VMEM((NC * CHUNK // SUB, 12 * SUB, HD), f32)],
        compiler_params=_params("parallel", "parallel", "arbitrary"),
    )(proj, proj, proj, proj, lb_param, g_head, o_pre, states, dcat)


def _rope_tables(S):
    half = ATT_HEAD_DIM // 2
    inv_freq = ROPE_THETA ** (-jnp.arange(half, dtype=f32) / half)
    ang = jnp.arange(S, dtype=f32)[:, None] * inv_freq[None, :]
    cos, sin = jnp.cos(ang), jnp.sin(ang)
    return jnp.tile(jnp.concatenate([cos, cos], axis=1), (1, 2)), jnp.tile(jnp.concatenate([-sin, sin], axis=1), (1, 2))


def _swap_halves(x, first_half):
    return jnp.where(first_half, pltpu.roll(x, LANES - ATT_HEAD_DIM // 2, 1), pltpu.roll(x, ATT_HEAD_DIM // 2, 1))


def _rope(x, cos, sin, first_half):
    return x * cos + _swap_halves(x, first_half) * sin


def _rope_bwd(dy, cos, sin, first_half):
    return dy * cos + _swap_halves(dy * sin, first_half)


def _attn_consts(n):
    lane = lax.broadcasted_iota(jnp.int32, (1, LANES), 1)
    low = lane < ATT_HEAD_DIM
    first_half = (lane % ATT_HEAD_DIM) < ATT_HEAD_DIM // 2
    top = lax.broadcasted_iota(jnp.int32, (LANES, 1), 0) < ATT_HEAD_DIM
    s = lax.broadcasted_iota(jnp.int32, (2 * ATT_BLOCK, ATT_BLOCK), 0)
    t = lax.broadcasted_iota(jnp.int32, (2 * ATT_BLOCK, ATT_BLOCK), 1)
    mask = (s > t) & (s <= t + ATT_BLOCK) & ((s >= ATT_BLOCK) | (n > 0))
    return low, first_half, top, mask


def _dup_kv(x, low):
    rolled = pltpu.roll(x, ATT_HEAD_DIM, 1)
    return [jnp.where(low, x, rolled), jnp.where(low, rolled, x)]


def _attn_head(qtm, kd, vdt, sink, mask):
    s = jnp.where(mask, _dot(kd, qtm) * ATT_SCALE, NEG_INF)
    m = jnp.maximum(jnp.max(s, axis=0, keepdims=True), sink)
    p = jnp.exp(s - m)
    psink = jnp.exp(sink - m)
    inv = 1.0 / (jnp.sum(p, axis=0, keepdims=True) + psink)
    pn = p * inv
    return pn, psink * inv, _dot(vdt, pn.astype(bf16))


def _swa_fwd(proj, sink_b, cos, sin, *, B, S):
    T = B * S
    L = ATT_BLOCK
    nB = S // L

    def body(q_ref, z_ref, kvc_ref, kvp_ref, sk_ref, cc_ref, sc_ref, cp_ref, sp_ref, cat_ref):
        n = pl.program_id(1)
        low, first_half, top, mask = _attn_consts(n)
        cc, sc = cc_ref[...], sc_ref[...]
        kc = _rope(kvc_ref[:, 0:LANES], cc, sc, first_half)
        kp = _rope(kvp_ref[:, 0:LANES], cp_ref[...], sp_ref[...], first_half)
        kd = [x.astype(bf16) for x in _dup_kv(jnp.concatenate([kp, kc], axis=0), low)]
        vdt = [x.T.astype(bf16) for x in _dup_kv(jnp.concatenate([kvp_ref[:, LANES:2 * LANES], kvc_ref[:, LANES:2 * LANES]], axis=0), low)]
        for pair in range(ATT_HEADS // 2):
            cols = slice(pair * LANES, (pair + 1) * LANES)
            j = (2 * pair) // ATT_GROUP
            qt = _rope(q_ref[:, cols], cc, sc, first_half).T
            outs = []
            for hh in range(2):
                h = 2 * pair + hh
                qtm = jnp.where(top if hh == 0 else ~top, qt, 0.0).astype(bf16)
                _, _, o = _attn_head(qtm, kd[j], vdt[j], sk_ref[h:h + 1, 0:1], mask)
                outs.append(o)
            zp = z_ref[:, cols]
            cat_ref[:, cols] = (jnp.where(top, outs[0], outs[1]).T * (zp * _sigmoid(zp))).astype(bf16)

    cur = lambda b, n: (b * nB + n, 0)
    prev = lambda b, n: (b * nB + jnp.maximum(n - 1, 0), 0)
    return pl.pallas_call(
        body, name="swa_fwd", grid=(B, nB),
        in_specs=[pl.BlockSpec((L, ATT_WIDTH), lambda b, n: (b * nB + n, QA_BLK)),
                  pl.BlockSpec((L, ATT_WIDTH), lambda b, n: (b * nB + n, ZA_BLK)),
                  pl.BlockSpec((L, 2 * KV_WIDTH), lambda b, n: (b * nB + n, KV_BLK)),
                  pl.BlockSpec((L, 2 * KV_WIDTH), lambda b, n: (b * nB + jnp.maximum(n - 1, 0), KV_BLK)),
                  pl.BlockSpec((ATT_HEADS, LANES), lambda b, n: (0, 0)),
                  pl.BlockSpec((L, LANES), lambda b, n: (n, 0)), pl.BlockSpec((L, LANES), lambda b, n: (n, 0)),
                  pl.BlockSpec((L, LANES), lambda b, n: (jnp.maximum(n - 1, 0), 0)),
                  pl.BlockSpec((L, LANES), lambda b, n: (jnp.maximum(n - 1, 0), 0))],
        out_specs=pl.BlockSpec((L, ATT_WIDTH), cur),
        out_shape=jax.ShapeDtypeStruct((T, ATT_WIDTH), bf16),
        compiler_params=_params("parallel", "parallel"),
    )(proj, proj, proj, proj, sink_b, cos, sin, cos, sin)


def _swa_bwd(proj, sink_b, cos, sin, dcat, *, B, S):
    T = B * S
    L = ATT_BLOCK
    nB = S // L

    def body(q_ref, z_ref, kvc_ref, kvp_ref, sk_ref, cc_ref, sc_ref, cp_ref, sp_ref, dc_ref,
             dq_ref, dz_ref, dkv_ref, dsk_ref, carry, ds_st, pn_st, q_st, do_st):
        step = pl.program_id(1)
        n = nB - 1 - step

        @pl.when((pl.program_id(0) == 0) & (step == 0))
        def _():
            dsk_ref[...] = jnp.zeros_like(dsk_ref)

        @pl.when(step == 0)
        def _():
            carry[...] = jnp.zeros_like(carry)
        low, first_half, top, mask = _attn_consts(n)
        cc, sc, cp, sp = cc_ref[...], sc_ref[...], cp_ref[...], sp_ref[...]
        kc = _rope(kvc_ref[:, 0:LANES], cc, sc, first_half)
        kp = _rope(kvp_ref[:, 0:LANES], cp, sp, first_half)
        kdf = _dup_kv(jnp.concatenate([kp, kc], axis=0), low)
        vdf = _dup_kv(jnp.concatenate([kvp_ref[:, LANES:2 * LANES], kvc_ref[:, LANES:2 * LANES]], axis=0), low)
        kd = [x.astype(bf16) for x in kdf]
        vd = [x.astype(bf16) for x in vdf]
        kdt = [x.T.astype(bf16) for x in kdf]
        vdt = [x.T.astype(bf16) for x in vdf]
        dkd, dvd = [], []
        for pair in range(ATT_HEADS // 2):
            cols = slice(pair * LANES, (pair + 1) * LANES)
            j = (2 * pair) // ATT_GROUP
            qp = _rope(q_ref[:, cols], cc, sc, first_half)
            qt = qp.T
            zp = z_ref[:, cols]
            dc = dc_ref[:, cols]
            sz = _sigmoid(zp)
            d_o = dc * (zp * sz)
            dot_ = d_o.T
            res = []
            for hh in range(2):
                rsel = top if hh == 0 else ~top
                qtm = jnp.where(rsel, qt, 0.0).astype(bf16)
                pn, psn, o = _attn_head(qtm, kd[j], vdt[j], sk_ref[2 * pair + hh:2 * pair + hh + 1, 0:1], mask)
                res.append((rsel, pn, psn, o))
            ot = jnp.where(top, res[0][3], res[1][3])
            dz_ref[:, cols] = (dc * ot.T * (sz * (1.0 + zp * (1.0 - sz)))).astype(bf16)
            dqts = []
            for hh in range(2):
                h = 2 * pair + hh
                rsel, pn, psn, _ = res[hh]
                lsel = low if hh == 0 else ~low
                dotm = jnp.where(rsel, dot_, 0.0)
                delta = jnp.sum(dotm * ot, axis=0, keepdims=True)
                dst = (pn * (_dot(vd[j], dotm.astype(bf16)) - delta) * ATT_SCALE).astype(bf16)
                dsk_ref[h:h + 1, :] += jnp.zeros((1, LANES), f32) - jnp.sum(psn * delta)
                dqts.append(_dot(kdt[j], dst))
                g = h % ATT_GROUP
                ds_st[:, g * LANES:(g + 1) * LANES] = dst
                pn_st[:, g * LANES:(g + 1) * LANES] = pn.astype(bf16)
                q_st[g * LANES:(g + 1) * LANES, :] = jnp.where(lsel, qp, 0.0).astype(bf16)
                do_st[g * LANES:(g + 1) * LANES, :] = jnp.where(lsel, d_o, 0.0).astype(bf16)
            dq_ref[:, cols] = _rope_bwd(jnp.where(top, dqts[0], dqts[1]).T, cc, sc, first_half).astype(bf16)
            if (2 * pair + 2) % ATT_GROUP == 0:
                dkd.append(_dot(ds_st[...], q_st[...]))
                dvd.append(_dot(pn_st[...], do_st[...]))
        dk = [x + pltpu.roll(x, ATT_HEAD_DIM, 1) for x in dkd]
        dv = [x + pltpu.roll(x, ATT_HEAD_DIM, 1) for x in dvd]
        dk = jnp.where(low, dk[0], dk[1])
        dv = jnp.where(low, dv[0], dv[1])
        dkv_ref[:, 0:LANES] = (_rope_bwd(dk[L:2 * L], cc, sc, first_half) + carry[:, 0:LANES]).astype(bf16)
        dkv_ref[:, LANES:2 * LANES] = (dv[L:2 * L] + carry[:, LANES:2 * LANES]).astype(bf16)
        carry[:, 0:LANES] = _rope_bwd(dk[0:L], cp, sp, first_half)
        carry[:, LANES:2 * LANES] = dv[0:L]

    rev = lambda b, s: b * nB + nB - 1 - s
    revp = lambda b, s: b * nB + jnp.maximum(nB - 2 - s, 0)
    wide = lambda blk: pl.BlockSpec((L, ATT_WIDTH), lambda b, s: (rev(b, s), blk))
    tab = pl.BlockSpec((L, LANES), lambda b, s: (nB - 1 - s, 0))
    tabp = pl.BlockSpec((L, LANES), lambda b, s: (jnp.maximum(nB - 2 - s, 0), 0))
    return pl.pallas_call(
        body, name="swa_bwd", grid=(B, nB),
        in_specs=[wide(QA_BLK), wide(ZA_BLK),
                  pl.BlockSpec((L, 2 * KV_WIDTH), lambda b, s: (rev(b, s), KV_BLK)),
                  pl.BlockSpec((L, 2 * KV_WIDTH), lambda b, s: (revp(b, s), KV_BLK)),
                  pl.BlockSpec((ATT_HEADS, LANES), lambda b, s: (0, 0)),
                  tab, tab, tabp, tabp, wide(0)],
        out_specs=[wide(0), wide(0), pl.BlockSpec((L, 2 * KV_WIDTH), lambda b, s: (rev(b, s), 0)),
                   pl.BlockSpec((ATT_HEADS, LANES), lambda b, s: (0, 0))],
        out_shape=[jax.ShapeDtypeStruct((T, ATT_WIDTH), bf16), jax.ShapeDtypeStruct((T, ATT_WIDTH), bf16),
                   jax.ShapeDtypeStruct((T, 2 * KV_WIDTH), bf16), jax.ShapeDtypeStruct((ATT_HEADS, LANES), f32)],
        scratch_shapes=[pltpu.VMEM((L, 2 * KV_WIDTH), f32),
                        pltpu.VMEM((2 * L, ATT_GROUP * LANES), bf16), pltpu.VMEM((2 * L, ATT_GROUP * LANES), bf16),
                        pltpu.VMEM((ATT_GROUP * LANES, LANES), bf16), pltpu.VMEM((ATT_GROUP * LANES, LANES), bf16)],
        compiler_params=_params("arbitrary", "arbitrary"),
    )(proj, proj, proj, proj, sink_b, cos, sin, cos, sin, dcat)


def _local_grads(x, target, wt, w_out_b, g_pre, g_post, lb_param, g_head, sinks, *, B, S):
    cos, sin = _rope_tables(S)
    wt_int = jnp.concatenate([wt[:, 0:5120], wt[:, 5376:6400], wt[:, 5120:5376]], axis=1)
    saved = []
    for l in range(DEPTH):
        proj, h = _in_proj(x, g_pre[l:l + 1], wt_int[l])
        ch, o_pre, states = _hgrn_fwd(proj, lb_param, g_head[l:l + 1], B=B, S=S, layer=l)
        sink_b = jnp.broadcast_to(sinks[l][:, None], (ATT_HEADS, LANES))
        ca = _swa_fwd(proj, sink_b, cos, sin, B=B, S=S)
        xn, y = _out_proj(ch, ca, w_out_b[l], x, g_post[l:l + 1])
        saved.append((x, proj, h, ch, o_pre, states, sink_b, ca, y))
        x = xn
    dx, loss = _loss_head(x, target)
    gw_in, gw_out, gg_pre, gg_post, g_lb, gg_head, g_sinks = [], [], [], [], [], [], []
    for l in reversed(range(DEPTH)):
        x_in, proj, h, ch, o_pre, states, sink_b, ca, y = saved[l]
        dch, dca, dwo, dgpost = _out_proj_bwd(dx, y, g_post[l:l + 1], w_out_b[l], ch, ca)
        dq, df, di, dz, dlb, dgh = _hgrn_bwd(proj, lb_param, g_head[l:l + 1], o_pre, states, dch, B=B, S=S, layer=l)
        dqa, dza, dkv, dsk = _swa_bwd(proj, sink_b, cos, sin, dca, B=B, S=S)
        dproj = jnp.concatenate([dq, df, di, dz, dqa, dkv, dza], axis=1)
        dx, dgpre = _in_proj_bwd(dproj, wt[l], x_in, g_pre[l:l + 1], dx)
        gw_in.append(_grad_w_in(h, dproj))
        gw_out.append(dwo)
        gg_pre.append(dgpre[0])
        gg_post.append(dgpost[0])
        g_lb.append(jnp.sum(dlb, axis=(0, 1)))
        gg_head.append(jnp.sum(dgh, axis=(0, 1, 2)))
        g_sinks.append(dsk[:, 0])
    rev = lambda xs: jnp.stack(xs[::-1])
    return (loss[0, 0], dx, rev(gw_in), rev(gw_out), rev(gg_pre), rev(gg_post), rev(g_lb), rev(gg_head), rev(g_sinks))


MESH = pl.DeviceIdType.MESH
ANY = pl.BlockSpec(memory_space=pl.ANY)


def _place():
    x, y, c = lax.axis_index("x"), lax.axis_index("y"), lax.axis_index("c")
    return x, y, c, [(1 - x, y), (x, 1 - y), (1 - x, 1 - y)]


def _rcopy(src, dst, send, recv, k, to):
    return pltpu.make_async_remote_copy(src_ref=src, dst_ref=dst, send_sem=send.at[k], recv_sem=recv.at[k],
                                        device_id=to, device_id_type=MESH)


def _gather_shards(shards):
    n = len(shards)

    def body(*refs):
        ins, outs = refs[:n], refs[n:2 * n]
        send, recv, lsem = refs[2 * n:]
        x, y, c, chips = _place()
        me = 2 * x + y
        sib = (x, y, 1 - c)
        local = [pltpu.make_async_copy(ins[a], outs[a].at[:, me], lsem.at[a]) for a in range(n)]
        for cp in local:
            cp.start()
        first = []
        for j, chip in enumerate(chips):
            for a in range(n):
                first.append(_rcopy(ins[a].at[:, c], outs[a].at[:, me, c], send, recv, j * n + a, (*chip, c)))
        for cp in first:
            cp.start()
        passed = []
        for j, (px, py) in enumerate(chips):
            for a in range(n):
                blk = outs[a].at[:, 2 * px + py, c]
                _rcopy(blk, blk, send, recv, j * n + a, sib).wait_recv()
                cp = _rcopy(blk, blk, send, recv, (3 + j) * n + a, sib)
                cp.start()
                passed.append(cp)
        for j, (px, py) in enumerate(chips):
            for a in range(n):
                blk = outs[a].at[:, 2 * px + py, 1 - c]
                _rcopy(blk, blk, send, recv, (3 + j) * n + a, sib).wait_recv()
        for cp in first + passed:
            cp.wait_send()
        for cp in local:
            cp.wait()

    return pl.pallas_call(
        body, name="gather_shards",
        in_specs=[ANY] * n, out_specs=[ANY] * n,
        out_shape=[jax.ShapeDtypeStruct((s.shape[0], 4) + s.shape[1:], s.dtype) for s in shards],
        scratch_shapes=[pltpu.SemaphoreType.DMA((6 * n,)), pltpu.SemaphoreType.DMA((6 * n,)),
                        pltpu.SemaphoreType.DMA((n,))],
    )(*shards)


def _pair_exchange(parts):
    n = len(parts)

    def body(*refs):
        ins, outs = refs[:n], refs[n:2 * n]
        send, recv = refs[2 * n:]
        x, y, c, _ = _place()
        cps = [_rcopy(ins[a].at[:, :, 1 - c], outs[a], send, recv, a, (x, y, 1 - c)) for a in range(n)]
        for cp in cps:
            cp.start()
        for cp in cps:
            cp.wait()

    return pl.pallas_call(
        body, name="pair_exchange", in_specs=[ANY] * n, out_specs=[ANY] * n,
        out_shape=[jax.ShapeDtypeStruct(p.shape[:2] + p.shape[3:], p.dtype) for p in parts],
        scratch_shapes=[pltpu.SemaphoreType.DMA((n,)), pltpu.SemaphoreType.DMA((n,))],
    )(*parts)


def _block_rows(r):
    return r if r <= 512 else r // 2


def _pair_add(part, got):
    L, K, _, r, C = part.shape
    rows = _block_rows(r)

    def body(c_ref, a_ref, b_ref, o_ref):
        o_ref[0, 0] = (a_ref[0, 0, 0] + b_ref[0, 0]).astype(bf16)

    blk = (1, 1, rows, C)
    return pl.pallas_call(
        body, name="pair_add",
        grid_spec=pltpu.PrefetchScalarGridSpec(
            num_scalar_prefetch=1, grid=(L, K, r // rows),
            in_specs=[pl.BlockSpec((1, 1, 1, rows, C), lambda l, k, i, c: (l, k, c[0], i, 0)),
                      pl.BlockSpec(blk, lambda l, k, i, c: (l, k, i, 0))],
            out_specs=pl.BlockSpec(blk, lambda l, k, i, c: (l, k, i, 0))),
        out_shape=jax.ShapeDtypeStruct((L, K, r, C), bf16),
        compiler_params=_params("parallel", "parallel", "parallel"),
    )(jnp.reshape(lax.axis_index("c"), (1,)).astype(jnp.int32), part, got)


def _chip_exchange(sums):
    n = len(sums)

    def body(*refs):
        ins, outs = refs[:n], refs[n:2 * n]
        send, recv = refs[2 * n:]
        x, y, c, chips = _place()
        cps = []
        for j, (px, py) in enumerate(chips):
            for a in range(n):
                cps.append(_rcopy(ins[a].at[:, 2 * px + py], outs[a].at[j], send, recv, j * n + a, (px, py, c)))
        for cp in cps:
            cp.start()
        for cp in cps:
            cp.wait()

    return pl.pallas_call(
        body, name="chip_exchange", in_specs=[ANY] * n, out_specs=[ANY] * n,
        out_shape=[jax.ShapeDtypeStruct((3, s.shape[0]) + s.shape[2:], s.dtype) for s in sums],
        scratch_shapes=[pltpu.SemaphoreType.DMA((3 * n,)), pltpu.SemaphoreType.DMA((3 * n,))],
    )(*sums)


def _chip_sum(mine, got):
    L, K, r, C = mine.shape
    rows = _block_rows(r)

    def body(k_ref, a_ref, b_ref, o_ref):
        o_ref[0] = (a_ref[0, 0].astype(f32) + b_ref[0, 0].astype(f32)) + (b_ref[1, 0].astype(f32) + b_ref[2, 0].astype(f32))

    chip = 2 * lax.axis_index("x") + lax.axis_index("y")
    return pl.pallas_call(
        body, name="chip_sum",
        grid_spec=pltpu.PrefetchScalarGridSpec(
            num_scalar_prefetch=1, grid=(L, r // rows),
            in_specs=[pl.BlockSpec((1, 1, rows, C), lambda l, i, k: (l, k[0], i, 0)),
                      pl.BlockSpec((3, 1, rows, C), lambda l, i, k: (0, l, i, 0))],
            out_specs=pl.BlockSpec((1, rows, C), lambda l, i, k: (l, i, 0))),
        out_shape=jax.ShapeDtypeStruct((L, r, C), f32),
        compiler_params=_params("parallel", "parallel"),
    )(jnp.reshape(chip, (1,)).astype(jnp.int32), mine, got)


def _pair_share(halves):
    n = len(halves)

    def body(*refs):
        ins, outs = refs[:n], refs[n:2 * n]
        send, recv, lsem = refs[2 * n:]
        x, y, c, _ = _place()
        local = [pltpu.make_async_copy(ins[a], outs[a].at[:, c], lsem.at[a]) for a in range(n)]
        cps = [_rcopy(ins[a], outs[a].at[:, c], send, recv, a, (x, y, 1 - c)) for a in range(n)]
        for cp in local + cps:
            cp.start()
        for a in range(n):
            theirs = outs[a].at[:, 1 - c]
            _rcopy(theirs, theirs, send, recv, a, (x, y, 1 - c)).wait_recv()
        for cp in cps:
            cp.wait_send()
        for cp in local:
            cp.wait()

    return pl.pallas_call(
        body, name="pair_share", in_specs=[ANY] * n, out_specs=[ANY] * n,
        out_shape=[jax.ShapeDtypeStruct((h.shape[0], 2) + h.shape[1:], h.dtype) for h in halves],
        scratch_shapes=[pltpu.SemaphoreType.DMA((n,)), pltpu.SemaphoreType.DMA((n,)), pltpu.SemaphoreType.DMA((n,))],
    )(*halves)


def _all_sum_small(v):
    def body(v_ref, o_ref, buf, send, recv):
        x, y, c, _ = _place()
        me = 4 * x + 2 * y + c
        buf[me] = v_ref[...]
        cps = []
        for m in range(1, 8):
            to = (x ^ (m >> 2), y ^ ((m >> 1) & 1), c ^ (m & 1))
            cps.append(_rcopy(v_ref, buf.at[me], send, recv, m - 1, to))
        for cp in cps:
            cp.start()
        for cp in cps:
            cp.wait()
        acc = buf[0]
        for d in range(1, 8):
            acc = acc + buf[d]
        o_ref[...] = acc

    vm = pl.BlockSpec(memory_space=pltpu.VMEM)
    return pl.pallas_call(
        body, name="all_sum_small", in_specs=[vm], out_specs=vm,
        out_shape=jax.ShapeDtypeStruct(v.shape, v.dtype),
        scratch_shapes=[pltpu.VMEM((8,) + v.shape, v.dtype), pltpu.SemaphoreType.DMA((7,)), pltpu.SemaphoreType.DMA((7,))],
    )(v)


def _adamw_math(w, g, m, v):
    m = ADAM_B1 * m + (1.0 - ADAM_B1) * g
    v = ADAM_B2 * v + (1.0 - ADAM_B2) * (g * g)
    m_hat = m / (1.0 - ADAM_B1 ** ADAM_STEP)
    v_hat = v / (1.0 - ADAM_B2 ** ADAM_STEP)
    return -ADAM_LR * (m_hat / (jnp.sqrt(v_hat) + ADAM_EPS) + ADAM_WD * w), m, v


def _adamw(w, g, m, v, *, rows=256):
    L, R, C = w.shape
    rows = min(rows, R)

    def body(w_ref, g_ref, m_ref, v_ref, d_ref, mo_ref, vo_ref):
        d_ref[...], mo_ref[...], vo_ref[...] = _adamw_math(w_ref[...], g_ref[...], m_ref[...], v_ref[...])

    blk = pl.BlockSpec((1, rows, C), lambda l, i: (l, i, 0))
    return pl.pallas_call(
        body, name="adamw", grid=(L, R // rows), in_specs=[blk] * 4, out_specs=[blk] * 3,
        out_shape=[jax.ShapeDtypeStruct(w.shape, f32)] * 3,
        compiler_params=_params("parallel", "parallel"),
    )(w, g, m, v)


def _shard_transposed(w, *, tm=256):
    L, R, C = w.shape

    def body(w_ref, o_ref, eye):
        @pl.when((pl.program_id(0) == 0) & (pl.program_id(1) == 0))
        def _():
            r = lax.broadcasted_iota(jnp.int32, (C, C), 0)
            c = lax.broadcasted_iota(jnp.int32, (C, C), 1)
            eye[...] = (r == c).astype(bf16)
        o_ref[0] = _dot(eye[...], w_ref[0].astype(bf16), NT).astype(bf16)

    return pl.pallas_call(
        body, name="shard_transposed", grid=(L, R // tm),
        in_specs=[pl.BlockSpec((1, tm, C), lambda l, i: (l, i, 0))],
        out_specs=pl.BlockSpec((1, C, tm), lambda l, i: (l, 0, i)),
        out_shape=jax.ShapeDtypeStruct((L, C, R), bf16),
        scratch_shapes=[pltpu.VMEM((C, C), bf16)],
        compiler_params=_params("arbitrary", "arbitrary"),
    )(w)


def _adamw_t(w, gt, m, v, *, tm=256):
    L, R, C = w.shape

    def body(w_ref, gt_ref, m_ref, v_ref, g_ref, d_ref, mo_ref, vo_ref):
        eye = (lax.broadcasted_iota(jnp.int32, (tm, tm), 0) == lax.broadcasted_iota(jnp.int32, (tm, tm), 1)).astype(bf16)
        x = gt_ref[0]
        hi = x.astype(bf16)
        rest = x - hi.astype(f32)
        mid = rest.astype(bf16)
        lo = (rest - mid.astype(f32)).astype(bf16)
        g = (_dot(eye, hi, NT) + _dot(eye, mid, NT)) + _dot(eye, lo, NT)
        g_ref[0] = g
        d_ref[0], mo_ref[0], vo_ref[0] = _adamw_math(w_ref[0], g, m_ref[0], v_ref[0])

    blk = pl.BlockSpec((1, tm, C), lambda l, i: (l, i, 0))
    return pl.pallas_call(
        body, name="adamw_t", grid=(L, R // tm),
        in_specs=[blk, pl.BlockSpec((1, C, tm), lambda l, i: (l, 0, i)), blk, blk], out_specs=[blk] * 4,
        out_shape=[jax.ShapeDtypeStruct(w.shape, f32)] * 4,
        compiler_params=_params("parallel", "parallel"),
    )(w, gt, m, v)


SMALL_ROWS = 4 * DEPTH


def _pack_small(g_pre, g_post, lb, g_head, sinks, loss=None):
    rows = []
    for l in range(DEPTH):
        tail = [g_head[l], sinks[l]]
        if loss is not None and l == 0:
            tail.append(jnp.reshape(loss, (1,)))
        tail = jnp.concatenate(tail)
        rows += [g_pre[l], g_post[l], lb[l], jnp.pad(tail, (0, D_MODEL - tail.shape[0]))]
    return jnp.stack(rows)


def _unpack_small(p):
    g_pre = jnp.stack([p[4 * l] for l in range(DEPTH)])
    g_post = jnp.stack([p[4 * l + 1] for l in range(DEPTH)])
    lb = jnp.stack([p[4 * l + 2] for l in range(DEPTH)])
    g_head = jnp.stack([p[4 * l + 3, :HG_HEAD_DIM] for l in range(DEPTH)])
    sinks = jnp.stack([p[4 * l + 3, HG_HEAD_DIM:HG_HEAD_DIM + ATT_HEADS] for l in range(DEPTH)])
    return g_pre, g_post, lb, g_head, sinks


def _small_update(gsum, w, m, v):
    def body(g_ref, w_ref, m_ref, v_ref, go_ref, d_ref, mo_ref, vo_ref):
        g = g_ref[...]
        w = w_ref[...]
        lbp = [w[4 * l + 2:4 * l + 3] for l in range(DEPTH)]
        mx = functools.reduce(jnp.maximum, lbp)
        e = [jnp.exp(t - mx) for t in lbp]
        tot = functools.reduce(jnp.add, e)
        p = [t / tot for t in e]
        glb = [g[4 * l + 2:4 * l + 3] for l in range(DEPTH)]
        row = lax.broadcasted_iota(jnp.int32, g.shape, 0)
        for j in range(DEPTH):
            gj = jnp.zeros_like(p[0])
            for l in range(DEPTH):
                for i in range(1, l + 1):
                    gj = gj + glb[l] * p[i] * ((1.0 if i == j else 0.0) - p[j])
            g = jnp.where(row == 4 * j + 2, gj, g)
        go_ref[...] = g
        d_ref[...], mo_ref[...], vo_ref[...] = _adamw_math(w, g, m_ref[...], v_ref[...])

    vm = pl.BlockSpec(memory_space=pltpu.VMEM)
    return pl.pallas_call(
        body, name="small_update", in_specs=[vm] * 4, out_specs=[vm] * 4,
        out_shape=[jax.ShapeDtypeStruct(gsum.shape, f32)] * 4,
    )(gsum, w, m, v)


def kernel(x, w_in, w_out, g_pre, g_post, lb_param, g_head, sinks, loss_target, m_w_in, m_w_out, m_g_pre, m_g_post, m_lb_param, m_g_head, m_sinks, v_w_in, v_w_out, v_g_pre, v_g_post, v_lb_param, v_g_head, v_sinks):
    B, S, _ = x.shape
    T = B * S
    L = DEPTH
    ri, ro = IN_WIDTH // 8, MIX_WIDTH // 8
    wt_all, wo_all = _gather_shards([_shard_transposed(w_in).reshape(L, 2, ri, D_MODEL),
                                     w_out.astype(bf16).reshape(L, 2, ro, D_MODEL)])

    loss, dx, gwt, gwo, ggpre, ggpost, glb, gghead, gsinks = _local_grads(
        x.reshape(T, D_MODEL), loss_target.reshape(T, D_MODEL), wt_all.reshape(L, IN_WIDTH, D_MODEL),
        wo_all.reshape(L, MIX_WIDTH, D_MODEL), g_pre, g_post, lb_param, g_head, sinks, B=B, S=S)

    parts = [gwt.reshape(L, 4, 2, ri, D_MODEL), gwo.reshape(L, 4, 2, ro, D_MODEL)]
    got = _pair_exchange(parts)
    sums = [_pair_add(p, r) for p, r in zip(parts, got)]
    recv = _chip_exchange(sums)
    halves = [_chip_sum(s, r) for s, r in zip(sums, recv)]
    gwt_mine, gwo_mine = _pair_share(halves)

    grad_w_in, d_w_in, nm_w_in, nv_w_in = _adamw_t(w_in, gwt_mine.reshape(L, 2 * ri, D_MODEL), m_w_in, v_w_in)
    grad_w_out = gwo_mine.reshape(L, 2 * ro, D_MODEL)
    d_w_out, nm_w_out, nv_w_out = _adamw(w_out, grad_w_out, m_w_out, v_w_out)

    gsum = _all_sum_small(_pack_small(ggpre, ggpost, glb, gghead, gsinks, loss))
    gs, ds, ms, vs = _small_update(
        gsum, _pack_small(g_pre, g_post, lb_param, g_head, sinks),
        _pack_small(m_g_pre, m_g_post, m_lb_param, m_g_head, m_sinks),
        _pack_small(v_g_pre, v_g_post, v_lb_param, v_g_head, v_sinks))
    loss_all = gsum[3, HG_HEAD_DIM + ATT_HEADS]
    return (loss_all, dx.reshape(B, S, D_MODEL), grad_w_in, grad_w_out, *_unpack_small(gs),
            d_w_in, d_w_out, *_unpack_small(ds), nm_w_in, nm_w_out, *_unpack_small(ms),
            nv_w_in, nv_w_out, *_unpack_small(vs))
```

```python
import functools
import math

import jax
import jax.numpy as jnp
from jax import lax
from jax.experimental import pallas as pl
from jax.experimental.pallas import tpu as pltpu

f32 = jnp.float32
bf16 = jnp.bfloat16

D_MODEL = 1024
DEPTH = 2
HG_WIDTH = 1024
HG_HEAD_DIM = 128
HG_HEADS = 8
CHUNK = 64
SUB = 16
ATT_WIDTH = 1024
ATT_HEAD_DIM = 64
ATT_HEADS = 16
ATT_GROUP = 8
KV_WIDTH = 128
ATT_BLOCK = 128
ATT_SCALE = 1.0 / math.sqrt(ATT_HEAD_DIM)
ROPE_THETA = 10000.0
IN_WIDTH = 6400
MIX_WIDTH = 2048
NORM_EPS = 1e-6
NEG_INF = -1e30
LB_FLOOR = 1e-20
LANES = 128
VMEM_LIMIT = 48 * 1024 * 1024

ADAM_LR = 0.001
ADAM_B1 = 0.9
ADAM_B2 = 0.999
ADAM_EPS = 1e-08
ADAM_WD = 0.01
ADAM_STEP = 10

QA_BLK, ZA_BLK, KV_BLK = 4, 5, 24

NT = (((1,), (1,)), ((), ()))
TN = (((0,), (0,)), ((), ()))


def _dot(a, b, dims=None, precision=None):
    if dims is None:
        return jnp.dot(a, b, preferred_element_type=f32, precision=precision)
    return lax.dot_general(a, b, dims, preferred_element_type=f32, precision=precision)


def _sigmoid(x):
    return 1.0 / (1.0 + jnp.exp(-x))


def _params(*sem):
    return pltpu.CompilerParams(dimension_semantics=sem, vmem_limit_bytes=VMEM_LIMIT)


def _in_proj(x, g, wt, *, tm=512, tn=1280):
    T = x.shape[0]
    tm = min(tm, T)

    def body(x_ref, g_ref, w_ref, p_ref, h_ref, hs):
        @pl.when(pl.program_id(1) == 0)
        def _():
            xv = x_ref[...]
            r = lax.rsqrt(jnp.mean(xv * xv, axis=-1, keepdims=True) + NORM_EPS)
            hv = (xv * r * g_ref[...]).astype(bf16)
            hs[...] = hv
            h_ref[...] = hv
        p_ref[...] = _dot(hs[...], w_ref[...], NT)

    return pl.pallas_call(
        body, name="in_proj", grid=(T // tm, IN_WIDTH // tn),
        in_specs=[pl.BlockSpec((tm, D_MODEL), lambda i, j: (i, 0)),
                  pl.BlockSpec((1, D_MODEL), lambda i, j: (0, 0)),
                  pl.BlockSpec((tn, D_MODEL), lambda i, j: (j, 0))],
        out_specs=[pl.BlockSpec((tm, tn), lambda i, j: (i, j)),
                   pl.BlockSpec((tm, D_MODEL), lambda i, j: (i, 0))],
        out_shape=[jax.ShapeDtypeStruct((T, IN_WIDTH), f32), jax.ShapeDtypeStruct((T, D_MODEL), bf16)],
        scratch_shapes=[pltpu.VMEM((tm, D_MODEL), bf16)],
        compiler_params=_params("parallel", "arbitrary"),
    )(x, g, wt)


def _out_proj(ch, ca, wo, x, g, *, tm=512):
    T = x.shape[0]
    tm = min(tm, T)
    half = MIX_WIDTH // 2

    def body(ch_ref, ca_ref, wo_ref, x_ref, g_ref, xn_ref, y_ref):
        y = _dot(ch_ref[...], wo_ref[0:half, :]) + _dot(ca_ref[...], wo_ref[half:MIX_WIDTH, :])
        r = lax.rsqrt(jnp.mean(y * y, axis=-1, keepdims=True) + NORM_EPS)
        y_ref[...] = y
        xn_ref[...] = x_ref[...] + y * r * g_ref[...]

    row = lambda i: (i, 0)
    fixed = lambda i: (0, 0)
    return pl.pallas_call(
        body, name="out_proj", grid=(T // tm,),
        in_specs=[pl.BlockSpec((tm, half), row), pl.BlockSpec((tm, half), row),
                  pl.BlockSpec((MIX_WIDTH, D_MODEL), fixed), pl.BlockSpec((tm, D_MODEL), row),
                  pl.BlockSpec((1, D_MODEL), fixed)],
        out_specs=[pl.BlockSpec((tm, D_MODEL), row), pl.BlockSpec((tm, D_MODEL), row)],
        out_shape=[jax.ShapeDtypeStruct((T, D_MODEL), f32)] * 2,
        compiler_params=_params("parallel"),
    )(ch, ca, wo, x, g)


def _loss_head(y, target, *, tm=512):
    T = y.shape[0]
    tm = min(tm, T)

    def body(y_ref, t_ref, d_ref, l_ref):
        @pl.when(pl.program_id(0) == 0)
        def _():
            l_ref[...] = jnp.zeros_like(l_ref)
        err = y_ref[...] - t_ref[...]
        d_ref[...] = err * (1.0 / D_MODEL)
        l_ref[...] += jnp.sum(err * err) * (0.5 / D_MODEL)

    row = lambda i: (i, 0)
    return pl.pallas_call(
        body, name="loss_head", grid=(T // tm,),
        in_specs=[pl.BlockSpec((tm, D_MODEL), row), pl.BlockSpec((tm, D_MODEL), row)],
        out_specs=[pl.BlockSpec((tm, D_MODEL), row), pl.BlockSpec((8, LANES), lambda i: (0, 0))],
        out_shape=[jax.ShapeDtypeStruct((T, D_MODEL), f32), jax.ShapeDtypeStruct((8, LANES), f32)],
        compiler_params=_params("arbitrary"),
    )(y, target)


def _out_proj_bwd(dxn, y, g, wo, ch, ca, *, tm=256):
    T = y.shape[0]
    tm = min(tm, T)
    half = MIX_WIDTH // 2

    def body(dx_ref, y_ref, g_ref, wo_ref, ch_ref, ca_ref, dch_ref, dca_ref, dwo_ref, dg_ref):
        @pl.when(pl.program_id(0) == 0)
        def _():
            dwo_ref[...] = jnp.zeros_like(dwo_ref)
            dg_ref[...] = jnp.zeros_like(dg_ref)
        y = y_ref[...]
        dx = dx_ref[...]
        r = lax.rsqrt(jnp.mean(y * y, axis=-1, keepdims=True) + NORM_EPS)
        gy = dx * g_ref[...]
        dy = r * gy - y * (r * r * r) * jnp.mean(gy * y, axis=-1, keepdims=True)
        dg_ref[...] += jnp.sum(dx * y * r, axis=0, keepdims=True)
        dyb = dy.astype(bf16)
        dch_ref[...] = _dot(dyb, wo_ref[0:half, :], NT)
        dca_ref[...] = _dot(dyb, wo_ref[half:MIX_WIDTH, :], NT)
        dwo_ref[0:half, :] += _dot(ch_ref[...], dyb, TN)
        dwo_ref[half:MIX_WIDTH, :] += _dot(ca_ref[...], dyb, TN)

    row = lambda i: (i, 0)
    fixed = lambda i: (0, 0)
    return pl.pallas_call(
        body, name="out_proj_bwd", grid=(T // tm,),
        in_specs=[pl.BlockSpec((tm, D_MODEL), row), pl.BlockSpec((tm, D_MODEL), row),
                  pl.BlockSpec((1, D_MODEL), fixed), pl.BlockSpec((MIX_WIDTH, D_MODEL), fixed),
                  pl.BlockSpec((tm, half), row), pl.BlockSpec((tm, half), row)],
        out_specs=[pl.BlockSpec((tm, half), row), pl.BlockSpec((tm, half), row),
                   pl.BlockSpec((MIX_WIDTH, D_MODEL), fixed), pl.BlockSpec((1, D_MODEL), fixed)],
        out_shape=[jax.ShapeDtypeStruct((T, half), f32), jax.ShapeDtypeStruct((T, half), f32),
                   jax.ShapeDtypeStruct((MIX_WIDTH, D_MODEL), f32), jax.ShapeDtypeStruct((1, D_MODEL), f32)],
        compiler_params=_params("arbitrary"),
    )(dxn, y, g, wo, ch, ca)


def _in_proj_bwd(dproj, wt, x, g, dxn, *, tm=512, tk=1280):
    T = x.shape[0]
    tm = min(tm, T)
    nk = IN_WIDTH // tk

    def body(dp_ref, w_ref, x_ref, g_ref, dxn_ref, dx_ref, dg_ref, acc):
        i, k = pl.program_id(0), pl.program_id(1)

        @pl.when((i == 0) & (k == 0))
        def _():
            dg_ref[...] = jnp.zeros_like(dg_ref)

        @pl.when(k == 0)
        def _():
            acc[...] = jnp.zeros_like(acc)
        acc[...] += _dot(dp_ref[...], w_ref[...])

        @pl.when(k == nk - 1)
        def _():
            dh = acc[...]
            xv = x_ref[...]
            r = lax.rsqrt(jnp.mean(xv * xv, axis=-1, keepdims=True) + NORM_EPS)
            gy = dh * g_ref[...]
            dx_ref[...] = dxn_ref[...] + r * gy - xv * (r * r * r) * jnp.mean(gy * xv, axis=-1, keepdims=True)
            dg_ref[...] += jnp.sum(dh * xv * r, axis=0, keepdims=True)

    return pl.pallas_call(
        body, name="in_proj_bwd", grid=(T // tm, nk),
        in_specs=[pl.BlockSpec((tm, tk), lambda i, k: (i, k)), pl.BlockSpec((tk, D_MODEL), lambda i, k: (k, 0)),
                  pl.BlockSpec((tm, D_MODEL), lambda i, k: (i, 0)), pl.BlockSpec((1, D_MODEL), lambda i, k: (0, 0)),
                  pl.BlockSpec((tm, D_MODEL), lambda i, k: (i, 0))],
        out_specs=[pl.BlockSpec((tm, D_MODEL), lambda i, k: (i, 0)), pl.BlockSpec((1, D_MODEL), lambda i, k: (0, 0))],
        out_shape=[jax.ShapeDtypeStruct((T, D_MODEL), f32), jax.ShapeDtypeStruct((1, D_MODEL), f32)],
        scratch_shapes=[pltpu.VMEM((tm, D_MODEL), f32)],
        compiler_params=_params("arbitrary", "arbitrary"),
    )(dproj, wt, x, g, dxn)


def _grad_w_in(h, dproj, *, tn=640, tk=1024):
    T = h.shape[0]
    tk = min(tk, T)

    def body(h_ref, dp_ref, o_ref):
        @pl.when(pl.program_id(1) == 0)
        def _():
            o_ref[...] = jnp.zeros_like(o_ref)
        o_ref[...] += _dot(dp_ref[...], h_ref[...], TN)

    return pl.pallas_call(
        body, name="grad_w_in", grid=(IN_WIDTH // tn, T // tk),
        in_specs=[pl.BlockSpec((tk, D_MODEL), lambda j, k: (k, 0)), pl.BlockSpec((tk, tn), lambda j, k: (k, j))],
        out_specs=pl.BlockSpec((tn, D_MODEL), lambda j, k: (j, 0)),
        out_shape=jax.ShapeDtypeStruct((IN_WIDTH, D_MODEL), f32),
        compiler_params=_params("parallel", "arbitrary"),
    )(h, dproj)


def _lower_bound(lbp, layer):
    m = jnp.max(lbp, axis=0, keepdims=True)
    e = jnp.exp(lbp - m)
    p = e / jnp.sum(e, axis=0, keepdims=True)
    acc = p[0:1]
    for i in range(1, layer + 1):
        acc = acc + p[i:i + 1]
    return acc - p[0:1]


def _gate_parts(qr, fr, lb, lbf):
    sq = _sigmoid(qr)
    e = jnp.exp(-jnp.abs(fr))
    inv = 1.0 / (1.0 + e)
    pos = fr >= 0
    sg = jnp.where(pos, inv, e * inv)
    nsg = jnp.where(pos, e * inv, inv)
    fg = lbf + (1.0 - lb) * sg
    return qr * sq, sq, sg, nsg, fg, jnp.log(fg), (1.0 - lb) * nsg


def _anchor_masks(transposed=False):
    t = lax.broadcasted_iota(jnp.int32, (CHUNK, CHUNK), 1 if transposed else 0)
    s = lax.broadcasted_iota(jnp.int32, (CHUNK, CHUNK), 0 if transposed else 1)
    anchors = tuple(range(SUB - 1, CHUNK - 1, SUB))
    return anchors, [(t > a) & (s <= a) & (s > a - SUB) for a in anchors]


def _seg_sum(seg, x):
    hi = x.astype(bf16)
    return _dot(seg, hi) + _dot(seg, (x - hi.astype(f32)).astype(bf16))


def _hgrn_fwd(proj, lb_param, g_head, *, B, S, layer):
    T = B * S
    TB = min(256, S)
    nT, NC = S // TB, TB // CHUNK
    nC = S // CHUNK
    HD = HG_HEAD_DIM

    def body(q_ref, f_ref, i_ref, z_ref, lb_ref, gh_ref, cat_ref, op_ref, st_ref,
             s_scr, b_scr, k_scr):
        @pl.when(pl.program_id(2) == 0)
        def _():
            s_scr[...] = jnp.zeros_like(s_scr)
        lb = _lower_bound(lb_ref[...], layer)
        lbf = jnp.maximum(lb, LB_FLOOR)
        gh = gh_ref[...]
        r_i = lax.broadcasted_iota(jnp.int32, (CHUNK, CHUNK), 0)
        c_i = lax.broadcasted_iota(jnp.int32, (CHUNK, CHUNK), 1)
        tril = (r_i >= c_i).astype(f32)
        rows8 = lax.broadcasted_iota(jnp.int32, (8, HD), 0)
        lane_c = lax.broadcasted_iota(jnp.int32, (8, CHUNK), 1)
        anchors, masks = _anchor_masks()

        def chunk(c, st):
            rs = slice(c * CHUNK, (c + 1) * CHUNK)
            b_s, k_s = b_scr.at[c], k_scr.at[c]
            q, _, _, _, _, logf, k = _gate_parts(q_ref[rs, :], f_ref[rs, :], lb, lbf)
            v = i_ref[rs, :]
            b = _dot(tril, logf, precision=lax.Precision.HIGHEST)
            b_s[...] = b
            k_s[...] = k
            pieces = []
            for blk in range(CHUNK // SUB):
                r0 = blk * SUB
                bp = [b[r0 + 8 * i:r0 + 8 * i + 8] for i in range(SUB // 8)]
                qp = [q[r0 + 8 * i:r0 + 8 * i + 8] for i in range(SUB // 8)]
                ap = [jnp.zeros((8, CHUNK), f32) for _ in range(SUB // 8)]
                for s in range(SUB):
                    bs = b_s[r0 + s:r0 + s + 1, :]
                    ks = k_s[r0 + s:r0 + s + 1, :]
                    for i in range(s // 8, SUB // 8):
                        diff = bp[i] - bs
                        if i == s // 8:
                            diff = jnp.where(rows8 >= s - 8 * i, diff, NEG_INF)
                        col = jnp.sum(jnp.exp(diff) * qp[i] * ks, axis=1, keepdims=True)
                        ap[i] = jnp.where(lane_c == r0 + s, col, ap[i])
                pieces += ap
            a_all = jnp.concatenate(pieces, axis=0)
            for an, mk in zip(anchors, masks):
                beta = b_s[an:an + 1, :]
                qh = (q * jnp.exp(jnp.minimum(b - beta, 0.0))).astype(bf16)
                kh = (k * jnp.exp(jnp.minimum(beta - b, 0.0))).astype(bf16)
                a_all = a_all + jnp.where(mk, _dot(qh, kh, NT), 0.0)
            st_ref[0, 0, c] = st
            vb16 = v.astype(bf16)
            o = _dot(a_all.astype(bf16), vb16) + _dot((q * jnp.exp(b)).astype(bf16), st.astype(bf16), NT)
            b_end = b_s[CHUNK - 1:CHUNK, :]
            kdec = (k * jnp.exp(b_end - b)).astype(bf16)
            st_next = jnp.exp(b_end) * st + _dot(vb16, kdec, TN)
            rr = lax.rsqrt(jnp.mean(o * o, axis=-1, keepdims=True) + NORM_EPS)
            zr = z_ref[rs, :]
            cat_ref[rs, :] = (o * rr * gh * (zr * _sigmoid(zr))).astype(bf16)
            op_ref[rs, :] = o
            return st_next

        st = s_scr[...]
        for c in range(NC):
            st = chunk(c, st)
        s_scr[...] = st

    def col(part):
        return pl.BlockSpec((TB, HD), lambda b, h, n: (b * nT + n, part * HG_HEADS + h))

    out_col = pl.BlockSpec((TB, HD), lambda b, h, n: (b * nT + n, h))
    return pl.pallas_call(
        body, name=f"hgrn_fwd_l{layer}", grid=(B, HG_HEADS, nT),
        in_specs=[col(0), col(1), col(2), col(3),
                  pl.BlockSpec((DEPTH, HD), lambda b, h, n: (0, h)),
                  pl.BlockSpec((1, HD), lambda b, h, n: (0, 0))],
        out_specs=[out_col, out_col,
                   pl.BlockSpec((1, 1, NC, HD, HD), lambda b, h, n: (b, h, n, 0, 0))],
        out_shape=[jax.ShapeDtypeStruct((T, HG_WIDTH), bf16), jax.ShapeDtypeStruct((T, HG_WIDTH), f32),
                   jax.ShapeDtypeStruct((B, HG_HEADS, nC, HD, HD), f32)],
        scratch_shapes=[pltpu.VMEM((HD, HD), f32), pltpu.VMEM((NC, CHUNK, HD), f32), pltpu.VMEM((NC, CHUNK, HD), f32)],
        compiler_params=_params("parallel", "parallel", "arbitrary"),
    )(proj, proj, proj, proj, lb_param, g_head)


def _hgrn_bwd(proj, lb_param, g_head, o_pre, states, dcat, *, B, S, layer):
    T = B * S
    TB = min(256, S)
    nT, NC = S // TB, TB // CHUNK
    HD = HG_HEAD_DIM

    def body(q_ref, f_ref, i_ref, z_ref, lb_ref, gh_ref, op_ref, st_ref, dc_ref,
             dq_ref, df_ref, di_ref, dz_ref, dlb_ref, dgh_ref,
             ds_scr, b_scr, q_scr, do_scr, wk_scr):
        @pl.when(pl.program_id(2) == 0)
        def _():
            ds_scr[...] = jnp.zeros_like(ds_scr)
            dlb_ref[...] = jnp.zeros_like(dlb_ref)
            dgh_ref[...] = jnp.zeros_like(dgh_ref)
        lb = _lower_bound(lb_ref[...], layer)
        lbf = jnp.maximum(lb, LB_FLOOR)
        ind = (lb > LB_FLOOR).astype(f32)
        gh = gh_ref[...]
        r_i = lax.broadcasted_iota(jnp.int32, (CHUNK, CHUNK), 0)
        c_i = lax.broadcasted_iota(jnp.int32, (CHUNK, CHUNK), 1)
        tril = (r_i >= c_i).astype(f32)
        triu = (c_i >= r_i).astype(f32)
        rows8 = lax.broadcasted_iota(jnp.int32, (8, HD), 0)
        lane_c = lax.broadcasted_iota(jnp.int32, (8, CHUNK), 1)
        last_row = lax.broadcasted_iota(jnp.int32, (CHUNK, HD), 0) == CHUNK - 1
        anchors, masks = _anchor_masks()
        _, masks_t = _anchor_masks(transposed=True)
        seg_t = lax.broadcasted_iota(jnp.int32, (SUB, 8 * SUB), 0)
        seg_r = lax.broadcasted_iota(jnp.int32, (SUB, 8 * SUB), 1) // 8
        seg0 = (seg_r == seg_t).astype(bf16)
        seg1 = (seg_r[:, 0:4 * SUB] + 8 == seg_t[:, 0:4 * SUB]).astype(bf16)

        def chunk(c, dst1):
            rs = slice(c * CHUNK, (c + 1) * CHUNK)
            b_s, q_s, do_s = b_scr.at[c], q_scr.at[c], do_scr.at[c]
            qr, fr = q_ref[rs, :], f_ref[rs, :]
            q, sq, sg, nsg, fg, logf, k = _gate_parts(qr, fr, lb, lbf)
            v = i_ref[rs, :]
            b = _dot(tril, logf, precision=lax.Precision.HIGHEST)
            o = op_ref[rs, :]
            dc = dc_ref[rs, :]
            zr = z_ref[rs, :]
            sz = _sigmoid(zr)
            rr = lax.rsqrt(jnp.mean(o * o, axis=-1, keepdims=True) + NORM_EPS)
            dz_ref[rs, :] = (dc * (o * rr * gh) * (sz * (1.0 + zr * (1.0 - sz)))).astype(bf16)
            dn = dc * (zr * sz)
            dgh_ref[0, 0] += jnp.sum(dn * o * rr, axis=0, keepdims=True)
            gdn = dn * gh
            d_o = rr * gdn - o * (rr * rr * rr) * jnp.mean(gdn * o, axis=-1, keepdims=True)
            b_s[...] = b
            q_s[...] = q
            do_s[...] = d_o
            dob = d_o.astype(bf16)
            vb16 = v.astype(bf16)
            d_a = _dot(dob, vb16, NT)
            d_q = jnp.zeros((CHUNK, HD), f32)
            d_k = jnp.zeros((CHUNK, HD), f32)
            at_all = jnp.zeros((CHUNK, CHUNK), f32)
            for an, mk, mkt in zip(anchors, masks, masks_t):
                beta = b_s[an:an + 1, :]
                eq = jnp.exp(jnp.minimum(b - beta, 0.0))
                ek = jnp.exp(jnp.minimum(beta - b, 0.0))
                qh = (q * eq).astype(bf16)
                kh = (k * ek).astype(bf16)
                at_all = at_all + jnp.where(mkt, _dot(kh, qh, NT), 0.0)
                d_aa = jnp.where(mk, d_a, 0.0).astype(bf16)
                d_q = d_q + _dot(d_aa, kh) * eq
                d_k = d_k + _dot(d_aa, qh, TN) * ek
            st0 = st_ref[0, 0, c]
            dst1b = dst1.astype(bf16)
            eb = jnp.exp(b)
            b_end = b_s[CHUNK - 1:CHUNK, :]
            edec = jnp.exp(b_end - b)
            e_end = jnp.exp(b_end)
            kdec = (k * edec).astype(bf16)
            qdec = (q * eb).astype(bf16)
            d_q = d_q + _dot(dob, st0.astype(bf16)) * eb
            d_v = _dot(kdec, dst1b, NT)
            d_k = d_k + _dot(vb16, dst1b) * edec
            st1 = e_end * st0 + _dot(vb16, kdec, TN)
            rterm = jnp.sum(dst1 * st1, axis=0, keepdims=True)
            dst0 = e_end * dst1 + _dot(dob, qdec, TN)
            dq_blocks, dk_pieces, at_pieces = [], [], []
            for blk in range(CHUNK // SUB):
                r0 = blk * SUB
                wk = wk_scr.at[c * (CHUNK // SUB) + blk]
                bp = [b[r0 + 8 * i:r0 + 8 * i + 8] for i in range(SUB // 8)]
                kp = [k[r0 + 8 * i:r0 + 8 * i + 8] for i in range(SUB // 8)]
                vp = [v[r0 + 8 * i:r0 + 8 * i + 8] for i in range(SUB // 8)]
                dkp = [jnp.zeros((8, HD), f32) for _ in range(SUB // 8)]
                atp = [jnp.zeros((8, CHUNK), f32) for _ in range(SUB // 8)]
                for t in range(SUB):
                    bt = b_s[r0 + t:r0 + t + 1, :]
                    qt = q_s[r0 + t:r0 + t + 1, :]
                    dot_ = do_s[r0 + t:r0 + t + 1, :]
                    for i in range(t // 8 + 1):
                        diff = bt - bp[i]
                        if i == t // 8:
                            diff = jnp.where(rows8 <= t - 8 * i, diff, NEG_INF)
                        e = jnp.exp(diff)
                        a = jnp.sum(e * kp[i] * qt, axis=1, keepdims=True)
                        atp[i] = jnp.where(lane_c == r0 + t, a, atp[i])
                        w = jnp.sum(vp[i] * dot_, axis=1, keepdims=True) * e
                        dkp[i] = dkp[i] + w * qt
                        row = 8 * t if i == 0 else 8 * SUB + 8 * (t - 8)
                        wk[row:row + 8, :] = w * kp[i]
                dq_blocks.append(_seg_sum(seg0, wk[0:8 * SUB, :]) + _seg_sum(seg1, wk[8 * SUB:12 * SUB, :]))
                dk_pieces += dkp
                at_pieces += atp
            d_q = d_q + jnp.concatenate(dq_blocks, axis=0)
            d_k = d_k + jnp.concatenate(dk_pieces, axis=0)
            d_v = d_v + _dot((at_all + jnp.concatenate(at_pieces, axis=0)).astype(bf16), dob)
            db = q * d_q - k * d_k + jnp.where(last_row, rterm, 0.0)
            dlt = _dot(triu, db, precision=lax.Precision.HIGHEST) - fg * d_k
            df_ref[rs, :] = (dlt * (1.0 - lb) * sg * nsg / fg).astype(bf16)
            dlb_ref[0] += jnp.sum(dlt * (ind - sg) / fg, axis=0, keepdims=True)
            dq_ref[rs, :] = (d_q * (sq * (1.0 + qr * (1.0 - sq)))).astype(bf16)
            di_ref[rs, :] = d_v.astype(bf16)
            return dst0

        dst = ds_scr[...]
        for c in reversed(range(NC)):
            dst = chunk(c, dst)
        ds_scr[...] = dst

    def col(part):
        return pl.BlockSpec((TB, HD), lambda b, h, n: (b * nT + nT - 1 - n, part * HG_HEADS + h))

    hcol = pl.BlockSpec((TB, HD), lambda b, h, n: (b * nT + nT - 1 - n, h))
    return pl.pallas_call(
        body, name=f"hgrn_bwd_l{layer}", grid=(B, HG_HEADS, nT),
        in_specs=[col(0), col(1), col(2), col(3),
                  pl.BlockSpec((DEPTH, HD), lambda b, h, n: (0, h)),
                  pl.BlockSpec((1, HD), lambda b, h, n: (0, 0)),
                  hcol,
                  pl.BlockSpec((1, 1, NC, HD, HD), lambda b, h, n: (b, h, nT - 1 - n, 0, 0)),
                  hcol],
        out_specs=[hcol, hcol, hcol, hcol,
                   pl.BlockSpec((1, 1, HD), lambda b, h, n: (b, 0, h)),
                   pl.BlockSpec((1, 1, 1, HD), lambda b, h, n: (b, h, 0, 0))],
        out_shape=[jax.ShapeDtypeStruct((T, HG_WIDTH), bf16)] * 4 + [
            jax.ShapeDtypeStruct((B, 1, HG_WIDTH), f32), jax.ShapeDtypeStruct((B, HG_HEADS, 1, HD), f32)],
        scratch_shapes=[pltpu.VMEM((HD, HD), f32)] + [pltpu.VMEM((NC, CHUNK, HD), f32)] * 3
        + [pltpu.VMEM((NC * CHUNK // SUB, 12 * SUB, HD), f32)],
        compiler_params=_params("parallel", "parallel", "arbitrary"),
    )(proj, proj, proj, proj, lb_param, g_head, o_pre, states, dcat)


def _rope_tables(S):
    half = ATT_HEAD_DIM // 2
    inv_freq = ROPE_THETA ** (-jnp.arange(half, dtype=f32) / half)
    ang = jnp.arange(S, dtype=f32)[:, None] * inv_freq[None, :]
    cos, sin = jnp.cos(ang), jnp.sin(ang)
    return jnp.tile(jnp.concatenate([cos, cos], axis=1), (1, 2)), jnp.tile(jnp.concatenate([-sin, sin], axis=1), (1, 2))


def _swap_halves(x, first_half):
    return jnp.where(first_half, pltpu.roll(x, LANES - ATT_HEAD_DIM // 2, 1), pltpu.roll(x, ATT_HEAD_DIM // 2, 1))


def _rope(x, cos, sin, first_half):
    return x * cos + _swap_halves(x, first_half) * sin


def _rope_bwd(dy, cos, sin, first_half):
    return dy * cos + _swap_halves(dy * sin, first_half)


def _attn_consts(n):
    lane = lax.broadcasted_iota(jnp.int32, (1, LANES), 1)
    low = lane < ATT_HEAD_DIM
    first_half = (lane % ATT_HEAD_DIM) < ATT_HEAD_DIM // 2
    top = lax.broadcasted_iota(jnp.int32, (LANES, 1), 0) < ATT_HEAD_DIM
    s = lax.broadcasted_iota(jnp.int32, (2 * ATT_BLOCK, ATT_BLOCK), 0)
    t = lax.broadcasted_iota(jnp.int32, (2 * ATT_BLOCK, ATT_BLOCK), 1)
    mask = (s > t) & (s <= t + ATT_BLOCK) & ((s >= ATT_BLOCK) | (n > 0))
    return low, first_half, top, mask


def _dup_kv(x, low):
    rolled = pltpu.roll(x, ATT_HEAD_DIM, 1)
    return [jnp.where(low, x, rolled), jnp.where(low, rolled, x)]


def _attn_head(qtm, kd, vdt, sink, mask):
    s = jnp.where(mask, _dot(kd, qtm) * ATT_SCALE, NEG_INF)
    m = jnp.maximum(jnp.max(s, axis=0, keepdims=True), sink)
    p = jnp.exp(s - m)
    psink = jnp.exp(sink - m)
    inv = 1.0 / (jnp.sum(p, axis=0, keepdims=True) + psink)
    pn = p * inv
    return pn, psink * inv, _dot(vdt, pn.astype(bf16))


def _swa_fwd(proj, sink_b, cos, sin, *, B, S):
    T = B * S
    L = ATT_BLOCK
    nB = S // L

    def body(q_ref, z_ref, kvc_ref, kvp_ref, sk_ref, cc_ref, sc_ref, cp_ref, sp_ref, cat_ref):
        n = pl.program_id(1)
        low, first_half, top, mask = _attn_consts(n)
        cc, sc = cc_ref[...], sc_ref[...]
        kc = _rope(kvc_ref[:, 0:LANES], cc, sc, first_half)
        kp = _rope(kvp_ref[:, 0:LANES], cp_ref[...], sp_ref[...], first_half)
        kd = [x.astype(bf16) for x in _dup_kv(jnp.concatenate([kp, kc], axis=0), low)]
        vdt = [x.T.astype(bf16) for x in _dup_kv(jnp.concatenate([kvp_ref[:, LANES:2 * LANES], kvc_ref[:, LANES:2 * LANES]], axis=0), low)]
        for pair in range(ATT_HEADS // 2):
            cols = slice(pair * LANES, (pair + 1) * LANES)
            j = (2 * pair) // ATT_GROUP
            qt = _rope(q_ref[:, cols], cc, sc, first_half).T
            outs = []
            for hh in range(2):
                h = 2 * pair + hh
                qtm = jnp.where(top if hh == 0 else ~top, qt, 0.0).astype(bf16)
                _, _, o = _attn_head(qtm, kd[j], vdt[j], sk_ref[h:h + 1, 0:1], mask)
                outs.append(o)
            zp = z_ref[:, cols]
            cat_ref[:, cols] = (jnp.where(top, outs[0], outs[1]).T * (zp * _sigmoid(zp))).astype(bf16)

    cur = lambda b, n: (b * nB + n, 0)
    prev = lambda b, n: (b * nB + jnp.maximum(n - 1, 0), 0)
    return pl.pallas_call(
        body, name="swa_fwd", grid=(B, nB),
        in_specs=[pl.BlockSpec((L, ATT_WIDTH), lambda b, n: (b * nB + n, QA_BLK)),
                  pl.BlockSpec((L, ATT_WIDTH), lambda b, n: (b * nB + n, ZA_BLK)),
                  pl.BlockSpec((L, 2 * KV_WIDTH), lambda b, n: (b * nB + n, KV_BLK)),
                  pl.BlockSpec((L, 2 * KV_WIDTH), lambda b, n: (b * nB + jnp.maximum(n - 1, 0), KV_BLK)),
                  pl.BlockSpec((ATT_HEADS, LANES), lambda b, n: (0, 0)),
                  pl.BlockSpec((L, LANES), lambda b, n: (n, 0)), pl.BlockSpec((L, LANES), lambda b, n: (n, 0)),
                  pl.BlockSpec((L, LANES), lambda b, n: (jnp.maximum(n - 1, 0), 0)),
                  pl.BlockSpec((L, LANES), lambda b, n: (jnp.maximum(n - 1, 0), 0))],
        out_specs=pl.BlockSpec((L, ATT_WIDTH), cur),
        out_shape=jax.ShapeDtypeStruct((T, ATT_WIDTH), bf16),
        compiler_params=_params("parallel", "parallel"),
    )(proj, proj, proj, proj, sink_b, cos, sin, cos, sin)


def _swa_bwd(proj, sink_b, cos, sin, dcat, *, B, S):
    T = B * S
    L = ATT_BLOCK
    nB = S // L

    def body(q_ref, z_ref, kvc_ref, kvp_ref, sk_ref, cc_ref, sc_ref, cp_ref, sp_ref, dc_ref,
             dq_ref, dz_ref, dkv_ref, dsk_ref, carry, ds_st, pn_st, q_st, do_st):
        step = pl.program_id(1)
        n = nB - 1 - step

        @pl.when((pl.program_id(0) == 0) & (step == 0))
        def _():
            dsk_ref[...] = jnp.zeros_like(dsk_ref)

        @pl.when(step == 0)
        def _():
            carry[...] = jnp.zeros_like(carry)
        low, first_half, top, mask = _attn_consts(n)
        cc, sc, cp, sp = cc_ref[...], sc_ref[...], cp_ref[...], sp_ref[...]
        kc = _rope(kvc_ref[:, 0:LANES], cc, sc, first_half)
        kp = _rope(kvp_ref[:, 0:LANES], cp, sp, first_half)
        kdf = _dup_kv(jnp.concatenate([kp, kc], axis=0), low)
        vdf = _dup_kv(jnp.concatenate([kvp_ref[:, LANES:2 * LANES], kvc_ref[:, LANES:2 * LANES]], axis=0), low)
        kd = [x.astype(bf16) for x in kdf]
        vd = [x.astype(bf16) for x in vdf]
        kdt = [x.T.astype(bf16) for x in kdf]
        vdt = [x.T.astype(bf16) for x in vdf]
        dkd, dvd = [], []
        for pair in range(ATT_HEADS // 2):
            cols = slice(pair * LANES, (pair + 1) * LANES)
            j = (2 * pair) // ATT_GROUP
            qp = _rope(q_ref[:, cols], cc, sc, first_half)
            qt = qp.T
            zp = z_ref[:, cols]
            dc = dc_ref[:, cols]
            sz = _sigmoid(zp)
            d_o = dc * (zp * sz)
            dot_ = d_o.T
            res = []
            for hh in range(2):
                rsel = top if hh == 0 else ~top
                qtm = jnp.where(rsel, qt, 0.0).astype(bf16)
                pn, psn, o = _attn_head(qtm, kd[j], vdt[j], sk_ref[2 * pair + hh:2 * pair + hh + 1, 0:1], mask)
                res.append((rsel, pn, psn, o))
            ot = jnp.where(top, res[0][3], res[1][3])
            dz_ref[:, cols] = (dc * ot.T * (sz * (1.0 + zp * (1.0 - sz)))).astype(bf16)
            dqts = []
            for hh in range(2):
                h = 2 * pair + hh
                rsel, pn, psn, _ = res[hh]
                lsel = low if hh == 0 else ~low
                dotm = jnp.where(rsel, dot_, 0.0)
                delta = jnp.sum(dotm * ot, axis=0, keepdims=True)
                dst = (pn * (_dot(vd[j], dotm.astype(bf16)) - delta) * ATT_SCALE).astype(bf16)
                dsk_ref[h:h + 1, :] += jnp.zeros((1, LANES), f32) - jnp.sum(psn * delta)
                dqts.append(_dot(kdt[j], dst))
                g = h % ATT_GROUP
                ds_st[:, g * LANES:(g + 1) * LANES] = dst
                pn_st[:, g * LANES:(g + 1) * LANES] = pn.astype(bf16)
                q_st[g * LANES:(g + 1) * LANES, :] = jnp.where(lsel, qp, 0.0).astype(bf16)
                do_st[g * LANES:(g + 1) * LANES, :] = jnp.where(lsel, d_o, 0.0).astype(bf16)
            dq_ref[:, cols] = _rope_bwd(jnp.where(top, dqts[0], dqts[1]).T, cc, sc, first_half).astype(bf16)
            if (2 * pair + 2) % ATT_GROUP == 0:
                dkd.append(_dot(ds_st[...], q_st[...]))
                dvd.append(_dot(pn_st[...], do_st[...]))
        dk = [x + pltpu.roll(x, ATT_HEAD_DIM, 1) for x in dkd]
        dv = [x + pltpu.roll(x, ATT_HEAD_DIM, 1) for x in dvd]
        dk = jnp.where(low, dk[0], dk[1])
        dv = jnp.where(low, dv[0], dv[1])
        dkv_ref[:, 0:LANES] = (_rope_bwd(dk[L:2 * L], cc, sc, first_half) + carry[:, 0:LANES]).astype(bf16)
        dkv_ref[:, LANES:2 * LANES] = (dv[L:2 * L] + carry[:, LANES:2 * LANES]).astype(bf16)
        carry[:, 0:LANES] = _rope_bwd(dk[0:L], cp, sp, first_half)
        carry[:, LANES:2 * LANES] = dv[0:L]

    rev = lambda b, s: b * nB + nB - 1 - s
    revp = lambda b, s: b * nB + jnp.maximum(nB - 2 - s, 0)
    wide = lambda blk: pl.BlockSpec((L, ATT_WIDTH), lambda b, s: (rev(b, s), blk))
    tab = pl.BlockSpec((L, LANES), lambda b, s: (nB - 1 - s, 0))
    tabp = pl.BlockSpec((L, LANES), lambda b, s: (jnp.maximum(nB - 2 - s, 0), 0))
    return pl.pallas_call(
        body, name="swa_bwd", grid=(B, nB),
        in_specs=[wide(QA_BLK), wide(ZA_BLK),
                  pl.BlockSpec((L, 2 * KV_WIDTH), lambda b, s: (rev(b, s), KV_BLK)),
                  pl.BlockSpec((L, 2 * KV_WIDTH), lambda b, s: (revp(b, s), KV_BLK)),
                  pl.BlockSpec((ATT_HEADS, LANES), lambda b, s: (0, 0)),
                  tab, tab, tabp, tabp, wide(0)],
        out_specs=[wide(0), wide(0), pl.BlockSpec((L, 2 * KV_WIDTH), lambda b, s: (rev(b, s), 0)),
                   pl.BlockSpec((ATT_HEADS, LANES), lambda b, s: (0, 0))],
        out_shape=[jax.ShapeDtypeStruct((T, ATT_WIDTH), bf16), jax.ShapeDtypeStruct((T, ATT_WIDTH), bf16),
                   jax.ShapeDtypeStruct((T, 2 * KV_WIDTH), bf16), jax.ShapeDtypeStruct((ATT_HEADS, LANES), f32)],
        scratch_shapes=[pltpu.VMEM((L, 2 * KV_WIDTH), f32),
                        pltpu.VMEM((2 * L, ATT_GROUP * LANES), bf16), pltpu.VMEM((2 * L, ATT_GROUP * LANES), bf16),
                        pltpu.VMEM((ATT_GROUP * LANES, LANES), bf16), pltpu.VMEM((ATT_GROUP * LANES, LANES), bf16)],
        compiler_params=_params("arbitrary", "arbitrary"),
    )(proj, proj, proj, proj, sink_b, cos, sin, cos, sin, dcat)


def _local_grads(x, target, wt, w_out_b, g_pre, g_post, lb_param, g_head, sinks, *, B, S):
    cos, sin = _rope_tables(S)
    wt_int = jnp.concatenate([wt[:, 0:5120], wt[:, 5376:6400], wt[:, 5120:5376]], axis=1)
    saved = []
    for l in range(DEPTH):
        proj, h = _in_proj(x, g_pre[l:l + 1], wt_int[l])
        ch, o_pre, states = _hgrn_fwd(proj, lb_param, g_head[l:l + 1], B=B, S=S, layer=l)
        sink_b = jnp.broadcast_to(sinks[l][:, None], (ATT_HEADS, LANES))
        ca = _swa_fwd(proj, sink_b, cos, sin, B=B, S=S)
        xn, y = _out_proj(ch, ca, w_out_b[l], x, g_post[l:l + 1])
        saved.append((x, proj, h, ch, o_pre, states, sink_b, ca, y))
        x = xn
    dx, loss = _loss_head(x, target)
    gw_in, gw_out, gg_pre, gg_post, g_lb, gg_head, g_sinks = [], [], [], [], [], [], []
    for l in reversed(range(DEPTH)):
        x_in, proj, h, ch, o_pre, states, sink_b, ca, y = saved[l]
        dch, dca, dwo, dgpost = _out_proj_bwd(dx, y, g_post[l:l + 1], w_out_b[l], ch, ca)
        dq, df, di, dz, dlb, dgh = _hgrn_bwd(proj, lb_param, g_head[l:l + 1], o_pre, states, dch, B=B, S=S, layer=l)
        dqa, dza, dkv, dsk = _swa_bwd(proj, sink_b, cos, sin, dca, B=B, S=S)
        dproj = jnp.concatenate([dq, df, di, dz, dqa, dkv, dza], axis=1)
        dx, dgpre = _in_proj_bwd(dproj, wt[l], x_in, g_pre[l:l + 1], dx)
        gw_in.append(_grad_w_in(h, dproj))
        gw_out.append(dwo)
        gg_pre.append(dgpre[0])
        gg_post.append(dgpost[0])
        g_lb.append(jnp.sum(dlb, axis=(0, 1)))
        gg_head.append(jnp.sum(dgh, axis=(0, 1, 2)))
        g_sinks.append(dsk[:, 0])
    rev = lambda xs: jnp.stack(xs[::-1])
    return (loss[0, 0], dx, rev(gw_in), rev(gw_out), rev(gg_pre), rev(gg_post), rev(g_lb), rev(gg_head), rev(g_sinks))


MESH = pl.DeviceIdType.MESH
ANY = pl.BlockSpec(memory_space=pl.ANY)


def _place():
    x, y, c = lax.axis_index("x"), lax.axis_index("y"), lax.axis_index("c")
    return x, y, c, [(1 - x, y), (x, 1 - y), (1 - x, 1 - y)]


def _rcopy(src, dst, send, recv, k, to):
    return pltpu.make_async_remote_copy(src_ref=src, dst_ref=dst, send_sem=send.at[k], recv_sem=recv.at[k],
                                        device_id=to, device_id_type=MESH)


def _gather_shards(shards):
    n = len(shards)

    def body(*refs):
        outs = refs[n:2 * n]
        send, recv = refs[2 * n:]
        x, y, c, chips = _place()
        me = 2 * x + y
        sib = (x, y, 1 - c)
        first = []
        for j, chip in enumerate(chips):
            for a in range(n):
                mine = outs[a].at[:, me, c]
                first.append(_rcopy(mine, mine, send, recv, j * n + a, (*chip, c)))
        for cp in first:
            cp.start()
        passed = []
        for j, (px, py) in enumerate(chips):
            for a in range(n):
                blk = outs[a].at[:, 2 * px + py, c]
                _rcopy(blk, blk, send, recv, j * n + a, sib).wait_recv()
                cp = _rcopy(blk, blk, send, recv, (3 + j) * n + a, sib)
                cp.start()
                passed.append(cp)
        for j, (px, py) in enumerate(chips):
            for a in range(n):
                blk = outs[a].at[:, 2 * px + py, 1 - c]
                _rcopy(blk, blk, send, recv, (3 + j) * n + a, sib).wait_recv()
        for cp in first + passed:
            cp.wait_send()

    return pl.pallas_call(
        body, name="gather_shards",
        in_specs=[ANY] * n, out_specs=[ANY] * n,
        out_shape=[jax.ShapeDtypeStruct(s.shape, s.dtype) for s in shards],
        input_output_aliases={a: a for a in range(n)},
        scratch_shapes=[pltpu.SemaphoreType.DMA((6 * n,)), pltpu.SemaphoreType.DMA((6 * n,))],
    )(*shards)


def _pair_exchange(parts):
    n = len(parts)

    def body(*refs):
        ins, outs = refs[:n], refs[n:2 * n]
        send, recv = refs[2 * n:]
        x, y, c, _ = _place()
        cps = [_rcopy(ins[a].at[:, :, 1 - c], outs[a], send, recv, a, (x, y, 1 - c)) for a in range(n)]
        for cp in cps:
            cp.start()
        for cp in cps:
            cp.wait()

    return pl.pallas_call(
        body, name="pair_exchange", in_specs=[ANY] * n, out_specs=[ANY] * n,
        out_shape=[jax.ShapeDtypeStruct(p.shape[:2] + p.shape[3:], p.dtype) for p in parts],
        scratch_shapes=[pltpu.SemaphoreType.DMA((n,)), pltpu.SemaphoreType.DMA((n,))],
    )(*parts)


def _block_rows(r):
    return r if r <= 512 else r // 2


def _pair_add(part, got):
    L, K, _, r, C = part.shape
    rows = _block_rows(r)

    def body(c_ref, a_ref, b_ref, o_ref):
        o_ref[0, 0] = (a_ref[0, 0, 0] + b_ref[0, 0]).astype(bf16)

    blk = (1, 1, rows, C)
    return pl.pallas_call(
        body, name="pair_add",
        grid_spec=pltpu.PrefetchScalarGridSpec(
            num_scalar_prefetch=1, grid=(L, K, r // rows),
            in_specs=[pl.BlockSpec((1, 1, 1, rows, C), lambda l, k, i, c: (l, k, c[0], i, 0)),
                      pl.BlockSpec(blk, lambda l, k, i, c: (l, k, i, 0))],
            out_specs=pl.BlockSpec(blk, lambda l, k, i, c: (l, k, i, 0))),
        out_shape=jax.ShapeDtypeStruct((L, K, r, C), bf16),
        compiler_params=_params("parallel", "parallel", "parallel"),
    )(jnp.reshape(lax.axis_index("c"), (1,)).astype(jnp.int32), part, got)


def _chip_exchange(sums):
    n = len(sums)

    def body(*refs):
        ins, outs = refs[:n], refs[n:2 * n]
        send, recv = refs[2 * n:]
        x, y, c, chips = _place()
        cps = []
        for j, (px, py) in enumerate(chips):
            for a in range(n):
                cps.append(_rcopy(ins[a].at[:, 2 * px + py], outs[a].at[j], send, recv, j * n + a, (px, py, c)))
        for cp in cps:
            cp.start()
        for cp in cps:
            cp.wait()

    return pl.pallas_call(
        body, name="chip_exchange", in_specs=[ANY] * n, out_specs=[ANY] * n,
        out_shape=[jax.ShapeDtypeStruct((3, s.shape[0]) + s.shape[2:], s.dtype) for s in sums],
        scratch_shapes=[pltpu.SemaphoreType.DMA((3 * n,)), pltpu.SemaphoreType.DMA((3 * n,))],
    )(*sums)


def _chip_sum(mine, got):
    L, K, r, C = mine.shape
    rows = _block_rows(r)

    def body(p_ref, a_ref, b_ref, o_ref):
        o_ref[0, 0] = (a_ref[0, 0].astype(f32) + b_ref[0, 0].astype(f32)) + (b_ref[1, 0].astype(f32) + b_ref[2, 0].astype(f32))

    place = jnp.stack([2 * lax.axis_index("x") + lax.axis_index("y"), lax.axis_index("c")]).astype(jnp.int32)
    return pl.pallas_call(
        body, name="chip_sum",
        grid_spec=pltpu.PrefetchScalarGridSpec(
            num_scalar_prefetch=1, grid=(L, r // rows),
            in_specs=[pl.BlockSpec((1, 1, rows, C), lambda l, i, p: (l, p[0], i, 0)),
                      pl.BlockSpec((3, 1, rows, C), lambda l, i, p: (0, l, i, 0))],
            out_specs=pl.BlockSpec((1, 1, rows, C), lambda l, i, p: (l, p[1], i, 0))),
        out_shape=jax.ShapeDtypeStruct((L, 2, r, C), f32),
        compiler_params=_params("parallel", "parallel"),
    )(place, mine, got)


def _pair_share(bufs):
    n = len(bufs)

    def body(*refs):
        outs = refs[n:2 * n]
        send, recv = refs[2 * n:]
        x, y, c, _ = _place()
        cps = [_rcopy(outs[a].at[:, c], outs[a].at[:, c], send, recv, a, (x, y, 1 - c)) for a in range(n)]
        for cp in cps:
            cp.start()
        for a in range(n):
            theirs = outs[a].at[:, 1 - c]
            _rcopy(theirs, theirs, send, recv, a, (x, y, 1 - c)).wait_recv()
        for cp in cps:
            cp.wait_send()

    return pl.pallas_call(
        body, name="pair_share", in_specs=[ANY] * n, out_specs=[ANY] * n,
        out_shape=[jax.ShapeDtypeStruct(b.shape, b.dtype) for b in bufs],
        input_output_aliases={a: a for a in range(n)},
        scratch_shapes=[pltpu.SemaphoreType.DMA((n,)), pltpu.SemaphoreType.DMA((n,))],
    )(*bufs)


def _all_sum_small(v):
    def body(v_ref, o_ref, buf, send, recv):
        x, y, c, _ = _place()
        me = 4 * x + 2 * y + c
        buf[me] = v_ref[...]
        cps = []
        for m in range(1, 8):
            to = (x ^ (m >> 2), y ^ ((m >> 1) & 1), c ^ (m & 1))
            cps.append(_rcopy(v_ref, buf.at[me], send, recv, m - 1, to))
        for cp in cps:
            cp.start()
        for cp in cps:
            cp.wait()
        acc = buf[0]
        for d in range(1, 8):
            acc = acc + buf[d]
        o_ref[...] = acc

    vm = pl.BlockSpec(memory_space=pltpu.VMEM)
    return pl.pallas_call(
        body, name="all_sum_small", in_specs=[vm], out_specs=vm,
        out_shape=jax.ShapeDtypeStruct(v.shape, v.dtype),
        scratch_shapes=[pltpu.VMEM((8,) + v.shape, v.dtype), pltpu.SemaphoreType.DMA((7,)), pltpu.SemaphoreType.DMA((7,))],
    )(v)


def _adamw_math(w, g, m, v):
    m = ADAM_B1 * m + (1.0 - ADAM_B1) * g
    v = ADAM_B2 * v + (1.0 - ADAM_B2) * (g * g)
    m_hat = m / (1.0 - ADAM_B1 ** ADAM_STEP)
    v_hat = v / (1.0 - ADAM_B2 ** ADAM_STEP)
    return -ADAM_LR * (m_hat / (jnp.sqrt(v_hat) + ADAM_EPS) + ADAM_WD * w), m, v


def _adamw(w, g, m, v, *, rows=256):
    L, R, C = w.shape
    rows = min(rows, R)

    def body(w_ref, g_ref, m_ref, v_ref, d_ref, mo_ref, vo_ref):
        d_ref[...], mo_ref[...], vo_ref[...] = _adamw_math(w_ref[...], g_ref[...], m_ref[...], v_ref[...])

    blk = pl.BlockSpec((1, rows, C), lambda l, i: (l, i, 0))
    return pl.pallas_call(
        body, name="adamw", grid=(L, R // rows), in_specs=[blk] * 4, out_specs=[blk] * 3,
        out_shape=[jax.ShapeDtypeStruct(w.shape, f32)] * 3,
        compiler_params=_params("parallel", "parallel"),
    )(w, g, m, v)


def _chip_index():
    return jnp.reshape(2 * lax.axis_index("x") + lax.axis_index("y"), (1,)).astype(jnp.int32)


def _shard_transposed(w, *, tm=256):
    L, R, C = w.shape

    def body(k_ref, w_ref, o_ref, eye):
        @pl.when((pl.program_id(0) == 0) & (pl.program_id(1) == 0))
        def _():
            r = lax.broadcasted_iota(jnp.int32, (C, C), 0)
            c = lax.broadcasted_iota(jnp.int32, (C, C), 1)
            eye[...] = (r == c).astype(bf16)
        o_ref[0, 0] = _dot(eye[...], w_ref[0].astype(bf16), NT).astype(bf16)

    return pl.pallas_call(
        body, name="shard_transposed",
        grid_spec=pltpu.PrefetchScalarGridSpec(
            num_scalar_prefetch=1, grid=(L, R // tm),
            in_specs=[pl.BlockSpec((1, tm, C), lambda l, i, k: (l, i, 0))],
            out_specs=pl.BlockSpec((1, 1, C, tm), lambda l, i, k: (l, k[0], 0, i)),
            scratch_shapes=[pltpu.VMEM((C, C), bf16)]),
        out_shape=jax.ShapeDtypeStruct((L, 4, C, R), bf16),
        compiler_params=_params("arbitrary", "arbitrary"),
    )(_chip_index(), w)


def _shard_placed(w):
    L, R, C = w.shape

    def body(k_ref, w_ref, o_ref):
        o_ref[0, 0] = w_ref[0].astype(bf16)

    return pl.pallas_call(
        body, name="shard_placed",
        grid_spec=pltpu.PrefetchScalarGridSpec(
            num_scalar_prefetch=1, grid=(L,),
            in_specs=[pl.BlockSpec((1, R, C), lambda l, k: (l, 0, 0))],
            out_specs=pl.BlockSpec((1, 1, R, C), lambda l, k: (l, k[0], 0, 0))),
        out_shape=jax.ShapeDtypeStruct((L, 4, R, C), bf16),
        compiler_params=_params("parallel"),
    )(_chip_index(), w)


def _adamw_t(w, gt, m, v, *, tm=256):
    L, R, C = w.shape

    def body(w_ref, gt_ref, m_ref, v_ref, g_ref, d_ref, mo_ref, vo_ref):
        eye = (lax.broadcasted_iota(jnp.int32, (tm, tm), 0) == lax.broadcasted_iota(jnp.int32, (tm, tm), 1)).astype(bf16)
        x = gt_ref[0]
        hi = x.astype(bf16)
        rest = x - hi.astype(f32)
        mid = rest.astype(bf16)
        lo = (rest - mid.astype(f32)).astype(bf16)
        g = (_dot(eye, hi, NT) + _dot(eye, mid, NT)) + _dot(eye, lo, NT)
        g_ref[0] = g
        d_ref[0], mo_ref[0], vo_ref[0] = _adamw_math(w_ref[0], g, m_ref[0], v_ref[0])

    blk = pl.BlockSpec((1, tm, C), lambda l, i: (l, i, 0))
    return pl.pallas_call(
        body, name="adamw_t", grid=(L, R // tm),
        in_specs=[blk, pl.BlockSpec((1, C, tm), lambda l, i: (l, 0, i)), blk, blk], out_specs=[blk] * 4,
        out_shape=[jax.ShapeDtypeStruct(w.shape, f32)] * 4,
        compiler_params=_params("parallel", "parallel"),
    )(w, gt, m, v)


SMALL_ROWS = 4 * DEPTH


def _pack_small(g_pre, g_post, lb, g_head, sinks, loss=None):
    rows = []
    for l in range(DEPTH):
        tail = [g_head[l], sinks[l]]
        if loss is not None and l == 0:
            tail.append(jnp.reshape(loss, (1,)))
        tail = jnp.concatenate(tail)
        rows += [g_pre[l], g_post[l], lb[l], jnp.pad(tail, (0, D_MODEL - tail.shape[0]))]
    return jnp.stack(rows)


def _unpack_small(p):
    g_pre = jnp.stack([p[4 * l] for l in range(DEPTH)])
    g_post = jnp.stack([p[4 * l + 1] for l in range(DEPTH)])
    lb = jnp.stack([p[4 * l + 2] for l in range(DEPTH)])
    g_head = jnp.stack([p[4 * l + 3, :HG_HEAD_DIM] for l in range(DEPTH)])
    sinks = jnp.stack([p[4 * l + 3, HG_HEAD_DIM:HG_HEAD_DIM + ATT_HEADS] for l in range(DEPTH)])
    return g_pre, g_post, lb, g_head, sinks


def _small_update(gsum, w, m, v):
    def body(g_ref, w_ref, m_ref, v_ref, go_ref, d_ref, mo_ref, vo_ref):
        g = g_ref[...]
        w = w_ref[...]
        lbp = [w[4 * l + 2:4 * l + 3] for l in range(DEPTH)]
        mx = functools.reduce(jnp.maximum, lbp)
        e = [jnp.exp(t - mx) for t in lbp]
        tot = functools.reduce(jnp.add, e)
        p = [t / tot for t in e]
        glb = [g[4 * l + 2:4 * l + 3] for l in range(DEPTH)]
        row = lax.broadcasted_iota(jnp.int32, g.shape, 0)
        for j in range(DEPTH):
            gj = jnp.zeros_like(p[0])
            for l in range(DEPTH):
                for i in range(1, l + 1):
                    gj = gj + glb[l] * p[i] * ((1.0 if i == j else 0.0) - p[j])
            g = jnp.where(row == 4 * j + 2, gj, g)
        go_ref[...] = g
        d_ref[...], mo_ref[...], vo_ref[...] = _adamw_math(w, g, m_ref[...], v_ref[...])

    vm = pl.BlockSpec(memory_space=pltpu.VMEM)
    return pl.pallas_call(
        body, name="small_update", in_specs=[vm] * 4, out_specs=[vm] * 4,
        out_shape=[jax.ShapeDtypeStruct(gsum.shape, f32)] * 4,
    )(gsum, w, m, v)


def kernel(x, w_in, w_out, g_pre, g_post, lb_param, g_head, sinks, loss_target, m_w_in, m_w_out, m_g_pre, m_g_post, m_lb_param, m_g_head, m_sinks, v_w_in, v_w_out, v_g_pre, v_g_post, v_lb_param, v_g_head, v_sinks):
    B, S, _ = x.shape
    T = B * S
    L = DEPTH
    ri, ro = IN_WIDTH // 8, MIX_WIDTH // 8
    wt_all, wo_all = _gather_shards([_shard_transposed(w_in).reshape(L, 4, 2, ri, D_MODEL),
                                     _shard_placed(w_out).reshape(L, 4, 2, ro, D_MODEL)])

    loss, dx, gwt, gwo, ggpre, ggpost, glb, gghead, gsinks = _local_grads(
        x.reshape(T, D_MODEL), loss_target.reshape(T, D_MODEL), wt_all.reshape(L, IN_WIDTH, D_MODEL),
        wo_all.reshape(L, MIX_WIDTH, D_MODEL), g_pre, g_post, lb_param, g_head, sinks, B=B, S=S)

    parts = [gwt.reshape(L, 4, 2, ri, D_MODEL), gwo.reshape(L, 4, 2, ro, D_MODEL)]
    got = _pair_exchange(parts)
    sums = [_pair_add(p, r) for p, r in zip(parts, got)]
    recv = _chip_exchange(sums)
    gwt_mine, gwo_mine = _pair_share([_chip_sum(s, r) for s, r in zip(sums, recv)])

    grad_w_in, d_w_in, nm_w_in, nv_w_in = _adamw_t(w_in, gwt_mine.reshape(L, 2 * ri, D_MODEL), m_w_in, v_w_in)
    grad_w_out = gwo_mine.reshape(L, 2 * ro, D_MODEL)
    d_w_out, nm_w_out, nv_w_out = _adamw(w_out, grad_w_out, m_w_out, v_w_out)

    gsum = _all_sum_small(_pack_small(ggpre, ggpost, glb, gghead, gsinks, loss))
    gs, ds, ms, vs = _small_update(
        gsum, _pack_small(g_pre, g_post, lb_param, g_head, sinks),
        _pack_small(m_g_pre, m_g_post, m_lb_param, m_g_head, m_sinks),
        _pack_small(v_g_pre, v_g_post, v_lb_param, v_g_head, v_sinks))
    loss_all = gsum[3, HG_HEAD_DIM + ATT_HEADS]
    return (loss_all, dx.reshape(B, S, D_MODEL), grad_w_in, grad_w_out, *_unpack_small(gs),
            d_w_in, d_w_out, *_unpack_small(ds), nm_w_in, nm_w_out, *_unpack_small(ms),
            nv_w_in, nv_w_out, *_unpack_small(vs))
```

```python
import functools
import math

import jax
import jax.numpy as jnp
from jax import lax
from jax.experimental import pallas as pl
from jax.experimental.pallas import tpu as pltpu

f32 = jnp.float32
bf16 = jnp.bfloat16

D_MODEL = 1024
DEPTH = 2
HG_WIDTH = 1024
HG_HEAD_DIM = 128
HG_HEADS = 8
CHUNK = 64
SUB = 16
ATT_WIDTH = 1024
ATT_HEAD_DIM = 64
ATT_HEADS = 16
ATT_GROUP = 8
KV_WIDTH = 128
ATT_BLOCK = 128
ATT_SCALE = 1.0 / math.sqrt(ATT_HEAD_DIM)
ROPE_THETA = 10000.0
IN_WIDTH = 6400
MIX_WIDTH = 2048
NORM_EPS = 1e-6
NEG_INF = -1e30
LB_FLOOR = 1e-20
LANES = 128
VMEM_LIMIT = 48 * 1024 * 1024

ADAM_LR = 0.001
ADAM_B1 = 0.9
ADAM_B2 = 0.999
ADAM_EPS = 1e-08
ADAM_WD = 0.01
ADAM_STEP = 10

QA_BLK, ZA_BLK, KV_BLK = 4, 5, 24

NT = (((1,), (1,)), ((), ()))
TN = (((0,), (0,)), ((), ()))


def _dot(a, b, dims=None, precision=None):
    if dims is None:
        return jnp.dot(a, b, preferred_element_type=f32, precision=precision)
    return lax.dot_general(a, b, dims, preferred_element_type=f32, precision=precision)


def _sigmoid(x):
    return 1.0 / (1.0 + jnp.exp(-x))


def _params(*sem):
    return pltpu.CompilerParams(dimension_semantics=sem, vmem_limit_bytes=VMEM_LIMIT)


TAIL = IN_WIDTH - 5120


def _in_proj(x, g, wt, tail, l, *, tm=512):
    T = x.shape[0]
    tm = min(tm, T)
    nmain = 5120 // TAIL

    def body(x_ref, g_ref, w_ref, t_ref, p_ref, h_ref, hs):
        j = pl.program_id(1)

        @pl.when(j == 0)
        def _():
            xv = x_ref[...]
            r = lax.rsqrt(jnp.mean(xv * xv, axis=-1, keepdims=True) + NORM_EPS)
            hv = (xv * r * g_ref[...]).astype(bf16)
            hs[...] = hv
            h_ref[...] = hv

        @pl.when(j < nmain)
        def _():
            p_ref[...] = _dot(hs[...], w_ref[...], NT)

        @pl.when(j == nmain)
        def _():
            p_ref[...] = _dot(hs[...], t_ref[...], NT)

    return pl.pallas_call(
        body, name="in_proj", grid=(T // tm, nmain + 1),
        in_specs=[pl.BlockSpec((tm, D_MODEL), lambda i, j: (i, 0)),
                  pl.BlockSpec((1, D_MODEL), lambda i, j: (0, 0)),
                  pl.BlockSpec((None, TAIL, D_MODEL), lambda i, j: (l, jnp.minimum(j, nmain - 1), 0)),
                  pl.BlockSpec((None, TAIL, D_MODEL), lambda i, j: (l, 0, 0))],
        out_specs=[pl.BlockSpec((tm, TAIL), lambda i, j: (i, j)),
                   pl.BlockSpec((tm, D_MODEL), lambda i, j: (i, 0))],
        out_shape=[jax.ShapeDtypeStruct((T, IN_WIDTH), f32), jax.ShapeDtypeStruct((T, D_MODEL), bf16)],
        scratch_shapes=[pltpu.VMEM((tm, D_MODEL), bf16)],
        compiler_params=_params("parallel", "arbitrary"),
    )(x, g, wt, tail)


def _out_proj(ch, ca, wo, l, x, g, *, tm=512):
    T = x.shape[0]
    tm = min(tm, T)
    half = MIX_WIDTH // 2

    def body(ch_ref, ca_ref, wo_ref, x_ref, g_ref, xn_ref, y_ref):
        y = _dot(ch_ref[...], wo_ref[0:half, :]) + _dot(ca_ref[...], wo_ref[half:MIX_WIDTH, :])
        r = lax.rsqrt(jnp.mean(y * y, axis=-1, keepdims=True) + NORM_EPS)
        y_ref[...] = y
        xn_ref[...] = x_ref[...] + y * r * g_ref[...]

    row = lambda i: (i, 0)
    fixed = lambda i: (0, 0)
    return pl.pallas_call(
        body, name="out_proj", grid=(T // tm,),
        in_specs=[pl.BlockSpec((tm, half), row), pl.BlockSpec((tm, half), row),
                  pl.BlockSpec((None, MIX_WIDTH, D_MODEL), lambda i: (l, 0, 0)), pl.BlockSpec((tm, D_MODEL), row),
                  pl.BlockSpec((1, D_MODEL), fixed)],
        out_specs=[pl.BlockSpec((tm, D_MODEL), row), pl.BlockSpec((tm, D_MODEL), row)],
        out_shape=[jax.ShapeDtypeStruct((T, D_MODEL), f32)] * 2,
        compiler_params=_params("parallel"),
    )(ch, ca, wo, x, g)


def _loss_head(y, target, *, tm=512):
    T = y.shape[0]
    tm = min(tm, T)

    def body(y_ref, t_ref, d_ref, l_ref):
        @pl.when(pl.program_id(0) == 0)
        def _():
            l_ref[...] = jnp.zeros_like(l_ref)
        err = y_ref[...] - t_ref[...]
        d_ref[...] = err * (1.0 / D_MODEL)
        l_ref[...] += jnp.sum(err * err) * (0.5 / D_MODEL)

    row = lambda i: (i, 0)
    return pl.pallas_call(
        body, name="loss_head", grid=(T // tm,),
        in_specs=[pl.BlockSpec((tm, D_MODEL), row), pl.BlockSpec((tm, D_MODEL), row)],
        out_specs=[pl.BlockSpec((tm, D_MODEL), row), pl.BlockSpec((8, LANES), lambda i: (0, 0))],
        out_shape=[jax.ShapeDtypeStruct((T, D_MODEL), f32), jax.ShapeDtypeStruct((8, LANES), f32)],
        compiler_params=_params("arbitrary"),
    )(y, target)


def _out_proj_bwd(dxn, y, g, wo, l, ch, ca, *, tm=256):
    T = y.shape[0]
    tm = min(tm, T)
    half = MIX_WIDTH // 2

    def body(dx_ref, y_ref, g_ref, wo_ref, ch_ref, ca_ref, dch_ref, dca_ref, dwo_ref, dg_ref):
        @pl.when(pl.program_id(0) == 0)
        def _():
            dwo_ref[...] = jnp.zeros_like(dwo_ref)
            dg_ref[...] = jnp.zeros_like(dg_ref)
        y = y_ref[...]
        dx = dx_ref[...]
        r = lax.rsqrt(jnp.mean(y * y, axis=-1, keepdims=True) + NORM_EPS)
        gy = dx * g_ref[...]
        dy = r * gy - y * (r * r * r) * jnp.mean(gy * y, axis=-1, keepdims=True)
        dg_ref[...] += jnp.sum(dx * y * r, axis=0, keepdims=True)
        dyb = dy.astype(bf16)
        dch_ref[...] = _dot(dyb, wo_ref[0:half, :], NT)
        dca_ref[...] = _dot(dyb, wo_ref[half:MIX_WIDTH, :], NT)
        dwo_ref[0:half, :] += _dot(ch_ref[...], dyb, TN)
        dwo_ref[half:MIX_WIDTH, :] += _dot(ca_ref[...], dyb, TN)

    row = lambda i: (i, 0)
    fixed = lambda i: (0, 0)
    return pl.pallas_call(
        body, name="out_proj_bwd", grid=(T // tm,),
        in_specs=[pl.BlockSpec((tm, D_MODEL), row), pl.BlockSpec((tm, D_MODEL), row),
                  pl.BlockSpec((1, D_MODEL), fixed), pl.BlockSpec((None, MIX_WIDTH, D_MODEL), lambda i: (l, 0, 0)),
                  pl.BlockSpec((tm, half), row), pl.BlockSpec((tm, half), row)],
        out_specs=[pl.BlockSpec((tm, half), row), pl.BlockSpec((tm, half), row),
                   pl.BlockSpec((MIX_WIDTH, D_MODEL), fixed), pl.BlockSpec((1, D_MODEL), fixed)],
        out_shape=[jax.ShapeDtypeStruct((T, half), f32), jax.ShapeDtypeStruct((T, half), f32),
                   jax.ShapeDtypeStruct((MIX_WIDTH, D_MODEL), f32), jax.ShapeDtypeStruct((1, D_MODEL), f32)],
        compiler_params=_params("arbitrary"),
    )(dxn, y, g, wo, ch, ca)


TILE = 256
PIECE_TILES = (4, 4, 4, 4, 4, 1, 4)
PIECE_START = tuple(sum(PIECE_TILES[:p]) for p in range(len(PIECE_TILES)))
N_TILES = sum(PIECE_TILES)


def _piece_specs(rows, index):
    def spec(s, n):
        def index_map(*g):
            r, t = index(*g)
            return r, jnp.clip(t - s, 0, n - 1)
        return pl.BlockSpec((rows, TILE), index_map)
    return [spec(s, n) for s, n in zip(PIECE_START, PIECE_TILES)]


def _for_piece(t, fn):
    for p, (s, n) in enumerate(zip(PIECE_START, PIECE_TILES)):
        pl.when((t >= s) & (t < s + n))(functools.partial(fn, p))


def _in_proj_bwd(pieces, wt, l, x, g, dxn, *, tm=1024):
    T = x.shape[0]
    tm = min(tm, T)
    nk = N_TILES
    npc = len(pieces)

    def body(*refs):
        dp_refs = refs[:npc]
        w_ref, x_ref, g_ref, dxn_ref, dx_ref, dg_ref, acc = refs[npc:]
        i, k = pl.program_id(0), pl.program_id(1)

        @pl.when((i == 0) & (k == 0))
        def _():
            dg_ref[...] = jnp.zeros_like(dg_ref)

        @pl.when(k == 0)
        def _():
            acc[...] = jnp.zeros_like(acc)

        def add(p):
            acc[...] += _dot(dp_refs[p][...], w_ref[...])
        _for_piece(k, add)

        @pl.when(k == nk - 1)
        def _():
            dh = acc[...]
            xv = x_ref[...]
            r = lax.rsqrt(jnp.mean(xv * xv, axis=-1, keepdims=True) + NORM_EPS)
            gy = dh * g_ref[...]
            dx_ref[...] = dxn_ref[...] + r * gy - xv * (r * r * r) * jnp.mean(gy * xv, axis=-1, keepdims=True)
            dg_ref[...] += jnp.sum(dh * xv * r, axis=0, keepdims=True)

    return pl.pallas_call(
        body, name="in_proj_bwd", grid=(T // tm, nk),
        in_specs=_piece_specs(tm, lambda i, k: (i, k)) + [
            pl.BlockSpec((None, TILE, D_MODEL), lambda i, k: (l, k, 0)),
            pl.BlockSpec((tm, D_MODEL), lambda i, k: (i, 0)), pl.BlockSpec((1, D_MODEL), lambda i, k: (0, 0)),
            pl.BlockSpec((tm, D_MODEL), lambda i, k: (i, 0))],
        out_specs=[pl.BlockSpec((tm, D_MODEL), lambda i, k: (i, 0)), pl.BlockSpec((1, D_MODEL), lambda i, k: (0, 0))],
        out_shape=[jax.ShapeDtypeStruct((T, D_MODEL), f32), jax.ShapeDtypeStruct((1, D_MODEL), f32)],
        scratch_shapes=[pltpu.VMEM((tm, D_MODEL), f32)],
        compiler_params=_params("arbitrary", "arbitrary"),
    )(*pieces, wt, x, g, dxn)


def _grad_w_in(h, pieces):
    T = h.shape[0]
    npc = len(pieces)

    def body(*refs):
        h_ref, dp_refs, o_ref = refs[0], refs[1:1 + npc], refs[1 + npc]

        def put(p):
            o_ref[...] = _dot(dp_refs[p][...], h_ref[...], TN)
        _for_piece(pl.program_id(0), put)

    return pl.pallas_call(
        body, name="grad_w_in", grid=(N_TILES,),
        in_specs=[pl.BlockSpec((T, D_MODEL), lambda j: (0, 0), pipeline_mode=pl.Buffered(1))]
        + _piece_specs(T, lambda j: (0, j)),
        out_specs=pl.BlockSpec((TILE, D_MODEL), lambda j: (j, 0)),
        out_shape=jax.ShapeDtypeStruct((IN_WIDTH, D_MODEL), f32),
        compiler_params=_params("parallel"),
    )(h, *pieces)


def _lower_bound(lbp, layer):
    m = jnp.max(lbp, axis=0, keepdims=True)
    e = jnp.exp(lbp - m)
    p = e / jnp.sum(e, axis=0, keepdims=True)
    acc = p[0:1]
    for i in range(1, layer + 1):
        acc = acc + p[i:i + 1]
    return acc - p[0:1]


def _gate_parts(qr, fr, lb, lbf):
    sq = _sigmoid(qr)
    e = jnp.exp(-jnp.abs(fr))
    inv = 1.0 / (1.0 + e)
    pos = fr >= 0
    sg = jnp.where(pos, inv, e * inv)
    nsg = jnp.where(pos, e * inv, inv)
    fg = lbf + (1.0 - lb) * sg
    return qr * sq, sq, sg, nsg, fg, jnp.log(fg), (1.0 - lb) * nsg


def _anchor_masks(transposed=False):
    t = lax.broadcasted_iota(jnp.int32, (CHUNK, CHUNK), 1 if transposed else 0)
    s = lax.broadcasted_iota(jnp.int32, (CHUNK, CHUNK), 0 if transposed else 1)
    anchors = tuple(range(SUB - 1, CHUNK - 1, SUB))
    return anchors, [(t > a) & (s <= a) & (s > a - SUB) for a in anchors]


def _seg_sum(seg, x):
    hi = x.astype(bf16)
    return _dot(seg, hi) + _dot(seg, (x - hi.astype(f32)).astype(bf16))


def _hgrn_fwd(proj, lb_param, g_head, *, B, S, layer):
    T = B * S
    TB = min(256, S)
    nT, NC = S // TB, TB // CHUNK
    nC = S // CHUNK
    HD = HG_HEAD_DIM

    def body(q_ref, f_ref, i_ref, z_ref, lb_ref, gh_ref, cat_ref, op_ref, st_ref,
             s_scr, b_scr, k_scr):
        @pl.when(pl.program_id(2) == 0)
        def _():
            s_scr[...] = jnp.zeros_like(s_scr)
        lb = _lower_bound(lb_ref[...], layer)
        lbf = jnp.maximum(lb, LB_FLOOR)
        gh = gh_ref[...]
        r_i = lax.broadcasted_iota(jnp.int32, (CHUNK, CHUNK), 0)
        c_i = lax.broadcasted_iota(jnp.int32, (CHUNK, CHUNK), 1)
        tril = (r_i >= c_i).astype(f32)
        rows8 = lax.broadcasted_iota(jnp.int32, (8, HD), 0)
        lane_c = lax.broadcasted_iota(jnp.int32, (8, CHUNK), 1)
        anchors, masks = _anchor_masks()

        def chunk(c, st):
            rs = slice(c * CHUNK, (c + 1) * CHUNK)
            b_s, k_s = b_scr.at[c], k_scr.at[c]
            q, _, _, _, _, logf, k = _gate_parts(q_ref[rs, :], f_ref[rs, :], lb, lbf)
            v = i_ref[rs, :]
            b = _dot(tril, logf, precision=lax.Precision.HIGHEST)
            b_s[...] = b
            k_s[...] = k
            pieces = []
            for blk in range(CHUNK // SUB):
                r0 = blk * SUB
                bp = [b[r0 + 8 * i:r0 + 8 * i + 8] for i in range(SUB // 8)]
                qp = [q[r0 + 8 * i:r0 + 8 * i + 8] for i in range(SUB // 8)]
                ap = [jnp.zeros((8, CHUNK), f32) for _ in range(SUB // 8)]
                for s in range(SUB):
                    bs = b_s[r0 + s:r0 + s + 1, :]
                    ks = k_s[r0 + s:r0 + s + 1, :]
                    for i in range(s // 8, SUB // 8):
                        diff = bp[i] - bs
                        if i == s // 8:
                            diff = jnp.where(rows8 >= s - 8 * i, diff, NEG_INF)
                        col = jnp.sum(jnp.exp(diff) * qp[i] * ks, axis=1, keepdims=True)
                        ap[i] = jnp.where(lane_c == r0 + s, col, ap[i])
                pieces += ap
            a_all = jnp.concatenate(pieces, axis=0)
            for an, mk in zip(anchors, masks):
                beta = b_s[an:an + 1, :]
                qh = (q * jnp.exp(jnp.minimum(b - beta, 0.0))).astype(bf16)
                kh = (k * jnp.exp(jnp.minimum(beta - b, 0.0))).astype(bf16)
                a_all = a_all + jnp.where(mk, _dot(qh, kh, NT), 0.0)
            st_ref[0, 0, c] = st
            vb16 = v.astype(bf16)
            o = _dot(a_all.astype(bf16), vb16) + _dot((q * jnp.exp(b)).astype(bf16), st.astype(bf16), NT)
            b_end = b_s[CHUNK - 1:CHUNK, :]
            kdec = (k * jnp.exp(b_end - b)).astype(bf16)
            st_next = jnp.exp(b_end) * st + _dot(vb16, kdec, TN)
            rr = lax.rsqrt(jnp.mean(o * o, axis=-1, keepdims=True) + NORM_EPS)
            zr = z_ref[rs, :]
            cat_ref[rs, :] = (o * rr * gh * (zr * _sigmoid(zr))).astype(bf16)
            op_ref[rs, :] = o
            return st_next

        st = s_scr[...]
        for c in range(NC):
            st = chunk(c, st)
        s_scr[...] = st

    def col(part):
        return pl.BlockSpec((TB, HD), lambda b, h, n: (b * nT + n, part * HG_HEADS + h))

    out_col = pl.BlockSpec((TB, HD), lambda b, h, n: (b * nT + n, h))
    return pl.pallas_call(
        body, name=f"hgrn_fwd_l{layer}", grid=(B, HG_HEADS, nT),
        in_specs=[col(0), col(1), col(2), col(3),
                  pl.BlockSpec((DEPTH, HD), lambda b, h, n: (0, h)),
                  pl.BlockSpec((1, HD), lambda b, h, n: (0, 0))],
        out_specs=[out_col, out_col,
                   pl.BlockSpec((1, 1, NC, HD, HD), lambda b, h, n: (b, h, n, 0, 0))],
        out_shape=[jax.ShapeDtypeStruct((T, HG_WIDTH), bf16), jax.ShapeDtypeStruct((T, HG_WIDTH), f32),
                   jax.ShapeDtypeStruct((B, HG_HEADS, nC, HD, HD), f32)],
        scratch_shapes=[pltpu.VMEM((HD, HD), f32), pltpu.VMEM((NC, CHUNK, HD), f32), pltpu.VMEM((NC, CHUNK, HD), f32)],
        compiler_params=_params("parallel", "parallel", "arbitrary"),
    )(proj, proj, proj, proj, lb_param, g_head)


def _hgrn_bwd(proj, lb_param, g_head, o_pre, states, dcat, *, B, S, layer):
    T = B * S
    TB = min(256, S)
    nT, NC = S // TB, TB // CHUNK
    HD = HG_HEAD_DIM

    def body(q_ref, f_ref, i_ref, z_ref, lb_ref, gh_ref, op_ref, st_ref, dc_ref,
             dq_ref, df_ref, di_ref, dz_ref, dlb_ref, dgh_ref,
             ds_scr, b_scr, q_scr, do_scr, wk_scr):
        @pl.when(pl.program_id(2) == 0)
        def _():
            ds_scr[...] = jnp.zeros_like(ds_scr)
            dlb_ref[...] = jnp.zeros_like(dlb_ref)
            dgh_ref[...] = jnp.zeros_like(dgh_ref)
        lb = _lower_bound(lb_ref[...], layer)
        lbf = jnp.maximum(lb, LB_FLOOR)
        ind = (lb > LB_FLOOR).astype(f32)
        gh = gh_ref[...]
        r_i = lax.broadcasted_iota(jnp.int32, (CHUNK, CHUNK), 0)
        c_i = lax.broadcasted_iota(jnp.int32, (CHUNK, CHUNK), 1)
        tril = (r_i >= c_i).astype(f32)
        triu = (c_i >= r_i).astype(f32)
        rows8 = lax.broadcasted_iota(jnp.int32, (8, HD), 0)
        lane_c = lax.broadcasted_iota(jnp.int32, (8, CHUNK), 1)
        last_row = lax.broadcasted_iota(jnp.int32, (CHUNK, HD), 0) == CHUNK - 1
        anchors, masks = _anchor_masks()
        _, masks_t = _anchor_masks(transposed=True)
        seg_t = lax.broadcasted_iota(jnp.int32, (SUB, 8 * SUB), 0)
        seg_r = lax.broadcasted_iota(jnp.int32, (SUB, 8 * SUB), 1) // 8
        seg0 = (seg_r == seg_t).astype(bf16)
        seg1 = (seg_r[:, 0:4 * SUB] + 8 == seg_t[:, 0:4 * SUB]).astype(bf16)

        def chunk(c, dst1):
            rs = slice(c * CHUNK, (c + 1) * CHUNK)
            b_s, q_s, do_s = b_scr.at[c], q_scr.at[c], do_scr.at[c]
            qr, fr = q_ref[rs, :], f_ref[rs, :]
            q, sq, sg, nsg, fg, logf, k = _gate_parts(qr, fr, lb, lbf)
            v = i_ref[rs, :]
            b = _dot(tril, logf, precision=lax.Precision.HIGHEST)
            o = op_ref[rs, :]
            dc = dc_ref[rs, :]
            zr = z_ref[rs, :]
            sz = _sigmoid(zr)
            rr = lax.rsqrt(jnp.mean(o * o, axis=-1, keepdims=True) + NORM_EPS)
            dz_ref[rs, :] = (dc * (o * rr * gh) * (sz * (1.0 + zr * (1.0 - sz)))).astype(bf16)
            dn = dc * (zr * sz)
            dgh_ref[0, 0] += jnp.sum(dn * o * rr, axis=0, keepdims=True)
            gdn = dn * gh
            d_o = rr * gdn - o * (rr * rr * rr) * jnp.mean(gdn * o, axis=-1, keepdims=True)
            b_s[...] = b
            q_s[...] = q
            do_s[...] = d_o
            dob = d_o.astype(bf16)
            vb16 = v.astype(bf16)
            d_a = _dot(dob, vb16, NT)
            d_q = jnp.zeros((CHUNK, HD), f32)
            d_k = jnp.zeros((CHUNK, HD), f32)
            at_all = jnp.zeros((CHUNK, CHUNK), f32)
            for an, mk, mkt in zip(anchors, masks, masks_t):
                beta = b_s[an:an + 1, :]
                eq = jnp.exp(jnp.minimum(b - beta, 0.0))
                ek = jnp.exp(jnp.minimum(beta - b, 0.0))
                qh = (q * eq).astype(bf16)
                kh = (k * ek).astype(bf16)
                at_all = at_all + jnp.where(mkt, _dot(kh, qh, NT), 0.0)
                d_aa = jnp.where(mk, d_a, 0.0).astype(bf16)
                d_q = d_q + _dot(d_aa, kh) * eq
                d_k = d_k + _dot(d_aa, qh, TN) * ek
            st0 = st_ref[0, 0, c]
            dst1b = dst1.astype(bf16)
            eb = jnp.exp(b)
            b_end = b_s[CHUNK - 1:CHUNK, :]
            edec = jnp.exp(b_end - b)
            e_end = jnp.exp(b_end)
            kdec = (k * edec).astype(bf16)
            qdec = (q * eb).astype(bf16)
            d_q = d_q + _dot(dob, st0.astype(bf16)) * eb
            d_v = _dot(kdec, dst1b, NT)
            d_k = d_k + _dot(vb16, dst1b) * edec
            st1 = e_end * st0 + _dot(vb16, kdec, TN)
            rterm = jnp.sum(dst1 * st1, axis=0, keepdims=True)
            dst0 = e_end * dst1 + _dot(dob, qdec, TN)
            dq_blocks, dk_pieces, at_pieces = [], [], []
            for blk in range(CHUNK // SUB):
                r0 = blk * SUB
                wk = wk_scr.at[c * (CHUNK // SUB) + blk]
                bp = [b[r0 + 8 * i:r0 + 8 * i + 8] for i in range(SUB // 8)]
                kp = [k[r0 + 8 * i:r0 + 8 * i + 8] for i in range(SUB // 8)]
                vp = [v[r0 + 8 * i:r0 + 8 * i + 8] for i in range(SUB // 8)]
                dkp = [jnp.zeros((8, HD), f32) for _ in range(SUB // 8)]
                atp = [jnp.zeros((8, CHUNK), f32) for _ in range(SUB // 8)]
                for t in range(SUB):
                    bt = b_s[r0 + t:r0 + t + 1, :]
                    qt = q_s[r0 + t:r0 + t + 1, :]
                    dot_ = do_s[r0 + t:r0 + t + 1, :]
                    for i in range(t // 8 + 1):
                        diff = bt - bp[i]
                        if i == t // 8:
                            diff = jnp.where(rows8 <= t - 8 * i, diff, NEG_INF)
                        e = jnp.exp(diff)
                        a = jnp.sum(e * kp[i] * qt, axis=1, keepdims=True)
                        atp[i] = jnp.where(lane_c == r0 + t, a, atp[i])
                        w = jnp.sum(vp[i] * dot_, axis=1, keepdims=True) * e
                        dkp[i] = dkp[i] + w * qt
                        row = 8 * t if i == 0 else 8 * SUB + 8 * (t - 8)
                        wk[row:row + 8, :] = w * kp[i]
                dq_blocks.append(_seg_sum(seg0, wk[0:8 * SUB, :]) + _seg_sum(seg1, wk[8 * SUB:12 * SUB, :]))
                dk_pieces += dkp
                at_pieces += atp
            d_q = d_q + jnp.concatenate(dq_blocks, axis=0)
            d_k = d_k + jnp.concatenate(dk_pieces, axis=0)
            d_v = d_v + _dot((at_all + jnp.concatenate(at_pieces, axis=0)).astype(bf16), dob)
            db = q * d_q - k * d_k + jnp.where(last_row, rterm, 0.0)
            dlt = _dot(triu, db, precision=lax.Precision.HIGHEST) - fg * d_k
            df_ref[rs, :] = (dlt * (1.0 - lb) * sg * nsg / fg).astype(bf16)
            dlb_ref[0] += jnp.sum(dlt * (ind - sg) / fg, axis=0, keepdims=True)
            dq_ref[rs, :] = (d_q * (sq * (1.0 + qr * (1.0 - sq)))).astype(bf16)
            di_ref[rs, :] = d_v.astype(bf16)
            return dst0

        dst = ds_scr[...]
        for c in reversed(range(NC)):
            dst = chunk(c, dst)
        ds_scr[...] = dst

    def col(part):
        return pl.BlockSpec((TB, HD), lambda b, h, n: (b * nT + nT - 1 - n, part * HG_HEADS + h))

    hcol = pl.BlockSpec((TB, HD), lambda b, h, n: (b * nT + nT - 1 - n, h))
    return pl.pallas_call(
        body, name=f"hgrn_bwd_l{layer}", grid=(B, HG_HEADS, nT),
        in_specs=[col(0), col(1), col(2), col(3),
                  pl.BlockSpec((DEPTH, HD), lambda b, h, n: (0, h)),
                  pl.BlockSpec((1, HD), lambda b, h, n: (0, 0)),
                  hcol,
                  pl.BlockSpec((1, 1, NC, HD, HD), lambda b, h, n: (b, h, nT - 1 - n, 0, 0)),
                  hcol],
        out_specs=[hcol, hcol, hcol, hcol,
                   pl.BlockSpec((1, 1, HD), lambda b, h, n: (b, 0, h)),
                   pl.BlockSpec((1, 1, 1, HD), lambda b, h, n: (b, h, 0, 0))],
        out_shape=[jax.ShapeDtypeStruct((T, HG_WIDTH), bf16)] * 4 + [
            jax.ShapeDtypeStruct((B, 1, HG_WIDTH), f32), jax.ShapeDtypeStruct((B, HG_HEADS, 1, HD), f32)],
        scratch_shapes=[pltpu.VMEM((HD, HD), f32)] + [pltpu.VMEM((NC, CHUNK, HD), f32)] * 3
        + [pltpu.VMEM((NC * CHUNK // SUB, 12 * SUB, HD), f32)],
        compiler_params=_params("parallel", "parallel", "arbitrary"),
    )(proj, proj, proj, proj, lb_param, g_head, o_pre, states, dcat)


def _rope_tables(S):
    half = ATT_HEAD_DIM // 2
    inv_freq = ROPE_THETA ** (-jnp.arange(half, dtype=f32) / half)
    ang = jnp.arange(S, dtype=f32)[:, None] * inv_freq[None, :]
    cos, sin = jnp.cos(ang), jnp.sin(ang)
    return jnp.tile(jnp.concatenate([cos, cos], axis=1), (1, 2)), jnp.tile(jnp.concatenate([-sin, sin], axis=1), (1, 2))


def _swap_halves(x, first_half):
    return jnp.where(first_half, pltpu.roll(x, LANES - ATT_HEAD_DIM // 2, 1), pltpu.roll(x, ATT_HEAD_DIM // 2, 1))


def _rope(x, cos, sin, first_half):
    return x * cos + _swap_halves(x, first_half) * sin


def _rope_bwd(dy, cos, sin, first_half):
    return dy * cos + _swap_halves(dy * sin, first_half)


def _attn_consts(n):
    lane = lax.broadcasted_iota(jnp.int32, (1, LANES), 1)
    low = lane < ATT_HEAD_DIM
    first_half = (lane % ATT_HEAD_DIM) < ATT_HEAD_DIM // 2
    top = lax.broadcasted_iota(jnp.int32, (LANES, 1), 0) < ATT_HEAD_DIM
    s = lax.broadcasted_iota(jnp.int32, (2 * ATT_BLOCK, ATT_BLOCK), 0)
    t = lax.broadcasted_iota(jnp.int32, (2 * ATT_BLOCK, ATT_BLOCK), 1)
    mask = (s > t) & (s <= t + ATT_BLOCK) & ((s >= ATT_BLOCK) | (n > 0))
    return low, first_half, top, mask


def _dup_kv(x, low):
    rolled = pltpu.roll(x, ATT_HEAD_DIM, 1)
    return [jnp.where(low, x, rolled), jnp.where(low, rolled, x)]


def _attn_head(qtm, kd, vdt, sink, mask):
    s = jnp.where(mask, _dot(kd, qtm) * ATT_SCALE, NEG_INF)
    m = jnp.maximum(jnp.max(s, axis=0, keepdims=True), sink)
    p = jnp.exp(s - m)
    psink = jnp.exp(sink - m)
    inv = 1.0 / (jnp.sum(p, axis=0, keepdims=True) + psink)
    pn = p * inv
    return pn, psink * inv, _dot(vdt, pn.astype(bf16))


def _swa_fwd(proj, sink_b, cos, sin, *, B, S):
    T = B * S
    L = ATT_BLOCK
    nB = S // L

    def body(q_ref, z_ref, kvc_ref, kvp_ref, sk_ref, cc_ref, sc_ref, cp_ref, sp_ref, cat_ref):
        n = pl.program_id(1)
        low, first_half, top, mask = _attn_consts(n)
        cc, sc = cc_ref[...], sc_ref[...]
        kc = _rope(kvc_ref[:, 0:LANES], cc, sc, first_half)
        kp = _rope(kvp_ref[:, 0:LANES], cp_ref[...], sp_ref[...], first_half)
        kd = [x.astype(bf16) for x in _dup_kv(jnp.concatenate([kp, kc], axis=0), low)]
        vdt = [x.T.astype(bf16) for x in _dup_kv(jnp.concatenate([kvp_ref[:, LANES:2 * LANES], kvc_ref[:, LANES:2 * LANES]], axis=0), low)]
        for pair in range(ATT_HEADS // 2):
            cols = slice(pair * LANES, (pair + 1) * LANES)
            j = (2 * pair) // ATT_GROUP
            qt = _rope(q_ref[:, cols], cc, sc, first_half).T
            outs = []
            for hh in range(2):
                h = 2 * pair + hh
                qtm = jnp.where(top if hh == 0 else ~top, qt, 0.0).astype(bf16)
                _, _, o = _attn_head(qtm, kd[j], vdt[j], sk_ref[h:h + 1, 0:1], mask)
                outs.append(o)
            zp = z_ref[:, cols]
            cat_ref[:, cols] = (jnp.where(top, outs[0], outs[1]).T * (zp * _sigmoid(zp))).astype(bf16)

    cur = lambda b, n: (b * nB + n, 0)
    prev = lambda b, n: (b * nB + jnp.maximum(n - 1, 0), 0)
    return pl.pallas_call(
        body, name="swa_fwd", grid=(B, nB),
        in_specs=[pl.BlockSpec((L, ATT_WIDTH), lambda b, n: (b * nB + n, QA_BLK)),
                  pl.BlockSpec((L, ATT_WIDTH), lambda b, n: (b * nB + n, ZA_BLK)),
                  pl.BlockSpec((L, 2 * KV_WIDTH), lambda b, n: (b * nB + n, KV_BLK)),
                  pl.BlockSpec((L, 2 * KV_WIDTH), lambda b, n: (b * nB + jnp.maximum(n - 1, 0), KV_BLK)),
                  pl.BlockSpec((ATT_HEADS, LANES), lambda b, n: (0, 0)),
                  pl.BlockSpec((L, LANES), lambda b, n: (n, 0)), pl.BlockSpec((L, LANES), lambda b, n: (n, 0)),
                  pl.BlockSpec((L, LANES), lambda b, n: (jnp.maximum(n - 1, 0), 0)),
                  pl.BlockSpec((L, LANES), lambda b, n: (jnp.maximum(n - 1, 0), 0))],
        out_specs=pl.BlockSpec((L, ATT_WIDTH), cur),
        out_shape=jax.ShapeDtypeStruct((T, ATT_WIDTH), bf16),
        compiler_params=_params("parallel", "parallel"),
    )(proj, proj, proj, proj, sink_b, cos, sin, cos, sin)


def _swa_bwd(proj, sink_b, cos, sin, dcat, *, B, S):
    T = B * S
    L = ATT_BLOCK
    nB = S // L

    def body(q_ref, z_ref, kvc_ref, kvp_ref, sk_ref, cc_ref, sc_ref, cp_ref, sp_ref, dc_ref,
             dq_ref, dz_ref, dkv_ref, dsk_ref, carry, ds_st, pn_st, q_st, do_st):
        step = pl.program_id(1)
        n = nB - 1 - step

        @pl.when((pl.program_id(0) == 0) & (step == 0))
        def _():
            dsk_ref[...] = jnp.zeros_like(dsk_ref)

        @pl.when(step == 0)
        def _():
            carry[...] = jnp.zeros_like(carry)
        low, first_half, top, mask = _attn_consts(n)
        cc, sc, cp, sp = cc_ref[...], sc_ref[...], cp_ref[...], sp_ref[...]
        kc = _rope(kvc_ref[:, 0:LANES], cc, sc, first_half)
        kp = _rope(kvp_ref[:, 0:LANES], cp, sp, first_half)
        kdf = _dup_kv(jnp.concatenate([kp, kc], axis=0), low)
        vdf = _dup_kv(jnp.concatenate([kvp_ref[:, LANES:2 * LANES], kvc_ref[:, LANES:2 * LANES]], axis=0), low)
        kd = [x.astype(bf16) for x in kdf]
        vd = [x.astype(bf16) for x in vdf]
        kdt = [x.T.astype(bf16) for x in kdf]
        vdt = [x.T.astype(bf16) for x in vdf]
        dkd, dvd = [], []
        for pair in range(ATT_HEADS // 2):
            cols = slice(pair * LANES, (pair + 1) * LANES)
            j = (2 * pair) // ATT_GROUP
            qp = _rope(q_ref[:, cols], cc, sc, first_half)
            qt = qp.T
            zp = z_ref[:, cols]
            dc = dc_ref[:, cols]
            sz = _sigmoid(zp)
            d_o = dc * (zp * sz)
            dot_ = d_o.T
            res = []
            for hh in range(2):
                rsel = top if hh == 0 else ~top
                qtm = jnp.where(rsel, qt, 0.0).astype(bf16)
                pn, psn, o = _attn_head(qtm, kd[j], vdt[j], sk_ref[2 * pair + hh:2 * pair + hh + 1, 0:1], mask)
                res.append((rsel, pn, psn, o))
            ot = jnp.where(top, res[0][3], res[1][3])
            dz_ref[:, cols] = (dc * ot.T * (sz * (1.0 + zp * (1.0 - sz)))).astype(bf16)
            dqts = []
            for hh in range(2):
                h = 2 * pair + hh
                rsel, pn, psn, _ = res[hh]
                lsel = low if hh == 0 else ~low
                dotm = jnp.where(rsel, dot_, 0.0)
                delta = jnp.sum(dotm * ot, axis=0, keepdims=True)
                dst = (pn * (_dot(vd[j], dotm.astype(bf16)) - delta) * ATT_SCALE).astype(bf16)
                dsk_ref[h:h + 1, :] += jnp.zeros((1, LANES), f32) - jnp.sum(psn * delta)
                dqts.append(_dot(kdt[j], dst))
                g = h % ATT_GROUP
                ds_st[:, g * LANES:(g + 1) * LANES] = dst
                pn_st[:, g * LANES:(g + 1) * LANES] = pn.astype(bf16)
                q_st[g * LANES:(g + 1) * LANES, :] = jnp.where(lsel, qp, 0.0).astype(bf16)
                do_st[g * LANES:(g + 1) * LANES, :] = jnp.where(lsel, d_o, 0.0).astype(bf16)
            dq_ref[:, cols] = _rope_bwd(jnp.where(top, dqts[0], dqts[1]).T, cc, sc, first_half).astype(bf16)
            if (2 * pair + 2) % ATT_GROUP == 0:
                dkd.append(_dot(ds_st[...], q_st[...]))
                dvd.append(_dot(pn_st[...], do_st[...]))
        dk = [x + pltpu.roll(x, ATT_HEAD_DIM, 1) for x in dkd]
        dv = [x + pltpu.roll(x, ATT_HEAD_DIM, 1) for x in dvd]
        dk = jnp.where(low, dk[0], dk[1])
        dv = jnp.where(low, dv[0], dv[1])
        dkv_ref[:, 0:LANES] = (_rope_bwd(dk[L:2 * L], cc, sc, first_half) + carry[:, 0:LANES]).astype(bf16)
        dkv_ref[:, LANES:2 * LANES] = (dv[L:2 * L] + carry[:, LANES:2 * LANES]).astype(bf16)
        carry[:, 0:LANES] = _rope_bwd(dk[0:L], cp, sp, first_half)
        carry[:, LANES:2 * LANES] = dv[0:L]

    rev = lambda b, s: b * nB + nB - 1 - s
    revp = lambda b, s: b * nB + jnp.maximum(nB - 2 - s, 0)
    wide = lambda blk: pl.BlockSpec((L, ATT_WIDTH), lambda b, s: (rev(b, s), blk))
    tab = pl.BlockSpec((L, LANES), lambda b, s: (nB - 1 - s, 0))
    tabp = pl.BlockSpec((L, LANES), lambda b, s: (jnp.maximum(nB - 2 - s, 0), 0))
    return pl.pallas_call(
        body, name="swa_bwd", grid=(B, nB),
        in_specs=[wide(QA_BLK), wide(ZA_BLK),
                  pl.BlockSpec((L, 2 * KV_WIDTH), lambda b, s: (rev(b, s), KV_BLK)),
                  pl.BlockSpec((L, 2 * KV_WIDTH), lambda b, s: (revp(b, s), KV_BLK)),
                  pl.BlockSpec((ATT_HEADS, LANES), lambda b, s: (0, 0)),
                  tab, tab, tabp, tabp, wide(0)],
        out_specs=[wide(0), wide(0), pl.BlockSpec((L, 2 * KV_WIDTH), lambda b, s: (rev(b, s), 0)),
                   pl.BlockSpec((ATT_HEADS, LANES), lambda b, s: (0, 0))],
        out_shape=[jax.ShapeDtypeStruct((T, ATT_WIDTH), bf16), jax.ShapeDtypeStruct((T, ATT_WIDTH), bf16),
                   jax.ShapeDtypeStruct((T, 2 * KV_WIDTH), bf16), jax.ShapeDtypeStruct((ATT_HEADS, LANES), f32)],
        scratch_shapes=[pltpu.VMEM((L, 2 * KV_WIDTH), f32),
                        pltpu.VMEM((2 * L, ATT_GROUP * LANES), bf16), pltpu.VMEM((2 * L, ATT_GROUP * LANES), bf16),
                        pltpu.VMEM((ATT_GROUP * LANES, LANES), bf16), pltpu.VMEM((ATT_GROUP * LANES, LANES), bf16)],
        compiler_params=_params("arbitrary", "arbitrary"),
    )(proj, proj, proj, proj, sink_b, cos, sin, cos, sin, dcat)


def _local_grads(x, target, wt, w_out_b, g_pre, g_post, lb_param, g_head, sinks, *, B, S):
    cos, sin = _rope_tables(S)
    tail = jnp.concatenate([wt[:, 5376:6400], wt[:, 5120:5376]], axis=1)
    saved = []
    for l in range(DEPTH):
        proj, h = _in_proj(x, g_pre[l:l + 1], wt, tail, l)
        ch, o_pre, states = _hgrn_fwd(proj, lb_param, g_head[l:l + 1], B=B, S=S, layer=l)
        sink_b = jnp.broadcast_to(sinks[l][:, None], (ATT_HEADS, LANES))
        ca = _swa_fwd(proj, sink_b, cos, sin, B=B, S=S)
        xn, y = _out_proj(ch, ca, w_out_b, l, x, g_post[l:l + 1])
        saved.append((x, proj, h, ch, o_pre, states, sink_b, ca, y))
        x = xn
    dx, loss = _loss_head(x, target)
    gw_in, gw_out, gg_pre, gg_post, g_lb, gg_head, g_sinks = [], [], [], [], [], [], []
    for l in reversed(range(DEPTH)):
        x_in, proj, h, ch, o_pre, states, sink_b, ca, y = saved[l]
        dch, dca, dwo, dgpost = _out_proj_bwd(dx, y, g_post[l:l + 1], w_out_b, l, ch, ca)
        dq, df, di, dz, dlb, dgh = _hgrn_bwd(proj, lb_param, g_head[l:l + 1], o_pre, states, dch, B=B, S=S, layer=l)
        dqa, dza, dkv, dsk = _swa_bwd(proj, sink_b, cos, sin, dca, B=B, S=S)
        pieces = [dq, df, di, dz, dqa, dkv, dza]
        dx, dgpre = _in_proj_bwd(pieces, wt, l, x_in, g_pre[l:l + 1], dx)
        gw_in.append(_grad_w_in(h, pieces))
        gw_out.append(dwo)
        gg_pre.append(dgpre[0])
        gg_post.append(dgpost[0])
        g_lb.append(jnp.sum(dlb, axis=(0, 1)))
        gg_head.append(jnp.sum(dgh, axis=(0, 1, 2)))
        g_sinks.append(dsk[:, 0])
    rev = lambda xs: jnp.stack(xs[::-1])
    return (loss[0, 0], dx, gw_in[::-1], gw_out[::-1], rev(gg_pre), rev(gg_post), rev(g_lb), rev(gg_head), rev(g_sinks))


MESH = pl.DeviceIdType.MESH
ANY = pl.BlockSpec(memory_space=pl.ANY)


def _place():
    x, y, c = lax.axis_index("x"), lax.axis_index("y"), lax.axis_index("c")
    return x, y, c, [(1 - x, y), (x, 1 - y), (1 - x, 1 - y)]


def _rcopy(src, dst, send, recv, k, to):
    return pltpu.make_async_remote_copy(src_ref=src, dst_ref=dst, send_sem=send.at[k], recv_sem=recv.at[k],
                                        device_id=to, device_id_type=MESH)


def _gather_shards(shards):
    n = len(shards)

    def body(*refs):
        outs = refs[n:2 * n]
        send, recv = refs[2 * n:]
        x, y, c, chips = _place()
        me = 2 * x + y
        sib = (x, y, 1 - c)
        first = []
        for j, chip in enumerate(chips):
            for a in range(n):
                mine = outs[a].at[:, me, c]
                first.append(_rcopy(mine, mine, send, recv, j * n + a, (*chip, c)))
        for cp in first:
            cp.start()
        passed = []
        for j, (px, py) in enumerate(chips):
            for a in range(n):
                blk = outs[a].at[:, 2 * px + py, c]
                _rcopy(blk, blk, send, recv, j * n + a, sib).wait_recv()
                cp = _rcopy(blk, blk, send, recv, (3 + j) * n + a, sib)
                cp.start()
                passed.append(cp)
        for j, (px, py) in enumerate(chips):
            for a in range(n):
                blk = outs[a].at[:, 2 * px + py, 1 - c]
                _rcopy(blk, blk, send, recv, (3 + j) * n + a, sib).wait_recv()
        for cp in first + passed:
            cp.wait_send()

    return pl.pallas_call(
        body, name="gather_shards",
        in_specs=[ANY] * n, out_specs=[ANY] * n,
        out_shape=[jax.ShapeDtypeStruct(s.shape, s.dtype) for s in shards],
        input_output_aliases={a: a for a in range(n)},
        scratch_shapes=[pltpu.SemaphoreType.DMA((6 * n,)), pltpu.SemaphoreType.DMA((6 * n,))],
    )(*shards)


def _pair_exchange(parts):
    n = len(parts)

    def body(*refs):
        ins, outs = refs[:n], refs[n:2 * n]
        send, recv = refs[2 * n:]
        x, y, c, _ = _place()
        cps = [_rcopy(ins[a].at[:, :, 1 - c], outs[a], send, recv, a, (x, y, 1 - c)) for a in range(n)]
        for cp in cps:
            cp.start()
        for cp in cps:
            cp.wait()

    return pl.pallas_call(
        body, name="pair_exchange", in_specs=[ANY] * n, out_specs=[ANY] * n,
        out_shape=[jax.ShapeDtypeStruct(p.shape[:2] + p.shape[3:], p.dtype) for p in parts],
        scratch_shapes=[pltpu.SemaphoreType.DMA((n,)), pltpu.SemaphoreType.DMA((n,))],
    )(*parts)


def _block_rows(r):
    return r if r <= 512 else r // 2


def _pair_add(part, got):
    L, K, _, r, C = part.shape
    rows = _block_rows(r)

    def body(c_ref, a_ref, b_ref, o_ref):
        o_ref[0, 0] = (a_ref[0, 0, 0] + b_ref[0, 0]).astype(bf16)

    blk = (1, 1, rows, C)
    return pl.pallas_call(
        body, name="pair_add",
        grid_spec=pltpu.PrefetchScalarGridSpec(
            num_scalar_prefetch=1, grid=(L, K, r // rows),
            in_specs=[pl.BlockSpec((1, 1, 1, rows, C), lambda l, k, i, c: (l, k, c[0], i, 0)),
                      pl.BlockSpec(blk, lambda l, k, i, c: (l, k, i, 0))],
            out_specs=pl.BlockSpec(blk, lambda l, k, i, c: (l, k, i, 0))),
        out_shape=jax.ShapeDtypeStruct((L, K, r, C), bf16),
        compiler_params=_params("parallel", "parallel", "parallel"),
    )(jnp.reshape(lax.axis_index("c"), (1,)).astype(jnp.int32), part, got)


def _chip_exchange(sums):
    n = len(sums)

    def body(*refs):
        ins, outs = refs[:n], refs[n:2 * n]
        send, recv = refs[2 * n:]
        x, y, c, chips = _place()
        cps = []
        for j, (px, py) in enumerate(chips):
            for a in range(n):
                cps.append(_rcopy(ins[a].at[:, 2 * px + py], outs[a].at[j], send, recv, j * n + a, (px, py, c)))
        for cp in cps:
            cp.start()
        for cp in cps:
            cp.wait()

    return pl.pallas_call(
        body, name="chip_exchange", in_specs=[ANY] * n, out_specs=[ANY] * n,
        out_shape=[jax.ShapeDtypeStruct((3, s.shape[0]) + s.shape[2:], s.dtype) for s in sums],
        scratch_shapes=[pltpu.SemaphoreType.DMA((3 * n,)), pltpu.SemaphoreType.DMA((3 * n,))],
    )(*sums)


def _chip_sum(mine, got):
    L, K, r, C = mine.shape
    rows = _block_rows(r)

    def body(p_ref, a_ref, b_ref, o_ref):
        o_ref[0, 0] = (a_ref[0, 0].astype(f32) + b_ref[0, 0].astype(f32)) + (b_ref[1, 0].astype(f32) + b_ref[2, 0].astype(f32))

    place = jnp.stack([2 * lax.axis_index("x") + lax.axis_index("y"), lax.axis_index("c")]).astype(jnp.int32)
    return pl.pallas_call(
        body, name="chip_sum",
        grid_spec=pltpu.PrefetchScalarGridSpec(
            num_scalar_prefetch=1, grid=(L, r // rows),
            in_specs=[pl.BlockSpec((1, 1, rows, C), lambda l, i, p: (l, p[0], i, 0)),
                      pl.BlockSpec((3, 1, rows, C), lambda l, i, p: (0, l, i, 0))],
            out_specs=pl.BlockSpec((1, 1, rows, C), lambda l, i, p: (l, p[1], i, 0))),
        out_shape=jax.ShapeDtypeStruct((L, 2, r, C), f32),
        compiler_params=_params("parallel", "parallel"),
    )(place, mine, got)


def _pair_share(bufs):
    n = len(bufs)

    def body(*refs):
        outs = refs[n:2 * n]
        send, recv = refs[2 * n:]
        x, y, c, _ = _place()
        cps = [_rcopy(outs[a].at[:, c], outs[a].at[:, c], send, recv, a, (x, y, 1 - c)) for a in range(n)]
        for cp in cps:
            cp.start()
        for a in range(n):
            theirs = outs[a].at[:, 1 - c]
            _rcopy(theirs, theirs, send, recv, a, (x, y, 1 - c)).wait_recv()
        for cp in cps:
            cp.wait_send()

    return pl.pallas_call(
        body, name="pair_share", in_specs=[ANY] * n, out_specs=[ANY] * n,
        out_shape=[jax.ShapeDtypeStruct(b.shape, b.dtype) for b in bufs],
        input_output_aliases={a: a for a in range(n)},
        scratch_shapes=[pltpu.SemaphoreType.DMA((n,)), pltpu.SemaphoreType.DMA((n,))],
    )(*bufs)


def _all_sum_small(v):
    def body(v_ref, o_ref, buf, send, recv):
        x, y, c, _ = _place()
        me = 4 * x + 2 * y + c
        buf[me] = v_ref[...]
        cps = []
        for m in range(1, 8):
            to = (x ^ (m >> 2), y ^ ((m >> 1) & 1), c ^ (m & 1))
            cps.append(_rcopy(v_ref, buf.at[me], send, recv, m - 1, to))
        for cp in cps:
            cp.start()
        for cp in cps:
            cp.wait()
        acc = buf[0]
        for d in range(1, 8):
            acc = acc + buf[d]
        o_ref[...] = acc

    vm = pl.BlockSpec(memory_space=pltpu.VMEM)
    return pl.pallas_call(
        body, name="all_sum_small", in_specs=[vm], out_specs=vm,
        out_shape=jax.ShapeDtypeStruct(v.shape, v.dtype),
        scratch_shapes=[pltpu.VMEM((8,) + v.shape, v.dtype), pltpu.SemaphoreType.DMA((7,)), pltpu.SemaphoreType.DMA((7,))],
    )(v)


def _adamw_math(w, g, m, v):
    m = ADAM_B1 * m + (1.0 - ADAM_B1) * g
    v = ADAM_B2 * v + (1.0 - ADAM_B2) * (g * g)
    m_hat = m / (1.0 - ADAM_B1 ** ADAM_STEP)
    v_hat = v / (1.0 - ADAM_B2 ** ADAM_STEP)
    return -ADAM_LR * (m_hat / (jnp.sqrt(v_hat) + ADAM_EPS) + ADAM_WD * w), m, v


def _adamw(w, g, m, v):
    L, R, C = w.shape
    rows = R // 4

    def body(w_ref, g_ref, m_ref, v_ref, d_ref, mo_ref, vo_ref):
        d_ref[...], mo_ref[...], vo_ref[...] = _adamw_math(w_ref[...], g_ref[...], m_ref[...], v_ref[...])

    blk = pl.BlockSpec((1, rows, C), lambda l, i: (l, i, 0))
    return pl.pallas_call(
        body, name="adamw", grid=(L, R // rows), in_specs=[blk] * 4, out_specs=[blk] * 3,
        out_shape=[jax.ShapeDtypeStruct(w.shape, f32)] * 3,
        compiler_params=_params("parallel", "parallel"),
    )(w, g, m, v)


def _chip_index():
    return jnp.reshape(2 * lax.axis_index("x") + lax.axis_index("y"), (1,)).astype(jnp.int32)


def _shard_placed(w):
    L, R, C = w.shape
    rows = R // 4

    def body(k_ref, w_ref, o_ref):
        o_ref[0, 0] = w_ref[0].astype(bf16)

    return pl.pallas_call(
        body, name="shard_placed",
        grid_spec=pltpu.PrefetchScalarGridSpec(
            num_scalar_prefetch=1, grid=(L, R // rows),
            in_specs=[pl.BlockSpec((1, rows, C), lambda l, i, k: (l, i, 0))],
            out_specs=pl.BlockSpec((1, 1, rows, C), lambda l, i, k: (l, k[0], i, 0))),
        out_shape=jax.ShapeDtypeStruct((L, 4, R, C), bf16),
        compiler_params=_params("parallel", "parallel"),
    )(_chip_index(), w)


SMALL_ROWS = 4 * DEPTH


def _pack_small(g_pre, g_post, lb, g_head, sinks, loss=None):
    rows = []
    for l in range(DEPTH):
        tail = [g_head[l], sinks[l]]
        if loss is not None and l == 0:
            tail.append(jnp.reshape(loss, (1,)))
        tail = jnp.concatenate(tail)
        rows += [g_pre[l], g_post[l], lb[l], jnp.pad(tail, (0, D_MODEL - tail.shape[0]))]
    return jnp.stack(rows)


def _unpack_small(p):
    g_pre = jnp.stack([p[4 * l] for l in range(DEPTH)])
    g_post = jnp.stack([p[4 * l + 1] for l in range(DEPTH)])
    lb = jnp.stack([p[4 * l + 2] for l in range(DEPTH)])
    g_head = jnp.stack([p[4 * l + 3, :HG_HEAD_DIM] for l in range(DEPTH)])
    sinks = jnp.stack([p[4 * l + 3, HG_HEAD_DIM:HG_HEAD_DIM + ATT_HEADS] for l in range(DEPTH)])
    return g_pre, g_post, lb, g_head, sinks


def _small_update(gsum, w, m, v):
    def body(g_ref, w_ref, m_ref, v_ref, go_ref, d_ref, mo_ref, vo_ref):
        g = g_ref[...]
        w = w_ref[...]
        lbp = [w[4 * l + 2:4 * l + 3] for l in range(DEPTH)]
        mx = functools.reduce(jnp.maximum, lbp)
        e = [jnp.exp(t - mx) for t in lbp]
        tot = functools.reduce(jnp.add, e)
        p = [t / tot for t in e]
        glb = [g[4 * l + 2:4 * l + 3] for l in range(DEPTH)]
        row = lax.broadcasted_iota(jnp.int32, g.shape, 0)
        for j in range(DEPTH):
            gj = jnp.zeros_like(p[0])
            for l in range(DEPTH):
                for i in range(1, l + 1):
                    gj = gj + glb[l] * p[i] * ((1.0 if i == j else 0.0) - p[j])
            g = jnp.where(row == 4 * j + 2, gj, g)
        go_ref[...] = g
        d_ref[...], mo_ref[...], vo_ref[...] = _adamw_math(w, g, m_ref[...], v_ref[...])

    vm = pl.BlockSpec(memory_space=pltpu.VMEM)
    return pl.pallas_call(
        body, name="small_update", in_specs=[vm] * 4, out_specs=[vm] * 4,
        out_shape=[jax.ShapeDtypeStruct(gsum.shape, f32)] * 4,
    )(gsum, w, m, v)


def kernel(x, w_in, w_out, g_pre, g_post, lb_param, g_head, sinks, loss_target, m_w_in, m_w_out, m_g_pre, m_g_post, m_lb_param, m_g_head, m_sinks, v_w_in, v_w_out, v_g_pre, v_g_post, v_lb_param, v_g_head, v_sinks):
    B, S, _ = x.shape
    T = B * S
    L = DEPTH
    ri, ro = IN_WIDTH // 8, MIX_WIDTH // 8
    tr = lambda a: jnp.transpose(a, (0, 2, 1))
    wt, mt, vt = tr(w_in), tr(m_w_in), tr(v_w_in)
    wt_all, wo_all = _gather_shards([_shard_placed(wt).reshape(L, 4, 2, ri, D_MODEL),
                                     _shard_placed(w_out).reshape(L, 4, 2, ro, D_MODEL)])

    loss, dx, gwt, gwo, ggpre, ggpost, glb, gghead, gsinks = _local_grads(
        x.reshape(T, D_MODEL), loss_target.reshape(T, D_MODEL), wt_all.reshape(L, IN_WIDTH, D_MODEL),
        wo_all.reshape(L, MIX_WIDTH, D_MODEL), g_pre, g_post, lb_param, g_head, sinks, B=B, S=S)

    parts = [g.reshape(1, 4, 2, ri, D_MODEL) for g in gwt] + [g.reshape(1, 4, 2, ro, D_MODEL) for g in gwo]
    got = _pair_exchange(parts)
    sums = [_pair_add(p, r) for p, r in zip(parts, got)]
    recv = _chip_exchange(sums)
    mine = _pair_share([_chip_sum(s, r) for s, r in zip(sums, recv)])
    gwt_mine = jnp.concatenate(mine[:L], axis=0).reshape(L, 2 * ri, D_MODEL)
    grad_w_out = jnp.concatenate(mine[L:], axis=0).reshape(L, 2 * ro, D_MODEL)

    d_wt, nm_wt, nv_wt = _adamw(wt, gwt_mine, mt, vt)
    grad_w_in, d_w_in, nm_w_in, nv_w_in = tr(gwt_mine), tr(d_wt), tr(nm_wt), tr(nv_wt)
    d_w_out, nm_w_out, nv_w_out = _adamw(w_out, grad_w_out, m_w_out, v_w_out)

    gsum = _all_sum_small(_pack_small(ggpre, ggpost, glb, gghead, gsinks, loss))
    gs, ds, ms, vs = _small_update(
        gsum, _pack_small(g_pre, g_post, lb_param, g_head, sinks),
        _pack_small(m_g_pre, m_g_post, m_lb_param, m_g_head, m_sinks),
        _pack_small(v_g_pre, v_g_post, v_lb_param, v_g_head, v_sinks))
    loss_all = gsum[3, HG_HEAD_DIM + ATT_HEADS]
    return (loss_all, dx.reshape(B, S, D_MODEL), grad_w_in, grad_w_out, *_unpack_small(gs),
            d_w_in, d_w_out, *_unpack_small(ds), nm_w_in, nm_w_out, *_unpack_small(ms),
            nv_w_in, nv_w_out, *_unpack_small(vs))
```

```python
import functools
import math

import jax
import jax.numpy as jnp
from jax import lax
from jax.experimental import pallas as pl
from jax.experimental.pallas import tpu as pltpu

f32 = jnp.float32
bf16 = jnp.bfloat16

D_MODEL = 1024
DEPTH = 2
HG_WIDTH = 1024
HG_HEAD_DIM = 128
HG_HEADS = 8
CHUNK = 64
SUB = 16
ATT_WIDTH = 1024
ATT_HEAD_DIM = 64
ATT_HEADS = 16
ATT_GROUP = 8
KV_WIDTH = 128
ATT_BLOCK = 128
ATT_SCALE = 1.0 / math.sqrt(ATT_HEAD_DIM)
ROPE_THETA = 10000.0
IN_WIDTH = 6400
MIX_WIDTH = 2048
NORM_EPS = 1e-6
NEG_INF = -1e30
LB_FLOOR = 1e-20
LANES = 128
VMEM_LIMIT = 48 * 1024 * 1024

ADAM_LR = 0.001
ADAM_B1 = 0.9
ADAM_B2 = 0.999
ADAM_EPS = 1e-08
ADAM_WD = 0.01
ADAM_STEP = 10

QA_BLK, ZA_BLK, KV_BLK = 4, 5, 24

NT = (((1,), (1,)), ((), ()))
TN = (((0,), (0,)), ((), ()))


def _dot(a, b, dims=None, precision=None):
    if dims is None:
        return jnp.dot(a, b, preferred_element_type=f32, precision=precision)
    return lax.dot_general(a, b, dims, preferred_element_type=f32, precision=precision)


def _sigmoid(x):
    return 1.0 / (1.0 + jnp.exp(-x))


def _params(*sem):
    return pltpu.CompilerParams(dimension_semantics=sem, vmem_limit_bytes=VMEM_LIMIT)


TAIL = IN_WIDTH - 5120


def _in_proj(x, g, wt, tail, l, *, tm=512):
    T = x.shape[0]
    tm = min(tm, T)
    nmain = 5120 // TAIL

    def body(x_ref, g_ref, w_ref, t_ref, p_ref, h_ref, hs):
        j = pl.program_id(1)

        @pl.when(j == 0)
        def _():
            xv = x_ref[...]
            r = lax.rsqrt(jnp.mean(xv * xv, axis=-1, keepdims=True) + NORM_EPS)
            hv = (xv * r * g_ref[...]).astype(bf16)
            hs[...] = hv
            h_ref[...] = hv

        @pl.when(j < nmain)
        def _():
            p_ref[...] = _dot(hs[...], w_ref[...], NT)

        @pl.when(j == nmain)
        def _():
            p_ref[...] = _dot(hs[...], t_ref[...], NT)

    return pl.pallas_call(
        body, name="in_proj", grid=(T // tm, nmain + 1),
        in_specs=[pl.BlockSpec((tm, D_MODEL), lambda i, j: (i, 0)),
                  pl.BlockSpec((1, D_MODEL), lambda i, j: (0, 0)),
                  pl.BlockSpec((None, TAIL, D_MODEL), lambda i, j: (l, jnp.minimum(j, nmain - 1), 0)),
                  pl.BlockSpec((None, TAIL, D_MODEL), lambda i, j: (l, 0, 0))],
        out_specs=[pl.BlockSpec((tm, TAIL), lambda i, j: (i, j)),
                   pl.BlockSpec((tm, D_MODEL), lambda i, j: (i, 0))],
        out_shape=[jax.ShapeDtypeStruct((T, IN_WIDTH), f32), jax.ShapeDtypeStruct((T, D_MODEL), bf16)],
        scratch_shapes=[pltpu.VMEM((tm, D_MODEL), bf16)],
        compiler_params=_params("parallel", "arbitrary"),
    )(x, g, wt, tail)


def _out_proj(ch, ca, wo, l, x, g, *, tm=512):
    T = x.shape[0]
    tm = min(tm, T)
    half = MIX_WIDTH // 2

    def body(ch_ref, ca_ref, wo_ref, x_ref, g_ref, xn_ref, y_ref):
        y = _dot(ch_ref[...], wo_ref[0:half, :]) + _dot(ca_ref[...], wo_ref[half:MIX_WIDTH, :])
        r = lax.rsqrt(jnp.mean(y * y, axis=-1, keepdims=True) + NORM_EPS)
        y_ref[...] = y
        xn_ref[...] = x_ref[...] + y * r * g_ref[...]

    row = lambda i: (i, 0)
    fixed = lambda i: (0, 0)
    return pl.pallas_call(
        body, name="out_proj", grid=(T // tm,),
        in_specs=[pl.BlockSpec((tm, half), row), pl.BlockSpec((tm, half), row),
                  pl.BlockSpec((None, MIX_WIDTH, D_MODEL), lambda i: (l, 0, 0)), pl.BlockSpec((tm, D_MODEL), row),
                  pl.BlockSpec((1, D_MODEL), fixed)],
        out_specs=[pl.BlockSpec((tm, D_MODEL), row), pl.BlockSpec((tm, D_MODEL), row)],
        out_shape=[jax.ShapeDtypeStruct((T, D_MODEL), f32)] * 2,
        compiler_params=_params("parallel"),
    )(ch, ca, wo, x, g)


def _loss_head(y, target, *, tm=512):
    T = y.shape[0]
    tm = min(tm, T)

    def body(y_ref, t_ref, d_ref, l_ref):
        @pl.when(pl.program_id(0) == 0)
        def _():
            l_ref[...] = jnp.zeros_like(l_ref)
        err = y_ref[...] - t_ref[...]
        d_ref[...] = err * (1.0 / D_MODEL)
        l_ref[...] += jnp.sum(err * err) * (0.5 / D_MODEL)

    row = lambda i: (i, 0)
    return pl.pallas_call(
        body, name="loss_head", grid=(T // tm,),
        in_specs=[pl.BlockSpec((tm, D_MODEL), row), pl.BlockSpec((tm, D_MODEL), row)],
        out_specs=[pl.BlockSpec((tm, D_MODEL), row), pl.BlockSpec((8, LANES), lambda i: (0, 0))],
        out_shape=[jax.ShapeDtypeStruct((T, D_MODEL), f32), jax.ShapeDtypeStruct((8, LANES), f32)],
        compiler_params=_params("arbitrary"),
    )(y, target)


def _out_proj_bwd(dxn, y, g, wo, l, ch, ca, *, tm=256, ride=None):
    T = y.shape[0]
    tm = min(tm, T)
    half = MIX_WIDTH // 2

    def body(dx_ref, y_ref, g_ref, wo_ref, ch_ref, ca_ref, dch_ref, dca_ref, dwo_ref, dg_ref):
        @pl.when(pl.program_id(0) == 0)
        def _():
            dwo_ref[...] = jnp.zeros_like(dwo_ref)
            dg_ref[...] = jnp.zeros_like(dg_ref)
        y = y_ref[...]
        dx = dx_ref[...]
        r = lax.rsqrt(jnp.mean(y * y, axis=-1, keepdims=True) + NORM_EPS)
        gy = dx * g_ref[...]
        dy = r * gy - y * (r * r * r) * jnp.mean(gy * y, axis=-1, keepdims=True)
        dg_ref[...] += jnp.sum(dx * y * r, axis=0, keepdims=True)
        dyb = dy.astype(bf16)
        dch_ref[...] = _dot(dyb, wo_ref[0:half, :], NT)
        dca_ref[...] = _dot(dyb, wo_ref[half:MIX_WIDTH, :], NT)
        dwo_ref[0:half, :] += _dot(ch_ref[...], dyb, TN)
        dwo_ref[half:MIX_WIDTH, :] += _dot(ca_ref[...], dyb, TN)

    row = lambda i: (i, 0)
    fixed = lambda i: (0, 0)
    return _call(
        body, (dxn, y, g, wo, ch, ca), name="out_proj_bwd", grid=(T // tm,),
        in_specs=[pl.BlockSpec((tm, D_MODEL), row), pl.BlockSpec((tm, D_MODEL), row),
                  pl.BlockSpec((1, D_MODEL), fixed), pl.BlockSpec((None, MIX_WIDTH, D_MODEL), lambda i: (l, 0, 0)),
                  pl.BlockSpec((tm, half), row), pl.BlockSpec((tm, half), row)],
        out_specs=[pl.BlockSpec((tm, half), row), pl.BlockSpec((tm, half), row),
                   pl.BlockSpec((MIX_WIDTH, D_MODEL), fixed), pl.BlockSpec((1, D_MODEL), fixed)],
        out_shape=[jax.ShapeDtypeStruct((T, half), f32), jax.ShapeDtypeStruct((T, half), f32),
                   jax.ShapeDtypeStruct((MIX_WIDTH, D_MODEL), f32), jax.ShapeDtypeStruct((1, D_MODEL), f32)],
        semantics=("arbitrary",), ride=ride)


TILE = 256
PIECE_TILES = (4, 4, 4, 4, 4, 1, 4)
PIECE_START = tuple(sum(PIECE_TILES[:p]) for p in range(len(PIECE_TILES)))
N_TILES = sum(PIECE_TILES)


def _piece_specs(rows, index):
    def spec(s, n):
        def index_map(*g):
            r, t = index(*g)
            return r, jnp.clip(t - s, 0, n - 1)
        return pl.BlockSpec((rows, TILE), index_map)
    return [spec(s, n) for s, n in zip(PIECE_START, PIECE_TILES)]


def _for_piece(t, fn):
    for p, (s, n) in enumerate(zip(PIECE_START, PIECE_TILES)):
        pl.when((t >= s) & (t < s + n))(functools.partial(fn, p))


def _in_proj_bwd(pieces, wt, tail, l, x, g, dxn, *, tm=512, ride=None):
    T = x.shape[0]
    tm = min(tm, T)
    npc = len(pieces)
    nk = npc
    nmain = npc - 2
    wide = ATT_WIDTH

    def body(*refs):
        dp_refs = refs[:npc]
        w_ref, tz_ref, tkv_ref, x_ref, g_ref, dxn_ref, dx_ref, dg_ref, acc = refs[npc:]
        i, k = pl.program_id(0), pl.program_id(1)

        @pl.when((i == 0) & (k == 0))
        def _():
            dg_ref[...] = jnp.zeros_like(dg_ref)

        @pl.when(k == 0)
        def _():
            acc[...] = jnp.zeros_like(acc)

        for p in range(npc):
            w_p = w_ref if p < nmain else (tkv_ref if p == nmain else tz_ref)

            def add(p=p, w_p=w_p):
                acc[...] += _dot(dp_refs[p][...], w_p[...])
            pl.when(k == p)(add)

        @pl.when(k == nk - 1)
        def _():
            dh = acc[...]
            xv = x_ref[...]
            r = lax.rsqrt(jnp.mean(xv * xv, axis=-1, keepdims=True) + NORM_EPS)
            gy = dh * g_ref[...]
            dx_ref[...] = dxn_ref[...] + r * gy - xv * (r * r * r) * jnp.mean(gy * xv, axis=-1, keepdims=True)
            dg_ref[...] += jnp.sum(dh * xv * r, axis=0, keepdims=True)

    return _call(
        body, (*pieces, wt, tail, tail, x, g, dxn), name="in_proj_bwd", grid=(T // tm, nk),
        in_specs=[pl.BlockSpec((tm, p.shape[1]), lambda i, k: (i, 0)) for p in pieces] + [
            pl.BlockSpec((None, wide, D_MODEL), lambda i, k: (l, jnp.minimum(k, nmain - 1), 0)),
            pl.BlockSpec((None, wide, D_MODEL), lambda i, k: (l, 0, 0)),
            pl.BlockSpec((None, TAIL - wide, D_MODEL), lambda i, k: (l, wide // (TAIL - wide), 0)),
            pl.BlockSpec((tm, D_MODEL), lambda i, k: (i, 0)), pl.BlockSpec((1, D_MODEL), lambda i, k: (0, 0)),
            pl.BlockSpec((tm, D_MODEL), lambda i, k: (i, 0))],
        out_specs=[pl.BlockSpec((tm, D_MODEL), lambda i, k: (i, 0)), pl.BlockSpec((1, D_MODEL), lambda i, k: (0, 0))],
        out_shape=[jax.ShapeDtypeStruct((T, D_MODEL), f32), jax.ShapeDtypeStruct((1, D_MODEL), f32)],
        scratch_shapes=[pltpu.VMEM((tm, D_MODEL), f32)],
        semantics=("arbitrary", "arbitrary"), ride=ride)


def _grad_w_in(h, pieces):
    T = h.shape[0]
    npc = len(pieces)

    def body(*refs):
        h_ref, dp_refs, o_ref = refs[0], refs[1:1 + npc], refs[1 + npc]

        def put(p):
            o_ref[...] = _dot(dp_refs[p][...], h_ref[...], TN)
        _for_piece(pl.program_id(0), put)

    return pl.pallas_call(
        body, name="grad_w_in", grid=(N_TILES,),
        in_specs=[pl.BlockSpec((T, D_MODEL), lambda j: (0, 0), pipeline_mode=pl.Buffered(1))]
        + _piece_specs(T, lambda j: (0, j)),
        out_specs=pl.BlockSpec((TILE, D_MODEL), lambda j: (j, 0)),
        out_shape=jax.ShapeDtypeStruct((IN_WIDTH, D_MODEL), f32),
        compiler_params=_params("parallel"),
    )(h, *pieces)


def _lower_bound(lbp, layer):
    m = jnp.max(lbp, axis=0, keepdims=True)
    e = jnp.exp(lbp - m)
    p = e / jnp.sum(e, axis=0, keepdims=True)
    acc = p[0:1]
    for i in range(1, layer + 1):
        acc = acc + p[i:i + 1]
    return acc - p[0:1]


def _gate_parts(qr, fr, lb, lbf):
    sq = _sigmoid(qr)
    e = jnp.exp(-jnp.abs(fr))
    inv = 1.0 / (1.0 + e)
    pos = fr >= 0
    sg = jnp.where(pos, inv, e * inv)
    nsg = jnp.where(pos, e * inv, inv)
    fg = lbf + (1.0 - lb) * sg
    return qr * sq, sq, sg, nsg, fg, jnp.log(fg), (1.0 - lb) * nsg


def _anchor_masks(transposed=False):
    t = lax.broadcasted_iota(jnp.int32, (CHUNK, CHUNK), 1 if transposed else 0)
    s = lax.broadcasted_iota(jnp.int32, (CHUNK, CHUNK), 0 if transposed else 1)
    anchors = tuple(range(SUB - 1, CHUNK - 1, SUB))
    return anchors, [(t > a) & (s <= a) & (s > a - SUB) for a in anchors]


def _seg_sum(seg, x):
    hi = x.astype(bf16)
    return _dot(seg, hi) + _dot(seg, (x - hi.astype(f32)).astype(bf16))


def _hgrn_fwd(proj, lb_param, g_head, *, B, S, layer, ride=None):
    T = B * S
    TB = min(256, S)
    nT, NC = S // TB, TB // CHUNK
    nC = S // CHUNK
    HD = HG_HEAD_DIM

    def body(q_ref, f_ref, i_ref, z_ref, lb_ref, gh_ref, cat_ref, op_ref, st_ref,
             s_scr, b_scr, k_scr):
        @pl.when(pl.program_id(2) == 0)
        def _():
            s_scr[...] = jnp.zeros_like(s_scr)
        lb = _lower_bound(lb_ref[...], layer)
        lbf = jnp.maximum(lb, LB_FLOOR)
        gh = gh_ref[...]
        r_i = lax.broadcasted_iota(jnp.int32, (CHUNK, CHUNK), 0)
        c_i = lax.broadcasted_iota(jnp.int32, (CHUNK, CHUNK), 1)
        tril = (r_i >= c_i).astype(f32)
        rows8 = lax.broadcasted_iota(jnp.int32, (8, HD), 0)
        lane_c = lax.broadcasted_iota(jnp.int32, (8, CHUNK), 1)
        anchors, masks = _anchor_masks()

        def chunk(c, st):
            rs = slice(c * CHUNK, (c + 1) * CHUNK)
            b_s, k_s = b_scr.at[c], k_scr.at[c]
            q, _, _, _, _, logf, k = _gate_parts(q_ref[rs, :], f_ref[rs, :], lb, lbf)
            v = i_ref[rs, :]
            b = _dot(tril, logf, precision=lax.Precision.HIGHEST)
            b_s[...] = b
            k_s[...] = k
            pieces = []
            for blk in range(CHUNK // SUB):
                r0 = blk * SUB
                bp = [b[r0 + 8 * i:r0 + 8 * i + 8] for i in range(SUB // 8)]
                qp = [q[r0 + 8 * i:r0 + 8 * i + 8] for i in range(SUB // 8)]
                ap = [jnp.zeros((8, CHUNK), f32) for _ in range(SUB // 8)]
                for s in range(SUB):
                    bs = b_s[r0 + s:r0 + s + 1, :]
                    ks = k_s[r0 + s:r0 + s + 1, :]
                    for i in range(s // 8, SUB // 8):
                        diff = bp[i] - bs
                        if i == s // 8:
                            diff = jnp.where(rows8 >= s - 8 * i, diff, NEG_INF)
                        col = jnp.sum(jnp.exp(diff) * qp[i] * ks, axis=1, keepdims=True)
                        ap[i] = jnp.where(lane_c == r0 + s, col, ap[i])
                pieces += ap
            a_all = jnp.concatenate(pieces, axis=0)
            for an, mk in zip(anchors, masks):
                beta = b_s[an:an + 1, :]
                qh = (q * jnp.exp(jnp.minimum(b - beta, 0.0))).astype(bf16)
                kh = (k * jnp.exp(jnp.minimum(beta - b, 0.0))).astype(bf16)
                a_all = a_all + jnp.where(mk, _dot(qh, kh, NT), 0.0)
            st_ref[0, 0, c] = st
            vb16 = v.astype(bf16)
            o = _dot(a_all.astype(bf16), vb16) + _dot((q * jnp.exp(b)).astype(bf16), st.astype(bf16), NT)
            b_end = b_s[CHUNK - 1:CHUNK, :]
            kdec = (k * jnp.exp(b_end - b)).astype(bf16)
            st_next = jnp.exp(b_end) * st + _dot(vb16, kdec, TN)
            rr = lax.rsqrt(jnp.mean(o * o, axis=-1, keepdims=True) + NORM_EPS)
            zr = z_ref[rs, :]
            cat_ref[rs, :] = (o * rr * gh * (zr * _sigmoid(zr))).astype(bf16)
            op_ref[rs, :] = o
            return st_next

        st = s_scr[...]
        for c in range(NC):
            st = chunk(c, st)
        s_scr[...] = st

    def col(part):
        return pl.BlockSpec((TB, HD), lambda b, h, n: (b * nT + n, part * HG_HEADS + h))

    out_col = pl.BlockSpec((TB, HD), lambda b, h, n: (b * nT + n, h))
    return _call(
        body, (proj, proj, proj, proj, lb_param, g_head),
        name=f"hgrn_fwd_l{layer}", grid=(B, HG_HEADS, nT),
        in_specs=[col(0), col(1), col(2), col(3),
                  pl.BlockSpec((DEPTH, HD), lambda b, h, n: (0, h)),
                  pl.BlockSpec((1, HD), lambda b, h, n: (0, 0))],
        out_specs=[out_col, out_col,
                   pl.BlockSpec((1, 1, NC, HD, HD), lambda b, h, n: (b, h, n, 0, 0))],
        out_shape=[jax.ShapeDtypeStruct((T, HG_WIDTH), bf16), jax.ShapeDtypeStruct((T, HG_WIDTH), f32),
                   jax.ShapeDtypeStruct((B, HG_HEADS, nC, HD, HD), f32)],
        scratch_shapes=[pltpu.VMEM((HD, HD), f32), pltpu.VMEM((NC, CHUNK, HD), f32), pltpu.VMEM((NC, CHUNK, HD), f32)],
        semantics=("parallel", "parallel", "arbitrary"), ride=ride)


def _hgrn_bwd(proj, lb_param, g_head, o_pre, states, dcat, *, B, S, layer, ride=None):
    T = B * S
    TB = min(256, S)
    nT, NC = S // TB, TB // CHUNK
    HD = HG_HEAD_DIM

    def body(q_ref, f_ref, i_ref, z_ref, lb_ref, gh_ref, op_ref, st_ref, dc_ref,
             dq_ref, df_ref, di_ref, dz_ref, dlb_ref, dgh_ref,
             ds_scr, b_scr, q_scr, do_scr, wk_scr):
        @pl.when(pl.program_id(2) == 0)
        def _():
            ds_scr[...] = jnp.zeros_like(ds_scr)
            dlb_ref[...] = jnp.zeros_like(dlb_ref)
            dgh_ref[...] = jnp.zeros_like(dgh_ref)
        lb = _lower_bound(lb_ref[...], layer)
        lbf = jnp.maximum(lb, LB_FLOOR)
        ind = (lb > LB_FLOOR).astype(f32)
        gh = gh_ref[...]
        r_i = lax.broadcasted_iota(jnp.int32, (CHUNK, CHUNK), 0)
        c_i = lax.broadcasted_iota(jnp.int32, (CHUNK, CHUNK), 1)
        tril = (r_i >= c_i).astype(f32)
        triu = (c_i >= r_i).astype(f32)
        rows8 = lax.broadcasted_iota(jnp.int32, (8, HD), 0)
        lane_c = lax.broadcasted_iota(jnp.int32, (8, CHUNK), 1)
        last_row = lax.broadcasted_iota(jnp.int32, (CHUNK, HD), 0) == CHUNK - 1
        anchors, masks = _anchor_masks()
        _, masks_t = _anchor_masks(transposed=True)
        seg_t = lax.broadcasted_iota(jnp.int32, (SUB, 8 * SUB), 0)
        seg_r = lax.broadcasted_iota(jnp.int32, (SUB, 8 * SUB), 1) // 8
        seg0 = (seg_r == seg_t).astype(bf16)
        seg1 = (seg_r[:, 0:4 * SUB] + 8 == seg_t[:, 0:4 * SUB]).astype(bf16)

        def chunk(c, dst1):
            rs = slice(c * CHUNK, (c + 1) * CHUNK)
            b_s, q_s, do_s = b_scr.at[c], q_scr.at[c], do_scr.at[c]
            qr, fr = q_ref[rs, :], f_ref[rs, :]
            q, sq, sg, nsg, fg, logf, k = _gate_parts(qr, fr, lb, lbf)
            v = i_ref[rs, :]
            b = _dot(tril, logf, precision=lax.Precision.HIGHEST)
            o = op_ref[rs, :]
            dc = dc_ref[rs, :]
            zr = z_ref[rs, :]
            sz = _sigmoid(zr)
            rr = lax.rsqrt(jnp.mean(o * o, axis=-1, keepdims=True) + NORM_EPS)
            dz_ref[rs, :] = (dc * (o * rr * gh) * (sz * (1.0 + zr * (1.0 - sz)))).astype(bf16)
            dn = dc * (zr * sz)
            dgh_ref[0, 0] += jnp.sum(dn * o * rr, axis=0, keepdims=True)
            gdn = dn * gh
            d_o = rr * gdn - o * (rr * rr * rr) * jnp.mean(gdn * o, axis=-1, keepdims=True)
            b_s[...] = b
            q_s[...] = q
            do_s[...] = d_o
            dob = d_o.astype(bf16)
            vb16 = v.astype(bf16)
            d_a = _dot(dob, vb16, NT)
            d_q = jnp.zeros((CHUNK, HD), f32)
            d_k = jnp.zeros((CHUNK, HD), f32)
            at_all = jnp.zeros((CHUNK, CHUNK), f32)
            for an, mk, mkt in zip(anchors, masks, masks_t):
                beta = b_s[an:an + 1, :]
                eq = jnp.exp(jnp.minimum(b - beta, 0.0))
                ek = jnp.exp(jnp.minimum(beta - b, 0.0))
                qh = (q * eq).astype(bf16)
                kh = (k * ek).astype(bf16)
                at_all = at_all + jnp.where(mkt, _dot(kh, qh, NT), 0.0)
                d_aa = jnp.where(mk, d_a, 0.0).astype(bf16)
                d_q = d_q + _dot(d_aa, kh) * eq
                d_k = d_k + _dot(d_aa, qh, TN) * ek
            st0 = st_ref[0, 0, c]
            dst1b = dst1.astype(bf16)
            eb = jnp.exp(b)
            b_end = b_s[CHUNK - 1:CHUNK, :]
            edec = jnp.exp(b_end - b)
            e_end = jnp.exp(b_end)
            kdec = (k * edec).astype(bf16)
            qdec = (q * eb).astype(bf16)
            d_q = d_q + _dot(dob, st0.astype(bf16)) * eb
            d_v = _dot(kdec, dst1b, NT)
            d_k = d_k + _dot(vb16, dst1b) * edec
            st1 = e_end * st0 + _dot(vb16, kdec, TN)
            rterm = jnp.sum(dst1 * st1, axis=0, keepdims=True)
            dst0 = e_end * dst1 + _dot(dob, qdec, TN)
            dq_blocks, dk_pieces, at_pieces = [], [], []
            for blk in range(CHUNK // SUB):
                r0 = blk * SUB
                wk = wk_scr.at[c * (CHUNK // SUB) + blk]
                bp = [b[r0 + 8 * i:r0 + 8 * i + 8] for i in range(SUB // 8)]
                kp = [k[r0 + 8 * i:r0 + 8 * i + 8] for i in range(SUB // 8)]
                vp = [v[r0 + 8 * i:r0 + 8 * i + 8] for i in range(SUB // 8)]
                dkp = [jnp.zeros((8, HD), f32) for _ in range(SUB // 8)]
                atp = [jnp.zeros((8, CHUNK), f32) for _ in range(SUB // 8)]
                for t in range(SUB):
                    bt = b_s[r0 + t:r0 + t + 1, :]
                    qt = q_s[r0 + t:r0 + t + 1, :]
                    dot_ = do_s[r0 + t:r0 + t + 1, :]
                    for i in range(t // 8 + 1):
                        diff = bt - bp[i]
                        if i == t // 8:
                            diff = jnp.where(rows8 <= t - 8 * i, diff, NEG_INF)
                        e = jnp.exp(diff)
                        a = jnp.sum(e * kp[i] * qt, axis=1, keepdims=True)
                        atp[i] = jnp.where(lane_c == r0 + t, a, atp[i])
                        w = jnp.sum(vp[i] * dot_, axis=1, keepdims=True) * e
                        dkp[i] = dkp[i] + w * qt
                        row = 8 * t if i == 0 else 8 * SUB + 8 * (t - 8)
                        wk[row:row + 8, :] = w * kp[i]
                dq_blocks.append(_seg_sum(seg0, wk[0:8 * SUB, :]) + _seg_sum(seg1, wk[8 * SUB:12 * SUB, :]))
                dk_pieces += dkp
                at_pieces += atp
            d_q = d_q + jnp.concatenate(dq_blocks, axis=0)
            d_k = d_k + jnp.concatenate(dk_pieces, axis=0)
            d_v = d_v + _dot((at_all + jnp.concatenate(at_pieces, axis=0)).astype(bf16), dob)
            db = q * d_q - k * d_k + jnp.where(last_row, rterm, 0.0)
            dlt = _dot(triu, db, precision=lax.Precision.HIGHEST) - fg * d_k
            df_ref[rs, :] = (dlt * (1.0 - lb) * sg * nsg / fg).astype(bf16)
            dlb_ref[0] += jnp.sum(dlt * (ind - sg) / fg, axis=0, keepdims=True)
            dq_ref[rs, :] = (d_q * (sq * (1.0 + qr * (1.0 - sq)))).astype(bf16)
            di_ref[rs, :] = d_v.astype(bf16)
            return dst0

        dst = ds_scr[...]
        for c in reversed(range(NC)):
            dst = chunk(c, dst)
        ds_scr[...] = dst

    def col(part):
        return pl.BlockSpec((TB, HD), lambda b, h, n: (b * nT + nT - 1 - n, part * HG_HEADS + h))

    hcol = pl.BlockSpec((TB, HD), lambda b, h, n: (b * nT + nT - 1 - n, h))
    return _call(
        body, (proj, proj, proj, proj, lb_param, g_head, o_pre, states, dcat),
        name=f"hgrn_bwd_l{layer}", grid=(B, HG_HEADS, nT),
        in_specs=[col(0), col(1), col(2), col(3),
                  pl.BlockSpec((DEPTH, HD), lambda b, h, n: (0, h)),
                  pl.BlockSpec((1, HD), lambda b, h, n: (0, 0)),
                  hcol,
                  pl.BlockSpec((1, 1, NC, HD, HD), lambda b, h, n: (b, h, nT - 1 - n, 0, 0)),
                  hcol],
        out_specs=[hcol, hcol, hcol, hcol,
                   pl.BlockSpec((1, 1, HD), lambda b, h, n: (b, 0, h)),
                   pl.BlockSpec((1, 1, 1, HD), lambda b, h, n: (b, h, 0, 0))],
        out_shape=[jax.ShapeDtypeStruct((T, HG_WIDTH), bf16)] * 4 + [
            jax.ShapeDtypeStruct((B, 1, HG_WIDTH), f32), jax.ShapeDtypeStruct((B, HG_HEADS, 1, HD), f32)],
        scratch_shapes=[pltpu.VMEM((HD, HD), f32)] + [pltpu.VMEM((NC, CHUNK, HD), f32)] * 3
        + [pltpu.VMEM((NC * CHUNK // SUB, 12 * SUB, HD), f32)],
        semantics=("parallel", "parallel", "arbitrary"), ride=ride)


def _rope_tables(S):
    half = ATT_HEAD_DIM // 2
    inv_freq = ROPE_THETA ** (-jnp.arange(half, dtype=f32) / half)
    ang = jnp.arange(S, dtype=f32)[:, None] * inv_freq[None, :]
    cos, sin = jnp.cos(ang), jnp.sin(ang)
    return jnp.tile(jnp.concatenate([cos, cos], axis=1), (1, 2)), jnp.tile(jnp.concatenate([-sin, sin], axis=1), (1, 2))


def _swap_halves(x, first_half):
    return jnp.where(first_half, pltpu.roll(x, LANES - ATT_HEAD_DIM // 2, 1), pltpu.roll(x, ATT_HEAD_DIM // 2, 1))


def _rope(x, cos, sin, first_half):
    return x * cos + _swap_halves(x, first_half) * sin


def _rope_bwd(dy, cos, sin, first_half):
    return dy * cos + _swap_halves(dy * sin, first_half)


def _attn_consts(n):
    lane = lax.broadcasted_iota(jnp.int32, (1, LANES), 1)
    low = lane < ATT_HEAD_DIM
    first_half = (lane % ATT_HEAD_DIM) < ATT_HEAD_DIM // 2
    top = lax.broadcasted_iota(jnp.int32, (LANES, 1), 0) < ATT_HEAD_DIM
    s = lax.broadcasted_iota(jnp.int32, (2 * ATT_BLOCK, ATT_BLOCK), 0)
    t = lax.broadcasted_iota(jnp.int32, (2 * ATT_BLOCK, ATT_BLOCK), 1)
    mask = (s > t) & (s <= t + ATT_BLOCK) & ((s >= ATT_BLOCK) | (n > 0))
    return low, first_half, top, mask


def _dup_kv(x, low):
    rolled = pltpu.roll(x, ATT_HEAD_DIM, 1)
    return [jnp.where(low, x, rolled), jnp.where(low, rolled, x)]


def _attn_head(qtm, kd, vdt, sink, mask):
    s = jnp.where(mask, _dot(kd, qtm) * ATT_SCALE, NEG_INF)
    m = jnp.maximum(jnp.max(s, axis=0, keepdims=True), sink)
    p = jnp.exp(s - m)
    psink = jnp.exp(sink - m)
    inv = 1.0 / (jnp.sum(p, axis=0, keepdims=True) + psink)
    pn = p * inv
    return pn, psink * inv, _dot(vdt, pn.astype(bf16))


def _swa_fwd(proj, sink_b, cos, sin, *, B, S, ride=None):
    T = B * S
    L = ATT_BLOCK
    nB = S // L

    def body(q_ref, z_ref, kvc_ref, kvp_ref, sk_ref, cc_ref, sc_ref, cp_ref, sp_ref, cat_ref):
        n = pl.program_id(1)
        low, first_half, top, mask = _attn_consts(n)
        cc, sc = cc_ref[...], sc_ref[...]
        kc = _rope(kvc_ref[:, 0:LANES], cc, sc, first_half)
        kp = _rope(kvp_ref[:, 0:LANES], cp_ref[...], sp_ref[...], first_half)
        kd = [x.astype(bf16) for x in _dup_kv(jnp.concatenate([kp, kc], axis=0), low)]
        vdt = [x.T.astype(bf16) for x in _dup_kv(jnp.concatenate([kvp_ref[:, LANES:2 * LANES], kvc_ref[:, LANES:2 * LANES]], axis=0), low)]
        for pair in range(ATT_HEADS // 2):
            cols = slice(pair * LANES, (pair + 1) * LANES)
            j = (2 * pair) // ATT_GROUP
            qt = _rope(q_ref[:, cols], cc, sc, first_half).T
            outs = []
            for hh in range(2):
                h = 2 * pair + hh
                qtm = jnp.where(top if hh == 0 else ~top, qt, 0.0).astype(bf16)
                _, _, o = _attn_head(qtm, kd[j], vdt[j], sk_ref[h:h + 1, 0:1], mask)
                outs.append(o)
            zp = z_ref[:, cols]
            cat_ref[:, cols] = (jnp.where(top, outs[0], outs[1]).T * (zp * _sigmoid(zp))).astype(bf16)

    cur = lambda b, n: (b * nB + n, 0)
    return _call(
        body, (proj, proj, proj, proj, sink_b, cos, sin, cos, sin), name="swa_fwd", grid=(B, nB),
        in_specs=[pl.BlockSpec((L, ATT_WIDTH), lambda b, n: (b * nB + n, QA_BLK)),
                  pl.BlockSpec((L, ATT_WIDTH), lambda b, n: (b * nB + n, ZA_BLK)),
                  pl.BlockSpec((L, 2 * KV_WIDTH), lambda b, n: (b * nB + n, KV_BLK)),
                  pl.BlockSpec((L, 2 * KV_WIDTH), lambda b, n: (b * nB + jnp.maximum(n - 1, 0), KV_BLK)),
                  pl.BlockSpec((ATT_HEADS, LANES), lambda b, n: (0, 0)),
                  pl.BlockSpec((L, LANES), lambda b, n: (n, 0)), pl.BlockSpec((L, LANES), lambda b, n: (n, 0)),
                  pl.BlockSpec((L, LANES), lambda b, n: (jnp.maximum(n - 1, 0), 0)),
                  pl.BlockSpec((L, LANES), lambda b, n: (jnp.maximum(n - 1, 0), 0))],
        out_specs=[pl.BlockSpec((L, ATT_WIDTH), cur)],
        out_shape=[jax.ShapeDtypeStruct((T, ATT_WIDTH), bf16)],
        semantics=("parallel", "parallel"), ride=ride)


def _swa_bwd(proj, sink_b, cos, sin, dcat, *, B, S):
    T = B * S
    L = ATT_BLOCK
    nB = S // L

    def body(q_ref, z_ref, kvc_ref, kvp_ref, sk_ref, cc_ref, sc_ref, cp_ref, sp_ref, dc_ref,
             dq_ref, dz_ref, dkv_ref, dsk_ref, carry, ds_st, pn_st, q_st, do_st):
        step = pl.program_id(1)
        n = nB - 1 - step

        @pl.when((pl.program_id(0) == 0) & (step == 0))
        def _():
            dsk_ref[...] = jnp.zeros_like(dsk_ref)

        @pl.when(step == 0)
        def _():
            carry[...] = jnp.zeros_like(carry)
        low, first_half, top, mask = _attn_consts(n)
        cc, sc, cp, sp = cc_ref[...], sc_ref[...], cp_ref[...], sp_ref[...]
        kc = _rope(kvc_ref[:, 0:LANES], cc, sc, first_half)
        kp = _rope(kvp_ref[:, 0:LANES], cp, sp, first_half)
        kdf = _dup_kv(jnp.concatenate([kp, kc], axis=0), low)
        vdf = _dup_kv(jnp.concatenate([kvp_ref[:, LANES:2 * LANES], kvc_ref[:, LANES:2 * LANES]], axis=0), low)
        kd = [x.astype(bf16) for x in kdf]
        vd = [x.astype(bf16) for x in vdf]
        kdt = [x.T.astype(bf16) for x in kdf]
        vdt = [x.T.astype(bf16) for x in vdf]
        dkd, dvd = [], []
        for pair in range(ATT_HEADS // 2):
            cols = slice(pair * LANES, (pair + 1) * LANES)
            j = (2 * pair) // ATT_GROUP
            qp = _rope(q_ref[:, cols], cc, sc, first_half)
            qt = qp.T
            zp = z_ref[:, cols]
            dc = dc_ref[:, cols]
            sz = _sigmoid(zp)
            d_o = dc * (zp * sz)
            dot_ = d_o.T
            res = []
            for hh in range(2):
                rsel = top if hh == 0 else ~top
                qtm = jnp.where(rsel, qt, 0.0).astype(bf16)
                pn, psn, o = _attn_head(qtm, kd[j], vdt[j], sk_ref[2 * pair + hh:2 * pair + hh + 1, 0:1], mask)
                res.append((rsel, pn, psn, o))
            ot = jnp.where(top, res[0][3], res[1][3])
            dz_ref[:, cols] = (dc * ot.T * (sz * (1.0 + zp * (1.0 - sz)))).astype(bf16)
            dqts = []
            for hh in range(2):
                h = 2 * pair + hh
                rsel, pn, psn, _ = res[hh]
                lsel = low if hh == 0 else ~low
                dotm = jnp.where(rsel, dot_, 0.0)
                delta = jnp.sum(dotm * ot, axis=0, keepdims=True)
                dst = (pn * (_dot(vd[j], dotm.astype(bf16)) - delta) * ATT_SCALE).astype(bf16)
                dsk_ref[h:h + 1, :] += jnp.zeros((1, LANES), f32) - jnp.sum(psn * delta)
                dqts.append(_dot(kdt[j], dst))
                g = h % ATT_GROUP
                ds_st[:, g * LANES:(g + 1) * LANES] = dst
                pn_st[:, g * LANES:(g + 1) * LANES] = pn.astype(bf16)
                q_st[g * LANES:(g + 1) * LANES, :] = jnp.where(lsel, qp, 0.0).astype(bf16)
                do_st[g * LANES:(g + 1) * LANES, :] = jnp.where(lsel, d_o, 0.0).astype(bf16)
            dq_ref[:, cols] = _rope_bwd(jnp.where(top, dqts[0], dqts[1]).T, cc, sc, first_half).astype(bf16)
            if (2 * pair + 2) % ATT_GROUP == 0:
                dkd.append(_dot(ds_st[...], q_st[...]))
                dvd.append(_dot(pn_st[...], do_st[...]))
        dk = [x + pltpu.roll(x, ATT_HEAD_DIM, 1) for x in dkd]
        dv = [x + pltpu.roll(x, ATT_HEAD_DIM, 1) for x in dvd]
        dk = jnp.where(low, dk[0], dk[1])
        dv = jnp.where(low, dv[0], dv[1])
        dkv_ref[:, 0:LANES] = (_rope_bwd(dk[L:2 * L], cc, sc, first_half) + carry[:, 0:LANES]).astype(bf16)
        dkv_ref[:, LANES:2 * LANES] = (dv[L:2 * L] + carry[:, LANES:2 * LANES]).astype(bf16)
        carry[:, 0:LANES] = _rope_bwd(dk[0:L], cp, sp, first_half)
        carry[:, LANES:2 * LANES] = dv[0:L]

    rev = lambda b, s: b * nB + nB - 1 - s
    revp = lambda b, s: b * nB + jnp.maximum(nB - 2 - s, 0)
    wide = lambda blk: pl.BlockSpec((L, ATT_WIDTH), lambda b, s: (rev(b, s), blk))
    tab = pl.BlockSpec((L, LANES), lambda b, s: (nB - 1 - s, 0))
    tabp = pl.BlockSpec((L, LANES), lambda b, s: (jnp.maximum(nB - 2 - s, 0), 0))
    return pl.pallas_call(
        body, name="swa_bwd", grid=(B, nB),
        in_specs=[wide(QA_BLK), wide(ZA_BLK),
                  pl.BlockSpec((L, 2 * KV_WIDTH), lambda b, s: (rev(b, s), KV_BLK)),
                  pl.BlockSpec((L, 2 * KV_WIDTH), lambda b, s: (revp(b, s), KV_BLK)),
                  pl.BlockSpec((ATT_HEADS, LANES), lambda b, s: (0, 0)),
                  tab, tab, tabp, tabp, wide(0)],
        out_specs=[wide(0), wide(0), pl.BlockSpec((L, 2 * KV_WIDTH), lambda b, s: (rev(b, s), 0)),
                   pl.BlockSpec((ATT_HEADS, LANES), lambda b, s: (0, 0))],
        out_shape=[jax.ShapeDtypeStruct((T, ATT_WIDTH), bf16), jax.ShapeDtypeStruct((T, ATT_WIDTH), bf16),
                   jax.ShapeDtypeStruct((T, 2 * KV_WIDTH), bf16), jax.ShapeDtypeStruct((ATT_HEADS, LANES), f32)],
        scratch_shapes=[pltpu.VMEM((L, 2 * KV_WIDTH), f32),
                        pltpu.VMEM((2 * L, ATT_GROUP * LANES), bf16), pltpu.VMEM((2 * L, ATT_GROUP * LANES), bf16),
                        pltpu.VMEM((ATT_GROUP * LANES, LANES), bf16), pltpu.VMEM((ATT_GROUP * LANES, LANES), bf16)],
        compiler_params=_params("arbitrary", "arbitrary"),
    )(proj, proj, proj, proj, sink_b, cos, sin, cos, sin, dcat)


def _train_step(x, target, bufs, g_pre, g_post, lb_param, g_head, sinks, *, B, S, exchange):
    L = DEPTH
    ri, ro = IN_WIDTH // 8, MIX_WIDTH // 8
    cos, sin = _rope_tables(S)
    full = list(bufs)
    if exchange:
        full[0] = _run_exchange(_gather_d2d(_run_exchange(_gather_ici(bufs[0]))))
    saved = []
    for l in range(L):
        wt = full[l][0].reshape(1, IN_WIDTH, D_MODEL)
        wo = full[l][1].reshape(1, MIX_WIDTH, D_MODEL)
        tail = jnp.concatenate([wt[:, 5376:6400], wt[:, 5120:5376]], axis=1)
        proj, h = _in_proj(x, g_pre[l:l + 1], wt, tail, 0)
        ahead = exchange and l + 1 < L
        (ch, o_pre, states), landed = _hgrn_fwd(proj, lb_param, g_head[l:l + 1], B=B, S=S, layer=l,
                                                ride=_gather_ici(bufs[l + 1]) if ahead else None)
        sink_b = jnp.broadcast_to(sinks[l][:, None], (ATT_HEADS, LANES))
        (ca,), passed = _swa_fwd(proj, sink_b, cos, sin, B=B, S=S, ride=_gather_d2d(landed) if ahead else None)
        if ahead:
            full[l + 1] = passed
        xn, y = _out_proj(ch, ca, wo, 0, x, g_post[l:l + 1])
        saved.append((x, proj, h, ch, o_pre, states, sink_b, ca, y, wt, tail, wo))
        x = xn
    dx, loss = _loss_head(x, target)

    def reduce_tail(sums, recv):
        return _run_exchange(_pair_share([_chip_sum(s, r) for s, r in zip(sums, recv)]))

    grads = [None] * L
    waiting = None
    gg_pre, gg_post, g_lb, gg_head, g_sinks = [], [], [], [], []
    for l in reversed(range(L)):
        x_in, proj, h, ch, o_pre, states, sink_b, ca, y, wt, tail, wo = saved[l]
        (dch, dca, dwo, dgpost), got = _out_proj_bwd(dx, y, g_post[l:l + 1], wo, 0, ch, ca,
                                                     ride=_pair_exchange(waiting) if waiting else None)
        sums = [_pair_add(p, r) for p, r in zip(waiting, got)] if waiting else None
        (dq, df, di, dz, dlb, dgh), recv = _hgrn_bwd(proj, lb_param, g_head[l:l + 1], o_pre, states, dch, B=B, S=S,
                                                     layer=l, ride=_chip_exchange(sums) if waiting else None)
        if waiting:
            grads[l + 1] = reduce_tail(sums, recv)
        dqa, dza, dkv, dsk = _swa_bwd(proj, sink_b, cos, sin, dca, B=B, S=S)
        pieces = [dq, df, di, dz, dqa, dkv, dza]
        gwt = _grad_w_in(h, pieces)
        mine = [gwt.reshape(1, 4, 2, ri, D_MODEL), dwo.reshape(1, 4, 2, ro, D_MODEL)]
        at_end = exchange and l == 0
        (dx, dgpre), got = _in_proj_bwd(pieces, wt, tail, 0, x_in, g_pre[l:l + 1], dx,
                                        ride=_pair_exchange(mine) if at_end else None)
        if at_end:
            sums = [_pair_add(p, r) for p, r in zip(mine, got)]
            grads[0] = reduce_tail(sums, _run_exchange(_chip_exchange(sums)))
        elif exchange:
            waiting = mine
        else:
            grads[l] = [gwt, dwo]
        gg_pre.append(dgpre[0])
        gg_post.append(dgpost[0])
        g_lb.append(jnp.sum(dlb, axis=(0, 1)))
        gg_head.append(jnp.sum(dgh, axis=(0, 1, 2)))
        g_sinks.append(dsk[:, 0])
    rev = lambda xs: jnp.stack(xs[::-1])
    return loss[0, 0], dx, grads, rev(gg_pre), rev(gg_post), rev(g_lb), rev(gg_head), rev(g_sinks)


MESH = pl.DeviceIdType.MESH
ANY = pl.BlockSpec(memory_space=pl.ANY)


def _place():
    x, y, c = lax.axis_index("x"), lax.axis_index("y"), lax.axis_index("c")
    return x, y, c, [(1 - x, y), (x, 1 - y), (1 - x, 1 - y)]


def _rcopy(src, dst, send, recv, k, to):
    return pltpu.make_async_remote_copy(src_ref=src, dst_ref=dst, send_sem=send.at[k], recv_sem=recv.at[k],
                                        device_id=to, device_id_type=MESH)


class _Exchange:
    def __init__(self, name, inputs, out_shapes, n_sems, plan, in_place=False):
        self.name, self.inputs, self.out_shapes, self.n_sems, self.plan = name, inputs, out_shapes, n_sems, plan
        self.aliases = {a: a for a in range(len(inputs))} if in_place else {}

    def start(self, ins, outs, send, recv):
        for cp in self.plan(ins, outs, send, recv)[0]:
            cp.start()

    def finish(self, ins, outs, send, recv):
        sent, arriving = self.plan(ins, outs, send, recv)
        for cp in arriving:
            cp.wait_recv()
        for cp in sent:
            cp.wait_send()

    def sems(self):
        return [pltpu.SemaphoreType.DMA((self.n_sems,)), pltpu.SemaphoreType.DMA((self.n_sems,))]


def _run_exchange(ex):
    n_in, n_out = len(ex.inputs), len(ex.out_shapes)

    def body(*refs):
        ins, outs = refs[:n_in], refs[n_in:n_in + n_out]
        send, recv = refs[n_in + n_out:]
        ex.start(ins, outs, send, recv)
        ex.finish(ins, outs, send, recv)

    return pl.pallas_call(
        body, name=ex.name, in_specs=[ANY] * n_in, out_specs=[ANY] * n_out, out_shape=ex.out_shapes,
        input_output_aliases=ex.aliases, scratch_shapes=ex.sems(),
    )(*ex.inputs)


def _call(body, operands, *, name, grid, in_specs, out_specs, out_shape, scratch_shapes=(), semantics, ride=None):
    if ride is None:
        outs = pl.pallas_call(body, name=name, grid=grid, in_specs=in_specs, out_specs=out_specs, out_shape=out_shape,
                              scratch_shapes=list(scratch_shapes), compiler_params=_params(*semantics))(*operands)
        return outs, []
    n_in, n_out, n_scr = len(in_specs), len(out_specs), len(scratch_shapes)
    r_in, r_out = len(ride.inputs), len(ride.out_shapes)

    def riding(*refs):
        refs = list(refs)
        ins, rins = refs[:n_in], refs[n_in:n_in + r_in]
        o0 = n_in + r_in
        outs, routs = refs[o0:o0 + n_out], refs[o0 + n_out:o0 + n_out + r_out]
        scr = refs[o0 + n_out + r_out:o0 + n_out + r_out + n_scr]
        send, recv = refs[-2:]
        ids = [pl.program_id(d) for d in range(len(grid))]
        first = functools.reduce(jnp.logical_and, [i == 0 for i in ids])
        last = functools.reduce(jnp.logical_and, [i == g - 1 for i, g in zip(ids, grid)])
        pl.when(first)(lambda: ride.start(rins, routs, send, recv))
        body(*ins, *outs, *scr)
        pl.when(last)(lambda: ride.finish(rins, routs, send, recv))

    res = pl.pallas_call(
        riding, name=name + "_" + ride.name, grid=grid,
        in_specs=list(in_specs) + [ANY] * r_in, out_specs=list(out_specs) + [ANY] * r_out,
        out_shape=list(out_shape) + list(ride.out_shapes),
        input_output_aliases={n_in + a: n_out + b for a, b in ride.aliases.items()},
        scratch_shapes=list(scratch_shapes) + ride.sems(),
        compiler_params=_params(*(["arbitrary"] * len(grid))),
    )(*operands, *ride.inputs)
    return res[:n_out], res[n_out:]


def _gather_ici(bufs, name="gather_ici"):
    n = len(bufs)

    def plan(ins, outs, send, recv):
        x, y, c, chips = _place()
        me = 2 * x + y
        sent, arriving = [], []
        for j, (px, py) in enumerate(chips):
            for a in range(n):
                mine, theirs = outs[a].at[:, me, c], outs[a].at[:, 2 * px + py, c]
                sent.append(_rcopy(mine, mine, send, recv, j * n + a, (px, py, c)))
                arriving.append(_rcopy(theirs, theirs, send, recv, j * n + a, (px, py, c)))
        return sent, arriving

    return _Exchange(name, bufs, [jax.ShapeDtypeStruct(b.shape, b.dtype) for b in bufs], 3 * n, plan, in_place=True)


def _gather_d2d(bufs, name="gather_d2d"):
    n = len(bufs)

    def plan(ins, outs, send, recv):
        x, y, c, chips = _place()
        sib = (x, y, 1 - c)
        sent, arriving = [], []
        for j, (px, py) in enumerate(chips):
            for a in range(n):
                got, theirs = outs[a].at[:, 2 * px + py, c], outs[a].at[:, 2 * px + py, 1 - c]
                sent.append(_rcopy(got, got, send, recv, j * n + a, sib))
                arriving.append(_rcopy(theirs, theirs, send, recv, j * n + a, sib))
        return sent, arriving

    return _Exchange(name, bufs, [jax.ShapeDtypeStruct(b.shape, b.dtype) for b in bufs], 3 * n, plan, in_place=True)


def _pair_exchange(parts):
    n = len(parts)

    def plan(ins, outs, send, recv):
        x, y, c, _ = _place()
        cps = [_rcopy(ins[a].at[:, :, 1 - c], outs[a], send, recv, a, (x, y, 1 - c)) for a in range(n)]
        return cps, cps

    return _Exchange("pair_exchange", parts,
                     [jax.ShapeDtypeStruct(p.shape[:2] + p.shape[3:], p.dtype) for p in parts], n, plan)


def _block_rows(r):
    return r if r <= 512 else r // 2


def _pair_add(part, got):
    L, K, _, r, C = part.shape
    rows = _block_rows(r)

    def body(c_ref, a_ref, b_ref, o_ref):
        o_ref[0, 0] = (a_ref[0, 0, 0] + b_ref[0, 0]).astype(bf16)

    blk = (1, 1, rows, C)
    return pl.pallas_call(
        body, name="pair_add",
        grid_spec=pltpu.PrefetchScalarGridSpec(
            num_scalar_prefetch=1, grid=(L, K, r // rows),
            in_specs=[pl.BlockSpec((1, 1, 1, rows, C), lambda l, k, i, c: (l, k, c[0], i, 0)),
                      pl.BlockSpec(blk, lambda l, k, i, c: (l, k, i, 0))],
            out_specs=pl.BlockSpec(blk, lambda l, k, i, c: (l, k, i, 0))),
        out_shape=jax.ShapeDtypeStruct((L, K, r, C), bf16),
        compiler_params=_params("parallel", "parallel", "parallel"),
    )(jnp.reshape(lax.axis_index("c"), (1,)).astype(jnp.int32), part, got)


def _chip_exchange(sums):
    n = len(sums)

    def plan(ins, outs, send, recv):
        x, y, c, chips = _place()
        cps = []
        for j, (px, py) in enumerate(chips):
            for a in range(n):
                cps.append(_rcopy(ins[a].at[:, 2 * px + py], outs[a].at[j], send, recv, j * n + a, (px, py, c)))
        return cps, cps

    return _Exchange("chip_exchange", sums,
                     [jax.ShapeDtypeStruct((3, s.shape[0]) + s.shape[2:], s.dtype) for s in sums], 3 * n, plan)


def _chip_sum(mine, got):
    L, K, r, C = mine.shape
    rows = _block_rows(r)

    def body(p_ref, a_ref, b_ref, o_ref):
        o_ref[0, 0] = (a_ref[0, 0].astype(f32) + b_ref[0, 0].astype(f32)) + (b_ref[1, 0].astype(f32) + b_ref[2, 0].astype(f32))

    place = jnp.stack([2 * lax.axis_index("x") + lax.axis_index("y"), lax.axis_index("c")]).astype(jnp.int32)
    return pl.pallas_call(
        body, name="chip_sum",
        grid_spec=pltpu.PrefetchScalarGridSpec(
            num_scalar_prefetch=1, grid=(L, r // rows),
            in_specs=[pl.BlockSpec((1, 1, rows, C), lambda l, i, p: (l, p[0], i, 0)),
                      pl.BlockSpec((3, 1, rows, C), lambda l, i, p: (0, l, i, 0))],
            out_specs=pl.BlockSpec((1, 1, rows, C), lambda l, i, p: (l, p[1], i, 0))),
        out_shape=jax.ShapeDtypeStruct((L, 2, r, C), f32),
        compiler_params=_params("parallel", "parallel"),
    )(place, mine, got)


def _pair_share(bufs):
    n = len(bufs)

    def plan(ins, outs, send, recv):
        x, y, c, _ = _place()
        sib = (x, y, 1 - c)
        sent = [_rcopy(outs[a].at[:, c], outs[a].at[:, c], send, recv, a, sib) for a in range(n)]
        arriving = [_rcopy(outs[a].at[:, 1 - c], outs[a].at[:, 1 - c], send, recv, a, sib) for a in range(n)]
        return sent, arriving

    return _Exchange("pair_share", bufs, [jax.ShapeDtypeStruct(b.shape, b.dtype) for b in bufs], n, plan, in_place=True)


def _all_sum_small(v):
    def body(v_ref, o_ref, buf, send, recv):
        x, y, c, _ = _place()
        me = 4 * x + 2 * y + c
        buf[me] = v_ref[...]
        cps = []
        for m in range(1, 8):
            to = (x ^ (m >> 2), y ^ ((m >> 1) & 1), c ^ (m & 1))
            cps.append(_rcopy(v_ref, buf.at[me], send, recv, m - 1, to))
        for cp in cps:
            cp.start()
        for cp in cps:
            cp.wait()
        acc = buf[0]
        for d in range(1, 8):
            acc = acc + buf[d]
        o_ref[...] = acc

    vm = pl.BlockSpec(memory_space=pltpu.VMEM)
    return pl.pallas_call(
        body, name="all_sum_small", in_specs=[vm], out_specs=vm,
        out_shape=jax.ShapeDtypeStruct(v.shape, v.dtype),
        scratch_shapes=[pltpu.VMEM((8,) + v.shape, v.dtype), pltpu.SemaphoreType.DMA((7,)), pltpu.SemaphoreType.DMA((7,))],
    )(v)


def _adamw_math(w, g, m, v):
    m = ADAM_B1 * m + (1.0 - ADAM_B1) * g
    v = ADAM_B2 * v + (1.0 - ADAM_B2) * (g * g)
    m_hat = m / (1.0 - ADAM_B1 ** ADAM_STEP)
    v_hat = v / (1.0 - ADAM_B2 ** ADAM_STEP)
    return -ADAM_LR * (m_hat / (jnp.sqrt(v_hat) + ADAM_EPS) + ADAM_WD * w), m, v


def _adamw(w, g, m, v):
    L, R, C = w.shape
    rows = R // 4

    def body(w_ref, g_ref, m_ref, v_ref, d_ref, mo_ref, vo_ref):
        d_ref[...], mo_ref[...], vo_ref[...] = _adamw_math(w_ref[...], g_ref[...], m_ref[...], v_ref[...])

    blk = pl.BlockSpec((1, rows, C), lambda l, i: (l, i, 0))
    return pl.pallas_call(
        body, name="adamw", grid=(L, R // rows), in_specs=[blk] * 4, out_specs=[blk] * 3,
        out_shape=[jax.ShapeDtypeStruct(w.shape, f32)] * 3,
        compiler_params=_params("parallel", "parallel"),
    )(w, g, m, v)


def _chip_index():
    return jnp.reshape(2 * lax.axis_index("x") + lax.axis_index("y"), (1,)).astype(jnp.int32)


def _shard_placed(w, l):
    _, R, C = w.shape
    rows = R // 4

    def body(k_ref, w_ref, o_ref):
        o_ref[0, 0] = w_ref[0].astype(bf16)

    return pl.pallas_call(
        body, name="shard_placed",
        grid_spec=pltpu.PrefetchScalarGridSpec(
            num_scalar_prefetch=1, grid=(R // rows,),
            in_specs=[pl.BlockSpec((1, rows, C), lambda i, k: (l, i, 0))],
            out_specs=pl.BlockSpec((1, 1, rows, C), lambda i, k: (0, k[0], i, 0))),
        out_shape=jax.ShapeDtypeStruct((1, 4, R, C), bf16),
        compiler_params=_params("parallel"),
    )(_chip_index(), w)


SMALL_ROWS = 4 * DEPTH


def _pack_small(g_pre, g_post, lb, g_head, sinks, loss=None):
    rows = []
    for l in range(DEPTH):
        tail = [g_head[l], sinks[l]]
        if loss is not None and l == 0:
            tail.append(jnp.reshape(loss, (1,)))
        tail = jnp.concatenate(tail)
        rows += [g_pre[l], g_post[l], lb[l], jnp.pad(tail, (0, D_MODEL - tail.shape[0]))]
    return jnp.stack(rows)


def _unpack_small(p):
    g_pre = jnp.stack([p[4 * l] for l in range(DEPTH)])
    g_post = jnp.stack([p[4 * l + 1] for l in range(DEPTH)])
    lb = jnp.stack([p[4 * l + 2] for l in range(DEPTH)])
    g_head = jnp.stack([p[4 * l + 3, :HG_HEAD_DIM] for l in range(DEPTH)])
    sinks = jnp.stack([p[4 * l + 3, HG_HEAD_DIM:HG_HEAD_DIM + ATT_HEADS] for l in range(DEPTH)])
    return g_pre, g_post, lb, g_head, sinks


def _small_update(gsum, w, m, v):
    def body(g_ref, w_ref, m_ref, v_ref, go_ref, d_ref, mo_ref, vo_ref):
        g = g_ref[...]
        w = w_ref[...]
        lbp = [w[4 * l + 2:4 * l + 3] for l in range(DEPTH)]
        mx = functools.reduce(jnp.maximum, lbp)
        e = [jnp.exp(t - mx) for t in lbp]
        tot = functools.reduce(jnp.add, e)
        p = [t / tot for t in e]
        glb = [g[4 * l + 2:4 * l + 3] for l in range(DEPTH)]
        row = lax.broadcasted_iota(jnp.int32, g.shape, 0)
        for j in range(DEPTH):
            gj = jnp.zeros_like(p[0])
            for l in range(DEPTH):
                for i in range(1, l + 1):
                    gj = gj + glb[l] * p[i] * ((1.0 if i == j else 0.0) - p[j])
            g = jnp.where(row == 4 * j + 2, gj, g)
        go_ref[...] = g
        d_ref[...], mo_ref[...], vo_ref[...] = _adamw_math(w, g, m_ref[...], v_ref[...])

    vm = pl.BlockSpec(memory_space=pltpu.VMEM)
    return pl.pallas_call(
        body, name="small_update", in_specs=[vm] * 4, out_specs=[vm] * 4,
        out_shape=[jax.ShapeDtypeStruct(gsum.shape, f32)] * 4,
    )(gsum, w, m, v)


def kernel(x, w_in, w_out, g_pre, g_post, lb_param, g_head, sinks, loss_target, m_w_in, m_w_out, m_g_pre, m_g_post, m_lb_param, m_g_head, m_sinks, v_w_in, v_w_out, v_g_pre, v_g_post, v_lb_param, v_g_head, v_sinks):
    B, S, _ = x.shape
    T = B * S
    L = DEPTH
    ri, ro = IN_WIDTH // 8, MIX_WIDTH // 8
    tr = lambda a: jnp.transpose(a, (0, 2, 1))
    wt, mt, vt = tr(w_in), tr(m_w_in), tr(v_w_in)
    bufs = [[_shard_placed(wt, l).reshape(1, 4, 2, ri, D_MODEL), _shard_placed(w_out, l).reshape(1, 4, 2, ro, D_MODEL)]
            for l in range(L)]
    loss, dx, grads, ggpre, ggpost, glb, gghead, gsinks = _train_step(
        x.reshape(T, D_MODEL), loss_target.reshape(T, D_MODEL), bufs, g_pre, g_post, lb_param, g_head, sinks,
        B=B, S=S, exchange=True)
    gwt_mine = jnp.concatenate([g[0] for g in grads], axis=0).reshape(L, 2 * ri, D_MODEL)
    grad_w_out = jnp.concatenate([g[1] for g in grads], axis=0).reshape(L, 2 * ro, D_MODEL)

    d_wt, nm_wt, nv_wt = _adamw(wt, gwt_mine, mt, vt)
    grad_w_in, d_w_in, nm_w_in, nv_w_in = tr(gwt_mine), tr(d_wt), tr(nm_wt), tr(nv_wt)
    d_w_out, nm_w_out, nv_w_out = _adamw(w_out, grad_w_out, m_w_out, v_w_out)

    gsum = _all_sum_small(_pack_small(ggpre, ggpost, glb, gghead, gsinks, loss))
    gs, ds, ms, vs = _small_update(
        gsum, _pack_small(g_pre, g_post, lb_param, g_head, sinks),
        _pack_small(m_g_pre, m_g_post, m_lb_param, m_g_head, m_sinks),
        _pack_small(v_g_pre, v_g_post, v_lb_param, v_g_head, v_sinks))
    loss_all = gsum[3, HG_HEAD_DIM + ATT_HEADS]
    return (loss_all, dx.reshape(B, S, D_MODEL), grad_w_in, grad_w_out, *_unpack_small(gs),
            d_w_in, d_w_out, *_unpack_small(ds), nm_w_in, nm_w_out, *_unpack_small(ms),
            nv_w_in, nv_w_out, *_unpack_small(vs))
```

```python
import functools
import math

import jax
import jax.numpy as jnp
from jax import lax
from jax.experimental import pallas as pl
from jax.experimental.pallas import tpu as pltpu

f32 = jnp.float32
bf16 = jnp.bfloat16

D_MODEL = 1024
DEPTH = 2
HG_WIDTH = 1024
HG_HEAD_DIM = 128
HG_HEADS = 8
CHUNK = 64
SUB = 16
ATT_WIDTH = 1024
ATT_HEAD_DIM = 64
ATT_HEADS = 16
ATT_GROUP = 8
KV_WIDTH = 128
ATT_BLOCK = 128
ATT_SCALE = 1.0 / math.sqrt(ATT_HEAD_DIM)
ROPE_THETA = 10000.0
IN_WIDTH = 6400
MIX_WIDTH = 2048
NORM_EPS = 1e-6
NEG_INF = -1e30
LB_FLOOR = 1e-20
LANES = 128
VMEM_LIMIT = 48 * 1024 * 1024

ADAM_LR = 0.001
ADAM_B1 = 0.9
ADAM_B2 = 0.999
ADAM_EPS = 1e-08
ADAM_WD = 0.01
ADAM_STEP = 10

QA_BLK, ZA_BLK, KV_BLK = 4, 5, 24

NT = (((1,), (1,)), ((), ()))
TN = (((0,), (0,)), ((), ()))


def _dot(a, b, dims=None, precision=None):
    if dims is None:
        return jnp.dot(a, b, preferred_element_type=f32, precision=precision)
    return lax.dot_general(a, b, dims, preferred_element_type=f32, precision=precision)


def _sigmoid(x):
    return 1.0 / (1.0 + jnp.exp(-x))


def _params(*sem):
    return pltpu.CompilerParams(dimension_semantics=sem, vmem_limit_bytes=VMEM_LIMIT)


TAIL = IN_WIDTH - 5120


def _in_proj(x, g, wt, tail, l, *, tm=1024):
    T = x.shape[0]
    tm = min(tm, T)
    nmain = 5120 // TAIL

    def body(x_ref, g_ref, w_ref, t_ref, p_ref, h_ref, hs):
        j = pl.program_id(1)

        @pl.when(j == 0)
        def _():
            xv = x_ref[...]
            r = lax.rsqrt(jnp.mean(xv * xv, axis=-1, keepdims=True) + NORM_EPS)
            hv = (xv * r * g_ref[...]).astype(bf16)
            hs[...] = hv
            h_ref[...] = hv

        @pl.when(j < nmain)
        def _():
            p_ref[...] = _dot(hs[...], w_ref[...], NT)

        @pl.when(j == nmain)
        def _():
            p_ref[...] = _dot(hs[...], t_ref[...], NT)

    return pl.pallas_call(
        body, name="in_proj", grid=(T // tm, nmain + 1),
        in_specs=[pl.BlockSpec((tm, D_MODEL), lambda i, j: (i, 0)),
                  pl.BlockSpec((1, D_MODEL), lambda i, j: (0, 0)),
                  pl.BlockSpec((None, TAIL, D_MODEL), lambda i, j: (l, jnp.minimum(j, nmain - 1), 0)),
                  pl.BlockSpec((None, TAIL, D_MODEL), lambda i, j: (l, 0, 0))],
        out_specs=[pl.BlockSpec((tm, TAIL), lambda i, j: (i, j)),
                   pl.BlockSpec((tm, D_MODEL), lambda i, j: (i, 0))],
        out_shape=[jax.ShapeDtypeStruct((T, IN_WIDTH), f32), jax.ShapeDtypeStruct((T, D_MODEL), bf16)],
        scratch_shapes=[pltpu.VMEM((tm, D_MODEL), bf16)],
        compiler_params=_params("parallel", "arbitrary"),
    )(x, g, wt, tail)


def _out_proj(ch, ca, wo, l, x, g, *, tm=512):
    T = x.shape[0]
    tm = min(tm, T)
    half = MIX_WIDTH // 2

    def body(ch_ref, ca_ref, wo_ref, x_ref, g_ref, xn_ref, y_ref):
        y = _dot(ch_ref[...], wo_ref[0:half, :]) + _dot(ca_ref[...], wo_ref[half:MIX_WIDTH, :])
        r = lax.rsqrt(jnp.mean(y * y, axis=-1, keepdims=True) + NORM_EPS)
        y_ref[...] = y
        xn_ref[...] = x_ref[...] + y * r * g_ref[...]

    row = lambda i: (i, 0)
    fixed = lambda i: (0, 0)
    return pl.pallas_call(
        body, name="out_proj", grid=(T // tm,),
        in_specs=[pl.BlockSpec((tm, half), row), pl.BlockSpec((tm, half), row),
                  pl.BlockSpec((None, MIX_WIDTH, D_MODEL), lambda i: (l, 0, 0)), pl.BlockSpec((tm, D_MODEL), row),
                  pl.BlockSpec((1, D_MODEL), fixed)],
        out_specs=[pl.BlockSpec((tm, D_MODEL), row), pl.BlockSpec((tm, D_MODEL), row)],
        out_shape=[jax.ShapeDtypeStruct((T, D_MODEL), f32)] * 2,
        compiler_params=_params("parallel"),
    )(ch, ca, wo, x, g)


def _loss_head(y, target, *, tm=512):
    T = y.shape[0]
    tm = min(tm, T)

    def body(y_ref, t_ref, d_ref, l_ref):
        @pl.when(pl.program_id(0) == 0)
        def _():
            l_ref[...] = jnp.zeros_like(l_ref)
        err = y_ref[...] - t_ref[...]
        d_ref[...] = err * (1.0 / D_MODEL)
        l_ref[...] += jnp.sum(err * err) * (0.5 / D_MODEL)

    row = lambda i: (i, 0)
    return pl.pallas_call(
        body, name="loss_head", grid=(T // tm,),
        in_specs=[pl.BlockSpec((tm, D_MODEL), row), pl.BlockSpec((tm, D_MODEL), row)],
        out_specs=[pl.BlockSpec((tm, D_MODEL), row), pl.BlockSpec((8, LANES), lambda i: (0, 0))],
        out_shape=[jax.ShapeDtypeStruct((T, D_MODEL), f32), jax.ShapeDtypeStruct((8, LANES), f32)],
        compiler_params=_params("arbitrary"),
    )(y, target)


def _out_proj_bwd(dxn, y, g, wo, l, ch, ca, *, tm=256, ride=None):
    T = y.shape[0]
    tm = min(tm, T)
    half = MIX_WIDTH // 2

    def body(dx_ref, y_ref, g_ref, wo_ref, ch_ref, ca_ref, dch_ref, dca_ref, dwo_ref, dg_ref):
        @pl.when(pl.program_id(0) == 0)
        def _():
            dwo_ref[...] = jnp.zeros_like(dwo_ref)
            dg_ref[...] = jnp.zeros_like(dg_ref)
        y = y_ref[...]
        dx = dx_ref[...]
        r = lax.rsqrt(jnp.mean(y * y, axis=-1, keepdims=True) + NORM_EPS)
        gy = dx * g_ref[...]
        dy = r * gy - y * (r * r * r) * jnp.mean(gy * y, axis=-1, keepdims=True)
        dg_ref[...] += jnp.sum(dx * y * r, axis=0, keepdims=True)
        dyb = dy.astype(bf16)
        dch_ref[...] = _dot(dyb, wo_ref[0:half, :], NT)
        dca_ref[...] = _dot(dyb, wo_ref[half:MIX_WIDTH, :], NT)
        dwo_ref[0:half, :] += _dot(ch_ref[...], dyb, TN)
        dwo_ref[half:MIX_WIDTH, :] += _dot(ca_ref[...], dyb, TN)

    row = lambda i: (i, 0)
    fixed = lambda i: (0, 0)
    return _call(
        body, (dxn, y, g, wo, ch, ca), name="out_proj_bwd", grid=(T // tm,),
        in_specs=[pl.BlockSpec((tm, D_MODEL), row), pl.BlockSpec((tm, D_MODEL), row),
                  pl.BlockSpec((1, D_MODEL), fixed), pl.BlockSpec((None, MIX_WIDTH, D_MODEL), lambda i: (l, 0, 0)),
                  pl.BlockSpec((tm, half), row), pl.BlockSpec((tm, half), row)],
        out_specs=[pl.BlockSpec((tm, half), row), pl.BlockSpec((tm, half), row),
                   pl.BlockSpec((MIX_WIDTH, D_MODEL), fixed), pl.BlockSpec((1, D_MODEL), fixed)],
        out_shape=[jax.ShapeDtypeStruct((T, half), f32), jax.ShapeDtypeStruct((T, half), f32),
                   jax.ShapeDtypeStruct((MIX_WIDTH, D_MODEL), f32), jax.ShapeDtypeStruct((1, D_MODEL), f32)],
        semantics=("arbitrary",), ride=ride)


TILE = 256
PIECE_TILES = (4, 4, 4, 4, 4, 1, 4)
PIECE_START = tuple(sum(PIECE_TILES[:p]) for p in range(len(PIECE_TILES)))
N_TILES = sum(PIECE_TILES)


def _piece_specs(rows, index):
    def spec(s, n):
        def index_map(*g):
            r, t = index(*g)
            return r, jnp.clip(t - s, 0, n - 1)
        return pl.BlockSpec((rows, TILE), index_map)
    return [spec(s, n) for s, n in zip(PIECE_START, PIECE_TILES)]


def _for_piece(t, fn):
    for p, (s, n) in enumerate(zip(PIECE_START, PIECE_TILES)):
        pl.when((t >= s) & (t < s + n))(functools.partial(fn, p))


def _in_proj_bwd(pieces, wt, tail, l, x, g, dxn, *, tm=512, ride=None):
    T = x.shape[0]
    tm = min(tm, T)
    npc = len(pieces)
    nk = npc
    nmain = npc - 2
    wide = ATT_WIDTH

    def body(*refs):
        dp_refs = refs[:npc]
        w_ref, tz_ref, tkv_ref, x_ref, g_ref, dxn_ref, dx_ref, dg_ref, acc = refs[npc:]
        i, k = pl.program_id(0), pl.program_id(1)

        @pl.when((i == 0) & (k == 0))
        def _():
            dg_ref[...] = jnp.zeros_like(dg_ref)

        @pl.when(k == 0)
        def _():
            acc[...] = jnp.zeros_like(acc)

        for p in range(npc):
            w_p = w_ref if p < nmain else (tkv_ref if p == nmain else tz_ref)

            def add(p=p, w_p=w_p):
                acc[...] += _dot(dp_refs[p][...], w_p[...])
            pl.when(k == p)(add)

        @pl.when(k == nk - 1)
        def _():
            dh = acc[...]
            xv = x_ref[...]
            r = lax.rsqrt(jnp.mean(xv * xv, axis=-1, keepdims=True) + NORM_EPS)
            gy = dh * g_ref[...]
            dx_ref[...] = dxn_ref[...] + r * gy - xv * (r * r * r) * jnp.mean(gy * xv, axis=-1, keepdims=True)
            dg_ref[...] += jnp.sum(dh * xv * r, axis=0, keepdims=True)

    return _call(
        body, (*pieces, wt, tail, tail, x, g, dxn), name="in_proj_bwd", grid=(T // tm, nk),
        in_specs=[pl.BlockSpec((tm, p.shape[1]), lambda i, k: (i, 0)) for p in pieces] + [
            pl.BlockSpec((None, wide, D_MODEL), lambda i, k: (l, jnp.minimum(k, nmain - 1), 0)),
            pl.BlockSpec((None, wide, D_MODEL), lambda i, k: (l, 0, 0)),
            pl.BlockSpec((None, TAIL - wide, D_MODEL), lambda i, k: (l, wide // (TAIL - wide), 0)),
            pl.BlockSpec((tm, D_MODEL), lambda i, k: (i, 0)), pl.BlockSpec((1, D_MODEL), lambda i, k: (0, 0)),
            pl.BlockSpec((tm, D_MODEL), lambda i, k: (i, 0))],
        out_specs=[pl.BlockSpec((tm, D_MODEL), lambda i, k: (i, 0)), pl.BlockSpec((1, D_MODEL), lambda i, k: (0, 0))],
        out_shape=[jax.ShapeDtypeStruct((T, D_MODEL), f32), jax.ShapeDtypeStruct((1, D_MODEL), f32)],
        scratch_shapes=[pltpu.VMEM((tm, D_MODEL), f32)],
        semantics=("arbitrary", "arbitrary"), ride=ride)


def _grad_w_in(h, pieces):
    T = h.shape[0]
    npc = len(pieces)

    def body(*refs):
        h_ref, dp_refs, o_ref = refs[0], refs[1:1 + npc], refs[1 + npc]

        def put(p):
            o_ref[...] = _dot(dp_refs[p][...], h_ref[...], TN)
        _for_piece(pl.program_id(0), put)

    return pl.pallas_call(
        body, name="grad_w_in", grid=(N_TILES,),
        in_specs=[pl.BlockSpec((T, D_MODEL), lambda j: (0, 0), pipeline_mode=pl.Buffered(1))]
        + _piece_specs(T, lambda j: (0, j)),
        out_specs=pl.BlockSpec((TILE, D_MODEL), lambda j: (j, 0)),
        out_shape=jax.ShapeDtypeStruct((IN_WIDTH, D_MODEL), f32),
        compiler_params=_params("parallel"),
    )(h, *pieces)


def _lower_bound(lbp, layer):
    m = jnp.max(lbp, axis=0, keepdims=True)
    e = jnp.exp(lbp - m)
    p = e / jnp.sum(e, axis=0, keepdims=True)
    acc = p[0:1]
    for i in range(1, layer + 1):
        acc = acc + p[i:i + 1]
    return acc - p[0:1]


def _gate_parts(qr, fr, lb, lbf):
    sq = _sigmoid(qr)
    e = jnp.exp(-jnp.abs(fr))
    inv = 1.0 / (1.0 + e)
    pos = fr >= 0
    sg = jnp.where(pos, inv, e * inv)
    nsg = jnp.where(pos, e * inv, inv)
    fg = lbf + (1.0 - lb) * sg
    return qr * sq, sq, sg, nsg, fg, jnp.log(fg), (1.0 - lb) * nsg


def _anchor_masks(transposed=False):
    t = lax.broadcasted_iota(jnp.int32, (CHUNK, CHUNK), 1 if transposed else 0)
    s = lax.broadcasted_iota(jnp.int32, (CHUNK, CHUNK), 0 if transposed else 1)
    anchors = tuple(range(SUB - 1, CHUNK - 1, SUB))
    return anchors, [(t > a) & (s <= a) & (s > a - SUB) for a in anchors]


def _seg_sum(seg, x):
    hi = x.astype(bf16)
    return _dot(seg, hi) + _dot(seg, (x - hi.astype(f32)).astype(bf16))


def _hgrn_fwd(proj, lb_param, g_head, *, B, S, layer, ride=None):
    T = B * S
    TB = min(512, S)
    nT, NC = S // TB, TB // CHUNK
    nC = S // CHUNK
    HD = HG_HEAD_DIM

    def body(q_ref, f_ref, i_ref, z_ref, lb_ref, gh_ref, cat_ref, op_ref, st_ref,
             s_scr, b_scr, k_scr):
        @pl.when(pl.program_id(2) == 0)
        def _():
            s_scr[...] = jnp.zeros_like(s_scr)
        lb = _lower_bound(lb_ref[...], layer)
        lbf = jnp.maximum(lb, LB_FLOOR)
        gh = gh_ref[...]
        r_i = lax.broadcasted_iota(jnp.int32, (CHUNK, CHUNK), 0)
        c_i = lax.broadcasted_iota(jnp.int32, (CHUNK, CHUNK), 1)
        tril = (r_i >= c_i).astype(f32)
        rows8 = lax.broadcasted_iota(jnp.int32, (8, HD), 0)
        lane_c = lax.broadcasted_iota(jnp.int32, (8, CHUNK), 1)
        anchors, masks = _anchor_masks()

        def chunk(c, st):
            rs = slice(c * CHUNK, (c + 1) * CHUNK)
            b_s, k_s = b_scr.at[c], k_scr.at[c]
            q, _, _, _, _, logf, k = _gate_parts(q_ref[rs, :], f_ref[rs, :], lb, lbf)
            v = i_ref[rs, :]
            b = _dot(tril, logf, precision=lax.Precision.HIGHEST)
            b_s[...] = b
            k_s[...] = k
            pieces = []
            for blk in range(CHUNK // SUB):
                r0 = blk * SUB
                bp = [b[r0 + 8 * i:r0 + 8 * i + 8] for i in range(SUB // 8)]
                qp = [q[r0 + 8 * i:r0 + 8 * i + 8] for i in range(SUB // 8)]
                ap = [jnp.zeros((8, CHUNK), f32) for _ in range(SUB // 8)]
                for s in range(SUB):
                    bs = b_s[r0 + s:r0 + s + 1, :]
                    ks = k_s[r0 + s:r0 + s + 1, :]
                    for i in range(s // 8, SUB // 8):
                        diff = bp[i] - bs
                        if i == s // 8:
                            diff = jnp.where(rows8 >= s - 8 * i, diff, NEG_INF)
                        col = jnp.sum(jnp.exp(diff) * qp[i] * ks, axis=1, keepdims=True)
                        ap[i] = jnp.where(lane_c == r0 + s, col, ap[i])
                pieces += ap
            a_all = jnp.concatenate(pieces, axis=0)
            for an, mk in zip(anchors, masks):
                beta = b_s[an:an + 1, :]
                qh = (q * jnp.exp(jnp.minimum(b - beta, 0.0))).astype(bf16)
                kh = (k * jnp.exp(jnp.minimum(beta - b, 0.0))).astype(bf16)
                a_all = a_all + jnp.where(mk, _dot(qh, kh, NT), 0.0)
            st_ref[0, 0, c] = st
            vb16 = v.astype(bf16)
            o = _dot(a_all.astype(bf16), vb16) + _dot((q * jnp.exp(b)).astype(bf16), st.astype(bf16), NT)
            b_end = b_s[CHUNK - 1:CHUNK, :]
            kdec = (k * jnp.exp(b_end - b)).astype(bf16)
            st_next = jnp.exp(b_end) * st + _dot(vb16, kdec, TN)
            rr = lax.rsqrt(jnp.mean(o * o, axis=-1, keepdims=True) + NORM_EPS)
            zr = z_ref[rs, :]
            cat_ref[rs, :] = (o * rr * gh * (zr * _sigmoid(zr))).astype(bf16)
            op_ref[rs, :] = o
            return st_next

        st = s_scr[...]
        for c in range(NC):
            st = chunk(c, st)
        s_scr[...] = st

    def col(part):
        return pl.BlockSpec((TB, HD), lambda b, h, n: (b * nT + n, part * HG_HEADS + h))

    out_col = pl.BlockSpec((TB, HD), lambda b, h, n: (b * nT + n, h))
    return _call(
        body, (proj, proj, proj, proj, lb_param, g_head),
        name=f"hgrn_fwd_l{layer}", grid=(B, HG_HEADS, nT),
        in_specs=[col(0), col(1), col(2), col(3),
                  pl.BlockSpec((DEPTH, HD), lambda b, h, n: (0, h)),
                  pl.BlockSpec((1, HD), lambda b, h, n: (0, 0))],
        out_specs=[out_col, out_col,
                   pl.BlockSpec((1, 1, NC, HD, HD), lambda b, h, n: (b, h, n, 0, 0))],
        out_shape=[jax.ShapeDtypeStruct((T, HG_WIDTH), bf16), jax.ShapeDtypeStruct((T, HG_WIDTH), f32),
                   jax.ShapeDtypeStruct((B, HG_HEADS, nC, HD, HD), f32)],
        scratch_shapes=[pltpu.VMEM((HD, HD), f32), pltpu.VMEM((NC, CHUNK, HD), f32), pltpu.VMEM((NC, CHUNK, HD), f32)],
        semantics=("parallel", "parallel", "arbitrary"), ride=ride)


def _hgrn_bwd(proj, lb_param, g_head, o_pre, states, dcat, *, B, S, layer, ride=None):
    T = B * S
    TB = min(512, S)
    nT, NC = S // TB, TB // CHUNK
    HD = HG_HEAD_DIM

    def body(q_ref, f_ref, i_ref, z_ref, lb_ref, gh_ref, op_ref, st_ref, dc_ref,
             dq_ref, df_ref, di_ref, dz_ref, dlb_ref, dgh_ref,
             ds_scr, b_scr, q_scr, do_scr, wk_scr):
        @pl.when(pl.program_id(2) == 0)
        def _():
            ds_scr[...] = jnp.zeros_like(ds_scr)
            dlb_ref[...] = jnp.zeros_like(dlb_ref)
            dgh_ref[...] = jnp.zeros_like(dgh_ref)
        lb = _lower_bound(lb_ref[...], layer)
        lbf = jnp.maximum(lb, LB_FLOOR)
        ind = (lb > LB_FLOOR).astype(f32)
        gh = gh_ref[...]
        r_i = lax.broadcasted_iota(jnp.int32, (CHUNK, CHUNK), 0)
        c_i = lax.broadcasted_iota(jnp.int32, (CHUNK, CHUNK), 1)
        tril = (r_i >= c_i).astype(f32)
        triu = (c_i >= r_i).astype(f32)
        rows8 = lax.broadcasted_iota(jnp.int32, (8, HD), 0)
        lane_c = lax.broadcasted_iota(jnp.int32, (8, CHUNK), 1)
        last_row = lax.broadcasted_iota(jnp.int32, (CHUNK, HD), 0) == CHUNK - 1
        anchors, masks = _anchor_masks()
        _, masks_t = _anchor_masks(transposed=True)
        seg_t = lax.broadcasted_iota(jnp.int32, (SUB, 8 * SUB), 0)
        seg_r = lax.broadcasted_iota(jnp.int32, (SUB, 8 * SUB), 1) // 8
        seg0 = (seg_r == seg_t).astype(bf16)
        seg1 = (seg_r[:, 0:4 * SUB] + 8 == seg_t[:, 0:4 * SUB]).astype(bf16)

        def chunk(c, dst1):
            rs = slice(c * CHUNK, (c + 1) * CHUNK)
            b_s, q_s, do_s = b_scr.at[c], q_scr.at[c], do_scr.at[c]
            qr, fr = q_ref[rs, :], f_ref[rs, :]
            q, sq, sg, nsg, fg, logf, k = _gate_parts(qr, fr, lb, lbf)
            v = i_ref[rs, :]
            b = _dot(tril, logf, precision=lax.Precision.HIGHEST)
            o = op_ref[rs, :]
            dc = dc_ref[rs, :]
            zr = z_ref[rs, :]
            sz = _sigmoid(zr)
            rr = lax.rsqrt(jnp.mean(o * o, axis=-1, keepdims=True) + NORM_EPS)
            dz_ref[rs, :] = (dc * (o * rr * gh) * (sz * (1.0 + zr * (1.0 - sz)))).astype(bf16)
            dn = dc * (zr * sz)
            dgh_ref[0, 0] += jnp.sum(dn * o * rr, axis=0, keepdims=True)
            gdn = dn * gh
            d_o = rr * gdn - o * (rr * rr * rr) * jnp.mean(gdn * o, axis=-1, keepdims=True)
            b_s[...] = b
            q_s[...] = q
            do_s[...] = d_o
            dob = d_o.astype(bf16)
            vb16 = v.astype(bf16)
            d_a = _dot(dob, vb16, NT)
            d_q = jnp.zeros((CHUNK, HD), f32)
            d_k = jnp.zeros((CHUNK, HD), f32)
            at_all = jnp.zeros((CHUNK, CHUNK), f32)
            for an, mk, mkt in zip(anchors, masks, masks_t):
                beta = b_s[an:an + 1, :]
                eq = jnp.exp(jnp.minimum(b - beta, 0.0))
                ek = jnp.exp(jnp.minimum(beta - b, 0.0))
                qh = (q * eq).astype(bf16)
                kh = (k * ek).astype(bf16)
                at_all = at_all + jnp.where(mkt, _dot(kh, qh, NT), 0.0)
                d_aa = jnp.where(mk, d_a, 0.0).astype(bf16)
                d_q = d_q + _dot(d_aa, kh) * eq
                d_k = d_k + _dot(d_aa, qh, TN) * ek
            st0 = st_ref[0, 0, c]
            dst1b = dst1.astype(bf16)
            eb = jnp.exp(b)
            b_end = b_s[CHUNK - 1:CHUNK, :]
            edec = jnp.exp(b_end - b)
            e_end = jnp.exp(b_end)
            kdec = (k * edec).astype(bf16)
            qdec = (q * eb).astype(bf16)
            d_q = d_q + _dot(dob, st0.astype(bf16)) * eb
            d_v = _dot(kdec, dst1b, NT)
            d_k = d_k + _dot(vb16, dst1b) * edec
            st1 = e_end * st0 + _dot(vb16, kdec, TN)
            rterm = jnp.sum(dst1 * st1, axis=0, keepdims=True)
            dst0 = e_end * dst1 + _dot(dob, qdec, TN)
            dq_blocks, dk_pieces, at_pieces = [], [], []
            for blk in range(CHUNK // SUB):
                r0 = blk * SUB
                wk = wk_scr.at[c * (CHUNK // SUB) + blk]
                bp = [b[r0 + 8 * i:r0 + 8 * i + 8] for i in range(SUB // 8)]
                kp = [k[r0 + 8 * i:r0 + 8 * i + 8] for i in range(SUB // 8)]
                vp = [v[r0 + 8 * i:r0 + 8 * i + 8] for i in range(SUB // 8)]
                dkp = [jnp.zeros((8, HD), f32) for _ in range(SUB // 8)]
                atp = [jnp.zeros((8, CHUNK), f32) for _ in range(SUB // 8)]
                for t in range(SUB):
                    bt = b_s[r0 + t:r0 + t + 1, :]
                    qt = q_s[r0 + t:r0 + t + 1, :]
                    dot_ = do_s[r0 + t:r0 + t + 1, :]
                    for i in range(t // 8 + 1):
                        diff = bt - bp[i]
                        if i == t // 8:
                            diff = jnp.where(rows8 <= t - 8 * i, diff, NEG_INF)
                        e = jnp.exp(diff)
                        a = jnp.sum(e * kp[i] * qt, axis=1, keepdims=True)
                        atp[i] = jnp.where(lane_c == r0 + t, a, atp[i])
                        w = jnp.sum(vp[i] * dot_, axis=1, keepdims=True) * e
                        dkp[i] = dkp[i] + w * qt
                        row = 8 * t if i == 0 else 8 * SUB + 8 * (t - 8)
                        wk[row:row + 8, :] = w * kp[i]
                dq_blocks.append(_seg_sum(seg0, wk[0:8 * SUB, :]) + _seg_sum(seg1, wk[8 * SUB:12 * SUB, :]))
                dk_pieces += dkp
                at_pieces += atp
            d_q = d_q + jnp.concatenate(dq_blocks, axis=0)
            d_k = d_k + jnp.concatenate(dk_pieces, axis=0)
            d_v = d_v + _dot((at_all + jnp.concatenate(at_pieces, axis=0)).astype(bf16), dob)
            db = q * d_q - k * d_k + jnp.where(last_row, rterm, 0.0)
            dlt = _dot(triu, db, precision=lax.Precision.HIGHEST) - fg * d_k
            df_ref[rs, :] = (dlt * (1.0 - lb) * sg * nsg / fg).astype(bf16)
            dlb_ref[0] += jnp.sum(dlt * (ind - sg) / fg, axis=0, keepdims=True)
            dq_ref[rs, :] = (d_q * (sq * (1.0 + qr * (1.0 - sq)))).astype(bf16)
            di_ref[rs, :] = d_v.astype(bf16)
            return dst0

        dst = ds_scr[...]
        for c in reversed(range(NC)):
            dst = chunk(c, dst)
        ds_scr[...] = dst

    def col(part):
        return pl.BlockSpec((TB, HD), lambda b, h, n: (b * nT + nT - 1 - n, part * HG_HEADS + h))

    hcol = pl.BlockSpec((TB, HD), lambda b, h, n: (b * nT + nT - 1 - n, h))
    return _call(
        body, (proj, proj, proj, proj, lb_param, g_head, o_pre, states, dcat),
        name=f"hgrn_bwd_l{layer}", grid=(B, HG_HEADS, nT),
        in_specs=[col(0), col(1), col(2), col(3),
                  pl.BlockSpec((DEPTH, HD), lambda b, h, n: (0, h)),
                  pl.BlockSpec((1, HD), lambda b, h, n: (0, 0)),
                  hcol,
                  pl.BlockSpec((1, 1, NC, HD, HD), lambda b, h, n: (b, h, nT - 1 - n, 0, 0)),
                  hcol],
        out_specs=[hcol, hcol, hcol, hcol,
                   pl.BlockSpec((1, 1, HD), lambda b, h, n: (b, 0, h)),
                   pl.BlockSpec((1, 1, 1, HD), lambda b, h, n: (b, h, 0, 0))],
        out_shape=[jax.ShapeDtypeStruct((T, HG_WIDTH), bf16)] * 4 + [
            jax.ShapeDtypeStruct((B, 1, HG_WIDTH), f32), jax.ShapeDtypeStruct((B, HG_HEADS, 1, HD), f32)],
        scratch_shapes=[pltpu.VMEM((HD, HD), f32)] + [pltpu.VMEM((NC, CHUNK, HD), f32)] * 3
        + [pltpu.VMEM((NC * CHUNK // SUB, 12 * SUB, HD), f32)],
        semantics=("parallel", "parallel", "arbitrary"), ride=ride)


def _rope_tables(S):
    half = ATT_HEAD_DIM // 2
    inv_freq = ROPE_THETA ** (-jnp.arange(half, dtype=f32) / half)
    ang = jnp.arange(S, dtype=f32)[:, None] * inv_freq[None, :]
    cos, sin = jnp.cos(ang), jnp.sin(ang)
    return jnp.tile(jnp.concatenate([cos, cos], axis=1), (1, 2)), jnp.tile(jnp.concatenate([-sin, sin], axis=1), (1, 2))


def _swap_halves(x, first_half):
    return jnp.where(first_half, pltpu.roll(x, LANES - ATT_HEAD_DIM // 2, 1), pltpu.roll(x, ATT_HEAD_DIM // 2, 1))


def _rope(x, cos, sin, first_half):
    return x * cos + _swap_halves(x, first_half) * sin


def _rope_bwd(dy, cos, sin, first_half):
    return dy * cos + _swap_halves(dy * sin, first_half)


def _attn_consts(n):
    lane = lax.broadcasted_iota(jnp.int32, (1, LANES), 1)
    low = lane < ATT_HEAD_DIM
    first_half = (lane % ATT_HEAD_DIM) < ATT_HEAD_DIM // 2
    top = lax.broadcasted_iota(jnp.int32, (LANES, 1), 0) < ATT_HEAD_DIM
    s = lax.broadcasted_iota(jnp.int32, (2 * ATT_BLOCK, ATT_BLOCK), 0)
    t = lax.broadcasted_iota(jnp.int32, (2 * ATT_BLOCK, ATT_BLOCK), 1)
    mask = (s > t) & (s <= t + ATT_BLOCK) & ((s >= ATT_BLOCK) | (n > 0))
    return low, first_half, top, mask


def _dup_kv(x, low):
    rolled = pltpu.roll(x, ATT_HEAD_DIM, 1)
    return [jnp.where(low, x, rolled), jnp.where(low, rolled, x)]


def _attn_head(qtm, kd, vdt, sink, mask):
    s = jnp.where(mask, _dot(kd, qtm) * ATT_SCALE, NEG_INF)
    m = jnp.maximum(jnp.max(s, axis=0, keepdims=True), sink)
    p = jnp.exp(s - m)
    psink = jnp.exp(sink - m)
    inv = 1.0 / (jnp.sum(p, axis=0, keepdims=True) + psink)
    pn = p * inv
    return pn, psink * inv, _dot(vdt, pn.astype(bf16))


def _swa_fwd(proj, sink_b, cos, sin, *, B, S, ride=None):
    T = B * S
    L = ATT_BLOCK
    nB = S // L

    def body(q_ref, z_ref, kvc_ref, kvp_ref, sk_ref, cc_ref, sc_ref, cp_ref, sp_ref, cat_ref):
        n = pl.program_id(1)
        low, first_half, top, mask = _attn_consts(n)
        cc, sc = cc_ref[...], sc_ref[...]
        kc = _rope(kvc_ref[:, 0:LANES], cc, sc, first_half)
        kp = _rope(kvp_ref[:, 0:LANES], cp_ref[...], sp_ref[...], first_half)
        kd = [x.astype(bf16) for x in _dup_kv(jnp.concatenate([kp, kc], axis=0), low)]
        vdt = [x.T.astype(bf16) for x in _dup_kv(jnp.concatenate([kvp_ref[:, LANES:2 * LANES], kvc_ref[:, LANES:2 * LANES]], axis=0), low)]
        for pair in range(ATT_HEADS // 2):
            cols = slice(pair * LANES, (pair + 1) * LANES)
            j = (2 * pair) // ATT_GROUP
            qt = _rope(q_ref[:, cols], cc, sc, first_half).T
            outs = []
            for hh in range(2):
                h = 2 * pair + hh
                qtm = jnp.where(top if hh == 0 else ~top, qt, 0.0).astype(bf16)
                outs.append(_attn_head(qtm, kd[j], vdt[j], sk_ref[h:h + 1, 0:1], mask)[2])
            zp = z_ref[:, cols]
            cat_ref[:, cols] = (jnp.where(top, outs[0], outs[1]).T * (zp * _sigmoid(zp))).astype(bf16)

    cur = lambda b, n: (b * nB + n, 0)
    return _call(
        body, (proj, proj, proj, proj, sink_b, cos, sin, cos, sin), name="swa_fwd", grid=(B, nB),
        in_specs=[pl.BlockSpec((L, ATT_WIDTH), lambda b, n: (b * nB + n, QA_BLK)),
                  pl.BlockSpec((L, ATT_WIDTH), lambda b, n: (b * nB + n, ZA_BLK)),
                  pl.BlockSpec((L, 2 * KV_WIDTH), lambda b, n: (b * nB + n, KV_BLK)),
                  pl.BlockSpec((L, 2 * KV_WIDTH), lambda b, n: (b * nB + jnp.maximum(n - 1, 0), KV_BLK)),
                  pl.BlockSpec((ATT_HEADS, LANES), lambda b, n: (0, 0)),
                  pl.BlockSpec((L, LANES), lambda b, n: (n, 0)), pl.BlockSpec((L, LANES), lambda b, n: (n, 0)),
                  pl.BlockSpec((L, LANES), lambda b, n: (jnp.maximum(n - 1, 0), 0)),
                  pl.BlockSpec((L, LANES), lambda b, n: (jnp.maximum(n - 1, 0), 0))],
        out_specs=[pl.BlockSpec((L, ATT_WIDTH), cur)],
        out_shape=[jax.ShapeDtypeStruct((T, ATT_WIDTH), bf16)],
        semantics=("parallel", "parallel"), ride=ride)


def _swa_bwd(proj, sink_b, cos, sin, dcat, *, B, S):
    T = B * S
    L = ATT_BLOCK
    nB = S // L

    def body(q_ref, z_ref, kvc_ref, kvp_ref, sk_ref, cc_ref, sc_ref, cp_ref, sp_ref, dc_ref,
             dq_ref, dz_ref, dkv_ref, dsk_ref, carry, ds_st, pn_st, q_st, do_st):
        step = pl.program_id(1)
        n = nB - 1 - step

        @pl.when((pl.program_id(0) == 0) & (step == 0))
        def _():
            dsk_ref[...] = jnp.zeros_like(dsk_ref)

        @pl.when(step == 0)
        def _():
            carry[...] = jnp.zeros_like(carry)
        low, first_half, top, mask = _attn_consts(n)
        cc, sc, cp, sp = cc_ref[...], sc_ref[...], cp_ref[...], sp_ref[...]
        kc = _rope(kvc_ref[:, 0:LANES], cc, sc, first_half)
        kp = _rope(kvp_ref[:, 0:LANES], cp, sp, first_half)
        kdf = _dup_kv(jnp.concatenate([kp, kc], axis=0), low)
        vdf = _dup_kv(jnp.concatenate([kvp_ref[:, LANES:2 * LANES], kvc_ref[:, LANES:2 * LANES]], axis=0), low)
        kd = [x.astype(bf16) for x in kdf]
        vd = [x.astype(bf16) for x in vdf]
        kdt = [x.T.astype(bf16) for x in kdf]
        vdt = [x.T.astype(bf16) for x in vdf]
        dkd, dvd = [], []
        for pair in range(ATT_HEADS // 2):
            cols = slice(pair * LANES, (pair + 1) * LANES)
            j = (2 * pair) // ATT_GROUP
            qp = _rope(q_ref[:, cols], cc, sc, first_half)
            qt = qp.T
            zp = z_ref[:, cols]
            dc = dc_ref[:, cols]
            sz = _sigmoid(zp)
            d_o = dc * (zp * sz)
            dot_ = d_o.T
            res = []
            for hh in range(2):
                rsel = top if hh == 0 else ~top
                qtm = jnp.where(rsel, qt, 0.0).astype(bf16)
                pn, psn, o = _attn_head(qtm, kd[j], vdt[j], sk_ref[2 * pair + hh:2 * pair + hh + 1, 0:1], mask)
                res.append((rsel, pn, psn, o))
            ot = jnp.where(top, res[0][3], res[1][3])
            dz_ref[:, cols] = (dc * ot.T * (sz * (1.0 + zp * (1.0 - sz)))).astype(bf16)
            dqts = []
            for hh in range(2):
                h = 2 * pair + hh
                rsel, pn, psn, _ = res[hh]
                lsel = low if hh == 0 else ~low
                dotm = jnp.where(rsel, dot_, 0.0)
                delta = jnp.sum(dotm * ot, axis=0, keepdims=True)
                dst = (pn * (_dot(vd[j], dotm.astype(bf16)) - delta) * ATT_SCALE).astype(bf16)
                dsk_ref[h:h + 1, :] += jnp.zeros((1, LANES), f32) - jnp.sum(psn * delta)
                dqts.append(_dot(kdt[j], dst))
                g = h % ATT_GROUP
                ds_st[:, g * LANES:(g + 1) * LANES] = dst
                pn_st[:, g * LANES:(g + 1) * LANES] = pn.astype(bf16)
                q_st[g * LANES:(g + 1) * LANES, :] = jnp.where(lsel, qp, 0.0).astype(bf16)
                do_st[g * LANES:(g + 1) * LANES, :] = jnp.where(lsel, d_o, 0.0).astype(bf16)
            dq_ref[:, cols] = _rope_bwd(jnp.where(top, dqts[0], dqts[1]).T, cc, sc, first_half).astype(bf16)
            if (2 * pair + 2) % ATT_GROUP == 0:
                dkd.append(_dot(ds_st[...], q_st[...]))
                dvd.append(_dot(pn_st[...], do_st[...]))
        dk = [x + pltpu.roll(x, ATT_HEAD_DIM, 1) for x in dkd]
        dv = [x + pltpu.roll(x, ATT_HEAD_DIM, 1) for x in dvd]
        dk = jnp.where(low, dk[0], dk[1])
        dv = jnp.where(low, dv[0], dv[1])
        dkv_ref[:, 0:LANES] = (_rope_bwd(dk[L:2 * L], cc, sc, first_half) + carry[:, 0:LANES]).astype(bf16)
        dkv_ref[:, LANES:2 * LANES] = (dv[L:2 * L] + carry[:, LANES:2 * LANES]).astype(bf16)
        carry[:, 0:LANES] = _rope_bwd(dk[0:L], cp, sp, first_half)
        carry[:, LANES:2 * LANES] = dv[0:L]

    rev = lambda b, s: b * nB + nB - 1 - s
    revp = lambda b, s: b * nB + jnp.maximum(nB - 2 - s, 0)
    wide = lambda blk: pl.BlockSpec((L, ATT_WIDTH), lambda b, s: (rev(b, s), blk))
    tab = pl.BlockSpec((L, LANES), lambda b, s: (nB - 1 - s, 0))
    tabp = pl.BlockSpec((L, LANES), lambda b, s: (jnp.maximum(nB - 2 - s, 0), 0))
    return pl.pallas_call(
        body, name="swa_bwd", grid=(B, nB),
        in_specs=[wide(QA_BLK), wide(ZA_BLK),
                  pl.BlockSpec((L, 2 * KV_WIDTH), lambda b, s: (rev(b, s), KV_BLK)),
                  pl.BlockSpec((L, 2 * KV_WIDTH), lambda b, s: (revp(b, s), KV_BLK)),
                  pl.BlockSpec((ATT_HEADS, LANES), lambda b, s: (0, 0)),
                  tab, tab, tabp, tabp, wide(0)],
        out_specs=[wide(0), wide(0), pl.BlockSpec((L, 2 * KV_WIDTH), lambda b, s: (rev(b, s), 0)),
                   pl.BlockSpec((ATT_HEADS, LANES), lambda b, s: (0, 0))],
        out_shape=[jax.ShapeDtypeStruct((T, ATT_WIDTH), bf16), jax.ShapeDtypeStruct((T, ATT_WIDTH), bf16),
                   jax.ShapeDtypeStruct((T, 2 * KV_WIDTH), bf16), jax.ShapeDtypeStruct((ATT_HEADS, LANES), f32)],
        scratch_shapes=[pltpu.VMEM((L, 2 * KV_WIDTH), f32),
                        pltpu.VMEM((2 * L, ATT_GROUP * LANES), bf16), pltpu.VMEM((2 * L, ATT_GROUP * LANES), bf16),
                        pltpu.VMEM((ATT_GROUP * LANES, LANES), bf16), pltpu.VMEM((ATT_GROUP * LANES, LANES), bf16)],
        compiler_params=_params("arbitrary", "arbitrary"),
    )(proj, proj, proj, proj, sink_b, cos, sin, cos, sin, dcat)


def _train_step(x, target, bufs, g_pre, g_post, lb_param, g_head, sinks, *, B, S, exchange):
    L = DEPTH
    ri, ro = IN_WIDTH // 8, MIX_WIDTH // 8
    cos, sin = _rope_tables(S)
    full = list(bufs)
    if exchange:
        full[0] = _run_exchange(_gather_d2d(_run_exchange(_gather_ici(bufs[0]))))
    saved = []
    for l in range(L):
        wt = full[l][0].reshape(1, IN_WIDTH, D_MODEL)
        wo = full[l][1].reshape(1, MIX_WIDTH, D_MODEL)
        tail = jnp.concatenate([wt[:, 5376:6400], wt[:, 5120:5376]], axis=1)
        proj, h = _in_proj(x, g_pre[l:l + 1], wt, tail, 0)
        ahead = exchange and l + 1 < L
        (ch, o_pre, states), landed = _hgrn_fwd(proj, lb_param, g_head[l:l + 1], B=B, S=S, layer=l,
                                                ride=_gather_ici(bufs[l + 1]) if ahead else None)
        sink_b = jnp.broadcast_to(sinks[l][:, None], (ATT_HEADS, LANES))
        (ca,), passed = _swa_fwd(proj, sink_b, cos, sin, B=B, S=S, ride=_gather_d2d(landed) if ahead else None)
        if ahead:
            full[l + 1] = passed
        xn, y = _out_proj(ch, ca, wo, 0, x, g_post[l:l + 1])
        saved.append((x, proj, h, ch, o_pre, states, sink_b, ca, y, wt, tail, wo))
        x = xn
    dx, loss = _loss_head(x, target)

    def reduce_tail(sums, recv):
        return _run_exchange(_pair_share([_chip_sum(s, r) for s, r in zip(sums, recv)]))

    grads = [None] * L
    waiting = None
    gg_pre, gg_post, g_lb, gg_head, g_sinks = [], [], [], [], []
    for l in reversed(range(L)):
        x_in, proj, h, ch, o_pre, states, sink_b, ca, y, wt, tail, wo = saved[l]
        (dch, dca, dwo, dgpost), got = _out_proj_bwd(dx, y, g_post[l:l + 1], wo, 0, ch, ca,
                                                     ride=_pair_exchange(waiting) if waiting else None)
        sums = [_pair_add(p, r) for p, r in zip(waiting, got)] if waiting else None
        (dq, df, di, dz, dlb, dgh), recv = _hgrn_bwd(proj, lb_param, g_head[l:l + 1], o_pre, states, dch, B=B, S=S,
                                                     layer=l, ride=_chip_exchange(sums) if waiting else None)
        if waiting:
            grads[l + 1] = reduce_tail(sums, recv)
        dqa, dza, dkv, dsk = _swa_bwd(proj, sink_b, cos, sin, dca, B=B, S=S)
        pieces = [dq, df, di, dz, dqa, dkv, dza]
        gwt = _grad_w_in(h, pieces)
        mine = [gwt.reshape(1, 4, 2, ri, D_MODEL), dwo.reshape(1, 4, 2, ro, D_MODEL)]
        at_end = exchange and l == 0
        (dx, dgpre), got = _in_proj_bwd(pieces, wt, tail, 0, x_in, g_pre[l:l + 1], dx,
                                        ride=_pair_exchange(mine) if at_end else None)
        if at_end:
            sums = [_pair_add(p, r) for p, r in zip(mine, got)]
            grads[0] = reduce_tail(sums, _run_exchange(_chip_exchange(sums)))
        elif exchange:
            waiting = mine
        else:
            grads[l] = [gwt, dwo]
        gg_pre.append(dgpre[0])
        gg_post.append(dgpost[0])
        g_lb.append(jnp.sum(dlb, axis=(0, 1)))
        gg_head.append(jnp.sum(dgh, axis=(0, 1, 2)))
        g_sinks.append(dsk[:, 0])
    rev = lambda xs: jnp.stack(xs[::-1])
    return loss[0, 0], dx, grads, rev(gg_pre), rev(gg_post), rev(g_lb), rev(gg_head), rev(g_sinks)


MESH = pl.DeviceIdType.MESH
ANY = pl.BlockSpec(memory_space=pl.ANY)


def _place():
    x, y, c = lax.axis_index("x"), lax.axis_index("y"), lax.axis_index("c")
    return x, y, c, [(1 - x, y), (x, 1 - y), (1 - x, 1 - y)]


def _rcopy(src, dst, send, recv, k, to):
    return pltpu.make_async_remote_copy(src_ref=src, dst_ref=dst, send_sem=send.at[k], recv_sem=recv.at[k],
                                        device_id=to, device_id_type=MESH)


class _Exchange:
    def __init__(self, name, inputs, out_shapes, n_sems, plan, in_place=False):
        self.name, self.inputs, self.out_shapes, self.n_sems, self.plan = name, inputs, out_shapes, n_sems, plan
        self.aliases = {a: a for a in range(len(inputs))} if in_place else {}

    def start(self, ins, outs, send, recv):
        for cp in self.plan(ins, outs, send, recv)[0]:
            cp.start()

    def finish(self, ins, outs, send, recv):
        sent, arriving = self.plan(ins, outs, send, recv)
        for cp in arriving:
            cp.wait_recv()
        for cp in sent:
            cp.wait_send()

    def sems(self):
        return [pltpu.SemaphoreType.DMA((self.n_sems,)), pltpu.SemaphoreType.DMA((self.n_sems,))]


def _run_exchange(ex):
    n_in, n_out = len(ex.inputs), len(ex.out_shapes)

    def body(*refs):
        ins, outs = refs[:n_in], refs[n_in:n_in + n_out]
        send, recv = refs[n_in + n_out:]
        ex.start(ins, outs, send, recv)
        ex.finish(ins, outs, send, recv)

    return pl.pallas_call(
        body, name=ex.name, in_specs=[ANY] * n_in, out_specs=[ANY] * n_out, out_shape=ex.out_shapes,
        input_output_aliases=ex.aliases, scratch_shapes=ex.sems(),
    )(*ex.inputs)


def _call(body, operands, *, name, grid, in_specs, out_specs, out_shape, scratch_shapes=(), semantics, ride=None):
    if ride is None:
        outs = pl.pallas_call(body, name=name, grid=grid, in_specs=in_specs, out_specs=out_specs, out_shape=out_shape,
                              scratch_shapes=list(scratch_shapes), compiler_params=_params(*semantics))(*operands)
        return outs, []
    n_in, n_out, n_scr = len(in_specs), len(out_specs), len(scratch_shapes)
    r_in, r_out = len(ride.inputs), len(ride.out_shapes)

    def riding(*refs):
        refs = list(refs)
        ins, rins = refs[:n_in], refs[n_in:n_in + r_in]
        o0 = n_in + r_in
        outs, routs = refs[o0:o0 + n_out], refs[o0 + n_out:o0 + n_out + r_out]
        scr = refs[o0 + n_out + r_out:o0 + n_out + r_out + n_scr]
        send, recv = refs[-2:]
        ids = [pl.program_id(d) for d in range(len(grid))]
        first = functools.reduce(jnp.logical_and, [i == 0 for i in ids])
        last = functools.reduce(jnp.logical_and, [i == g - 1 for i, g in zip(ids, grid)])
        pl.when(first)(lambda: ride.start(rins, routs, send, recv))
        body(*ins, *outs, *scr)
        pl.when(last)(lambda: ride.finish(rins, routs, send, recv))

    res = pl.pallas_call(
        riding, name=name + "_" + ride.name, grid=grid,
        in_specs=list(in_specs) + [ANY] * r_in, out_specs=list(out_specs) + [ANY] * r_out,
        out_shape=list(out_shape) + list(ride.out_shapes),
        input_output_aliases={n_in + a: n_out + b for a, b in ride.aliases.items()},
        scratch_shapes=list(scratch_shapes) + ride.sems(),
        compiler_params=_params(*(["arbitrary"] * len(grid))),
    )(*operands, *ride.inputs)
    return res[:n_out], res[n_out:]


def _gather_ici(bufs, name="gather_ici"):
    n = len(bufs)

    def plan(ins, outs, send, recv):
        x, y, c, chips = _place()
        me = 2 * x + y
        sent, arriving = [], []
        for j, (px, py) in enumerate(chips):
            for a in range(n):
                mine, theirs = outs[a].at[:, me, c], outs[a].at[:, 2 * px + py, c]
                sent.append(_rcopy(mine, mine, send, recv, j * n + a, (px, py, c)))
                arriving.append(_rcopy(theirs, theirs, send, recv, j * n + a, (px, py, c)))
        return sent, arriving

    return _Exchange(name, bufs, [jax.ShapeDtypeStruct(b.shape, b.dtype) for b in bufs], 3 * n, plan, in_place=True)


def _gather_d2d(bufs, name="gather_d2d"):
    n = len(bufs)

    def plan(ins, outs, send, recv):
        x, y, c, chips = _place()
        sib = (x, y, 1 - c)
        sent, arriving = [], []
        for j, (px, py) in enumerate(chips):
            for a in range(n):
                got, theirs = outs[a].at[:, 2 * px + py, c], outs[a].at[:, 2 * px + py, 1 - c]
                sent.append(_rcopy(got, got, send, recv, j * n + a, sib))
                arriving.append(_rcopy(theirs, theirs, send, recv, j * n + a, sib))
        return sent, arriving

    return _Exchange(name, bufs, [jax.ShapeDtypeStruct(b.shape, b.dtype) for b in bufs], 3 * n, plan, in_place=True)


def _pair_exchange(parts):
    n = len(parts)

    def plan(ins, outs, send, recv):
        x, y, c, _ = _place()
        cps = [_rcopy(ins[a].at[:, :, 1 - c], outs[a], send, recv, a, (x, y, 1 - c)) for a in range(n)]
        return cps, cps

    return _Exchange("pair_exchange", parts,
                     [jax.ShapeDtypeStruct(p.shape[:2] + p.shape[3:], p.dtype) for p in parts], n, plan)


def _block_rows(r):
    return r if r <= 512 else r // 2


def _pair_add(part, got):
    L, K, _, r, C = part.shape
    rows = _block_rows(r)

    def body(c_ref, a_ref, b_ref, o_ref):
        o_ref[0, 0] = (a_ref[0, 0, 0] + b_ref[0, 0]).astype(bf16)

    blk = (1, 1, rows, C)
    return pl.pallas_call(
        body, name="pair_add",
        grid_spec=pltpu.PrefetchScalarGridSpec(
            num_scalar_prefetch=1, grid=(L, K, r // rows),
            in_specs=[pl.BlockSpec((1, 1, 1, rows, C), lambda l, k, i, c: (l, k, c[0], i, 0)),
                      pl.BlockSpec(blk, lambda l, k, i, c: (l, k, i, 0))],
            out_specs=pl.BlockSpec(blk, lambda l, k, i, c: (l, k, i, 0))),
        out_shape=jax.ShapeDtypeStruct((L, K, r, C), bf16),
        compiler_params=_params("parallel", "parallel", "parallel"),
    )(jnp.reshape(lax.axis_index("c"), (1,)).astype(jnp.int32), part, got)


def _chip_exchange(sums):
    n = len(sums)

    def plan(ins, outs, send, recv):
        x, y, c, chips = _place()
        cps = []
        for j, (px, py) in enumerate(chips):
            for a in range(n):
                cps.append(_rcopy(ins[a].at[:, 2 * px + py], outs[a].at[j], send, recv, j * n + a, (px, py, c)))
        return cps, cps

    return _Exchange("chip_exchange", sums,
                     [jax.ShapeDtypeStruct((3, s.shape[0]) + s.shape[2:], s.dtype) for s in sums], 3 * n, plan)


def _chip_sum(mine, got):
    L, K, r, C = mine.shape
    rows = _block_rows(r)

    def body(p_ref, a_ref, b_ref, o_ref):
        o_ref[0, 0] = (a_ref[0, 0].astype(f32) + b_ref[0, 0].astype(f32)) + (b_ref[1, 0].astype(f32) + b_ref[2, 0].astype(f32))

    place = jnp.stack([2 * lax.axis_index("x") + lax.axis_index("y"), lax.axis_index("c")]).astype(jnp.int32)
    return pl.pallas_call(
        body, name="chip_sum",
        grid_spec=pltpu.PrefetchScalarGridSpec(
            num_scalar_prefetch=1, grid=(L, r // rows),
            in_specs=[pl.BlockSpec((1, 1, rows, C), lambda l, i, p: (l, p[0], i, 0)),
                      pl.BlockSpec((3, 1, rows, C), lambda l, i, p: (0, l, i, 0))],
            out_specs=pl.BlockSpec((1, 1, rows, C), lambda l, i, p: (l, p[1], i, 0))),
        out_shape=jax.ShapeDtypeStruct((L, 2, r, C), f32),
        compiler_params=_params("parallel", "parallel"),
    )(place, mine, got)


def _pair_share(bufs):
    n = len(bufs)

    def plan(ins, outs, send, recv):
        x, y, c, _ = _place()
        sib = (x, y, 1 - c)
        sent = [_rcopy(outs[a].at[:, c], outs[a].at[:, c], send, recv, a, sib) for a in range(n)]
        arriving = [_rcopy(outs[a].at[:, 1 - c], outs[a].at[:, 1 - c], send, recv, a, sib) for a in range(n)]
        return sent, arriving

    return _Exchange("pair_share", bufs, [jax.ShapeDtypeStruct(b.shape, b.dtype) for b in bufs], n, plan, in_place=True)


def _all_sum_small(v):
    def body(v_ref, o_ref, buf, send, recv):
        x, y, c, _ = _place()
        me = 4 * x + 2 * y + c
        buf[me] = v_ref[...]
        cps = []
        for m in range(1, 8):
            to = (x ^ (m >> 2), y ^ ((m >> 1) & 1), c ^ (m & 1))
            cps.append(_rcopy(v_ref, buf.at[me], send, recv, m - 1, to))
        for cp in cps:
            cp.start()
        for cp in cps:
            cp.wait()
        acc = buf[0]
        for d in range(1, 8):
            acc = acc + buf[d]
        o_ref[...] = acc

    vm = pl.BlockSpec(memory_space=pltpu.VMEM)
    return pl.pallas_call(
        body, name="all_sum_small", in_specs=[vm], out_specs=vm,
        out_shape=jax.ShapeDtypeStruct(v.shape, v.dtype),
        scratch_shapes=[pltpu.VMEM((8,) + v.shape, v.dtype), pltpu.SemaphoreType.DMA((7,)), pltpu.SemaphoreType.DMA((7,))],
    )(v)


def _adamw_math(w, g, m, v):
    m = ADAM_B1 * m + (1.0 - ADAM_B1) * g
    v = ADAM_B2 * v + (1.0 - ADAM_B2) * (g * g)
    m_hat = m / (1.0 - ADAM_B1 ** ADAM_STEP)
    v_hat = v / (1.0 - ADAM_B2 ** ADAM_STEP)
    return -ADAM_LR * (m_hat / (jnp.sqrt(v_hat) + ADAM_EPS) + ADAM_WD * w), m, v


def _adamw(w, g, m, v):
    L, R, C = w.shape
    rows = R // 4

    def body(w_ref, g_ref, m_ref, v_ref, d_ref, mo_ref, vo_ref):
        d_ref[...], mo_ref[...], vo_ref[...] = _adamw_math(w_ref[...], g_ref[...], m_ref[...], v_ref[...])

    blk = pl.BlockSpec((1, rows, C), lambda l, i: (l, i, 0))
    return pl.pallas_call(
        body, name="adamw", grid=(L, R // rows), in_specs=[blk] * 4, out_specs=[blk] * 3,
        out_shape=[jax.ShapeDtypeStruct(w.shape, f32)] * 3,
        compiler_params=_params("parallel", "parallel"),
    )(w, g, m, v)


def _chip_index():
    return jnp.reshape(2 * lax.axis_index("x") + lax.axis_index("y"), (1,)).astype(jnp.int32)


def _shard_placed(w, l):
    _, R, C = w.shape
    rows = R // 4

    def body(k_ref, w_ref, o_ref):
        o_ref[0, 0] = w_ref[0].astype(bf16)

    return pl.pallas_call(
        body, name="shard_placed",
        grid_spec=pltpu.PrefetchScalarGridSpec(
            num_scalar_prefetch=1, grid=(R // rows,),
            in_specs=[pl.BlockSpec((1, rows, C), lambda i, k: (l, i, 0))],
            out_specs=pl.BlockSpec((1, 1, rows, C), lambda i, k: (0, k[0], i, 0))),
        out_shape=jax.ShapeDtypeStruct((1, 4, R, C), bf16),
        compiler_params=_params("parallel"),
    )(_chip_index(), w)


SMALL_ROWS = 4 * DEPTH


def _pack_small(g_pre, g_post, lb, g_head, sinks, loss=None):
    rows = []
    for l in range(DEPTH):
        tail = [g_head[l], sinks[l]]
        if loss is not None and l == 0:
            tail.append(jnp.reshape(loss, (1,)))
        tail = jnp.concatenate(tail)
        rows += [g_pre[l], g_post[l], lb[l], jnp.pad(tail, (0, D_MODEL - tail.shape[0]))]
    return jnp.stack(rows)


def _unpack_small(p):
    g_pre = jnp.stack([p[4 * l] for l in range(DEPTH)])
    g_post = jnp.stack([p[4 * l + 1] for l in range(DEPTH)])
    lb = jnp.stack([p[4 * l + 2] for l in range(DEPTH)])
    g_head = jnp.stack([p[4 * l + 3, :HG_HEAD_DIM] for l in range(DEPTH)])
    sinks = jnp.stack([p[4 * l + 3, HG_HEAD_DIM:HG_HEAD_DIM + ATT_HEADS] for l in range(DEPTH)])
    return g_pre, g_post, lb, g_head, sinks


def _small_update(gsum, w, m, v):
    def body(g_ref, w_ref, m_ref, v_ref, go_ref, d_ref, mo_ref, vo_ref):
        g = g_ref[...]
        w = w_ref[...]
        lbp = [w[4 * l + 2:4 * l + 3] for l in range(DEPTH)]
        mx = functools.reduce(jnp.maximum, lbp)
        e = [jnp.exp(t - mx) for t in lbp]
        tot = functools.reduce(jnp.add, e)
        p = [t / tot for t in e]
        glb = [g[4 * l + 2:4 * l + 3] for l in range(DEPTH)]
        row = lax.broadcasted_iota(jnp.int32, g.shape, 0)
        for j in range(DEPTH):
            gj = jnp.zeros_like(p[0])
            for l in range(DEPTH):
                for i in range(1, l + 1):
                    gj = gj + glb[l] * p[i] * ((1.0 if i == j else 0.0) - p[j])
            g = jnp.where(row == 4 * j + 2, gj, g)
        go_ref[...] = g
        d_ref[...], mo_ref[...], vo_ref[...] = _adamw_math(w, g, m_ref[...], v_ref[...])

    vm = pl.BlockSpec(memory_space=pltpu.VMEM)
    return pl.pallas_call(
        body, name="small_update", in_specs=[vm] * 4, out_specs=[vm] * 4,
        out_shape=[jax.ShapeDtypeStruct(gsum.shape, f32)] * 4,
    )(gsum, w, m, v)


def kernel(x, w_in, w_out, g_pre, g_post, lb_param, g_head, sinks, loss_target, m_w_in, m_w_out, m_g_pre, m_g_post, m_lb_param, m_g_head, m_sinks, v_w_in, v_w_out, v_g_pre, v_g_post, v_lb_param, v_g_head, v_sinks):
    B, S, _ = x.shape
    T = B * S
    L = DEPTH
    ri, ro = IN_WIDTH // 8, MIX_WIDTH // 8
    tr = lambda a: jnp.transpose(a, (0, 2, 1))
    wt, mt, vt = tr(w_in), tr(m_w_in), tr(v_w_in)
    bufs = [[_shard_placed(wt, l).reshape(1, 4, 2, ri, D_MODEL), _shard_placed(w_out, l).reshape(1, 4, 2, ro, D_MODEL)]
            for l in range(L)]
    loss, dx, grads, ggpre, ggpost, glb, gghead, gsinks = _train_step(
        x.reshape(T, D_MODEL), loss_target.reshape(T, D_MODEL), bufs, g_pre, g_post, lb_param, g_head, sinks,
        B=B, S=S, exchange=True)
    gwt_mine = jnp.concatenate([g[0] for g in grads], axis=0).reshape(L, 2 * ri, D_MODEL)
    grad_w_out = jnp.concatenate([g[1] for g in grads], axis=0).reshape(L, 2 * ro, D_MODEL)

    d_wt, nm_wt, nv_wt = _adamw(wt, gwt_mine, mt, vt)
    grad_w_in, d_w_in, nm_w_in, nv_w_in = tr(gwt_mine), tr(d_wt), tr(nm_wt), tr(nv_wt)
    d_w_out, nm_w_out, nv_w_out = _adamw(w_out, grad_w_out, m_w_out, v_w_out)

    gsum = _all_sum_small(_pack_small(ggpre, ggpost, glb, gghead, gsinks, loss))
    gs, ds, ms, vs = _small_update(
        gsum, _pack_small(g_pre, g_post, lb_param, g_head, sinks),
        _pack_small(m_g_pre, m_g_post, m_lb_param, m_g_head, m_sinks),
        _pack_small(v_g_pre, v_g_post, v_lb_param, v_g_head, v_sinks))
    loss_all = gsum[3, HG_HEAD_DIM + ATT_HEADS]
    return (loss_all, dx.reshape(B, S, D_MODEL), grad_w_in, grad_w_out, *_unpack_small(gs),
            d_w_in, d_w_out, *_unpack_small(ds), nm_w_in, nm_w_out, *_unpack_small(ms),
            nv_w_in, nv_w_out, *_unpack_small(vs))
```

```python
import functools
import math

import jax
import jax.numpy as jnp
from jax import lax
from jax.experimental import pallas as pl
from jax.experimental.pallas import tpu as pltpu

f32 = jnp.float32
bf16 = jnp.bfloat16

D_MODEL = 1024
DEPTH = 2
HG_WIDTH = 1024
HG_HEAD_DIM = 128
HG_HEADS = 8
CHUNK = 64
SUB = 16
ATT_WIDTH = 1024
ATT_HEAD_DIM = 64
ATT_HEADS = 16
ATT_GROUP = 8
KV_WIDTH = 128
ATT_BLOCK = 128
ATT_SCALE = 1.0 / math.sqrt(ATT_HEAD_DIM)
ROPE_THETA = 10000.0
IN_WIDTH = 6400
MIX_WIDTH = 2048
NORM_EPS = 1e-6
NEG_INF = -1e30
LB_FLOOR = 1e-20
LANES = 128
VMEM_LIMIT = 48 * 1024 * 1024

ADAM_LR = 0.001
ADAM_B1 = 0.9
ADAM_B2 = 0.999
ADAM_EPS = 1e-08
ADAM_WD = 0.01
ADAM_STEP = 10

QA_BLK, ZA_BLK, KV_BLK = 4, 5, 24

NT = (((1,), (1,)), ((), ()))
TN = (((0,), (0,)), ((), ()))


def _dot(a, b, dims=None, precision=None):
    if dims is None:
        return jnp.dot(a, b, preferred_element_type=f32, precision=precision)
    return lax.dot_general(a, b, dims, preferred_element_type=f32, precision=precision)


def _sigmoid(x):
    return 1.0 / (1.0 + jnp.exp(-x))


def _params(*sem):
    return pltpu.CompilerParams(dimension_semantics=sem, vmem_limit_bytes=VMEM_LIMIT)


TAIL = IN_WIDTH - 5120


def _in_proj(x, g, wt, tail, l, *, tm=1024):
    T = x.shape[0]
    tm = min(tm, T)
    nmain = 5120 // TAIL

    def body(x_ref, g_ref, w_ref, t_ref, p_ref, h_ref, hs):
        j = pl.program_id(1)

        @pl.when(j == 0)
        def _():
            xv = x_ref[...]
            r = lax.rsqrt(jnp.mean(xv * xv, axis=-1, keepdims=True) + NORM_EPS)
            hv = (xv * r * g_ref[...]).astype(bf16)
            hs[...] = hv
            h_ref[...] = hv

        @pl.when(j < nmain)
        def _():
            p_ref[...] = _dot(hs[...], w_ref[...], NT)

        @pl.when(j == nmain)
        def _():
            p_ref[...] = _dot(hs[...], t_ref[...], NT)

    return pl.pallas_call(
        body, name="in_proj", grid=(T // tm, nmain + 1),
        in_specs=[pl.BlockSpec((tm, D_MODEL), lambda i, j: (i, 0)),
                  pl.BlockSpec((1, D_MODEL), lambda i, j: (0, 0)),
                  pl.BlockSpec((None, TAIL, D_MODEL), lambda i, j: (l, jnp.minimum(j, nmain - 1), 0)),
                  pl.BlockSpec((None, TAIL, D_MODEL), lambda i, j: (l, 0, 0))],
        out_specs=[pl.BlockSpec((tm, TAIL), lambda i, j: (i, j)),
                   pl.BlockSpec((tm, D_MODEL), lambda i, j: (i, 0))],
        out_shape=[jax.ShapeDtypeStruct((T, IN_WIDTH), f32), jax.ShapeDtypeStruct((T, D_MODEL), bf16)],
        scratch_shapes=[pltpu.VMEM((tm, D_MODEL), bf16)],
        compiler_params=_params("parallel", "arbitrary"),
    )(x, g, wt, tail)


def _out_proj(ch, ca, wo, l, x, g, *, tm=512):
    T = x.shape[0]
    tm = min(tm, T)
    half = MIX_WIDTH // 2

    def body(ch_ref, ca_ref, wo_ref, x_ref, g_ref, xn_ref, y_ref):
        y = _dot(ch_ref[...], wo_ref[0:half, :]) + _dot(ca_ref[...], wo_ref[half:MIX_WIDTH, :])
        r = lax.rsqrt(jnp.mean(y * y, axis=-1, keepdims=True) + NORM_EPS)
        y_ref[...] = y
        xn_ref[...] = x_ref[...] + y * r * g_ref[...]

    row = lambda i: (i, 0)
    fixed = lambda i: (0, 0)
    return pl.pallas_call(
        body, name="out_proj", grid=(T // tm,),
        in_specs=[pl.BlockSpec((tm, half), row), pl.BlockSpec((tm, half), row),
                  pl.BlockSpec((None, MIX_WIDTH, D_MODEL), lambda i: (l, 0, 0)), pl.BlockSpec((tm, D_MODEL), row),
                  pl.BlockSpec((1, D_MODEL), fixed)],
        out_specs=[pl.BlockSpec((tm, D_MODEL), row), pl.BlockSpec((tm, D_MODEL), row)],
        out_shape=[jax.ShapeDtypeStruct((T, D_MODEL), f32)] * 2,
        compiler_params=_params("parallel"),
    )(ch, ca, wo, x, g)


def _loss_head(y, target, *, tm=512):
    T = y.shape[0]
    tm = min(tm, T)

    def body(y_ref, t_ref, d_ref, l_ref):
        @pl.when(pl.program_id(0) == 0)
        def _():
            l_ref[...] = jnp.zeros_like(l_ref)
        err = y_ref[...] - t_ref[...]
        d_ref[...] = err * (1.0 / D_MODEL)
        l_ref[...] += jnp.sum(err * err) * (0.5 / D_MODEL)

    row = lambda i: (i, 0)
    return pl.pallas_call(
        body, name="loss_head", grid=(T // tm,),
        in_specs=[pl.BlockSpec((tm, D_MODEL), row), pl.BlockSpec((tm, D_MODEL), row)],
        out_specs=[pl.BlockSpec((tm, D_MODEL), row), pl.BlockSpec((8, LANES), lambda i: (0, 0))],
        out_shape=[jax.ShapeDtypeStruct((T, D_MODEL), f32), jax.ShapeDtypeStruct((8, LANES), f32)],
        compiler_params=_params("arbitrary"),
    )(y, target)


def _out_proj_bwd(dxn, y, g, wo, l, ch, ca, *, tm=256, ride=None):
    T = y.shape[0]
    tm = min(tm, T)
    half = MIX_WIDTH // 2

    def body(dx_ref, y_ref, g_ref, wo_ref, ch_ref, ca_ref, dch_ref, dca_ref, dwo_ref, dg_ref):
        @pl.when(pl.program_id(0) == 0)
        def _():
            dwo_ref[...] = jnp.zeros_like(dwo_ref)
            dg_ref[...] = jnp.zeros_like(dg_ref)
        y = y_ref[...]
        dx = dx_ref[...]
        r = lax.rsqrt(jnp.mean(y * y, axis=-1, keepdims=True) + NORM_EPS)
        gy = dx * g_ref[...]
        dy = r * gy - y * (r * r * r) * jnp.mean(gy * y, axis=-1, keepdims=True)
        dg_ref[...] += jnp.sum(dx * y * r, axis=0, keepdims=True)
        dyb = dy.astype(bf16)
        dch_ref[...] = _dot(dyb, wo_ref[0:half, :], NT)
        dca_ref[...] = _dot(dyb, wo_ref[half:MIX_WIDTH, :], NT)
        dwo_ref[0:half, :] += _dot(ch_ref[...], dyb, TN)
        dwo_ref[half:MIX_WIDTH, :] += _dot(ca_ref[...], dyb, TN)

    row = lambda i: (i, 0)
    fixed = lambda i: (0, 0)
    return _call(
        body, (dxn, y, g, wo, ch, ca), name="out_proj_bwd", grid=(T // tm,),
        in_specs=[pl.BlockSpec((tm, D_MODEL), row), pl.BlockSpec((tm, D_MODEL), row),
                  pl.BlockSpec((1, D_MODEL), fixed), pl.BlockSpec((None, MIX_WIDTH, D_MODEL), lambda i: (l, 0, 0)),
                  pl.BlockSpec((tm, half), row), pl.BlockSpec((tm, half), row)],
        out_specs=[pl.BlockSpec((tm, half), row), pl.BlockSpec((tm, half), row),
                   pl.BlockSpec((MIX_WIDTH, D_MODEL), fixed), pl.BlockSpec((1, D_MODEL), fixed)],
        out_shape=[jax.ShapeDtypeStruct((T, half), f32), jax.ShapeDtypeStruct((T, half), f32),
                   jax.ShapeDtypeStruct((MIX_WIDTH, D_MODEL), f32), jax.ShapeDtypeStruct((1, D_MODEL), f32)],
        semantics=("arbitrary",), ride=ride)


TILE = 256
PIECE_TILES = (4, 4, 4, 4, 4, 1, 4)
PIECE_START = tuple(sum(PIECE_TILES[:p]) for p in range(len(PIECE_TILES)))
N_TILES = sum(PIECE_TILES)


def _piece_specs(rows, index):
    def spec(s, n):
        def index_map(*g):
            r, t = index(*g)
            return r, jnp.clip(t - s, 0, n - 1)
        return pl.BlockSpec((rows, TILE), index_map)
    return [spec(s, n) for s, n in zip(PIECE_START, PIECE_TILES)]


def _for_piece(t, fn):
    for p, (s, n) in enumerate(zip(PIECE_START, PIECE_TILES)):
        pl.when((t >= s) & (t < s + n))(functools.partial(fn, p))


def _in_proj_bwd(pieces, wt, tail, l, x, g, dxn, *, tm=512, blocks=None, dx_into=None, ride=None):
    T = x.shape[0]
    tm = min(tm, T)
    first, count = blocks or (0, T // tm)
    npc = len(pieces)
    nk = npc
    nmain = npc - 2
    wide = ATT_WIDTH
    extra = [] if dx_into is None else [dx_into]

    def body(*refs):
        dp_refs = refs[:npc]
        w_ref, tz_ref, tkv_ref, x_ref, g_ref, dxn_ref = refs[npc:npc + 6]
        dx_ref, dg_ref, acc = refs[npc + 6 + len(extra):]
        i, k = pl.program_id(0), pl.program_id(1)

        @pl.when((i == 0) & (k == 0))
        def _():
            dg_ref[...] = jnp.zeros_like(dg_ref)

        @pl.when(k == 0)
        def _():
            acc[...] = jnp.zeros_like(acc)

        for p in range(npc):
            w_p = w_ref if p < nmain else (tkv_ref if p == nmain else tz_ref)

            def add(p=p, w_p=w_p):
                acc[...] += _dot(dp_refs[p][...], w_p[...])
            pl.when(k == p)(add)

        @pl.when(k == nk - 1)
        def _():
            dh = acc[...]
            xv = x_ref[...]
            r = lax.rsqrt(jnp.mean(xv * xv, axis=-1, keepdims=True) + NORM_EPS)
            gy = dh * g_ref[...]
            dx_ref[...] = dxn_ref[...] + r * gy - xv * (r * r * r) * jnp.mean(gy * xv, axis=-1, keepdims=True)
            dg_ref[...] += jnp.sum(dh * xv * r, axis=0, keepdims=True)

    rows = lambda i, k: (first + i, 0)
    return _call(
        body, (*pieces, wt, tail, tail, x, g, dxn, *extra), name="in_proj_bwd", grid=(count, nk),
        in_specs=[pl.BlockSpec((tm, p.shape[1]), rows) for p in pieces] + [
            pl.BlockSpec((None, wide, D_MODEL), lambda i, k: (l, jnp.minimum(k, nmain - 1), 0)),
            pl.BlockSpec((None, wide, D_MODEL), lambda i, k: (l, 0, 0)),
            pl.BlockSpec((None, TAIL - wide, D_MODEL), lambda i, k: (l, wide // (TAIL - wide), 0)),
            pl.BlockSpec((tm, D_MODEL), rows), pl.BlockSpec((1, D_MODEL), lambda i, k: (0, 0)),
            pl.BlockSpec((tm, D_MODEL), rows)] + [ANY] * len(extra),
        out_specs=[pl.BlockSpec((tm, D_MODEL), rows), pl.BlockSpec((1, D_MODEL), lambda i, k: (0, 0))],
        out_shape=[jax.ShapeDtypeStruct((T, D_MODEL), f32), jax.ShapeDtypeStruct((1, D_MODEL), f32)],
        scratch_shapes=[pltpu.VMEM((tm, D_MODEL), f32)],
        semantics=("arbitrary", "arbitrary"), ride=ride, aliases={npc + 6: 0} if extra else None)


def _grad_w_in(h, pieces, *, ride=None):
    T = h.shape[0]
    npc = len(pieces)

    def body(*refs):
        h_ref, dp_refs, o_ref = refs[0], refs[1:1 + npc], refs[1 + npc]

        def put(p):
            o_ref[...] = _dot(dp_refs[p][...], h_ref[...], TN)
        _for_piece(pl.program_id(0), put)

    return _call(
        body, (h, *pieces), name="grad_w_in", grid=(N_TILES,),
        in_specs=[pl.BlockSpec((T, D_MODEL), lambda j: (0, 0), pipeline_mode=pl.Buffered(1))]
        + _piece_specs(T, lambda j: (0, j)),
        out_specs=[pl.BlockSpec((TILE, D_MODEL), lambda j: (j, 0))],
        out_shape=[jax.ShapeDtypeStruct((IN_WIDTH, D_MODEL), f32)],
        semantics=("parallel",), ride=ride)


def _lower_bound(lbp, layer):
    m = jnp.max(lbp, axis=0, keepdims=True)
    e = jnp.exp(lbp - m)
    p = e / jnp.sum(e, axis=0, keepdims=True)
    acc = p[0:1]
    for i in range(1, layer + 1):
        acc = acc + p[i:i + 1]
    return acc - p[0:1]


def _gate_parts(qr, fr, lb, lbf):
    sq = _sigmoid(qr)
    e = jnp.exp(-jnp.abs(fr))
    inv = 1.0 / (1.0 + e)
    pos = fr >= 0
    sg = jnp.where(pos, inv, e * inv)
    nsg = jnp.where(pos, e * inv, inv)
    fg = lbf + (1.0 - lb) * sg
    return qr * sq, sq, sg, nsg, fg, jnp.log(fg), (1.0 - lb) * nsg


def _anchor_masks(transposed=False):
    t = lax.broadcasted_iota(jnp.int32, (CHUNK, CHUNK), 1 if transposed else 0)
    s = lax.broadcasted_iota(jnp.int32, (CHUNK, CHUNK), 0 if transposed else 1)
    anchors = tuple(range(SUB - 1, CHUNK - 1, SUB))
    return anchors, [(t > a) & (s <= a) & (s > a - SUB) for a in anchors]


def _seg_sum(seg, x):
    hi = x.astype(bf16)
    return _dot(seg, hi) + _dot(seg, (x - hi.astype(f32)).astype(bf16))


def _hgrn_fwd(proj, lb_param, g_head, *, B, S, layer, ride=None):
    T = B * S
    TB = min(512, S)
    nT, NC = S // TB, TB // CHUNK
    nC = S // CHUNK
    HD = HG_HEAD_DIM

    def body(q_ref, f_ref, i_ref, z_ref, lb_ref, gh_ref, cat_ref, op_ref, st_ref,
             s_scr, b_scr, k_scr):
        @pl.when(pl.program_id(2) == 0)
        def _():
            s_scr[...] = jnp.zeros_like(s_scr)
        lb = _lower_bound(lb_ref[...], layer)
        lbf = jnp.maximum(lb, LB_FLOOR)
        gh = gh_ref[...]
        r_i = lax.broadcasted_iota(jnp.int32, (CHUNK, CHUNK), 0)
        c_i = lax.broadcasted_iota(jnp.int32, (CHUNK, CHUNK), 1)
        tril = (r_i >= c_i).astype(f32)
        rows8 = lax.broadcasted_iota(jnp.int32, (8, HD), 0)
        lane_c = lax.broadcasted_iota(jnp.int32, (8, CHUNK), 1)
        anchors, masks = _anchor_masks()

        def chunk(c, st):
            rs = slice(c * CHUNK, (c + 1) * CHUNK)
            b_s, k_s = b_scr.at[c], k_scr.at[c]
            q, _, _, _, _, logf, k = _gate_parts(q_ref[rs, :], f_ref[rs, :], lb, lbf)
            v = i_ref[rs, :]
            b = _dot(tril, logf, precision=lax.Precision.HIGHEST)
            b_s[...] = b
            k_s[...] = k
            pieces = []
            for blk in range(CHUNK // SUB):
                r0 = blk * SUB
                bp = [b[r0 + 8 * i:r0 + 8 * i + 8] for i in range(SUB // 8)]
                qp = [q[r0 + 8 * i:r0 + 8 * i + 8] for i in range(SUB // 8)]
                ap = [jnp.zeros((8, CHUNK), f32) for _ in range(SUB // 8)]
                for s in range(SUB):
                    bs = b_s[r0 + s:r0 + s + 1, :]
                    ks = k_s[r0 + s:r0 + s + 1, :]
                    for i in range(s // 8, SUB // 8):
                        diff = bp[i] - bs
                        if i == s // 8:
                            diff = jnp.where(rows8 >= s - 8 * i, diff, NEG_INF)
                        col = jnp.sum(jnp.exp(diff) * qp[i] * ks, axis=1, keepdims=True)
                        ap[i] = jnp.where(lane_c == r0 + s, col, ap[i])
                pieces += ap
            a_all = jnp.concatenate(pieces, axis=0)
            for an, mk in zip(anchors, masks):
                beta = b_s[an:an + 1, :]
                qh = (q * jnp.exp(jnp.minimum(b - beta, 0.0))).astype(bf16)
                kh = (k * jnp.exp(jnp.minimum(beta - b, 0.0))).astype(bf16)
                a_all = a_all + jnp.where(mk, _dot(qh, kh, NT), 0.0)
            st_ref[0, 0, c] = st
            vb16 = v.astype(bf16)
            o = _dot(a_all.astype(bf16), vb16) + _dot((q * jnp.exp(b)).astype(bf16), st.astype(bf16), NT)
            b_end = b_s[CHUNK - 1:CHUNK, :]
            kdec = (k * jnp.exp(b_end - b)).astype(bf16)
            st_next = jnp.exp(b_end) * st + _dot(vb16, kdec, TN)
            rr = lax.rsqrt(jnp.mean(o * o, axis=-1, keepdims=True) + NORM_EPS)
            zr = z_ref[rs, :]
            cat_ref[rs, :] = (o * rr * gh * (zr * _sigmoid(zr))).astype(bf16)
            op_ref[rs, :] = o
            return st_next

        st = s_scr[...]
        for c in range(NC):
            st = chunk(c, st)
        s_scr[...] = st

    def col(part):
        return pl.BlockSpec((TB, HD), lambda b, h, n: (b * nT + n, part * HG_HEADS + h))

    out_col = pl.BlockSpec((TB, HD), lambda b, h, n: (b * nT + n, h))
    return _call(
        body, (proj, proj, proj, proj, lb_param, g_head),
        name=f"hgrn_fwd_l{layer}", grid=(B, HG_HEADS, nT),
        in_specs=[col(0), col(1), col(2), col(3),
                  pl.BlockSpec((DEPTH, HD), lambda b, h, n: (0, h)),
                  pl.BlockSpec((1, HD), lambda b, h, n: (0, 0))],
        out_specs=[out_col, out_col,
                   pl.BlockSpec((1, 1, NC, HD, HD), lambda b, h, n: (b, h, n, 0, 0))],
        out_shape=[jax.ShapeDtypeStruct((T, HG_WIDTH), bf16), jax.ShapeDtypeStruct((T, HG_WIDTH), f32),
                   jax.ShapeDtypeStruct((B, HG_HEADS, nC, HD, HD), f32)],
        scratch_shapes=[pltpu.VMEM((HD, HD), f32), pltpu.VMEM((NC, CHUNK, HD), f32), pltpu.VMEM((NC, CHUNK, HD), f32)],
        semantics=("parallel", "parallel", "arbitrary"), ride=ride)


def _hgrn_bwd(proj, lb_param, g_head, o_pre, states, dcat, *, B, S, layer, ride=None):
    T = B * S
    TB = min(512, S)
    nT, NC = S // TB, TB // CHUNK
    HD = HG_HEAD_DIM

    def body(q_ref, f_ref, i_ref, z_ref, lb_ref, gh_ref, op_ref, st_ref, dc_ref,
             dq_ref, df_ref, di_ref, dz_ref, dlb_ref, dgh_ref,
             ds_scr, b_scr, q_scr, do_scr, wk_scr):
        @pl.when(pl.program_id(2) == 0)
        def _():
            ds_scr[...] = jnp.zeros_like(ds_scr)
            dlb_ref[...] = jnp.zeros_like(dlb_ref)
            dgh_ref[...] = jnp.zeros_like(dgh_ref)
        lb = _lower_bound(lb_ref[...], layer)
        lbf = jnp.maximum(lb, LB_FLOOR)
        ind = (lb > LB_FLOOR).astype(f32)
        gh = gh_ref[...]
        r_i = lax.broadcasted_iota(jnp.int32, (CHUNK, CHUNK), 0)
        c_i = lax.broadcasted_iota(jnp.int32, (CHUNK, CHUNK), 1)
        tril = (r_i >= c_i).astype(f32)
        triu = (c_i >= r_i).astype(f32)
        rows8 = lax.broadcasted_iota(jnp.int32, (8, HD), 0)
        lane_c = lax.broadcasted_iota(jnp.int32, (8, CHUNK), 1)
        last_row = lax.broadcasted_iota(jnp.int32, (CHUNK, HD), 0) == CHUNK - 1
        anchors, masks = _anchor_masks()
        _, masks_t = _anchor_masks(transposed=True)
        seg_t = lax.broadcasted_iota(jnp.int32, (SUB, 8 * SUB), 0)
        seg_r = lax.broadcasted_iota(jnp.int32, (SUB, 8 * SUB), 1) // 8
        seg0 = (seg_r == seg_t).astype(bf16)
        seg1 = (seg_r[:, 0:4 * SUB] + 8 == seg_t[:, 0:4 * SUB]).astype(bf16)

        def chunk(c, dst1):
            rs = slice(c * CHUNK, (c + 1) * CHUNK)
            b_s, q_s, do_s = b_scr.at[c], q_scr.at[c], do_scr.at[c]
            qr, fr = q_ref[rs, :], f_ref[rs, :]
            q, sq, sg, nsg, fg, logf, k = _gate_parts(qr, fr, lb, lbf)
            v = i_ref[rs, :]
            b = _dot(tril, logf, precision=lax.Precision.HIGHEST)
            o = op_ref[rs, :]
            dc = dc_ref[rs, :]
            zr = z_ref[rs, :]
            sz = _sigmoid(zr)
            rr = lax.rsqrt(jnp.mean(o * o, axis=-1, keepdims=True) + NORM_EPS)
            dz_ref[rs, :] = (dc * (o * rr * gh) * (sz * (1.0 + zr * (1.0 - sz)))).astype(bf16)
            dn = dc * (zr * sz)
            dgh_ref[0, 0] += jnp.sum(dn * o * rr, axis=0, keepdims=True)
            gdn = dn * gh
            d_o = rr * gdn - o * (rr * rr * rr) * jnp.mean(gdn * o, axis=-1, keepdims=True)
            b_s[...] = b
            q_s[...] = q
            do_s[...] = d_o
            dob = d_o.astype(bf16)
            vb16 = v.astype(bf16)
            d_a = _dot(dob, vb16, NT)
            d_q = jnp.zeros((CHUNK, HD), f32)
            d_k = jnp.zeros((CHUNK, HD), f32)
            at_all = jnp.zeros((CHUNK, CHUNK), f32)
            for an, mk, mkt in zip(anchors, masks, masks_t):
                beta = b_s[an:an + 1, :]
                eq = jnp.exp(jnp.minimum(b - beta, 0.0))
                ek = jnp.exp(jnp.minimum(beta - b, 0.0))
                qh = (q * eq).astype(bf16)
                kh = (k * ek).astype(bf16)
                at_all = at_all + jnp.where(mkt, _dot(kh, qh, NT), 0.0)
                d_aa = jnp.where(mk, d_a, 0.0).astype(bf16)
                d_q = d_q + _dot(d_aa, kh) * eq
                d_k = d_k + _dot(d_aa, qh, TN) * ek
            st0 = st_ref[0, 0, c]
            dst1b = dst1.astype(bf16)
            eb = jnp.exp(b)
            b_end = b_s[CHUNK - 1:CHUNK, :]
            edec = jnp.exp(b_end - b)
            e_end = jnp.exp(b_end)
            kdec = (k * edec).astype(bf16)
            qdec = (q * eb).astype(bf16)
            d_q = d_q + _dot(dob, st0.astype(bf16)) * eb
            d_v = _dot(kdec, dst1b, NT)
            d_k = d_k + _dot(vb16, dst1b) * edec
            st1 = e_end * st0 + _dot(vb16, kdec, TN)
            rterm = jnp.sum(dst1 * st1, axis=0, keepdims=True)
            dst0 = e_end * dst1 + _dot(dob, qdec, TN)
            dq_blocks, dk_pieces, at_pieces = [], [], []
            for blk in range(CHUNK // SUB):
                r0 = blk * SUB
                wk = wk_scr.at[c * (CHUNK // SUB) + blk]
                bp = [b[r0 + 8 * i:r0 + 8 * i + 8] for i in range(SUB // 8)]
                kp = [k[r0 + 8 * i:r0 + 8 * i + 8] for i in range(SUB // 8)]
                vp = [v[r0 + 8 * i:r0 + 8 * i + 8] for i in range(SUB // 8)]
                dkp = [jnp.zeros((8, HD), f32) for _ in range(SUB // 8)]
                atp = [jnp.zeros((8, CHUNK), f32) for _ in range(SUB // 8)]
                for t in range(SUB):
                    bt = b_s[r0 + t:r0 + t + 1, :]
                    qt = q_s[r0 + t:r0 + t + 1, :]
                    dot_ = do_s[r0 + t:r0 + t + 1, :]
                    for i in range(t // 8 + 1):
                        diff = bt - bp[i]
                        if i == t // 8:
                            diff = jnp.where(rows8 <= t - 8 * i, diff, NEG_INF)
                        e = jnp.exp(diff)
                        a = jnp.sum(e * kp[i] * qt, axis=1, keepdims=True)
                        atp[i] = jnp.where(lane_c == r0 + t, a, atp[i])
                        w = jnp.sum(vp[i] * dot_, axis=1, keepdims=True) * e
                        dkp[i] = dkp[i] + w * qt
                        row = 8 * t if i == 0 else 8 * SUB + 8 * (t - 8)
                        wk[row:row + 8, :] = w * kp[i]
                dq_blocks.append(_seg_sum(seg0, wk[0:8 * SUB, :]) + _seg_sum(seg1, wk[8 * SUB:12 * SUB, :]))
                dk_pieces += dkp
                at_pieces += atp
            d_q = d_q + jnp.concatenate(dq_blocks, axis=0)
            d_k = d_k + jnp.concatenate(dk_pieces, axis=0)
            d_v = d_v + _dot((at_all + jnp.concatenate(at_pieces, axis=0)).astype(bf16), dob)
            db = q * d_q - k * d_k + jnp.where(last_row, rterm, 0.0)
            dlt = _dot(triu, db, precision=lax.Precision.HIGHEST) - fg * d_k
            df_ref[rs, :] = (dlt * (1.0 - lb) * sg * nsg / fg).astype(bf16)
            dlb_ref[0] += jnp.sum(dlt * (ind - sg) / fg, axis=0, keepdims=True)
            dq_ref[rs, :] = (d_q * (sq * (1.0 + qr * (1.0 - sq)))).astype(bf16)
            di_ref[rs, :] = d_v.astype(bf16)
            return dst0

        dst = ds_scr[...]
        for c in reversed(range(NC)):
            dst = chunk(c, dst)
        ds_scr[...] = dst

    def col(part):
        return pl.BlockSpec((TB, HD), lambda b, h, n: (b * nT + nT - 1 - n, part * HG_HEADS + h))

    hcol = pl.BlockSpec((TB, HD), lambda b, h, n: (b * nT + nT - 1 - n, h))
    return _call(
        body, (proj, proj, proj, proj, lb_param, g_head, o_pre, states, dcat),
        name=f"hgrn_bwd_l{layer}", grid=(B, HG_HEADS, nT),
        in_specs=[col(0), col(1), col(2), col(3),
                  pl.BlockSpec((DEPTH, HD), lambda b, h, n: (0, h)),
                  pl.BlockSpec((1, HD), lambda b, h, n: (0, 0)),
                  hcol,
                  pl.BlockSpec((1, 1, NC, HD, HD), lambda b, h, n: (b, h, nT - 1 - n, 0, 0)),
                  hcol],
        out_specs=[hcol, hcol, hcol, hcol,
                   pl.BlockSpec((1, 1, HD), lambda b, h, n: (b, 0, h)),
                   pl.BlockSpec((1, 1, 1, HD), lambda b, h, n: (b, h, 0, 0))],
        out_shape=[jax.ShapeDtypeStruct((T, HG_WIDTH), bf16)] * 4 + [
            jax.ShapeDtypeStruct((B, 1, HG_WIDTH), f32), jax.ShapeDtypeStruct((B, HG_HEADS, 1, HD), f32)],
        scratch_shapes=[pltpu.VMEM((HD, HD), f32)] + [pltpu.VMEM((NC, CHUNK, HD), f32)] * 3
        + [pltpu.VMEM((NC * CHUNK // SUB, 12 * SUB, HD), f32)],
        semantics=("parallel", "parallel", "arbitrary"), ride=ride)


def _rope_tables(S):
    half = ATT_HEAD_DIM // 2
    inv_freq = ROPE_THETA ** (-jnp.arange(half, dtype=f32) / half)
    ang = jnp.arange(S, dtype=f32)[:, None] * inv_freq[None, :]
    cos, sin = jnp.cos(ang), jnp.sin(ang)
    return jnp.tile(jnp.concatenate([cos, cos], axis=1), (1, 2)), jnp.tile(jnp.concatenate([-sin, sin], axis=1), (1, 2))


def _swap_halves(x, first_half):
    return jnp.where(first_half, pltpu.roll(x, LANES - ATT_HEAD_DIM // 2, 1), pltpu.roll(x, ATT_HEAD_DIM // 2, 1))


def _rope(x, cos, sin, first_half):
    return x * cos + _swap_halves(x, first_half) * sin


def _rope_bwd(dy, cos, sin, first_half):
    return dy * cos + _swap_halves(dy * sin, first_half)


def _attn_consts(n):
    lane = lax.broadcasted_iota(jnp.int32, (1, LANES), 1)
    low = lane < ATT_HEAD_DIM
    first_half = (lane % ATT_HEAD_DIM) < ATT_HEAD_DIM // 2
    top = lax.broadcasted_iota(jnp.int32, (LANES, 1), 0) < ATT_HEAD_DIM
    s = lax.broadcasted_iota(jnp.int32, (2 * ATT_BLOCK, ATT_BLOCK), 0)
    t = lax.broadcasted_iota(jnp.int32, (2 * ATT_BLOCK, ATT_BLOCK), 1)
    mask = (s > t) & (s <= t + ATT_BLOCK) & ((s >= ATT_BLOCK) | (n > 0))
    return low, first_half, top, mask


def _dup_kv(x, low):
    rolled = pltpu.roll(x, ATT_HEAD_DIM, 1)
    return [jnp.where(low, x, rolled), jnp.where(low, rolled, x)]


def _attn_head(qtm, kd, vdt, sink, mask):
    s = jnp.where(mask, _dot(kd, qtm) * ATT_SCALE, NEG_INF)
    m = jnp.maximum(jnp.max(s, axis=0, keepdims=True), sink)
    p = jnp.exp(s - m)
    psink = jnp.exp(sink - m)
    inv = 1.0 / (jnp.sum(p, axis=0, keepdims=True) + psink)
    pn = p * inv
    return pn, psink * inv, _dot(vdt, pn.astype(bf16))


def _swa_fwd(proj, sink_b, cos, sin, *, B, S, ride=None):
    T = B * S
    L = ATT_BLOCK
    nB = S // L

    def body(q_ref, z_ref, kvc_ref, kvp_ref, sk_ref, cc_ref, sc_ref, cp_ref, sp_ref, cat_ref):
        n = pl.program_id(1)
        low, first_half, top, mask = _attn_consts(n)
        cc, sc = cc_ref[...], sc_ref[...]
        kc = _rope(kvc_ref[:, 0:LANES], cc, sc, first_half)
        kp = _rope(kvp_ref[:, 0:LANES], cp_ref[...], sp_ref[...], first_half)
        kd = [x.astype(bf16) for x in _dup_kv(jnp.concatenate([kp, kc], axis=0), low)]
        vdt = [x.T.astype(bf16) for x in _dup_kv(jnp.concatenate([kvp_ref[:, LANES:2 * LANES], kvc_ref[:, LANES:2 * LANES]], axis=0), low)]
        for pair in range(ATT_HEADS // 2):
            cols = slice(pair * LANES, (pair + 1) * LANES)
            j = (2 * pair) // ATT_GROUP
            qt = _rope(q_ref[:, cols], cc, sc, first_half).T
            outs = []
            for hh in range(2):
                h = 2 * pair + hh
                qtm = jnp.where(top if hh == 0 else ~top, qt, 0.0).astype(bf16)
                outs.append(_attn_head(qtm, kd[j], vdt[j], sk_ref[h:h + 1, 0:1], mask)[2])
            zp = z_ref[:, cols]
            cat_ref[:, cols] = (jnp.where(top, outs[0], outs[1]).T * (zp * _sigmoid(zp))).astype(bf16)

    cur = lambda b, n: (b * nB + n, 0)
    return _call(
        body, (proj, proj, proj, proj, sink_b, cos, sin, cos, sin), name="swa_fwd", grid=(B, nB),
        in_specs=[pl.BlockSpec((L, ATT_WIDTH), lambda b, n: (b * nB + n, QA_BLK)),
                  pl.BlockSpec((L, ATT_WIDTH), lambda b, n: (b * nB + n, ZA_BLK)),
                  pl.BlockSpec((L, 2 * KV_WIDTH), lambda b, n: (b * nB + n, KV_BLK)),
                  pl.BlockSpec((L, 2 * KV_WIDTH), lambda b, n: (b * nB + jnp.maximum(n - 1, 0), KV_BLK)),
                  pl.BlockSpec((ATT_HEADS, LANES), lambda b, n: (0, 0)),
                  pl.BlockSpec((L, LANES), lambda b, n: (n, 0)), pl.BlockSpec((L, LANES), lambda b, n: (n, 0)),
                  pl.BlockSpec((L, LANES), lambda b, n: (jnp.maximum(n - 1, 0), 0)),
                  pl.BlockSpec((L, LANES), lambda b, n: (jnp.maximum(n - 1, 0), 0))],
        out_specs=[pl.BlockSpec((L, ATT_WIDTH), cur)],
        out_shape=[jax.ShapeDtypeStruct((T, ATT_WIDTH), bf16)],
        semantics=("parallel", "parallel"), ride=ride)


def _swa_bwd(proj, sink_b, cos, sin, dcat, *, B, S, ride=None):
    T = B * S
    L = ATT_BLOCK
    nB = S // L

    def body(q_ref, z_ref, kvc_ref, kvp_ref, sk_ref, cc_ref, sc_ref, cp_ref, sp_ref, dc_ref,
             dq_ref, dz_ref, dkv_ref, dsk_ref, carry, ds_st, pn_st, q_st, do_st):
        step = pl.program_id(1)
        n = nB - 1 - step

        @pl.when((pl.program_id(0) == 0) & (step == 0))
        def _():
            dsk_ref[...] = jnp.zeros_like(dsk_ref)

        @pl.when(step == 0)
        def _():
            carry[...] = jnp.zeros_like(carry)
        low, first_half, top, mask = _attn_consts(n)
        cc, sc, cp, sp = cc_ref[...], sc_ref[...], cp_ref[...], sp_ref[...]
        kc = _rope(kvc_ref[:, 0:LANES], cc, sc, first_half)
        kp = _rope(kvp_ref[:, 0:LANES], cp, sp, first_half)
        kdf = _dup_kv(jnp.concatenate([kp, kc], axis=0), low)
        vdf = _dup_kv(jnp.concatenate([kvp_ref[:, LANES:2 * LANES], kvc_ref[:, LANES:2 * LANES]], axis=0), low)
        kd = [x.astype(bf16) for x in kdf]
        vd = [x.astype(bf16) for x in vdf]
        kdt = [x.T.astype(bf16) for x in kdf]
        vdt = [x.T.astype(bf16) for x in vdf]
        dkd, dvd = [], []
        for pair in range(ATT_HEADS // 2):
            cols = slice(pair * LANES, (pair + 1) * LANES)
            j = (2 * pair) // ATT_GROUP
            qp = _rope(q_ref[:, cols], cc, sc, first_half)
            qt = qp.T
            zp = z_ref[:, cols]
            dc = dc_ref[:, cols]
            sz = _sigmoid(zp)
            d_o = dc * (zp * sz)
            dot_ = d_o.T
            res = []
            for hh in range(2):
                rsel = top if hh == 0 else ~top
                qtm = jnp.where(rsel, qt, 0.0).astype(bf16)
                pn, psn, o = _attn_head(qtm, kd[j], vdt[j], sk_ref[2 * pair + hh:2 * pair + hh + 1, 0:1], mask)
                res.append((rsel, pn, psn, o))
            ot = jnp.where(top, res[0][3], res[1][3])
            dz_ref[:, cols] = (dc * ot.T * (sz * (1.0 + zp * (1.0 - sz)))).astype(bf16)
            dqts = []
            for hh in range(2):
                h = 2 * pair + hh
                rsel, pn, psn, _ = res[hh]
                lsel = low if hh == 0 else ~low
                dotm = jnp.where(rsel, dot_, 0.0)
                delta = jnp.sum(dotm * ot, axis=0, keepdims=True)
                dst = (pn * (_dot(vd[j], dotm.astype(bf16)) - delta) * ATT_SCALE).astype(bf16)
                dsk_ref[h:h + 1, :] += jnp.zeros((1, LANES), f32) - jnp.sum(psn * delta)
                dqts.append(_dot(kdt[j], dst))
                g = h % ATT_GROUP
                ds_st[:, g * LANES:(g + 1) * LANES] = dst
                pn_st[:, g * LANES:(g + 1) * LANES] = pn.astype(bf16)
                q_st[g * LANES:(g + 1) * LANES, :] = jnp.where(lsel, qp, 0.0).astype(bf16)
                do_st[g * LANES:(g + 1) * LANES, :] = jnp.where(lsel, d_o, 0.0).astype(bf16)
            dq_ref[:, cols] = _rope_bwd(jnp.where(top, dqts[0], dqts[1]).T, cc, sc, first_half).astype(bf16)
            if (2 * pair + 2) % ATT_GROUP == 0:
                dkd.append(_dot(ds_st[...], q_st[...]))
                dvd.append(_dot(pn_st[...], do_st[...]))
        dk = [x + pltpu.roll(x, ATT_HEAD_DIM, 1) for x in dkd]
        dv = [x + pltpu.roll(x, ATT_HEAD_DIM, 1) for x in dvd]
        dk = jnp.where(low, dk[0], dk[1])
        dv = jnp.where(low, dv[0], dv[1])
        dkv_ref[:, 0:LANES] = (_rope_bwd(dk[L:2 * L], cc, sc, first_half) + carry[:, 0:LANES]).astype(bf16)
        dkv_ref[:, LANES:2 * LANES] = (dv[L:2 * L] + carry[:, LANES:2 * LANES]).astype(bf16)
        carry[:, 0:LANES] = _rope_bwd(dk[0:L], cp, sp, first_half)
        carry[:, LANES:2 * LANES] = dv[0:L]

    rev = lambda b, s: b * nB + nB - 1 - s
    revp = lambda b, s: b * nB + jnp.maximum(nB - 2 - s, 0)
    wide = lambda blk: pl.BlockSpec((L, ATT_WIDTH), lambda b, s: (rev(b, s), blk))
    tab = pl.BlockSpec((L, LANES), lambda b, s: (nB - 1 - s, 0))
    tabp = pl.BlockSpec((L, LANES), lambda b, s: (jnp.maximum(nB - 2 - s, 0), 0))
    return _call(
        body, (proj, proj, proj, proj, sink_b, cos, sin, cos, sin, dcat), name="swa_bwd", grid=(B, nB),
        in_specs=[wide(QA_BLK), wide(ZA_BLK),
                  pl.BlockSpec((L, 2 * KV_WIDTH), lambda b, s: (rev(b, s), KV_BLK)),
                  pl.BlockSpec((L, 2 * KV_WIDTH), lambda b, s: (revp(b, s), KV_BLK)),
                  pl.BlockSpec((ATT_HEADS, LANES), lambda b, s: (0, 0)),
                  tab, tab, tabp, tabp, wide(0)],
        out_specs=[wide(0), wide(0), pl.BlockSpec((L, 2 * KV_WIDTH), lambda b, s: (rev(b, s), 0)),
                   pl.BlockSpec((ATT_HEADS, LANES), lambda b, s: (0, 0))],
        out_shape=[jax.ShapeDtypeStruct((T, ATT_WIDTH), bf16), jax.ShapeDtypeStruct((T, ATT_WIDTH), bf16),
                   jax.ShapeDtypeStruct((T, 2 * KV_WIDTH), bf16), jax.ShapeDtypeStruct((ATT_HEADS, LANES), f32)],
        scratch_shapes=[pltpu.VMEM((L, 2 * KV_WIDTH), f32),
                        pltpu.VMEM((2 * L, ATT_GROUP * LANES), bf16), pltpu.VMEM((2 * L, ATT_GROUP * LANES), bf16),
                        pltpu.VMEM((ATT_GROUP * LANES, LANES), bf16), pltpu.VMEM((ATT_GROUP * LANES, LANES), bf16)],
        semantics=("arbitrary", "arbitrary"), ride=ride)


def _train_step(x, target, bufs, g_pre, g_post, lb_param, g_head, sinks, *, B, S, exchange):
    L = DEPTH
    T = x.shape[0]
    ri, ro = IN_WIDTH // 8, MIX_WIDTH // 8
    cos, sin = _rope_tables(S)
    full = list(bufs)
    if exchange:
        full[0] = _run_exchange(_gather_d2d(_run_exchange(_gather_ici(bufs[0]))))
    saved = []
    for l in range(L):
        wt = full[l][0].reshape(1, IN_WIDTH, D_MODEL)
        wo = full[l][1].reshape(1, MIX_WIDTH, D_MODEL)
        tail = jnp.concatenate([wt[:, 5376:6400], wt[:, 5120:5376]], axis=1)
        proj, h = _in_proj(x, g_pre[l:l + 1], wt, tail, 0)
        ahead = exchange and l + 1 < L
        (ch, o_pre, states), landed = _hgrn_fwd(proj, lb_param, g_head[l:l + 1], B=B, S=S, layer=l,
                                                ride=_gather_ici(bufs[l + 1]) if ahead else None)
        sink_b = jnp.broadcast_to(sinks[l][:, None], (ATT_HEADS, LANES))
        (ca,), passed = _swa_fwd(proj, sink_b, cos, sin, B=B, S=S, ride=_gather_d2d(landed) if ahead else None)
        if ahead:
            full[l + 1] = passed
        xn, y = _out_proj(ch, ca, wo, 0, x, g_post[l:l + 1])
        saved.append((x, proj, h, ch, o_pre, states, sink_b, ca, y, wt, tail, wo))
        x = xn
    dx, loss = _loss_head(x, target)

    def reduce_tail(sums, recv):
        return _run_exchange(_pair_share([_chip_sum(s, r) for s, r in zip(sums, recv)]))

    grads = [None] * L
    waiting = None
    gg_pre, gg_post, g_lb, gg_head, g_sinks = [], [], [], [], []
    for l in reversed(range(L)):
        x_in, proj, h, ch, o_pre, states, sink_b, ca, y, wt, tail, wo = saved[l]
        (dch, dca, dwo, dgpost), got = _out_proj_bwd(dx, y, g_post[l:l + 1], wo, 0, ch, ca,
                                                     ride=_pair_exchange(waiting) if waiting else None)
        sums = [_pair_add(p, r) for p, r in zip(waiting, got)] if waiting else None
        (dq, df, di, dz, dlb, dgh), recv = _hgrn_bwd(proj, lb_param, g_head[l:l + 1], o_pre, states, dch, B=B, S=S,
                                                     layer=l, ride=_chip_exchange(sums) if waiting else None)
        if waiting:
            grads[l + 1] = reduce_tail(sums, recv)
        at_end = exchange and l == 0
        part_o = [dwo.reshape(1, 4, 2, ro, D_MODEL)]
        (dqa, dza, dkv, dsk), got_o = _swa_bwd(proj, sink_b, cos, sin, dca, B=B, S=S,
                                              ride=_pair_exchange(part_o) if at_end else None)
        pieces = [dq, df, di, dz, dqa, dkv, dza]
        sums_o = [_pair_add(part_o[0], got_o[0])] if at_end else None
        (gwt,), recv_o = _grad_w_in(h, pieces, ride=_chip_exchange(sums_o) if at_end else None)
        part_t = [gwt.reshape(1, 4, 2, ri, D_MODEL)]
        if at_end:
            tm = min(512, T // 2)
            nb = T // tm
            na = max(1, nb // 4)
            (dx_a, dg_a), got_t = _in_proj_bwd(pieces, wt, tail, 0, x_in, g_pre[l:l + 1], dx, tm=tm, blocks=(0, na),
                                               ride=_pair_exchange(part_t))
            sums_t = [_pair_add(part_t[0], got_t[0])]
            (dx, dg_b), recv_t = _in_proj_bwd(pieces, wt, tail, 0, x_in, g_pre[l:l + 1], dx, tm=tm, blocks=(na, nb - na),
                                              dx_into=dx_a, ride=_chip_exchange(sums_t))
            dgpre = dg_a + dg_b
            grads[0] = reduce_tail(sums_t + sums_o, recv_t + recv_o)
        else:
            (dx, dgpre), _ = _in_proj_bwd(pieces, wt, tail, 0, x_in, g_pre[l:l + 1], dx)
            if exchange:
                waiting = part_t + part_o
            else:
                grads[l] = [gwt, dwo]
        gg_pre.append(dgpre[0])
        gg_post.append(dgpost[0])
        g_lb.append(jnp.sum(dlb, axis=(0, 1)))
        gg_head.append(jnp.sum(dgh, axis=(0, 1, 2)))
        g_sinks.append(dsk[:, 0])
    rev = lambda xs: jnp.stack(xs[::-1])
    return loss[0, 0], dx, grads, rev(gg_pre), rev(gg_post), rev(g_lb), rev(gg_head), rev(g_sinks)


MESH = pl.DeviceIdType.MESH
ANY = pl.BlockSpec(memory_space=pl.ANY)


def _place():
    x, y, c = lax.axis_index("x"), lax.axis_index("y"), lax.axis_index("c")
    return x, y, c, [(1 - x, y), (x, 1 - y), (1 - x, 1 - y)]


def _rcopy(src, dst, send, recv, k, to):
    return pltpu.make_async_remote_copy(src_ref=src, dst_ref=dst, send_sem=send.at[k], recv_sem=recv.at[k],
                                        device_id=to, device_id_type=MESH)


class _Exchange:
    def __init__(self, name, inputs, out_shapes, n_sems, plan, in_place=False):
        self.name, self.inputs, self.out_shapes, self.n_sems, self.plan = name, inputs, out_shapes, n_sems, plan
        self.aliases = {a: a for a in range(len(inputs))} if in_place else {}

    def start(self, ins, outs, send, recv):
        for cp in self.plan(ins, outs, send, recv)[0]:
            cp.start()

    def finish(self, ins, outs, send, recv):
        sent, arriving = self.plan(ins, outs, send, recv)
        for cp in arriving:
            cp.wait_recv()
        for cp in sent:
            cp.wait_send()

    def sems(self):
        return [pltpu.SemaphoreType.DMA((self.n_sems,)), pltpu.SemaphoreType.DMA((self.n_sems,))]


def _run_exchange(ex):
    n_in, n_out = len(ex.inputs), len(ex.out_shapes)

    def body(*refs):
        ins, outs = refs[:n_in], refs[n_in:n_in + n_out]
        send, recv = refs[n_in + n_out:]
        ex.start(ins, outs, send, recv)
        ex.finish(ins, outs, send, recv)

    return pl.pallas_call(
        body, name=ex.name, in_specs=[ANY] * n_in, out_specs=[ANY] * n_out, out_shape=ex.out_shapes,
        input_output_aliases=ex.aliases, scratch_shapes=ex.sems(),
    )(*ex.inputs)


def _call(body, operands, *, name, grid, in_specs, out_specs, out_shape, scratch_shapes=(), semantics, ride=None,
          aliases=None):
    aliases = dict(aliases or {})
    if ride is None:
        outs = pl.pallas_call(body, name=name, grid=grid, in_specs=in_specs, out_specs=out_specs, out_shape=out_shape,
                              input_output_aliases=aliases, scratch_shapes=list(scratch_shapes),
                              compiler_params=_params(*semantics))(*operands)
        return outs, []
    n_in, n_out, n_scr = len(in_specs), len(out_specs), len(scratch_shapes)
    r_in, r_out = len(ride.inputs), len(ride.out_shapes)

    def riding(*refs):
        refs = list(refs)
        ins, rins = refs[:n_in], refs[n_in:n_in + r_in]
        o0 = n_in + r_in
        outs, routs = refs[o0:o0 + n_out], refs[o0 + n_out:o0 + n_out + r_out]
        scr = refs[o0 + n_out + r_out:o0 + n_out + r_out + n_scr]
        send, recv = refs[-2:]
        ids = [pl.program_id(d) for d in range(len(grid))]
        first = functools.reduce(jnp.logical_and, [i == 0 for i in ids])
        last = functools.reduce(jnp.logical_and, [i == g - 1 for i, g in zip(ids, grid)])
        pl.when(first)(lambda: ride.start(rins, routs, send, recv))
        body(*ins, *outs, *scr)
        pl.when(last)(lambda: ride.finish(rins, routs, send, recv))

    res = pl.pallas_call(
        riding, name=name + "_" + ride.name, grid=grid,
        in_specs=list(in_specs) + [ANY] * r_in, out_specs=list(out_specs) + [ANY] * r_out,
        out_shape=list(out_shape) + list(ride.out_shapes),
        input_output_aliases={**aliases, **{n_in + a: n_out + b for a, b in ride.aliases.items()}},
        scratch_shapes=list(scratch_shapes) + ride.sems(),
        compiler_params=_params(*(["arbitrary"] * len(grid))),
    )(*operands, *ride.inputs)
    return res[:n_out], res[n_out:]


def _gather_ici(bufs, name="gather_ici"):
    n = len(bufs)

    def plan(ins, outs, send, recv):
        x, y, c, chips = _place()
        me = 2 * x + y
        sent, arriving = [], []
        for j, (px, py) in enumerate(chips):
            for a in range(n):
                mine, theirs = outs[a].at[:, me, c], outs[a].at[:, 2 * px + py, c]
                sent.append(_rcopy(mine, mine, send, recv, j * n + a, (px, py, c)))
                arriving.append(_rcopy(theirs, theirs, send, recv, j * n + a, (px, py, c)))
        return sent, arriving

    return _Exchange(name, bufs, [jax.ShapeDtypeStruct(b.shape, b.dtype) for b in bufs], 3 * n, plan, in_place=True)


def _gather_d2d(bufs, name="gather_d2d"):
    n = len(bufs)

    def plan(ins, outs, send, recv):
        x, y, c, chips = _place()
        sib = (x, y, 1 - c)
        sent, arriving = [], []
        for j, (px, py) in enumerate(chips):
            for a in range(n):
                got, theirs = outs[a].at[:, 2 * px + py, c], outs[a].at[:, 2 * px + py, 1 - c]
                sent.append(_rcopy(got, got, send, recv, j * n + a, sib))
                arriving.append(_rcopy(theirs, theirs, send, recv, j * n + a, sib))
        return sent, arriving

    return _Exchange(name, bufs, [jax.ShapeDtypeStruct(b.shape, b.dtype) for b in bufs], 3 * n, plan, in_place=True)


def _pair_exchange(parts):
    n = len(parts)

    def plan(ins, outs, send, recv):
        x, y, c, _ = _place()
        cps = [_rcopy(ins[a].at[:, :, 1 - c], outs[a], send, recv, a, (x, y, 1 - c)) for a in range(n)]
        return cps, cps

    return _Exchange("pair_exchange", parts,
                     [jax.ShapeDtypeStruct(p.shape[:2] + p.shape[3:], p.dtype) for p in parts], n, plan)


def _block_rows(r):
    return r if r <= 512 else r // 2


def _pair_add(part, got):
    L, K, _, r, C = part.shape
    rows = _block_rows(r)

    def body(c_ref, a_ref, b_ref, o_ref):
        o_ref[0, 0] = (a_ref[0, 0, 0] + b_ref[0, 0]).astype(bf16)

    blk = (1, 1, rows, C)
    return pl.pallas_call(
        body, name="pair_add",
        grid_spec=pltpu.PrefetchScalarGridSpec(
            num_scalar_prefetch=1, grid=(L, K, r // rows),
            in_specs=[pl.BlockSpec((1, 1, 1, rows, C), lambda l, k, i, c: (l, k, c[0], i, 0)),
                      pl.BlockSpec(blk, lambda l, k, i, c: (l, k, i, 0))],
            out_specs=pl.BlockSpec(blk, lambda l, k, i, c: (l, k, i, 0))),
        out_shape=jax.ShapeDtypeStruct((L, K, r, C), bf16),
        compiler_params=_params("parallel", "parallel", "parallel"),
    )(jnp.reshape(lax.axis_index("c"), (1,)).astype(jnp.int32), part, got)


def _chip_exchange(sums):
    n = len(sums)

    def plan(ins, outs, send, recv):
        x, y, c, chips = _place()
        cps = []
        for j, (px, py) in enumerate(chips):
            for a in range(n):
                cps.append(_rcopy(ins[a].at[:, 2 * px + py], outs[a].at[j], send, recv, j * n + a, (px, py, c)))
        return cps, cps

    return _Exchange("chip_exchange", sums,
                     [jax.ShapeDtypeStruct((3, s.shape[0]) + s.shape[2:], s.dtype) for s in sums], 3 * n, plan)


def _chip_sum(mine, got):
    L, K, r, C = mine.shape
    rows = _block_rows(r)

    def body(p_ref, a_ref, b_ref, o_ref):
        o_ref[0, 0] = (a_ref[0, 0].astype(f32) + b_ref[0, 0].astype(f32)) + (b_ref[1, 0].astype(f32) + b_ref[2, 0].astype(f32))

    place = jnp.stack([2 * lax.axis_index("x") + lax.axis_index("y"), lax.axis_index("c")]).astype(jnp.int32)
    return pl.pallas_call(
        body, name="chip_sum",
        grid_spec=pltpu.PrefetchScalarGridSpec(
            num_scalar_prefetch=1, grid=(L, r // rows),
            in_specs=[pl.BlockSpec((1, 1, rows, C), lambda l, i, p: (l, p[0], i, 0)),
                      pl.BlockSpec((3, 1, rows, C), lambda l, i, p: (0, l, i, 0))],
            out_specs=pl.BlockSpec((1, 1, rows, C), lambda l, i, p: (l, p[1], i, 0))),
        out_shape=jax.ShapeDtypeStruct((L, 2, r, C), f32),
        compiler_params=_params("parallel", "parallel"),
    )(place, mine, got)


def _pair_share(bufs):
    n = len(bufs)

    def plan(ins, outs, send, recv):
        x, y, c, _ = _place()
        sib = (x, y, 1 - c)
        sent = [_rcopy(outs[a].at[:, c], outs[a].at[:, c], send, recv, a, sib) for a in range(n)]
        arriving = [_rcopy(outs[a].at[:, 1 - c], outs[a].at[:, 1 - c], send, recv, a, sib) for a in range(n)]
        return sent, arriving

    return _Exchange("pair_share", bufs, [jax.ShapeDtypeStruct(b.shape, b.dtype) for b in bufs], n, plan, in_place=True)


def _all_sum_small(v):
    def body(v_ref, o_ref, buf, send, recv):
        x, y, c, _ = _place()
        me = 4 * x + 2 * y + c
        buf[me] = v_ref[...]
        cps = []
        for m in range(1, 8):
            to = (x ^ (m >> 2), y ^ ((m >> 1) & 1), c ^ (m & 1))
            cps.append(_rcopy(v_ref, buf.at[me], send, recv, m - 1, to))
        for cp in cps:
            cp.start()
        for cp in cps:
            cp.wait()
        acc = buf[0]
        for d in range(1, 8):
            acc = acc + buf[d]
        o_ref[...] = acc

    vm = pl.BlockSpec(memory_space=pltpu.VMEM)
    return pl.pallas_call(
        body, name="all_sum_small", in_specs=[vm], out_specs=vm,
        out_shape=jax.ShapeDtypeStruct(v.shape, v.dtype),
        scratch_shapes=[pltpu.VMEM((8,) + v.shape, v.dtype), pltpu.SemaphoreType.DMA((7,)), pltpu.SemaphoreType.DMA((7,))],
    )(v)


def _adamw_math(w, g, m, v):
    m = ADAM_B1 * m + (1.0 - ADAM_B1) * g
    v = ADAM_B2 * v + (1.0 - ADAM_B2) * (g * g)
    m_hat = m / (1.0 - ADAM_B1 ** ADAM_STEP)
    v_hat = v / (1.0 - ADAM_B2 ** ADAM_STEP)
    return -ADAM_LR * (m_hat / (jnp.sqrt(v_hat) + ADAM_EPS) + ADAM_WD * w), m, v


def _adamw(w, g, m, v):
    L, R, C = w.shape
    rows = R // 4

    def body(w_ref, g_ref, m_ref, v_ref, d_ref, mo_ref, vo_ref):
        d_ref[...], mo_ref[...], vo_ref[...] = _adamw_math(w_ref[...], g_ref[...], m_ref[...], v_ref[...])

    blk = pl.BlockSpec((1, rows, C), lambda l, i: (l, i, 0))
    return pl.pallas_call(
        body, name="adamw", grid=(L, R // rows), in_specs=[blk] * 4, out_specs=[blk] * 3,
        out_shape=[jax.ShapeDtypeStruct(w.shape, f32)] * 3,
        compiler_params=_params("parallel", "parallel"),
    )(w, g, m, v)


def _chip_index():
    return jnp.reshape(2 * lax.axis_index("x") + lax.axis_index("y"), (1,)).astype(jnp.int32)


def _shard_placed(w, l):
    _, R, C = w.shape
    rows = R // 4

    def body(k_ref, w_ref, o_ref):
        o_ref[0, 0] = w_ref[0].astype(bf16)

    return pl.pallas_call(
        body, name="shard_placed",
        grid_spec=pltpu.PrefetchScalarGridSpec(
            num_scalar_prefetch=1, grid=(R // rows,),
            in_specs=[pl.BlockSpec((1, rows, C), lambda i, k: (l, i, 0))],
            out_specs=pl.BlockSpec((1, 1, rows, C), lambda i, k: (0, k[0], i, 0))),
        out_shape=jax.ShapeDtypeStruct((1, 4, R, C), bf16),
        compiler_params=_params("parallel"),
    )(_chip_index(), w)


SMALL_ROWS = 4 * DEPTH


def _pack_small(g_pre, g_post, lb, g_head, sinks, loss=None):
    rows = []
    for l in range(DEPTH):
        tail = [g_head[l], sinks[l]]
        if loss is not None and l == 0:
            tail.append(jnp.reshape(loss, (1,)))
        tail = jnp.concatenate(tail)
        rows += [g_pre[l], g_post[l], lb[l], jnp.pad(tail, (0, D_MODEL - tail.shape[0]))]
    return jnp.stack(rows)


def _unpack_small(p):
    g_pre = jnp.stack([p[4 * l] for l in range(DEPTH)])
    g_post = jnp.stack([p[4 * l + 1] for l in range(DEPTH)])
    lb = jnp.stack([p[4 * l + 2] for l in range(DEPTH)])
    g_head = jnp.stack([p[4 * l + 3, :HG_HEAD_DIM] for l in range(DEPTH)])
    sinks = jnp.stack([p[4 * l + 3, HG_HEAD_DIM:HG_HEAD_DIM + ATT_HEADS] for l in range(DEPTH)])
    return g_pre, g_post, lb, g_head, sinks


def _small_update(gsum, w, m, v):
    def body(g_ref, w_ref, m_ref, v_ref, go_ref, d_ref, mo_ref, vo_ref):
        g = g_ref[...]
        w = w_ref[...]
        lbp = [w[4 * l + 2:4 * l + 3] for l in range(DEPTH)]
        mx = functools.reduce(jnp.maximum, lbp)
        e = [jnp.exp(t - mx) for t in lbp]
        tot = functools.reduce(jnp.add, e)
        p = [t / tot for t in e]
        glb = [g[4 * l + 2:4 * l + 3] for l in range(DEPTH)]
        row = lax.broadcasted_iota(jnp.int32, g.shape, 0)
        for j in range(DEPTH):
            gj = jnp.zeros_like(p[0])
            for l in range(DEPTH):
                for i in range(1, l + 1):
                    gj = gj + glb[l] * p[i] * ((1.0 if i == j else 0.0) - p[j])
            g = jnp.where(row == 4 * j + 2, gj, g)
        go_ref[...] = g
        d_ref[...], mo_ref[...], vo_ref[...] = _adamw_math(w, g, m_ref[...], v_ref[...])

    vm = pl.BlockSpec(memory_space=pltpu.VMEM)
    return pl.pallas_call(
        body, name="small_update", in_specs=[vm] * 4, out_specs=[vm] * 4,
        out_shape=[jax.ShapeDtypeStruct(gsum.shape, f32)] * 4,
    )(gsum, w, m, v)


def kernel(x, w_in, w_out, g_pre, g_post, lb_param, g_head, sinks, loss_target, m_w_in, m_w_out, m_g_pre, m_g_post, m_lb_param, m_g_head, m_sinks, v_w_in, v_w_out, v_g_pre, v_g_post, v_lb_param, v_g_head, v_sinks):
    B, S, _ = x.shape
    T = B * S
    L = DEPTH
    ri, ro = IN_WIDTH // 8, MIX_WIDTH // 8
    tr = lambda a: jnp.transpose(a, (0, 2, 1))
    wt, mt, vt = tr(w_in), tr(m_w_in), tr(v_w_in)
    bufs = [[_shard_placed(wt, l).reshape(1, 4, 2, ri, D_MODEL), _shard_placed(w_out, l).reshape(1, 4, 2, ro, D_MODEL)]
            for l in range(L)]
    loss, dx, grads, ggpre, ggpost, glb, gghead, gsinks = _train_step(
        x.reshape(T, D_MODEL), loss_target.reshape(T, D_MODEL), bufs, g_pre, g_post, lb_param, g_head, sinks,
        B=B, S=S, exchange=True)
    gwt_mine = jnp.concatenate([g[0] for g in grads], axis=0).reshape(L, 2 * ri, D_MODEL)
    grad_w_out = jnp.concatenate([g[1] for g in grads], axis=0).reshape(L, 2 * ro, D_MODEL)

    d_wt, nm_wt, nv_wt = _adamw(wt, gwt_mine, mt, vt)
    grad_w_in, d_w_in, nm_w_in, nv_w_in = tr(gwt_mine), tr(d_wt), tr(nm_wt), tr(nv_wt)
    d_w_out, nm_w_out, nv_w_out = _adamw(w_out, grad_w_out, m_w_out, v_w_out)

    gsum = _all_sum_small(_pack_small(ggpre, ggpost, glb, gghead, gsinks, loss))
    gs, ds, ms, vs = _small_update(
        gsum, _pack_small(g_pre, g_post, lb_param, g_head, sinks),
        _pack_small(m_g_pre, m_g_post, m_lb_param, m_g_head, m_sinks),
        _pack_small(v_g_pre, v_g_post, v_lb_param, v_g_head, v_sinks))
    loss_all = gsum[3, HG_HEAD_DIM + ATT_HEADS]
    return (loss_all, dx.reshape(B, S, D_MODEL), grad_w_in, grad_w_out, *_unpack_small(gs),
            d_w_in, d_w_out, *_unpack_small(ds), nm_w_in, nm_w_out, *_unpack_small(ms),
            nv_w_in, nv_w_out, *_unpack_small(vs))
```

```python
import functools
import math

import jax
import jax.numpy as jnp
from jax import lax
from jax.experimental import pallas as pl
from jax.experimental.pallas import tpu as pltpu

f32 = jnp.float32
bf16 = jnp.bfloat16

D_MODEL = 1024
DEPTH = 2
HG_WIDTH = 1024
HG_HEAD_DIM = 128
HG_HEADS = 8
CHUNK = 64
SUB = 16
ATT_WIDTH = 1024
ATT_HEAD_DIM = 64
ATT_HEADS = 16
ATT_GROUP = 8
KV_WIDTH = 128
ATT_BLOCK = 128
ATT_SCALE = 1.0 / math.sqrt(ATT_HEAD_DIM)
ROPE_THETA = 10000.0
IN_WIDTH = 6400
MIX_WIDTH = 2048
NORM_EPS = 1e-6
NEG_INF = -1e30
LB_FLOOR = 1e-20
LANES = 128
VMEM_LIMIT = 48 * 1024 * 1024

ADAM_LR = 0.001
ADAM_B1 = 0.9
ADAM_B2 = 0.999
ADAM_EPS = 1e-08
ADAM_WD = 0.01
ADAM_STEP = 10

QA_BLK, ZA_BLK, KV_BLK = 4, 5, 24

NT = (((1,), (1,)), ((), ()))
TN = (((0,), (0,)), ((), ()))


def _dot(a, b, dims=None, precision=None):
    if dims is None:
        return jnp.dot(a, b, preferred_element_type=f32, precision=precision)
    return lax.dot_general(a, b, dims, preferred_element_type=f32, precision=precision)


def _sigmoid(x):
    return 1.0 / (1.0 + jnp.exp(-x))


def _params(*sem):
    return pltpu.CompilerParams(dimension_semantics=sem, vmem_limit_bytes=VMEM_LIMIT)


TAIL = IN_WIDTH - 5120


def _in_proj(x, g, wt, tail, l, *, tm=1024):
    T = x.shape[0]
    tm = min(tm, T)
    nmain = 5120 // TAIL

    def body(x_ref, g_ref, w_ref, t_ref, p_ref, h_ref, hs):
        j = pl.program_id(1)

        @pl.when(j == 0)
        def _():
            xv = x_ref[...]
            r = lax.rsqrt(jnp.mean(xv * xv, axis=-1, keepdims=True) + NORM_EPS)
            hv = (xv * r * g_ref[...]).astype(bf16)
            hs[...] = hv
            h_ref[...] = hv

        @pl.when(j < nmain)
        def _():
            p_ref[...] = _dot(hs[...], w_ref[...], NT)

        @pl.when(j == nmain)
        def _():
            p_ref[...] = _dot(hs[...], t_ref[...], NT)

    return pl.pallas_call(
        body, name="in_proj", grid=(T // tm, nmain + 1),
        in_specs=[pl.BlockSpec((tm, D_MODEL), lambda i, j: (i, 0)),
                  pl.BlockSpec((1, D_MODEL), lambda i, j: (0, 0)),
                  pl.BlockSpec((None, TAIL, D_MODEL), lambda i, j: (l, jnp.minimum(j, nmain - 1), 0)),
                  pl.BlockSpec((None, TAIL, D_MODEL), lambda i, j: (l, 0, 0))],
        out_specs=[pl.BlockSpec((tm, TAIL), lambda i, j: (i, j)),
                   pl.BlockSpec((tm, D_MODEL), lambda i, j: (i, 0))],
        out_shape=[jax.ShapeDtypeStruct((T, IN_WIDTH), f32), jax.ShapeDtypeStruct((T, D_MODEL), bf16)],
        scratch_shapes=[pltpu.VMEM((tm, D_MODEL), bf16)],
        compiler_params=_params("parallel", "arbitrary"),
    )(x, g, wt, tail)


def _out_proj(ch, ca, wo, l, x, g, *, tm=512):
    T = x.shape[0]
    tm = min(tm, T)
    half = MIX_WIDTH // 2

    def body(ch_ref, ca_ref, wo_ref, x_ref, g_ref, xn_ref, y_ref):
        y = _dot(ch_ref[...], wo_ref[0:half, :]) + _dot(ca_ref[...], wo_ref[half:MIX_WIDTH, :])
        r = lax.rsqrt(jnp.mean(y * y, axis=-1, keepdims=True) + NORM_EPS)
        y_ref[...] = y
        xn_ref[...] = x_ref[...] + y * r * g_ref[...]

    row = lambda i: (i, 0)
    fixed = lambda i: (0, 0)
    return pl.pallas_call(
        body, name="out_proj", grid=(T // tm,),
        in_specs=[pl.BlockSpec((tm, half), row), pl.BlockSpec((tm, half), row),
                  pl.BlockSpec((None, MIX_WIDTH, D_MODEL), lambda i: (l, 0, 0)), pl.BlockSpec((tm, D_MODEL), row),
                  pl.BlockSpec((1, D_MODEL), fixed)],
        out_specs=[pl.BlockSpec((tm, D_MODEL), row), pl.BlockSpec((tm, D_MODEL), row)],
        out_shape=[jax.ShapeDtypeStruct((T, D_MODEL), f32)] * 2,
        compiler_params=_params("parallel"),
    )(ch, ca, wo, x, g)


def _loss_head(y, target, *, tm=512):
    T = y.shape[0]
    tm = min(tm, T)

    def body(y_ref, t_ref, d_ref, l_ref):
        @pl.when(pl.program_id(0) == 0)
        def _():
            l_ref[...] = jnp.zeros_like(l_ref)
        err = y_ref[...] - t_ref[...]
        d_ref[...] = err * (1.0 / D_MODEL)
        l_ref[...] += jnp.sum(err * err) * (0.5 / D_MODEL)

    row = lambda i: (i, 0)
    return pl.pallas_call(
        body, name="loss_head", grid=(T // tm,),
        in_specs=[pl.BlockSpec((tm, D_MODEL), row), pl.BlockSpec((tm, D_MODEL), row)],
        out_specs=[pl.BlockSpec((tm, D_MODEL), row), pl.BlockSpec((8, LANES), lambda i: (0, 0))],
        out_shape=[jax.ShapeDtypeStruct((T, D_MODEL), f32), jax.ShapeDtypeStruct((8, LANES), f32)],
        compiler_params=_params("arbitrary"),
    )(y, target)


def _out_proj_bwd(dxn, y, g, wo, l, ch, ca, *, tm=256, ride=None):
    T = y.shape[0]
    tm = min(tm, T)
    half = MIX_WIDTH // 2

    def body(dx_ref, y_ref, g_ref, wo_ref, ch_ref, ca_ref, dch_ref, dca_ref, dwo_ref, dg_ref):
        @pl.when(pl.program_id(0) == 0)
        def _():
            dwo_ref[...] = jnp.zeros_like(dwo_ref)
            dg_ref[...] = jnp.zeros_like(dg_ref)
        y = y_ref[...]
        dx = dx_ref[...]
        r = lax.rsqrt(jnp.mean(y * y, axis=-1, keepdims=True) + NORM_EPS)
        gy = dx * g_ref[...]
        dy = r * gy - y * (r * r * r) * jnp.mean(gy * y, axis=-1, keepdims=True)
        dg_ref[...] += jnp.sum(dx * y * r, axis=0, keepdims=True)
        dyb = dy.astype(bf16)
        dch_ref[...] = _dot(dyb, wo_ref[0:half, :], NT)
        dca_ref[...] = _dot(dyb, wo_ref[half:MIX_WIDTH, :], NT)
        dwo_ref[0:half, :] += _dot(ch_ref[...], dyb, TN)
        dwo_ref[half:MIX_WIDTH, :] += _dot(ca_ref[...], dyb, TN)

    row = lambda i: (i, 0)
    fixed = lambda i: (0, 0)
    return _call(
        body, (dxn, y, g, wo, ch, ca), name="out_proj_bwd", grid=(T // tm,),
        in_specs=[pl.BlockSpec((tm, D_MODEL), row), pl.BlockSpec((tm, D_MODEL), row),
                  pl.BlockSpec((1, D_MODEL), fixed), pl.BlockSpec((None, MIX_WIDTH, D_MODEL), lambda i: (l, 0, 0)),
                  pl.BlockSpec((tm, half), row), pl.BlockSpec((tm, half), row)],
        out_specs=[pl.BlockSpec((tm, half), row), pl.BlockSpec((tm, half), row),
                   pl.BlockSpec((MIX_WIDTH, D_MODEL), fixed), pl.BlockSpec((1, D_MODEL), fixed)],
        out_shape=[jax.ShapeDtypeStruct((T, half), f32), jax.ShapeDtypeStruct((T, half), f32),
                   jax.ShapeDtypeStruct((MIX_WIDTH, D_MODEL), f32), jax.ShapeDtypeStruct((1, D_MODEL), f32)],
        semantics=("arbitrary",), ride=ride)


TILE = 256
PIECE_TILES = (4, 4, 4, 4, 4, 1, 4)
PIECE_START = tuple(sum(PIECE_TILES[:p]) for p in range(len(PIECE_TILES)))
N_TILES = sum(PIECE_TILES)


def _piece_specs(rows, index):
    def spec(s, n):
        def index_map(*g):
            r, t = index(*g)
            return r, jnp.clip(t - s, 0, n - 1)
        return pl.BlockSpec((rows, TILE), index_map)
    return [spec(s, n) for s, n in zip(PIECE_START, PIECE_TILES)]


def _for_piece(t, fn):
    for p, (s, n) in enumerate(zip(PIECE_START, PIECE_TILES)):
        pl.when((t >= s) & (t < s + n))(functools.partial(fn, p))


def _in_proj_bwd(pieces, wt, tail, l, x, g, dxn, *, tm=512, blocks=None, dx_into=None, ride=None):
    T = x.shape[0]
    tm = min(tm, T)
    first, count = blocks or (0, T // tm)
    npc = len(pieces)
    nk = npc
    nmain = npc - 2
    wide = ATT_WIDTH
    extra = [] if dx_into is None else [dx_into]

    def body(*refs):
        dp_refs = refs[:npc]
        w_ref, tz_ref, tkv_ref, x_ref, g_ref, dxn_ref = refs[npc:npc + 6]
        dx_ref, dg_ref, acc = refs[npc + 6 + len(extra):]
        i, k = pl.program_id(0), pl.program_id(1)

        @pl.when((i == 0) & (k == 0))
        def _():
            dg_ref[...] = jnp.zeros_like(dg_ref)

        @pl.when(k == 0)
        def _():
            acc[...] = jnp.zeros_like(acc)

        for p in range(npc):
            w_p = w_ref if p < nmain else (tkv_ref if p == nmain else tz_ref)

            def add(p=p, w_p=w_p):
                acc[...] += _dot(dp_refs[p][...], w_p[...])
            pl.when(k == p)(add)

        @pl.when(k == nk - 1)
        def _():
            dh = acc[...]
            xv = x_ref[...]
            r = lax.rsqrt(jnp.mean(xv * xv, axis=-1, keepdims=True) + NORM_EPS)
            gy = dh * g_ref[...]
            dx_ref[...] = dxn_ref[...] + r * gy - xv * (r * r * r) * jnp.mean(gy * xv, axis=-1, keepdims=True)
            dg_ref[...] += jnp.sum(dh * xv * r, axis=0, keepdims=True)

    rows = lambda i, k: (first + i, 0)
    return _call(
        body, (*pieces, wt, tail, tail, x, g, dxn, *extra), name="in_proj_bwd", grid=(count, nk),
        in_specs=[pl.BlockSpec((tm, p.shape[1]), rows) for p in pieces] + [
            pl.BlockSpec((None, wide, D_MODEL), lambda i, k: (l, jnp.minimum(k, nmain - 1), 0)),
            pl.BlockSpec((None, wide, D_MODEL), lambda i, k: (l, 0, 0)),
            pl.BlockSpec((None, TAIL - wide, D_MODEL), lambda i, k: (l, wide // (TAIL - wide), 0)),
            pl.BlockSpec((tm, D_MODEL), rows), pl.BlockSpec((1, D_MODEL), lambda i, k: (0, 0)),
            pl.BlockSpec((tm, D_MODEL), rows)] + [ANY] * len(extra),
        out_specs=[pl.BlockSpec((tm, D_MODEL), rows), pl.BlockSpec((1, D_MODEL), lambda i, k: (0, 0))],
        out_shape=[jax.ShapeDtypeStruct((T, D_MODEL), f32), jax.ShapeDtypeStruct((1, D_MODEL), f32)],
        scratch_shapes=[pltpu.VMEM((tm, D_MODEL), f32)],
        semantics=("arbitrary", "arbitrary"), ride=ride, aliases={npc + 6: 0} if extra else None)


def _grad_w_in(h, pieces, *, ride=None):
    T = h.shape[0]
    npc = len(pieces)

    def body(*refs):
        h_ref, dp_refs, o_ref = refs[0], refs[1:1 + npc], refs[1 + npc]

        def put(p):
            o_ref[...] = _dot(dp_refs[p][...], h_ref[...], TN)
        _for_piece(pl.program_id(0), put)

    return _call(
        body, (h, *pieces), name="grad_w_in", grid=(N_TILES,),
        in_specs=[pl.BlockSpec((T, D_MODEL), lambda j: (0, 0), pipeline_mode=pl.Buffered(1))]
        + _piece_specs(T, lambda j: (0, j)),
        out_specs=[pl.BlockSpec((TILE, D_MODEL), lambda j: (j, 0))],
        out_shape=[jax.ShapeDtypeStruct((IN_WIDTH, D_MODEL), f32)],
        semantics=("parallel",), ride=ride)


def _lower_bound(lbp, layer):
    m = jnp.max(lbp, axis=0, keepdims=True)
    e = jnp.exp(lbp - m)
    p = e / jnp.sum(e, axis=0, keepdims=True)
    acc = p[0:1]
    for i in range(1, layer + 1):
        acc = acc + p[i:i + 1]
    return acc - p[0:1]


def _gate_parts(qr, fr, lb, lbf):
    sq = _sigmoid(qr)
    e = jnp.exp(-jnp.abs(fr))
    inv = 1.0 / (1.0 + e)
    pos = fr >= 0
    sg = jnp.where(pos, inv, e * inv)
    nsg = jnp.where(pos, e * inv, inv)
    fg = lbf + (1.0 - lb) * sg
    return qr * sq, sq, sg, nsg, fg, jnp.log(fg), (1.0 - lb) * nsg


LEVELS = tuple(SUB << j for j in range((CHUNK // SUB).bit_length() - 1))


def _level_masks(transposed=False):
    t = lax.broadcasted_iota(jnp.int32, (CHUNK, CHUNK), 1 if transposed else 0)
    s = lax.broadcasted_iota(jnp.int32, (CHUNK, CHUNK), 0 if transposed else 1)
    return [(t % (2 * m) >= m) & (s % (2 * m) < m) & (t // (2 * m) == s // (2 * m)) for m in LEVELS]


def _level_anchor(b_s, row, m):
    beta = b_s[m - 1:m, :]
    for g in range(1, CHUNK // (2 * m)):
        beta = jnp.where(row >= g * 2 * m, b_s[g * 2 * m + m - 1:g * 2 * m + m, :], beta)
    return beta


def _seg_sum(seg, x):
    hi = x.astype(bf16)
    return _dot(seg, hi) + _dot(seg, (x - hi.astype(f32)).astype(bf16))


def _hgrn_fwd(proj, lb_param, g_head, *, B, S, layer, ride=None):
    T = B * S
    TB = min(512, S)
    nT, NC = S // TB, TB // CHUNK
    nC = S // CHUNK
    HD = HG_HEAD_DIM

    def body(q_ref, f_ref, i_ref, z_ref, lb_ref, gh_ref, cat_ref, op_ref, st_ref,
             s_scr, b_scr, k_scr):
        @pl.when(pl.program_id(2) == 0)
        def _():
            s_scr[...] = jnp.zeros_like(s_scr)
        lb = _lower_bound(lb_ref[...], layer)
        lbf = jnp.maximum(lb, LB_FLOOR)
        gh = gh_ref[...]
        r_i = lax.broadcasted_iota(jnp.int32, (CHUNK, CHUNK), 0)
        c_i = lax.broadcasted_iota(jnp.int32, (CHUNK, CHUNK), 1)
        tril = (r_i >= c_i).astype(bf16)
        rows8 = lax.broadcasted_iota(jnp.int32, (8, HD), 0)
        row_c = lax.broadcasted_iota(jnp.int32, (CHUNK, HD), 0)
        lane_c = lax.broadcasted_iota(jnp.int32, (8, CHUNK), 1)
        masks = _level_masks()

        def chunk(c, st):
            rs = slice(c * CHUNK, (c + 1) * CHUNK)
            b_s, k_s = b_scr.at[c], k_scr.at[c]
            q, _, _, _, _, logf, k = _gate_parts(q_ref[rs, :], f_ref[rs, :], lb, lbf)
            v = i_ref[rs, :]
            b = _seg_sum(tril, logf)
            b_s[...] = b
            k_s[...] = k
            pieces = []
            for blk in range(CHUNK // SUB):
                r0 = blk * SUB
                bp = [b[r0 + 8 * i:r0 + 8 * i + 8] for i in range(SUB // 8)]
                qp = [q[r0 + 8 * i:r0 + 8 * i + 8] for i in range(SUB // 8)]
                ap = [jnp.zeros((8, CHUNK), f32) for _ in range(SUB // 8)]
                for s in range(SUB):
                    bs = b_s[r0 + s:r0 + s + 1, :]
                    ks = k_s[r0 + s:r0 + s + 1, :]
                    for i in range(s // 8, SUB // 8):
                        diff = bp[i] - bs
                        if i == s // 8:
                            diff = jnp.where(rows8 >= s - 8 * i, diff, NEG_INF)
                        col = jnp.sum(jnp.exp(diff) * qp[i] * ks, axis=1, keepdims=True)
                        ap[i] = jnp.where(lane_c == r0 + s, col, ap[i])
                pieces += ap
            a_all = jnp.concatenate(pieces, axis=0)
            for m, mk in zip(LEVELS, masks):
                beta = _level_anchor(b_s, row_c, m)
                qh = (q * jnp.exp(jnp.minimum(b - beta, 0.0))).astype(bf16)
                kh = (k * jnp.exp(jnp.minimum(beta - b, 0.0))).astype(bf16)
                a_all = a_all + jnp.where(mk, _dot(qh, kh, NT), 0.0)
            st_ref[0, 0, c] = st
            vb16 = v.astype(bf16)
            o = _dot(a_all.astype(bf16), vb16) + _dot((q * jnp.exp(b)).astype(bf16), st.astype(bf16), NT)
            b_end = b_s[CHUNK - 1:CHUNK, :]
            kdec = (k * jnp.exp(b_end - b)).astype(bf16)
            st_next = jnp.exp(b_end) * st + _dot(vb16, kdec, TN)
            rr = lax.rsqrt(jnp.mean(o * o, axis=-1, keepdims=True) + NORM_EPS)
            zr = z_ref[rs, :]
            cat_ref[rs, :] = (o * rr * gh * (zr * _sigmoid(zr))).astype(bf16)
            op_ref[rs, :] = o
            return st_next

        st = s_scr[...]
        for c in range(NC):
            st = chunk(c, st)
        s_scr[...] = st

    def col(part):
        return pl.BlockSpec((TB, HD), lambda b, h, n: (b * nT + n, part * HG_HEADS + h))

    out_col = pl.BlockSpec((TB, HD), lambda b, h, n: (b * nT + n, h))
    return _call(
        body, (proj, proj, proj, proj, lb_param, g_head),
        name=f"hgrn_fwd_l{layer}", grid=(B, HG_HEADS, nT),
        in_specs=[col(0), col(1), col(2), col(3),
                  pl.BlockSpec((DEPTH, HD), lambda b, h, n: (0, h)),
                  pl.BlockSpec((1, HD), lambda b, h, n: (0, 0))],
        out_specs=[out_col, out_col,
                   pl.BlockSpec((1, 1, NC, HD, HD), lambda b, h, n: (b, h, n, 0, 0))],
        out_shape=[jax.ShapeDtypeStruct((T, HG_WIDTH), bf16), jax.ShapeDtypeStruct((T, HG_WIDTH), f32),
                   jax.ShapeDtypeStruct((B, HG_HEADS, nC, HD, HD), f32)],
        scratch_shapes=[pltpu.VMEM((HD, HD), f32), pltpu.VMEM((NC, CHUNK, HD), f32), pltpu.VMEM((NC, CHUNK, HD), f32)],
        semantics=("parallel", "parallel", "arbitrary"), ride=ride)


def _hgrn_bwd(proj, lb_param, g_head, o_pre, states, dcat, *, B, S, layer, ride=None):
    T = B * S
    TB = min(512, S)
    nT, NC = S // TB, TB // CHUNK
    HD = HG_HEAD_DIM

    def body(q_ref, f_ref, i_ref, z_ref, lb_ref, gh_ref, op_ref, st_ref, dc_ref,
             dq_ref, df_ref, di_ref, dz_ref, dlb_ref, dgh_ref,
             ds_scr, b_scr, q_scr, do_scr, wk_scr):
        @pl.when(pl.program_id(2) == 0)
        def _():
            ds_scr[...] = jnp.zeros_like(ds_scr)
            dlb_ref[...] = jnp.zeros_like(dlb_ref)
            dgh_ref[...] = jnp.zeros_like(dgh_ref)
        lb = _lower_bound(lb_ref[...], layer)
        lbf = jnp.maximum(lb, LB_FLOOR)
        ind = (lb > LB_FLOOR).astype(f32)
        gh = gh_ref[...]
        r_i = lax.broadcasted_iota(jnp.int32, (CHUNK, CHUNK), 0)
        c_i = lax.broadcasted_iota(jnp.int32, (CHUNK, CHUNK), 1)
        tril = (r_i >= c_i).astype(bf16)
        triu = (c_i >= r_i).astype(bf16)
        rows8 = lax.broadcasted_iota(jnp.int32, (8, HD), 0)
        row_c = lax.broadcasted_iota(jnp.int32, (CHUNK, HD), 0)
        lane_c = lax.broadcasted_iota(jnp.int32, (8, CHUNK), 1)
        last_row = row_c == CHUNK - 1
        masks = _level_masks()
        masks_t = _level_masks(transposed=True)
        seg_t = lax.broadcasted_iota(jnp.int32, (SUB, 8 * SUB), 0)
        seg_r = lax.broadcasted_iota(jnp.int32, (SUB, 8 * SUB), 1) // 8
        seg0 = (seg_r == seg_t).astype(bf16)
        seg1 = (seg_r[:, 0:4 * SUB] + 8 == seg_t[:, 0:4 * SUB]).astype(bf16)

        def chunk(c, dst1):
            rs = slice(c * CHUNK, (c + 1) * CHUNK)
            b_s, q_s, do_s = b_scr.at[c], q_scr.at[c], do_scr.at[c]
            qr, fr = q_ref[rs, :], f_ref[rs, :]
            q, sq, sg, nsg, fg, logf, k = _gate_parts(qr, fr, lb, lbf)
            v = i_ref[rs, :]
            b = _seg_sum(tril, logf)
            o = op_ref[rs, :]
            dc = dc_ref[rs, :]
            zr = z_ref[rs, :]
            sz = _sigmoid(zr)
            rr = lax.rsqrt(jnp.mean(o * o, axis=-1, keepdims=True) + NORM_EPS)
            dz_ref[rs, :] = (dc * (o * rr * gh) * (sz * (1.0 + zr * (1.0 - sz)))).astype(bf16)
            dn = dc * (zr * sz)
            dgh_ref[0, 0] += jnp.sum(dn * o * rr, axis=0, keepdims=True)
            gdn = dn * gh
            d_o = rr * gdn - o * (rr * rr * rr) * jnp.mean(gdn * o, axis=-1, keepdims=True)
            b_s[...] = b
            q_s[...] = q
            do_s[...] = d_o
            dob = d_o.astype(bf16)
            vb16 = v.astype(bf16)
            d_a = _dot(dob, vb16, NT)
            d_q = jnp.zeros((CHUNK, HD), f32)
            d_k = jnp.zeros((CHUNK, HD), f32)
            at_all = jnp.zeros((CHUNK, CHUNK), f32)
            for m, mk, mkt in zip(LEVELS, masks, masks_t):
                beta = _level_anchor(b_s, row_c, m)
                eq = jnp.exp(jnp.minimum(b - beta, 0.0))
                ek = jnp.exp(jnp.minimum(beta - b, 0.0))
                qh = (q * eq).astype(bf16)
                kh = (k * ek).astype(bf16)
                at_all = at_all + jnp.where(mkt, _dot(kh, qh, NT), 0.0)
                d_aa = jnp.where(mk, d_a, 0.0).astype(bf16)
                d_q = d_q + _dot(d_aa, kh) * eq
                d_k = d_k + _dot(d_aa, qh, TN) * ek
            st0 = st_ref[0, 0, c]
            dst1b = dst1.astype(bf16)
            eb = jnp.exp(b)
            b_end = b_s[CHUNK - 1:CHUNK, :]
            edec = jnp.exp(b_end - b)
            e_end = jnp.exp(b_end)
            kdec = (k * edec).astype(bf16)
            qdec = (q * eb).astype(bf16)
            d_q = d_q + _dot(dob, st0.astype(bf16)) * eb
            d_v = _dot(kdec, dst1b, NT)
            d_k = d_k + _dot(vb16, dst1b) * edec
            st1 = e_end * st0 + _dot(vb16, kdec, TN)
            rterm = jnp.sum(dst1 * st1, axis=0, keepdims=True)
            dst0 = e_end * dst1 + _dot(dob, qdec, TN)
            dq_blocks, dk_pieces, at_pieces = [], [], []
            for blk in range(CHUNK // SUB):
                r0 = blk * SUB
                wk = wk_scr.at[c * (CHUNK // SUB) + blk]
                bp = [b[r0 + 8 * i:r0 + 8 * i + 8] for i in range(SUB // 8)]
                kp = [k[r0 + 8 * i:r0 + 8 * i + 8] for i in range(SUB // 8)]
                vp = [v[r0 + 8 * i:r0 + 8 * i + 8] for i in range(SUB // 8)]
                dkp = [jnp.zeros((8, HD), f32) for _ in range(SUB // 8)]
                atp = [jnp.zeros((8, CHUNK), f32) for _ in range(SUB // 8)]
                for t in range(SUB):
                    bt = b_s[r0 + t:r0 + t + 1, :]
                    qt = q_s[r0 + t:r0 + t + 1, :]
                    dot_ = do_s[r0 + t:r0 + t + 1, :]
                    for i in range(t // 8 + 1):
                        diff = bt - bp[i]
                        if i == t // 8:
                            diff = jnp.where(rows8 <= t - 8 * i, diff, NEG_INF)
                        e = jnp.exp(diff)
                        a = jnp.sum(e * kp[i] * qt, axis=1, keepdims=True)
                        atp[i] = jnp.where(lane_c == r0 + t, a, atp[i])
                        w = jnp.sum(vp[i] * dot_, axis=1, keepdims=True) * e
                        dkp[i] = dkp[i] + w * qt
                        row = 8 * t if i == 0 else 8 * SUB + 8 * (t - 8)
                        wk[row:row + 8, :] = w * kp[i]
                dq_blocks.append(_seg_sum(seg0, wk[0:8 * SUB, :]) + _seg_sum(seg1, wk[8 * SUB:12 * SUB, :]))
                dk_pieces += dkp
                at_pieces += atp
            d_q = d_q + jnp.concatenate(dq_blocks, axis=0)
            d_k = d_k + jnp.concatenate(dk_pieces, axis=0)
            d_v = d_v + _dot((at_all + jnp.concatenate(at_pieces, axis=0)).astype(bf16), dob)
            db = q * d_q - k * d_k + jnp.where(last_row, rterm, 0.0)
            dlt = _seg_sum(triu, db) - fg * d_k
            df_ref[rs, :] = (dlt * (1.0 - lb) * sg * nsg / fg).astype(bf16)
            dlb_ref[0] += jnp.sum(dlt * (ind - sg) / fg, axis=0, keepdims=True)
            dq_ref[rs, :] = (d_q * (sq * (1.0 + qr * (1.0 - sq)))).astype(bf16)
            di_ref[rs, :] = d_v.astype(bf16)
            return dst0

        dst = ds_scr[...]
        for c in reversed(range(NC)):
            dst = chunk(c, dst)
        ds_scr[...] = dst

    def col(part):
        return pl.BlockSpec((TB, HD), lambda b, h, n: (b * nT + nT - 1 - n, part * HG_HEADS + h))

    hcol = pl.BlockSpec((TB, HD), lambda b, h, n: (b * nT + nT - 1 - n, h))
    return _call(
        body, (proj, proj, proj, proj, lb_param, g_head, o_pre, states, dcat),
        name=f"hgrn_bwd_l{layer}", grid=(B, HG_HEADS, nT),
        in_specs=[col(0), col(1), col(2), col(3),
                  pl.BlockSpec((DEPTH, HD), lambda b, h, n: (0, h)),
                  pl.BlockSpec((1, HD), lambda b, h, n: (0, 0)),
                  hcol,
                  pl.BlockSpec((1, 1, NC, HD, HD), lambda b, h, n: (b, h, nT - 1 - n, 0, 0)),
                  hcol],
        out_specs=[hcol, hcol, hcol, hcol,
                   pl.BlockSpec((1, 1, HD), lambda b, h, n: (b, 0, h)),
                   pl.BlockSpec((1, 1, 1, HD), lambda b, h, n: (b, h, 0, 0))],
        out_shape=[jax.ShapeDtypeStruct((T, HG_WIDTH), bf16)] * 4 + [
            jax.ShapeDtypeStruct((B, 1, HG_WIDTH), f32), jax.ShapeDtypeStruct((B, HG_HEADS, 1, HD), f32)],
        scratch_shapes=[pltpu.VMEM((HD, HD), f32)] + [pltpu.VMEM((NC, CHUNK, HD), f32)] * 3
        + [pltpu.VMEM((NC * CHUNK // SUB, 12 * SUB, HD), f32)],
        semantics=("parallel", "parallel", "arbitrary"), ride=ride)


def _rope_tables(S):
    half = ATT_HEAD_DIM // 2
    inv_freq = ROPE_THETA ** (-jnp.arange(half, dtype=f32) / half)
    ang = jnp.arange(S, dtype=f32)[:, None] * inv_freq[None, :]
    cos, sin = jnp.cos(ang), jnp.sin(ang)
    return jnp.tile(jnp.concatenate([cos, cos], axis=1), (1, 2)), jnp.tile(jnp.concatenate([-sin, sin], axis=1), (1, 2))


def _swap_halves(x, first_half):
    return jnp.where(first_half, pltpu.roll(x, LANES - ATT_HEAD_DIM // 2, 1), pltpu.roll(x, ATT_HEAD_DIM // 2, 1))


def _rope(x, cos, sin, first_half):
    return x * cos + _swap_halves(x, first_half) * sin


def _rope_bwd(dy, cos, sin, first_half):
    return dy * cos + _swap_halves(dy * sin, first_half)


def _attn_consts(n):
    lane = lax.broadcasted_iota(jnp.int32, (1, LANES), 1)
    low = lane < ATT_HEAD_DIM
    first_half = (lane % ATT_HEAD_DIM) < ATT_HEAD_DIM // 2
    top = lax.broadcasted_iota(jnp.int32, (LANES, 1), 0) < ATT_HEAD_DIM
    s = lax.broadcasted_iota(jnp.int32, (2 * ATT_BLOCK, ATT_BLOCK), 0)
    t = lax.broadcasted_iota(jnp.int32, (2 * ATT_BLOCK, ATT_BLOCK), 1)
    mask = (s > t) & (s <= t + ATT_BLOCK) & ((s >= ATT_BLOCK) | (n > 0))
    return low, first_half, top, mask


def _dup_kv(x, low):
    rolled = pltpu.roll(x, ATT_HEAD_DIM, 1)
    return [jnp.where(low, x, rolled), jnp.where(low, rolled, x)]


def _attn_head(qtm, kd, vdt, sink, mask):
    s = jnp.where(mask, _dot(kd, qtm) * ATT_SCALE, NEG_INF)
    m = jnp.maximum(jnp.max(s, axis=0, keepdims=True), sink)
    p = jnp.exp(s - m)
    psink = jnp.exp(sink - m)
    inv = 1.0 / (jnp.sum(p, axis=0, keepdims=True) + psink)
    pn = p * inv
    return pn, psink * inv, _dot(vdt, pn.astype(bf16))


def _swa_fwd(proj, sink_b, cos, sin, *, B, S, ride=None):
    T = B * S
    L = ATT_BLOCK
    nB = S // L

    def body(q_ref, z_ref, kvc_ref, kvp_ref, sk_ref, cc_ref, sc_ref, cp_ref, sp_ref, cat_ref):
        n = pl.program_id(1)
        low, first_half, top, mask = _attn_consts(n)
        cc, sc = cc_ref[...], sc_ref[...]
        kc = _rope(kvc_ref[:, 0:LANES], cc, sc, first_half)
        kp = _rope(kvp_ref[:, 0:LANES], cp_ref[...], sp_ref[...], first_half)
        kd = [x.astype(bf16) for x in _dup_kv(jnp.concatenate([kp, kc], axis=0), low)]
        vdt = [x.T.astype(bf16) for x in _dup_kv(jnp.concatenate([kvp_ref[:, LANES:2 * LANES], kvc_ref[:, LANES:2 * LANES]], axis=0), low)]
        for pair in range(ATT_HEADS // 2):
            cols = slice(pair * LANES, (pair + 1) * LANES)
            j = (2 * pair) // ATT_GROUP
            qt = _rope(q_ref[:, cols], cc, sc, first_half).T
            outs = []
            for hh in range(2):
                h = 2 * pair + hh
                qtm = jnp.where(top if hh == 0 else ~top, qt, 0.0).astype(bf16)
                outs.append(_attn_head(qtm, kd[j], vdt[j], sk_ref[h:h + 1, 0:1], mask)[2])
            zp = z_ref[:, cols]
            cat_ref[:, cols] = (jnp.where(top, outs[0], outs[1]).T * (zp * _sigmoid(zp))).astype(bf16)

    cur = lambda b, n: (b * nB + n, 0)
    return _call(
        body, (proj, proj, proj, proj, sink_b, cos, sin, cos, sin), name="swa_fwd", grid=(B, nB),
        in_specs=[pl.BlockSpec((L, ATT_WIDTH), lambda b, n: (b * nB + n, QA_BLK)),
                  pl.BlockSpec((L, ATT_WIDTH), lambda b, n: (b * nB + n, ZA_BLK)),
                  pl.BlockSpec((L, 2 * KV_WIDTH), lambda b, n: (b * nB + n, KV_BLK)),
                  pl.BlockSpec((L, 2 * KV_WIDTH), lambda b, n: (b * nB + jnp.maximum(n - 1, 0), KV_BLK)),
                  pl.BlockSpec((ATT_HEADS, LANES), lambda b, n: (0, 0)),
                  pl.BlockSpec((L, LANES), lambda b, n: (n, 0)), pl.BlockSpec((L, LANES), lambda b, n: (n, 0)),
                  pl.BlockSpec((L, LANES), lambda b, n: (jnp.maximum(n - 1, 0), 0)),
                  pl.BlockSpec((L, LANES), lambda b, n: (jnp.maximum(n - 1, 0), 0))],
        out_specs=[pl.BlockSpec((L, ATT_WIDTH), cur)],
        out_shape=[jax.ShapeDtypeStruct((T, ATT_WIDTH), bf16)],
        semantics=("parallel", "parallel"), ride=ride)


def _swa_bwd(proj, sink_b, cos, sin, dcat, *, B, S, ride=None):
    T = B * S
    L = ATT_BLOCK
    nB = S // L

    def body(q_ref, z_ref, kvc_ref, kvp_ref, sk_ref, cc_ref, sc_ref, cp_ref, sp_ref, dc_ref,
             dq_ref, dz_ref, dkv_ref, dsk_ref, carry, ds_st, pn_st, q_st, do_st):
        step = pl.program_id(1)
        n = nB - 1 - step

        @pl.when((pl.program_id(0) == 0) & (step == 0))
        def _():
            dsk_ref[...] = jnp.zeros_like(dsk_ref)

        @pl.when(step == 0)
        def _():
            carry[...] = jnp.zeros_like(carry)
        low, first_half, top, mask = _attn_consts(n)
        cc, sc, cp, sp = cc_ref[...], sc_ref[...], cp_ref[...], sp_ref[...]
        kc = _rope(kvc_ref[:, 0:LANES], cc, sc, first_half)
        kp = _rope(kvp_ref[:, 0:LANES], cp, sp, first_half)
        kdf = _dup_kv(jnp.concatenate([kp, kc], axis=0), low)
        vdf = _dup_kv(jnp.concatenate([kvp_ref[:, LANES:2 * LANES], kvc_ref[:, LANES:2 * LANES]], axis=0), low)
        kd = [x.astype(bf16) for x in kdf]
        vd = [x.astype(bf16) for x in vdf]
        kdt = [x.T.astype(bf16) for x in kdf]
        vdt = [x.T.astype(bf16) for x in vdf]
        dkd, dvd = [], []
        for pair in range(ATT_HEADS // 2):
            cols = slice(pair * LANES, (pair + 1) * LANES)
            j = (2 * pair) // ATT_GROUP
            qp = _rope(q_ref[:, cols], cc, sc, first_half)
            qt = qp.T
            zp = z_ref[:, cols]
            dc = dc_ref[:, cols]
            sz = _sigmoid(zp)
            d_o = dc * (zp * sz)
            dot_ = d_o.T
            res = []
            for hh in range(2):
                rsel = top if hh == 0 else ~top
                qtm = jnp.where(rsel, qt, 0.0).astype(bf16)
                pn, psn, o = _attn_head(qtm, kd[j], vdt[j], sk_ref[2 * pair + hh:2 * pair + hh + 1, 0:1], mask)
                res.append((rsel, pn, psn, o))
            ot = jnp.where(top, res[0][3], res[1][3])
            dz_ref[:, cols] = (dc * ot.T * (sz * (1.0 + zp * (1.0 - sz)))).astype(bf16)
            dqts = []
            for hh in range(2):
                h = 2 * pair + hh
                rsel, pn, psn, _ = res[hh]
                lsel = low if hh == 0 else ~low
                dotm = jnp.where(rsel, dot_, 0.0)
                delta = jnp.sum(dotm * ot, axis=0, keepdims=True)
                dst = (pn * (_dot(vd[j], dotm.astype(bf16)) - delta) * ATT_SCALE).astype(bf16)
                dsk_ref[h:h + 1, :] += jnp.zeros((1, LANES), f32) - jnp.sum(psn * delta)
                dqts.append(_dot(kdt[j], dst))
                g = h % ATT_GROUP
                ds_st[:, g * LANES:(g + 1) * LANES] = dst
                pn_st[:, g * LANES:(g + 1) * LANES] = pn.astype(bf16)
                q_st[g * LANES:(g + 1) * LANES, :] = jnp.where(lsel, qp, 0.0).astype(bf16)
                do_st[g * LANES:(g + 1) * LANES, :] = jnp.where(lsel, d_o, 0.0).astype(bf16)
            dq_ref[:, cols] = _rope_bwd(jnp.where(top, dqts[0], dqts[1]).T, cc, sc, first_half).astype(bf16)
            if (2 * pair + 2) % ATT_GROUP == 0:
                dkd.append(_dot(ds_st[...], q_st[...]))
                dvd.append(_dot(pn_st[...], do_st[...]))
        dk = [x + pltpu.roll(x, ATT_HEAD_DIM, 1) for x in dkd]
        dv = [x + pltpu.roll(x, ATT_HEAD_DIM, 1) for x in dvd]
        dk = jnp.where(low, dk[0], dk[1])
        dv = jnp.where(low, dv[0], dv[1])
        dkv_ref[:, 0:LANES] = (_rope_bwd(dk[L:2 * L], cc, sc, first_half) + carry[:, 0:LANES]).astype(bf16)
        dkv_ref[:, LANES:2 * LANES] = (dv[L:2 * L] + carry[:, LANES:2 * LANES]).astype(bf16)
        carry[:, 0:LANES] = _rope_bwd(dk[0:L], cp, sp, first_half)
        carry[:, LANES:2 * LANES] = dv[0:L]

    rev = lambda b, s: b * nB + nB - 1 - s
    revp = lambda b, s: b * nB + jnp.maximum(nB - 2 - s, 0)
    wide = lambda blk: pl.BlockSpec((L, ATT_WIDTH), lambda b, s: (rev(b, s), blk))
    tab = pl.BlockSpec((L, LANES), lambda b, s: (nB - 1 - s, 0))
    tabp = pl.BlockSpec((L, LANES), lambda b, s: (jnp.maximum(nB - 2 - s, 0), 0))
    return _call(
        body, (proj, proj, proj, proj, sink_b, cos, sin, cos, sin, dcat), name="swa_bwd", grid=(B, nB),
        in_specs=[wide(QA_BLK), wide(ZA_BLK),
                  pl.BlockSpec((L, 2 * KV_WIDTH), lambda b, s: (rev(b, s), KV_BLK)),
                  pl.BlockSpec((L, 2 * KV_WIDTH), lambda b, s: (revp(b, s), KV_BLK)),
                  pl.BlockSpec((ATT_HEADS, LANES), lambda b, s: (0, 0)),
                  tab, tab, tabp, tabp, wide(0)],
        out_specs=[wide(0), wide(0), pl.BlockSpec((L, 2 * KV_WIDTH), lambda b, s: (rev(b, s), 0)),
                   pl.BlockSpec((ATT_HEADS, LANES), lambda b, s: (0, 0))],
        out_shape=[jax.ShapeDtypeStruct((T, ATT_WIDTH), bf16), jax.ShapeDtypeStruct((T, ATT_WIDTH), bf16),
                   jax.ShapeDtypeStruct((T, 2 * KV_WIDTH), bf16), jax.ShapeDtypeStruct((ATT_HEADS, LANES), f32)],
        scratch_shapes=[pltpu.VMEM((L, 2 * KV_WIDTH), f32),
                        pltpu.VMEM((2 * L, ATT_GROUP * LANES), bf16), pltpu.VMEM((2 * L, ATT_GROUP * LANES), bf16),
                        pltpu.VMEM((ATT_GROUP * LANES, LANES), bf16), pltpu.VMEM((ATT_GROUP * LANES, LANES), bf16)],
        semantics=("arbitrary", "arbitrary"), ride=ride)


def _train_step(x, target, bufs, g_pre, g_post, lb_param, g_head, sinks, *, B, S, exchange):
    L = DEPTH
    T = x.shape[0]
    ri, ro = IN_WIDTH // 8, MIX_WIDTH // 8
    cos, sin = _rope_tables(S)
    full = list(bufs)
    if exchange:
        full[0] = _run_exchange(_gather_d2d(_run_exchange(_gather_ici(bufs[0]))))
    saved = []
    for l in range(L):
        wt = full[l][0].reshape(1, IN_WIDTH, D_MODEL)
        wo = full[l][1].reshape(1, MIX_WIDTH, D_MODEL)
        tail = jnp.concatenate([wt[:, 5376:6400], wt[:, 5120:5376]], axis=1)
        proj, h = _in_proj(x, g_pre[l:l + 1], wt, tail, 0)
        ahead = exchange and l + 1 < L
        (ch, o_pre, states), landed = _hgrn_fwd(proj, lb_param, g_head[l:l + 1], B=B, S=S, layer=l,
                                                ride=_gather_ici(bufs[l + 1]) if ahead else None)
        sink_b = jnp.broadcast_to(sinks[l][:, None], (ATT_HEADS, LANES))
        (ca,), passed = _swa_fwd(proj, sink_b, cos, sin, B=B, S=S, ride=_gather_d2d(landed) if ahead else None)
        if ahead:
            full[l + 1] = passed
        xn, y = _out_proj(ch, ca, wo, 0, x, g_post[l:l + 1])
        saved.append((x, proj, h, ch, o_pre, states, sink_b, ca, y, wt, tail, wo))
        x = xn
    dx, loss = _loss_head(x, target)

    def reduce_tail(sums, recv):
        return _run_exchange(_pair_share([_chip_sum(s, r) for s, r in zip(sums, recv)]))

    grads = [None] * L
    waiting = None
    gg_pre, gg_post, g_lb, gg_head, g_sinks = [], [], [], [], []
    for l in reversed(range(L)):
        x_in, proj, h, ch, o_pre, states, sink_b, ca, y, wt, tail, wo = saved[l]
        (dch, dca, dwo, dgpost), got = _out_proj_bwd(dx, y, g_post[l:l + 1], wo, 0, ch, ca,
                                                     ride=_pair_exchange(waiting) if waiting else None)
        sums = [_pair_add(p, r) for p, r in zip(waiting, got)] if waiting else None
        (dq, df, di, dz, dlb, dgh), recv = _hgrn_bwd(proj, lb_param, g_head[l:l + 1], o_pre, states, dch, B=B, S=S,
                                                     layer=l, ride=_chip_exchange(sums) if waiting else None)
        if waiting:
            grads[l + 1] = reduce_tail(sums, recv)
        at_end = exchange and l == 0
        part_o = [dwo.reshape(1, 4, 2, ro, D_MODEL)]
        (dqa, dza, dkv, dsk), got_o = _swa_bwd(proj, sink_b, cos, sin, dca, B=B, S=S,
                                              ride=_pair_exchange(part_o) if at_end else None)
        pieces = [dq, df, di, dz, dqa, dkv, dza]
        sums_o = [_pair_add(part_o[0], got_o[0])] if at_end else None
        (gwt,), recv_o = _grad_w_in(h, pieces, ride=_chip_exchange(sums_o) if at_end else None)
        part_t = [gwt.reshape(1, 4, 2, ri, D_MODEL)]
        if at_end:
            tm = min(512, T // 2)
            nb = T // tm
            na = max(1, nb // 4)
            (dx_a, dg_a), got_t = _in_proj_bwd(pieces, wt, tail, 0, x_in, g_pre[l:l + 1], dx, tm=tm, blocks=(0, na),
                                               ride=_pair_exchange(part_t))
            sums_t = [_pair_add(part_t[0], got_t[0])]
            (dx, dg_b), recv_t = _in_proj_bwd(pieces, wt, tail, 0, x_in, g_pre[l:l + 1], dx, tm=tm, blocks=(na, nb - na),
                                              dx_into=dx_a, ride=_chip_exchange(sums_t))
            dgpre = dg_a + dg_b
            grads[0] = reduce_tail(sums_t + sums_o, recv_t + recv_o)
        else:
            (dx, dgpre), _ = _in_proj_bwd(pieces, wt, tail, 0, x_in, g_pre[l:l + 1], dx)
            if exchange:
                waiting = part_t + part_o
            else:
                grads[l] = [gwt, dwo]
        gg_pre.append(dgpre[0])
        gg_post.append(dgpost[0])
        g_lb.append(jnp.sum(dlb, axis=(0, 1)))
        gg_head.append(jnp.sum(dgh, axis=(0, 1, 2)))
        g_sinks.append(dsk[:, 0])
    rev = lambda xs: jnp.stack(xs[::-1])
    return loss[0, 0], dx, grads, rev(gg_pre), rev(gg_post), rev(g_lb), rev(gg_head), rev(g_sinks)


MESH = pl.DeviceIdType.MESH
ANY = pl.BlockSpec(memory_space=pl.ANY)


def _place():
    x, y, c = lax.axis_index("x"), lax.axis_index("y"), lax.axis_index("c")
    return x, y, c, [(1 - x, y), (x, 1 - y), (1 - x, 1 - y)]


def _rcopy(src, dst, send, recv, k, to):
    return pltpu.make_async_remote_copy(src_ref=src, dst_ref=dst, send_sem=send.at[k], recv_sem=recv.at[k],
                                        device_id=to, device_id_type=MESH)


class _Exchange:
    def __init__(self, name, inputs, out_shapes, n_sems, plan, in_place=False):
        self.name, self.inputs, self.out_shapes, self.n_sems, self.plan = name, inputs, out_shapes, n_sems, plan
        self.aliases = {a: a for a in range(len(inputs))} if in_place else {}

    def start(self, ins, outs, send, recv):
        for cp in self.plan(ins, outs, send, recv)[0]:
            cp.start()

    def finish(self, ins, outs, send, recv):
        sent, arriving = self.plan(ins, outs, send, recv)
        for cp in arriving:
            cp.wait_recv()
        for cp in sent:
            cp.wait_send()

    def sems(self):
        return [pltpu.SemaphoreType.DMA((self.n_sems,)), pltpu.SemaphoreType.DMA((self.n_sems,))]


def _run_exchange(ex):
    n_in, n_out = len(ex.inputs), len(ex.out_shapes)

    def body(*refs):
        ins, outs = refs[:n_in], refs[n_in:n_in + n_out]
        send, recv = refs[n_in + n_out:]
        ex.start(ins, outs, send, recv)
        ex.finish(ins, outs, send, recv)

    return pl.pallas_call(
        body, name=ex.name, in_specs=[ANY] * n_in, out_specs=[ANY] * n_out, out_shape=ex.out_shapes,
        input_output_aliases=ex.aliases, scratch_shapes=ex.sems(),
    )(*ex.inputs)


def _call(body, operands, *, name, grid, in_specs, out_specs, out_shape, scratch_shapes=(), semantics, ride=None,
          aliases=None):
    aliases = dict(aliases or {})
    if ride is None:
        outs = pl.pallas_call(body, name=name, grid=grid, in_specs=in_specs, out_specs=out_specs, out_shape=out_shape,
                              input_output_aliases=aliases, scratch_shapes=list(scratch_shapes),
                              compiler_params=_params(*semantics))(*operands)
        return outs, []
    n_in, n_out, n_scr = len(in_specs), len(out_specs), len(scratch_shapes)
    r_in, r_out = len(ride.inputs), len(ride.out_shapes)

    def riding(*refs):
        refs = list(refs)
        ins, rins = refs[:n_in], refs[n_in:n_in + r_in]
        o0 = n_in + r_in
        outs, routs = refs[o0:o0 + n_out], refs[o0 + n_out:o0 + n_out + r_out]
        scr = refs[o0 + n_out + r_out:o0 + n_out + r_out + n_scr]
        send, recv = refs[-2:]
        ids = [pl.program_id(d) for d in range(len(grid))]
        first = functools.reduce(jnp.logical_and, [i == 0 for i in ids])
        last = functools.reduce(jnp.logical_and, [i == g - 1 for i, g in zip(ids, grid)])
        pl.when(first)(lambda: ride.start(rins, routs, send, recv))
        body(*ins, *outs, *scr)
        pl.when(last)(lambda: ride.finish(rins, routs, send, recv))

    res = pl.pallas_call(
        riding, name=name + "_" + ride.name, grid=grid,
        in_specs=list(in_specs) + [ANY] * r_in, out_specs=list(out_specs) + [ANY] * r_out,
        out_shape=list(out_shape) + list(ride.out_shapes),
        input_output_aliases={**aliases, **{n_in + a: n_out + b for a, b in ride.aliases.items()}},
        scratch_shapes=list(scratch_shapes) + ride.sems(),
        compiler_params=_params(*(["arbitrary"] * len(grid))),
    )(*operands, *ride.inputs)
    return res[:n_out], res[n_out:]


def _gather_ici(bufs, name="gather_ici"):
    n = len(bufs)

    def plan(ins, outs, send, recv):
        x, y, c, chips = _place()
        me = 2 * x + y
        sent, arriving = [], []
        for j, (px, py) in enumerate(chips):
            for a in range(n):
                mine, theirs = outs[a].at[:, me, c], outs[a].at[:, 2 * px + py, c]
                sent.append(_rcopy(mine, mine, send, recv, j * n + a, (px, py, c)))
                arriving.append(_rcopy(theirs, theirs, send, recv, j * n + a, (px, py, c)))
        return sent, arriving

    return _Exchange(name, bufs, [jax.ShapeDtypeStruct(b.shape, b.dtype) for b in bufs], 3 * n, plan, in_place=True)


def _gather_d2d(bufs, name="gather_d2d"):
    n = len(bufs)

    def plan(ins, outs, send, recv):
        x, y, c, chips = _place()
        sib = (x, y, 1 - c)
        sent, arriving = [], []
        for j, (px, py) in enumerate(chips):
            for a in range(n):
                got, theirs = outs[a].at[:, 2 * px + py, c], outs[a].at[:, 2 * px + py, 1 - c]
                sent.append(_rcopy(got, got, send, recv, j * n + a, sib))
                arriving.append(_rcopy(theirs, theirs, send, recv, j * n + a, sib))
        return sent, arriving

    return _Exchange(name, bufs, [jax.ShapeDtypeStruct(b.shape, b.dtype) for b in bufs], 3 * n, plan, in_place=True)


def _pair_exchange(parts):
    n = len(parts)

    def plan(ins, outs, send, recv):
        x, y, c, _ = _place()
        cps = [_rcopy(ins[a].at[:, :, 1 - c], outs[a], send, recv, a, (x, y, 1 - c)) for a in range(n)]
        return cps, cps

    return _Exchange("pair_exchange", parts,
                     [jax.ShapeDtypeStruct(p.shape[:2] + p.shape[3:], p.dtype) for p in parts], n, plan)


def _block_rows(r):
    return r if r <= 512 else r // 2


def _pair_add(part, got):
    L, K, _, r, C = part.shape
    rows = _block_rows(r)

    def body(c_ref, a_ref, b_ref, o_ref):
        o_ref[0, 0] = (a_ref[0, 0, 0] + b_ref[0, 0]).astype(bf16)

    blk = (1, 1, rows, C)
    return pl.pallas_call(
        body, name="pair_add",
        grid_spec=pltpu.PrefetchScalarGridSpec(
            num_scalar_prefetch=1, grid=(L, K, r // rows),
            in_specs=[pl.BlockSpec((1, 1, 1, rows, C), lambda l, k, i, c: (l, k, c[0], i, 0)),
                      pl.BlockSpec(blk, lambda l, k, i, c: (l, k, i, 0))],
            out_specs=pl.BlockSpec(blk, lambda l, k, i, c: (l, k, i, 0))),
        out_shape=jax.ShapeDtypeStruct((L, K, r, C), bf16),
        compiler_params=_params("parallel", "parallel", "parallel"),
    )(jnp.reshape(lax.axis_index("c"), (1,)).astype(jnp.int32), part, got)


def _chip_exchange(sums):
    n = len(sums)

    def plan(ins, outs, send, recv):
        x, y, c, chips = _place()
        cps = []
        for j, (px, py) in enumerate(chips):
            for a in range(n):
                cps.append(_rcopy(ins[a].at[:, 2 * px + py], outs[a].at[j], send, recv, j * n + a, (px, py, c)))
        return cps, cps

    return _Exchange("chip_exchange", sums,
                     [jax.ShapeDtypeStruct((3, s.shape[0]) + s.shape[2:], s.dtype) for s in sums], 3 * n, plan)


def _chip_sum(mine, got):
    L, K, r, C = mine.shape
    rows = _block_rows(r)

    def body(p_ref, a_ref, b_ref, o_ref):
        o_ref[0, 0] = (a_ref[0, 0].astype(f32) + b_ref[0, 0].astype(f32)) + (b_ref[1, 0].astype(f32) + b_ref[2, 0].astype(f32))

    place = jnp.stack([2 * lax.axis_index("x") + lax.axis_index("y"), lax.axis_index("c")]).astype(jnp.int32)
    return pl.pallas_call(
        body, name="chip_sum",
        grid_spec=pltpu.PrefetchScalarGridSpec(
            num_scalar_prefetch=1, grid=(L, r // rows),
            in_specs=[pl.BlockSpec((1, 1, rows, C), lambda l, i, p: (l, p[0], i, 0)),
                      pl.BlockSpec((3, 1, rows, C), lambda l, i, p: (0, l, i, 0))],
            out_specs=pl.BlockSpec((1, 1, rows, C), lambda l, i, p: (l, p[1], i, 0))),
        out_shape=jax.ShapeDtypeStruct((L, 2, r, C), f32),
        compiler_params=_params("parallel", "parallel"),
    )(place, mine, got)


def _pair_share(bufs):
    n = len(bufs)

    def plan(ins, outs, send, recv):
        x, y, c, _ = _place()
        sib = (x, y, 1 - c)
        sent = [_rcopy(outs[a].at[:, c], outs[a].at[:, c], send, recv, a, sib) for a in range(n)]
        arriving = [_rcopy(outs[a].at[:, 1 - c], outs[a].at[:, 1 - c], send, recv, a, sib) for a in range(n)]
        return sent, arriving

    return _Exchange("pair_share", bufs, [jax.ShapeDtypeStruct(b.shape, b.dtype) for b in bufs], n, plan, in_place=True)


def _all_sum_small(v):
    def body(v_ref, o_ref, buf, send, recv):
        x, y, c, _ = _place()
        me = 4 * x + 2 * y + c
        buf[me] = v_ref[...]
        cps = []
        for m in range(1, 8):
            to = (x ^ (m >> 2), y ^ ((m >> 1) & 1), c ^ (m & 1))
            cps.append(_rcopy(v_ref, buf.at[me], send, recv, m - 1, to))
        for cp in cps:
            cp.start()
        for cp in cps:
            cp.wait()
        acc = buf[0]
        for d in range(1, 8):
            acc = acc + buf[d]
        o_ref[...] = acc

    vm = pl.BlockSpec(memory_space=pltpu.VMEM)
    return pl.pallas_call(
        body, name="all_sum_small", in_specs=[vm], out_specs=vm,
        out_shape=jax.ShapeDtypeStruct(v.shape, v.dtype),
        scratch_shapes=[pltpu.VMEM((8,) + v.shape, v.dtype), pltpu.SemaphoreType.DMA((7,)), pltpu.SemaphoreType.DMA((7,))],
    )(v)


def _adamw_math(w, g, m, v):
    m = ADAM_B1 * m + (1.0 - ADAM_B1) * g
    v = ADAM_B2 * v + (1.0 - ADAM_B2) * (g * g)
    m_hat = m / (1.0 - ADAM_B1 ** ADAM_STEP)
    v_hat = v / (1.0 - ADAM_B2 ** ADAM_STEP)
    return -ADAM_LR * (m_hat / (jnp.sqrt(v_hat) + ADAM_EPS) + ADAM_WD * w), m, v


def _adamw(w, g, m, v):
    L, R, C = w.shape
    rows = R // 4

    def body(w_ref, g_ref, m_ref, v_ref, d_ref, mo_ref, vo_ref):
        d_ref[...], mo_ref[...], vo_ref[...] = _adamw_math(w_ref[...], g_ref[...], m_ref[...], v_ref[...])

    blk = pl.BlockSpec((1, rows, C), lambda l, i: (l, i, 0))
    return pl.pallas_call(
        body, name="adamw", grid=(L, R // rows), in_specs=[blk] * 4, out_specs=[blk] * 3,
        out_shape=[jax.ShapeDtypeStruct(w.shape, f32)] * 3,
        compiler_params=_params("parallel", "parallel"),
    )(w, g, m, v)


def _chip_index():
    return jnp.reshape(2 * lax.axis_index("x") + lax.axis_index("y"), (1,)).astype(jnp.int32)


def _shard_placed(w, l):
    _, R, C = w.shape
    rows = R // 4

    def body(k_ref, w_ref, o_ref):
        o_ref[0, 0] = w_ref[0].astype(bf16)

    return pl.pallas_call(
        body, name="shard_placed",
        grid_spec=pltpu.PrefetchScalarGridSpec(
            num_scalar_prefetch=1, grid=(R // rows,),
            in_specs=[pl.BlockSpec((1, rows, C), lambda i, k: (l, i, 0))],
            out_specs=pl.BlockSpec((1, 1, rows, C), lambda i, k: (0, k[0], i, 0))),
        out_shape=jax.ShapeDtypeStruct((1, 4, R, C), bf16),
        compiler_params=_params("parallel"),
    )(_chip_index(), w)


SMALL_ROWS = 4 * DEPTH


def _pack_small(g_pre, g_post, lb, g_head, sinks, loss=None):
    rows = []
    for l in range(DEPTH):
        tail = [g_head[l], sinks[l]]
        if loss is not None and l == 0:
            tail.append(jnp.reshape(loss, (1,)))
        tail = jnp.concatenate(tail)
        rows += [g_pre[l], g_post[l], lb[l], jnp.pad(tail, (0, D_MODEL - tail.shape[0]))]
    return jnp.stack(rows)


def _unpack_small(p):
    g_pre = jnp.stack([p[4 * l] for l in range(DEPTH)])
    g_post = jnp.stack([p[4 * l + 1] for l in range(DEPTH)])
    lb = jnp.stack([p[4 * l + 2] for l in range(DEPTH)])
    g_head = jnp.stack([p[4 * l + 3, :HG_HEAD_DIM] for l in range(DEPTH)])
    sinks = jnp.stack([p[4 * l + 3, HG_HEAD_DIM:HG_HEAD_DIM + ATT_HEADS] for l in range(DEPTH)])
    return g_pre, g_post, lb, g_head, sinks


def _small_update(gsum, w, m, v):
    def body(g_ref, w_ref, m_ref, v_ref, go_ref, d_ref, mo_ref, vo_ref):
        g = g_ref[...]
        w = w_ref[...]
        lbp = [w[4 * l + 2:4 * l + 3] for l in range(DEPTH)]
        mx = functools.reduce(jnp.maximum, lbp)
        e = [jnp.exp(t - mx) for t in lbp]
        tot = functools.reduce(jnp.add, e)
        p = [t / tot for t in e]
        glb = [g[4 * l + 2:4 * l + 3] for l in range(DEPTH)]
        row = lax.broadcasted_iota(jnp.int32, g.shape, 0)
        for j in range(DEPTH):
            gj = jnp.zeros_like(p[0])
            for l in range(DEPTH):
                for i in range(1, l + 1):
                    gj = gj + glb[l] * p[i] * ((1.0 if i == j else 0.0) - p[j])
            g = jnp.where(row == 4 * j + 2, gj, g)
        go_ref[...] = g
        d_ref[...], mo_ref[...], vo_ref[...] = _adamw_math(w, g, m_ref[...], v_ref[...])

    vm = pl.BlockSpec(memory_space=pltpu.VMEM)
    return pl.pallas_call(
        body, name="small_update", in_specs=[vm] * 4, out_specs=[vm] * 4,
        out_shape=[jax.ShapeDtypeStruct(gsum.shape, f32)] * 4,
    )(gsum, w, m, v)


def kernel(x, w_in, w_out, g_pre, g_post, lb_param, g_head, sinks, loss_target, m_w_in, m_w_out, m_g_pre, m_g_post, m_lb_param, m_g_head, m_sinks, v_w_in, v_w_out, v_g_pre, v_g_post, v_lb_param, v_g_head, v_sinks):
    B, S, _ = x.shape
    T = B * S
    L = DEPTH
    ri, ro = IN_WIDTH // 8, MIX_WIDTH // 8
    tr = lambda a: jnp.transpose(a, (0, 2, 1))
    wt, mt, vt = tr(w_in), tr(m_w_in), tr(v_w_in)
    bufs = [[_shard_placed(wt, l).reshape(1, 4, 2, ri, D_MODEL), _shard_placed(w_out, l).reshape(1, 4, 2, ro, D_MODEL)]
            for l in range(L)]
    loss, dx, grads, ggpre, ggpost, glb, gghead, gsinks = _train_step(
        x.reshape(T, D_MODEL), loss_target.reshape(T, D_MODEL), bufs, g_pre, g_post, lb_param, g_head, sinks,
        B=B, S=S, exchange=True)
    gwt_mine = jnp.concatenate([g[0] for g in grads], axis=0).reshape(L, 2 * ri, D_MODEL)
    grad_w_out = jnp.concatenate([g[1] for g in grads], axis=0).reshape(L, 2 * ro, D_MODEL)

    d_wt, nm_wt, nv_wt = _adamw(wt, gwt_mine, mt, vt)
    grad_w_in, d_w_in, nm_w_in, nv_w_in = tr(gwt_mine), tr(d_wt), tr(nm_wt), tr(nv_wt)
    d_w_out, nm_w_out, nv_w_out = _adamw(w_out, grad_w_out, m_w_out, v_w_out)

    gsum = _all_sum_small(_pack_small(ggpre, ggpost, glb, gghead, gsinks, loss))
    gs, ds, ms, vs = _small_update(
        gsum, _pack_small(g_pre, g_post, lb_param, g_head, sinks),
        _pack_small(m_g_pre, m_g_post, m_lb_param, m_g_head, m_sinks),
        _pack_small(v_g_pre, v_g_post, v_lb_param, v_g_head, v_sinks))
    loss_all = gsum[3, HG_HEAD_DIM + ATT_HEADS]
    return (loss_all, dx.reshape(B, S, D_MODEL), grad_w_in, grad_w_out, *_unpack_small(gs),
            d_w_in, d_w_out, *_unpack_small(ds), nm_w_in, nm_w_out, *_unpack_small(ms),
            nv_w_in, nv_w_out, *_unpack_small(vs))
```

```python
import functools
import math

import jax
import jax.numpy as jnp
from jax import lax
from jax.experimental import pallas as pl
from jax.experimental.pallas import tpu as pltpu

f32 = jnp.float32
bf16 = jnp.bfloat16

D_MODEL = 1024
DEPTH = 2
HG_WIDTH = 1024
HG_HEAD_DIM = 128
HG_HEADS = 8
CHUNK = 64
SUB = 16
ATT_WIDTH = 1024
ATT_HEAD_DIM = 64
ATT_HEADS = 16
ATT_GROUP = 8
KV_WIDTH = 128
ATT_BLOCK = 128
ATT_SCALE = 1.0 / math.sqrt(ATT_HEAD_DIM)
ROPE_THETA = 10000.0
IN_WIDTH = 6400
MIX_WIDTH = 2048
NORM_EPS = 1e-6
NEG_INF = -1e30
LB_FLOOR = 1e-20
LANES = 128
VMEM_LIMIT = 48 * 1024 * 1024

ADAM_LR = 0.001
ADAM_B1 = 0.9
ADAM_B2 = 0.999
ADAM_EPS = 1e-08
ADAM_WD = 0.01
ADAM_STEP = 10

QA_BLK, ZA_BLK, KV_BLK = 4, 5, 24

NT = (((1,), (1,)), ((), ()))
TN = (((0,), (0,)), ((), ()))


def _dot(a, b, dims=None, precision=None):
    if dims is None:
        return jnp.dot(a, b, preferred_element_type=f32, precision=precision)
    return lax.dot_general(a, b, dims, preferred_element_type=f32, precision=precision)


def _sigmoid(x):
    return 1.0 / (1.0 + jnp.exp(-x))


def _params(*sem):
    return pltpu.CompilerParams(dimension_semantics=sem, vmem_limit_bytes=VMEM_LIMIT)


TAIL = IN_WIDTH - 5120


def _in_proj(x, g, wt, tail, l, *, tm=1024):
    T = x.shape[0]
    tm = min(tm, T)
    nmain = 5120 // TAIL

    def body(x_ref, g_ref, w_ref, t_ref, p_ref, h_ref, hs):
        j = pl.program_id(1)

        @pl.when(j == 0)
        def _():
            xv = x_ref[...]
            r = lax.rsqrt(jnp.mean(xv * xv, axis=-1, keepdims=True) + NORM_EPS)
            hv = (xv * r * g_ref[...]).astype(bf16)
            hs[...] = hv
            h_ref[...] = hv

        @pl.when(j < nmain)
        def _():
            p_ref[...] = _dot(hs[...], w_ref[...], NT)

        @pl.when(j == nmain)
        def _():
            p_ref[...] = _dot(hs[...], t_ref[...], NT)

    return pl.pallas_call(
        body, name="in_proj", grid=(T // tm, nmain + 1),
        in_specs=[pl.BlockSpec((tm, D_MODEL), lambda i, j: (i, 0)),
                  pl.BlockSpec((1, D_MODEL), lambda i, j: (0, 0)),
                  pl.BlockSpec((None, TAIL, D_MODEL), lambda i, j: (l, jnp.minimum(j, nmain - 1), 0)),
                  pl.BlockSpec((None, TAIL, D_MODEL), lambda i, j: (l, 0, 0))],
        out_specs=[pl.BlockSpec((tm, TAIL), lambda i, j: (i, j)),
                   pl.BlockSpec((tm, D_MODEL), lambda i, j: (i, 0))],
        out_shape=[jax.ShapeDtypeStruct((T, IN_WIDTH), f32), jax.ShapeDtypeStruct((T, D_MODEL), bf16)],
        scratch_shapes=[pltpu.VMEM((tm, D_MODEL), bf16)],
        compiler_params=_params("parallel", "arbitrary"),
    )(x, g, wt, tail)


def _out_proj(ch, ca, wo, l, x, g, *, tm=512):
    T = x.shape[0]
    tm = min(tm, T)
    half = MIX_WIDTH // 2

    def body(ch_ref, ca_ref, wo_ref, x_ref, g_ref, xn_ref, y_ref):
        y = _dot(ch_ref[...], wo_ref[0:half, :]) + _dot(ca_ref[...], wo_ref[half:MIX_WIDTH, :])
        r = lax.rsqrt(jnp.mean(y * y, axis=-1, keepdims=True) + NORM_EPS)
        y_ref[...] = y
        xn_ref[...] = x_ref[...] + y * r * g_ref[...]

    row = lambda i: (i, 0)
    fixed = lambda i: (0, 0)
    return pl.pallas_call(
        body, name="out_proj", grid=(T // tm,),
        in_specs=[pl.BlockSpec((tm, half), row), pl.BlockSpec((tm, half), row),
                  pl.BlockSpec((None, MIX_WIDTH, D_MODEL), lambda i: (l, 0, 0)), pl.BlockSpec((tm, D_MODEL), row),
                  pl.BlockSpec((1, D_MODEL), fixed)],
        out_specs=[pl.BlockSpec((tm, D_MODEL), row), pl.BlockSpec((tm, D_MODEL), row)],
        out_shape=[jax.ShapeDtypeStruct((T, D_MODEL), f32)] * 2,
        compiler_params=_params("parallel"),
    )(ch, ca, wo, x, g)


def _loss_head(y, target, *, tm=512):
    T = y.shape[0]
    tm = min(tm, T)

    def body(y_ref, t_ref, d_ref, l_ref):
        @pl.when(pl.program_id(0) == 0)
        def _():
            l_ref[...] = jnp.zeros_like(l_ref)
        err = y_ref[...] - t_ref[...]
        d_ref[...] = err * (1.0 / D_MODEL)
        l_ref[...] += jnp.sum(err * err) * (0.5 / D_MODEL)

    row = lambda i: (i, 0)
    return pl.pallas_call(
        body, name="loss_head", grid=(T // tm,),
        in_specs=[pl.BlockSpec((tm, D_MODEL), row), pl.BlockSpec((tm, D_MODEL), row)],
        out_specs=[pl.BlockSpec((tm, D_MODEL), row), pl.BlockSpec((8, LANES), lambda i: (0, 0))],
        out_shape=[jax.ShapeDtypeStruct((T, D_MODEL), f32), jax.ShapeDtypeStruct((8, LANES), f32)],
        compiler_params=_params("arbitrary"),
    )(y, target)


def _out_proj_bwd(dxn, y, g, wo, l, ch, ca, *, tm=256, ride=None):
    T = y.shape[0]
    tm = min(tm, T)
    half = MIX_WIDTH // 2

    def body(dx_ref, y_ref, g_ref, wo_ref, ch_ref, ca_ref, dch_ref, dca_ref, dwo_ref, dg_ref):
        @pl.when(pl.program_id(0) == 0)
        def _():
            dwo_ref[...] = jnp.zeros_like(dwo_ref)
            dg_ref[...] = jnp.zeros_like(dg_ref)
        y = y_ref[...]
        dx = dx_ref[...]
        r = lax.rsqrt(jnp.mean(y * y, axis=-1, keepdims=True) + NORM_EPS)
        gy = dx * g_ref[...]
        dy = r * gy - y * (r * r * r) * jnp.mean(gy * y, axis=-1, keepdims=True)
        dg_ref[...] += jnp.sum(dx * y * r, axis=0, keepdims=True)
        dyb = dy.astype(bf16)
        dch_ref[...] = _dot(dyb, wo_ref[0:half, :], NT)
        dca_ref[...] = _dot(dyb, wo_ref[half:MIX_WIDTH, :], NT)
        dwo_ref[0:half, :] += _dot(ch_ref[...], dyb, TN)
        dwo_ref[half:MIX_WIDTH, :] += _dot(ca_ref[...], dyb, TN)

    row = lambda i: (i, 0)
    fixed = lambda i: (0, 0)
    return _call(
        body, (dxn, y, g, wo, ch, ca), name="out_proj_bwd", grid=(T // tm,),
        in_specs=[pl.BlockSpec((tm, D_MODEL), row), pl.BlockSpec((tm, D_MODEL), row),
                  pl.BlockSpec((1, D_MODEL), fixed), pl.BlockSpec((None, MIX_WIDTH, D_MODEL), lambda i: (l, 0, 0)),
                  pl.BlockSpec((tm, half), row), pl.BlockSpec((tm, half), row)],
        out_specs=[pl.BlockSpec((tm, half), row), pl.BlockSpec((tm, half), row),
                   pl.BlockSpec((MIX_WIDTH, D_MODEL), fixed), pl.BlockSpec((1, D_MODEL), fixed)],
        out_shape=[jax.ShapeDtypeStruct((T, half), f32), jax.ShapeDtypeStruct((T, half), f32),
                   jax.ShapeDtypeStruct((MIX_WIDTH, D_MODEL), f32), jax.ShapeDtypeStruct((1, D_MODEL), f32)],
        semantics=("arbitrary",), ride=ride)


TILE = 256
PIECE_TILES = (4, 4, 4, 4, 4, 1, 4)
PIECE_START = tuple(sum(PIECE_TILES[:p]) for p in range(len(PIECE_TILES)))
N_TILES = sum(PIECE_TILES)


def _piece_specs(rows, index):
    def spec(s, n):
        def index_map(*g):
            r, t = index(*g)
            return r, jnp.clip(t - s, 0, n - 1)
        return pl.BlockSpec((rows, TILE), index_map)
    return [spec(s, n) for s, n in zip(PIECE_START, PIECE_TILES)]


def _for_piece(t, fn):
    for p, (s, n) in enumerate(zip(PIECE_START, PIECE_TILES)):
        pl.when((t >= s) & (t < s + n))(functools.partial(fn, p))


def _in_proj_bwd(pieces, wt, tail, l, x, g, dxn, *, tm=512, blocks=None, dx_into=None, ride=None):
    T = x.shape[0]
    tm = min(tm, T)
    first, count = blocks or (0, T // tm)
    npc = len(pieces)
    nk = npc
    nmain = npc - 2
    wide = ATT_WIDTH
    extra = [] if dx_into is None else [dx_into]

    def body(*refs):
        dp_refs = refs[:npc]
        w_ref, tz_ref, tkv_ref, x_ref, g_ref, dxn_ref = refs[npc:npc + 6]
        dx_ref, dg_ref, acc = refs[npc + 6 + len(extra):]
        i, k = pl.program_id(0), pl.program_id(1)

        @pl.when((i == 0) & (k == 0))
        def _():
            dg_ref[...] = jnp.zeros_like(dg_ref)

        @pl.when(k == 0)
        def _():
            acc[...] = jnp.zeros_like(acc)

        for p in range(npc):
            w_p = w_ref if p < nmain else (tkv_ref if p == nmain else tz_ref)

            def add(p=p, w_p=w_p):
                acc[...] += _dot(dp_refs[p][...], w_p[...])
            pl.when(k == p)(add)

        @pl.when(k == nk - 1)
        def _():
            dh = acc[...]
            xv = x_ref[...]
            r = lax.rsqrt(jnp.mean(xv * xv, axis=-1, keepdims=True) + NORM_EPS)
            gy = dh * g_ref[...]
            dx_ref[...] = dxn_ref[...] + r * gy - xv * (r * r * r) * jnp.mean(gy * xv, axis=-1, keepdims=True)
            dg_ref[...] += jnp.sum(dh * xv * r, axis=0, keepdims=True)

    rows = lambda i, k: (first + i, 0)
    return _call(
        body, (*pieces, wt, tail, tail, x, g, dxn, *extra), name="in_proj_bwd", grid=(count, nk),
        in_specs=[pl.BlockSpec((tm, p.shape[1]), rows) for p in pieces] + [
            pl.BlockSpec((None, wide, D_MODEL), lambda i, k: (l, jnp.minimum(k, nmain - 1), 0)),
            pl.BlockSpec((None, wide, D_MODEL), lambda i, k: (l, 0, 0)),
            pl.BlockSpec((None, TAIL - wide, D_MODEL), lambda i, k: (l, wide // (TAIL - wide), 0)),
            pl.BlockSpec((tm, D_MODEL), rows), pl.BlockSpec((1, D_MODEL), lambda i, k: (0, 0)),
            pl.BlockSpec((tm, D_MODEL), rows)] + [ANY] * len(extra),
        out_specs=[pl.BlockSpec((tm, D_MODEL), rows), pl.BlockSpec((1, D_MODEL), lambda i, k: (0, 0))],
        out_shape=[jax.ShapeDtypeStruct((T, D_MODEL), f32), jax.ShapeDtypeStruct((1, D_MODEL), f32)],
        scratch_shapes=[pltpu.VMEM((tm, D_MODEL), f32)],
        semantics=("arbitrary", "arbitrary"), ride=ride, aliases={npc + 6: 0} if extra else None)


def _grad_w_in(h, pieces, *, ride=None):
    T = h.shape[0]
    npc = len(pieces)

    def body(*refs):
        h_ref, dp_refs, o_ref = refs[0], refs[1:1 + npc], refs[1 + npc]

        def put(p):
            o_ref[...] = _dot(dp_refs[p][...], h_ref[...], TN)
        _for_piece(pl.program_id(0), put)

    return _call(
        body, (h, *pieces), name="grad_w_in", grid=(N_TILES,),
        in_specs=[pl.BlockSpec((T, D_MODEL), lambda j: (0, 0), pipeline_mode=pl.Buffered(1))]
        + _piece_specs(T, lambda j: (0, j)),
        out_specs=[pl.BlockSpec((TILE, D_MODEL), lambda j: (j, 0))],
        out_shape=[jax.ShapeDtypeStruct((IN_WIDTH, D_MODEL), f32)],
        semantics=("parallel",), ride=ride)


def _lower_bound(lbp, layer):
    m = jnp.max(lbp, axis=0, keepdims=True)
    e = jnp.exp(lbp - m)
    p = e / jnp.sum(e, axis=0, keepdims=True)
    acc = p[0:1]
    for i in range(1, layer + 1):
        acc = acc + p[i:i + 1]
    return acc - p[0:1]


def _gate_parts(qr, fr, lb, lbf):
    sq = _sigmoid(qr)
    e = jnp.exp(-jnp.abs(fr))
    inv = 1.0 / (1.0 + e)
    pos = fr >= 0
    sg = jnp.where(pos, inv, e * inv)
    nsg = jnp.where(pos, e * inv, inv)
    fg = lbf + (1.0 - lb) * sg
    return qr * sq, sq, sg, nsg, fg, jnp.log(fg), (1.0 - lb) * nsg


LEVELS = tuple(SUB << j for j in range((CHUNK // SUB).bit_length() - 1))


def _level_masks(transposed=False):
    t = lax.broadcasted_iota(jnp.int32, (CHUNK, CHUNK), 1 if transposed else 0)
    s = lax.broadcasted_iota(jnp.int32, (CHUNK, CHUNK), 0 if transposed else 1)
    return [(t % (2 * m) >= m) & (s % (2 * m) < m) & (t // (2 * m) == s // (2 * m)) for m in LEVELS]


def _level_anchor(b_s, row, m):
    beta = b_s[m - 1:m, :]
    for g in range(1, CHUNK // (2 * m)):
        beta = jnp.where(row >= g * 2 * m, b_s[g * 2 * m + m - 1:g * 2 * m + m, :], beta)
    return beta


INTERLEAVE = 8


def _interleaved(chunks):
    for g0 in range(0, len(chunks), INTERLEAVE):
        live = chunks[g0:g0 + INTERLEAVE]
        while live:
            for gen in list(live):
                try:
                    next(gen)
                except StopIteration:
                    live.remove(gen)


def _seg_sum(seg, x):
    hi = x.astype(bf16)
    return _dot(seg, hi) + _dot(seg, (x - hi.astype(f32)).astype(bf16))


def _hgrn_fwd(proj, lb_param, g_head, *, B, S, layer, ride=None):
    T = B * S
    TB = min(512, S)
    nT, NC = S // TB, TB // CHUNK
    nC = S // CHUNK
    HD = HG_HEAD_DIM

    def body(q_ref, f_ref, i_ref, z_ref, lb_ref, gh_ref, cat_ref, op_ref, st_ref,
             s_scr, b_scr, k_scr):
        @pl.when(pl.program_id(2) == 0)
        def _():
            s_scr[...] = jnp.zeros_like(s_scr)
        lb = _lower_bound(lb_ref[...], layer)
        lbf = jnp.maximum(lb, LB_FLOOR)
        gh = gh_ref[...]
        r_i = lax.broadcasted_iota(jnp.int32, (CHUNK, CHUNK), 0)
        c_i = lax.broadcasted_iota(jnp.int32, (CHUNK, CHUNK), 1)
        tril = (r_i >= c_i).astype(bf16)
        rows8 = lax.broadcasted_iota(jnp.int32, (8, HD), 0)
        row_c = lax.broadcasted_iota(jnp.int32, (CHUNK, HD), 0)
        lane_c = lax.broadcasted_iota(jnp.int32, (8, CHUNK), 1)
        masks = _level_masks()

        def chunk(c, carried):
            rs = slice(c * CHUNK, (c + 1) * CHUNK)
            b_s, k_s = b_scr.at[c], k_scr.at[c]
            q, _, _, _, _, logf, k = _gate_parts(q_ref[rs, :], f_ref[rs, :], lb, lbf)
            v = i_ref[rs, :]
            b = _seg_sum(tril, logf)
            b_s[...] = b
            k_s[...] = k
            yield
            pieces = []
            for blk in range(CHUNK // SUB):
                r0 = blk * SUB
                bp = [b[r0 + 8 * i:r0 + 8 * i + 8] for i in range(SUB // 8)]
                qp = [q[r0 + 8 * i:r0 + 8 * i + 8] for i in range(SUB // 8)]
                ap = [jnp.zeros((8, CHUNK), f32) for _ in range(SUB // 8)]
                for s in range(SUB):
                    bs = b_s[r0 + s:r0 + s + 1, :]
                    ks = k_s[r0 + s:r0 + s + 1, :]
                    for i in range(s // 8, SUB // 8):
                        diff = bp[i] - bs
                        if i == s // 8:
                            diff = jnp.where(rows8 >= s - 8 * i, diff, NEG_INF)
                        col = jnp.sum(jnp.exp(diff) * qp[i] * ks, axis=1, keepdims=True)
                        ap[i] = jnp.where(lane_c == r0 + s, col, ap[i])
                pieces += ap
                yield
            a_all = jnp.concatenate(pieces, axis=0)
            for m, mk in zip(LEVELS, masks):
                beta = _level_anchor(b_s, row_c, m)
                qh = (q * jnp.exp(jnp.minimum(b - beta, 0.0))).astype(bf16)
                kh = (k * jnp.exp(jnp.minimum(beta - b, 0.0))).astype(bf16)
                a_all = a_all + jnp.where(mk, _dot(qh, kh, NT), 0.0)
            yield
            st = carried[0]
            st_ref[0, 0, c] = st
            vb16 = v.astype(bf16)
            o = _dot(a_all.astype(bf16), vb16) + _dot((q * jnp.exp(b)).astype(bf16), st.astype(bf16), NT)
            b_end = b_s[CHUNK - 1:CHUNK, :]
            kdec = (k * jnp.exp(b_end - b)).astype(bf16)
            carried[0] = jnp.exp(b_end) * st + _dot(vb16, kdec, TN)
            rr = lax.rsqrt(jnp.mean(o * o, axis=-1, keepdims=True) + NORM_EPS)
            zr = z_ref[rs, :]
            cat_ref[rs, :] = (o * rr * gh * (zr * _sigmoid(zr))).astype(bf16)
            op_ref[rs, :] = o

        carried = [s_scr[...]]
        _interleaved([chunk(c, carried) for c in range(NC)])
        s_scr[...] = carried[0]

    def col(part):
        return pl.BlockSpec((TB, HD), lambda b, h, n: (b * nT + n, part * HG_HEADS + h))

    out_col = pl.BlockSpec((TB, HD), lambda b, h, n: (b * nT + n, h))
    return _call(
        body, (proj, proj, proj, proj, lb_param, g_head),
        name=f"hgrn_fwd_l{layer}", grid=(B, HG_HEADS, nT),
        in_specs=[col(0), col(1), col(2), col(3),
                  pl.BlockSpec((DEPTH, HD), lambda b, h, n: (0, h)),
                  pl.BlockSpec((1, HD), lambda b, h, n: (0, 0))],
        out_specs=[out_col, out_col,
                   pl.BlockSpec((1, 1, NC, HD, HD), lambda b, h, n: (b, h, n, 0, 0))],
        out_shape=[jax.ShapeDtypeStruct((T, HG_WIDTH), bf16), jax.ShapeDtypeStruct((T, HG_WIDTH), f32),
                   jax.ShapeDtypeStruct((B, HG_HEADS, nC, HD, HD), f32)],
        scratch_shapes=[pltpu.VMEM((HD, HD), f32), pltpu.VMEM((NC, CHUNK, HD), f32), pltpu.VMEM((NC, CHUNK, HD), f32)],
        semantics=("parallel", "parallel", "arbitrary"), ride=ride)


def _hgrn_bwd(proj, lb_param, g_head, o_pre, states, dcat, *, B, S, layer, ride=None):
    T = B * S
    TB = min(512, S)
    nT, NC = S // TB, TB // CHUNK
    HD = HG_HEAD_DIM

    def body(q_ref, f_ref, i_ref, z_ref, lb_ref, gh_ref, op_ref, st_ref, dc_ref,
             dq_ref, df_ref, di_ref, dz_ref, dlb_ref, dgh_ref,
             ds_scr, b_scr, q_scr, do_scr, wk_scr):
        @pl.when(pl.program_id(2) == 0)
        def _():
            ds_scr[...] = jnp.zeros_like(ds_scr)
            dlb_ref[...] = jnp.zeros_like(dlb_ref)
            dgh_ref[...] = jnp.zeros_like(dgh_ref)
        lb = _lower_bound(lb_ref[...], layer)
        lbf = jnp.maximum(lb, LB_FLOOR)
        ind = (lb > LB_FLOOR).astype(f32)
        gh = gh_ref[...]
        r_i = lax.broadcasted_iota(jnp.int32, (CHUNK, CHUNK), 0)
        c_i = lax.broadcasted_iota(jnp.int32, (CHUNK, CHUNK), 1)
        tril = (r_i >= c_i).astype(bf16)
        triu = (c_i >= r_i).astype(bf16)
        rows8 = lax.broadcasted_iota(jnp.int32, (8, HD), 0)
        row_c = lax.broadcasted_iota(jnp.int32, (CHUNK, HD), 0)
        lane_c = lax.broadcasted_iota(jnp.int32, (8, CHUNK), 1)
        last_row = row_c == CHUNK - 1
        masks = _level_masks()
        masks_t = _level_masks(transposed=True)
        seg_t = lax.broadcasted_iota(jnp.int32, (SUB, 8 * SUB), 0)
        seg_r = lax.broadcasted_iota(jnp.int32, (SUB, 8 * SUB), 1) // 8
        seg0 = (seg_r == seg_t).astype(bf16)
        seg1 = (seg_r[:, 0:4 * SUB] + 8 == seg_t[:, 0:4 * SUB]).astype(bf16)

        def chunk(c, carried):
            rs = slice(c * CHUNK, (c + 1) * CHUNK)
            b_s, q_s, do_s = b_scr.at[c], q_scr.at[c], do_scr.at[c]
            qr, fr = q_ref[rs, :], f_ref[rs, :]
            q, sq, sg, nsg, fg, logf, k = _gate_parts(qr, fr, lb, lbf)
            v = i_ref[rs, :]
            b = _seg_sum(tril, logf)
            o = op_ref[rs, :]
            dc = dc_ref[rs, :]
            zr = z_ref[rs, :]
            sz = _sigmoid(zr)
            rr = lax.rsqrt(jnp.mean(o * o, axis=-1, keepdims=True) + NORM_EPS)
            dz_ref[rs, :] = (dc * (o * rr * gh) * (sz * (1.0 + zr * (1.0 - sz)))).astype(bf16)
            dn = dc * (zr * sz)
            dgh_ref[0, 0] += jnp.sum(dn * o * rr, axis=0, keepdims=True)
            gdn = dn * gh
            d_o = rr * gdn - o * (rr * rr * rr) * jnp.mean(gdn * o, axis=-1, keepdims=True)
            b_s[...] = b
            q_s[...] = q
            do_s[...] = d_o
            dob = d_o.astype(bf16)
            vb16 = v.astype(bf16)
            d_a = _dot(dob, vb16, NT)
            yield
            d_q = jnp.zeros((CHUNK, HD), f32)
            d_k = jnp.zeros((CHUNK, HD), f32)
            at_all = jnp.zeros((CHUNK, CHUNK), f32)
            for m, mk, mkt in zip(LEVELS, masks, masks_t):
                beta = _level_anchor(b_s, row_c, m)
                eq = jnp.exp(jnp.minimum(b - beta, 0.0))
                ek = jnp.exp(jnp.minimum(beta - b, 0.0))
                qh = (q * eq).astype(bf16)
                kh = (k * ek).astype(bf16)
                at_all = at_all + jnp.where(mkt, _dot(kh, qh, NT), 0.0)
                d_aa = jnp.where(mk, d_a, 0.0).astype(bf16)
                d_q = d_q + _dot(d_aa, kh) * eq
                d_k = d_k + _dot(d_aa, qh, TN) * ek
            yield
            dq_blocks, dk_pieces, at_pieces = [], [], []
            for blk in range(CHUNK // SUB):
                r0 = blk * SUB
                wk = wk_scr.at[c * (CHUNK // SUB) + blk]
                bp = [b[r0 + 8 * i:r0 + 8 * i + 8] for i in range(SUB // 8)]
                kp = [k[r0 + 8 * i:r0 + 8 * i + 8] for i in range(SUB // 8)]
                vp = [v[r0 + 8 * i:r0 + 8 * i + 8] for i in range(SUB // 8)]
                dkp = [jnp.zeros((8, HD), f32) for _ in range(SUB // 8)]
                atp = [jnp.zeros((8, CHUNK), f32) for _ in range(SUB // 8)]
                for t in range(SUB):
                    bt = b_s[r0 + t:r0 + t + 1, :]
                    qt = q_s[r0 + t:r0 + t + 1, :]
                    dot_ = do_s[r0 + t:r0 + t + 1, :]
                    for i in range(t // 8 + 1):
                        diff = bt - bp[i]
                        if i == t // 8:
                            diff = jnp.where(rows8 <= t - 8 * i, diff, NEG_INF)
                        e = jnp.exp(diff)
                        a = jnp.sum(e * kp[i] * qt, axis=1, keepdims=True)
                        atp[i] = jnp.where(lane_c == r0 + t, a, atp[i])
                        w = jnp.sum(vp[i] * dot_, axis=1, keepdims=True) * e
                        dkp[i] = dkp[i] + w * qt
                        row = 8 * t if i == 0 else 8 * SUB + 8 * (t - 8)
                        wk[row:row + 8, :] = w * kp[i]
                dq_blk = _seg_sum(seg0, wk[0:8 * SUB, :])
                if SUB > 8:
                    dq_blk = dq_blk + _seg_sum(seg1, wk[8 * SUB:12 * SUB, :])
                dq_blocks.append(dq_blk)
                dk_pieces += dkp
                at_pieces += atp
                yield
            dst1 = carried[0]
            st0 = st_ref[0, 0, c]
            dst1b = dst1.astype(bf16)
            eb = jnp.exp(b)
            b_end = b_s[CHUNK - 1:CHUNK, :]
            edec = jnp.exp(b_end - b)
            e_end = jnp.exp(b_end)
            kdec = (k * edec).astype(bf16)
            qdec = (q * eb).astype(bf16)
            st1 = e_end * st0 + _dot(vb16, kdec, TN)
            rterm = jnp.sum(dst1 * st1, axis=0, keepdims=True)
            carried[0] = e_end * dst1 + _dot(dob, qdec, TN)
            d_q = d_q + _dot(dob, st0.astype(bf16)) * eb + jnp.concatenate(dq_blocks, axis=0)
            d_k = d_k + _dot(vb16, dst1b) * edec + jnp.concatenate(dk_pieces, axis=0)
            d_v = _dot(kdec, dst1b, NT) + _dot((at_all + jnp.concatenate(at_pieces, axis=0)).astype(bf16), dob)
            db = q * d_q - k * d_k + jnp.where(last_row, rterm, 0.0)
            dlt = _seg_sum(triu, db) - fg * d_k
            df_ref[rs, :] = (dlt * (1.0 - lb) * sg * nsg / fg).astype(bf16)
            dlb_ref[0] += jnp.sum(dlt * (ind - sg) / fg, axis=0, keepdims=True)
            dq_ref[rs, :] = (d_q * (sq * (1.0 + qr * (1.0 - sq)))).astype(bf16)
            di_ref[rs, :] = d_v.astype(bf16)

        carried = [ds_scr[...]]
        _interleaved([chunk(c, carried) for c in reversed(range(NC))])
        ds_scr[...] = carried[0]

    def col(part):
        return pl.BlockSpec((TB, HD), lambda b, h, n: (b * nT + nT - 1 - n, part * HG_HEADS + h))

    hcol = pl.BlockSpec((TB, HD), lambda b, h, n: (b * nT + nT - 1 - n, h))
    return _call(
        body, (proj, proj, proj, proj, lb_param, g_head, o_pre, states, dcat),
        name=f"hgrn_bwd_l{layer}", grid=(B, HG_HEADS, nT),
        in_specs=[col(0), col(1), col(2), col(3),
                  pl.BlockSpec((DEPTH, HD), lambda b, h, n: (0, h)),
                  pl.BlockSpec((1, HD), lambda b, h, n: (0, 0)),
                  hcol,
                  pl.BlockSpec((1, 1, NC, HD, HD), lambda b, h, n: (b, h, nT - 1 - n, 0, 0)),
                  hcol],
        out_specs=[hcol, hcol, hcol, hcol,
                   pl.BlockSpec((1, 1, HD), lambda b, h, n: (b, 0, h)),
                   pl.BlockSpec((1, 1, 1, HD), lambda b, h, n: (b, h, 0, 0))],
        out_shape=[jax.ShapeDtypeStruct((T, HG_WIDTH), bf16)] * 4 + [
            jax.ShapeDtypeStruct((B, 1, HG_WIDTH), f32), jax.ShapeDtypeStruct((B, HG_HEADS, 1, HD), f32)],
        scratch_shapes=[pltpu.VMEM((HD, HD), f32)] + [pltpu.VMEM((NC, CHUNK, HD), f32)] * 3
        + [pltpu.VMEM((NC * CHUNK // SUB, 12 * SUB, HD), f32)],
        semantics=("parallel", "parallel", "arbitrary"), ride=ride)


def _rope_tables(S):
    half = ATT_HEAD_DIM // 2
    inv_freq = ROPE_THETA ** (-jnp.arange(half, dtype=f32) / half)
    ang = jnp.arange(S, dtype=f32)[:, None] * inv_freq[None, :]
    cos, sin = jnp.cos(ang), jnp.sin(ang)
    return jnp.tile(jnp.concatenate([cos, cos], axis=1), (1, 2)), jnp.tile(jnp.concatenate([-sin, sin], axis=1), (1, 2))


def _swap_halves(x, first_half):
    return jnp.where(first_half, pltpu.roll(x, LANES - ATT_HEAD_DIM // 2, 1), pltpu.roll(x, ATT_HEAD_DIM // 2, 1))


def _rope(x, cos, sin, first_half):
    return x * cos + _swap_halves(x, first_half) * sin


def _rope_bwd(dy, cos, sin, first_half):
    return dy * cos + _swap_halves(dy * sin, first_half)


def _attn_consts(n):
    lane = lax.broadcasted_iota(jnp.int32, (1, LANES), 1)
    low = lane < ATT_HEAD_DIM
    first_half = (lane % ATT_HEAD_DIM) < ATT_HEAD_DIM // 2
    top = lax.broadcasted_iota(jnp.int32, (LANES, 1), 0) < ATT_HEAD_DIM
    s = lax.broadcasted_iota(jnp.int32, (2 * ATT_BLOCK, ATT_BLOCK), 0)
    t = lax.broadcasted_iota(jnp.int32, (2 * ATT_BLOCK, ATT_BLOCK), 1)
    mask = (s > t) & (s <= t + ATT_BLOCK) & ((s >= ATT_BLOCK) | (n > 0))
    return low, first_half, top, mask


def _dup_kv(x, low):
    rolled = pltpu.roll(x, ATT_HEAD_DIM, 1)
    return [jnp.where(low, x, rolled), jnp.where(low, rolled, x)]


def _attn_head(qtm, kd, vdt, sink, mask):
    s = jnp.where(mask, _dot(kd, qtm) * ATT_SCALE, NEG_INF)
    m = jnp.maximum(jnp.max(s, axis=0, keepdims=True), sink)
    p = jnp.exp(s - m)
    psink = jnp.exp(sink - m)
    inv = 1.0 / (jnp.sum(p, axis=0, keepdims=True) + psink)
    pn = p * inv
    return pn, psink * inv, _dot(vdt, pn.astype(bf16))


def _swa_fwd(proj, sink_b, cos, sin, *, B, S, ride=None):
    T = B * S
    L = ATT_BLOCK
    nB = S // L

    def body(q_ref, z_ref, kvc_ref, kvp_ref, sk_ref, cc_ref, sc_ref, cp_ref, sp_ref, cat_ref):
        n = pl.program_id(1)
        low, first_half, top, mask = _attn_consts(n)
        cc, sc = cc_ref[...], sc_ref[...]
        kc = _rope(kvc_ref[:, 0:LANES], cc, sc, first_half)
        kp = _rope(kvp_ref[:, 0:LANES], cp_ref[...], sp_ref[...], first_half)
        kd = [x.astype(bf16) for x in _dup_kv(jnp.concatenate([kp, kc], axis=0), low)]
        vdt = [x.T.astype(bf16) for x in _dup_kv(jnp.concatenate([kvp_ref[:, LANES:2 * LANES], kvc_ref[:, LANES:2 * LANES]], axis=0), low)]
        for pair in range(ATT_HEADS // 2):
            cols = slice(pair * LANES, (pair + 1) * LANES)
            j = (2 * pair) // ATT_GROUP
            qt = _rope(q_ref[:, cols], cc, sc, first_half).T
            outs = []
            for hh in range(2):
                h = 2 * pair + hh
                qtm = jnp.where(top if hh == 0 else ~top, qt, 0.0).astype(bf16)
                outs.append(_attn_head(qtm, kd[j], vdt[j], sk_ref[h:h + 1, 0:1], mask)[2])
            zp = z_ref[:, cols]
            cat_ref[:, cols] = (jnp.where(top, outs[0], outs[1]).T * (zp * _sigmoid(zp))).astype(bf16)

    cur = lambda b, n: (b * nB + n, 0)
    return _call(
        body, (proj, proj, proj, proj, sink_b, cos, sin, cos, sin), name="swa_fwd", grid=(B, nB),
        in_specs=[pl.BlockSpec((L, ATT_WIDTH), lambda b, n: (b * nB + n, QA_BLK)),
                  pl.BlockSpec((L, ATT_WIDTH), lambda b, n: (b * nB + n, ZA_BLK)),
                  pl.BlockSpec((L, 2 * KV_WIDTH), lambda b, n: (b * nB + n, KV_BLK)),
                  pl.BlockSpec((L, 2 * KV_WIDTH), lambda b, n: (b * nB + jnp.maximum(n - 1, 0), KV_BLK)),
                  pl.BlockSpec((ATT_HEADS, LANES), lambda b, n: (0, 0)),
                  pl.BlockSpec((L, LANES), lambda b, n: (n, 0)), pl.BlockSpec((L, LANES), lambda b, n: (n, 0)),
                  pl.BlockSpec((L, LANES), lambda b, n: (jnp.maximum(n - 1, 0), 0)),
                  pl.BlockSpec((L, LANES), lambda b, n: (jnp.maximum(n - 1, 0), 0))],
        out_specs=[pl.BlockSpec((L, ATT_WIDTH), cur)],
        out_shape=[jax.ShapeDtypeStruct((T, ATT_WIDTH), bf16)],
        semantics=("parallel", "parallel"), ride=ride)


def _swa_bwd(proj, sink_b, cos, sin, dcat, *, B, S, ride=None):
    T = B * S
    L = ATT_BLOCK
    nB = S // L

    def body(q_ref, z_ref, kvc_ref, kvp_ref, sk_ref, cc_ref, sc_ref, cp_ref, sp_ref, dc_ref,
             dq_ref, dz_ref, dkv_ref, dsk_ref, carry, ds_st, pn_st, q_st, do_st):
        step = pl.program_id(1)
        n = nB - 1 - step

        @pl.when((pl.program_id(0) == 0) & (step == 0))
        def _():
            dsk_ref[...] = jnp.zeros_like(dsk_ref)

        @pl.when(step == 0)
        def _():
            carry[...] = jnp.zeros_like(carry)
        low, first_half, top, mask = _attn_consts(n)
        cc, sc, cp, sp = cc_ref[...], sc_ref[...], cp_ref[...], sp_ref[...]
        kc = _rope(kvc_ref[:, 0:LANES], cc, sc, first_half)
        kp = _rope(kvp_ref[:, 0:LANES], cp, sp, first_half)
        kdf = _dup_kv(jnp.concatenate([kp, kc], axis=0), low)
        vdf = _dup_kv(jnp.concatenate([kvp_ref[:, LANES:2 * LANES], kvc_ref[:, LANES:2 * LANES]], axis=0), low)
        kd = [x.astype(bf16) for x in kdf]
        vd = [x.astype(bf16) for x in vdf]
        kdt = [x.T.astype(bf16) for x in kdf]
        vdt = [x.T.astype(bf16) for x in vdf]
        dkd, dvd = [], []
        for pair in range(ATT_HEADS // 2):
            cols = slice(pair * LANES, (pair + 1) * LANES)
            j = (2 * pair) // ATT_GROUP
            qp = _rope(q_ref[:, cols], cc, sc, first_half)
            qt = qp.T
            zp = z_ref[:, cols]
            dc = dc_ref[:, cols]
            sz = _sigmoid(zp)
            d_o = dc * (zp * sz)
            dot_ = d_o.T
            res = []
            for hh in range(2):
                rsel = top if hh == 0 else ~top
                qtm = jnp.where(rsel, qt, 0.0).astype(bf16)
                pn, psn, o = _attn_head(qtm, kd[j], vdt[j], sk_ref[2 * pair + hh:2 * pair + hh + 1, 0:1], mask)
                res.append((rsel, pn, psn, o))
            ot = jnp.where(top, res[0][3], res[1][3])
            dz_ref[:, cols] = (dc * ot.T * (sz * (1.0 + zp * (1.0 - sz)))).astype(bf16)
            dqts = []
            for hh in range(2):
                h = 2 * pair + hh
                rsel, pn, psn, _ = res[hh]
                lsel = low if hh == 0 else ~low
                dotm = jnp.where(rsel, dot_, 0.0)
                delta = jnp.sum(dotm * ot, axis=0, keepdims=True)
                dst = (pn * (_dot(vd[j], dotm.astype(bf16)) - delta) * ATT_SCALE).astype(bf16)
                dsk_ref[h:h + 1, :] += jnp.zeros((1, LANES), f32) - jnp.sum(psn * delta)
                dqts.append(_dot(kdt[j], dst))
                g = h % ATT_GROUP
                ds_st[:, g * LANES:(g + 1) * LANES] = dst
                pn_st[:, g * LANES:(g + 1) * LANES] = pn.astype(bf16)
                q_st[g * LANES:(g + 1) * LANES, :] = jnp.where(lsel, qp, 0.0).astype(bf16)
                do_st[g * LANES:(g + 1) * LANES, :] = jnp.where(lsel, d_o, 0.0).astype(bf16)
            dq_ref[:, cols] = _rope_bwd(jnp.where(top, dqts[0], dqts[1]).T, cc, sc, first_half).astype(bf16)
            if (2 * pair + 2) % ATT_GROUP == 0:
                dkd.append(_dot(ds_st[...], q_st[...]))
                dvd.append(_dot(pn_st[...], do_st[...]))
        dk = [x + pltpu.roll(x, ATT_HEAD_DIM, 1) for x in dkd]
        dv = [x + pltpu.roll(x, ATT_HEAD_DIM, 1) for x in dvd]
        dk = jnp.where(low, dk[0], dk[1])
        dv = jnp.where(low, dv[0], dv[1])
        dkv_ref[:, 0:LANES] = (_rope_bwd(dk[L:2 * L], cc, sc, first_half) + carry[:, 0:LANES]).astype(bf16)
        dkv_ref[:, LANES:2 * LANES] = (dv[L:2 * L] + carry[:, LANES:2 * LANES]).astype(bf16)
        carry[:, 0:LANES] = _rope_bwd(dk[0:L], cp, sp, first_half)
        carry[:, LANES:2 * LANES] = dv[0:L]

    rev = lambda b, s: b * nB + nB - 1 - s
    revp = lambda b, s: b * nB + jnp.maximum(nB - 2 - s, 0)
    wide = lambda blk: pl.BlockSpec((L, ATT_WIDTH), lambda b, s: (rev(b, s), blk))
    tab = pl.BlockSpec((L, LANES), lambda b, s: (nB - 1 - s, 0))
    tabp = pl.BlockSpec((L, LANES), lambda b, s: (jnp.maximum(nB - 2 - s, 0), 0))
    return _call(
        body, (proj, proj, proj, proj, sink_b, cos, sin, cos, sin, dcat), name="swa_bwd", grid=(B, nB),
        in_specs=[wide(QA_BLK), wide(ZA_BLK),
                  pl.BlockSpec((L, 2 * KV_WIDTH), lambda b, s: (rev(b, s), KV_BLK)),
                  pl.BlockSpec((L, 2 * KV_WIDTH), lambda b, s: (revp(b, s), KV_BLK)),
                  pl.BlockSpec((ATT_HEADS, LANES), lambda b, s: (0, 0)),
                  tab, tab, tabp, tabp, wide(0)],
        out_specs=[wide(0), wide(0), pl.BlockSpec((L, 2 * KV_WIDTH), lambda b, s: (rev(b, s), 0)),
                   pl.BlockSpec((ATT_HEADS, LANES), lambda b, s: (0, 0))],
        out_shape=[jax.ShapeDtypeStruct((T, ATT_WIDTH), bf16), jax.ShapeDtypeStruct((T, ATT_WIDTH), bf16),
                   jax.ShapeDtypeStruct((T, 2 * KV_WIDTH), bf16), jax.ShapeDtypeStruct((ATT_HEADS, LANES), f32)],
        scratch_shapes=[pltpu.VMEM((L, 2 * KV_WIDTH), f32),
                        pltpu.VMEM((2 * L, ATT_GROUP * LANES), bf16), pltpu.VMEM((2 * L, ATT_GROUP * LANES), bf16),
                        pltpu.VMEM((ATT_GROUP * LANES, LANES), bf16), pltpu.VMEM((ATT_GROUP * LANES, LANES), bf16)],
        semantics=("arbitrary", "arbitrary"), ride=ride)


def _train_step(x, target, bufs, g_pre, g_post, lb_param, g_head, sinks, *, B, S, exchange):
    L = DEPTH
    T = x.shape[0]
    ri, ro = IN_WIDTH // 8, MIX_WIDTH // 8
    cos, sin = _rope_tables(S)
    full = list(bufs)
    if exchange:
        full[0] = _run_exchange(_gather_d2d(_run_exchange(_gather_ici(bufs[0]))))
    saved = []
    for l in range(L):
        wt = full[l][0].reshape(1, IN_WIDTH, D_MODEL)
        wo = full[l][1].reshape(1, MIX_WIDTH, D_MODEL)
        tail = jnp.concatenate([wt[:, 5376:6400], wt[:, 5120:5376]], axis=1)
        proj, h = _in_proj(x, g_pre[l:l + 1], wt, tail, 0)
        ahead = exchange and l + 1 < L
        (ch, o_pre, states), landed = _hgrn_fwd(proj, lb_param, g_head[l:l + 1], B=B, S=S, layer=l,
                                                ride=_gather_ici(bufs[l + 1]) if ahead else None)
        sink_b = jnp.broadcast_to(sinks[l][:, None], (ATT_HEADS, LANES))
        (ca,), passed = _swa_fwd(proj, sink_b, cos, sin, B=B, S=S, ride=_gather_d2d(landed) if ahead else None)
        if ahead:
            full[l + 1] = passed
        xn, y = _out_proj(ch, ca, wo, 0, x, g_post[l:l + 1])
        saved.append((x, proj, h, ch, o_pre, states, sink_b, ca, y, wt, tail, wo))
        x = xn
    dx, loss = _loss_head(x, target)

    def reduce_tail(sums, recv):
        return _run_exchange(_pair_share([_chip_sum(s, r) for s, r in zip(sums, recv)]))

    grads = [None] * L
    waiting = None
    gg_pre, gg_post, g_lb, gg_head, g_sinks = [], [], [], [], []
    for l in reversed(range(L)):
        x_in, proj, h, ch, o_pre, states, sink_b, ca, y, wt, tail, wo = saved[l]
        (dch, dca, dwo, dgpost), got = _out_proj_bwd(dx, y, g_post[l:l + 1], wo, 0, ch, ca,
                                                     ride=_pair_exchange(waiting) if waiting else None)
        sums = [_pair_add(p, r) for p, r in zip(waiting, got)] if waiting else None
        (dq, df, di, dz, dlb, dgh), recv = _hgrn_bwd(proj, lb_param, g_head[l:l + 1], o_pre, states, dch, B=B, S=S,
                                                     layer=l, ride=_chip_exchange(sums) if waiting else None)
        if waiting:
            grads[l + 1] = reduce_tail(sums, recv)
        at_end = exchange and l == 0
        part_o = [dwo.reshape(1, 4, 2, ro, D_MODEL)]
        (dqa, dza, dkv, dsk), got_o = _swa_bwd(proj, sink_b, cos, sin, dca, B=B, S=S,
                                              ride=_pair_exchange(part_o) if at_end else None)
        pieces = [dq, df, di, dz, dqa, dkv, dza]
        sums_o = [_pair_add(part_o[0], got_o[0])] if at_end else None
        (gwt,), recv_o = _grad_w_in(h, pieces, ride=_chip_exchange(sums_o) if at_end else None)
        part_t = [gwt.reshape(1, 4, 2, ri, D_MODEL)]
        if at_end:
            tm = min(512, T // 2)
            nb = T // tm
            na = max(1, nb // 4)
            (dx_a, dg_a), got_t = _in_proj_bwd(pieces, wt, tail, 0, x_in, g_pre[l:l + 1], dx, tm=tm, blocks=(0, na),
                                               ride=_pair_exchange(part_t))
            sums_t = [_pair_add(part_t[0], got_t[0])]
            (dx, dg_b), recv_t = _in_proj_bwd(pieces, wt, tail, 0, x_in, g_pre[l:l + 1], dx, tm=tm, blocks=(na, nb - na),
                                              dx_into=dx_a, ride=_chip_exchange(sums_t))
            dgpre = dg_a + dg_b
            grads[0] = reduce_tail(sums_t + sums_o, recv_t + recv_o)
        else:
            (dx, dgpre), _ = _in_proj_bwd(pieces, wt, tail, 0, x_in, g_pre[l:l + 1], dx)
            if exchange:
                waiting = part_t + part_o
            else:
                grads[l] = [gwt, dwo]
        gg_pre.append(dgpre[0])
        gg_post.append(dgpost[0])
        g_lb.append(jnp.sum(dlb, axis=(0, 1)))
        gg_head.append(jnp.sum(dgh, axis=(0, 1, 2)))
        g_sinks.append(dsk[:, 0])
    rev = lambda xs: jnp.stack(xs[::-1])
    return loss[0, 0], dx, grads, rev(gg_pre), rev(gg_post), rev(g_lb), rev(gg_head), rev(g_sinks)


MESH = pl.DeviceIdType.MESH
ANY = pl.BlockSpec(memory_space=pl.ANY)


def _place():
    x, y, c = lax.axis_index("x"), lax.axis_index("y"), lax.axis_index("c")
    return x, y, c, [(1 - x, y), (x, 1 - y), (1 - x, 1 - y)]


def _rcopy(src, dst, send, recv, k, to):
    return pltpu.make_async_remote_copy(src_ref=src, dst_ref=dst, send_sem=send.at[k], recv_sem=recv.at[k],
                                        device_id=to, device_id_type=MESH)


class _Exchange:
    def __init__(self, name, inputs, out_shapes, n_sems, plan, in_place=False):
        self.name, self.inputs, self.out_shapes, self.n_sems, self.plan = name, inputs, out_shapes, n_sems, plan
        self.aliases = {a: a for a in range(len(inputs))} if in_place else {}

    def start(self, ins, outs, send, recv):
        for cp in self.plan(ins, outs, send, recv)[0]:
            cp.start()

    def finish(self, ins, outs, send, recv):
        sent, arriving = self.plan(ins, outs, send, recv)
        for cp in arriving:
            cp.wait_recv()
        for cp in sent:
            cp.wait_send()

    def sems(self):
        return [pltpu.SemaphoreType.DMA((self.n_sems,)), pltpu.SemaphoreType.DMA((self.n_sems,))]


def _run_exchange(ex):
    n_in, n_out = len(ex.inputs), len(ex.out_shapes)

    def body(*refs):
        ins, outs = refs[:n_in], refs[n_in:n_in + n_out]
        send, recv = refs[n_in + n_out:]
        ex.start(ins, outs, send, recv)
        ex.finish(ins, outs, send, recv)

    return pl.pallas_call(
        body, name=ex.name, in_specs=[ANY] * n_in, out_specs=[ANY] * n_out, out_shape=ex.out_shapes,
        input_output_aliases=ex.aliases, scratch_shapes=ex.sems(),
    )(*ex.inputs)


def _call(body, operands, *, name, grid, in_specs, out_specs, out_shape, scratch_shapes=(), semantics, ride=None,
          aliases=None):
    aliases = dict(aliases or {})
    if ride is None:
        outs = pl.pallas_call(body, name=name, grid=grid, in_specs=in_specs, out_specs=out_specs, out_shape=out_shape,
                              input_output_aliases=aliases, scratch_shapes=list(scratch_shapes),
                              compiler_params=_params(*semantics))(*operands)
        return outs, []
    n_in, n_out, n_scr = len(in_specs), len(out_specs), len(scratch_shapes)
    r_in, r_out = len(ride.inputs), len(ride.out_shapes)

    def riding(*refs):
        refs = list(refs)
        ins, rins = refs[:n_in], refs[n_in:n_in + r_in]
        o0 = n_in + r_in
        outs, routs = refs[o0:o0 + n_out], refs[o0 + n_out:o0 + n_out + r_out]
        scr = refs[o0 + n_out + r_out:o0 + n_out + r_out + n_scr]
        send, recv = refs[-2:]
        ids = [pl.program_id(d) for d in range(len(grid))]
        first = functools.reduce(jnp.logical_and, [i == 0 for i in ids])
        last = functools.reduce(jnp.logical_and, [i == g - 1 for i, g in zip(ids, grid)])
        pl.when(first)(lambda: ride.start(rins, routs, send, recv))
        body(*ins, *outs, *scr)
        pl.when(last)(lambda: ride.finish(rins, routs, send, recv))

    res = pl.pallas_call(
        riding, name=name + "_" + ride.name, grid=grid,
        in_specs=list(in_specs) + [ANY] * r_in, out_specs=list(out_specs) + [ANY] * r_out,
        out_shape=list(out_shape) + list(ride.out_shapes),
        input_output_aliases={**aliases, **{n_in + a: n_out + b for a, b in ride.aliases.items()}},
        scratch_shapes=list(scratch_shapes) + ride.sems(),
        compiler_params=_params(*(["arbitrary"] * len(grid))),
    )(*operands, *ride.inputs)
    return res[:n_out], res[n_out:]


def _gather_ici(bufs, name="gather_ici"):
    n = len(bufs)

    def plan(ins, outs, send, recv):
        x, y, c, chips = _place()
        me = 2 * x + y
        sent, arriving = [], []
        for j, (px, py) in enumerate(chips):
            for a in range(n):
                mine, theirs = outs[a].at[:, me, c], outs[a].at[:, 2 * px + py, c]
                sent.append(_rcopy(mine, mine, send, recv, j * n + a, (px, py, c)))
                arriving.append(_rcopy(theirs, theirs, send, recv, j * n + a, (px, py, c)))
        return sent, arriving

    return _Exchange(name, bufs, [jax.ShapeDtypeStruct(b.shape, b.dtype) for b in bufs], 3 * n, plan, in_place=True)


def _gather_d2d(bufs, name="gather_d2d"):
    n = len(bufs)

    def plan(ins, outs, send, recv):
        x, y, c, chips = _place()
        sib = (x, y, 1 - c)
        sent, arriving = [], []
        for j, (px, py) in enumerate(chips):
            for a in range(n):
                got, theirs = outs[a].at[:, 2 * px + py, c], outs[a].at[:, 2 * px + py, 1 - c]
                sent.append(_rcopy(got, got, send, recv, j * n + a, sib))
                arriving.append(_rcopy(theirs, theirs, send, recv, j * n + a, sib))
        return sent, arriving

    return _Exchange(name, bufs, [jax.ShapeDtypeStruct(b.shape, b.dtype) for b in bufs], 3 * n, plan, in_place=True)


def _pair_exchange(parts):
    n = len(parts)

    def plan(ins, outs, send, recv):
        x, y, c, _ = _place()
        cps = [_rcopy(ins[a].at[:, :, 1 - c], outs[a], send, recv, a, (x, y, 1 - c)) for a in range(n)]
        return cps, cps

    return _Exchange("pair_exchange", parts,
                     [jax.ShapeDtypeStruct(p.shape[:2] + p.shape[3:], p.dtype) for p in parts], n, plan)


def _block_rows(r):
    return r if r <= 512 else r // 2


def _pair_add(part, got):
    L, K, _, r, C = part.shape
    rows = _block_rows(r)

    def body(c_ref, a_ref, b_ref, o_ref):
        o_ref[0, 0] = (a_ref[0, 0, 0] + b_ref[0, 0]).astype(bf16)

    blk = (1, 1, rows, C)
    return pl.pallas_call(
        body, name="pair_add",
        grid_spec=pltpu.PrefetchScalarGridSpec(
            num_scalar_prefetch=1, grid=(L, K, r // rows),
            in_specs=[pl.BlockSpec((1, 1, 1, rows, C), lambda l, k, i, c: (l, k, c[0], i, 0)),
                      pl.BlockSpec(blk, lambda l, k, i, c: (l, k, i, 0))],
            out_specs=pl.BlockSpec(blk, lambda l, k, i, c: (l, k, i, 0))),
        out_shape=jax.ShapeDtypeStruct((L, K, r, C), bf16),
        compiler_params=_params("parallel", "parallel", "parallel"),
    )(jnp.reshape(lax.axis_index("c"), (1,)).astype(jnp.int32), part, got)


def _chip_exchange(sums):
    n = len(sums)

    def plan(ins, outs, send, recv):
        x, y, c, chips = _place()
        cps = []
        for j, (px, py) in enumerate(chips):
            for a in range(n):
                cps.append(_rcopy(ins[a].at[:, 2 * px + py], outs[a].at[j], send, recv, j * n + a, (px, py, c)))
        return cps, cps

    return _Exchange("chip_exchange", sums,
                     [jax.ShapeDtypeStruct((3, s.shape[0]) + s.shape[2:], s.dtype) for s in sums], 3 * n, plan)


def _chip_sum(mine, got):
    L, K, r, C = mine.shape
    rows = _block_rows(r)

    def body(p_ref, a_ref, b_ref, o_ref):
        o_ref[0, 0] = (a_ref[0, 0].astype(f32) + b_ref[0, 0].astype(f32)) + (b_ref[1, 0].astype(f32) + b_ref[2, 0].astype(f32))

    place = jnp.stack([2 * lax.axis_index("x") + lax.axis_index("y"), lax.axis_index("c")]).astype(jnp.int32)
    return pl.pallas_call(
        body, name="chip_sum",
        grid_spec=pltpu.PrefetchScalarGridSpec(
            num_scalar_prefetch=1, grid=(L, r // rows),
            in_specs=[pl.BlockSpec((1, 1, rows, C), lambda l, i, p: (l, p[0], i, 0)),
                      pl.BlockSpec((3, 1, rows, C), lambda l, i, p: (0, l, i, 0))],
            out_specs=pl.BlockSpec((1, 1, rows, C), lambda l, i, p: (l, p[1], i, 0))),
        out_shape=jax.ShapeDtypeStruct((L, 2, r, C), f32),
        compiler_params=_params("parallel", "parallel"),
    )(place, mine, got)


def _pair_share(bufs):
    n = len(bufs)

    def plan(ins, outs, send, recv):
        x, y, c, _ = _place()
        sib = (x, y, 1 - c)
        sent = [_rcopy(outs[a].at[:, c], outs[a].at[:, c], send, recv, a, sib) for a in range(n)]
        arriving = [_rcopy(outs[a].at[:, 1 - c], outs[a].at[:, 1 - c], send, recv, a, sib) for a in range(n)]
        return sent, arriving

    return _Exchange("pair_share", bufs, [jax.ShapeDtypeStruct(b.shape, b.dtype) for b in bufs], n, plan, in_place=True)


def _all_sum_small(v):
    def body(v_ref, o_ref, buf, send, recv):
        x, y, c, _ = _place()
        me = 4 * x + 2 * y + c
        buf[me] = v_ref[...]
        cps = []
        for m in range(1, 8):
            to = (x ^ (m >> 2), y ^ ((m >> 1) & 1), c ^ (m & 1))
            cps.append(_rcopy(v_ref, buf.at[me], send, recv, m - 1, to))
        for cp in cps:
            cp.start()
        for cp in cps:
            cp.wait()
        acc = buf[0]
        for d in range(1, 8):
            acc = acc + buf[d]
        o_ref[...] = acc

    vm = pl.BlockSpec(memory_space=pltpu.VMEM)
    return pl.pallas_call(
        body, name="all_sum_small", in_specs=[vm], out_specs=vm,
        out_shape=jax.ShapeDtypeStruct(v.shape, v.dtype),
        scratch_shapes=[pltpu.VMEM((8,) + v.shape, v.dtype), pltpu.SemaphoreType.DMA((7,)), pltpu.SemaphoreType.DMA((7,))],
    )(v)


def _adamw_math(w, g, m, v):
    m = ADAM_B1 * m + (1.0 - ADAM_B1) * g
    v = ADAM_B2 * v + (1.0 - ADAM_B2) * (g * g)
    m_hat = m / (1.0 - ADAM_B1 ** ADAM_STEP)
    v_hat = v / (1.0 - ADAM_B2 ** ADAM_STEP)
    return -ADAM_LR * (m_hat / (jnp.sqrt(v_hat) + ADAM_EPS) + ADAM_WD * w), m, v


def _adamw(w, g, m, v):
    L, R, C = w.shape
    rows = R // 4

    def body(w_ref, g_ref, m_ref, v_ref, d_ref, mo_ref, vo_ref):
        d_ref[...], mo_ref[...], vo_ref[...] = _adamw_math(w_ref[...], g_ref[...], m_ref[...], v_ref[...])

    blk = pl.BlockSpec((1, rows, C), lambda l, i: (l, i, 0))
    return pl.pallas_call(
        body, name="adamw", grid=(L, R // rows), in_specs=[blk] * 4, out_specs=[blk] * 3,
        out_shape=[jax.ShapeDtypeStruct(w.shape, f32)] * 3,
        compiler_params=_params("parallel", "parallel"),
    )(w, g, m, v)


def _chip_index():
    return jnp.reshape(2 * lax.axis_index("x") + lax.axis_index("y"), (1,)).astype(jnp.int32)


def _shard_placed(w, l):
    _, R, C = w.shape
    rows = R // 4

    def body(k_ref, w_ref, o_ref):
        o_ref[0, 0] = w_ref[0].astype(bf16)

    return pl.pallas_call(
        body, name="shard_placed",
        grid_spec=pltpu.PrefetchScalarGridSpec(
            num_scalar_prefetch=1, grid=(R // rows,),
            in_specs=[pl.BlockSpec((1, rows, C), lambda i, k: (l, i, 0))],
            out_specs=pl.BlockSpec((1, 1, rows, C), lambda i, k: (0, k[0], i, 0))),
        out_shape=jax.ShapeDtypeStruct((1, 4, R, C), bf16),
        compiler_params=_params("parallel"),
    )(_chip_index(), w)


SMALL_ROWS = 4 * DEPTH


def _pack_small(g_pre, g_post, lb, g_head, sinks, loss=None):
    rows = []
    for l in range(DEPTH):
        tail = [g_head[l], sinks[l]]
        if loss is not None and l == 0:
            tail.append(jnp.reshape(loss, (1,)))
        tail = jnp.concatenate(tail)
        rows += [g_pre[l], g_post[l], lb[l], jnp.pad(tail, (0, D_MODEL - tail.shape[0]))]
    return jnp.stack(rows)


def _unpack_small(p):
    g_pre = jnp.stack([p[4 * l] for l in range(DEPTH)])
    g_post = jnp.stack([p[4 * l + 1] for l in range(DEPTH)])
    lb = jnp.stack([p[4 * l + 2] for l in range(DEPTH)])
    g_head = jnp.stack([p[4 * l + 3, :HG_HEAD_DIM] for l in range(DEPTH)])
    sinks = jnp.stack([p[4 * l + 3, HG_HEAD_DIM:HG_HEAD_DIM + ATT_HEADS] for l in range(DEPTH)])
    return g_pre, g_post, lb, g_head, sinks


def _small_update(gsum, w, m, v):
    def body(g_ref, w_ref, m_ref, v_ref, go_ref, d_ref, mo_ref, vo_ref):
        g = g_ref[...]
        w = w_ref[...]
        lbp = [w[4 * l + 2:4 * l + 3] for l in range(DEPTH)]
        mx = functools.reduce(jnp.maximum, lbp)
        e = [jnp.exp(t - mx) for t in lbp]
        tot = functools.reduce(jnp.add, e)
        p = [t / tot for t in e]
        glb = [g[4 * l + 2:4 * l + 3] for l in range(DEPTH)]
        row = lax.broadcasted_iota(jnp.int32, g.shape, 0)
        for j in range(DEPTH):
            gj = jnp.zeros_like(p[0])
            for l in range(DEPTH):
                for i in range(1, l + 1):
                    gj = gj + glb[l] * p[i] * ((1.0 if i == j else 0.0) - p[j])
            g = jnp.where(row == 4 * j + 2, gj, g)
        go_ref[...] = g
        d_ref[...], mo_ref[...], vo_ref[...] = _adamw_math(w, g, m_ref[...], v_ref[...])

    vm = pl.BlockSpec(memory_space=pltpu.VMEM)
    return pl.pallas_call(
        body, name="small_update", in_specs=[vm] * 4, out_specs=[vm] * 4,
        out_shape=[jax.ShapeDtypeStruct(gsum.shape, f32)] * 4,
    )(gsum, w, m, v)


def kernel(x, w_in, w_out, g_pre, g_post, lb_param, g_head, sinks, loss_target, m_w_in, m_w_out, m_g_pre, m_g_post, m_lb_param, m_g_head, m_sinks, v_w_in, v_w_out, v_g_pre, v_g_post, v_lb_param, v_g_head, v_sinks):
    B, S, _ = x.shape
    T = B * S
    L = DEPTH
    ri, ro = IN_WIDTH // 8, MIX_WIDTH // 8
    tr = lambda a: jnp.transpose(a, (0, 2, 1))
    wt, mt, vt = tr(w_in), tr(m_w_in), tr(v_w_in)
    bufs = [[_shard_placed(wt, l).reshape(1, 4, 2, ri, D_MODEL), _shard_placed(w_out, l).reshape(1, 4, 2, ro, D_MODEL)]
            for l in range(L)]
    loss, dx, grads, ggpre, ggpost, glb, gghead, gsinks = _train_step(
        x.reshape(T, D_MODEL), loss_target.reshape(T, D_MODEL), bufs, g_pre, g_post, lb_param, g_head, sinks,
        B=B, S=S, exchange=True)
    gwt_mine = jnp.concatenate([g[0] for g in grads], axis=0).reshape(L, 2 * ri, D_MODEL)
    grad_w_out = jnp.concatenate([g[1] for g in grads], axis=0).reshape(L, 2 * ro, D_MODEL)

    d_wt, nm_wt, nv_wt = _adamw(wt, gwt_mine, mt, vt)
    grad_w_in, d_w_in, nm_w_in, nv_w_in = tr(gwt_mine), tr(d_wt), tr(nm_wt), tr(nv_wt)
    d_w_out, nm_w_out, nv_w_out = _adamw(w_out, grad_w_out, m_w_out, v_w_out)

    gsum = _all_sum_small(_pack_small(ggpre, ggpost, glb, gghead, gsinks, loss))
    gs, ds, ms, vs = _small_update(
        gsum, _pack_small(g_pre, g_post, lb_param, g_head, sinks),
        _pack_small(m_g_pre, m_g_post, m_lb_param, m_g_head, m_sinks),
        _pack_small(v_g_pre, v_g_post, v_lb_param, v_g_head, v_sinks))
    loss_all = gsum[3, HG_HEAD_DIM + ATT_HEADS]
    return (loss_all, dx.reshape(B, S, D_MODEL), grad_w_in, grad_w_out, *_unpack_small(gs),
            d_w_in, d_w_out, *_unpack_small(ds), nm_w_in, nm_w_out, *_unpack_small(ms),
            nv_w_in, nv_w_out, *_unpack_small(vs))
```

```python
import functools
import math

import jax
import jax.numpy as jnp
from jax import lax
from jax.experimental import pallas as pl
from jax.experimental.pallas import tpu as pltpu

f32 = jnp.float32
bf16 = jnp.bfloat16

D_MODEL = 1024
DEPTH = 2
HG_WIDTH = 1024
HG_HEAD_DIM = 128
HG_HEADS = 8
CHUNK = 64
SUB = 16
ATT_WIDTH = 1024
ATT_HEAD_DIM = 64
ATT_HEADS = 16
ATT_GROUP = 8
KV_WIDTH = 128
ATT_BLOCK = 128
ATT_SCALE = 1.0 / math.sqrt(ATT_HEAD_DIM)
ROPE_THETA = 10000.0
IN_WIDTH = 6400
MIX_WIDTH = 2048
NORM_EPS = 1e-6
NEG_INF = -1e30
LB_FLOOR = 1e-20
LANES = 128
VMEM_LIMIT = 48 * 1024 * 1024

ADAM_LR = 0.001
ADAM_B1 = 0.9
ADAM_B2 = 0.999
ADAM_EPS = 1e-08
ADAM_WD = 0.01
ADAM_STEP = 10

QA_BLK, ZA_BLK, KV_BLK = 4, 5, 24

NT = (((1,), (1,)), ((), ()))
TN = (((0,), (0,)), ((), ()))


def _dot(a, b, dims=None, precision=None):
    if dims is None:
        return jnp.dot(a, b, preferred_element_type=f32, precision=precision)
    return lax.dot_general(a, b, dims, preferred_element_type=f32, precision=precision)


def _sigmoid(x):
    return 1.0 / (1.0 + jnp.exp(-x))


def _params(*sem):
    return pltpu.CompilerParams(dimension_semantics=sem, vmem_limit_bytes=VMEM_LIMIT)


TAIL = IN_WIDTH - 5120


def _in_proj(x, g, wt, tail, l, *, tm=1024):
    T = x.shape[0]
    tm = min(tm, T)
    nmain = 5120 // TAIL

    def body(x_ref, g_ref, w_ref, t_ref, p_ref, h_ref, hs):
        j = pl.program_id(1)

        @pl.when(j == 0)
        def _():
            xv = x_ref[...]
            r = lax.rsqrt(jnp.mean(xv * xv, axis=-1, keepdims=True) + NORM_EPS)
            hv = (xv * r * g_ref[...]).astype(bf16)
            hs[...] = hv
            h_ref[...] = hv

        @pl.when(j < nmain)
        def _():
            p_ref[...] = _dot(hs[...], w_ref[...], NT)

        @pl.when(j == nmain)
        def _():
            p_ref[...] = _dot(hs[...], t_ref[...], NT)

    return pl.pallas_call(
        body, name="in_proj", grid=(T // tm, nmain + 1),
        in_specs=[pl.BlockSpec((tm, D_MODEL), lambda i, j: (i, 0)),
                  pl.BlockSpec((1, D_MODEL), lambda i, j: (0, 0)),
                  pl.BlockSpec((None, TAIL, D_MODEL), lambda i, j: (l, jnp.minimum(j, nmain - 1), 0)),
                  pl.BlockSpec((None, TAIL, D_MODEL), lambda i, j: (l, 0, 0))],
        out_specs=[pl.BlockSpec((tm, TAIL), lambda i, j: (i, j)),
                   pl.BlockSpec((tm, D_MODEL), lambda i, j: (i, 0))],
        out_shape=[jax.ShapeDtypeStruct((T, IN_WIDTH), f32), jax.ShapeDtypeStruct((T, D_MODEL), bf16)],
        scratch_shapes=[pltpu.VMEM((tm, D_MODEL), bf16)],
        compiler_params=_params("parallel", "arbitrary"),
    )(x, g, wt, tail)


def _out_proj(ch, ca, wo, l, x, g, *, tm=512):
    T = x.shape[0]
    tm = min(tm, T)
    half = MIX_WIDTH // 2

    def body(ch_ref, ca_ref, wo_ref, x_ref, g_ref, xn_ref, y_ref):
        y = _dot(ch_ref[...], wo_ref[0:half, :]) + _dot(ca_ref[...], wo_ref[half:MIX_WIDTH, :])
        r = lax.rsqrt(jnp.mean(y * y, axis=-1, keepdims=True) + NORM_EPS)
        y_ref[...] = y
        xn_ref[...] = x_ref[...] + y * r * g_ref[...]

    row = lambda i: (i, 0)
    fixed = lambda i: (0, 0)
    return pl.pallas_call(
        body, name="out_proj", grid=(T // tm,),
        in_specs=[pl.BlockSpec((tm, half), row), pl.BlockSpec((tm, half), row),
                  pl.BlockSpec((None, MIX_WIDTH, D_MODEL), lambda i: (l, 0, 0)), pl.BlockSpec((tm, D_MODEL), row),
                  pl.BlockSpec((1, D_MODEL), fixed)],
        out_specs=[pl.BlockSpec((tm, D_MODEL), row), pl.BlockSpec((tm, D_MODEL), row)],
        out_shape=[jax.ShapeDtypeStruct((T, D_MODEL), f32)] * 2,
        compiler_params=_params("parallel"),
    )(ch, ca, wo, x, g)


def _loss_head(y, target, *, tm=512):
    T = y.shape[0]
    tm = min(tm, T)

    def body(y_ref, t_ref, d_ref, l_ref):
        @pl.when(pl.program_id(0) == 0)
        def _():
            l_ref[...] = jnp.zeros_like(l_ref)
        err = y_ref[...] - t_ref[...]
        d_ref[...] = err * (1.0 / D_MODEL)
        l_ref[...] += jnp.sum(err * err) * (0.5 / D_MODEL)

    row = lambda i: (i, 0)
    return pl.pallas_call(
        body, name="loss_head", grid=(T // tm,),
        in_specs=[pl.BlockSpec((tm, D_MODEL), row), pl.BlockSpec((tm, D_MODEL), row)],
        out_specs=[pl.BlockSpec((tm, D_MODEL), row), pl.BlockSpec((8, LANES), lambda i: (0, 0))],
        out_shape=[jax.ShapeDtypeStruct((T, D_MODEL), f32), jax.ShapeDtypeStruct((8, LANES), f32)],
        compiler_params=_params("arbitrary"),
    )(y, target)


def _out_proj_bwd(dxn, y, g, wo, l, ch, ca, *, tm=256, ride=None):
    T = y.shape[0]
    tm = min(tm, T)
    half = MIX_WIDTH // 2

    def body(dx_ref, y_ref, g_ref, wo_ref, ch_ref, ca_ref, dch_ref, dca_ref, dwo_ref, dg_ref):
        @pl.when(pl.program_id(0) == 0)
        def _():
            dwo_ref[...] = jnp.zeros_like(dwo_ref)
            dg_ref[...] = jnp.zeros_like(dg_ref)
        y = y_ref[...]
        dx = dx_ref[...]
        r = lax.rsqrt(jnp.mean(y * y, axis=-1, keepdims=True) + NORM_EPS)
        gy = dx * g_ref[...]
        dy = r * gy - y * (r * r * r) * jnp.mean(gy * y, axis=-1, keepdims=True)
        dg_ref[...] += jnp.sum(dx * y * r, axis=0, keepdims=True)
        dyb = dy.astype(bf16)
        dch_ref[...] = _dot(dyb, wo_ref[0:half, :], NT)
        dca_ref[...] = _dot(dyb, wo_ref[half:MIX_WIDTH, :], NT)
        dwo_ref[0:half, :] += _dot(ch_ref[...], dyb, TN)
        dwo_ref[half:MIX_WIDTH, :] += _dot(ca_ref[...], dyb, TN)

    row = lambda i: (i, 0)
    fixed = lambda i: (0, 0)
    return _call(
        body, (dxn, y, g, wo, ch, ca), name="out_proj_bwd", grid=(T // tm,),
        in_specs=[pl.BlockSpec((tm, D_MODEL), row), pl.BlockSpec((tm, D_MODEL), row),
                  pl.BlockSpec((1, D_MODEL), fixed), pl.BlockSpec((None, MIX_WIDTH, D_MODEL), lambda i: (l, 0, 0)),
                  pl.BlockSpec((tm, half), row), pl.BlockSpec((tm, half), row)],
        out_specs=[pl.BlockSpec((tm, half), row), pl.BlockSpec((tm, half), row),
                   pl.BlockSpec((MIX_WIDTH, D_MODEL), fixed), pl.BlockSpec((1, D_MODEL), fixed)],
        out_shape=[jax.ShapeDtypeStruct((T, half), f32), jax.ShapeDtypeStruct((T, half), f32),
                   jax.ShapeDtypeStruct((MIX_WIDTH, D_MODEL), f32), jax.ShapeDtypeStruct((1, D_MODEL), f32)],
        semantics=("arbitrary",), ride=ride)


TILE = 256
PIECE_TILES = (4, 4, 4, 4, 4, 1, 4)
PIECE_START = tuple(sum(PIECE_TILES[:p]) for p in range(len(PIECE_TILES)))
N_TILES = sum(PIECE_TILES)


def _piece_specs(rows, index):
    def spec(s, n):
        def index_map(*g):
            r, t = index(*g)
            return r, jnp.clip(t - s, 0, n - 1)
        return pl.BlockSpec((rows, TILE), index_map)
    return [spec(s, n) for s, n in zip(PIECE_START, PIECE_TILES)]


def _for_piece(t, fn):
    for p, (s, n) in enumerate(zip(PIECE_START, PIECE_TILES)):
        pl.when((t >= s) & (t < s + n))(functools.partial(fn, p))


def _in_proj_bwd(pieces, wt, tail, l, x, g, dxn, *, tm=512, blocks=None, dx_into=None, ride=None):
    T = x.shape[0]
    tm = min(tm, T)
    first, count = blocks or (0, T // tm)
    npc = len(pieces)
    nk = npc
    nmain = npc - 2
    wide = ATT_WIDTH
    extra = [] if dx_into is None else [dx_into]

    def body(*refs):
        dp_refs = refs[:npc]
        w_ref, tz_ref, tkv_ref, x_ref, g_ref, dxn_ref = refs[npc:npc + 6]
        dx_ref, dg_ref, acc = refs[npc + 6 + len(extra):]
        i, k = pl.program_id(0), pl.program_id(1)

        @pl.when((i == 0) & (k == 0))
        def _():
            dg_ref[...] = jnp.zeros_like(dg_ref)

        @pl.when(k == 0)
        def _():
            acc[...] = jnp.zeros_like(acc)

        for p in range(npc):
            w_p = w_ref if p < nmain else (tkv_ref if p == nmain else tz_ref)

            def add(p=p, w_p=w_p):
                acc[...] += _dot(dp_refs[p][...], w_p[...])
            pl.when(k == p)(add)

        @pl.when(k == nk - 1)
        def _():
            dh = acc[...]
            xv = x_ref[...]
            r = lax.rsqrt(jnp.mean(xv * xv, axis=-1, keepdims=True) + NORM_EPS)
            gy = dh * g_ref[...]
            dx_ref[...] = dxn_ref[...] + r * gy - xv * (r * r * r) * jnp.mean(gy * xv, axis=-1, keepdims=True)
            dg_ref[...] += jnp.sum(dh * xv * r, axis=0, keepdims=True)

    rows = lambda i, k: (first + i, 0)
    return _call(
        body, (*pieces, wt, tail, tail, x, g, dxn, *extra), name="in_proj_bwd", grid=(count, nk),
        in_specs=[pl.BlockSpec((tm, p.shape[1]), rows) for p in pieces] + [
            pl.BlockSpec((None, wide, D_MODEL), lambda i, k: (l, jnp.minimum(k, nmain - 1), 0)),
            pl.BlockSpec((None, wide, D_MODEL), lambda i, k: (l, 0, 0)),
            pl.BlockSpec((None, TAIL - wide, D_MODEL), lambda i, k: (l, wide // (TAIL - wide), 0)),
            pl.BlockSpec((tm, D_MODEL), rows), pl.BlockSpec((1, D_MODEL), lambda i, k: (0, 0)),
            pl.BlockSpec((tm, D_MODEL), rows)] + [ANY] * len(extra),
        out_specs=[pl.BlockSpec((tm, D_MODEL), rows), pl.BlockSpec((1, D_MODEL), lambda i, k: (0, 0))],
        out_shape=[jax.ShapeDtypeStruct((T, D_MODEL), f32), jax.ShapeDtypeStruct((1, D_MODEL), f32)],
        scratch_shapes=[pltpu.VMEM((tm, D_MODEL), f32)],
        semantics=("arbitrary", "arbitrary"), ride=ride, aliases={npc + 6: 0} if extra else None)


def _grad_w_in(h, pieces, *, ride=None):
    T = h.shape[0]
    npc = len(pieces)

    def body(*refs):
        h_ref, dp_refs, o_ref = refs[0], refs[1:1 + npc], refs[1 + npc]

        def put(p):
            o_ref[...] = _dot(dp_refs[p][...], h_ref[...], TN)
        _for_piece(pl.program_id(0), put)

    return _call(
        body, (h, *pieces), name="grad_w_in", grid=(N_TILES,),
        in_specs=[pl.BlockSpec((T, D_MODEL), lambda j: (0, 0), pipeline_mode=pl.Buffered(1))]
        + _piece_specs(T, lambda j: (0, j)),
        out_specs=[pl.BlockSpec((TILE, D_MODEL), lambda j: (j, 0))],
        out_shape=[jax.ShapeDtypeStruct((IN_WIDTH, D_MODEL), f32)],
        semantics=("parallel",), ride=ride)


def _lower_bound(lbp, layer):
    m = jnp.max(lbp, axis=0, keepdims=True)
    e = jnp.exp(lbp - m)
    p = e / jnp.sum(e, axis=0, keepdims=True)
    acc = p[0:1]
    for i in range(1, layer + 1):
        acc = acc + p[i:i + 1]
    return acc - p[0:1]


def _gate_parts(qr, fr, lb, lbf):
    sq = _sigmoid(qr)
    e = jnp.exp(-jnp.abs(fr))
    inv = 1.0 / (1.0 + e)
    pos = fr >= 0
    sg = jnp.where(pos, inv, e * inv)
    nsg = jnp.where(pos, e * inv, inv)
    fg = lbf + (1.0 - lb) * sg
    return qr * sq, sq, sg, nsg, fg, jnp.log(fg), (1.0 - lb) * nsg


LEVELS = tuple(SUB << j for j in range((CHUNK // SUB).bit_length() - 1))


def _level_masks(transposed=False):
    t = lax.broadcasted_iota(jnp.int32, (CHUNK, CHUNK), 1 if transposed else 0)
    s = lax.broadcasted_iota(jnp.int32, (CHUNK, CHUNK), 0 if transposed else 1)
    return [(t % (2 * m) >= m) & (s % (2 * m) < m) & (t // (2 * m) == s // (2 * m)) for m in LEVELS]


def _level_anchor(b_s, row, m):
    beta = b_s[m - 1:m, :]
    for g in range(1, CHUNK // (2 * m)):
        beta = jnp.where(row >= g * 2 * m, b_s[g * 2 * m + m - 1:g * 2 * m + m, :], beta)
    return beta


INTERLEAVE = 8
SWA_INTERLEAVE = 4


def _interleaved(chunks, width=INTERLEAVE):
    for g0 in range(0, len(chunks), width):
        live = chunks[g0:g0 + width]
        while live:
            for gen in list(live):
                try:
                    next(gen)
                except StopIteration:
                    live.remove(gen)


def _seg_sum(seg, x):
    hi = x.astype(bf16)
    return _dot(seg, hi) + _dot(seg, (x - hi.astype(f32)).astype(bf16))


def _hgrn_fwd(proj, lb_param, g_head, *, B, S, layer, ride=None):
    T = B * S
    TB = min(512, S)
    nT, NC = S // TB, TB // CHUNK
    nC = S // CHUNK
    HD = HG_HEAD_DIM

    def body(q_ref, f_ref, i_ref, z_ref, lb_ref, gh_ref, cat_ref, op_ref, st_ref,
             s_scr, b_scr, k_scr):
        @pl.when(pl.program_id(2) == 0)
        def _():
            s_scr[...] = jnp.zeros_like(s_scr)
        lb = _lower_bound(lb_ref[...], layer)
        lbf = jnp.maximum(lb, LB_FLOOR)
        gh = gh_ref[...]
        r_i = lax.broadcasted_iota(jnp.int32, (CHUNK, CHUNK), 0)
        c_i = lax.broadcasted_iota(jnp.int32, (CHUNK, CHUNK), 1)
        tril = (r_i >= c_i).astype(bf16)
        rows8 = lax.broadcasted_iota(jnp.int32, (8, HD), 0)
        row_c = lax.broadcasted_iota(jnp.int32, (CHUNK, HD), 0)
        lane_c = lax.broadcasted_iota(jnp.int32, (8, CHUNK), 1)
        masks = _level_masks()

        def chunk(c, carried):
            rs = slice(c * CHUNK, (c + 1) * CHUNK)
            b_s, k_s = b_scr.at[c], k_scr.at[c]
            q, _, _, _, _, logf, k = _gate_parts(q_ref[rs, :], f_ref[rs, :], lb, lbf)
            v = i_ref[rs, :]
            b = _seg_sum(tril, logf)
            b_s[...] = b
            k_s[...] = k
            yield
            pieces = []
            for blk in range(CHUNK // SUB):
                r0 = blk * SUB
                bp = [b[r0 + 8 * i:r0 + 8 * i + 8] for i in range(SUB // 8)]
                qp = [q[r0 + 8 * i:r0 + 8 * i + 8] for i in range(SUB // 8)]
                ap = [jnp.zeros((8, CHUNK), f32) for _ in range(SUB // 8)]
                for s in range(SUB):
                    bs = b_s[r0 + s:r0 + s + 1, :]
                    ks = k_s[r0 + s:r0 + s + 1, :]
                    for i in range(s // 8, SUB // 8):
                        diff = bp[i] - bs
                        if i == s // 8:
                            diff = jnp.where(rows8 >= s - 8 * i, diff, NEG_INF)
                        col = jnp.sum(jnp.exp(diff) * qp[i] * ks, axis=1, keepdims=True)
                        ap[i] = jnp.where(lane_c == r0 + s, col, ap[i])
                pieces += ap
                yield
            a_all = jnp.concatenate(pieces, axis=0)
            for m, mk in zip(LEVELS, masks):
                beta = _level_anchor(b_s, row_c, m)
                qh = (q * jnp.exp(jnp.minimum(b - beta, 0.0))).astype(bf16)
                kh = (k * jnp.exp(jnp.minimum(beta - b, 0.0))).astype(bf16)
                a_all = a_all + jnp.where(mk, _dot(qh, kh, NT), 0.0)
            yield
            st = carried[0]
            st_ref[0, 0, c] = st
            vb16 = v.astype(bf16)
            o = _dot(a_all.astype(bf16), vb16) + _dot((q * jnp.exp(b)).astype(bf16), st.astype(bf16), NT)
            b_end = b_s[CHUNK - 1:CHUNK, :]
            kdec = (k * jnp.exp(b_end - b)).astype(bf16)
            carried[0] = jnp.exp(b_end) * st + _dot(vb16, kdec, TN)
            rr = lax.rsqrt(jnp.mean(o * o, axis=-1, keepdims=True) + NORM_EPS)
            zr = z_ref[rs, :]
            cat_ref[rs, :] = (o * rr * gh * (zr * _sigmoid(zr))).astype(bf16)
            op_ref[rs, :] = o

        carried = [s_scr[...]]
        _interleaved([chunk(c, carried) for c in range(NC)])
        s_scr[...] = carried[0]

    def col(part):
        return pl.BlockSpec((TB, HD), lambda b, h, n: (b * nT + n, part * HG_HEADS + h))

    out_col = pl.BlockSpec((TB, HD), lambda b, h, n: (b * nT + n, h))
    return _call(
        body, (proj, proj, proj, proj, lb_param, g_head),
        name=f"hgrn_fwd_l{layer}", grid=(B, HG_HEADS, nT),
        in_specs=[col(0), col(1), col(2), col(3),
                  pl.BlockSpec((DEPTH, HD), lambda b, h, n: (0, h)),
                  pl.BlockSpec((1, HD), lambda b, h, n: (0, 0))],
        out_specs=[out_col, out_col,
                   pl.BlockSpec((1, 1, NC, HD, HD), lambda b, h, n: (b, h, n, 0, 0))],
        out_shape=[jax.ShapeDtypeStruct((T, HG_WIDTH), bf16), jax.ShapeDtypeStruct((T, HG_WIDTH), f32),
                   jax.ShapeDtypeStruct((B, HG_HEADS, nC, HD, HD), f32)],
        scratch_shapes=[pltpu.VMEM((HD, HD), f32), pltpu.VMEM((NC, CHUNK, HD), f32), pltpu.VMEM((NC, CHUNK, HD), f32)],
        semantics=("parallel", "parallel", "arbitrary"), ride=ride)


def _hgrn_bwd(proj, lb_param, g_head, o_pre, states, dcat, *, B, S, layer, ride=None):
    T = B * S
    TB = min(512, S)
    nT, NC = S // TB, TB // CHUNK
    HD = HG_HEAD_DIM

    def body(q_ref, f_ref, i_ref, z_ref, lb_ref, gh_ref, op_ref, st_ref, dc_ref,
             dq_ref, df_ref, di_ref, dz_ref, dlb_ref, dgh_ref,
             ds_scr, b_scr, q_scr, do_scr, wk_scr):
        @pl.when(pl.program_id(2) == 0)
        def _():
            ds_scr[...] = jnp.zeros_like(ds_scr)
            dlb_ref[...] = jnp.zeros_like(dlb_ref)
            dgh_ref[...] = jnp.zeros_like(dgh_ref)
        lb = _lower_bound(lb_ref[...], layer)
        lbf = jnp.maximum(lb, LB_FLOOR)
        ind = (lb > LB_FLOOR).astype(f32)
        gh = gh_ref[...]
        r_i = lax.broadcasted_iota(jnp.int32, (CHUNK, CHUNK), 0)
        c_i = lax.broadcasted_iota(jnp.int32, (CHUNK, CHUNK), 1)
        tril = (r_i >= c_i).astype(bf16)
        triu = (c_i >= r_i).astype(bf16)
        rows8 = lax.broadcasted_iota(jnp.int32, (8, HD), 0)
        row_c = lax.broadcasted_iota(jnp.int32, (CHUNK, HD), 0)
        lane_c = lax.broadcasted_iota(jnp.int32, (8, CHUNK), 1)
        last_row = row_c == CHUNK - 1
        masks = _level_masks()
        masks_t = _level_masks(transposed=True)
        seg_t = lax.broadcasted_iota(jnp.int32, (SUB, 8 * SUB), 0)
        seg_r = lax.broadcasted_iota(jnp.int32, (SUB, 8 * SUB), 1) // 8
        seg0 = (seg_r == seg_t).astype(bf16)
        seg1 = (seg_r[:, 0:4 * SUB] + 8 == seg_t[:, 0:4 * SUB]).astype(bf16)

        def chunk(c, carried):
            rs = slice(c * CHUNK, (c + 1) * CHUNK)
            b_s, q_s, do_s = b_scr.at[c], q_scr.at[c], do_scr.at[c]
            qr, fr = q_ref[rs, :], f_ref[rs, :]
            q, sq, sg, nsg, fg, logf, k = _gate_parts(qr, fr, lb, lbf)
            v = i_ref[rs, :]
            b = _seg_sum(tril, logf)
            o = op_ref[rs, :]
            dc = dc_ref[rs, :]
            zr = z_ref[rs, :]
            sz = _sigmoid(zr)
            rr = lax.rsqrt(jnp.mean(o * o, axis=-1, keepdims=True) + NORM_EPS)
            dz_ref[rs, :] = (dc * (o * rr * gh) * (sz * (1.0 + zr * (1.0 - sz)))).astype(bf16)
            dn = dc * (zr * sz)
            dgh_ref[0, 0] += jnp.sum(dn * o * rr, axis=0, keepdims=True)
            gdn = dn * gh
            d_o = rr * gdn - o * (rr * rr * rr) * jnp.mean(gdn * o, axis=-1, keepdims=True)
            b_s[...] = b
            q_s[...] = q
            do_s[...] = d_o
            dob = d_o.astype(bf16)
            vb16 = v.astype(bf16)
            d_a = _dot(dob, vb16, NT)
            yield
            d_q = jnp.zeros((CHUNK, HD), f32)
            d_k = jnp.zeros((CHUNK, HD), f32)
            at_all = jnp.zeros((CHUNK, CHUNK), f32)
            for m, mk, mkt in zip(LEVELS, masks, masks_t):
                beta = _level_anchor(b_s, row_c, m)
                eq = jnp.exp(jnp.minimum(b - beta, 0.0))
                ek = jnp.exp(jnp.minimum(beta - b, 0.0))
                qh = (q * eq).astype(bf16)
                kh = (k * ek).astype(bf16)
                at_all = at_all + jnp.where(mkt, _dot(kh, qh, NT), 0.0)
                d_aa = jnp.where(mk, d_a, 0.0).astype(bf16)
                d_q = d_q + _dot(d_aa, kh) * eq
                d_k = d_k + _dot(d_aa, qh, TN) * ek
            yield
            dq_blocks, dk_pieces, at_pieces = [], [], []
            for blk in range(CHUNK // SUB):
                r0 = blk * SUB
                wk = wk_scr.at[c * (CHUNK // SUB) + blk]
                bp = [b[r0 + 8 * i:r0 + 8 * i + 8] for i in range(SUB // 8)]
                kp = [k[r0 + 8 * i:r0 + 8 * i + 8] for i in range(SUB // 8)]
                vp = [v[r0 + 8 * i:r0 + 8 * i + 8] for i in range(SUB // 8)]
                dkp = [jnp.zeros((8, HD), f32) for _ in range(SUB // 8)]
                atp = [jnp.zeros((8, CHUNK), f32) for _ in range(SUB // 8)]
                for t in range(SUB):
                    bt = b_s[r0 + t:r0 + t + 1, :]
                    qt = q_s[r0 + t:r0 + t + 1, :]
                    dot_ = do_s[r0 + t:r0 + t + 1, :]
                    for i in range(t // 8 + 1):
                        diff = bt - bp[i]
                        if i == t // 8:
                            diff = jnp.where(rows8 <= t - 8 * i, diff, NEG_INF)
                        e = jnp.exp(diff)
                        a = jnp.sum(e * kp[i] * qt, axis=1, keepdims=True)
                        atp[i] = jnp.where(lane_c == r0 + t, a, atp[i])
                        w = jnp.sum(vp[i] * dot_, axis=1, keepdims=True) * e
                        dkp[i] = dkp[i] + w * qt
                        row = 8 * t if i == 0 else 8 * SUB + 8 * (t - 8)
                        wk[row:row + 8, :] = w * kp[i]
                dq_blk = _seg_sum(seg0, wk[0:8 * SUB, :])
                if SUB > 8:
                    dq_blk = dq_blk + _seg_sum(seg1, wk[8 * SUB:12 * SUB, :])
                dq_blocks.append(dq_blk)
                dk_pieces += dkp
                at_pieces += atp
                yield
            dst1 = carried[0]
            st0 = st_ref[0, 0, c]
            dst1b = dst1.astype(bf16)
            eb = jnp.exp(b)
            b_end = b_s[CHUNK - 1:CHUNK, :]
            edec = jnp.exp(b_end - b)
            e_end = jnp.exp(b_end)
            kdec = (k * edec).astype(bf16)
            qdec = (q * eb).astype(bf16)
            st1 = e_end * st0 + _dot(vb16, kdec, TN)
            rterm = jnp.sum(dst1 * st1, axis=0, keepdims=True)
            carried[0] = e_end * dst1 + _dot(dob, qdec, TN)
            d_q = d_q + _dot(dob, st0.astype(bf16)) * eb + jnp.concatenate(dq_blocks, axis=0)
            d_k = d_k + _dot(vb16, dst1b) * edec + jnp.concatenate(dk_pieces, axis=0)
            d_v = _dot(kdec, dst1b, NT) + _dot((at_all + jnp.concatenate(at_pieces, axis=0)).astype(bf16), dob)
            db = q * d_q - k * d_k + jnp.where(last_row, rterm, 0.0)
            dlt = _seg_sum(triu, db) - fg * d_k
            df_ref[rs, :] = (dlt * (1.0 - lb) * sg * nsg / fg).astype(bf16)
            dlb_ref[0] += jnp.sum(dlt * (ind - sg) / fg, axis=0, keepdims=True)
            dq_ref[rs, :] = (d_q * (sq * (1.0 + qr * (1.0 - sq)))).astype(bf16)
            di_ref[rs, :] = d_v.astype(bf16)

        carried = [ds_scr[...]]
        _interleaved([chunk(c, carried) for c in reversed(range(NC))])
        ds_scr[...] = carried[0]

    def col(part):
        return pl.BlockSpec((TB, HD), lambda b, h, n: (b * nT + nT - 1 - n, part * HG_HEADS + h))

    hcol = pl.BlockSpec((TB, HD), lambda b, h, n: (b * nT + nT - 1 - n, h))
    return _call(
        body, (proj, proj, proj, proj, lb_param, g_head, o_pre, states, dcat),
        name=f"hgrn_bwd_l{layer}", grid=(B, HG_HEADS, nT),
        in_specs=[col(0), col(1), col(2), col(3),
                  pl.BlockSpec((DEPTH, HD), lambda b, h, n: (0, h)),
                  pl.BlockSpec((1, HD), lambda b, h, n: (0, 0)),
                  hcol,
                  pl.BlockSpec((1, 1, NC, HD, HD), lambda b, h, n: (b, h, nT - 1 - n, 0, 0)),
                  hcol],
        out_specs=[hcol, hcol, hcol, hcol,
                   pl.BlockSpec((1, 1, HD), lambda b, h, n: (b, 0, h)),
                   pl.BlockSpec((1, 1, 1, HD), lambda b, h, n: (b, h, 0, 0))],
        out_shape=[jax.ShapeDtypeStruct((T, HG_WIDTH), bf16)] * 4 + [
            jax.ShapeDtypeStruct((B, 1, HG_WIDTH), f32), jax.ShapeDtypeStruct((B, HG_HEADS, 1, HD), f32)],
        scratch_shapes=[pltpu.VMEM((HD, HD), f32)] + [pltpu.VMEM((NC, CHUNK, HD), f32)] * 3
        + [pltpu.VMEM((NC * CHUNK // SUB, 12 * SUB, HD), f32)],
        semantics=("parallel", "parallel", "arbitrary"), ride=ride)


def _rope_tables(S):
    half = ATT_HEAD_DIM // 2
    inv_freq = ROPE_THETA ** (-jnp.arange(half, dtype=f32) / half)
    ang = jnp.arange(S, dtype=f32)[:, None] * inv_freq[None, :]
    cos, sin = jnp.cos(ang), jnp.sin(ang)
    return jnp.tile(jnp.concatenate([cos, cos], axis=1), (1, 2)), jnp.tile(jnp.concatenate([-sin, sin], axis=1), (1, 2))


def _swap_halves(x, first_half):
    return jnp.where(first_half, pltpu.roll(x, LANES - ATT_HEAD_DIM // 2, 1), pltpu.roll(x, ATT_HEAD_DIM // 2, 1))


def _rope(x, cos, sin, first_half):
    return x * cos + _swap_halves(x, first_half) * sin


def _rope_bwd(dy, cos, sin, first_half):
    return dy * cos + _swap_halves(dy * sin, first_half)


def _attn_consts(n):
    lane = lax.broadcasted_iota(jnp.int32, (1, LANES), 1)
    low = lane < ATT_HEAD_DIM
    first_half = (lane % ATT_HEAD_DIM) < ATT_HEAD_DIM // 2
    top = lax.broadcasted_iota(jnp.int32, (LANES, 1), 0) < ATT_HEAD_DIM
    s = lax.broadcasted_iota(jnp.int32, (2 * ATT_BLOCK, ATT_BLOCK), 0)
    t = lax.broadcasted_iota(jnp.int32, (2 * ATT_BLOCK, ATT_BLOCK), 1)
    mask = (s > t) & (s <= t + ATT_BLOCK) & ((s >= ATT_BLOCK) | (n > 0))
    return low, first_half, top, mask


def _dup_kv(x, low):
    rolled = pltpu.roll(x, ATT_HEAD_DIM, 1)
    return [jnp.where(low, x, rolled), jnp.where(low, rolled, x)]


def _attn_head(qtm, kd, vdt, sink, mask):
    s = jnp.where(mask, _dot(kd, qtm) * ATT_SCALE, NEG_INF)
    m = jnp.maximum(jnp.max(s, axis=0, keepdims=True), sink)
    p = jnp.exp(s - m)
    psink = jnp.exp(sink - m)
    inv = 1.0 / (jnp.sum(p, axis=0, keepdims=True) + psink)
    pn = p * inv
    return pn, psink * inv, _dot(vdt, pn.astype(bf16))


def _swa_fwd(proj, sink_b, cos, sin, *, B, S, ride=None):
    T = B * S
    L = ATT_BLOCK
    nB = S // L

    def body(q_ref, z_ref, kvc_ref, kvp_ref, sk_ref, cc_ref, sc_ref, cp_ref, sp_ref, cat_ref):
        n = pl.program_id(1)
        low, first_half, top, mask = _attn_consts(n)
        cc, sc = cc_ref[...], sc_ref[...]
        kc = _rope(kvc_ref[:, 0:LANES], cc, sc, first_half)
        kp = _rope(kvp_ref[:, 0:LANES], cp_ref[...], sp_ref[...], first_half)
        kd = [x.astype(bf16) for x in _dup_kv(jnp.concatenate([kp, kc], axis=0), low)]
        vdt = [x.T.astype(bf16) for x in _dup_kv(jnp.concatenate([kvp_ref[:, LANES:2 * LANES], kvc_ref[:, LANES:2 * LANES]], axis=0), low)]
        def head_pair(pair):
            cols = slice(pair * LANES, (pair + 1) * LANES)
            j = (2 * pair) // ATT_GROUP
            qt = _rope(q_ref[:, cols], cc, sc, first_half).T
            yield
            outs = []
            for hh in range(2):
                h = 2 * pair + hh
                qtm = jnp.where(top if hh == 0 else ~top, qt, 0.0).astype(bf16)
                outs.append(_attn_head(qtm, kd[j], vdt[j], sk_ref[h:h + 1, 0:1], mask)[2])
                yield
            zp = z_ref[:, cols]
            cat_ref[:, cols] = (jnp.where(top, outs[0], outs[1]).T * (zp * _sigmoid(zp))).astype(bf16)

        _interleaved([head_pair(p) for p in range(ATT_HEADS // 2)], SWA_INTERLEAVE)

    cur = lambda b, n: (b * nB + n, 0)
    return _call(
        body, (proj, proj, proj, proj, sink_b, cos, sin, cos, sin), name="swa_fwd", grid=(B, nB),
        in_specs=[pl.BlockSpec((L, ATT_WIDTH), lambda b, n: (b * nB + n, QA_BLK)),
                  pl.BlockSpec((L, ATT_WIDTH), lambda b, n: (b * nB + n, ZA_BLK)),
                  pl.BlockSpec((L, 2 * KV_WIDTH), lambda b, n: (b * nB + n, KV_BLK)),
                  pl.BlockSpec((L, 2 * KV_WIDTH), lambda b, n: (b * nB + jnp.maximum(n - 1, 0), KV_BLK)),
                  pl.BlockSpec((ATT_HEADS, LANES), lambda b, n: (0, 0)),
                  pl.BlockSpec((L, LANES), lambda b, n: (n, 0)), pl.BlockSpec((L, LANES), lambda b, n: (n, 0)),
                  pl.BlockSpec((L, LANES), lambda b, n: (jnp.maximum(n - 1, 0), 0)),
                  pl.BlockSpec((L, LANES), lambda b, n: (jnp.maximum(n - 1, 0), 0))],
        out_specs=[pl.BlockSpec((L, ATT_WIDTH), cur)],
        out_shape=[jax.ShapeDtypeStruct((T, ATT_WIDTH), bf16)],
        semantics=("parallel", "parallel"), ride=ride)


def _swa_bwd(proj, sink_b, cos, sin, dcat, *, B, S, ride=None):
    T = B * S
    L = ATT_BLOCK
    nB = S // L

    def body(q_ref, z_ref, kvc_ref, kvp_ref, sk_ref, cc_ref, sc_ref, cp_ref, sp_ref, dc_ref,
             dq_ref, dz_ref, dkv_ref, dsk_ref, carry, ds_st, pn_st, q_st, do_st):
        step = pl.program_id(1)
        n = nB - 1 - step

        @pl.when((pl.program_id(0) == 0) & (step == 0))
        def _():
            dsk_ref[...] = jnp.zeros_like(dsk_ref)

        @pl.when(step == 0)
        def _():
            carry[...] = jnp.zeros_like(carry)
        low, first_half, top, mask = _attn_consts(n)
        cc, sc, cp, sp = cc_ref[...], sc_ref[...], cp_ref[...], sp_ref[...]
        kc = _rope(kvc_ref[:, 0:LANES], cc, sc, first_half)
        kp = _rope(kvp_ref[:, 0:LANES], cp, sp, first_half)
        kdf = _dup_kv(jnp.concatenate([kp, kc], axis=0), low)
        vdf = _dup_kv(jnp.concatenate([kvp_ref[:, LANES:2 * LANES], kvc_ref[:, LANES:2 * LANES]], axis=0), low)
        kd = [x.astype(bf16) for x in kdf]
        vd = [x.astype(bf16) for x in vdf]
        kdt = [x.T.astype(bf16) for x in kdf]
        vdt = [x.T.astype(bf16) for x in vdf]
        dkd, dvd = [], []
        def head_pair(pair):
            cols = slice(pair * LANES, (pair + 1) * LANES)
            j = (2 * pair) // ATT_GROUP
            qp = _rope(q_ref[:, cols], cc, sc, first_half)
            qt = qp.T
            zp = z_ref[:, cols]
            dc = dc_ref[:, cols]
            sz = _sigmoid(zp)
            d_o = dc * (zp * sz)
            dot_ = d_o.T
            yield
            res = []
            for hh in range(2):
                rsel = top if hh == 0 else ~top
                qtm = jnp.where(rsel, qt, 0.0).astype(bf16)
                pn, psn, o = _attn_head(qtm, kd[j], vdt[j], sk_ref[2 * pair + hh:2 * pair + hh + 1, 0:1], mask)
                res.append((rsel, pn, psn, o))
                yield
            ot = jnp.where(top, res[0][3], res[1][3])
            dz_ref[:, cols] = (dc * ot.T * (sz * (1.0 + zp * (1.0 - sz)))).astype(bf16)
            dqts = []
            for hh in range(2):
                h = 2 * pair + hh
                rsel, pn, psn, _ = res[hh]
                lsel = low if hh == 0 else ~low
                dotm = jnp.where(rsel, dot_, 0.0)
                delta = jnp.sum(dotm * ot, axis=0, keepdims=True)
                dst = (pn * (_dot(vd[j], dotm.astype(bf16)) - delta) * ATT_SCALE).astype(bf16)
                dsk_ref[h:h + 1, :] += jnp.zeros((1, LANES), f32) - jnp.sum(psn * delta)
                dqts.append(_dot(kdt[j], dst))
                g = h % ATT_GROUP
                ds_st[:, g * LANES:(g + 1) * LANES] = dst
                pn_st[:, g * LANES:(g + 1) * LANES] = pn.astype(bf16)
                q_st[g * LANES:(g + 1) * LANES, :] = jnp.where(lsel, qp, 0.0).astype(bf16)
                do_st[g * LANES:(g + 1) * LANES, :] = jnp.where(lsel, d_o, 0.0).astype(bf16)
                yield
            dq_ref[:, cols] = _rope_bwd(jnp.where(top, dqts[0], dqts[1]).T, cc, sc, first_half).astype(bf16)

        pairs_per_group = ATT_GROUP // 2
        for grp in range(ATT_HEADS // ATT_GROUP):
            _interleaved([head_pair(grp * pairs_per_group + p) for p in range(pairs_per_group)], SWA_INTERLEAVE)
            dkd.append(_dot(ds_st[...], q_st[...]))
            dvd.append(_dot(pn_st[...], do_st[...]))
        dk = [x + pltpu.roll(x, ATT_HEAD_DIM, 1) for x in dkd]
        dv = [x + pltpu.roll(x, ATT_HEAD_DIM, 1) for x in dvd]
        dk = jnp.where(low, dk[0], dk[1])
        dv = jnp.where(low, dv[0], dv[1])
        dkv_ref[:, 0:LANES] = (_rope_bwd(dk[L:2 * L], cc, sc, first_half) + carry[:, 0:LANES]).astype(bf16)
        dkv_ref[:, LANES:2 * LANES] = (dv[L:2 * L] + carry[:, LANES:2 * LANES]).astype(bf16)
        carry[:, 0:LANES] = _rope_bwd(dk[0:L], cp, sp, first_half)
        carry[:, LANES:2 * LANES] = dv[0:L]

    rev = lambda b, s: b * nB + nB - 1 - s
    revp = lambda b, s: b * nB + jnp.maximum(nB - 2 - s, 0)
    wide = lambda blk: pl.BlockSpec((L, ATT_WIDTH), lambda b, s: (rev(b, s), blk))
    tab = pl.BlockSpec((L, LANES), lambda b, s: (nB - 1 - s, 0))
    tabp = pl.BlockSpec((L, LANES), lambda b, s: (jnp.maximum(nB - 2 - s, 0), 0))
    return _call(
        body, (proj, proj, proj, proj, sink_b, cos, sin, cos, sin, dcat), name="swa_bwd", grid=(B, nB),
        in_specs=[wide(QA_BLK), wide(ZA_BLK),
                  pl.BlockSpec((L, 2 * KV_WIDTH), lambda b, s: (rev(b, s), KV_BLK)),
                  pl.BlockSpec((L, 2 * KV_WIDTH), lambda b, s: (revp(b, s), KV_BLK)),
                  pl.BlockSpec((ATT_HEADS, LANES), lambda b, s: (0, 0)),
                  tab, tab, tabp, tabp, wide(0)],
        out_specs=[wide(0), wide(0), pl.BlockSpec((L, 2 * KV_WIDTH), lambda b, s: (rev(b, s), 0)),
                   pl.BlockSpec((ATT_HEADS, LANES), lambda b, s: (0, 0))],
        out_shape=[jax.ShapeDtypeStruct((T, ATT_WIDTH), bf16), jax.ShapeDtypeStruct((T, ATT_WIDTH), bf16),
                   jax.ShapeDtypeStruct((T, 2 * KV_WIDTH), bf16), jax.ShapeDtypeStruct((ATT_HEADS, LANES), f32)],
        scratch_shapes=[pltpu.VMEM((L, 2 * KV_WIDTH), f32),
                        pltpu.VMEM((2 * L, ATT_GROUP * LANES), bf16), pltpu.VMEM((2 * L, ATT_GROUP * LANES), bf16),
                        pltpu.VMEM((ATT_GROUP * LANES, LANES), bf16), pltpu.VMEM((ATT_GROUP * LANES, LANES), bf16)],
        semantics=("arbitrary", "arbitrary"), ride=ride)


def _train_step(x, target, bufs, g_pre, g_post, lb_param, g_head, sinks, *, B, S, exchange):
    L = DEPTH
    T = x.shape[0]
    ri, ro = IN_WIDTH // 8, MIX_WIDTH // 8
    cos, sin = _rope_tables(S)
    full = list(bufs)
    if exchange:
        full[0] = _run_exchange(_gather_d2d(_run_exchange(_gather_ici(bufs[0]))))
    saved = []
    for l in range(L):
        wt = full[l][0].reshape(1, IN_WIDTH, D_MODEL)
        wo = full[l][1].reshape(1, MIX_WIDTH, D_MODEL)
        tail = jnp.concatenate([wt[:, 5376:6400], wt[:, 5120:5376]], axis=1)
        proj, h = _in_proj(x, g_pre[l:l + 1], wt, tail, 0)
        ahead = exchange and l + 1 < L
        (ch, o_pre, states), landed = _hgrn_fwd(proj, lb_param, g_head[l:l + 1], B=B, S=S, layer=l,
                                                ride=_gather_ici(bufs[l + 1]) if ahead else None)
        sink_b = jnp.broadcast_to(sinks[l][:, None], (ATT_HEADS, LANES))
        (ca,), passed = _swa_fwd(proj, sink_b, cos, sin, B=B, S=S, ride=_gather_d2d(landed) if ahead else None)
        if ahead:
            full[l + 1] = passed
        xn, y = _out_proj(ch, ca, wo, 0, x, g_post[l:l + 1])
        saved.append((x, proj, h, ch, o_pre, states, sink_b, ca, y, wt, tail, wo))
        x = xn
    dx, loss = _loss_head(x, target)

    def reduce_tail(sums, recv):
        return _run_exchange(_pair_share([_chip_sum(s, r) for s, r in zip(sums, recv)]))

    grads = [None] * L
    waiting = None
    gg_pre, gg_post, g_lb, gg_head, g_sinks = [], [], [], [], []
    for l in reversed(range(L)):
        x_in, proj, h, ch, o_pre, states, sink_b, ca, y, wt, tail, wo = saved[l]
        (dch, dca, dwo, dgpost), got = _out_proj_bwd(dx, y, g_post[l:l + 1], wo, 0, ch, ca,
                                                     ride=_pair_exchange(waiting) if waiting else None)
        sums = [_pair_add(p, r) for p, r in zip(waiting, got)] if waiting else None
        (dq, df, di, dz, dlb, dgh), recv = _hgrn_bwd(proj, lb_param, g_head[l:l + 1], o_pre, states, dch, B=B, S=S,
                                                     layer=l, ride=_chip_exchange(sums) if waiting else None)
        if waiting:
            grads[l + 1] = reduce_tail(sums, recv)
        at_end = exchange and l == 0
        part_o = [dwo.reshape(1, 4, 2, ro, D_MODEL)]
        (dqa, dza, dkv, dsk), got_o = _swa_bwd(proj, sink_b, cos, sin, dca, B=B, S=S,
                                              ride=_pair_exchange(part_o) if at_end else None)
        pieces = [dq, df, di, dz, dqa, dkv, dza]
        sums_o = [_pair_add(part_o[0], got_o[0])] if at_end else None
        (gwt,), recv_o = _grad_w_in(h, pieces, ride=_chip_exchange(sums_o) if at_end else None)
        part_t = [gwt.reshape(1, 4, 2, ri, D_MODEL)]
        if at_end:
            tm = min(512, T // 2)
            nb = T // tm
            na = max(1, nb // 4)
            (dx_a, dg_a), got_t = _in_proj_bwd(pieces, wt, tail, 0, x_in, g_pre[l:l + 1], dx, tm=tm, blocks=(0, na),
                                               ride=_pair_exchange(part_t))
            sums_t = [_pair_add(part_t[0], got_t[0])]
            (dx, dg_b), recv_t = _in_proj_bwd(pieces, wt, tail, 0, x_in, g_pre[l:l + 1], dx, tm=tm, blocks=(na, nb - na),
                                              dx_into=dx_a, ride=_chip_exchange(sums_t))
            dgpre = dg_a + dg_b
            grads[0] = reduce_tail(sums_t + sums_o, recv_t + recv_o)
        else:
            (dx, dgpre), _ = _in_proj_bwd(pieces, wt, tail, 0, x_in, g_pre[l:l + 1], dx)
            if exchange:
                waiting = part_t + part_o
            else:
                grads[l] = [gwt, dwo]
        gg_pre.append(dgpre[0])
        gg_post.append(dgpost[0])
        g_lb.append(jnp.sum(dlb, axis=(0, 1)))
        gg_head.append(jnp.sum(dgh, axis=(0, 1, 2)))
        g_sinks.append(dsk[:, 0])
    rev = lambda xs: jnp.stack(xs[::-1])
    return loss[0, 0], dx, grads, rev(gg_pre), rev(gg_post), rev(g_lb), rev(gg_head), rev(g_sinks)


MESH = pl.DeviceIdType.MESH
ANY = pl.BlockSpec(memory_space=pl.ANY)


def _place():
    x, y, c = lax.axis_index("x"), lax.axis_index("y"), lax.axis_index("c")
    return x, y, c, [(1 - x, y), (x, 1 - y), (1 - x, 1 - y)]


def _rcopy(src, dst, send, recv, k, to):
    return pltpu.make_async_remote_copy(src_ref=src, dst_ref=dst, send_sem=send.at[k], recv_sem=recv.at[k],
                                        device_id=to, device_id_type=MESH)


class _Exchange:
    def __init__(self, name, inputs, out_shapes, n_sems, plan, in_place=False):
        self.name, self.inputs, self.out_shapes, self.n_sems, self.plan = name, inputs, out_shapes, n_sems, plan
        self.aliases = {a: a for a in range(len(inputs))} if in_place else {}

    def start(self, ins, outs, send, recv):
        for cp in self.plan(ins, outs, send, recv)[0]:
            cp.start()

    def finish(self, ins, outs, send, recv):
        sent, arriving = self.plan(ins, outs, send, recv)
        for cp in arriving:
            cp.wait_recv()
        for cp in sent:
            cp.wait_send()

    def sems(self):
        return [pltpu.SemaphoreType.DMA((self.n_sems,)), pltpu.SemaphoreType.DMA((self.n_sems,))]


def _run_exchange(ex):
    n_in, n_out = len(ex.inputs), len(ex.out_shapes)

    def body(*refs):
        ins, outs = refs[:n_in], refs[n_in:n_in + n_out]
        send, recv = refs[n_in + n_out:]
        ex.start(ins, outs, send, recv)
        ex.finish(ins, outs, send, recv)

    return pl.pallas_call(
        body, name=ex.name, in_specs=[ANY] * n_in, out_specs=[ANY] * n_out, out_shape=ex.out_shapes,
        input_output_aliases=ex.aliases, scratch_shapes=ex.sems(),
    )(*ex.inputs)


def _call(body, operands, *, name, grid, in_specs, out_specs, out_shape, scratch_shapes=(), semantics, ride=None,
          aliases=None):
    aliases = dict(aliases or {})
    if ride is None:
        outs = pl.pallas_call(body, name=name, grid=grid, in_specs=in_specs, out_specs=out_specs, out_shape=out_shape,
                              input_output_aliases=aliases, scratch_shapes=list(scratch_shapes),
                              compiler_params=_params(*semantics))(*operands)
        return outs, []
    n_in, n_out, n_scr = len(in_specs), len(out_specs), len(scratch_shapes)
    r_in, r_out = len(ride.inputs), len(ride.out_shapes)

    def riding(*refs):
        refs = list(refs)
        ins, rins = refs[:n_in], refs[n_in:n_in + r_in]
        o0 = n_in + r_in
        outs, routs = refs[o0:o0 + n_out], refs[o0 + n_out:o0 + n_out + r_out]
        scr = refs[o0 + n_out + r_out:o0 + n_out + r_out + n_scr]
        send, recv = refs[-2:]
        ids = [pl.program_id(d) for d in range(len(grid))]
        first = functools.reduce(jnp.logical_and, [i == 0 for i in ids])
        last = functools.reduce(jnp.logical_and, [i == g - 1 for i, g in zip(ids, grid)])
        pl.when(first)(lambda: ride.start(rins, routs, send, recv))
        body(*ins, *outs, *scr)
        pl.when(last)(lambda: ride.finish(rins, routs, send, recv))

    res = pl.pallas_call(
        riding, name=name + "_" + ride.name, grid=grid,
        in_specs=list(in_specs) + [ANY] * r_in, out_specs=list(out_specs) + [ANY] * r_out,
        out_shape=list(out_shape) + list(ride.out_shapes),
        input_output_aliases={**aliases, **{n_in + a: n_out + b for a, b in ride.aliases.items()}},
        scratch_shapes=list(scratch_shapes) + ride.sems(),
        compiler_params=_params(*(["arbitrary"] * len(grid))),
    )(*operands, *ride.inputs)
    return res[:n_out], res[n_out:]


def _gather_ici(bufs, name="gather_ici"):
    n = len(bufs)

    def plan(ins, outs, send, recv):
        x, y, c, chips = _place()
        me = 2 * x + y
        sent, arriving = [], []
        for j, (px, py) in enumerate(chips):
            for a in range(n):
                mine, theirs = outs[a].at[:, me, c], outs[a].at[:, 2 * px + py, c]
                sent.append(_rcopy(mine, mine, send, recv, j * n + a, (px, py, c)))
                arriving.append(_rcopy(theirs, theirs, send, recv, j * n + a, (px, py, c)))
        return sent, arriving

    return _Exchange(name, bufs, [jax.ShapeDtypeStruct(b.shape, b.dtype) for b in bufs], 3 * n, plan, in_place=True)


def _gather_d2d(bufs, name="gather_d2d"):
    n = len(bufs)

    def plan(ins, outs, send, recv):
        x, y, c, chips = _place()
        sib = (x, y, 1 - c)
        sent, arriving = [], []
        for j, (px, py) in enumerate(chips):
            for a in range(n):
                got, theirs = outs[a].at[:, 2 * px + py, c], outs[a].at[:, 2 * px + py, 1 - c]
                sent.append(_rcopy(got, got, send, recv, j * n + a, sib))
                arriving.append(_rcopy(theirs, theirs, send, recv, j * n + a, sib))
        return sent, arriving

    return _Exchange(name, bufs, [jax.ShapeDtypeStruct(b.shape, b.dtype) for b in bufs], 3 * n, plan, in_place=True)


def _pair_exchange(parts):
    n = len(parts)

    def plan(ins, outs, send, recv):
        x, y, c, _ = _place()
        cps = [_rcopy(ins[a].at[:, :, 1 - c], outs[a], send, recv, a, (x, y, 1 - c)) for a in range(n)]
        return cps, cps

    return _Exchange("pair_exchange", parts,
                     [jax.ShapeDtypeStruct(p.shape[:2] + p.shape[3:], p.dtype) for p in parts], n, plan)


def _block_rows(r):
    return r if r <= 512 else r // 2


def _pair_add(part, got):
    L, K, _, r, C = part.shape
    rows = _block_rows(r)

    def body(c_ref, a_ref, b_ref, o_ref):
        o_ref[0, 0] = (a_ref[0, 0, 0] + b_ref[0, 0]).astype(bf16)

    blk = (1, 1, rows, C)
    return pl.pallas_call(
        body, name="pair_add",
        grid_spec=pltpu.PrefetchScalarGridSpec(
            num_scalar_prefetch=1, grid=(L, K, r // rows),
            in_specs=[pl.BlockSpec((1, 1, 1, rows, C), lambda l, k, i, c: (l, k, c[0], i, 0)),
                      pl.BlockSpec(blk, lambda l, k, i, c: (l, k, i, 0))],
            out_specs=pl.BlockSpec(blk, lambda l, k, i, c: (l, k, i, 0))),
        out_shape=jax.ShapeDtypeStruct((L, K, r, C), bf16),
        compiler_params=_params("parallel", "parallel", "parallel"),
    )(jnp.reshape(lax.axis_index("c"), (1,)).astype(jnp.int32), part, got)


def _chip_exchange(sums):
    n = len(sums)

    def plan(ins, outs, send, recv):
        x, y, c, chips = _place()
        cps = []
        for j, (px, py) in enumerate(chips):
            for a in range(n):
                cps.append(_rcopy(ins[a].at[:, 2 * px + py], outs[a].at[j], send, recv, j * n + a, (px, py, c)))
        return cps, cps

    return _Exchange("chip_exchange", sums,
                     [jax.ShapeDtypeStruct((3, s.shape[0]) + s.shape[2:], s.dtype) for s in sums], 3 * n, plan)


def _chip_sum(mine, got):
    L, K, r, C = mine.shape
    rows = _block_rows(r)

    def body(p_ref, a_ref, b_ref, o_ref):
        o_ref[0, 0] = (a_ref[0, 0].astype(f32) + b_ref[0, 0].astype(f32)) + (b_ref[1, 0].astype(f32) + b_ref[2, 0].astype(f32))

    place = jnp.stack([2 * lax.axis_index("x") + lax.axis_index("y"), lax.axis_index("c")]).astype(jnp.int32)
    return pl.pallas_call(
        body, name="chip_sum",
        grid_spec=pltpu.PrefetchScalarGridSpec(
            num_scalar_prefetch=1, grid=(L, r // rows),
            in_specs=[pl.BlockSpec((1, 1, rows, C), lambda l, i, p: (l, p[0], i, 0)),
                      pl.BlockSpec((3, 1, rows, C), lambda l, i, p: (0, l, i, 0))],
            out_specs=pl.BlockSpec((1, 1, rows, C), lambda l, i, p: (l, p[1], i, 0))),
        out_shape=jax.ShapeDtypeStruct((L, 2, r, C), f32),
        compiler_params=_params("parallel", "parallel"),
    )(place, mine, got)


def _pair_share(bufs):
    n = len(bufs)

    def plan(ins, outs, send, recv):
        x, y, c, _ = _place()
        sib = (x, y, 1 - c)
        sent = [_rcopy(outs[a].at[:, c], outs[a].at[:, c], send, recv, a, sib) for a in range(n)]
        arriving = [_rcopy(outs[a].at[:, 1 - c], outs[a].at[:, 1 - c], send, recv, a, sib) for a in range(n)]
        return sent, arriving

    return _Exchange("pair_share", bufs, [jax.ShapeDtypeStruct(b.shape, b.dtype) for b in bufs], n, plan, in_place=True)


def _all_sum_small(v):
    def body(v_ref, o_ref, buf, send, recv):
        x, y, c, _ = _place()
        me = 4 * x + 2 * y + c
        buf[me] = v_ref[...]
        cps = []
        for m in range(1, 8):
            to = (x ^ (m >> 2), y ^ ((m >> 1) & 1), c ^ (m & 1))
            cps.append(_rcopy(v_ref, buf.at[me], send, recv, m - 1, to))
        for cp in cps:
            cp.start()
        for cp in cps:
            cp.wait()
        acc = buf[0]
        for d in range(1, 8):
            acc = acc + buf[d]
        o_ref[...] = acc

    vm = pl.BlockSpec(memory_space=pltpu.VMEM)
    return pl.pallas_call(
        body, name="all_sum_small", in_specs=[vm], out_specs=vm,
        out_shape=jax.ShapeDtypeStruct(v.shape, v.dtype),
        scratch_shapes=[pltpu.VMEM((8,) + v.shape, v.dtype), pltpu.SemaphoreType.DMA((7,)), pltpu.SemaphoreType.DMA((7,))],
    )(v)


def _adamw_math(w, g, m, v):
    m = ADAM_B1 * m + (1.0 - ADAM_B1) * g
    v = ADAM_B2 * v + (1.0 - ADAM_B2) * (g * g)
    m_hat = m / (1.0 - ADAM_B1 ** ADAM_STEP)
    v_hat = v / (1.0 - ADAM_B2 ** ADAM_STEP)
    return -ADAM_LR * (m_hat / (jnp.sqrt(v_hat) + ADAM_EPS) + ADAM_WD * w), m, v


def _adamw(w, g, m, v):
    L, R, C = w.shape
    rows = R // 4

    def body(w_ref, g_ref, m_ref, v_ref, d_ref, mo_ref, vo_ref):
        d_ref[...], mo_ref[...], vo_ref[...] = _adamw_math(w_ref[...], g_ref[...], m_ref[...], v_ref[...])

    blk = pl.BlockSpec((1, rows, C), lambda l, i: (l, i, 0))
    return pl.pallas_call(
        body, name="adamw", grid=(L, R // rows), in_specs=[blk] * 4, out_specs=[blk] * 3,
        out_shape=[jax.ShapeDtypeStruct(w.shape, f32)] * 3,
        compiler_params=_params("parallel", "parallel"),
    )(w, g, m, v)


def _chip_index():
    return jnp.reshape(2 * lax.axis_index("x") + lax.axis_index("y"), (1,)).astype(jnp.int32)


def _shard_placed(w, l):
    _, R, C = w.shape
    rows = R // 4

    def body(k_ref, w_ref, o_ref):
        o_ref[0, 0] = w_ref[0].astype(bf16)

    return pl.pallas_call(
        body, name="shard_placed",
        grid_spec=pltpu.PrefetchScalarGridSpec(
            num_scalar_prefetch=1, grid=(R // rows,),
            in_specs=[pl.BlockSpec((1, rows, C), lambda i, k: (l, i, 0))],
            out_specs=pl.BlockSpec((1, 1, rows, C), lambda i, k: (0, k[0], i, 0))),
        out_shape=jax.ShapeDtypeStruct((1, 4, R, C), bf16),
        compiler_params=_params("parallel"),
    )(_chip_index(), w)


SMALL_ROWS = 4 * DEPTH


def _pack_small(g_pre, g_post, lb, g_head, sinks, loss=None):
    rows = []
    for l in range(DEPTH):
        tail = [g_head[l], sinks[l]]
        if loss is not None and l == 0:
            tail.append(jnp.reshape(loss, (1,)))
        tail = jnp.concatenate(tail)
        rows += [g_pre[l], g_post[l], lb[l], jnp.pad(tail, (0, D_MODEL - tail.shape[0]))]
    return jnp.stack(rows)


def _unpack_small(p):
    g_pre = jnp.stack([p[4 * l] for l in range(DEPTH)])
    g_post = jnp.stack([p[4 * l + 1] for l in range(DEPTH)])
    lb = jnp.stack([p[4 * l + 2] for l in range(DEPTH)])
    g_head = jnp.stack([p[4 * l + 3, :HG_HEAD_DIM] for l in range(DEPTH)])
    sinks = jnp.stack([p[4 * l + 3, HG_HEAD_DIM:HG_HEAD_DIM + ATT_HEADS] for l in range(DEPTH)])
    return g_pre, g_post, lb, g_head, sinks


def _small_update(gsum, w, m, v):
    def body(g_ref, w_ref, m_ref, v_ref, go_ref, d_ref, mo_ref, vo_ref):
        g = g_ref[...]
        w = w_ref[...]
        lbp = [w[4 * l + 2:4 * l + 3] for l in range(DEPTH)]
        mx = functools.reduce(jnp.maximum, lbp)
        e = [jnp.exp(t - mx) for t in lbp]
        tot = functools.reduce(jnp.add, e)
        p = [t / tot for t in e]
        glb = [g[4 * l + 2:4 * l + 3] for l in range(DEPTH)]
        row = lax.broadcasted_iota(jnp.int32, g.shape, 0)
        for j in range(DEPTH):
            gj = jnp.zeros_like(p[0])
            for l in range(DEPTH):
                for i in range(1, l + 1):
                    gj = gj + glb[l] * p[i] * ((1.0 if i == j else 0.0) - p[j])
            g = jnp.where(row == 4 * j + 2, gj, g)
        go_ref[...] = g
        d_ref[...], mo_ref[...], vo_ref[...] = _adamw_math(w, g, m_ref[...], v_ref[...])

    vm = pl.BlockSpec(memory_space=pltpu.VMEM)
    return pl.pallas_call(
        body, name="small_update", in_specs=[vm] * 4, out_specs=[vm] * 4,
        out_shape=[jax.ShapeDtypeStruct(gsum.shape, f32)] * 4,
    )(gsum, w, m, v)


def kernel(x, w_in, w_out, g_pre, g_post, lb_param, g_head, sinks, loss_target, m_w_in, m_w_out, m_g_pre, m_g_post, m_lb_param, m_g_head, m_sinks, v_w_in, v_w_out, v_g_pre, v_g_post, v_lb_param, v_g_head, v_sinks):
    B, S, _ = x.shape
    T = B * S
    L = DEPTH
    ri, ro = IN_WIDTH // 8, MIX_WIDTH // 8
    tr = lambda a: jnp.transpose(a, (0, 2, 1))
    wt, mt, vt = tr(w_in), tr(m_w_in), tr(v_w_in)
    bufs = [[_shard_placed(wt, l).reshape(1, 4, 2, ri, D_MODEL), _shard_placed(w_out, l).reshape(1, 4, 2, ro, D_MODEL)]
            for l in range(L)]
    loss, dx, grads, ggpre, ggpost, glb, gghead, gsinks = _train_step(
        x.reshape(T, D_MODEL), loss_target.reshape(T, D_MODEL), bufs, g_pre, g_post, lb_param, g_head, sinks,
        B=B, S=S, exchange=True)
    gwt_mine = jnp.concatenate([g[0] for g in grads], axis=0).reshape(L, 2 * ri, D_MODEL)
    grad_w_out = jnp.concatenate([g[1] for g in grads], axis=0).reshape(L, 2 * ro, D_MODEL)

    d_wt, nm_wt, nv_wt = _adamw(wt, gwt_mine, mt, vt)
    grad_w_in, d_w_in, nm_w_in, nv_w_in = tr(gwt_mine), tr(d_wt), tr(nm_wt), tr(nv_wt)
    d_w_out, nm_w_out, nv_w_out = _adamw(w_out, grad_w_out, m_w_out, v_w_out)

    gsum = _all_sum_small(_pack_small(ggpre, ggpost, glb, gghead, gsinks, loss))
    gs, ds, ms, vs = _small_update(
        gsum, _pack_small(g_pre, g_post, lb_param, g_head, sinks),
        _pack_small(m_g_pre, m_g_post, m_lb_param, m_g_head, m_sinks),
        _pack_small(v_g_pre, v_g_post, v_lb_param, v_g_head, v_sinks))
    loss_all = gsum[3, HG_HEAD_DIM + ATT_HEADS]
    return (loss_all, dx.reshape(B, S, D_MODEL), grad_w_in, grad_w_out, *_unpack_small(gs),
            d_w_in, d_w_out, *_unpack_small(ds), nm_w_in, nm_w_out, *_unpack_small(ms),
            nv_w_in, nv_w_out, *_unpack_small(vs))
```

```python
import functools
import math

import jax
import jax.numpy as jnp
from jax import lax
from jax.experimental import pallas as pl
from jax.experimental.pallas import tpu as pltpu

f32 = jnp.float32
bf16 = jnp.bfloat16

D_MODEL = 1024
DEPTH = 2
HG_WIDTH = 1024
HG_HEAD_DIM = 128
HG_HEADS = 8
CHUNK = 64
SUB = 16
ATT_WIDTH = 1024
ATT_HEAD_DIM = 64
ATT_HEADS = 16
ATT_GROUP = 8
KV_WIDTH = 128
ATT_BLOCK = 128
ATT_SCALE = 1.0 / math.sqrt(ATT_HEAD_DIM)
ROPE_THETA = 10000.0
IN_WIDTH = 6400
MIX_WIDTH = 2048
NORM_EPS = 1e-6
NEG_INF = -1e30
LB_FLOOR = 1e-20
LANES = 128
VMEM_LIMIT = 48 * 1024 * 1024

ADAM_LR = 0.001
ADAM_B1 = 0.9
ADAM_B2 = 0.999
ADAM_EPS = 1e-08
ADAM_WD = 0.01
ADAM_STEP = 10

QA_BLK, ZA_BLK, KV_BLK = 4, 5, 24

NT = (((1,), (1,)), ((), ()))
TN = (((0,), (0,)), ((), ()))


def _dot(a, b, dims=None, precision=None):
    if dims is None:
        return jnp.dot(a, b, preferred_element_type=f32, precision=precision)
    return lax.dot_general(a, b, dims, preferred_element_type=f32, precision=precision)


def _sigmoid(x):
    return 1.0 / (1.0 + jnp.exp(-x))


def _params(*sem):
    return pltpu.CompilerParams(dimension_semantics=sem, vmem_limit_bytes=VMEM_LIMIT)


TAIL = IN_WIDTH - 5120


def _in_proj(x, g, wt, tail, l, *, tm=1024):
    T = x.shape[0]
    tm = min(tm, T)
    nmain = 5120 // TAIL

    def body(x_ref, g_ref, w_ref, t_ref, p_ref, h_ref, hs):
        j = pl.program_id(1)

        @pl.when(j == 0)
        def _():
            xv = x_ref[...]
            r = lax.rsqrt(jnp.mean(xv * xv, axis=-1, keepdims=True) + NORM_EPS)
            hv = (xv * r * g_ref[...]).astype(bf16)
            hs[...] = hv
            h_ref[...] = hv

        @pl.when(j < nmain)
        def _():
            p_ref[...] = _dot(hs[...], w_ref[...], NT)

        @pl.when(j == nmain)
        def _():
            p_ref[...] = _dot(hs[...], t_ref[...], NT)

    return pl.pallas_call(
        body, name="in_proj", grid=(T // tm, nmain + 1),
        in_specs=[pl.BlockSpec((tm, D_MODEL), lambda i, j: (i, 0)),
                  pl.BlockSpec((1, D_MODEL), lambda i, j: (0, 0)),
                  pl.BlockSpec((None, TAIL, D_MODEL), lambda i, j: (l, jnp.minimum(j, nmain - 1), 0)),
                  pl.BlockSpec((None, TAIL, D_MODEL), lambda i, j: (l, 0, 0))],
        out_specs=[pl.BlockSpec((tm, TAIL), lambda i, j: (i, j)),
                   pl.BlockSpec((tm, D_MODEL), lambda i, j: (i, 0))],
        out_shape=[jax.ShapeDtypeStruct((T, IN_WIDTH), f32), jax.ShapeDtypeStruct((T, D_MODEL), bf16)],
        scratch_shapes=[pltpu.VMEM((tm, D_MODEL), bf16)],
        compiler_params=_params("parallel", "arbitrary"),
    )(x, g, wt, tail)


def _out_proj(ch, ca, wo, l, x, g, *, tm=512):
    T = x.shape[0]
    tm = min(tm, T)
    half = MIX_WIDTH // 2

    def body(ch_ref, ca_ref, wo_ref, x_ref, g_ref, xn_ref, y_ref):
        y = _dot(ch_ref[...], wo_ref[0:half, :]) + _dot(ca_ref[...], wo_ref[half:MIX_WIDTH, :])
        r = lax.rsqrt(jnp.mean(y * y, axis=-1, keepdims=True) + NORM_EPS)
        y_ref[...] = y
        xn_ref[...] = x_ref[...] + y * r * g_ref[...]

    row = lambda i: (i, 0)
    fixed = lambda i: (0, 0)
    return pl.pallas_call(
        body, name="out_proj", grid=(T // tm,),
        in_specs=[pl.BlockSpec((tm, half), row), pl.BlockSpec((tm, half), row),
                  pl.BlockSpec((None, MIX_WIDTH, D_MODEL), lambda i: (l, 0, 0)), pl.BlockSpec((tm, D_MODEL), row),
                  pl.BlockSpec((1, D_MODEL), fixed)],
        out_specs=[pl.BlockSpec((tm, D_MODEL), row), pl.BlockSpec((tm, D_MODEL), row)],
        out_shape=[jax.ShapeDtypeStruct((T, D_MODEL), f32)] * 2,
        compiler_params=_params("parallel"),
    )(ch, ca, wo, x, g)


def _loss_head(y, target, *, tm=512):
    T = y.shape[0]
    tm = min(tm, T)

    def body(y_ref, t_ref, d_ref, l_ref):
        @pl.when(pl.program_id(0) == 0)
        def _():
            l_ref[...] = jnp.zeros_like(l_ref)
        err = y_ref[...] - t_ref[...]
        d_ref[...] = err * (1.0 / D_MODEL)
        l_ref[...] += jnp.sum(err * err) * (0.5 / D_MODEL)

    row = lambda i: (i, 0)
    return pl.pallas_call(
        body, name="loss_head", grid=(T // tm,),
        in_specs=[pl.BlockSpec((tm, D_MODEL), row), pl.BlockSpec((tm, D_MODEL), row)],
        out_specs=[pl.BlockSpec((tm, D_MODEL), row), pl.BlockSpec((8, LANES), lambda i: (0, 0))],
        out_shape=[jax.ShapeDtypeStruct((T, D_MODEL), f32), jax.ShapeDtypeStruct((8, LANES), f32)],
        compiler_params=_params("arbitrary"),
    )(y, target)


def _out_proj_bwd(dxn, y, g, wo, l, ch, ca, *, tm=256, ride=None):
    T = y.shape[0]
    tm = min(tm, T)
    half = MIX_WIDTH // 2

    def body(dx_ref, y_ref, g_ref, wo_ref, ch_ref, ca_ref, dch_ref, dca_ref, dwo_ref, dg_ref):
        @pl.when(pl.program_id(0) == 0)
        def _():
            dwo_ref[...] = jnp.zeros_like(dwo_ref)
            dg_ref[...] = jnp.zeros_like(dg_ref)
        y = y_ref[...]
        dx = dx_ref[...]
        r = lax.rsqrt(jnp.mean(y * y, axis=-1, keepdims=True) + NORM_EPS)
        gy = dx * g_ref[...]
        dy = r * gy - y * (r * r * r) * jnp.mean(gy * y, axis=-1, keepdims=True)
        dg_ref[...] += jnp.sum(dx * y * r, axis=0, keepdims=True)
        dyb = dy.astype(bf16)
        dch_ref[...] = _dot(dyb, wo_ref[0:half, :], NT)
        dca_ref[...] = _dot(dyb, wo_ref[half:MIX_WIDTH, :], NT)
        dwo_ref[0:half, :] += _dot(ch_ref[...], dyb, TN)
        dwo_ref[half:MIX_WIDTH, :] += _dot(ca_ref[...], dyb, TN)

    row = lambda i: (i, 0)
    fixed = lambda i: (0, 0)
    return _call(
        body, (dxn, y, g, wo, ch, ca), name="out_proj_bwd", grid=(T // tm,),
        in_specs=[pl.BlockSpec((tm, D_MODEL), row), pl.BlockSpec((tm, D_MODEL), row),
                  pl.BlockSpec((1, D_MODEL), fixed), pl.BlockSpec((None, MIX_WIDTH, D_MODEL), lambda i: (l, 0, 0)),
                  pl.BlockSpec((tm, half), row), pl.BlockSpec((tm, half), row)],
        out_specs=[pl.BlockSpec((tm, half), row), pl.BlockSpec((tm, half), row),
                   pl.BlockSpec((MIX_WIDTH, D_MODEL), fixed), pl.BlockSpec((1, D_MODEL), fixed)],
        out_shape=[jax.ShapeDtypeStruct((T, half), f32), jax.ShapeDtypeStruct((T, half), f32),
                   jax.ShapeDtypeStruct((MIX_WIDTH, D_MODEL), f32), jax.ShapeDtypeStruct((1, D_MODEL), f32)],
        semantics=("arbitrary",), ride=ride)


TILE = 256
PIECE_TILES = (4, 4, 4, 4, 4, 1, 4)
PIECE_START = tuple(sum(PIECE_TILES[:p]) for p in range(len(PIECE_TILES)))
N_TILES = sum(PIECE_TILES)


def _piece_specs(rows, index):
    def spec(s, n):
        def index_map(*g):
            r, t = index(*g)
            return r, jnp.clip(t - s, 0, n - 1)
        return pl.BlockSpec((rows, TILE), index_map)
    return [spec(s, n) for s, n in zip(PIECE_START, PIECE_TILES)]


def _for_piece(t, fn):
    for p, (s, n) in enumerate(zip(PIECE_START, PIECE_TILES)):
        pl.when((t >= s) & (t < s + n))(functools.partial(fn, p))


def _in_proj_bwd(pieces, wt, tail, l, x, g, dxn, *, tm=512, blocks=None, dx_into=None, ride=None):
    T = x.shape[0]
    tm = min(tm, T)
    first, count = blocks or (0, T // tm)
    npc = len(pieces)
    nk = npc
    nmain = npc - 2
    wide = ATT_WIDTH
    extra = [] if dx_into is None else [dx_into]

    def body(*refs):
        dp_refs = refs[:npc]
        w_ref, tz_ref, tkv_ref, x_ref, g_ref, dxn_ref = refs[npc:npc + 6]
        dx_ref, dg_ref, acc = refs[npc + 6 + len(extra):]
        i, k = pl.program_id(0), pl.program_id(1)

        @pl.when((i == 0) & (k == 0))
        def _():
            dg_ref[...] = jnp.zeros_like(dg_ref)

        @pl.when(k == 0)
        def _():
            acc[...] = jnp.zeros_like(acc)

        for p in range(npc):
            w_p = w_ref if p < nmain else (tkv_ref if p == nmain else tz_ref)

            def add(p=p, w_p=w_p):
                acc[...] += _dot(dp_refs[p][...], w_p[...])
            pl.when(k == p)(add)

        @pl.when(k == nk - 1)
        def _():
            dh = acc[...]
            xv = x_ref[...]
            r = lax.rsqrt(jnp.mean(xv * xv, axis=-1, keepdims=True) + NORM_EPS)
            gy = dh * g_ref[...]
            dx_ref[...] = dxn_ref[...] + r * gy - xv * (r * r * r) * jnp.mean(gy * xv, axis=-1, keepdims=True)
            dg_ref[...] += jnp.sum(dh * xv * r, axis=0, keepdims=True)

    rows = lambda i, k: (first + i, 0)
    return _call(
        body, (*pieces, wt, tail, tail, x, g, dxn, *extra), name="in_proj_bwd", grid=(count, nk),
        in_specs=[pl.BlockSpec((tm, p.shape[1]), rows) for p in pieces] + [
            pl.BlockSpec((None, wide, D_MODEL), lambda i, k: (l, jnp.minimum(k, nmain - 1), 0)),
            pl.BlockSpec((None, wide, D_MODEL), lambda i, k: (l, 0, 0)),
            pl.BlockSpec((None, TAIL - wide, D_MODEL), lambda i, k: (l, wide // (TAIL - wide), 0)),
            pl.BlockSpec((tm, D_MODEL), rows), pl.BlockSpec((1, D_MODEL), lambda i, k: (0, 0)),
            pl.BlockSpec((tm, D_MODEL), rows)] + [ANY] * len(extra),
        out_specs=[pl.BlockSpec((tm, D_MODEL), rows), pl.BlockSpec((1, D_MODEL), lambda i, k: (0, 0))],
        out_shape=[jax.ShapeDtypeStruct((T, D_MODEL), f32), jax.ShapeDtypeStruct((1, D_MODEL), f32)],
        scratch_shapes=[pltpu.VMEM((tm, D_MODEL), f32)],
        semantics=("arbitrary", "arbitrary"), ride=ride, aliases={npc + 6: 0} if extra else None)


def _grad_w_in(h, pieces, *, ride=None):
    T = h.shape[0]
    npc = len(pieces)

    def body(*refs):
        h_ref, dp_refs, o_ref = refs[0], refs[1:1 + npc], refs[1 + npc]

        def put(p):
            o_ref[...] = _dot(dp_refs[p][...], h_ref[...], TN)
        _for_piece(pl.program_id(0), put)

    return _call(
        body, (h, *pieces), name="grad_w_in", grid=(N_TILES,),
        in_specs=[pl.BlockSpec((T, D_MODEL), lambda j: (0, 0), pipeline_mode=pl.Buffered(1))]
        + _piece_specs(T, lambda j: (0, j)),
        out_specs=[pl.BlockSpec((TILE, D_MODEL), lambda j: (j, 0))],
        out_shape=[jax.ShapeDtypeStruct((IN_WIDTH, D_MODEL), f32)],
        semantics=("parallel",), ride=ride)


def _lower_bound(lbp, layer):
    m = jnp.max(lbp, axis=0, keepdims=True)
    e = jnp.exp(lbp - m)
    p = e / jnp.sum(e, axis=0, keepdims=True)
    acc = p[0:1]
    for i in range(1, layer + 1):
        acc = acc + p[i:i + 1]
    return acc - p[0:1]


def _gate_parts(qr, fr, lb, lbf):
    sq = _sigmoid(qr)
    e = jnp.exp(-jnp.abs(fr))
    inv = 1.0 / (1.0 + e)
    pos = fr >= 0
    sg = jnp.where(pos, inv, e * inv)
    nsg = jnp.where(pos, e * inv, inv)
    fg = lbf + (1.0 - lb) * sg
    return qr * sq, sq, sg, nsg, fg, jnp.log(fg), (1.0 - lb) * nsg


LEVELS = tuple(SUB << j for j in range((CHUNK // SUB).bit_length() - 1))


def _level_masks(transposed=False):
    t = lax.broadcasted_iota(jnp.int32, (CHUNK, CHUNK), 1 if transposed else 0)
    s = lax.broadcasted_iota(jnp.int32, (CHUNK, CHUNK), 0 if transposed else 1)
    return [(t % (2 * m) >= m) & (s % (2 * m) < m) & (t // (2 * m) == s // (2 * m)) for m in LEVELS]


def _level_anchor(b_s, row, m):
    beta = b_s[m - 1:m, :]
    for g in range(1, CHUNK // (2 * m)):
        beta = jnp.where(row >= g * 2 * m, b_s[g * 2 * m + m - 1:g * 2 * m + m, :], beta)
    return beta


FWD_INTERLEAVE = 16
BWD_INTERLEAVE = 8
SWA_INTERLEAVE = 4


def _interleaved(chunks, width):
    for g0 in range(0, len(chunks), width):
        live = chunks[g0:g0 + width]
        while live:
            for gen in list(live):
                try:
                    next(gen)
                except StopIteration:
                    live.remove(gen)


def _seg_sum(seg, x):
    hi = x.astype(bf16)
    return _dot(seg, hi) + _dot(seg, (x - hi.astype(f32)).astype(bf16))


def _hgrn_fwd(proj, lb_param, g_head, *, B, S, layer, ride=None):
    T = B * S
    TB = min(1024, S)
    nT, NC = S // TB, TB // CHUNK
    nC = S // CHUNK
    HD = HG_HEAD_DIM

    def body(q_ref, f_ref, i_ref, z_ref, lb_ref, gh_ref, cat_ref, op_ref, st_ref,
             s_scr, b_scr, k_scr):
        @pl.when(pl.program_id(2) == 0)
        def _():
            s_scr[...] = jnp.zeros_like(s_scr)
        lb = _lower_bound(lb_ref[...], layer)
        lbf = jnp.maximum(lb, LB_FLOOR)
        gh = gh_ref[...]
        r_i = lax.broadcasted_iota(jnp.int32, (CHUNK, CHUNK), 0)
        c_i = lax.broadcasted_iota(jnp.int32, (CHUNK, CHUNK), 1)
        tril = (r_i >= c_i).astype(bf16)
        rows8 = lax.broadcasted_iota(jnp.int32, (8, HD), 0)
        row_c = lax.broadcasted_iota(jnp.int32, (CHUNK, HD), 0)
        lane_c = lax.broadcasted_iota(jnp.int32, (8, CHUNK), 1)
        masks = _level_masks()

        def chunk(c, carried):
            rs = slice(c * CHUNK, (c + 1) * CHUNK)
            b_s, k_s = b_scr.at[c], k_scr.at[c]
            q, _, _, _, _, logf, k = _gate_parts(q_ref[rs, :], f_ref[rs, :], lb, lbf)
            v = i_ref[rs, :]
            b = _seg_sum(tril, logf)
            b_s[...] = b
            k_s[...] = k
            yield
            pieces = []
            for blk in range(CHUNK // SUB):
                r0 = blk * SUB
                bp = [b[r0 + 8 * i:r0 + 8 * i + 8] for i in range(SUB // 8)]
                qp = [q[r0 + 8 * i:r0 + 8 * i + 8] for i in range(SUB // 8)]
                ap = [jnp.zeros((8, CHUNK), f32) for _ in range(SUB // 8)]
                for s in range(SUB):
                    bs = b_s[r0 + s:r0 + s + 1, :]
                    ks = k_s[r0 + s:r0 + s + 1, :]
                    for i in range(s // 8, SUB // 8):
                        diff = bp[i] - bs
                        if i == s // 8:
                            diff = jnp.where(rows8 >= s - 8 * i, diff, NEG_INF)
                        col = jnp.sum(jnp.exp(diff) * qp[i] * ks, axis=1, keepdims=True)
                        ap[i] = jnp.where(lane_c == r0 + s, col, ap[i])
                pieces += ap
                yield
            a_all = jnp.concatenate(pieces, axis=0)
            for m, mk in zip(LEVELS, masks):
                beta = _level_anchor(b_s, row_c, m)
                qh = (q * jnp.exp(jnp.minimum(b - beta, 0.0))).astype(bf16)
                kh = (k * jnp.exp(jnp.minimum(beta - b, 0.0))).astype(bf16)
                a_all = a_all + jnp.where(mk, _dot(qh, kh, NT), 0.0)
            yield
            st = carried[0]
            st_ref[0, 0, c] = st
            vb16 = v.astype(bf16)
            o = _dot(a_all.astype(bf16), vb16) + _dot((q * jnp.exp(b)).astype(bf16), st.astype(bf16), NT)
            b_end = b_s[CHUNK - 1:CHUNK, :]
            kdec = (k * jnp.exp(b_end - b)).astype(bf16)
            carried[0] = jnp.exp(b_end) * st + _dot(vb16, kdec, TN)
            rr = lax.rsqrt(jnp.mean(o * o, axis=-1, keepdims=True) + NORM_EPS)
            zr = z_ref[rs, :]
            cat_ref[rs, :] = (o * rr * gh * (zr * _sigmoid(zr))).astype(bf16)
            op_ref[rs, :] = o

        carried = [s_scr[...]]
        _interleaved([chunk(c, carried) for c in range(NC)], FWD_INTERLEAVE)
        s_scr[...] = carried[0]

    def col(part):
        return pl.BlockSpec((TB, HD), lambda b, h, n: (b * nT + n, part * HG_HEADS + h))

    out_col = pl.BlockSpec((TB, HD), lambda b, h, n: (b * nT + n, h))
    return _call(
        body, (proj, proj, proj, proj, lb_param, g_head),
        name=f"hgrn_fwd_l{layer}", grid=(B, HG_HEADS, nT),
        in_specs=[col(0), col(1), col(2), col(3),
                  pl.BlockSpec((DEPTH, HD), lambda b, h, n: (0, h)),
                  pl.BlockSpec((1, HD), lambda b, h, n: (0, 0))],
        out_specs=[out_col, out_col,
                   pl.BlockSpec((1, 1, NC, HD, HD), lambda b, h, n: (b, h, n, 0, 0))],
        out_shape=[jax.ShapeDtypeStruct((T, HG_WIDTH), bf16), jax.ShapeDtypeStruct((T, HG_WIDTH), f32),
                   jax.ShapeDtypeStruct((B, HG_HEADS, nC, HD, HD), f32)],
        scratch_shapes=[pltpu.VMEM((HD, HD), f32), pltpu.VMEM((NC, CHUNK, HD), f32), pltpu.VMEM((NC, CHUNK, HD), f32)],
        semantics=("parallel", "parallel", "arbitrary"), ride=ride)


def _hgrn_bwd(proj, lb_param, g_head, o_pre, states, dcat, *, B, S, layer, ride=None):
    T = B * S
    TB = min(1024, S)
    nT, NC = S // TB, TB // CHUNK
    HD = HG_HEAD_DIM

    def body(q_ref, f_ref, i_ref, z_ref, lb_ref, gh_ref, op_ref, st_ref, dc_ref,
             dq_ref, df_ref, di_ref, dz_ref, dlb_ref, dgh_ref,
             ds_scr, b_scr, q_scr, do_scr, wk_scr):
        @pl.when(pl.program_id(2) == 0)
        def _():
            ds_scr[...] = jnp.zeros_like(ds_scr)
            dlb_ref[...] = jnp.zeros_like(dlb_ref)
            dgh_ref[...] = jnp.zeros_like(dgh_ref)
        lb = _lower_bound(lb_ref[...], layer)
        lbf = jnp.maximum(lb, LB_FLOOR)
        ind = (lb > LB_FLOOR).astype(f32)
        gh = gh_ref[...]
        r_i = lax.broadcasted_iota(jnp.int32, (CHUNK, CHUNK), 0)
        c_i = lax.broadcasted_iota(jnp.int32, (CHUNK, CHUNK), 1)
        tril = (r_i >= c_i).astype(bf16)
        triu = (c_i >= r_i).astype(bf16)
        rows8 = lax.broadcasted_iota(jnp.int32, (8, HD), 0)
        row_c = lax.broadcasted_iota(jnp.int32, (CHUNK, HD), 0)
        lane_c = lax.broadcasted_iota(jnp.int32, (8, CHUNK), 1)
        last_row = row_c == CHUNK - 1
        masks = _level_masks()
        masks_t = _level_masks(transposed=True)
        seg_t = lax.broadcasted_iota(jnp.int32, (SUB, 8 * SUB), 0)
        seg_r = lax.broadcasted_iota(jnp.int32, (SUB, 8 * SUB), 1) // 8
        seg0 = (seg_r == seg_t).astype(bf16)
        seg1 = (seg_r[:, 0:4 * SUB] + 8 == seg_t[:, 0:4 * SUB]).astype(bf16)

        def chunk(c, carried):
            rs = slice(c * CHUNK, (c + 1) * CHUNK)
            b_s, q_s, do_s = b_scr.at[c], q_scr.at[c], do_scr.at[c]
            qr, fr = q_ref[rs, :], f_ref[rs, :]
            q, sq, sg, nsg, fg, logf, k = _gate_parts(qr, fr, lb, lbf)
            v = i_ref[rs, :]
            b = _seg_sum(tril, logf)
            o = op_ref[rs, :]
            dc = dc_ref[rs, :]
            zr = z_ref[rs, :]
            sz = _sigmoid(zr)
            rr = lax.rsqrt(jnp.mean(o * o, axis=-1, keepdims=True) + NORM_EPS)
            dz_ref[rs, :] = (dc * (o * rr * gh) * (sz * (1.0 + zr * (1.0 - sz)))).astype(bf16)
            dn = dc * (zr * sz)
            dgh_ref[0, 0] += jnp.sum(dn * o * rr, axis=0, keepdims=True)
            gdn = dn * gh
            d_o = rr * gdn - o * (rr * rr * rr) * jnp.mean(gdn * o, axis=-1, keepdims=True)
            b_s[...] = b
            q_s[...] = q
            do_s[...] = d_o
            dob = d_o.astype(bf16)
            vb16 = v.astype(bf16)
            d_a = _dot(dob, vb16, NT)
            yield
            d_q = jnp.zeros((CHUNK, HD), f32)
            d_k = jnp.zeros((CHUNK, HD), f32)
            at_all = jnp.zeros((CHUNK, CHUNK), f32)
            for m, mk, mkt in zip(LEVELS, masks, masks_t):
                beta = _level_anchor(b_s, row_c, m)
                eq = jnp.exp(jnp.minimum(b - beta, 0.0))
                ek = jnp.exp(jnp.minimum(beta - b, 0.0))
                qh = (q * eq).astype(bf16)
                kh = (k * ek).astype(bf16)
                at_all = at_all + jnp.where(mkt, _dot(kh, qh, NT), 0.0)
                d_aa = jnp.where(mk, d_a, 0.0).astype(bf16)
                d_q = d_q + _dot(d_aa, kh) * eq
                d_k = d_k + _dot(d_aa, qh, TN) * ek
            yield
            dq_blocks, dk_pieces, at_pieces = [], [], []
            for blk in range(CHUNK // SUB):
                r0 = blk * SUB
                wk = wk_scr.at[c * (CHUNK // SUB) + blk]
                bp = [b[r0 + 8 * i:r0 + 8 * i + 8] for i in range(SUB // 8)]
                kp = [k[r0 + 8 * i:r0 + 8 * i + 8] for i in range(SUB // 8)]
                vp = [v[r0 + 8 * i:r0 + 8 * i + 8] for i in range(SUB // 8)]
                dkp = [jnp.zeros((8, HD), f32) for _ in range(SUB // 8)]
                atp = [jnp.zeros((8, CHUNK), f32) for _ in range(SUB // 8)]
                for t in range(SUB):
                    bt = b_s[r0 + t:r0 + t + 1, :]
                    qt = q_s[r0 + t:r0 + t + 1, :]
                    dot_ = do_s[r0 + t:r0 + t + 1, :]
                    for i in range(t // 8 + 1):
                        diff = bt - bp[i]
                        if i == t // 8:
                            diff = jnp.where(rows8 <= t - 8 * i, diff, NEG_INF)
                        e = jnp.exp(diff)
                        a = jnp.sum(e * kp[i] * qt, axis=1, keepdims=True)
                        atp[i] = jnp.where(lane_c == r0 + t, a, atp[i])
                        w = jnp.sum(vp[i] * dot_, axis=1, keepdims=True) * e
                        dkp[i] = dkp[i] + w * qt
                        row = 8 * t if i == 0 else 8 * SUB + 8 * (t - 8)
                        wk[row:row + 8, :] = w * kp[i]
                dq_blk = _seg_sum(seg0, wk[0:8 * SUB, :])
                if SUB > 8:
                    dq_blk = dq_blk + _seg_sum(seg1, wk[8 * SUB:12 * SUB, :])
                dq_blocks.append(dq_blk)
                dk_pieces += dkp
                at_pieces += atp
                yield
            dst1 = carried[0]
            st0 = st_ref[0, 0, c]
            dst1b = dst1.astype(bf16)
            eb = jnp.exp(b)
            b_end = b_s[CHUNK - 1:CHUNK, :]
            edec = jnp.exp(b_end - b)
            e_end = jnp.exp(b_end)
            kdec = (k * edec).astype(bf16)
            qdec = (q * eb).astype(bf16)
            st1 = e_end * st0 + _dot(vb16, kdec, TN)
            rterm = jnp.sum(dst1 * st1, axis=0, keepdims=True)
            carried[0] = e_end * dst1 + _dot(dob, qdec, TN)
            d_q = d_q + _dot(dob, st0.astype(bf16)) * eb + jnp.concatenate(dq_blocks, axis=0)
            d_k = d_k + _dot(vb16, dst1b) * edec + jnp.concatenate(dk_pieces, axis=0)
            d_v = _dot(kdec, dst1b, NT) + _dot((at_all + jnp.concatenate(at_pieces, axis=0)).astype(bf16), dob)
            db = q * d_q - k * d_k + jnp.where(last_row, rterm, 0.0)
            dlt = _seg_sum(triu, db) - fg * d_k
            df_ref[rs, :] = (dlt * (1.0 - lb) * sg * nsg / fg).astype(bf16)
            dlb_ref[0] += jnp.sum(dlt * (ind - sg) / fg, axis=0, keepdims=True)
            dq_ref[rs, :] = (d_q * (sq * (1.0 + qr * (1.0 - sq)))).astype(bf16)
            di_ref[rs, :] = d_v.astype(bf16)

        carried = [ds_scr[...]]
        _interleaved([chunk(c, carried) for c in reversed(range(NC))], BWD_INTERLEAVE)
        ds_scr[...] = carried[0]

    def col(part):
        return pl.BlockSpec((TB, HD), lambda b, h, n: (b * nT + nT - 1 - n, part * HG_HEADS + h))

    hcol = pl.BlockSpec((TB, HD), lambda b, h, n: (b * nT + nT - 1 - n, h))
    return _call(
        body, (proj, proj, proj, proj, lb_param, g_head, o_pre, states, dcat),
        name=f"hgrn_bwd_l{layer}", grid=(B, HG_HEADS, nT),
        in_specs=[col(0), col(1), col(2), col(3),
                  pl.BlockSpec((DEPTH, HD), lambda b, h, n: (0, h)),
                  pl.BlockSpec((1, HD), lambda b, h, n: (0, 0)),
                  hcol,
                  pl.BlockSpec((1, 1, NC, HD, HD), lambda b, h, n: (b, h, nT - 1 - n, 0, 0)),
                  hcol],
        out_specs=[hcol, hcol, hcol, hcol,
                   pl.BlockSpec((1, 1, HD), lambda b, h, n: (b, 0, h)),
                   pl.BlockSpec((1, 1, 1, HD), lambda b, h, n: (b, h, 0, 0))],
        out_shape=[jax.ShapeDtypeStruct((T, HG_WIDTH), bf16)] * 4 + [
            jax.ShapeDtypeStruct((B, 1, HG_WIDTH), f32), jax.ShapeDtypeStruct((B, HG_HEADS, 1, HD), f32)],
        scratch_shapes=[pltpu.VMEM((HD, HD), f32)] + [pltpu.VMEM((NC, CHUNK, HD), f32)] * 3
        + [pltpu.VMEM((NC * CHUNK // SUB, 12 * SUB, HD), f32)],
        semantics=("parallel", "parallel", "arbitrary"), ride=ride)


def _rope_tables(S):
    half = ATT_HEAD_DIM // 2
    inv_freq = ROPE_THETA ** (-jnp.arange(half, dtype=f32) / half)
    ang = jnp.arange(S, dtype=f32)[:, None] * inv_freq[None, :]
    cos, sin = jnp.cos(ang), jnp.sin(ang)
    return jnp.tile(jnp.concatenate([cos, cos], axis=1), (1, 2)), jnp.tile(jnp.concatenate([-sin, sin], axis=1), (1, 2))


def _swap_halves(x, first_half):
    return jnp.where(first_half, pltpu.roll(x, LANES - ATT_HEAD_DIM // 2, 1), pltpu.roll(x, ATT_HEAD_DIM // 2, 1))


def _rope(x, cos, sin, first_half):
    return x * cos + _swap_halves(x, first_half) * sin


def _rope_bwd(dy, cos, sin, first_half):
    return dy * cos + _swap_halves(dy * sin, first_half)


def _attn_consts(n):
    lane = lax.broadcasted_iota(jnp.int32, (1, LANES), 1)
    low = lane < ATT_HEAD_DIM
    first_half = (lane % ATT_HEAD_DIM) < ATT_HEAD_DIM // 2
    top = lax.broadcasted_iota(jnp.int32, (LANES, 1), 0) < ATT_HEAD_DIM
    s = lax.broadcasted_iota(jnp.int32, (2 * ATT_BLOCK, ATT_BLOCK), 0)
    t = lax.broadcasted_iota(jnp.int32, (2 * ATT_BLOCK, ATT_BLOCK), 1)
    mask = (s > t) & (s <= t + ATT_BLOCK) & ((s >= ATT_BLOCK) | (n > 0))
    return low, first_half, top, mask


def _dup_kv(x, low):
    rolled = pltpu.roll(x, ATT_HEAD_DIM, 1)
    return [jnp.where(low, x, rolled), jnp.where(low, rolled, x)]


def _attn_head(qtm, kd, vdt, sink, mask):
    s = jnp.where(mask, _dot(kd, qtm) * ATT_SCALE, NEG_INF)
    m = jnp.maximum(jnp.max(s, axis=0, keepdims=True), sink)
    p = jnp.exp(s - m)
    psink = jnp.exp(sink - m)
    inv = 1.0 / (jnp.sum(p, axis=0, keepdims=True) + psink)
    pn = p * inv
    return pn, psink * inv, _dot(vdt, pn.astype(bf16))


def _swa_fwd(proj, sink_b, cos, sin, *, B, S, ride=None):
    T = B * S
    L = ATT_BLOCK
    nB = S // L

    def body(q_ref, z_ref, kvc_ref, kvp_ref, sk_ref, cc_ref, sc_ref, cp_ref, sp_ref, cat_ref):
        n = pl.program_id(1)
        low, first_half, top, mask = _attn_consts(n)
        cc, sc = cc_ref[...], sc_ref[...]
        kc = _rope(kvc_ref[:, 0:LANES], cc, sc, first_half)
        kp = _rope(kvp_ref[:, 0:LANES], cp_ref[...], sp_ref[...], first_half)
        kd = [x.astype(bf16) for x in _dup_kv(jnp.concatenate([kp, kc], axis=0), low)]
        vdt = [x.T.astype(bf16) for x in _dup_kv(jnp.concatenate([kvp_ref[:, LANES:2 * LANES], kvc_ref[:, LANES:2 * LANES]], axis=0), low)]
        def head_pair(pair):
            cols = slice(pair * LANES, (pair + 1) * LANES)
            j = (2 * pair) // ATT_GROUP
            qt = _rope(q_ref[:, cols], cc, sc, first_half).T
            yield
            outs = []
            for hh in range(2):
                h = 2 * pair + hh
                qtm = jnp.where(top if hh == 0 else ~top, qt, 0.0).astype(bf16)
                outs.append(_attn_head(qtm, kd[j], vdt[j], sk_ref[h:h + 1, 0:1], mask)[2])
                yield
            zp = z_ref[:, cols]
            cat_ref[:, cols] = (jnp.where(top, outs[0], outs[1]).T * (zp * _sigmoid(zp))).astype(bf16)

        _interleaved([head_pair(p) for p in range(ATT_HEADS // 2)], SWA_INTERLEAVE)

    cur = lambda b, n: (b * nB + n, 0)
    return _call(
        body, (proj, proj, proj, proj, sink_b, cos, sin, cos, sin), name="swa_fwd", grid=(B, nB),
        in_specs=[pl.BlockSpec((L, ATT_WIDTH), lambda b, n: (b * nB + n, QA_BLK)),
                  pl.BlockSpec((L, ATT_WIDTH), lambda b, n: (b * nB + n, ZA_BLK)),
                  pl.BlockSpec((L, 2 * KV_WIDTH), lambda b, n: (b * nB + n, KV_BLK)),
                  pl.BlockSpec((L, 2 * KV_WIDTH), lambda b, n: (b * nB + jnp.maximum(n - 1, 0), KV_BLK)),
                  pl.BlockSpec((ATT_HEADS, LANES), lambda b, n: (0, 0)),
                  pl.BlockSpec((L, LANES), lambda b, n: (n, 0)), pl.BlockSpec((L, LANES), lambda b, n: (n, 0)),
                  pl.BlockSpec((L, LANES), lambda b, n: (jnp.maximum(n - 1, 0), 0)),
                  pl.BlockSpec((L, LANES), lambda b, n: (jnp.maximum(n - 1, 0), 0))],
        out_specs=[pl.BlockSpec((L, ATT_WIDTH), cur)],
        out_shape=[jax.ShapeDtypeStruct((T, ATT_WIDTH), bf16)],
        semantics=("parallel", "parallel"), ride=ride)


def _swa_bwd(proj, sink_b, cos, sin, dcat, *, B, S, ride=None):
    T = B * S
    L = ATT_BLOCK
    nB = S // L

    def body(q_ref, z_ref, kvc_ref, kvp_ref, sk_ref, cc_ref, sc_ref, cp_ref, sp_ref, dc_ref,
             dq_ref, dz_ref, dkv_ref, dsk_ref, carry, ds_st, pn_st, q_st, do_st):
        step = pl.program_id(1)
        n = nB - 1 - step

        @pl.when((pl.program_id(0) == 0) & (step == 0))
        def _():
            dsk_ref[...] = jnp.zeros_like(dsk_ref)

        @pl.when(step == 0)
        def _():
            carry[...] = jnp.zeros_like(carry)
        low, first_half, top, mask = _attn_consts(n)
        cc, sc, cp, sp = cc_ref[...], sc_ref[...], cp_ref[...], sp_ref[...]
        kc = _rope(kvc_ref[:, 0:LANES], cc, sc, first_half)
        kp = _rope(kvp_ref[:, 0:LANES], cp, sp, first_half)
        kdf = _dup_kv(jnp.concatenate([kp, kc], axis=0), low)
        vdf = _dup_kv(jnp.concatenate([kvp_ref[:, LANES:2 * LANES], kvc_ref[:, LANES:2 * LANES]], axis=0), low)
        kd = [x.astype(bf16) for x in kdf]
        vd = [x.astype(bf16) for x in vdf]
        kdt = [x.T.astype(bf16) for x in kdf]
        vdt = [x.T.astype(bf16) for x in vdf]
        dkd, dvd = [], []
        def head_pair(pair):
            cols = slice(pair * LANES, (pair + 1) * LANES)
            j = (2 * pair) // ATT_GROUP
            qp = _rope(q_ref[:, cols], cc, sc, first_half)
            qt = qp.T
            zp = z_ref[:, cols]
            dc = dc_ref[:, cols]
            sz = _sigmoid(zp)
            d_o = dc * (zp * sz)
            dot_ = d_o.T
            yield
            res = []
            for hh in range(2):
                rsel = top if hh == 0 else ~top
                qtm = jnp.where(rsel, qt, 0.0).astype(bf16)
                pn, psn, o = _attn_head(qtm, kd[j], vdt[j], sk_ref[2 * pair + hh:2 * pair + hh + 1, 0:1], mask)
                res.append((rsel, pn, psn, o))
                yield
            ot = jnp.where(top, res[0][3], res[1][3])
            dz_ref[:, cols] = (dc * ot.T * (sz * (1.0 + zp * (1.0 - sz)))).astype(bf16)
            dqts = []
            for hh in range(2):
                h = 2 * pair + hh
                rsel, pn, psn, _ = res[hh]
                lsel = low if hh == 0 else ~low
                dotm = jnp.where(rsel, dot_, 0.0)
                delta = jnp.sum(dotm * ot, axis=0, keepdims=True)
                dst = (pn * (_dot(vd[j], dotm.astype(bf16)) - delta) * ATT_SCALE).astype(bf16)
                dsk_ref[h:h + 1, :] += jnp.zeros((1, LANES), f32) - jnp.sum(psn * delta)
                dqts.append(_dot(kdt[j], dst))
                g = h % ATT_GROUP
                ds_st[:, g * LANES:(g + 1) * LANES] = dst
                pn_st[:, g * LANES:(g + 1) * LANES] = pn.astype(bf16)
                q_st[g * LANES:(g + 1) * LANES, :] = jnp.where(lsel, qp, 0.0).astype(bf16)
                do_st[g * LANES:(g + 1) * LANES, :] = jnp.where(lsel, d_o, 0.0).astype(bf16)
                yield
            dq_ref[:, cols] = _rope_bwd(jnp.where(top, dqts[0], dqts[1]).T, cc, sc, first_half).astype(bf16)

        pairs_per_group = ATT_GROUP // 2
        for grp in range(ATT_HEADS // ATT_GROUP):
            _interleaved([head_pair(grp * pairs_per_group + p) for p in range(pairs_per_group)], SWA_INTERLEAVE)
            dkd.append(_dot(ds_st[...], q_st[...]))
            dvd.append(_dot(pn_st[...], do_st[...]))
        dk = [x + pltpu.roll(x, ATT_HEAD_DIM, 1) for x in dkd]
        dv = [x + pltpu.roll(x, ATT_HEAD_DIM, 1) for x in dvd]
        dk = jnp.where(low, dk[0], dk[1])
        dv = jnp.where(low, dv[0], dv[1])
        dkv_ref[:, 0:LANES] = (_rope_bwd(dk[L:2 * L], cc, sc, first_half) + carry[:, 0:LANES]).astype(bf16)
        dkv_ref[:, LANES:2 * LANES] = (dv[L:2 * L] + carry[:, LANES:2 * LANES]).astype(bf16)
        carry[:, 0:LANES] = _rope_bwd(dk[0:L], cp, sp, first_half)
        carry[:, LANES:2 * LANES] = dv[0:L]

    rev = lambda b, s: b * nB + nB - 1 - s
    revp = lambda b, s: b * nB + jnp.maximum(nB - 2 - s, 0)
    wide = lambda blk: pl.BlockSpec((L, ATT_WIDTH), lambda b, s: (rev(b, s), blk))
    tab = pl.BlockSpec((L, LANES), lambda b, s: (nB - 1 - s, 0))
    tabp = pl.BlockSpec((L, LANES), lambda b, s: (jnp.maximum(nB - 2 - s, 0), 0))
    return _call(
        body, (proj, proj, proj, proj, sink_b, cos, sin, cos, sin, dcat), name="swa_bwd", grid=(B, nB),
        in_specs=[wide(QA_BLK), wide(ZA_BLK),
                  pl.BlockSpec((L, 2 * KV_WIDTH), lambda b, s: (rev(b, s), KV_BLK)),
                  pl.BlockSpec((L, 2 * KV_WIDTH), lambda b, s: (revp(b, s), KV_BLK)),
                  pl.BlockSpec((ATT_HEADS, LANES), lambda b, s: (0, 0)),
                  tab, tab, tabp, tabp, wide(0)],
        out_specs=[wide(0), wide(0), pl.BlockSpec((L, 2 * KV_WIDTH), lambda b, s: (rev(b, s), 0)),
                   pl.BlockSpec((ATT_HEADS, LANES), lambda b, s: (0, 0))],
        out_shape=[jax.ShapeDtypeStruct((T, ATT_WIDTH), bf16), jax.ShapeDtypeStruct((T, ATT_WIDTH), bf16),
                   jax.ShapeDtypeStruct((T, 2 * KV_WIDTH), bf16), jax.ShapeDtypeStruct((ATT_HEADS, LANES), f32)],
        scratch_shapes=[pltpu.VMEM((L, 2 * KV_WIDTH), f32),
                        pltpu.VMEM((2 * L, ATT_GROUP * LANES), bf16), pltpu.VMEM((2 * L, ATT_GROUP * LANES), bf16),
                        pltpu.VMEM((ATT_GROUP * LANES, LANES), bf16), pltpu.VMEM((ATT_GROUP * LANES, LANES), bf16)],
        semantics=("arbitrary", "arbitrary"), ride=ride)


def _train_step(x, target, bufs, g_pre, g_post, lb_param, g_head, sinks, *, B, S, exchange):
    L = DEPTH
    T = x.shape[0]
    ri, ro = IN_WIDTH // 8, MIX_WIDTH // 8
    cos, sin = _rope_tables(S)
    full = list(bufs)
    if exchange:
        full[0] = _run_exchange(_gather_d2d(_run_exchange(_gather_ici(bufs[0]))))
    saved = []
    for l in range(L):
        wt = full[l][0].reshape(1, IN_WIDTH, D_MODEL)
        wo = full[l][1].reshape(1, MIX_WIDTH, D_MODEL)
        tail = jnp.concatenate([wt[:, 5376:6400], wt[:, 5120:5376]], axis=1)
        proj, h = _in_proj(x, g_pre[l:l + 1], wt, tail, 0)
        ahead = exchange and l + 1 < L
        (ch, o_pre, states), landed = _hgrn_fwd(proj, lb_param, g_head[l:l + 1], B=B, S=S, layer=l,
                                                ride=_gather_ici(bufs[l + 1]) if ahead else None)
        sink_b = jnp.broadcast_to(sinks[l][:, None], (ATT_HEADS, LANES))
        (ca,), passed = _swa_fwd(proj, sink_b, cos, sin, B=B, S=S, ride=_gather_d2d(landed) if ahead else None)
        if ahead:
            full[l + 1] = passed
        xn, y = _out_proj(ch, ca, wo, 0, x, g_post[l:l + 1])
        saved.append((x, proj, h, ch, o_pre, states, sink_b, ca, y, wt, tail, wo))
        x = xn
    dx, loss = _loss_head(x, target)

    def reduce_tail(sums, recv):
        return _run_exchange(_pair_share([_chip_sum(s, r) for s, r in zip(sums, recv)]))

    grads = [None] * L
    waiting = None
    gg_pre, gg_post, g_lb, gg_head, g_sinks = [], [], [], [], []
    for l in reversed(range(L)):
        x_in, proj, h, ch, o_pre, states, sink_b, ca, y, wt, tail, wo = saved[l]
        (dch, dca, dwo, dgpost), got = _out_proj_bwd(dx, y, g_post[l:l + 1], wo, 0, ch, ca,
                                                     ride=_pair_exchange(waiting) if waiting else None)
        sums = [_pair_add(p, r) for p, r in zip(waiting, got)] if waiting else None
        (dq, df, di, dz, dlb, dgh), recv = _hgrn_bwd(proj, lb_param, g_head[l:l + 1], o_pre, states, dch, B=B, S=S,
                                                     layer=l, ride=_chip_exchange(sums) if waiting else None)
        if waiting:
            grads[l + 1] = reduce_tail(sums, recv)
        at_end = exchange and l == 0
        part_o = [dwo.reshape(1, 4, 2, ro, D_MODEL)]
        (dqa, dza, dkv, dsk), got_o = _swa_bwd(proj, sink_b, cos, sin, dca, B=B, S=S,
                                              ride=_pair_exchange(part_o) if at_end else None)
        pieces = [dq, df, di, dz, dqa, dkv, dza]
        sums_o = [_pair_add(part_o[0], got_o[0])] if at_end else None
        (gwt,), recv_o = _grad_w_in(h, pieces, ride=_chip_exchange(sums_o) if at_end else None)
        part_t = [gwt.reshape(1, 4, 2, ri, D_MODEL)]
        if at_end:
            tm = min(512, T // 2)
            nb = T // tm
            na = max(1, nb // 4)
            (dx_a, dg_a), got_t = _in_proj_bwd(pieces, wt, tail, 0, x_in, g_pre[l:l + 1], dx, tm=tm, blocks=(0, na),
                                               ride=_pair_exchange(part_t))
            sums_t = [_pair_add(part_t[0], got_t[0])]
            (dx, dg_b), recv_t = _in_proj_bwd(pieces, wt, tail, 0, x_in, g_pre[l:l + 1], dx, tm=tm, blocks=(na, nb - na),
                                              dx_into=dx_a, ride=_chip_exchange(sums_t))
            dgpre = dg_a + dg_b
            grads[0] = reduce_tail(sums_t + sums_o, recv_t + recv_o)
        else:
            (dx, dgpre), _ = _in_proj_bwd(pieces, wt, tail, 0, x_in, g_pre[l:l + 1], dx)
            if exchange:
                waiting = part_t + part_o
            else:
                grads[l] = [gwt, dwo]
        gg_pre.append(dgpre[0])
        gg_post.append(dgpost[0])
        g_lb.append(jnp.sum(dlb, axis=(0, 1)))
        gg_head.append(jnp.sum(dgh, axis=(0, 1, 2)))
        g_sinks.append(dsk[:, 0])
    rev = lambda xs: jnp.stack(xs[::-1])
    return loss[0, 0], dx, grads, rev(gg_pre), rev(gg_post), rev(g_lb), rev(gg_head), rev(g_sinks)


MESH = pl.DeviceIdType.MESH
ANY = pl.BlockSpec(memory_space=pl.ANY)


def _place():
    x, y, c = lax.axis_index("x"), lax.axis_index("y"), lax.axis_index("c")
    return x, y, c, [(1 - x, y), (x, 1 - y), (1 - x, 1 - y)]


def _rcopy(src, dst, send, recv, k, to):
    return pltpu.make_async_remote_copy(src_ref=src, dst_ref=dst, send_sem=send.at[k], recv_sem=recv.at[k],
                                        device_id=to, device_id_type=MESH)


class _Exchange:
    def __init__(self, name, inputs, out_shapes, n_sems, plan, in_place=False):
        self.name, self.inputs, self.out_shapes, self.n_sems, self.plan = name, inputs, out_shapes, n_sems, plan
        self.aliases = {a: a for a in range(len(inputs))} if in_place else {}

    def start(self, ins, outs, send, recv):
        for cp in self.plan(ins, outs, send, recv)[0]:
            cp.start()

    def finish(self, ins, outs, send, recv):
        sent, arriving = self.plan(ins, outs, send, recv)
        for cp in arriving:
            cp.wait_recv()
        for cp in sent:
            cp.wait_send()

    def sems(self):
        return [pltpu.SemaphoreType.DMA((self.n_sems,)), pltpu.SemaphoreType.DMA((self.n_sems,))]


def _run_exchange(ex):
    n_in, n_out = len(ex.inputs), len(ex.out_shapes)

    def body(*refs):
        ins, outs = refs[:n_in], refs[n_in:n_in + n_out]
        send, recv = refs[n_in + n_out:]
        ex.start(ins, outs, send, recv)
        ex.finish(ins, outs, send, recv)

    return pl.pallas_call(
        body, name=ex.name, in_specs=[ANY] * n_in, out_specs=[ANY] * n_out, out_shape=ex.out_shapes,
        input_output_aliases=ex.aliases, scratch_shapes=ex.sems(),
    )(*ex.inputs)


def _call(body, operands, *, name, grid, in_specs, out_specs, out_shape, scratch_shapes=(), semantics, ride=None,
          aliases=None):
    aliases = dict(aliases or {})
    if ride is None:
        outs = pl.pallas_call(body, name=name, grid=grid, in_specs=in_specs, out_specs=out_specs, out_shape=out_shape,
                              input_output_aliases=aliases, scratch_shapes=list(scratch_shapes),
                              compiler_params=_params(*semantics))(*operands)
        return outs, []
    n_in, n_out, n_scr = len(in_specs), len(out_specs), len(scratch_shapes)
    r_in, r_out = len(ride.inputs), len(ride.out_shapes)

    def riding(*refs):
        refs = list(refs)
        ins, rins = refs[:n_in], refs[n_in:n_in + r_in]
        o0 = n_in + r_in
        outs, routs = refs[o0:o0 + n_out], refs[o0 + n_out:o0 + n_out + r_out]
        scr = refs[o0 + n_out + r_out:o0 + n_out + r_out + n_scr]
        send, recv = refs[-2:]
        ids = [pl.program_id(d) for d in range(len(grid))]
        first = functools.reduce(jnp.logical_and, [i == 0 for i in ids])
        last = functools.reduce(jnp.logical_and, [i == g - 1 for i, g in zip(ids, grid)])
        pl.when(first)(lambda: ride.start(rins, routs, send, recv))
        body(*ins, *outs, *scr)
        pl.when(last)(lambda: ride.finish(rins, routs, send, recv))

    res = pl.pallas_call(
        riding, name=name + "_" + ride.name, grid=grid,
        in_specs=list(in_specs) + [ANY] * r_in, out_specs=list(out_specs) + [ANY] * r_out,
        out_shape=list(out_shape) + list(ride.out_shapes),
        input_output_aliases={**aliases, **{n_in + a: n_out + b for a, b in ride.aliases.items()}},
        scratch_shapes=list(scratch_shapes) + ride.sems(),
        compiler_params=_params(*(["arbitrary"] * len(grid))),
    )(*operands, *ride.inputs)
    return res[:n_out], res[n_out:]


def _gather_ici(bufs, name="gather_ici"):
    n = len(bufs)

    def plan(ins, outs, send, recv):
        x, y, c, chips = _place()
        me = 2 * x + y
        sent, arriving = [], []
        for j, (px, py) in enumerate(chips):
            for a in range(n):
                mine, theirs = outs[a].at[:, me, c], outs[a].at[:, 2 * px + py, c]
                sent.append(_rcopy(mine, mine, send, recv, j * n + a, (px, py, c)))
                arriving.append(_rcopy(theirs, theirs, send, recv, j * n + a, (px, py, c)))
        return sent, arriving

    return _Exchange(name, bufs, [jax.ShapeDtypeStruct(b.shape, b.dtype) for b in bufs], 3 * n, plan, in_place=True)


def _gather_d2d(bufs, name="gather_d2d"):
    n = len(bufs)

    def plan(ins, outs, send, recv):
        x, y, c, chips = _place()
        sib = (x, y, 1 - c)
        sent, arriving = [], []
        for j, (px, py) in enumerate(chips):
            for a in range(n):
                got, theirs = outs[a].at[:, 2 * px + py, c], outs[a].at[:, 2 * px + py, 1 - c]
                sent.append(_rcopy(got, got, send, recv, j * n + a, sib))
                arriving.append(_rcopy(theirs, theirs, send, recv, j * n + a, sib))
        return sent, arriving

    return _Exchange(name, bufs, [jax.ShapeDtypeStruct(b.shape, b.dtype) for b in bufs], 3 * n, plan, in_place=True)


def _pair_exchange(parts):
    n = len(parts)

    def plan(ins, outs, send, recv):
        x, y, c, _ = _place()
        cps = [_rcopy(ins[a].at[:, :, 1 - c], outs[a], send, recv, a, (x, y, 1 - c)) for a in range(n)]
        return cps, cps

    return _Exchange("pair_exchange", parts,
                     [jax.ShapeDtypeStruct(p.shape[:2] + p.shape[3:], p.dtype) for p in parts], n, plan)


def _block_rows(r):
    return r if r <= 512 else r // 2


def _pair_add(part, got):
    L, K, _, r, C = part.shape
    rows = _block_rows(r)

    def body(c_ref, a_ref, b_ref, o_ref):
        o_ref[0, 0] = (a_ref[0, 0, 0] + b_ref[0, 0]).astype(bf16)

    blk = (1, 1, rows, C)
    return pl.pallas_call(
        body, name="pair_add",
        grid_spec=pltpu.PrefetchScalarGridSpec(
            num_scalar_prefetch=1, grid=(L, K, r // rows),
            in_specs=[pl.BlockSpec((1, 1, 1, rows, C), lambda l, k, i, c: (l, k, c[0], i, 0)),
                      pl.BlockSpec(blk, lambda l, k, i, c: (l, k, i, 0))],
            out_specs=pl.BlockSpec(blk, lambda l, k, i, c: (l, k, i, 0))),
        out_shape=jax.ShapeDtypeStruct((L, K, r, C), bf16),
        compiler_params=_params("parallel", "parallel", "parallel"),
    )(jnp.reshape(lax.axis_index("c"), (1,)).astype(jnp.int32), part, got)


def _chip_exchange(sums):
    n = len(sums)

    def plan(ins, outs, send, recv):
        x, y, c, chips = _place()
        cps = []
        for j, (px, py) in enumerate(chips):
            for a in range(n):
                cps.append(_rcopy(ins[a].at[:, 2 * px + py], outs[a].at[j], send, recv, j * n + a, (px, py, c)))
        return cps, cps

    return _Exchange("chip_exchange", sums,
                     [jax.ShapeDtypeStruct((3, s.shape[0]) + s.shape[2:], s.dtype) for s in sums], 3 * n, plan)


def _chip_sum(mine, got):
    L, K, r, C = mine.shape
    rows = _block_rows(r)

    def body(p_ref, a_ref, b_ref, o_ref):
        o_ref[0, 0] = (a_ref[0, 0].astype(f32) + b_ref[0, 0].astype(f32)) + (b_ref[1, 0].astype(f32) + b_ref[2, 0].astype(f32))

    place = jnp.stack([2 * lax.axis_index("x") + lax.axis_index("y"), lax.axis_index("c")]).astype(jnp.int32)
    return pl.pallas_call(
        body, name="chip_sum",
        grid_spec=pltpu.PrefetchScalarGridSpec(
            num_scalar_prefetch=1, grid=(L, r // rows),
            in_specs=[pl.BlockSpec((1, 1, rows, C), lambda l, i, p: (l, p[0], i, 0)),
                      pl.BlockSpec((3, 1, rows, C), lambda l, i, p: (0, l, i, 0))],
            out_specs=pl.BlockSpec((1, 1, rows, C), lambda l, i, p: (l, p[1], i, 0))),
        out_shape=jax.ShapeDtypeStruct((L, 2, r, C), f32),
        compiler_params=_params("parallel", "parallel"),
    )(place, mine, got)


def _pair_share(bufs):
    n = len(bufs)

    def plan(ins, outs, send, recv):
        x, y, c, _ = _place()
        sib = (x, y, 1 - c)
        sent = [_rcopy(outs[a].at[:, c], outs[a].at[:, c], send, recv, a, sib) for a in range(n)]
        arriving = [_rcopy(outs[a].at[:, 1 - c], outs[a].at[:, 1 - c], send, recv, a, sib) for a in range(n)]
        return sent, arriving

    return _Exchange("pair_share", bufs, [jax.ShapeDtypeStruct(b.shape, b.dtype) for b in bufs], n, plan, in_place=True)


def _all_sum_small(v):
    def body(v_ref, o_ref, buf, send, recv):
        x, y, c, _ = _place()
        me = 4 * x + 2 * y + c
        buf[me] = v_ref[...]
        cps = []
        for m in range(1, 8):
            to = (x ^ (m >> 2), y ^ ((m >> 1) & 1), c ^ (m & 1))
            cps.append(_rcopy(v_ref, buf.at[me], send, recv, m - 1, to))
        for cp in cps:
            cp.start()
        for cp in cps:
            cp.wait()
        acc = buf[0]
        for d in range(1, 8):
            acc = acc + buf[d]
        o_ref[...] = acc

    vm = pl.BlockSpec(memory_space=pltpu.VMEM)
    return pl.pallas_call(
        body, name="all_sum_small", in_specs=[vm], out_specs=vm,
        out_shape=jax.ShapeDtypeStruct(v.shape, v.dtype),
        scratch_shapes=[pltpu.VMEM((8,) + v.shape, v.dtype), pltpu.SemaphoreType.DMA((7,)), pltpu.SemaphoreType.DMA((7,))],
    )(v)


def _adamw_math(w, g, m, v):
    m = ADAM_B1 * m + (1.0 - ADAM_B1) * g
    v = ADAM_B2 * v + (1.0 - ADAM_B2) * (g * g)
    m_hat = m / (1.0 - ADAM_B1 ** ADAM_STEP)
    v_hat = v / (1.0 - ADAM_B2 ** ADAM_STEP)
    return -ADAM_LR * (m_hat / (jnp.sqrt(v_hat) + ADAM_EPS) + ADAM_WD * w), m, v


def _adamw(w, g, m, v):
    L, R, C = w.shape
    rows = R // 4

    def body(w_ref, g_ref, m_ref, v_ref, d_ref, mo_ref, vo_ref):
        d_ref[...], mo_ref[...], vo_ref[...] = _adamw_math(w_ref[...], g_ref[...], m_ref[...], v_ref[...])

    blk = pl.BlockSpec((1, rows, C), lambda l, i: (l, i, 0))
    return pl.pallas_call(
        body, name="adamw", grid=(L, R // rows), in_specs=[blk] * 4, out_specs=[blk] * 3,
        out_shape=[jax.ShapeDtypeStruct(w.shape, f32)] * 3,
        compiler_params=_params("parallel", "parallel"),
    )(w, g, m, v)


def _chip_index():
    return jnp.reshape(2 * lax.axis_index("x") + lax.axis_index("y"), (1,)).astype(jnp.int32)


def _shard_placed(w, l):
    _, R, C = w.shape
    rows = R // 4

    def body(k_ref, w_ref, o_ref):
        o_ref[0, 0] = w_ref[0].astype(bf16)

    return pl.pallas_call(
        body, name="shard_placed",
        grid_spec=pltpu.PrefetchScalarGridSpec(
            num_scalar_prefetch=1, grid=(R // rows,),
            in_specs=[pl.BlockSpec((1, rows, C), lambda i, k: (l, i, 0))],
            out_specs=pl.BlockSpec((1, 1, rows, C), lambda i, k: (0, k[0], i, 0))),
        out_shape=jax.ShapeDtypeStruct((1, 4, R, C), bf16),
        compiler_params=_params("parallel"),
    )(_chip_index(), w)


SMALL_ROWS = 4 * DEPTH


def _pack_small(g_pre, g_post, lb, g_head, sinks, loss=None):
    rows = []
    for l in range(DEPTH):
        tail = [g_head[l], sinks[l]]
        if loss is not None and l == 0:
            tail.append(jnp.reshape(loss, (1,)))
        tail = jnp.concatenate(tail)
        rows += [g_pre[l], g_post[l], lb[l], jnp.pad(tail, (0, D_MODEL - tail.shape[0]))]
    return jnp.stack(rows)


def _unpack_small(p):
    g_pre = jnp.stack([p[4 * l] for l in range(DEPTH)])
    g_post = jnp.stack([p[4 * l + 1] for l in range(DEPTH)])
    lb = jnp.stack([p[4 * l + 2] for l in range(DEPTH)])
    g_head = jnp.stack([p[4 * l + 3, :HG_HEAD_DIM] for l in range(DEPTH)])
    sinks = jnp.stack([p[4 * l + 3, HG_HEAD_DIM:HG_HEAD_DIM + ATT_HEADS] for l in range(DEPTH)])
    return g_pre, g_post, lb, g_head, sinks


def _small_update(gsum, w, m, v):
    def body(g_ref, w_ref, m_ref, v_ref, go_ref, d_ref, mo_ref, vo_ref):
        g = g_ref[...]
        w = w_ref[...]
        lbp = [w[4 * l + 2:4 * l + 3] for l in range(DEPTH)]
        mx = functools.reduce(jnp.maximum, lbp)
        e = [jnp.exp(t - mx) for t in lbp]
        tot = functools.reduce(jnp.add, e)
        p = [t / tot for t in e]
        glb = [g[4 * l + 2:4 * l + 3] for l in range(DEPTH)]
        row = lax.broadcasted_iota(jnp.int32, g.shape, 0)
        for j in range(DEPTH):
            gj = jnp.zeros_like(p[0])
            for l in range(DEPTH):
                for i in range(1, l + 1):
                    gj = gj + glb[l] * p[i] * ((1.0 if i == j else 0.0) - p[j])
            g = jnp.where(row == 4 * j + 2, gj, g)
        go_ref[...] = g
        d_ref[...], mo_ref[...], vo_ref[...] = _adamw_math(w, g, m_ref[...], v_ref[...])

    vm = pl.BlockSpec(memory_space=pltpu.VMEM)
    return pl.pallas_call(
        body, name="small_update", in_specs=[vm] * 4, out_specs=[vm] * 4,
        out_shape=[jax.ShapeDtypeStruct(gsum.shape, f32)] * 4,
    )(gsum, w, m, v)


def kernel(x, w_in, w_out, g_pre, g_post, lb_param, g_head, sinks, loss_target, m_w_in, m_w_out, m_g_pre, m_g_post, m_lb_param, m_g_head, m_sinks, v_w_in, v_w_out, v_g_pre, v_g_post, v_lb_param, v_g_head, v_sinks):
    B, S, _ = x.shape
    T = B * S
    L = DEPTH
    ri, ro = IN_WIDTH // 8, MIX_WIDTH // 8
    tr = lambda a: jnp.transpose(a, (0, 2, 1))
    wt, mt, vt = tr(w_in), tr(m_w_in), tr(v_w_in)
    bufs = [[_shard_placed(wt, l).reshape(1, 4, 2, ri, D_MODEL), _shard_placed(w_out, l).reshape(1, 4, 2, ro, D_MODEL)]
            for l in range(L)]
    loss, dx, grads, ggpre, ggpost, glb, gghead, gsinks = _train_step(
        x.reshape(T, D_MODEL), loss_target.reshape(T, D_MODEL), bufs, g_pre, g_post, lb_param, g_head, sinks,
        B=B, S=S, exchange=True)
    gwt_mine = jnp.concatenate([g[0] for g in grads], axis=0).reshape(L, 2 * ri, D_MODEL)
    grad_w_out = jnp.concatenate([g[1] for g in grads], axis=0).reshape(L, 2 * ro, D_MODEL)

    d_wt, nm_wt, nv_wt = _adamw(wt, gwt_mine, mt, vt)
    grad_w_in, d_w_in, nm_w_in, nv_w_in = tr(gwt_mine), tr(d_wt), tr(nm_wt), tr(nv_wt)
    d_w_out, nm_w_out, nv_w_out = _adamw(w_out, grad_w_out, m_w_out, v_w_out)

    gsum = _all_sum_small(_pack_small(ggpre, ggpost, glb, gghead, gsinks, loss))
    gs, ds, ms, vs = _small_update(
        gsum, _pack_small(g_pre, g_post, lb_param, g_head, sinks),
        _pack_small(m_g_pre, m_g_post, m_lb_param, m_g_head, m_sinks),
        _pack_small(v_g_pre, v_g_post, v_lb_param, v_g_head, v_sinks))
    loss_all = gsum[3, HG_HEAD_DIM + ATT_HEADS]
    return (loss_all, dx.reshape(B, S, D_MODEL), grad_w_in, grad_w_out, *_unpack_small(gs),
            d_w_in, d_w_out, *_unpack_small(ds), nm_w_in, nm_w_out, *_unpack_small(ms),
            nv_w_in, nv_w_out, *_unpack_small(vs))
```

```python
import functools
import math

import jax
import jax.numpy as jnp
from jax import lax
from jax.experimental import pallas as pl
from jax.experimental.pallas import tpu as pltpu

f32 = jnp.float32
bf16 = jnp.bfloat16

D_MODEL = 1024
DEPTH = 2
HG_WIDTH = 1024
HG_HEAD_DIM = 128
HG_HEADS = 8
CHUNK = 64
SUB = 16
ATT_WIDTH = 1024
ATT_HEAD_DIM = 64
ATT_HEADS = 16
ATT_GROUP = 8
KV_WIDTH = 128
ATT_BLOCK = 128
ATT_SCALE = 1.0 / math.sqrt(ATT_HEAD_DIM)
ROPE_THETA = 10000.0
IN_WIDTH = 6400
MIX_WIDTH = 2048
NORM_EPS = 1e-6
NEG_INF = -1e30
LB_FLOOR = 1e-20
LANES = 128
VMEM_LIMIT = 48 * 1024 * 1024

ADAM_LR = 0.001
ADAM_B1 = 0.9
ADAM_B2 = 0.999
ADAM_EPS = 1e-08
ADAM_WD = 0.01
ADAM_STEP = 10

QA_BLK, ZA_BLK, KV_BLK = 4, 5, 24

NT = (((1,), (1,)), ((), ()))
TN = (((0,), (0,)), ((), ()))


def _dot(a, b, dims=None, precision=None):
    if dims is None:
        return jnp.dot(a, b, preferred_element_type=f32, precision=precision)
    return lax.dot_general(a, b, dims, preferred_element_type=f32, precision=precision)


def _sigmoid(x):
    return 1.0 / (1.0 + jnp.exp(-x))


def _params(*sem):
    return pltpu.CompilerParams(dimension_semantics=sem, vmem_limit_bytes=VMEM_LIMIT)


TAIL = IN_WIDTH - 5120


def _in_proj(x, g, wt, tail, l, *, tm=1024):
    T = x.shape[0]
    tm = min(tm, T)
    nmain = 5120 // TAIL

    def body(x_ref, g_ref, w_ref, t_ref, p_ref, h_ref, hs):
        j = pl.program_id(1)

        @pl.when(j == 0)
        def _():
            xv = x_ref[...]
            r = lax.rsqrt(jnp.mean(xv * xv, axis=-1, keepdims=True) + NORM_EPS)
            hv = (xv * r * g_ref[...]).astype(bf16)
            hs[...] = hv
            h_ref[...] = hv

        @pl.when(j < nmain)
        def _():
            p_ref[...] = _dot(hs[...], w_ref[pl.ds(pl.multiple_of(j * TAIL, TAIL), TAIL), :], NT)

        @pl.when(j == nmain)
        def _():
            p_ref[...] = _dot(hs[...], t_ref[...], NT)

    resident = pl.Buffered(1)
    return pl.pallas_call(
        body, name="in_proj", grid=(T // tm, nmain + 1),
        in_specs=[pl.BlockSpec((tm, D_MODEL), lambda i, j: (i, 0)),
                  pl.BlockSpec((1, D_MODEL), lambda i, j: (0, 0)),
                  pl.BlockSpec((None, nmain * TAIL, D_MODEL), lambda i, j: (l, 0, 0), pipeline_mode=resident),
                  pl.BlockSpec((None, TAIL, D_MODEL), lambda i, j: (l, 0, 0), pipeline_mode=resident)],
        out_specs=[pl.BlockSpec((tm, TAIL), lambda i, j: (i, j)),
                   pl.BlockSpec((tm, D_MODEL), lambda i, j: (i, 0))],
        out_shape=[jax.ShapeDtypeStruct((T, IN_WIDTH), f32), jax.ShapeDtypeStruct((T, D_MODEL), bf16)],
        scratch_shapes=[pltpu.VMEM((tm, D_MODEL), bf16)],
        compiler_params=_params("parallel", "arbitrary"),
    )(x, g, wt, tail)


def _out_proj(ch, ca, wo, l, x, g, *, tm=512):
    T = x.shape[0]
    tm = min(tm, T)
    half = MIX_WIDTH // 2

    def body(ch_ref, ca_ref, wo_ref, x_ref, g_ref, xn_ref, y_ref):
        y = _dot(ch_ref[...], wo_ref[0:half, :]) + _dot(ca_ref[...], wo_ref[half:MIX_WIDTH, :])
        r = lax.rsqrt(jnp.mean(y * y, axis=-1, keepdims=True) + NORM_EPS)
        y_ref[...] = y
        xn_ref[...] = x_ref[...] + y * r * g_ref[...]

    row = lambda i: (i, 0)
    fixed = lambda i: (0, 0)
    return pl.pallas_call(
        body, name="out_proj", grid=(T // tm,),
        in_specs=[pl.BlockSpec((tm, half), row), pl.BlockSpec((tm, half), row),
                  pl.BlockSpec((None, MIX_WIDTH, D_MODEL), lambda i: (l, 0, 0)), pl.BlockSpec((tm, D_MODEL), row),
                  pl.BlockSpec((1, D_MODEL), fixed)],
        out_specs=[pl.BlockSpec((tm, D_MODEL), row), pl.BlockSpec((tm, D_MODEL), row)],
        out_shape=[jax.ShapeDtypeStruct((T, D_MODEL), f32)] * 2,
        compiler_params=_params("parallel"),
    )(ch, ca, wo, x, g)


def _loss_head(y, target, *, tm=512):
    T = y.shape[0]
    tm = min(tm, T)

    def body(y_ref, t_ref, d_ref, l_ref):
        @pl.when(pl.program_id(0) == 0)
        def _():
            l_ref[...] = jnp.zeros_like(l_ref)
        err = y_ref[...] - t_ref[...]
        d_ref[...] = err * (1.0 / D_MODEL)
        l_ref[...] += jnp.sum(err * err) * (0.5 / D_MODEL)

    row = lambda i: (i, 0)
    return pl.pallas_call(
        body, name="loss_head", grid=(T // tm,),
        in_specs=[pl.BlockSpec((tm, D_MODEL), row), pl.BlockSpec((tm, D_MODEL), row)],
        out_specs=[pl.BlockSpec((tm, D_MODEL), row), pl.BlockSpec((8, LANES), lambda i: (0, 0))],
        out_shape=[jax.ShapeDtypeStruct((T, D_MODEL), f32), jax.ShapeDtypeStruct((8, LANES), f32)],
        compiler_params=_params("arbitrary"),
    )(y, target)


def _out_proj_bwd(dxn, y, g, wo, l, ch, ca, *, tm=256, ride=None):
    T = y.shape[0]
    tm = min(tm, T)
    half = MIX_WIDTH // 2

    def body(dx_ref, y_ref, g_ref, wo_ref, ch_ref, ca_ref, dch_ref, dca_ref, dwo_ref, dg_ref):
        @pl.when(pl.program_id(0) == 0)
        def _():
            dwo_ref[...] = jnp.zeros_like(dwo_ref)
            dg_ref[...] = jnp.zeros_like(dg_ref)
        y = y_ref[...]
        dx = dx_ref[...]
        r = lax.rsqrt(jnp.mean(y * y, axis=-1, keepdims=True) + NORM_EPS)
        gy = dx * g_ref[...]
        dy = r * gy - y * (r * r * r) * jnp.mean(gy * y, axis=-1, keepdims=True)
        dg_ref[...] += jnp.sum(dx * y * r, axis=0, keepdims=True)
        dyb = dy.astype(bf16)
        dch_ref[...] = _dot(dyb, wo_ref[0:half, :], NT)
        dca_ref[...] = _dot(dyb, wo_ref[half:MIX_WIDTH, :], NT)
        dwo_ref[0:half, :] += _dot(ch_ref[...], dyb, TN)
        dwo_ref[half:MIX_WIDTH, :] += _dot(ca_ref[...], dyb, TN)

    row = lambda i: (i, 0)
    fixed = lambda i: (0, 0)
    return _call(
        body, (dxn, y, g, wo, ch, ca), name="out_proj_bwd", grid=(T // tm,),
        in_specs=[pl.BlockSpec((tm, D_MODEL), row), pl.BlockSpec((tm, D_MODEL), row),
                  pl.BlockSpec((1, D_MODEL), fixed), pl.BlockSpec((None, MIX_WIDTH, D_MODEL), lambda i: (l, 0, 0)),
                  pl.BlockSpec((tm, half), row), pl.BlockSpec((tm, half), row)],
        out_specs=[pl.BlockSpec((tm, half), row), pl.BlockSpec((tm, half), row),
                   pl.BlockSpec((MIX_WIDTH, D_MODEL), fixed), pl.BlockSpec((1, D_MODEL), fixed)],
        out_shape=[jax.ShapeDtypeStruct((T, half), f32), jax.ShapeDtypeStruct((T, half), f32),
                   jax.ShapeDtypeStruct((MIX_WIDTH, D_MODEL), f32), jax.ShapeDtypeStruct((1, D_MODEL), f32)],
        semantics=("arbitrary",), ride=ride)


TILE = 256
PIECE_TILES = (4, 4, 4, 4, 4, 1, 4)
PIECE_START = tuple(sum(PIECE_TILES[:p]) for p in range(len(PIECE_TILES)))
N_TILES = sum(PIECE_TILES)


def _piece_specs(rows, index):
    def spec(s, n):
        def index_map(*g):
            r, t = index(*g)
            return r, jnp.clip(t - s, 0, n - 1)
        return pl.BlockSpec((rows, TILE), index_map)
    return [spec(s, n) for s, n in zip(PIECE_START, PIECE_TILES)]


def _for_piece(t, fn):
    for p, (s, n) in enumerate(zip(PIECE_START, PIECE_TILES)):
        pl.when((t >= s) & (t < s + n))(functools.partial(fn, p))


def _in_proj_bwd(pieces, wt, l, x, g, dxn, *, tm=512, blocks=None, dx_into=None, ride=None):
    T = x.shape[0]
    tm = min(tm, T)
    first, count = blocks or (0, T // tm)
    npc = len(pieces)
    starts = [sum(p.shape[1] for p in pieces[:i]) for i in range(npc)]
    extra = [] if dx_into is None else [dx_into]

    def body(*refs):
        dp_refs = refs[:npc]
        w_ref, x_ref, g_ref, dxn_ref = refs[npc:npc + 4]
        dx_ref, dg_ref = refs[npc + 4 + len(extra):]

        @pl.when(pl.program_id(0) == 0)
        def _():
            dg_ref[...] = jnp.zeros_like(dg_ref)
        dh = None
        for p in range(npc):
            term = _dot(dp_refs[p][...], w_ref[starts[p]:starts[p] + pieces[p].shape[1], :])
            dh = term if dh is None else dh + term
        xv = x_ref[...]
        r = lax.rsqrt(jnp.mean(xv * xv, axis=-1, keepdims=True) + NORM_EPS)
        gy = dh * g_ref[...]
        dx_ref[...] = dxn_ref[...] + r * gy - xv * (r * r * r) * jnp.mean(gy * xv, axis=-1, keepdims=True)
        dg_ref[...] += jnp.sum(dh * xv * r, axis=0, keepdims=True)

    rows = lambda i: (first + i, 0)
    return _call(
        body, (*pieces, wt, x, g, dxn, *extra), name="in_proj_bwd", grid=(count,),
        in_specs=[pl.BlockSpec((tm, p.shape[1]), rows) for p in pieces] + [
            pl.BlockSpec((None, IN_WIDTH, D_MODEL), lambda i: (l, 0, 0), pipeline_mode=pl.Buffered(1)),
            pl.BlockSpec((tm, D_MODEL), rows), pl.BlockSpec((1, D_MODEL), lambda i: (0, 0)),
            pl.BlockSpec((tm, D_MODEL), rows)] + [ANY] * len(extra),
        out_specs=[pl.BlockSpec((tm, D_MODEL), rows), pl.BlockSpec((1, D_MODEL), lambda i: (0, 0))],
        out_shape=[jax.ShapeDtypeStruct((T, D_MODEL), f32), jax.ShapeDtypeStruct((1, D_MODEL), f32)],
        semantics=("arbitrary",), ride=ride, aliases={npc + 4: 0} if extra else None)


def _grad_w_in(h, pieces, *, ride=None):
    T = h.shape[0]
    npc = len(pieces)

    def body(*refs):
        h_ref, dp_refs, o_ref = refs[0], refs[1:1 + npc], refs[1 + npc]

        def put(p):
            o_ref[...] = _dot(dp_refs[p][...], h_ref[...], TN)
        _for_piece(pl.program_id(0), put)

    return _call(
        body, (h, *pieces), name="grad_w_in", grid=(N_TILES,),
        in_specs=[pl.BlockSpec((T, D_MODEL), lambda j: (0, 0), pipeline_mode=pl.Buffered(1))]
        + _piece_specs(T, lambda j: (0, j)),
        out_specs=[pl.BlockSpec((TILE, D_MODEL), lambda j: (j, 0))],
        out_shape=[jax.ShapeDtypeStruct((IN_WIDTH, D_MODEL), f32)],
        semantics=("parallel",), ride=ride)


def _lower_bound(lbp, layer):
    m = jnp.max(lbp, axis=0, keepdims=True)
    e = jnp.exp(lbp - m)
    p = e / jnp.sum(e, axis=0, keepdims=True)
    acc = p[0:1]
    for i in range(1, layer + 1):
        acc = acc + p[i:i + 1]
    return acc - p[0:1]


def _gate_parts(qr, fr, lb, lbf):
    sq = _sigmoid(qr)
    e = jnp.exp(-jnp.abs(fr))
    inv = 1.0 / (1.0 + e)
    pos = fr >= 0
    sg = jnp.where(pos, inv, e * inv)
    nsg = jnp.where(pos, e * inv, inv)
    fg = lbf + (1.0 - lb) * sg
    return qr * sq, sq, sg, nsg, fg, jnp.log(fg), (1.0 - lb) * nsg


LEVELS = tuple(SUB << j for j in range((CHUNK // SUB).bit_length() - 1))


def _level_masks(transposed=False):
    t = lax.broadcasted_iota(jnp.int32, (CHUNK, CHUNK), 1 if transposed else 0)
    s = lax.broadcasted_iota(jnp.int32, (CHUNK, CHUNK), 0 if transposed else 1)
    return [(t % (2 * m) >= m) & (s % (2 * m) < m) & (t // (2 * m) == s // (2 * m)) for m in LEVELS]


def _level_anchor(b_s, row, m):
    beta = b_s[m - 1:m, :]
    for g in range(1, CHUNK // (2 * m)):
        beta = jnp.where(row >= g * 2 * m, b_s[g * 2 * m + m - 1:g * 2 * m + m, :], beta)
    return beta


FWD_INTERLEAVE = 16
BWD_INTERLEAVE = 8
SWA_INTERLEAVE = 4


def _interleaved(chunks, width):
    for g0 in range(0, len(chunks), width):
        live = chunks[g0:g0 + width]
        while live:
            for gen in list(live):
                try:
                    next(gen)
                except StopIteration:
                    live.remove(gen)


def _seg_sum(seg, x):
    hi = x.astype(bf16)
    return _dot(seg, hi) + _dot(seg, (x - hi.astype(f32)).astype(bf16))


def _hgrn_fwd(proj, lb_param, g_head, *, B, S, layer, ride=None):
    T = B * S
    TB = min(1024, S)
    nT, NC = S // TB, TB // CHUNK
    nC = S // CHUNK
    HD = HG_HEAD_DIM

    def body(q_ref, f_ref, i_ref, z_ref, lb_ref, gh_ref, cat_ref, op_ref, st_ref,
             s_scr, b_scr, k_scr):
        @pl.when(pl.program_id(2) == 0)
        def _():
            s_scr[...] = jnp.zeros_like(s_scr)
        lb = _lower_bound(lb_ref[...], layer)
        lbf = jnp.maximum(lb, LB_FLOOR)
        gh = gh_ref[...]
        r_i = lax.broadcasted_iota(jnp.int32, (CHUNK, CHUNK), 0)
        c_i = lax.broadcasted_iota(jnp.int32, (CHUNK, CHUNK), 1)
        tril = (r_i >= c_i).astype(bf16)
        rows8 = lax.broadcasted_iota(jnp.int32, (8, HD), 0)
        row_c = lax.broadcasted_iota(jnp.int32, (CHUNK, HD), 0)
        lane_c = lax.broadcasted_iota(jnp.int32, (8, CHUNK), 1)
        masks = _level_masks()

        def chunk(c, carried):
            rs = slice(c * CHUNK, (c + 1) * CHUNK)
            b_s, k_s = b_scr.at[c], k_scr.at[c]
            q, _, _, _, _, logf, k = _gate_parts(q_ref[rs, :], f_ref[rs, :], lb, lbf)
            v = i_ref[rs, :]
            b = _seg_sum(tril, logf)
            b_s[...] = b
            k_s[...] = k
            yield
            pieces = []
            for blk in range(CHUNK // SUB):
                r0 = blk * SUB
                bp = [b[r0 + 8 * i:r0 + 8 * i + 8] for i in range(SUB // 8)]
                qp = [q[r0 + 8 * i:r0 + 8 * i + 8] for i in range(SUB // 8)]
                ap = [jnp.zeros((8, CHUNK), f32) for _ in range(SUB // 8)]
                for s in range(SUB):
                    bs = b_s[r0 + s:r0 + s + 1, :]
                    ks = k_s[r0 + s:r0 + s + 1, :]
                    for i in range(s // 8, SUB // 8):
                        diff = bp[i] - bs
                        if i == s // 8:
                            diff = jnp.where(rows8 >= s - 8 * i, diff, NEG_INF)
                        col = jnp.sum(jnp.exp(diff) * qp[i] * ks, axis=1, keepdims=True)
                        ap[i] = jnp.where(lane_c == r0 + s, col, ap[i])
                pieces += ap
                yield
            a_all = jnp.concatenate(pieces, axis=0)
            for m, mk in zip(LEVELS, masks):
                beta = _level_anchor(b_s, row_c, m)
                qh = (q * jnp.exp(jnp.minimum(b - beta, 0.0))).astype(bf16)
                kh = (k * jnp.exp(jnp.minimum(beta - b, 0.0))).astype(bf16)
                a_all = a_all + jnp.where(mk, _dot(qh, kh, NT), 0.0)
            yield
            st = carried[0]
            st_ref[0, 0, c] = st
            vb16 = v.astype(bf16)
            o = _dot(a_all.astype(bf16), vb16) + _dot((q * jnp.exp(b)).astype(bf16), st.astype(bf16), NT)
            b_end = b_s[CHUNK - 1:CHUNK, :]
            kdec = (k * jnp.exp(b_end - b)).astype(bf16)
            carried[0] = jnp.exp(b_end) * st + _dot(vb16, kdec, TN)
            rr = lax.rsqrt(jnp.mean(o * o, axis=-1, keepdims=True) + NORM_EPS)
            zr = z_ref[rs, :]
            cat_ref[rs, :] = (o * rr * gh * (zr * _sigmoid(zr))).astype(bf16)
            op_ref[rs, :] = o

        carried = [s_scr[...]]
        _interleaved([chunk(c, carried) for c in range(NC)], FWD_INTERLEAVE)
        s_scr[...] = carried[0]

    def col(part):
        return pl.BlockSpec((TB, HD), lambda b, h, n: (b * nT + n, part * HG_HEADS + h))

    out_col = pl.BlockSpec((TB, HD), lambda b, h, n: (b * nT + n, h))
    return _call(
        body, (proj, proj, proj, proj, lb_param, g_head),
        name=f"hgrn_fwd_l{layer}", grid=(B, HG_HEADS, nT),
        in_specs=[col(0), col(1), col(2), col(3),
                  pl.BlockSpec((DEPTH, HD), lambda b, h, n: (0, h)),
                  pl.BlockSpec((1, HD), lambda b, h, n: (0, 0))],
        out_specs=[out_col, out_col,
                   pl.BlockSpec((1, 1, NC, HD, HD), lambda b, h, n: (b, h, n, 0, 0))],
        out_shape=[jax.ShapeDtypeStruct((T, HG_WIDTH), bf16), jax.ShapeDtypeStruct((T, HG_WIDTH), f32),
                   jax.ShapeDtypeStruct((B, HG_HEADS, nC, HD, HD), f32)],
        scratch_shapes=[pltpu.VMEM((HD, HD), f32), pltpu.VMEM((NC, CHUNK, HD), f32), pltpu.VMEM((NC, CHUNK, HD), f32)],
        semantics=("parallel", "parallel", "arbitrary"), ride=ride)


def _hgrn_bwd(proj, lb_param, g_head, o_pre, states, dcat, *, B, S, layer, ride=None):
    T = B * S
    TB = min(1024, S)
    nT, NC = S // TB, TB // CHUNK
    HD = HG_HEAD_DIM

    def body(q_ref, f_ref, i_ref, z_ref, lb_ref, gh_ref, op_ref, st_ref, dc_ref,
             dq_ref, df_ref, di_ref, dz_ref, dlb_ref, dgh_ref,
             ds_scr, b_scr, q_scr, do_scr, wk_scr):
        @pl.when(pl.program_id(2) == 0)
        def _():
            ds_scr[...] = jnp.zeros_like(ds_scr)
            dlb_ref[...] = jnp.zeros_like(dlb_ref)
            dgh_ref[...] = jnp.zeros_like(dgh_ref)
        lb = _lower_bound(lb_ref[...], layer)
        lbf = jnp.maximum(lb, LB_FLOOR)
        ind = (lb > LB_FLOOR).astype(f32)
        gh = gh_ref[...]
        r_i = lax.broadcasted_iota(jnp.int32, (CHUNK, CHUNK), 0)
        c_i = lax.broadcasted_iota(jnp.int32, (CHUNK, CHUNK), 1)
        tril = (r_i >= c_i).astype(bf16)
        triu = (c_i >= r_i).astype(bf16)
        rows8 = lax.broadcasted_iota(jnp.int32, (8, HD), 0)
        row_c = lax.broadcasted_iota(jnp.int32, (CHUNK, HD), 0)
        lane_c = lax.broadcasted_iota(jnp.int32, (8, CHUNK), 1)
        last_row = row_c == CHUNK - 1
        masks = _level_masks()
        masks_t = _level_masks(transposed=True)
        seg_t = lax.broadcasted_iota(jnp.int32, (SUB, 8 * SUB), 0)
        seg_r = lax.broadcasted_iota(jnp.int32, (SUB, 8 * SUB), 1) // 8
        seg0 = (seg_r == seg_t).astype(bf16)
        seg1 = (seg_r[:, 0:4 * SUB] + 8 == seg_t[:, 0:4 * SUB]).astype(bf16)

        def chunk(c, carried):
            rs = slice(c * CHUNK, (c + 1) * CHUNK)
            b_s, q_s, do_s = b_scr.at[c], q_scr.at[c], do_scr.at[c]
            qr, fr = q_ref[rs, :], f_ref[rs, :]
            q, sq, sg, nsg, fg, logf, k = _gate_parts(qr, fr, lb, lbf)
            v = i_ref[rs, :]
            b = _seg_sum(tril, logf)
            o = op_ref[rs, :]
            dc = dc_ref[rs, :]
            zr = z_ref[rs, :]
            sz = _sigmoid(zr)
            rr = lax.rsqrt(jnp.mean(o * o, axis=-1, keepdims=True) + NORM_EPS)
            dz_ref[rs, :] = (dc * (o * rr * gh) * (sz * (1.0 + zr * (1.0 - sz)))).astype(bf16)
            dn = dc * (zr * sz)
            dgh_ref[0, 0] += jnp.sum(dn * o * rr, axis=0, keepdims=True)
            gdn = dn * gh
            d_o = rr * gdn - o * (rr * rr * rr) * jnp.mean(gdn * o, axis=-1, keepdims=True)
            b_s[...] = b
            q_s[...] = q
            do_s[...] = d_o
            dob = d_o.astype(bf16)
            vb16 = v.astype(bf16)
            d_a = _dot(dob, vb16, NT)
            yield
            d_q = jnp.zeros((CHUNK, HD), f32)
            d_k = jnp.zeros((CHUNK, HD), f32)
            at_all = jnp.zeros((CHUNK, CHUNK), f32)
            for m, mk, mkt in zip(LEVELS, masks, masks_t):
                beta = _level_anchor(b_s, row_c, m)
                eq = jnp.exp(jnp.minimum(b - beta, 0.0))
                ek = jnp.exp(jnp.minimum(beta - b, 0.0))
                qh = (q * eq).astype(bf16)
                kh = (k * ek).astype(bf16)
                at_all = at_all + jnp.where(mkt, _dot(kh, qh, NT), 0.0)
                d_aa = jnp.where(mk, d_a, 0.0).astype(bf16)
                d_q = d_q + _dot(d_aa, kh) * eq
                d_k = d_k + _dot(d_aa, qh, TN) * ek
            yield
            dq_blocks, dk_pieces, at_pieces = [], [], []
            for blk in range(CHUNK // SUB):
                r0 = blk * SUB
                wk = wk_scr.at[c * (CHUNK // SUB) + blk]
                bp = [b[r0 + 8 * i:r0 + 8 * i + 8] for i in range(SUB // 8)]
                kp = [k[r0 + 8 * i:r0 + 8 * i + 8] for i in range(SUB // 8)]
                vp = [v[r0 + 8 * i:r0 + 8 * i + 8] for i in range(SUB // 8)]
                dkp = [jnp.zeros((8, HD), f32) for _ in range(SUB // 8)]
                atp = [jnp.zeros((8, CHUNK), f32) for _ in range(SUB // 8)]
                for t in range(SUB):
                    bt = b_s[r0 + t:r0 + t + 1, :]
                    qt = q_s[r0 + t:r0 + t + 1, :]
                    dot_ = do_s[r0 + t:r0 + t + 1, :]
                    for i in range(t // 8 + 1):
                        diff = bt - bp[i]
                        if i == t // 8:
                            diff = jnp.where(rows8 <= t - 8 * i, diff, NEG_INF)
                        e = jnp.exp(diff)
                        a = jnp.sum(e * kp[i] * qt, axis=1, keepdims=True)
                        atp[i] = jnp.where(lane_c == r0 + t, a, atp[i])
                        w = jnp.sum(vp[i] * dot_, axis=1, keepdims=True) * e
                        dkp[i] = dkp[i] + w * qt
                        row = 8 * t if i == 0 else 8 * SUB + 8 * (t - 8)
                        wk[row:row + 8, :] = w * kp[i]
                dq_blk = _seg_sum(seg0, wk[0:8 * SUB, :])
                if SUB > 8:
                    dq_blk = dq_blk + _seg_sum(seg1, wk[8 * SUB:12 * SUB, :])
                dq_blocks.append(dq_blk)
                dk_pieces += dkp
                at_pieces += atp
                yield
            dst1 = carried[0]
            st0 = st_ref[0, 0, c]
            dst1b = dst1.astype(bf16)
            eb = jnp.exp(b)
            b_end = b_s[CHUNK - 1:CHUNK, :]
            edec = jnp.exp(b_end - b)
            e_end = jnp.exp(b_end)
            kdec = (k * edec).astype(bf16)
            qdec = (q * eb).astype(bf16)
            st1 = e_end * st0 + _dot(vb16, kdec, TN)
            rterm = jnp.sum(dst1 * st1, axis=0, keepdims=True)
            carried[0] = e_end * dst1 + _dot(dob, qdec, TN)
            d_q = d_q + _dot(dob, st0.astype(bf16)) * eb + jnp.concatenate(dq_blocks, axis=0)
            d_k = d_k + _dot(vb16, dst1b) * edec + jnp.concatenate(dk_pieces, axis=0)
            d_v = _dot(kdec, dst1b, NT) + _dot((at_all + jnp.concatenate(at_pieces, axis=0)).astype(bf16), dob)
            db = q * d_q - k * d_k + jnp.where(last_row, rterm, 0.0)
            dlt = _seg_sum(triu, db) - fg * d_k
            df_ref[rs, :] = (dlt * (1.0 - lb) * sg * nsg / fg).astype(bf16)
            dlb_ref[0] += jnp.sum(dlt * (ind - sg) / fg, axis=0, keepdims=True)
            dq_ref[rs, :] = (d_q * (sq * (1.0 + qr * (1.0 - sq)))).astype(bf16)
            di_ref[rs, :] = d_v.astype(bf16)

        carried = [ds_scr[...]]
        _interleaved([chunk(c, carried) for c in reversed(range(NC))], BWD_INTERLEAVE)
        ds_scr[...] = carried[0]

    def col(part):
        return pl.BlockSpec((TB, HD), lambda b, h, n: (b * nT + nT - 1 - n, part * HG_HEADS + h))

    hcol = pl.BlockSpec((TB, HD), lambda b, h, n: (b * nT + nT - 1 - n, h))
    return _call(
        body, (proj, proj, proj, proj, lb_param, g_head, o_pre, states, dcat),
        name=f"hgrn_bwd_l{layer}", grid=(B, HG_HEADS, nT),
        in_specs=[col(0), col(1), col(2), col(3),
                  pl.BlockSpec((DEPTH, HD), lambda b, h, n: (0, h)),
                  pl.BlockSpec((1, HD), lambda b, h, n: (0, 0)),
                  hcol,
                  pl.BlockSpec((1, 1, NC, HD, HD), lambda b, h, n: (b, h, nT - 1 - n, 0, 0)),
                  hcol],
        out_specs=[hcol, hcol, hcol, hcol,
                   pl.BlockSpec((1, 1, HD), lambda b, h, n: (b, 0, h)),
                   pl.BlockSpec((1, 1, 1, HD), lambda b, h, n: (b, h, 0, 0))],
        out_shape=[jax.ShapeDtypeStruct((T, HG_WIDTH), bf16)] * 4 + [
            jax.ShapeDtypeStruct((B, 1, HG_WIDTH), f32), jax.ShapeDtypeStruct((B, HG_HEADS, 1, HD), f32)],
        scratch_shapes=[pltpu.VMEM((HD, HD), f32)] + [pltpu.VMEM((NC, CHUNK, HD), f32)] * 3
        + [pltpu.VMEM((NC * CHUNK // SUB, 12 * SUB, HD), f32)],
        semantics=("parallel", "parallel", "arbitrary"), ride=ride)


def _rope_tables(S):
    half = ATT_HEAD_DIM // 2
    inv_freq = ROPE_THETA ** (-jnp.arange(half, dtype=f32) / half)
    ang = jnp.arange(S, dtype=f32)[:, None] * inv_freq[None, :]
    cos, sin = jnp.cos(ang), jnp.sin(ang)
    return jnp.tile(jnp.concatenate([cos, cos], axis=1), (1, 2)), jnp.tile(jnp.concatenate([-sin, sin], axis=1), (1, 2))


def _swap_halves(x, first_half):
    return jnp.where(first_half, pltpu.roll(x, LANES - ATT_HEAD_DIM // 2, 1), pltpu.roll(x, ATT_HEAD_DIM // 2, 1))


def _rope(x, cos, sin, first_half):
    return x * cos + _swap_halves(x, first_half) * sin


def _rope_bwd(dy, cos, sin, first_half):
    return dy * cos + _swap_halves(dy * sin, first_half)


def _attn_consts(n):
    lane = lax.broadcasted_iota(jnp.int32, (1, LANES), 1)
    low = lane < ATT_HEAD_DIM
    first_half = (lane % ATT_HEAD_DIM) < ATT_HEAD_DIM // 2
    top = lax.broadcasted_iota(jnp.int32, (LANES, 1), 0) < ATT_HEAD_DIM
    s = lax.broadcasted_iota(jnp.int32, (2 * ATT_BLOCK, ATT_BLOCK), 0)
    t = lax.broadcasted_iota(jnp.int32, (2 * ATT_BLOCK, ATT_BLOCK), 1)
    mask = (s > t) & (s <= t + ATT_BLOCK) & ((s >= ATT_BLOCK) | (n > 0))
    return low, first_half, top, mask


def _dup_kv(x, low):
    rolled = pltpu.roll(x, ATT_HEAD_DIM, 1)
    return [jnp.where(low, x, rolled), jnp.where(low, rolled, x)]


def _attn_head(qtm, kd, vdt, sink, mask):
    s = jnp.where(mask, _dot(kd, qtm) * ATT_SCALE, NEG_INF)
    m = jnp.maximum(jnp.max(s, axis=0, keepdims=True), sink)
    p = jnp.exp(s - m)
    psink = jnp.exp(sink - m)
    inv = 1.0 / (jnp.sum(p, axis=0, keepdims=True) + psink)
    pn = p * inv
    return pn, psink * inv, _dot(vdt, pn.astype(bf16))


def _swa_fwd(proj, sink_b, cos, sin, *, B, S, ride=None):
    T = B * S
    L = ATT_BLOCK
    nB = S // L

    def body(q_ref, z_ref, kvc_ref, kvp_ref, sk_ref, cc_ref, sc_ref, cp_ref, sp_ref, cat_ref):
        n = pl.program_id(1)
        low, first_half, top, mask = _attn_consts(n)
        cc, sc = cc_ref[...], sc_ref[...]
        kc = _rope(kvc_ref[:, 0:LANES], cc, sc, first_half)
        kp = _rope(kvp_ref[:, 0:LANES], cp_ref[...], sp_ref[...], first_half)
        kd = [x.astype(bf16) for x in _dup_kv(jnp.concatenate([kp, kc], axis=0), low)]
        vdt = [x.T.astype(bf16) for x in _dup_kv(jnp.concatenate([kvp_ref[:, LANES:2 * LANES], kvc_ref[:, LANES:2 * LANES]], axis=0), low)]
        def head_pair(pair):
            cols = slice(pair * LANES, (pair + 1) * LANES)
            j = (2 * pair) // ATT_GROUP
            qt = _rope(q_ref[:, cols], cc, sc, first_half).T
            yield
            outs = []
            for hh in range(2):
                h = 2 * pair + hh
                qtm = jnp.where(top if hh == 0 else ~top, qt, 0.0).astype(bf16)
                outs.append(_attn_head(qtm, kd[j], vdt[j], sk_ref[h:h + 1, 0:1], mask)[2])
                yield
            zp = z_ref[:, cols]
            cat_ref[:, cols] = (jnp.where(top, outs[0], outs[1]).T * (zp * _sigmoid(zp))).astype(bf16)

        _interleaved([head_pair(p) for p in range(ATT_HEADS // 2)], SWA_INTERLEAVE)

    cur = lambda b, n: (b * nB + n, 0)
    return _call(
        body, (proj, proj, proj, proj, sink_b, cos, sin, cos, sin), name="swa_fwd", grid=(B, nB),
        in_specs=[pl.BlockSpec((L, ATT_WIDTH), lambda b, n: (b * nB + n, QA_BLK)),
                  pl.BlockSpec((L, ATT_WIDTH), lambda b, n: (b * nB + n, ZA_BLK)),
                  pl.BlockSpec((L, 2 * KV_WIDTH), lambda b, n: (b * nB + n, KV_BLK)),
                  pl.BlockSpec((L, 2 * KV_WIDTH), lambda b, n: (b * nB + jnp.maximum(n - 1, 0), KV_BLK)),
                  pl.BlockSpec((ATT_HEADS, LANES), lambda b, n: (0, 0)),
                  pl.BlockSpec((L, LANES), lambda b, n: (n, 0)), pl.BlockSpec((L, LANES), lambda b, n: (n, 0)),
                  pl.BlockSpec((L, LANES), lambda b, n: (jnp.maximum(n - 1, 0), 0)),
                  pl.BlockSpec((L, LANES), lambda b, n: (jnp.maximum(n - 1, 0), 0))],
        out_specs=[pl.BlockSpec((L, ATT_WIDTH), cur)],
        out_shape=[jax.ShapeDtypeStruct((T, ATT_WIDTH), bf16)],
        semantics=("parallel", "parallel"), ride=ride)


def _swa_bwd(proj, sink_b, cos, sin, dcat, *, B, S, ride=None):
    T = B * S
    L = ATT_BLOCK
    nB = S // L

    def body(q_ref, z_ref, kvc_ref, kvp_ref, sk_ref, cc_ref, sc_ref, cp_ref, sp_ref, dc_ref,
             dq_ref, dz_ref, dkv_ref, dsk_ref, carry, ds_st, pn_st, q_st, do_st):
        step = pl.program_id(1)
        n = nB - 1 - step

        @pl.when((pl.program_id(0) == 0) & (step == 0))
        def _():
            dsk_ref[...] = jnp.zeros_like(dsk_ref)

        @pl.when(step == 0)
        def _():
            carry[...] = jnp.zeros_like(carry)
        low, first_half, top, mask = _attn_consts(n)
        cc, sc, cp, sp = cc_ref[...], sc_ref[...], cp_ref[...], sp_ref[...]
        kc = _rope(kvc_ref[:, 0:LANES], cc, sc, first_half)
        kp = _rope(kvp_ref[:, 0:LANES], cp, sp, first_half)
        kdf = _dup_kv(jnp.concatenate([kp, kc], axis=0), low)
        vdf = _dup_kv(jnp.concatenate([kvp_ref[:, LANES:2 * LANES], kvc_ref[:, LANES:2 * LANES]], axis=0), low)
        kd = [x.astype(bf16) for x in kdf]
        vd = [x.astype(bf16) for x in vdf]
        kdt = [x.T.astype(bf16) for x in kdf]
        vdt = [x.T.astype(bf16) for x in vdf]
        dkd, dvd = [], []
        def head_pair(pair):
            cols = slice(pair * LANES, (pair + 1) * LANES)
            j = (2 * pair) // ATT_GROUP
            qp = _rope(q_ref[:, cols], cc, sc, first_half)
            qt = qp.T
            zp = z_ref[:, cols]
            dc = dc_ref[:, cols]
            sz = _sigmoid(zp)
            d_o = dc * (zp * sz)
            dot_ = d_o.T
            yield
            res = []
            for hh in range(2):
                rsel = top if hh == 0 else ~top
                qtm = jnp.where(rsel, qt, 0.0).astype(bf16)
                pn, psn, o = _attn_head(qtm, kd[j], vdt[j], sk_ref[2 * pair + hh:2 * pair + hh + 1, 0:1], mask)
                res.append((rsel, pn, psn, o))
                yield
            ot = jnp.where(top, res[0][3], res[1][3])
            dz_ref[:, cols] = (dc * ot.T * (sz * (1.0 + zp * (1.0 - sz)))).astype(bf16)
            dqts = []
            for hh in range(2):
                h = 2 * pair + hh
                rsel, pn, psn, _ = res[hh]
                lsel = low if hh == 0 else ~low
                dotm = jnp.where(rsel, dot_, 0.0)
                delta = jnp.sum(dotm * ot, axis=0, keepdims=True)
                dst = (pn * (_dot(vd[j], dotm.astype(bf16)) - delta) * ATT_SCALE).astype(bf16)
                dsk_ref[h:h + 1, :] += jnp.zeros((1, LANES), f32) - jnp.sum(psn * delta)
                dqts.append(_dot(kdt[j], dst))
                g = h % ATT_GROUP
                ds_st[:, g * LANES:(g + 1) * LANES] = dst
                pn_st[:, g * LANES:(g + 1) * LANES] = pn.astype(bf16)
                q_st[g * LANES:(g + 1) * LANES, :] = jnp.where(lsel, qp, 0.0).astype(bf16)
                do_st[g * LANES:(g + 1) * LANES, :] = jnp.where(lsel, d_o, 0.0).astype(bf16)
                yield
            dq_ref[:, cols] = _rope_bwd(jnp.where(top, dqts[0], dqts[1]).T, cc, sc, first_half).astype(bf16)

        pairs_per_group = ATT_GROUP // 2
        for grp in range(ATT_HEADS // ATT_GROUP):
            _interleaved([head_pair(grp * pairs_per_group + p) for p in range(pairs_per_group)], SWA_INTERLEAVE)
            dkd.append(_dot(ds_st[...], q_st[...]))
            dvd.append(_dot(pn_st[...], do_st[...]))
        dk = [x + pltpu.roll(x, ATT_HEAD_DIM, 1) for x in dkd]
        dv = [x + pltpu.roll(x, ATT_HEAD_DIM, 1) for x in dvd]
        dk = jnp.where(low, dk[0], dk[1])
        dv = jnp.where(low, dv[0], dv[1])
        dkv_ref[:, 0:LANES] = (_rope_bwd(dk[L:2 * L], cc, sc, first_half) + carry[:, 0:LANES]).astype(bf16)
        dkv_ref[:, LANES:2 * LANES] = (dv[L:2 * L] + carry[:, LANES:2 * LANES]).astype(bf16)
        carry[:, 0:LANES] = _rope_bwd(dk[0:L], cp, sp, first_half)
        carry[:, LANES:2 * LANES] = dv[0:L]

    rev = lambda b, s: b * nB + nB - 1 - s
    revp = lambda b, s: b * nB + jnp.maximum(nB - 2 - s, 0)
    wide = lambda blk: pl.BlockSpec((L, ATT_WIDTH), lambda b, s: (rev(b, s), blk))
    tab = pl.BlockSpec((L, LANES), lambda b, s: (nB - 1 - s, 0))
    tabp = pl.BlockSpec((L, LANES), lambda b, s: (jnp.maximum(nB - 2 - s, 0), 0))
    return _call(
        body, (proj, proj, proj, proj, sink_b, cos, sin, cos, sin, dcat), name="swa_bwd", grid=(B, nB),
        in_specs=[wide(QA_BLK), wide(ZA_BLK),
                  pl.BlockSpec((L, 2 * KV_WIDTH), lambda b, s: (rev(b, s), KV_BLK)),
                  pl.BlockSpec((L, 2 * KV_WIDTH), lambda b, s: (revp(b, s), KV_BLK)),
                  pl.BlockSpec((ATT_HEADS, LANES), lambda b, s: (0, 0)),
                  tab, tab, tabp, tabp, wide(0)],
        out_specs=[wide(0), wide(0), pl.BlockSpec((L, 2 * KV_WIDTH), lambda b, s: (rev(b, s), 0)),
                   pl.BlockSpec((ATT_HEADS, LANES), lambda b, s: (0, 0))],
        out_shape=[jax.ShapeDtypeStruct((T, ATT_WIDTH), bf16), jax.ShapeDtypeStruct((T, ATT_WIDTH), bf16),
                   jax.ShapeDtypeStruct((T, 2 * KV_WIDTH), bf16), jax.ShapeDtypeStruct((ATT_HEADS, LANES), f32)],
        scratch_shapes=[pltpu.VMEM((L, 2 * KV_WIDTH), f32),
                        pltpu.VMEM((2 * L, ATT_GROUP * LANES), bf16), pltpu.VMEM((2 * L, ATT_GROUP * LANES), bf16),
                        pltpu.VMEM((ATT_GROUP * LANES, LANES), bf16), pltpu.VMEM((ATT_GROUP * LANES, LANES), bf16)],
        semantics=("arbitrary", "arbitrary"), ride=ride)


def _train_step(x, target, bufs, g_pre, g_post, lb_param, g_head, sinks, *, B, S, exchange):
    L = DEPTH
    T = x.shape[0]
    ri, ro = IN_WIDTH // 8, MIX_WIDTH // 8
    cos, sin = _rope_tables(S)
    full = list(bufs)
    if exchange:
        full[0] = _run_exchange(_gather_d2d(_run_exchange(_gather_ici(bufs[0]))))
    saved = []
    for l in range(L):
        wt = full[l][0].reshape(1, IN_WIDTH, D_MODEL)
        wo = full[l][1].reshape(1, MIX_WIDTH, D_MODEL)
        tail = jnp.concatenate([wt[:, 5376:6400], wt[:, 5120:5376]], axis=1)
        proj, h = _in_proj(x, g_pre[l:l + 1], wt, tail, 0)
        ahead = exchange and l + 1 < L
        (ch, o_pre, states), landed = _hgrn_fwd(proj, lb_param, g_head[l:l + 1], B=B, S=S, layer=l,
                                                ride=_gather_ici(bufs[l + 1]) if ahead else None)
        sink_b = jnp.broadcast_to(sinks[l][:, None], (ATT_HEADS, LANES))
        (ca,), passed = _swa_fwd(proj, sink_b, cos, sin, B=B, S=S, ride=_gather_d2d(landed) if ahead else None)
        if ahead:
            full[l + 1] = passed
        xn, y = _out_proj(ch, ca, wo, 0, x, g_post[l:l + 1])
        saved.append((x, proj, h, ch, o_pre, states, sink_b, ca, y, wt, tail, wo))
        x = xn
    dx, loss = _loss_head(x, target)

    def reduce_tail(sums, recv):
        return _run_exchange(_pair_share([_chip_sum(s, r) for s, r in zip(sums, recv)]))

    grads = [None] * L
    waiting = None
    gg_pre, gg_post, g_lb, gg_head, g_sinks = [], [], [], [], []
    for l in reversed(range(L)):
        x_in, proj, h, ch, o_pre, states, sink_b, ca, y, wt, tail, wo = saved[l]
        (dch, dca, dwo, dgpost), got = _out_proj_bwd(dx, y, g_post[l:l + 1], wo, 0, ch, ca,
                                                     ride=_pair_exchange(waiting) if waiting else None)
        sums = [_pair_add(p, r) for p, r in zip(waiting, got)] if waiting else None
        (dq, df, di, dz, dlb, dgh), recv = _hgrn_bwd(proj, lb_param, g_head[l:l + 1], o_pre, states, dch, B=B, S=S,
                                                     layer=l, ride=_chip_exchange(sums) if waiting else None)
        if waiting:
            grads[l + 1] = reduce_tail(sums, recv)
        at_end = exchange and l == 0
        part_o = [dwo.reshape(1, 4, 2, ro, D_MODEL)]
        (dqa, dza, dkv, dsk), got_o = _swa_bwd(proj, sink_b, cos, sin, dca, B=B, S=S,
                                              ride=_pair_exchange(part_o) if at_end else None)
        pieces = [dq, df, di, dz, dqa, dkv, dza]
        sums_o = [_pair_add(part_o[0], got_o[0])] if at_end else None
        (gwt,), recv_o = _grad_w_in(h, pieces, ride=_chip_exchange(sums_o) if at_end else None)
        part_t = [gwt.reshape(1, 4, 2, ri, D_MODEL)]
        if at_end:
            tm = min(512, T // 2)
            nb = T // tm
            na = max(1, nb // 4)
            (dx_a, dg_a), got_t = _in_proj_bwd(pieces, wt, 0, x_in, g_pre[l:l + 1], dx, tm=tm, blocks=(0, na),
                                               ride=_pair_exchange(part_t))
            sums_t = [_pair_add(part_t[0], got_t[0])]
            (dx, dg_b), recv_t = _in_proj_bwd(pieces, wt, 0, x_in, g_pre[l:l + 1], dx, tm=tm, blocks=(na, nb - na),
                                              dx_into=dx_a, ride=_chip_exchange(sums_t))
            dgpre = dg_a + dg_b
            grads[0] = reduce_tail(sums_t + sums_o, recv_t + recv_o)
        else:
            (dx, dgpre), _ = _in_proj_bwd(pieces, wt, 0, x_in, g_pre[l:l + 1], dx)
            if exchange:
                waiting = part_t + part_o
            else:
                grads[l] = [gwt, dwo]
        gg_pre.append(dgpre[0])
        gg_post.append(dgpost[0])
        g_lb.append(jnp.sum(dlb, axis=(0, 1)))
        gg_head.append(jnp.sum(dgh, axis=(0, 1, 2)))
        g_sinks.append(dsk[:, 0])
    rev = lambda xs: jnp.stack(xs[::-1])
    return loss[0, 0], dx, grads, rev(gg_pre), rev(gg_post), rev(g_lb), rev(gg_head), rev(g_sinks)


MESH = pl.DeviceIdType.MESH
ANY = pl.BlockSpec(memory_space=pl.ANY)


def _place():
    x, y, c = lax.axis_index("x"), lax.axis_index("y"), lax.axis_index("c")
    return x, y, c, [(1 - x, y), (x, 1 - y), (1 - x, 1 - y)]


def _rcopy(src, dst, send, recv, k, to):
    return pltpu.make_async_remote_copy(src_ref=src, dst_ref=dst, send_sem=send.at[k], recv_sem=recv.at[k],
                                        device_id=to, device_id_type=MESH)


class _Exchange:
    def __init__(self, name, inputs, out_shapes, n_sems, plan, in_place=False):
        self.name, self.inputs, self.out_shapes, self.n_sems, self.plan = name, inputs, out_shapes, n_sems, plan
        self.aliases = {a: a for a in range(len(inputs))} if in_place else {}

    def start(self, ins, outs, send, recv):
        for cp in self.plan(ins, outs, send, recv)[0]:
            cp.start()

    def finish(self, ins, outs, send, recv):
        sent, arriving = self.plan(ins, outs, send, recv)
        for cp in arriving:
            cp.wait_recv()
        for cp in sent:
            cp.wait_send()

    def sems(self):
        return [pltpu.SemaphoreType.DMA((self.n_sems,)), pltpu.SemaphoreType.DMA((self.n_sems,))]


def _run_exchange(ex):
    n_in, n_out = len(ex.inputs), len(ex.out_shapes)

    def body(*refs):
        ins, outs = refs[:n_in], refs[n_in:n_in + n_out]
        send, recv = refs[n_in + n_out:]
        ex.start(ins, outs, send, recv)
        ex.finish(ins, outs, send, recv)

    return pl.pallas_call(
        body, name=ex.name, in_specs=[ANY] * n_in, out_specs=[ANY] * n_out, out_shape=ex.out_shapes,
        input_output_aliases=ex.aliases, scratch_shapes=ex.sems(),
    )(*ex.inputs)


def _call(body, operands, *, name, grid, in_specs, out_specs, out_shape, scratch_shapes=(), semantics, ride=None,
          aliases=None):
    aliases = dict(aliases or {})
    if ride is None:
        outs = pl.pallas_call(body, name=name, grid=grid, in_specs=in_specs, out_specs=out_specs, out_shape=out_shape,
                              input_output_aliases=aliases, scratch_shapes=list(scratch_shapes),
                              compiler_params=_params(*semantics))(*operands)
        return outs, []
    n_in, n_out, n_scr = len(in_specs), len(out_specs), len(scratch_shapes)
    r_in, r_out = len(ride.inputs), len(ride.out_shapes)

    def riding(*refs):
        refs = list(refs)
        ins, rins = refs[:n_in], refs[n_in:n_in + r_in]
        o0 = n_in + r_in
        outs, routs = refs[o0:o0 + n_out], refs[o0 + n_out:o0 + n_out + r_out]
        scr = refs[o0 + n_out + r_out:o0 + n_out + r_out + n_scr]
        send, recv = refs[-2:]
        ids = [pl.program_id(d) for d in range(len(grid))]
        first = functools.reduce(jnp.logical_and, [i == 0 for i in ids])
        last = functools.reduce(jnp.logical_and, [i == g - 1 for i, g in zip(ids, grid)])
        pl.when(first)(lambda: ride.start(rins, routs, send, recv))
        body(*ins, *outs, *scr)
        pl.when(last)(lambda: ride.finish(rins, routs, send, recv))

    res = pl.pallas_call(
        riding, name=name + "_" + ride.name, grid=grid,
        in_specs=list(in_specs) + [ANY] * r_in, out_specs=list(out_specs) + [ANY] * r_out,
        out_shape=list(out_shape) + list(ride.out_shapes),
        input_output_aliases={**aliases, **{n_in + a: n_out + b for a, b in ride.aliases.items()}},
        scratch_shapes=list(scratch_shapes) + ride.sems(),
        compiler_params=_params(*(["arbitrary"] * len(grid))),
    )(*operands, *ride.inputs)
    return res[:n_out], res[n_out:]


def _gather_ici(bufs, name="gather_ici"):
    n = len(bufs)

    def plan(ins, outs, send, recv):
        x, y, c, chips = _place()
        me = 2 * x + y
        sent, arriving = [], []
        for j, (px, py) in enumerate(chips):
            for a in range(n):
                mine, theirs = outs[a].at[:, me, c], outs[a].at[:, 2 * px + py, c]
                sent.append(_rcopy(mine, mine, send, recv, j * n + a, (px, py, c)))
                arriving.append(_rcopy(theirs, theirs, send, recv, j * n + a, (px, py, c)))
        return sent, arriving

    return _Exchange(name, bufs, [jax.ShapeDtypeStruct(b.shape, b.dtype) for b in bufs], 3 * n, plan, in_place=True)


def _gather_d2d(bufs, name="gather_d2d"):
    n = len(bufs)

    def plan(ins, outs, send, recv):
        x, y, c, chips = _place()
        sib = (x, y, 1 - c)
        sent, arriving = [], []
        for j, (px, py) in enumerate(chips):
            for a in range(n):
                got, theirs = outs[a].at[:, 2 * px + py, c], outs[a].at[:, 2 * px + py, 1 - c]
                sent.append(_rcopy(got, got, send, recv, j * n + a, sib))
                arriving.append(_rcopy(theirs, theirs, send, recv, j * n + a, sib))
        return sent, arriving

    return _Exchange(name, bufs, [jax.ShapeDtypeStruct(b.shape, b.dtype) for b in bufs], 3 * n, plan, in_place=True)


def _pair_exchange(parts):
    n = len(parts)

    def plan(ins, outs, send, recv):
        x, y, c, _ = _place()
        cps = [_rcopy(ins[a].at[:, :, 1 - c], outs[a], send, recv, a, (x, y, 1 - c)) for a in range(n)]
        return cps, cps

    return _Exchange("pair_exchange", parts,
                     [jax.ShapeDtypeStruct(p.shape[:2] + p.shape[3:], p.dtype) for p in parts], n, plan)


def _block_rows(r):
    return r if r <= 512 else r // 2


def _pair_add(part, got):
    L, K, _, r, C = part.shape
    rows = _block_rows(r)

    def body(c_ref, a_ref, b_ref, o_ref):
        o_ref[0, 0] = (a_ref[0, 0, 0] + b_ref[0, 0]).astype(bf16)

    blk = (1, 1, rows, C)
    return pl.pallas_call(
        body, name="pair_add",
        grid_spec=pltpu.PrefetchScalarGridSpec(
            num_scalar_prefetch=1, grid=(L, K, r // rows),
            in_specs=[pl.BlockSpec((1, 1, 1, rows, C), lambda l, k, i, c: (l, k, c[0], i, 0)),
                      pl.BlockSpec(blk, lambda l, k, i, c: (l, k, i, 0))],
            out_specs=pl.BlockSpec(blk, lambda l, k, i, c: (l, k, i, 0))),
        out_shape=jax.ShapeDtypeStruct((L, K, r, C), bf16),
        compiler_params=_params("parallel", "parallel", "parallel"),
    )(jnp.reshape(lax.axis_index("c"), (1,)).astype(jnp.int32), part, got)


def _chip_exchange(sums):
    n = len(sums)

    def plan(ins, outs, send, recv):
        x, y, c, chips = _place()
        cps = []
        for j, (px, py) in enumerate(chips):
            for a in range(n):
                cps.append(_rcopy(ins[a].at[:, 2 * px + py], outs[a].at[j], send, recv, j * n + a, (px, py, c)))
        return cps, cps

    return _Exchange("chip_exchange", sums,
                     [jax.ShapeDtypeStruct((3, s.shape[0]) + s.shape[2:], s.dtype) for s in sums], 3 * n, plan)


def _chip_sum(mine, got):
    L, K, r, C = mine.shape
    rows = _block_rows(r)

    def body(p_ref, a_ref, b_ref, o_ref):
        o_ref[0, 0] = (a_ref[0, 0].astype(f32) + b_ref[0, 0].astype(f32)) + (b_ref[1, 0].astype(f32) + b_ref[2, 0].astype(f32))

    place = jnp.stack([2 * lax.axis_index("x") + lax.axis_index("y"), lax.axis_index("c")]).astype(jnp.int32)
    return pl.pallas_call(
        body, name="chip_sum",
        grid_spec=pltpu.PrefetchScalarGridSpec(
            num_scalar_prefetch=1, grid=(L, r // rows),
            in_specs=[pl.BlockSpec((1, 1, rows, C), lambda l, i, p: (l, p[0], i, 0)),
                      pl.BlockSpec((3, 1, rows, C), lambda l, i, p: (0, l, i, 0))],
            out_specs=pl.BlockSpec((1, 1, rows, C), lambda l, i, p: (l, p[1], i, 0))),
        out_shape=jax.ShapeDtypeStruct((L, 2, r, C), f32),
        compiler_params=_params("parallel", "parallel"),
    )(place, mine, got)


def _pair_share(bufs):
    n = len(bufs)

    def plan(ins, outs, send, recv):
        x, y, c, _ = _place()
        sib = (x, y, 1 - c)
        sent = [_rcopy(outs[a].at[:, c], outs[a].at[:, c], send, recv, a, sib) for a in range(n)]
        arriving = [_rcopy(outs[a].at[:, 1 - c], outs[a].at[:, 1 - c], send, recv, a, sib) for a in range(n)]
        return sent, arriving

    return _Exchange("pair_share", bufs, [jax.ShapeDtypeStruct(b.shape, b.dtype) for b in bufs], n, plan, in_place=True)


def _all_sum_small(v):
    def body(v_ref, o_ref, buf, send, recv):
        x, y, c, _ = _place()
        me = 4 * x + 2 * y + c
        buf[me] = v_ref[...]
        cps = []
        for m in range(1, 8):
            to = (x ^ (m >> 2), y ^ ((m >> 1) & 1), c ^ (m & 1))
            cps.append(_rcopy(v_ref, buf.at[me], send, recv, m - 1, to))
        for cp in cps:
            cp.start()
        for cp in cps:
            cp.wait()
        acc = buf[0]
        for d in range(1, 8):
            acc = acc + buf[d]
        o_ref[...] = acc

    vm = pl.BlockSpec(memory_space=pltpu.VMEM)
    return pl.pallas_call(
        body, name="all_sum_small", in_specs=[vm], out_specs=vm,
        out_shape=jax.ShapeDtypeStruct(v.shape, v.dtype),
        scratch_shapes=[pltpu.VMEM((8,) + v.shape, v.dtype), pltpu.SemaphoreType.DMA((7,)), pltpu.SemaphoreType.DMA((7,))],
    )(v)


def _adamw_math(w, g, m, v):
    m = ADAM_B1 * m + (1.0 - ADAM_B1) * g
    v = ADAM_B2 * v + (1.0 - ADAM_B2) * (g * g)
    m_hat = m / (1.0 - ADAM_B1 ** ADAM_STEP)
    v_hat = v / (1.0 - ADAM_B2 ** ADAM_STEP)
    return -ADAM_LR * (m_hat / (jnp.sqrt(v_hat) + ADAM_EPS) + ADAM_WD * w), m, v


def _adamw(w, g, m, v):
    L, R, C = w.shape
    rows = R // 4

    def body(w_ref, g_ref, m_ref, v_ref, d_ref, mo_ref, vo_ref):
        d_ref[...], mo_ref[...], vo_ref[...] = _adamw_math(w_ref[...], g_ref[...], m_ref[...], v_ref[...])

    blk = pl.BlockSpec((1, rows, C), lambda l, i: (l, i, 0))
    return pl.pallas_call(
        body, name="adamw", grid=(L, R // rows), in_specs=[blk] * 4, out_specs=[blk] * 3,
        out_shape=[jax.ShapeDtypeStruct(w.shape, f32)] * 3,
        compiler_params=_params("parallel", "parallel"),
    )(w, g, m, v)


def _chip_index():
    return jnp.reshape(2 * lax.axis_index("x") + lax.axis_index("y"), (1,)).astype(jnp.int32)


def _shard_placed(w, l):
    _, R, C = w.shape
    rows = R // 4

    def body(k_ref, w_ref, o_ref):
        o_ref[0, 0] = w_ref[0].astype(bf16)

    return pl.pallas_call(
        body, name="shard_placed",
        grid_spec=pltpu.PrefetchScalarGridSpec(
            num_scalar_prefetch=1, grid=(R // rows,),
            in_specs=[pl.BlockSpec((1, rows, C), lambda i, k: (l, i, 0))],
            out_specs=pl.BlockSpec((1, 1, rows, C), lambda i, k: (0, k[0], i, 0))),
        out_shape=jax.ShapeDtypeStruct((1, 4, R, C), bf16),
        compiler_params=_params("parallel"),
    )(_chip_index(), w)


SMALL_ROWS = 4 * DEPTH


def _pack_small(g_pre, g_post, lb, g_head, sinks, loss=None):
    rows = []
    for l in range(DEPTH):
        tail = [g_head[l], sinks[l]]
        if loss is not None and l == 0:
            tail.append(jnp.reshape(loss, (1,)))
        tail = jnp.concatenate(tail)
        rows += [g_pre[l], g_post[l], lb[l], jnp.pad(tail, (0, D_MODEL - tail.shape[0]))]
    return jnp.stack(rows)


def _unpack_small(p):
    g_pre = jnp.stack([p[4 * l] for l in range(DEPTH)])
    g_post = jnp.stack([p[4 * l + 1] for l in range(DEPTH)])
    lb = jnp.stack([p[4 * l + 2] for l in range(DEPTH)])
    g_head = jnp.stack([p[4 * l + 3, :HG_HEAD_DIM] for l in range(DEPTH)])
    sinks = jnp.stack([p[4 * l + 3, HG_HEAD_DIM:HG_HEAD_DIM + ATT_HEADS] for l in range(DEPTH)])
    return g_pre, g_post, lb, g_head, sinks


def _small_update(gsum, w, m, v):
    def body(g_ref, w_ref, m_ref, v_ref, go_ref, d_ref, mo_ref, vo_ref):
        g = g_ref[...]
        w = w_ref[...]
        lbp = [w[4 * l + 2:4 * l + 3] for l in range(DEPTH)]
        mx = functools.reduce(jnp.maximum, lbp)
        e = [jnp.exp(t - mx) for t in lbp]
        tot = functools.reduce(jnp.add, e)
        p = [t / tot for t in e]
        glb = [g[4 * l + 2:4 * l + 3] for l in range(DEPTH)]
        row = lax.broadcasted_iota(jnp.int32, g.shape, 0)
        for j in range(DEPTH):
            gj = jnp.zeros_like(p[0])
            for l in range(DEPTH):
                for i in range(1, l + 1):
                    gj = gj + glb[l] * p[i] * ((1.0 if i == j else 0.0) - p[j])
            g = jnp.where(row == 4 * j + 2, gj, g)
        go_ref[...] = g
        d_ref[...], mo_ref[...], vo_ref[...] = _adamw_math(w, g, m_ref[...], v_ref[...])

    vm = pl.BlockSpec(memory_space=pltpu.VMEM)
    return pl.pallas_call(
        body, name="small_update", in_specs=[vm] * 4, out_specs=[vm] * 4,
        out_shape=[jax.ShapeDtypeStruct(gsum.shape, f32)] * 4,
    )(gsum, w, m, v)


def kernel(x, w_in, w_out, g_pre, g_post, lb_param, g_head, sinks, loss_target, m_w_in, m_w_out, m_g_pre, m_g_post, m_lb_param, m_g_head, m_sinks, v_w_in, v_w_out, v_g_pre, v_g_post, v_lb_param, v_g_head, v_sinks):
    B, S, _ = x.shape
    T = B * S
    L = DEPTH
    ri, ro = IN_WIDTH // 8, MIX_WIDTH // 8
    tr = lambda a: jnp.transpose(a, (0, 2, 1))
    wt, mt, vt = tr(w_in), tr(m_w_in), tr(v_w_in)
    bufs = [[_shard_placed(wt, l).reshape(1, 4, 2, ri, D_MODEL), _shard_placed(w_out, l).reshape(1, 4, 2, ro, D_MODEL)]
            for l in range(L)]
    loss, dx, grads, ggpre, ggpost, glb, gghead, gsinks = _train_step(
        x.reshape(T, D_MODEL), loss_target.reshape(T, D_MODEL), bufs, g_pre, g_post, lb_param, g_head, sinks,
        B=B, S=S, exchange=True)
    gwt_mine = jnp.concatenate([g[0] for g in grads], axis=0).reshape(L, 2 * ri, D_MODEL)
    grad_w_out = jnp.concatenate([g[1] for g in grads], axis=0).reshape(L, 2 * ro, D_MODEL)

    d_wt, nm_wt, nv_wt = _adamw(wt, gwt_mine, mt, vt)
    grad_w_in, d_w_in, nm_w_in, nv_w_in = tr(gwt_mine), tr(d_wt), tr(nm_wt), tr(nv_wt)
    d_w_out, nm_w_out, nv_w_out = _adamw(w_out, grad_w_out, m_w_out, v_w_out)

    gsum = _all_sum_small(_pack_small(ggpre, ggpost, glb, gghead, gsinks, loss))
    gs, ds, ms, vs = _small_update(
        gsum, _pack_small(g_pre, g_post, lb_param, g_head, sinks),
        _pack_small(m_g_pre, m_g_post, m_lb_param, m_g_head, m_sinks),
        _pack_small(v_g_pre, v_g_post, v_lb_param, v_g_head, v_sinks))
    loss_all = gsum[3, HG_HEAD_DIM + ATT_HEADS]
    return (loss_all, dx.reshape(B, S, D_MODEL), grad_w_in, grad_w_out, *_unpack_small(gs),
            d_w_in, d_w_out, *_unpack_small(ds), nm_w_in, nm_w_out, *_unpack_small(ms),
            nv_w_in, nv_w_out, *_unpack_small(vs))
```

```python
import functools
import math

import jax
import jax.numpy as jnp
from jax import lax
from jax.experimental import pallas as pl
from jax.experimental.pallas import tpu as pltpu

f32 = jnp.float32
bf16 = jnp.bfloat16

D_MODEL = 1024
DEPTH = 2
HG_WIDTH = 1024
HG_HEAD_DIM = 128
HG_HEADS = 8
CHUNK = 64
SUB = 16
ATT_WIDTH = 1024
ATT_HEAD_DIM = 64
ATT_HEADS = 16
ATT_GROUP = 8
KV_WIDTH = 128
ATT_BLOCK = 128
ATT_SCALE = 1.0 / math.sqrt(ATT_HEAD_DIM)
ROPE_THETA = 10000.0
IN_WIDTH = 6400
MIX_WIDTH = 2048
NORM_EPS = 1e-6
NEG_INF = -1e30
LB_FLOOR = 1e-20
LANES = 128
VMEM_LIMIT = 48 * 1024 * 1024

ADAM_LR = 0.001
ADAM_B1 = 0.9
ADAM_B2 = 0.999
ADAM_EPS = 1e-08
ADAM_WD = 0.01
ADAM_STEP = 10

QA_BLK, ZA_BLK, KV_BLK = 4, 5, 24

NT = (((1,), (1,)), ((), ()))
TN = (((0,), (0,)), ((), ()))


def _dot(a, b, dims=None):
    if dims is None:
        return jnp.dot(a, b, preferred_element_type=f32)
    return lax.dot_general(a, b, dims, preferred_element_type=f32)


def _sigmoid(x):
    return 1.0 / (1.0 + jnp.exp(-x))


def _params(*sem):
    return pltpu.CompilerParams(dimension_semantics=sem, vmem_limit_bytes=VMEM_LIMIT)


TAIL = IN_WIDTH - 5120


def _in_proj(x, g, wt, tail, l, *, tm=1024, ride=None):
    T = x.shape[0]
    tm = min(tm, T)
    nmain = 5120 // TAIL

    def body(x_ref, g_ref, w_ref, t_ref, p_ref, h_ref, hs):
        j = pl.program_id(1)

        @pl.when(j == 0)
        def _():
            xv = x_ref[...]
            r = lax.rsqrt(jnp.mean(xv * xv, axis=-1, keepdims=True) + NORM_EPS)
            hv = (xv * r * g_ref[...]).astype(bf16)
            hs[...] = hv
            h_ref[...] = hv

        @pl.when(j < nmain)
        def _():
            p_ref[...] = _dot(hs[...], w_ref[pl.ds(pl.multiple_of(j * TAIL, TAIL), TAIL), :], NT)

        @pl.when(j == nmain)
        def _():
            p_ref[...] = _dot(hs[...], t_ref[...], NT)

    resident = pl.Buffered(1)
    return _call(
        body, (x, g, wt, tail), name="in_proj", grid=(T // tm, nmain + 1),
        in_specs=[pl.BlockSpec((tm, D_MODEL), lambda i, j: (i, 0)),
                  pl.BlockSpec((1, D_MODEL), lambda i, j: (0, 0)),
                  pl.BlockSpec((None, nmain * TAIL, D_MODEL), lambda i, j: (l, 0, 0), pipeline_mode=resident),
                  pl.BlockSpec((None, TAIL, D_MODEL), lambda i, j: (l, 0, 0), pipeline_mode=resident)],
        out_specs=[pl.BlockSpec((tm, TAIL), lambda i, j: (i, j)),
                   pl.BlockSpec((tm, D_MODEL), lambda i, j: (i, 0))],
        out_shape=[jax.ShapeDtypeStruct((T, IN_WIDTH), f32), jax.ShapeDtypeStruct((T, D_MODEL), bf16)],
        scratch_shapes=[pltpu.VMEM((tm, D_MODEL), bf16)],
        semantics=("parallel", "arbitrary"), ride=ride)


def _out_proj(ch, ca, wo, l, x, g, *, tm=512):
    T = x.shape[0]
    tm = min(tm, T)
    half = MIX_WIDTH // 2

    def body(ch_ref, ca_ref, wo_ref, x_ref, g_ref, xn_ref, y_ref):
        y = _dot(ch_ref[...], wo_ref[0:half, :]) + _dot(ca_ref[...], wo_ref[half:MIX_WIDTH, :])
        r = lax.rsqrt(jnp.mean(y * y, axis=-1, keepdims=True) + NORM_EPS)
        y_ref[...] = y
        xn_ref[...] = x_ref[...] + y * r * g_ref[...]

    row = lambda i: (i, 0)
    fixed = lambda i: (0, 0)
    return pl.pallas_call(
        body, name="out_proj", grid=(T // tm,),
        in_specs=[pl.BlockSpec((tm, half), row), pl.BlockSpec((tm, half), row),
                  pl.BlockSpec((None, MIX_WIDTH, D_MODEL), lambda i: (l, 0, 0)), pl.BlockSpec((tm, D_MODEL), row),
                  pl.BlockSpec((1, D_MODEL), fixed)],
        out_specs=[pl.BlockSpec((tm, D_MODEL), row), pl.BlockSpec((tm, D_MODEL), row)],
        out_shape=[jax.ShapeDtypeStruct((T, D_MODEL), f32)] * 2,
        compiler_params=_params("parallel"),
    )(ch, ca, wo, x, g)


def _loss_head(y, target, *, tm=512):
    T = y.shape[0]
    tm = min(tm, T)

    def body(y_ref, t_ref, d_ref, l_ref):
        @pl.when(pl.program_id(0) == 0)
        def _():
            l_ref[...] = jnp.zeros_like(l_ref)
        err = y_ref[...] - t_ref[...]
        d_ref[...] = err * (1.0 / D_MODEL)
        l_ref[...] += jnp.sum(err * err) * (0.5 / D_MODEL)

    row = lambda i: (i, 0)
    return pl.pallas_call(
        body, name="loss_head", grid=(T // tm,),
        in_specs=[pl.BlockSpec((tm, D_MODEL), row), pl.BlockSpec((tm, D_MODEL), row)],
        out_specs=[pl.BlockSpec((tm, D_MODEL), row), pl.BlockSpec((8, LANES), lambda i: (0, 0))],
        out_shape=[jax.ShapeDtypeStruct((T, D_MODEL), f32), jax.ShapeDtypeStruct((8, LANES), f32)],
        compiler_params=_params("arbitrary"),
    )(y, target)


def _out_proj_bwd(dxn, y, g, wo, l, ch, ca, *, tm=256, ride=None):
    T = y.shape[0]
    tm = min(tm, T)
    half = MIX_WIDTH // 2

    def body(dx_ref, y_ref, g_ref, wo_ref, ch_ref, ca_ref, dch_ref, dca_ref, dwo_ref, dg_ref):
        @pl.when(pl.program_id(0) == 0)
        def _():
            dwo_ref[...] = jnp.zeros_like(dwo_ref)
            dg_ref[...] = jnp.zeros_like(dg_ref)
        y = y_ref[...]
        dx = dx_ref[...]
        r = lax.rsqrt(jnp.mean(y * y, axis=-1, keepdims=True) + NORM_EPS)
        gy = dx * g_ref[...]
        dy = r * gy - y * (r * r * r) * jnp.mean(gy * y, axis=-1, keepdims=True)
        dg_ref[...] += jnp.sum(dx * y * r, axis=0, keepdims=True)
        dyb = dy.astype(bf16)
        dch_ref[...] = _dot(dyb, wo_ref[0:half, :], NT)
        dca_ref[...] = _dot(dyb, wo_ref[half:MIX_WIDTH, :], NT)
        dwo_ref[0:half, :] += _dot(ch_ref[...], dyb, TN)
        dwo_ref[half:MIX_WIDTH, :] += _dot(ca_ref[...], dyb, TN)

    row = lambda i: (i, 0)
    fixed = lambda i: (0, 0)
    return _call(
        body, (dxn, y, g, wo, ch, ca), name="out_proj_bwd", grid=(T // tm,),
        in_specs=[pl.BlockSpec((tm, D_MODEL), row), pl.BlockSpec((tm, D_MODEL), row),
                  pl.BlockSpec((1, D_MODEL), fixed), pl.BlockSpec((None, MIX_WIDTH, D_MODEL), lambda i: (l, 0, 0)),
                  pl.BlockSpec((tm, half), row), pl.BlockSpec((tm, half), row)],
        out_specs=[pl.BlockSpec((tm, half), row), pl.BlockSpec((tm, half), row),
                   pl.BlockSpec((MIX_WIDTH, D_MODEL), fixed), pl.BlockSpec((1, D_MODEL), fixed)],
        out_shape=[jax.ShapeDtypeStruct((T, half), f32), jax.ShapeDtypeStruct((T, half), f32),
                   jax.ShapeDtypeStruct((MIX_WIDTH, D_MODEL), f32), jax.ShapeDtypeStruct((1, D_MODEL), f32)],
        semantics=("arbitrary",), ride=ride)


TILE = 256
PIECE_TILES = (4, 4, 4, 4, 4, 1, 4)
PIECE_START = tuple(sum(PIECE_TILES[:p]) for p in range(len(PIECE_TILES)))
N_TILES = sum(PIECE_TILES)


def _piece_specs(rows, index):
    def spec(s, n):
        def index_map(*g):
            r, t = index(*g)
            return r, jnp.clip(t - s, 0, n - 1)
        return pl.BlockSpec((rows, TILE), index_map)
    return [spec(s, n) for s, n in zip(PIECE_START, PIECE_TILES)]


def _for_piece(t, fn):
    for p, (s, n) in enumerate(zip(PIECE_START, PIECE_TILES)):
        pl.when((t >= s) & (t < s + n))(functools.partial(fn, p))


def _in_proj_bwd(pieces, wt, l, x, g, dxn, *, tm=512, blocks=None, dx_into=None, ride=None):
    T = x.shape[0]
    tm = min(tm, T)
    first, count = blocks or (0, T // tm)
    npc = len(pieces)
    starts = [sum(p.shape[1] for p in pieces[:i]) for i in range(npc)]
    extra = [] if dx_into is None else [dx_into]

    def body(*refs):
        dp_refs = refs[:npc]
        w_ref, x_ref, g_ref, dxn_ref = refs[npc:npc + 4]
        dx_ref, dg_ref = refs[npc + 4 + len(extra):]

        @pl.when(pl.program_id(0) == 0)
        def _():
            dg_ref[...] = jnp.zeros_like(dg_ref)
        dh = None
        for p in range(npc):
            term = _dot(dp_refs[p][...], w_ref[starts[p]:starts[p] + pieces[p].shape[1], :])
            dh = term if dh is None else dh + term
        xv = x_ref[...]
        r = lax.rsqrt(jnp.mean(xv * xv, axis=-1, keepdims=True) + NORM_EPS)
        gy = dh * g_ref[...]
        dx_ref[...] = dxn_ref[...] + r * gy - xv * (r * r * r) * jnp.mean(gy * xv, axis=-1, keepdims=True)
        dg_ref[...] += jnp.sum(dh * xv * r, axis=0, keepdims=True)

    rows = lambda i: (first + i, 0)
    return _call(
        body, (*pieces, wt, x, g, dxn, *extra), name="in_proj_bwd", grid=(count,),
        in_specs=[pl.BlockSpec((tm, p.shape[1]), rows) for p in pieces] + [
            pl.BlockSpec((None, IN_WIDTH, D_MODEL), lambda i: (l, 0, 0), pipeline_mode=pl.Buffered(1)),
            pl.BlockSpec((tm, D_MODEL), rows), pl.BlockSpec((1, D_MODEL), lambda i: (0, 0)),
            pl.BlockSpec((tm, D_MODEL), rows)] + [ANY] * len(extra),
        out_specs=[pl.BlockSpec((tm, D_MODEL), rows), pl.BlockSpec((1, D_MODEL), lambda i: (0, 0))],
        out_shape=[jax.ShapeDtypeStruct((T, D_MODEL), f32), jax.ShapeDtypeStruct((1, D_MODEL), f32)],
        semantics=("arbitrary",), ride=ride, aliases={npc + 4: 0} if extra else None)


def _grad_w_in(h, pieces, *, ride=None):
    T = h.shape[0]
    npc = len(pieces)

    def body(*refs):
        h_ref, dp_refs, o_ref = refs[0], refs[1:1 + npc], refs[1 + npc]

        def put(p):
            o_ref[...] = _dot(dp_refs[p][...], h_ref[...], TN)
        _for_piece(pl.program_id(0), put)

    return _call(
        body, (h, *pieces), name="grad_w_in", grid=(N_TILES,),
        in_specs=[pl.BlockSpec((T, D_MODEL), lambda j: (0, 0), pipeline_mode=pl.Buffered(1))]
        + _piece_specs(T, lambda j: (0, j)),
        out_specs=[pl.BlockSpec((TILE, D_MODEL), lambda j: (j, 0))],
        out_shape=[jax.ShapeDtypeStruct((IN_WIDTH, D_MODEL), f32)],
        semantics=("parallel",), ride=ride)


def _lower_bound(lbp, layer):
    m = jnp.max(lbp, axis=0, keepdims=True)
    e = jnp.exp(lbp - m)
    p = e / jnp.sum(e, axis=0, keepdims=True)
    acc = p[0:1]
    for i in range(1, layer + 1):
        acc = acc + p[i:i + 1]
    return acc - p[0:1]


def _gate_parts(qr, fr, lb, lbf):
    sq = _sigmoid(qr)
    e = jnp.exp(-jnp.abs(fr))
    inv = 1.0 / (1.0 + e)
    pos = fr >= 0
    sg = jnp.where(pos, inv, e * inv)
    nsg = jnp.where(pos, e * inv, inv)
    fg = lbf + (1.0 - lb) * sg
    return qr * sq, sq, sg, nsg, fg, jnp.log(fg), (1.0 - lb) * nsg


LEVELS = tuple(SUB << j for j in range((CHUNK // SUB).bit_length() - 1))


def _level_masks(transposed=False):
    t = lax.broadcasted_iota(jnp.int32, (CHUNK, CHUNK), 1 if transposed else 0)
    s = lax.broadcasted_iota(jnp.int32, (CHUNK, CHUNK), 0 if transposed else 1)
    return [(t % (2 * m) >= m) & (s % (2 * m) < m) & (t // (2 * m) == s // (2 * m)) for m in LEVELS]


def _level_anchor(b_s, row, m):
    beta = b_s[m - 1:m, :]
    for g in range(1, CHUNK // (2 * m)):
        beta = jnp.where(row >= g * 2 * m, b_s[g * 2 * m + m - 1:g * 2 * m + m, :], beta)
    return beta


FWD_INTERLEAVE = 16
BWD_INTERLEAVE = 8
SWA_INTERLEAVE = 4


def _interleaved(chunks, width):
    for g0 in range(0, len(chunks), width):
        live = chunks[g0:g0 + width]
        while live:
            for gen in list(live):
                try:
                    next(gen)
                except StopIteration:
                    live.remove(gen)


def _seg_sum(seg, x):
    hi = x.astype(bf16)
    return _dot(seg, hi) + _dot(seg, (x - hi.astype(f32)).astype(bf16))


def _hgrn_fwd(proj, lb_param, g_head, *, B, S, layer, ride=None):
    T = B * S
    TB = min(1024, S)
    nT, NC = S // TB, TB // CHUNK
    nC = S // CHUNK
    HD = HG_HEAD_DIM

    def body(q_ref, f_ref, i_ref, z_ref, lb_ref, gh_ref, cat_ref, op_ref, st_ref,
             s_scr, b_scr, k_scr):
        @pl.when(pl.program_id(2) == 0)
        def _():
            s_scr[...] = jnp.zeros_like(s_scr)
        lb = _lower_bound(lb_ref[...], layer)
        lbf = jnp.maximum(lb, LB_FLOOR)
        gh = gh_ref[...]
        r_i = lax.broadcasted_iota(jnp.int32, (CHUNK, CHUNK), 0)
        c_i = lax.broadcasted_iota(jnp.int32, (CHUNK, CHUNK), 1)
        tril = (r_i >= c_i).astype(bf16)
        rows8 = lax.broadcasted_iota(jnp.int32, (8, HD), 0)
        row_c = lax.broadcasted_iota(jnp.int32, (CHUNK, HD), 0)
        lane_c = lax.broadcasted_iota(jnp.int32, (8, CHUNK), 1)
        masks = _level_masks()

        def chunk(c, carried):
            rs = slice(c * CHUNK, (c + 1) * CHUNK)
            b_s, k_s = b_scr.at[c], k_scr.at[c]
            q, _, _, _, _, logf, k = _gate_parts(q_ref[rs, :], f_ref[rs, :], lb, lbf)
            v = i_ref[rs, :]
            b = _seg_sum(tril, logf)
            b_s[...] = b
            k_s[...] = k
            yield
            pieces = []
            for blk in range(CHUNK // SUB):
                r0 = blk * SUB
                bp = [b[r0 + 8 * i:r0 + 8 * i + 8] for i in range(SUB // 8)]
                qp = [q[r0 + 8 * i:r0 + 8 * i + 8] for i in range(SUB // 8)]
                ap = [jnp.zeros((8, CHUNK), f32) for _ in range(SUB // 8)]
                for s in range(SUB):
                    bs = b_s[r0 + s:r0 + s + 1, :]
                    ks = k_s[r0 + s:r0 + s + 1, :]
                    for i in range(s // 8, SUB // 8):
                        diff = bp[i] - bs
                        if i == s // 8:
                            diff = jnp.where(rows8 >= s - 8 * i, diff, NEG_INF)
                        col = jnp.sum(jnp.exp(diff) * qp[i] * ks, axis=1, keepdims=True)
                        ap[i] = jnp.where(lane_c == r0 + s, col, ap[i])
                pieces += ap
                yield
            a_all = jnp.concatenate(pieces, axis=0)
            for m, mk in zip(LEVELS, masks):
                beta = _level_anchor(b_s, row_c, m)
                qh = (q * jnp.exp(jnp.minimum(b - beta, 0.0))).astype(bf16)
                kh = (k * jnp.exp(jnp.minimum(beta - b, 0.0))).astype(bf16)
                a_all = a_all + jnp.where(mk, _dot(qh, kh, NT), 0.0)
            yield
            st = carried[0]
            st_ref[0, 0, c] = st
            vb16 = v.astype(bf16)
            o = _dot(a_all.astype(bf16), vb16) + _dot((q * jnp.exp(b)).astype(bf16), st.astype(bf16), NT)
            b_end = b_s[CHUNK - 1:CHUNK, :]
            kdec = (k * jnp.exp(b_end - b)).astype(bf16)
            carried[0] = jnp.exp(b_end) * st + _dot(vb16, kdec, TN)
            rr = lax.rsqrt(jnp.mean(o * o, axis=-1, keepdims=True) + NORM_EPS)
            zr = z_ref[rs, :]
            cat_ref[rs, :] = (o * rr * gh * (zr * _sigmoid(zr))).astype(bf16)
            op_ref[rs, :] = o

        carried = [s_scr[...]]
        _interleaved([chunk(c, carried) for c in range(NC)], FWD_INTERLEAVE)
        s_scr[...] = carried[0]

    def col(part):
        return pl.BlockSpec((TB, HD), lambda b, h, n: (b * nT + n, part * HG_HEADS + h))

    out_col = pl.BlockSpec((TB, HD), lambda b, h, n: (b * nT + n, h))
    return _call(
        body, (proj, proj, proj, proj, lb_param, g_head),
        name=f"hgrn_fwd_l{layer}", grid=(B, HG_HEADS, nT),
        in_specs=[col(0), col(1), col(2), col(3),
                  pl.BlockSpec((DEPTH, HD), lambda b, h, n: (0, h)),
                  pl.BlockSpec((1, HD), lambda b, h, n: (0, 0))],
        out_specs=[out_col, out_col,
                   pl.BlockSpec((1, 1, NC, HD, HD), lambda b, h, n: (b, h, n, 0, 0))],
        out_shape=[jax.ShapeDtypeStruct((T, HG_WIDTH), bf16), jax.ShapeDtypeStruct((T, HG_WIDTH), f32),
                   jax.ShapeDtypeStruct((B, HG_HEADS, nC, HD, HD), f32)],
        scratch_shapes=[pltpu.VMEM((HD, HD), f32), pltpu.VMEM((NC, CHUNK, HD), f32), pltpu.VMEM((NC, CHUNK, HD), f32)],
        semantics=("parallel", "parallel", "arbitrary"), ride=ride)


def _hgrn_bwd(proj, lb_param, g_head, o_pre, states, dcat, *, B, S, layer, ride=None):
    T = B * S
    TB = min(1024, S)
    nT, NC = S // TB, TB // CHUNK
    HD = HG_HEAD_DIM

    def body(q_ref, f_ref, i_ref, z_ref, lb_ref, gh_ref, op_ref, st_ref, dc_ref,
             dq_ref, df_ref, di_ref, dz_ref, dlb_ref, dgh_ref,
             ds_scr, b_scr, q_scr, do_scr, wk_scr):
        @pl.when(pl.program_id(2) == 0)
        def _():
            ds_scr[...] = jnp.zeros_like(ds_scr)
            dlb_ref[...] = jnp.zeros_like(dlb_ref)
            dgh_ref[...] = jnp.zeros_like(dgh_ref)
        lb = _lower_bound(lb_ref[...], layer)
        lbf = jnp.maximum(lb, LB_FLOOR)
        ind = (lb > LB_FLOOR).astype(f32)
        gh = gh_ref[...]
        r_i = lax.broadcasted_iota(jnp.int32, (CHUNK, CHUNK), 0)
        c_i = lax.broadcasted_iota(jnp.int32, (CHUNK, CHUNK), 1)
        tril = (r_i >= c_i).astype(bf16)
        triu = (c_i >= r_i).astype(bf16)
        rows8 = lax.broadcasted_iota(jnp.int32, (8, HD), 0)
        row_c = lax.broadcasted_iota(jnp.int32, (CHUNK, HD), 0)
        lane_c = lax.broadcasted_iota(jnp.int32, (8, CHUNK), 1)
        last_row = row_c == CHUNK - 1
        masks = _level_masks()
        masks_t = _level_masks(transposed=True)
        seg_t = lax.broadcasted_iota(jnp.int32, (SUB, 8 * SUB), 0)
        seg_r = lax.broadcasted_iota(jnp.int32, (SUB, 8 * SUB), 1) // 8
        seg0 = (seg_r == seg_t).astype(bf16)
        seg1 = (seg_r[:, 0:4 * SUB] + 8 == seg_t[:, 0:4 * SUB]).astype(bf16)

        def chunk(c, carried):
            rs = slice(c * CHUNK, (c + 1) * CHUNK)
            b_s, q_s, do_s = b_scr.at[c], q_scr.at[c], do_scr.at[c]
            qr, fr = q_ref[rs, :], f_ref[rs, :]
            q, sq, sg, nsg, fg, logf, k = _gate_parts(qr, fr, lb, lbf)
            v = i_ref[rs, :]
            b = _seg_sum(tril, logf)
            o = op_ref[rs, :]
            dc = dc_ref[rs, :]
            zr = z_ref[rs, :]
            sz = _sigmoid(zr)
            rr = lax.rsqrt(jnp.mean(o * o, axis=-1, keepdims=True) + NORM_EPS)
            dz_ref[rs, :] = (dc * (o * rr * gh) * (sz * (1.0 + zr * (1.0 - sz)))).astype(bf16)
            dn = dc * (zr * sz)
            dgh_ref[0, 0] += jnp.sum(dn * o * rr, axis=0, keepdims=True)
            gdn = dn * gh
            d_o = rr * gdn - o * (rr * rr * rr) * jnp.mean(gdn * o, axis=-1, keepdims=True)
            b_s[...] = b
            q_s[...] = q
            do_s[...] = d_o
            dob = d_o.astype(bf16)
            vb16 = v.astype(bf16)
            d_a = _dot(dob, vb16, NT)
            yield
            d_q = jnp.zeros((CHUNK, HD), f32)
            d_k = jnp.zeros((CHUNK, HD), f32)
            at_all = jnp.zeros((CHUNK, CHUNK), f32)
            for m, mk, mkt in zip(LEVELS, masks, masks_t):
                beta = _level_anchor(b_s, row_c, m)
                eq = jnp.exp(jnp.minimum(b - beta, 0.0))
                ek = jnp.exp(jnp.minimum(beta - b, 0.0))
                qh = (q * eq).astype(bf16)
                kh = (k * ek).astype(bf16)
                at_all = at_all + jnp.where(mkt, _dot(kh, qh, NT), 0.0)
                d_aa = jnp.where(mk, d_a, 0.0).astype(bf16)
                d_q = d_q + _dot(d_aa, kh) * eq
                d_k = d_k + _dot(d_aa, qh, TN) * ek
            yield
            dq_blocks, dk_pieces, at_pieces = [], [], []
            for blk in range(CHUNK // SUB):
                r0 = blk * SUB
                wk = wk_scr.at[c * (CHUNK // SUB) + blk]
                bp = [b[r0 + 8 * i:r0 + 8 * i + 8] for i in range(SUB // 8)]
                kp = [k[r0 + 8 * i:r0 + 8 * i + 8] for i in range(SUB // 8)]
                vp = [v[r0 + 8 * i:r0 + 8 * i + 8] for i in range(SUB // 8)]
                dkp = [jnp.zeros((8, HD), f32) for _ in range(SUB // 8)]
                atp = [jnp.zeros((8, CHUNK), f32) for _ in range(SUB // 8)]
                for t in range(SUB):
                    bt = b_s[r0 + t:r0 + t + 1, :]
                    qt = q_s[r0 + t:r0 + t + 1, :]
                    dot_ = do_s[r0 + t:r0 + t + 1, :]
                    for i in range(t // 8 + 1):
                        diff = bt - bp[i]
                        if i == t // 8:
                            diff = jnp.where(rows8 <= t - 8 * i, diff, NEG_INF)
                        e = jnp.exp(diff)
                        a = jnp.sum(e * kp[i] * qt, axis=1, keepdims=True)
                        atp[i] = jnp.where(lane_c == r0 + t, a, atp[i])
                        w = jnp.sum(vp[i] * dot_, axis=1, keepdims=True) * e
                        dkp[i] = dkp[i] + w * qt
                        row = 8 * t if i == 0 else 8 * SUB + 8 * (t - 8)
                        wk[row:row + 8, :] = w * kp[i]
                dq_blk = _seg_sum(seg0, wk[0:8 * SUB, :])
                if SUB > 8:
                    dq_blk = dq_blk + _seg_sum(seg1, wk[8 * SUB:12 * SUB, :])
                dq_blocks.append(dq_blk)
                dk_pieces += dkp
                at_pieces += atp
                yield
            dst1 = carried[0]
            st0 = st_ref[0, 0, c]
            dst1b = dst1.astype(bf16)
            eb = jnp.exp(b)
            b_end = b_s[CHUNK - 1:CHUNK, :]
            edec = jnp.exp(b_end - b)
            e_end = jnp.exp(b_end)
            kdec = (k * edec).astype(bf16)
            qdec = (q * eb).astype(bf16)
            st1 = e_end * st0 + _dot(vb16, kdec, TN)
            rterm = jnp.sum(dst1 * st1, axis=0, keepdims=True)
            carried[0] = e_end * dst1 + _dot(dob, qdec, TN)
            d_q = d_q + _dot(dob, st0.astype(bf16)) * eb + jnp.concatenate(dq_blocks, axis=0)
            d_k = d_k + _dot(vb16, dst1b) * edec + jnp.concatenate(dk_pieces, axis=0)
            d_v = _dot(kdec, dst1b, NT) + _dot((at_all + jnp.concatenate(at_pieces, axis=0)).astype(bf16), dob)
            db = q * d_q - k * d_k + jnp.where(last_row, rterm, 0.0)
            dlt = _seg_sum(triu, db) - fg * d_k
            df_ref[rs, :] = (dlt * (1.0 - lb) * sg * nsg / fg).astype(bf16)
            dlb_ref[0] += jnp.sum(dlt * (ind - sg) / fg, axis=0, keepdims=True)
            dq_ref[rs, :] = (d_q * (sq * (1.0 + qr * (1.0 - sq)))).astype(bf16)
            di_ref[rs, :] = d_v.astype(bf16)

        carried = [ds_scr[...]]
        _interleaved([chunk(c, carried) for c in reversed(range(NC))], BWD_INTERLEAVE)
        ds_scr[...] = carried[0]

    def col(part):
        return pl.BlockSpec((TB, HD), lambda b, h, n: (b * nT + nT - 1 - n, part * HG_HEADS + h))

    hcol = pl.BlockSpec((TB, HD), lambda b, h, n: (b * nT + nT - 1 - n, h))
    return _call(
        body, (proj, proj, proj, proj, lb_param, g_head, o_pre, states, dcat),
        name=f"hgrn_bwd_l{layer}", grid=(B, HG_HEADS, nT),
        in_specs=[col(0), col(1), col(2), col(3),
                  pl.BlockSpec((DEPTH, HD), lambda b, h, n: (0, h)),
                  pl.BlockSpec((1, HD), lambda b, h, n: (0, 0)),
                  hcol,
                  pl.BlockSpec((1, 1, NC, HD, HD), lambda b, h, n: (b, h, nT - 1 - n, 0, 0)),
                  hcol],
        out_specs=[hcol, hcol, hcol, hcol,
                   pl.BlockSpec((1, 1, HD), lambda b, h, n: (b, 0, h)),
                   pl.BlockSpec((1, 1, 1, HD), lambda b, h, n: (b, h, 0, 0))],
        out_shape=[jax.ShapeDtypeStruct((T, HG_WIDTH), bf16)] * 4 + [
            jax.ShapeDtypeStruct((B, 1, HG_WIDTH), f32), jax.ShapeDtypeStruct((B, HG_HEADS, 1, HD), f32)],
        scratch_shapes=[pltpu.VMEM((HD, HD), f32)] + [pltpu.VMEM((NC, CHUNK, HD), f32)] * 3
        + [pltpu.VMEM((NC * CHUNK // SUB, 12 * SUB, HD), f32)],
        semantics=("parallel", "parallel", "arbitrary"), ride=ride)


def _rope_tables(S):
    half = ATT_HEAD_DIM // 2
    inv_freq = ROPE_THETA ** (-jnp.arange(half, dtype=f32) / half)
    ang = jnp.arange(S, dtype=f32)[:, None] * inv_freq[None, :]
    cos, sin = jnp.cos(ang), jnp.sin(ang)
    return jnp.tile(jnp.concatenate([cos, cos], axis=1), (1, 2)), jnp.tile(jnp.concatenate([-sin, sin], axis=1), (1, 2))


def _swap_halves(x, first_half):
    return jnp.where(first_half, pltpu.roll(x, LANES - ATT_HEAD_DIM // 2, 1), pltpu.roll(x, ATT_HEAD_DIM // 2, 1))


def _rope(x, cos, sin, first_half):
    return x * cos + _swap_halves(x, first_half) * sin


def _rope_bwd(dy, cos, sin, first_half):
    return dy * cos + _swap_halves(dy * sin, first_half)


def _attn_consts(n):
    lane = lax.broadcasted_iota(jnp.int32, (1, LANES), 1)
    low = lane < ATT_HEAD_DIM
    first_half = (lane % ATT_HEAD_DIM) < ATT_HEAD_DIM // 2
    top = lax.broadcasted_iota(jnp.int32, (LANES, 1), 0) < ATT_HEAD_DIM
    s = lax.broadcasted_iota(jnp.int32, (2 * ATT_BLOCK, ATT_BLOCK), 0)
    t = lax.broadcasted_iota(jnp.int32, (2 * ATT_BLOCK, ATT_BLOCK), 1)
    mask = (s > t) & (s <= t + ATT_BLOCK) & ((s >= ATT_BLOCK) | (n > 0))
    return low, first_half, top, mask


def _dup_kv(x, low):
    rolled = pltpu.roll(x, ATT_HEAD_DIM, 1)
    return [jnp.where(low, x, rolled), jnp.where(low, rolled, x)]


def _attn_head(qtm, kd, vdt, sink, mask):
    s = jnp.where(mask, _dot(kd, qtm) * ATT_SCALE, NEG_INF)
    m = jnp.maximum(jnp.max(s, axis=0, keepdims=True), sink)
    p = jnp.exp(s - m)
    psink = jnp.exp(sink - m)
    inv = 1.0 / (jnp.sum(p, axis=0, keepdims=True) + psink)
    pn = p * inv
    return pn, psink * inv, _dot(vdt, pn.astype(bf16))


def _swa_fwd(proj, sink_b, cos, sin, *, B, S, ride=None):
    T = B * S
    L = ATT_BLOCK
    nB = S // L

    def body(q_ref, z_ref, kvc_ref, kvp_ref, sk_ref, cc_ref, sc_ref, cp_ref, sp_ref, cat_ref):
        n = pl.program_id(1)
        low, first_half, top, mask = _attn_consts(n)
        cc, sc = cc_ref[...], sc_ref[...]
        kc = _rope(kvc_ref[:, 0:LANES], cc, sc, first_half)
        kp = _rope(kvp_ref[:, 0:LANES], cp_ref[...], sp_ref[...], first_half)
        kd = [x.astype(bf16) for x in _dup_kv(jnp.concatenate([kp, kc], axis=0), low)]
        vdt = [x.T.astype(bf16) for x in _dup_kv(jnp.concatenate([kvp_ref[:, LANES:2 * LANES], kvc_ref[:, LANES:2 * LANES]], axis=0), low)]
        def head_pair(pair):
            cols = slice(pair * LANES, (pair + 1) * LANES)
            j = (2 * pair) // ATT_GROUP
            qt = _rope(q_ref[:, cols], cc, sc, first_half).T
            yield
            outs = []
            for hh in range(2):
                h = 2 * pair + hh
                qtm = jnp.where(top if hh == 0 else ~top, qt, 0.0).astype(bf16)
                outs.append(_attn_head(qtm, kd[j], vdt[j], sk_ref[h:h + 1, 0:1], mask)[2])
                yield
            zp = z_ref[:, cols]
            cat_ref[:, cols] = (jnp.where(top, outs[0], outs[1]).T * (zp * _sigmoid(zp))).astype(bf16)

        _interleaved([head_pair(p) for p in range(ATT_HEADS // 2)], SWA_INTERLEAVE)

    cur = lambda b, n: (b * nB + n, 0)
    return _call(
        body, (proj, proj, proj, proj, sink_b, cos, sin, cos, sin), name="swa_fwd", grid=(B, nB),
        in_specs=[pl.BlockSpec((L, ATT_WIDTH), lambda b, n: (b * nB + n, QA_BLK)),
                  pl.BlockSpec((L, ATT_WIDTH), lambda b, n: (b * nB + n, ZA_BLK)),
                  pl.BlockSpec((L, 2 * KV_WIDTH), lambda b, n: (b * nB + n, KV_BLK)),
                  pl.BlockSpec((L, 2 * KV_WIDTH), lambda b, n: (b * nB + jnp.maximum(n - 1, 0), KV_BLK)),
                  pl.BlockSpec((ATT_HEADS, LANES), lambda b, n: (0, 0)),
                  pl.BlockSpec((L, LANES), lambda b, n: (n, 0)), pl.BlockSpec((L, LANES), lambda b, n: (n, 0)),
                  pl.BlockSpec((L, LANES), lambda b, n: (jnp.maximum(n - 1, 0), 0)),
                  pl.BlockSpec((L, LANES), lambda b, n: (jnp.maximum(n - 1, 0), 0))],
        out_specs=[pl.BlockSpec((L, ATT_WIDTH), cur)],
        out_shape=[jax.ShapeDtypeStruct((T, ATT_WIDTH), bf16)],
        semantics=("parallel", "parallel"), ride=ride)


def _swa_bwd(proj, sink_b, cos, sin, dcat, *, B, S, ride=None):
    T = B * S
    L = ATT_BLOCK
    nB = S // L

    def body(q_ref, z_ref, kvc_ref, kvp_ref, sk_ref, cc_ref, sc_ref, cp_ref, sp_ref, dc_ref,
             dq_ref, dz_ref, dkv_ref, dsk_ref, carry, ds_st, pn_st, q_st, do_st):
        step = pl.program_id(1)
        n = nB - 1 - step

        @pl.when((pl.program_id(0) == 0) & (step == 0))
        def _():
            dsk_ref[...] = jnp.zeros_like(dsk_ref)

        @pl.when(step == 0)
        def _():
            carry[...] = jnp.zeros_like(carry)
        low, first_half, top, mask = _attn_consts(n)
        cc, sc, cp, sp = cc_ref[...], sc_ref[...], cp_ref[...], sp_ref[...]
        kc = _rope(kvc_ref[:, 0:LANES], cc, sc, first_half)
        kp = _rope(kvp_ref[:, 0:LANES], cp, sp, first_half)
        kdf = _dup_kv(jnp.concatenate([kp, kc], axis=0), low)
        vdf = _dup_kv(jnp.concatenate([kvp_ref[:, LANES:2 * LANES], kvc_ref[:, LANES:2 * LANES]], axis=0), low)
        kd = [x.astype(bf16) for x in kdf]
        vd = [x.astype(bf16) for x in vdf]
        kdt = [x.T.astype(bf16) for x in kdf]
        vdt = [x.T.astype(bf16) for x in vdf]
        dkd, dvd = [], []
        def head_pair(pair):
            cols = slice(pair * LANES, (pair + 1) * LANES)
            j = (2 * pair) // ATT_GROUP
            qp = _rope(q_ref[:, cols], cc, sc, first_half)
            qt = qp.T
            zp = z_ref[:, cols]
            dc = dc_ref[:, cols]
            sz = _sigmoid(zp)
            d_o = dc * (zp * sz)
            dot_ = d_o.T
            yield
            res = []
            for hh in range(2):
                rsel = top if hh == 0 else ~top
                qtm = jnp.where(rsel, qt, 0.0).astype(bf16)
                pn, psn, o = _attn_head(qtm, kd[j], vdt[j], sk_ref[2 * pair + hh:2 * pair + hh + 1, 0:1], mask)
                res.append((rsel, pn, psn, o))
                yield
            ot = jnp.where(top, res[0][3], res[1][3])
            dz_ref[:, cols] = (dc * ot.T * (sz * (1.0 + zp * (1.0 - sz)))).astype(bf16)
            dqts = []
            for hh in range(2):
                h = 2 * pair + hh
                rsel, pn, psn, _ = res[hh]
                lsel = low if hh == 0 else ~low
                dotm = jnp.where(rsel, dot_, 0.0)
                delta = jnp.sum(dotm * ot, axis=0, keepdims=True)
                dst = (pn * (_dot(vd[j], dotm.astype(bf16)) - delta) * ATT_SCALE).astype(bf16)
                dsk_ref[h:h + 1, :] += jnp.zeros((1, LANES), f32) - jnp.sum(psn * delta)
                dqts.append(_dot(kdt[j], dst))
                g = h % ATT_GROUP
                ds_st[:, g * LANES:(g + 1) * LANES] = dst
                pn_st[:, g * LANES:(g + 1) * LANES] = pn.astype(bf16)
                q_st[g * LANES:(g + 1) * LANES, :] = jnp.where(lsel, qp, 0.0).astype(bf16)
                do_st[g * LANES:(g + 1) * LANES, :] = jnp.where(lsel, d_o, 0.0).astype(bf16)
                yield
            dq_ref[:, cols] = _rope_bwd(jnp.where(top, dqts[0], dqts[1]).T, cc, sc, first_half).astype(bf16)

        pairs_per_group = ATT_GROUP // 2
        for grp in range(ATT_HEADS // ATT_GROUP):
            _interleaved([head_pair(grp * pairs_per_group + p) for p in range(pairs_per_group)], SWA_INTERLEAVE)
            dkd.append(_dot(ds_st[...], q_st[...]))
            dvd.append(_dot(pn_st[...], do_st[...]))
        dk = [x + pltpu.roll(x, ATT_HEAD_DIM, 1) for x in dkd]
        dv = [x + pltpu.roll(x, ATT_HEAD_DIM, 1) for x in dvd]
        dk = jnp.where(low, dk[0], dk[1])
        dv = jnp.where(low, dv[0], dv[1])
        dkv_ref[:, 0:LANES] = (_rope_bwd(dk[L:2 * L], cc, sc, first_half) + carry[:, 0:LANES]).astype(bf16)
        dkv_ref[:, LANES:2 * LANES] = (dv[L:2 * L] + carry[:, LANES:2 * LANES]).astype(bf16)
        carry[:, 0:LANES] = _rope_bwd(dk[0:L], cp, sp, first_half)
        carry[:, LANES:2 * LANES] = dv[0:L]

    rev = lambda b, s: b * nB + nB - 1 - s
    revp = lambda b, s: b * nB + jnp.maximum(nB - 2 - s, 0)
    wide = lambda blk: pl.BlockSpec((L, ATT_WIDTH), lambda b, s: (rev(b, s), blk))
    tab = pl.BlockSpec((L, LANES), lambda b, s: (nB - 1 - s, 0))
    tabp = pl.BlockSpec((L, LANES), lambda b, s: (jnp.maximum(nB - 2 - s, 0), 0))
    return _call(
        body, (proj, proj, proj, proj, sink_b, cos, sin, cos, sin, dcat), name="swa_bwd", grid=(B, nB),
        in_specs=[wide(QA_BLK), wide(ZA_BLK),
                  pl.BlockSpec((L, 2 * KV_WIDTH), lambda b, s: (rev(b, s), KV_BLK)),
                  pl.BlockSpec((L, 2 * KV_WIDTH), lambda b, s: (revp(b, s), KV_BLK)),
                  pl.BlockSpec((ATT_HEADS, LANES), lambda b, s: (0, 0)),
                  tab, tab, tabp, tabp, wide(0)],
        out_specs=[wide(0), wide(0), pl.BlockSpec((L, 2 * KV_WIDTH), lambda b, s: (rev(b, s), 0)),
                   pl.BlockSpec((ATT_HEADS, LANES), lambda b, s: (0, 0))],
        out_shape=[jax.ShapeDtypeStruct((T, ATT_WIDTH), bf16), jax.ShapeDtypeStruct((T, ATT_WIDTH), bf16),
                   jax.ShapeDtypeStruct((T, 2 * KV_WIDTH), bf16), jax.ShapeDtypeStruct((ATT_HEADS, LANES), f32)],
        scratch_shapes=[pltpu.VMEM((L, 2 * KV_WIDTH), f32),
                        pltpu.VMEM((2 * L, ATT_GROUP * LANES), bf16), pltpu.VMEM((2 * L, ATT_GROUP * LANES), bf16),
                        pltpu.VMEM((ATT_GROUP * LANES, LANES), bf16), pltpu.VMEM((ATT_GROUP * LANES, LANES), bf16)],
        semantics=("arbitrary", "arbitrary"), ride=ride)


def _train_step(x, target, bufs, g_pre, g_post, lb_param, g_head, sinks, *, B, S, exchange):
    L = DEPTH
    T = x.shape[0]
    ri, ro = IN_WIDTH // 8, MIX_WIDTH // 8
    cos, sin = _rope_tables(S)
    full = [list(b) for b in bufs]
    if exchange:
        full[0][0] = _run_exchange(_gather_d2d(_run_exchange(_gather_ici(bufs[0][:1]))))[0]
    saved = []
    for l in range(L):
        wt = full[l][0].reshape(1, IN_WIDTH, D_MODEL)
        tail = jnp.concatenate([wt[:, 5376:6400], wt[:, 5120:5376]], axis=1)
        first = exchange and l == 0
        ahead = exchange and l + 1 < L
        (proj, h), wo_landed = _in_proj(x, g_pre[l:l + 1], wt, tail, 0,
                                       ride=_gather_ici(bufs[0][1:], "gather_ici_wo") if first else None)
        (ch, o_pre, states), landed = _hgrn_fwd(
            proj, lb_param, g_head[l:l + 1], B=B, S=S, layer=l,
            ride=_merge(_gather_ici(bufs[l + 1]) if ahead else None,
                        _gather_d2d(wo_landed, "gather_d2d_wo") if first else None))
        if first:
            full[0][1] = landed[-1]
            landed = landed[:-1]
        wo = full[l][1].reshape(1, MIX_WIDTH, D_MODEL)
        sink_b = jnp.broadcast_to(sinks[l][:, None], (ATT_HEADS, LANES))
        (ca,), passed = _swa_fwd(proj, sink_b, cos, sin, B=B, S=S, ride=_gather_d2d(landed) if ahead else None)
        if ahead:
            full[l + 1] = list(passed)
        xn, y = _out_proj(ch, ca, wo, 0, x, g_post[l:l + 1])
        saved.append((x, proj, h, ch, o_pre, states, sink_b, ca, y, wt, tail, wo))
        x = xn
    dx, loss = _loss_head(x, target)

    def reduce_tail(sums, recv):
        return _run_exchange(_pair_share([_chip_sum(s, r) for s, r in zip(sums, recv)]))

    grads = [None] * L
    waiting = None
    gg_pre, gg_post, g_lb, gg_head, g_sinks = [], [], [], [], []
    for l in reversed(range(L)):
        x_in, proj, h, ch, o_pre, states, sink_b, ca, y, wt, tail, wo = saved[l]
        (dch, dca, dwo, dgpost), got = _out_proj_bwd(dx, y, g_post[l:l + 1], wo, 0, ch, ca,
                                                     ride=_pair_exchange(waiting) if waiting else None)
        sums = [_pair_add(p, r) for p, r in zip(waiting, got)] if waiting else None
        (dq, df, di, dz, dlb, dgh), recv = _hgrn_bwd(proj, lb_param, g_head[l:l + 1], o_pre, states, dch, B=B, S=S,
                                                     layer=l, ride=_chip_exchange(sums) if waiting else None)
        at_end = exchange and l == 0
        part_o = [dwo.reshape(1, 4, 2, ro, D_MODEL)]
        halves = [_chip_sum(s, r) for s, r in zip(sums, recv)] if waiting else None
        (dqa, dza, dkv, dsk), rode = _swa_bwd(proj, sink_b, cos, sin, dca, B=B, S=S,
                                             ride=_merge(_pair_exchange(part_o) if at_end else None,
                                                         _pair_share(halves) if waiting else None))
        got_o = rode[:1]
        if waiting:
            grads[l + 1] = list(rode[-len(halves):])
        pieces = [dq, df, di, dz, dqa, dkv, dza]
        sums_o = [_pair_add(part_o[0], got_o[0])] if at_end else None
        (gwt,), recv_o = _grad_w_in(h, pieces, ride=_chip_exchange(sums_o) if at_end else None)
        part_t = [gwt.reshape(1, 4, 2, ri, D_MODEL)]
        if at_end:
            tm = min(512, T // 2)
            nb = T // tm
            na = max(1, nb // 4)
            (dx_a, dg_a), got_t = _in_proj_bwd(pieces, wt, 0, x_in, g_pre[l:l + 1], dx, tm=tm, blocks=(0, na),
                                               ride=_pair_exchange(part_t))
            sums_t = [_pair_add(part_t[0], got_t[0])]
            (dx, dg_b), recv_t = _in_proj_bwd(pieces, wt, 0, x_in, g_pre[l:l + 1], dx, tm=tm, blocks=(na, nb - na),
                                              dx_into=dx_a, ride=_chip_exchange(sums_t))
            dgpre = dg_a + dg_b
            grads[0] = reduce_tail(sums_t + sums_o, recv_t + recv_o)
        else:
            (dx, dgpre), _ = _in_proj_bwd(pieces, wt, 0, x_in, g_pre[l:l + 1], dx)
            if exchange:
                waiting = part_t + part_o
            else:
                grads[l] = [gwt, dwo]
        gg_pre.append(dgpre[0])
        gg_post.append(dgpost[0])
        g_lb.append(jnp.sum(dlb, axis=(0, 1)))
        gg_head.append(jnp.sum(dgh, axis=(0, 1, 2)))
        g_sinks.append(dsk[:, 0])
    rev = lambda xs: jnp.stack(xs[::-1])
    return loss[0, 0], dx, grads, rev(gg_pre), rev(gg_post), rev(g_lb), rev(gg_head), rev(g_sinks)


MESH = pl.DeviceIdType.MESH
ANY = pl.BlockSpec(memory_space=pl.ANY)


def _place():
    x, y, c = lax.axis_index("x"), lax.axis_index("y"), lax.axis_index("c")
    return x, y, c, [(1 - x, y), (x, 1 - y), (1 - x, 1 - y)]


def _rcopy(src, dst, send, recv, k, to):
    return pltpu.make_async_remote_copy(src_ref=src, dst_ref=dst, send_sem=send.at[k], recv_sem=recv.at[k],
                                        device_id=to, device_id_type=MESH)


class _Exchange:
    def __init__(self, name, inputs, out_shapes, n_sems, plan, in_place=False):
        self.name, self.inputs, self.out_shapes = name, list(inputs), list(out_shapes)
        self.n_sems, self.plan = n_sems, plan
        self.aliases = {a: a for a in range(len(inputs))} if in_place else {}

    def start(self, ins, outs, send, recv):
        for cp in self.plan(ins, outs, send, recv)[0]:
            cp.start()

    def finish(self, ins, outs, send, recv):
        sent, arriving = self.plan(ins, outs, send, recv)
        for cp in arriving:
            cp.wait_recv()
        for cp in sent:
            cp.wait_send()

    def sems(self):
        return [pltpu.SemaphoreType.DMA((self.n_sems,)), pltpu.SemaphoreType.DMA((self.n_sems,))]


class _SemView:
    def __init__(self, sems, offset):
        self.sems, self.offset = sems, offset

    @property
    def at(self):
        return self

    def __getitem__(self, k):
        return self.sems.at[self.offset + k]


def _both(a, b):
    ai, ao = len(a.inputs), len(a.out_shapes)

    def plan(ins, outs, send, recv):
        sa, ra = a.plan(ins[:ai], outs[:ao], send, recv)
        sb, rb = b.plan(ins[ai:], outs[ao:], _SemView(send, a.n_sems), _SemView(recv, a.n_sems))
        return sa + sb, ra + rb

    ex = _Exchange(a.name + "_" + b.name, a.inputs + b.inputs, a.out_shapes + b.out_shapes, a.n_sems + b.n_sems, plan)
    ex.aliases = {**a.aliases, **{ai + i: ao + o for i, o in b.aliases.items()}}
    return ex


def _merge(*rides):
    rides = [r for r in rides if r is not None]
    return functools.reduce(_both, rides) if rides else None


def _run_exchange(ex):
    n_in, n_out = len(ex.inputs), len(ex.out_shapes)

    def body(*refs):
        ins, outs = refs[:n_in], refs[n_in:n_in + n_out]
        send, recv = refs[n_in + n_out:]
        ex.start(ins, outs, send, recv)
        ex.finish(ins, outs, send, recv)

    return pl.pallas_call(
        body, name=ex.name, in_specs=[ANY] * n_in, out_specs=[ANY] * n_out, out_shape=ex.out_shapes,
        input_output_aliases=ex.aliases, scratch_shapes=ex.sems(),
    )(*ex.inputs)


def _call(body, operands, *, name, grid, in_specs, out_specs, out_shape, scratch_shapes=(), semantics, ride=None,
          aliases=None):
    aliases = dict(aliases or {})
    if ride is None:
        outs = pl.pallas_call(body, name=name, grid=grid, in_specs=in_specs, out_specs=out_specs, out_shape=out_shape,
                              input_output_aliases=aliases, scratch_shapes=list(scratch_shapes),
                              compiler_params=_params(*semantics))(*operands)
        return outs, []
    n_in, n_out, n_scr = len(in_specs), len(out_specs), len(scratch_shapes)
    r_in, r_out = len(ride.inputs), len(ride.out_shapes)

    def riding(*refs):
        refs = list(refs)
        ins, rins = refs[:n_in], refs[n_in:n_in + r_in]
        o0 = n_in + r_in
        outs, routs = refs[o0:o0 + n_out], refs[o0 + n_out:o0 + n_out + r_out]
        scr = refs[o0 + n_out + r_out:o0 + n_out + r_out + n_scr]
        send, recv = refs[-2:]
        ids = [pl.program_id(d) for d in range(len(grid))]
        first = functools.reduce(jnp.logical_and, [i == 0 for i in ids])
        last = functools.reduce(jnp.logical_and, [i == g - 1 for i, g in zip(ids, grid)])
        pl.when(first)(lambda: ride.start(rins, routs, send, recv))
        body(*ins, *outs, *scr)
        pl.when(last)(lambda: ride.finish(rins, routs, send, recv))

    res = pl.pallas_call(
        riding, name=name + "_" + ride.name, grid=grid,
        in_specs=list(in_specs) + [ANY] * r_in, out_specs=list(out_specs) + [ANY] * r_out,
        out_shape=list(out_shape) + list(ride.out_shapes),
        input_output_aliases={**aliases, **{n_in + a: n_out + b for a, b in ride.aliases.items()}},
        scratch_shapes=list(scratch_shapes) + ride.sems(),
        compiler_params=_params(*(["arbitrary"] * len(grid))),
    )(*operands, *ride.inputs)
    return res[:n_out], res[n_out:]


def _gather_ici(bufs, name="gather_ici"):
    n = len(bufs)

    def plan(ins, outs, send, recv):
        x, y, c, chips = _place()
        me = 2 * x + y
        sent, arriving = [], []
        for j, (px, py) in enumerate(chips):
            for a in range(n):
                mine, theirs = outs[a].at[:, me, c], outs[a].at[:, 2 * px + py, c]
                sent.append(_rcopy(mine, mine, send, recv, j * n + a, (px, py, c)))
                arriving.append(_rcopy(theirs, theirs, send, recv, j * n + a, (px, py, c)))
        return sent, arriving

    return _Exchange(name, bufs, [jax.ShapeDtypeStruct(b.shape, b.dtype) for b in bufs], 3 * n, plan, in_place=True)


def _gather_d2d(bufs, name="gather_d2d"):
    n = len(bufs)

    def plan(ins, outs, send, recv):
        x, y, c, chips = _place()
        sib = (x, y, 1 - c)
        sent, arriving = [], []
        for j, (px, py) in enumerate(chips):
            for a in range(n):
                got, theirs = outs[a].at[:, 2 * px + py, c], outs[a].at[:, 2 * px + py, 1 - c]
                sent.append(_rcopy(got, got, send, recv, j * n + a, sib))
                arriving.append(_rcopy(theirs, theirs, send, recv, j * n + a, sib))
        return sent, arriving

    return _Exchange(name, bufs, [jax.ShapeDtypeStruct(b.shape, b.dtype) for b in bufs], 3 * n, plan, in_place=True)


def _pair_exchange(parts):
    n = len(parts)

    def plan(ins, outs, send, recv):
        x, y, c, _ = _place()
        cps = [_rcopy(ins[a].at[:, :, 1 - c], outs[a], send, recv, a, (x, y, 1 - c)) for a in range(n)]
        return cps, cps

    return _Exchange("pair_exchange", parts,
                     [jax.ShapeDtypeStruct(p.shape[:2] + p.shape[3:], p.dtype) for p in parts], n, plan)


def _block_rows(r):
    return r if r <= 512 else r // 2


def _pair_add(part, got):
    L, K, _, r, C = part.shape
    rows = _block_rows(r)

    def body(c_ref, a_ref, b_ref, o_ref):
        o_ref[0, 0] = (a_ref[0, 0, 0] + b_ref[0, 0]).astype(bf16)

    blk = (1, 1, rows, C)
    return pl.pallas_call(
        body, name="pair_add",
        grid_spec=pltpu.PrefetchScalarGridSpec(
            num_scalar_prefetch=1, grid=(L, K, r // rows),
            in_specs=[pl.BlockSpec((1, 1, 1, rows, C), lambda l, k, i, c: (l, k, c[0], i, 0)),
                      pl.BlockSpec(blk, lambda l, k, i, c: (l, k, i, 0))],
            out_specs=pl.BlockSpec(blk, lambda l, k, i, c: (l, k, i, 0))),
        out_shape=jax.ShapeDtypeStruct((L, K, r, C), bf16),
        compiler_params=_params("parallel", "parallel", "parallel"),
    )(jnp.reshape(lax.axis_index("c"), (1,)).astype(jnp.int32), part, got)


def _chip_exchange(sums):
    n = len(sums)

    def plan(ins, outs, send, recv):
        x, y, c, chips = _place()
        cps = []
        for j, (px, py) in enumerate(chips):
            for a in range(n):
                cps.append(_rcopy(ins[a].at[:, 2 * px + py], outs[a].at[j], send, recv, j * n + a, (px, py, c)))
        return cps, cps

    return _Exchange("chip_exchange", sums,
                     [jax.ShapeDtypeStruct((3, s.shape[0]) + s.shape[2:], s.dtype) for s in sums], 3 * n, plan)


def _chip_sum(mine, got):
    L, K, r, C = mine.shape
    rows = _block_rows(r)

    def body(p_ref, a_ref, b_ref, o_ref):
        o_ref[0, 0] = (a_ref[0, 0].astype(f32) + b_ref[0, 0].astype(f32)) + (b_ref[1, 0].astype(f32) + b_ref[2, 0].astype(f32))

    place = jnp.stack([2 * lax.axis_index("x") + lax.axis_index("y"), lax.axis_index("c")]).astype(jnp.int32)
    return pl.pallas_call(
        body, name="chip_sum",
        grid_spec=pltpu.PrefetchScalarGridSpec(
            num_scalar_prefetch=1, grid=(L, r // rows),
            in_specs=[pl.BlockSpec((1, 1, rows, C), lambda l, i, p: (l, p[0], i, 0)),
                      pl.BlockSpec((3, 1, rows, C), lambda l, i, p: (0, l, i, 0))],
            out_specs=pl.BlockSpec((1, 1, rows, C), lambda l, i, p: (l, p[1], i, 0))),
        out_shape=jax.ShapeDtypeStruct((L, 2, r, C), f32),
        compiler_params=_params("parallel", "parallel"),
    )(place, mine, got)


def _pair_share(bufs):
    n = len(bufs)

    def plan(ins, outs, send, recv):
        x, y, c, _ = _place()
        sib = (x, y, 1 - c)
        sent = [_rcopy(outs[a].at[:, c], outs[a].at[:, c], send, recv, a, sib) for a in range(n)]
        arriving = [_rcopy(outs[a].at[:, 1 - c], outs[a].at[:, 1 - c], send, recv, a, sib) for a in range(n)]
        return sent, arriving

    return _Exchange("pair_share", bufs, [jax.ShapeDtypeStruct(b.shape, b.dtype) for b in bufs], n, plan, in_place=True)


def _all_sum_small(v):
    def body(v_ref, o_ref, buf, send, recv):
        x, y, c, _ = _place()
        me = 4 * x + 2 * y + c
        buf[me] = v_ref[...]
        cps = []
        for m in range(1, 8):
            to = (x ^ (m >> 2), y ^ ((m >> 1) & 1), c ^ (m & 1))
            cps.append(_rcopy(v_ref, buf.at[me], send, recv, m - 1, to))
        for cp in cps:
            cp.start()
        for cp in cps:
            cp.wait()
        acc = buf[0]
        for d in range(1, 8):
            acc = acc + buf[d]
        o_ref[...] = acc

    vm = pl.BlockSpec(memory_space=pltpu.VMEM)
    return pl.pallas_call(
        body, name="all_sum_small", in_specs=[vm], out_specs=vm,
        out_shape=jax.ShapeDtypeStruct(v.shape, v.dtype),
        scratch_shapes=[pltpu.VMEM((8,) + v.shape, v.dtype), pltpu.SemaphoreType.DMA((7,)), pltpu.SemaphoreType.DMA((7,))],
    )(v)


def _adamw_math(w, g, m, v):
    m = ADAM_B1 * m + (1.0 - ADAM_B1) * g
    v = ADAM_B2 * v + (1.0 - ADAM_B2) * (g * g)
    m_hat = m / (1.0 - ADAM_B1 ** ADAM_STEP)
    v_hat = v / (1.0 - ADAM_B2 ** ADAM_STEP)
    return -ADAM_LR * (m_hat / (jnp.sqrt(v_hat) + ADAM_EPS) + ADAM_WD * w), m, v


def _adamw(w, g, m, v):
    L, R, C = w.shape
    rows = R // 4

    def body(w_ref, g_ref, m_ref, v_ref, d_ref, mo_ref, vo_ref):
        d_ref[...], mo_ref[...], vo_ref[...] = _adamw_math(w_ref[...], g_ref[...], m_ref[...], v_ref[...])

    blk = pl.BlockSpec((1, rows, C), lambda l, i: (l, i, 0))
    return pl.pallas_call(
        body, name="adamw", grid=(L, R // rows), in_specs=[blk] * 4, out_specs=[blk] * 3,
        out_shape=[jax.ShapeDtypeStruct(w.shape, f32)] * 3,
        compiler_params=_params("parallel", "parallel"),
    )(w, g, m, v)


def _chip_index():
    return jnp.reshape(2 * lax.axis_index("x") + lax.axis_index("y"), (1,)).astype(jnp.int32)


def _shard_placed(w, l):
    _, R, C = w.shape
    rows = R // 4

    def body(k_ref, w_ref, o_ref):
        o_ref[0, 0] = w_ref[0].astype(bf16)

    return pl.pallas_call(
        body, name="shard_placed",
        grid_spec=pltpu.PrefetchScalarGridSpec(
            num_scalar_prefetch=1, grid=(R // rows,),
            in_specs=[pl.BlockSpec((1, rows, C), lambda i, k: (l, i, 0))],
            out_specs=pl.BlockSpec((1, 1, rows, C), lambda i, k: (0, k[0], i, 0))),
        out_shape=jax.ShapeDtypeStruct((1, 4, R, C), bf16),
        compiler_params=_params("parallel"),
    )(_chip_index(), w)


def _pack_small(g_pre, g_post, lb, g_head, sinks, loss=None):
    rows = []
    for l in range(DEPTH):
        tail = [g_head[l], sinks[l]]
        if loss is not None and l == 0:
            tail.append(jnp.reshape(loss, (1,)))
        tail = jnp.concatenate(tail)
        rows += [g_pre[l], g_post[l], lb[l], jnp.pad(tail, (0, D_MODEL - tail.shape[0]))]
    return jnp.stack(rows)


def _unpack_small(p):
    g_pre = jnp.stack([p[4 * l] for l in range(DEPTH)])
    g_post = jnp.stack([p[4 * l + 1] for l in range(DEPTH)])
    lb = jnp.stack([p[4 * l + 2] for l in range(DEPTH)])
    g_head = jnp.stack([p[4 * l + 3, :HG_HEAD_DIM] for l in range(DEPTH)])
    sinks = jnp.stack([p[4 * l + 3, HG_HEAD_DIM:HG_HEAD_DIM + ATT_HEADS] for l in range(DEPTH)])
    return g_pre, g_post, lb, g_head, sinks


def _small_update(gsum, w, m, v):
    def body(g_ref, w_ref, m_ref, v_ref, go_ref, d_ref, mo_ref, vo_ref):
        g = g_ref[...]
        w = w_ref[...]
        lbp = [w[4 * l + 2:4 * l + 3] for l in range(DEPTH)]
        mx = functools.reduce(jnp.maximum, lbp)
        e = [jnp.exp(t - mx) for t in lbp]
        tot = functools.reduce(jnp.add, e)
        p = [t / tot for t in e]
        glb = [g[4 * l + 2:4 * l + 3] for l in range(DEPTH)]
        row = lax.broadcasted_iota(jnp.int32, g.shape, 0)
        for j in range(DEPTH):
            gj = jnp.zeros_like(p[0])
            for l in range(DEPTH):
                for i in range(1, l + 1):
                    gj = gj + glb[l] * p[i] * ((1.0 if i == j else 0.0) - p[j])
            g = jnp.where(row == 4 * j + 2, gj, g)
        go_ref[...] = g
        d_ref[...], mo_ref[...], vo_ref[...] = _adamw_math(w, g, m_ref[...], v_ref[...])

    vm = pl.BlockSpec(memory_space=pltpu.VMEM)
    return pl.pallas_call(
        body, name="small_update", in_specs=[vm] * 4, out_specs=[vm] * 4,
        out_shape=[jax.ShapeDtypeStruct(gsum.shape, f32)] * 4,
    )(gsum, w, m, v)


def kernel(x, w_in, w_out, g_pre, g_post, lb_param, g_head, sinks, loss_target, m_w_in, m_w_out, m_g_pre, m_g_post, m_lb_param, m_g_head, m_sinks, v_w_in, v_w_out, v_g_pre, v_g_post, v_lb_param, v_g_head, v_sinks):
    B, S, _ = x.shape
    T = B * S
    L = DEPTH
    ri, ro = IN_WIDTH // 8, MIX_WIDTH // 8
    tr = lambda a: jnp.transpose(a, (0, 2, 1))
    wt, mt, vt = tr(w_in), tr(m_w_in), tr(v_w_in)
    bufs = [[_shard_placed(wt, l).reshape(1, 4, 2, ri, D_MODEL), _shard_placed(w_out, l).reshape(1, 4, 2, ro, D_MODEL)]
            for l in range(L)]
    loss, dx, grads, ggpre, ggpost, glb, gghead, gsinks = _train_step(
        x.reshape(T, D_MODEL), loss_target.reshape(T, D_MODEL), bufs, g_pre, g_post, lb_param, g_head, sinks,
        B=B, S=S, exchange=True)
    gwt_mine = jnp.concatenate([g[0] for g in grads], axis=0).reshape(L, 2 * ri, D_MODEL)
    grad_w_out = jnp.concatenate([g[1] for g in grads], axis=0).reshape(L, 2 * ro, D_MODEL)

    d_wt, nm_wt, nv_wt = _adamw(wt, gwt_mine, mt, vt)
    grad_w_in, d_w_in, nm_w_in, nv_w_in = tr(gwt_mine), tr(d_wt), tr(nm_wt), tr(nv_wt)
    d_w_out, nm_w_out, nv_w_out = _adamw(w_out, grad_w_out, m_w_out, v_w_out)

    gsum = _all_sum_small(_pack_small(ggpre, ggpost, glb, gghead, gsinks, loss))
    gs, ds, ms, vs = _small_update(
        gsum, _pack_small(g_pre, g_post, lb_param, g_head, sinks),
        _pack_small(m_g_pre, m_g_post, m_lb_param, m_g_head, m_sinks),
        _pack_small(v_g_pre, v_g_post, v_lb_param, v_g_head, v_sinks))
    loss_all = gsum[3, HG_HEAD_DIM + ATT_HEADS]
    return (loss_all, dx.reshape(B, S, D_MODEL), grad_w_in, grad_w_out, *_unpack_small(gs),
            d_w_in, d_w_out, *_unpack_small(ds), nm_w_in, nm_w_out, *_unpack_small(ms),
            nv_w_in, nv_w_out, *_unpack_small(vs))
```

```python
import functools
import math

import jax
import jax.numpy as jnp
from jax import lax
from jax.experimental import pallas as pl
from jax.experimental.pallas import tpu as pltpu

f32 = jnp.float32
bf16 = jnp.bfloat16

D_MODEL = 1024
DEPTH = 2
HG_WIDTH = 1024
HG_HEAD_DIM = 128
HG_HEADS = 8
CHUNK = 64
SUB = 16
ATT_WIDTH = 1024
ATT_HEAD_DIM = 64
ATT_HEADS = 16
ATT_GROUP = 8
KV_WIDTH = 128
ATT_BLOCK = 128
ATT_SCALE = 1.0 / math.sqrt(ATT_HEAD_DIM)
ROPE_THETA = 10000.0
IN_WIDTH = 6400
MIX_WIDTH = 2048
NORM_EPS = 1e-6
NEG_INF = -1e30
LB_FLOOR = 1e-20
LANES = 128
VMEM_LIMIT = 48 * 1024 * 1024

ADAM_LR = 0.001
ADAM_B1 = 0.9
ADAM_B2 = 0.999
ADAM_EPS = 1e-08
ADAM_WD = 0.01
ADAM_STEP = 10

QA_BLK, ZA_BLK, KV_BLK = 4, 5, 24

NT = (((1,), (1,)), ((), ()))
TN = (((0,), (0,)), ((), ()))


def _dot(a, b, dims=None):
    if dims is None:
        return jnp.dot(a, b, preferred_element_type=f32)
    return lax.dot_general(a, b, dims, preferred_element_type=f32)


def _sigmoid(x):
    return 1.0 / (1.0 + jnp.exp(-x))


def _params(*sem):
    return pltpu.CompilerParams(dimension_semantics=sem, vmem_limit_bytes=VMEM_LIMIT)


TAIL = IN_WIDTH - 5120


def _in_proj(x, g, wt, tail, l, *, tm=1024, ride=None):
    T = x.shape[0]
    tm = min(tm, T)
    nmain = 5120 // TAIL

    def body(x_ref, g_ref, w_ref, t_ref, p_ref, h_ref, hs):
        j = pl.program_id(1)

        @pl.when(j == 0)
        def _():
            xv = x_ref[...]
            r = lax.rsqrt(jnp.mean(xv * xv, axis=-1, keepdims=True) + NORM_EPS)
            hv = (xv * r * g_ref[...]).astype(bf16)
            hs[...] = hv
            h_ref[...] = hv

        @pl.when(j < nmain)
        def _():
            p_ref[...] = _dot(hs[...], w_ref[pl.ds(pl.multiple_of(j * TAIL, TAIL), TAIL), :], NT)

        @pl.when(j == nmain)
        def _():
            p_ref[...] = _dot(hs[...], t_ref[...], NT)

    resident = pl.Buffered(1)
    return _call(
        body, (x, g, wt, tail), name="in_proj", grid=(T // tm, nmain + 1),
        in_specs=[pl.BlockSpec((tm, D_MODEL), lambda i, j: (i, 0)),
                  pl.BlockSpec((1, D_MODEL), lambda i, j: (0, 0)),
                  pl.BlockSpec((None, nmain * TAIL, D_MODEL), lambda i, j: (l, 0, 0), pipeline_mode=resident),
                  pl.BlockSpec((None, TAIL, D_MODEL), lambda i, j: (l, 0, 0), pipeline_mode=resident)],
        out_specs=[pl.BlockSpec((tm, TAIL), lambda i, j: (i, j)),
                   pl.BlockSpec((tm, D_MODEL), lambda i, j: (i, 0))],
        out_shape=[jax.ShapeDtypeStruct((T, IN_WIDTH), f32), jax.ShapeDtypeStruct((T, D_MODEL), bf16)],
        scratch_shapes=[pltpu.VMEM((tm, D_MODEL), bf16)],
        semantics=("parallel", "arbitrary"), ride=ride)


def _out_proj(ch, ca, wo, l, x, g, *, tm=512):
    T = x.shape[0]
    tm = min(tm, T)
    half = MIX_WIDTH // 2

    def body(ch_ref, ca_ref, wo_ref, x_ref, g_ref, xn_ref, y_ref):
        y = _dot(ch_ref[...], wo_ref[0:half, :]) + _dot(ca_ref[...], wo_ref[half:MIX_WIDTH, :])
        r = lax.rsqrt(jnp.mean(y * y, axis=-1, keepdims=True) + NORM_EPS)
        y_ref[...] = y
        xn_ref[...] = x_ref[...] + y * r * g_ref[...]

    row = lambda i: (i, 0)
    fixed = lambda i: (0, 0)
    return pl.pallas_call(
        body, name="out_proj", grid=(T // tm,),
        in_specs=[pl.BlockSpec((tm, half), row), pl.BlockSpec((tm, half), row),
                  pl.BlockSpec((None, MIX_WIDTH, D_MODEL), lambda i: (l, 0, 0)), pl.BlockSpec((tm, D_MODEL), row),
                  pl.BlockSpec((1, D_MODEL), fixed)],
        out_specs=[pl.BlockSpec((tm, D_MODEL), row), pl.BlockSpec((tm, D_MODEL), row)],
        out_shape=[jax.ShapeDtypeStruct((T, D_MODEL), f32)] * 2,
        compiler_params=_params("parallel"),
    )(ch, ca, wo, x, g)


def _out_proj_loss(ch, ca, wo, l, x, g, target, *, tm=512):
    T = x.shape[0]
    tm = min(tm, T)
    half = MIX_WIDTH // 2

    def body(ch_ref, ca_ref, wo_ref, x_ref, g_ref, t_ref, d_ref, y_ref, l_ref):
        @pl.when(pl.program_id(0) == 0)
        def _():
            l_ref[...] = jnp.zeros_like(l_ref)
        y = _dot(ch_ref[...], wo_ref[0:half, :]) + _dot(ca_ref[...], wo_ref[half:MIX_WIDTH, :])
        r = lax.rsqrt(jnp.mean(y * y, axis=-1, keepdims=True) + NORM_EPS)
        y_ref[...] = y
        err = (x_ref[...] + y * r * g_ref[...]) - t_ref[...]
        d_ref[...] = err * (1.0 / D_MODEL)
        l_ref[...] += jnp.sum(err * err) * (0.5 / D_MODEL)

    row = lambda i: (i, 0)
    fixed = lambda i: (0, 0)
    return pl.pallas_call(
        body, name="out_proj_loss", grid=(T // tm,),
        in_specs=[pl.BlockSpec((tm, half), row), pl.BlockSpec((tm, half), row),
                  pl.BlockSpec((None, MIX_WIDTH, D_MODEL), lambda i: (l, 0, 0)), pl.BlockSpec((tm, D_MODEL), row),
                  pl.BlockSpec((1, D_MODEL), fixed), pl.BlockSpec((tm, D_MODEL), row)],
        out_specs=[pl.BlockSpec((tm, D_MODEL), row), pl.BlockSpec((tm, D_MODEL), row),
                   pl.BlockSpec((8, LANES), fixed)],
        out_shape=[jax.ShapeDtypeStruct((T, D_MODEL), f32)] * 2 + [jax.ShapeDtypeStruct((8, LANES), f32)],
        compiler_params=_params("arbitrary"),
    )(ch, ca, wo, x, g, target)


def _out_proj_bwd(dxn, y, g, wo, l, ch, ca, *, tm=256, ride=None):
    T = y.shape[0]
    tm = min(tm, T)
    half = MIX_WIDTH // 2

    def body(dx_ref, y_ref, g_ref, wo_ref, ch_ref, ca_ref, dch_ref, dca_ref, dwo_ref, dg_ref):
        @pl.when(pl.program_id(0) == 0)
        def _():
            dwo_ref[...] = jnp.zeros_like(dwo_ref)
            dg_ref[...] = jnp.zeros_like(dg_ref)
        y = y_ref[...]
        dx = dx_ref[...]
        r = lax.rsqrt(jnp.mean(y * y, axis=-1, keepdims=True) + NORM_EPS)
        gy = dx * g_ref[...]
        dy = r * gy - y * (r * r * r) * jnp.mean(gy * y, axis=-1, keepdims=True)
        dg_ref[...] += jnp.sum(dx * y * r, axis=0, keepdims=True)
        dyb = dy.astype(bf16)
        dch_ref[...] = _dot(dyb, wo_ref[0:half, :], NT)
        dca_ref[...] = _dot(dyb, wo_ref[half:MIX_WIDTH, :], NT)
        dwo_ref[0:half, :] += _dot(ch_ref[...], dyb, TN)
        dwo_ref[half:MIX_WIDTH, :] += _dot(ca_ref[...], dyb, TN)

    row = lambda i: (i, 0)
    fixed = lambda i: (0, 0)
    return _call(
        body, (dxn, y, g, wo, ch, ca), name="out_proj_bwd", grid=(T // tm,),
        in_specs=[pl.BlockSpec((tm, D_MODEL), row), pl.BlockSpec((tm, D_MODEL), row),
                  pl.BlockSpec((1, D_MODEL), fixed), pl.BlockSpec((None, MIX_WIDTH, D_MODEL), lambda i: (l, 0, 0)),
                  pl.BlockSpec((tm, half), row), pl.BlockSpec((tm, half), row)],
        out_specs=[pl.BlockSpec((tm, half), row), pl.BlockSpec((tm, half), row),
                   pl.BlockSpec((MIX_WIDTH, D_MODEL), fixed), pl.BlockSpec((1, D_MODEL), fixed)],
        out_shape=[jax.ShapeDtypeStruct((T, half), f32), jax.ShapeDtypeStruct((T, half), f32),
                   jax.ShapeDtypeStruct((MIX_WIDTH, D_MODEL), f32), jax.ShapeDtypeStruct((1, D_MODEL), f32)],
        semantics=("arbitrary",), ride=ride)


TILE = 256
PIECE_TILES = (4, 4, 4, 4, 4, 1, 4)
PIECE_START = tuple(sum(PIECE_TILES[:p]) for p in range(len(PIECE_TILES)))
N_TILES = sum(PIECE_TILES)


def _piece_specs(rows, index):
    def spec(s, n):
        def index_map(*g):
            r, t = index(*g)
            return r, jnp.clip(t - s, 0, n - 1)
        return pl.BlockSpec((rows, TILE), index_map)
    return [spec(s, n) for s, n in zip(PIECE_START, PIECE_TILES)]


def _for_piece(t, fn):
    for p, (s, n) in enumerate(zip(PIECE_START, PIECE_TILES)):
        pl.when((t >= s) & (t < s + n))(functools.partial(fn, p))


def _in_proj_bwd(pieces, wt, l, x, g, dxn, *, tm=512, blocks=None, dx_into=None, ride=None):
    T = x.shape[0]
    tm = min(tm, T)
    first, count = blocks or (0, T // tm)
    npc = len(pieces)
    starts = [sum(p.shape[1] for p in pieces[:i]) for i in range(npc)]
    extra = [] if dx_into is None else [dx_into]

    def body(*refs):
        dp_refs = refs[:npc]
        w_ref, x_ref, g_ref, dxn_ref = refs[npc:npc + 4]
        dx_ref, dg_ref = refs[npc + 4 + len(extra):]

        @pl.when(pl.program_id(0) == 0)
        def _():
            dg_ref[...] = jnp.zeros_like(dg_ref)
        dh = None
        for p in range(npc):
            term = _dot(dp_refs[p][...], w_ref[starts[p]:starts[p] + pieces[p].shape[1], :])
            dh = term if dh is None else dh + term
        xv = x_ref[...]
        r = lax.rsqrt(jnp.mean(xv * xv, axis=-1, keepdims=True) + NORM_EPS)
        gy = dh * g_ref[...]
        dx_ref[...] = dxn_ref[...] + r * gy - xv * (r * r * r) * jnp.mean(gy * xv, axis=-1, keepdims=True)
        dg_ref[...] += jnp.sum(dh * xv * r, axis=0, keepdims=True)

    rows = lambda i: (first + i, 0)
    return _call(
        body, (*pieces, wt, x, g, dxn, *extra), name="in_proj_bwd", grid=(count,),
        in_specs=[pl.BlockSpec((tm, p.shape[1]), rows) for p in pieces] + [
            pl.BlockSpec((None, IN_WIDTH, D_MODEL), lambda i: (l, 0, 0), pipeline_mode=pl.Buffered(1)),
            pl.BlockSpec((tm, D_MODEL), rows), pl.BlockSpec((1, D_MODEL), lambda i: (0, 0)),
            pl.BlockSpec((tm, D_MODEL), rows)] + [ANY] * len(extra),
        out_specs=[pl.BlockSpec((tm, D_MODEL), rows), pl.BlockSpec((1, D_MODEL), lambda i: (0, 0))],
        out_shape=[jax.ShapeDtypeStruct((T, D_MODEL), f32), jax.ShapeDtypeStruct((1, D_MODEL), f32)],
        semantics=("arbitrary",), ride=ride, aliases={npc + 4: 0} if extra else None)


def _grad_w_in(h, pieces, *, ride=None):
    T = h.shape[0]
    npc = len(pieces)

    def body(*refs):
        h_ref, dp_refs, o_ref = refs[0], refs[1:1 + npc], refs[1 + npc]

        def put(p):
            o_ref[...] = _dot(dp_refs[p][...], h_ref[...], TN)
        _for_piece(pl.program_id(0), put)

    return _call(
        body, (h, *pieces), name="grad_w_in", grid=(N_TILES,),
        in_specs=[pl.BlockSpec((T, D_MODEL), lambda j: (0, 0), pipeline_mode=pl.Buffered(1))]
        + _piece_specs(T, lambda j: (0, j)),
        out_specs=[pl.BlockSpec((TILE, D_MODEL), lambda j: (j, 0))],
        out_shape=[jax.ShapeDtypeStruct((IN_WIDTH, D_MODEL), f32)],
        semantics=("parallel",), ride=ride)


def _lower_bound(lbp, layer):
    m = jnp.max(lbp, axis=0, keepdims=True)
    e = jnp.exp(lbp - m)
    p = e / jnp.sum(e, axis=0, keepdims=True)
    acc = p[0:1]
    for i in range(1, layer + 1):
        acc = acc + p[i:i + 1]
    return acc - p[0:1]


def _gate_parts(qr, fr, lb, lbf):
    sq = _sigmoid(qr)
    e = jnp.exp(-jnp.abs(fr))
    inv = 1.0 / (1.0 + e)
    pos = fr >= 0
    sg = jnp.where(pos, inv, e * inv)
    nsg = jnp.where(pos, e * inv, inv)
    fg = lbf + (1.0 - lb) * sg
    return qr * sq, sq, sg, nsg, fg, jnp.log(fg), (1.0 - lb) * nsg


LEVELS = tuple(SUB << j for j in range((CHUNK // SUB).bit_length() - 1))


def _level_masks(transposed=False):
    t = lax.broadcasted_iota(jnp.int32, (CHUNK, CHUNK), 1 if transposed else 0)
    s = lax.broadcasted_iota(jnp.int32, (CHUNK, CHUNK), 0 if transposed else 1)
    return [(t % (2 * m) >= m) & (s % (2 * m) < m) & (t // (2 * m) == s // (2 * m)) for m in LEVELS]


def _level_anchor(b_s, row, m):
    beta = b_s[m - 1:m, :]
    for g in range(1, CHUNK // (2 * m)):
        beta = jnp.where(row >= g * 2 * m, b_s[g * 2 * m + m - 1:g * 2 * m + m, :], beta)
    return beta


FWD_INTERLEAVE = 16
BWD_INTERLEAVE = 8
SWA_INTERLEAVE = 4


def _interleaved(chunks, width):
    for g0 in range(0, len(chunks), width):
        live = chunks[g0:g0 + width]
        while live:
            for gen in list(live):
                try:
                    next(gen)
                except StopIteration:
                    live.remove(gen)


def _seg_sum(seg, x):
    hi = x.astype(bf16)
    return _dot(seg, hi) + _dot(seg, (x - hi.astype(f32)).astype(bf16))


def _hgrn_fwd(proj, lb_param, g_head, *, B, S, layer, ride=None):
    T = B * S
    TB = min(1024, S)
    nT, NC = S // TB, TB // CHUNK
    nC = S // CHUNK
    HD = HG_HEAD_DIM

    def body(q_ref, f_ref, i_ref, z_ref, lb_ref, gh_ref, cat_ref, op_ref, st_ref,
             s_scr, b_scr, k_scr):
        @pl.when(pl.program_id(2) == 0)
        def _():
            s_scr[...] = jnp.zeros_like(s_scr)
        lb = _lower_bound(lb_ref[...], layer)
        lbf = jnp.maximum(lb, LB_FLOOR)
        gh = gh_ref[...]
        r_i = lax.broadcasted_iota(jnp.int32, (CHUNK, CHUNK), 0)
        c_i = lax.broadcasted_iota(jnp.int32, (CHUNK, CHUNK), 1)
        tril = (r_i >= c_i).astype(bf16)
        rows8 = lax.broadcasted_iota(jnp.int32, (8, HD), 0)
        row_c = lax.broadcasted_iota(jnp.int32, (CHUNK, HD), 0)
        lane_c = lax.broadcasted_iota(jnp.int32, (8, CHUNK), 1)
        masks = _level_masks()

        def chunk(c, carried):
            rs = slice(c * CHUNK, (c + 1) * CHUNK)
            b_s, k_s = b_scr.at[c], k_scr.at[c]
            q, _, _, _, _, logf, k = _gate_parts(q_ref[rs, :], f_ref[rs, :], lb, lbf)
            v = i_ref[rs, :]
            b = _seg_sum(tril, logf)
            b_s[...] = b
            k_s[...] = k
            yield
            pieces = []
            for blk in range(CHUNK // SUB):
                r0 = blk * SUB
                bp = [b[r0 + 8 * i:r0 + 8 * i + 8] for i in range(SUB // 8)]
                qp = [q[r0 + 8 * i:r0 + 8 * i + 8] for i in range(SUB // 8)]
                ap = [jnp.zeros((8, CHUNK), f32) for _ in range(SUB // 8)]
                for s in range(SUB):
                    bs = b_s[r0 + s:r0 + s + 1, :]
                    ks = k_s[r0 + s:r0 + s + 1, :]
                    for i in range(s // 8, SUB // 8):
                        diff = bp[i] - bs
                        if i == s // 8:
                            diff = jnp.where(rows8 >= s - 8 * i, diff, NEG_INF)
                        col = jnp.sum(jnp.exp(diff) * qp[i] * ks, axis=1, keepdims=True)
                        ap[i] = jnp.where(lane_c == r0 + s, col, ap[i])
                pieces += ap
                yield
            a_all = jnp.concatenate(pieces, axis=0)
            for m, mk in zip(LEVELS, masks):
                beta = _level_anchor(b_s, row_c, m)
                qh = (q * jnp.exp(jnp.minimum(b - beta, 0.0))).astype(bf16)
                kh = (k * jnp.exp(jnp.minimum(beta - b, 0.0))).astype(bf16)
                a_all = a_all + jnp.where(mk, _dot(qh, kh, NT), 0.0)
            yield
            st = carried[0]
            st_ref[0, 0, c] = st
            vb16 = v.astype(bf16)
            o = _dot(a_all.astype(bf16), vb16) + _dot((q * jnp.exp(b)).astype(bf16), st.astype(bf16), NT)
            b_end = b_s[CHUNK - 1:CHUNK, :]
            kdec = (k * jnp.exp(b_end - b)).astype(bf16)
            carried[0] = jnp.exp(b_end) * st + _dot(vb16, kdec, TN)
            rr = lax.rsqrt(jnp.mean(o * o, axis=-1, keepdims=True) + NORM_EPS)
            zr = z_ref[rs, :]
            cat_ref[rs, :] = (o * rr * gh * (zr * _sigmoid(zr))).astype(bf16)
            op_ref[rs, :] = o

        carried = [s_scr[...]]
        _interleaved([chunk(c, carried) for c in range(NC)], FWD_INTERLEAVE)
        s_scr[...] = carried[0]

    def col(part):
        return pl.BlockSpec((TB, HD), lambda b, h, n: (b * nT + n, part * HG_HEADS + h))

    out_col = pl.BlockSpec((TB, HD), lambda b, h, n: (b * nT + n, h))
    return _call(
        body, (proj, proj, proj, proj, lb_param, g_head),
        name=f"hgrn_fwd_l{layer}", grid=(B, HG_HEADS, nT),
        in_specs=[col(0), col(1), col(2), col(3),
                  pl.BlockSpec((DEPTH, HD), lambda b, h, n: (0, h)),
                  pl.BlockSpec((1, HD), lambda b, h, n: (0, 0))],
        out_specs=[out_col, out_col,
                   pl.BlockSpec((1, 1, NC, HD, HD), lambda b, h, n: (b, h, n, 0, 0))],
        out_shape=[jax.ShapeDtypeStruct((T, HG_WIDTH), bf16), jax.ShapeDtypeStruct((T, HG_WIDTH), f32),
                   jax.ShapeDtypeStruct((B, HG_HEADS, nC, HD, HD), f32)],
        scratch_shapes=[pltpu.VMEM((HD, HD), f32), pltpu.VMEM((NC, CHUNK, HD), f32), pltpu.VMEM((NC, CHUNK, HD), f32)],
        semantics=("parallel", "parallel", "arbitrary"), ride=ride)


def _hgrn_bwd(proj, lb_param, g_head, o_pre, states, dcat, *, B, S, layer, ride=None):
    T = B * S
    TB = min(1024, S)
    nT, NC = S // TB, TB // CHUNK
    HD = HG_HEAD_DIM

    def body(q_ref, f_ref, i_ref, z_ref, lb_ref, gh_ref, op_ref, st_ref, dc_ref,
             dq_ref, df_ref, di_ref, dz_ref, dlb_ref, dgh_ref,
             ds_scr, b_scr, q_scr, do_scr, wk_scr):
        @pl.when(pl.program_id(2) == 0)
        def _():
            ds_scr[...] = jnp.zeros_like(ds_scr)
            dlb_ref[...] = jnp.zeros_like(dlb_ref)
            dgh_ref[...] = jnp.zeros_like(dgh_ref)
        lb = _lower_bound(lb_ref[...], layer)
        lbf = jnp.maximum(lb, LB_FLOOR)
        ind = (lb > LB_FLOOR).astype(f32)
        gh = gh_ref[...]
        r_i = lax.broadcasted_iota(jnp.int32, (CHUNK, CHUNK), 0)
        c_i = lax.broadcasted_iota(jnp.int32, (CHUNK, CHUNK), 1)
        tril = (r_i >= c_i).astype(bf16)
        triu = (c_i >= r_i).astype(bf16)
        rows8 = lax.broadcasted_iota(jnp.int32, (8, HD), 0)
        row_c = lax.broadcasted_iota(jnp.int32, (CHUNK, HD), 0)
        lane_c = lax.broadcasted_iota(jnp.int32, (8, CHUNK), 1)
        last_row = row_c == CHUNK - 1
        masks = _level_masks()
        masks_t = _level_masks(transposed=True)
        seg_t = lax.broadcasted_iota(jnp.int32, (SUB, 8 * SUB), 0)
        seg_r = lax.broadcasted_iota(jnp.int32, (SUB, 8 * SUB), 1) // 8
        seg0 = (seg_r == seg_t).astype(bf16)
        seg1 = (seg_r[:, 0:4 * SUB] + 8 == seg_t[:, 0:4 * SUB]).astype(bf16)

        def chunk(c, carried):
            rs = slice(c * CHUNK, (c + 1) * CHUNK)
            b_s, q_s, do_s = b_scr.at[c], q_scr.at[c], do_scr.at[c]
            qr, fr = q_ref[rs, :], f_ref[rs, :]
            q, sq, sg, nsg, fg, logf, k = _gate_parts(qr, fr, lb, lbf)
            v = i_ref[rs, :]
            b = _seg_sum(tril, logf)
            o = op_ref[rs, :]
            dc = dc_ref[rs, :]
            zr = z_ref[rs, :]
            sz = _sigmoid(zr)
            rr = lax.rsqrt(jnp.mean(o * o, axis=-1, keepdims=True) + NORM_EPS)
            dz_ref[rs, :] = (dc * (o * rr * gh) * (sz * (1.0 + zr * (1.0 - sz)))).astype(bf16)
            dn = dc * (zr * sz)
            dgh_ref[0, 0] += jnp.sum(dn * o * rr, axis=0, keepdims=True)
            gdn = dn * gh
            d_o = rr * gdn - o * (rr * rr * rr) * jnp.mean(gdn * o, axis=-1, keepdims=True)
            b_s[...] = b
            q_s[...] = q
            do_s[...] = d_o
            dob = d_o.astype(bf16)
            vb16 = v.astype(bf16)
            d_a = _dot(dob, vb16, NT)
            yield
            d_q = jnp.zeros((CHUNK, HD), f32)
            d_k = jnp.zeros((CHUNK, HD), f32)
            at_all = jnp.zeros((CHUNK, CHUNK), f32)
            for m, mk, mkt in zip(LEVELS, masks, masks_t):
                beta = _level_anchor(b_s, row_c, m)
                eq = jnp.exp(jnp.minimum(b - beta, 0.0))
                ek = jnp.exp(jnp.minimum(beta - b, 0.0))
                qh = (q * eq).astype(bf16)
                kh = (k * ek).astype(bf16)
                at_all = at_all + jnp.where(mkt, _dot(kh, qh, NT), 0.0)
                d_aa = jnp.where(mk, d_a, 0.0).astype(bf16)
                d_q = d_q + _dot(d_aa, kh) * eq
                d_k = d_k + _dot(d_aa, qh, TN) * ek
            yield
            dq_blocks, dk_pieces, at_pieces = [], [], []
            for blk in range(CHUNK // SUB):
                r0 = blk * SUB
                wk = wk_scr.at[c * (CHUNK // SUB) + blk]
                bp = [b[r0 + 8 * i:r0 + 8 * i + 8] for i in range(SUB // 8)]
                kp = [k[r0 + 8 * i:r0 + 8 * i + 8] for i in range(SUB // 8)]
                vp = [v[r0 + 8 * i:r0 + 8 * i + 8] for i in range(SUB // 8)]
                dkp = [jnp.zeros((8, HD), f32) for _ in range(SUB // 8)]
                atp = [jnp.zeros((8, CHUNK), f32) for _ in range(SUB // 8)]
                for t in range(SUB):
                    bt = b_s[r0 + t:r0 + t + 1, :]
                    qt = q_s[r0 + t:r0 + t + 1, :]
                    dot_ = do_s[r0 + t:r0 + t + 1, :]
                    for i in range(t // 8 + 1):
                        diff = bt - bp[i]
                        if i == t // 8:
                            diff = jnp.where(rows8 <= t - 8 * i, diff, NEG_INF)
                        e = jnp.exp(diff)
                        a = jnp.sum(e * kp[i] * qt, axis=1, keepdims=True)
                        atp[i] = jnp.where(lane_c == r0 + t, a, atp[i])
                        w = jnp.sum(vp[i] * dot_, axis=1, keepdims=True) * e
                        dkp[i] = dkp[i] + w * qt
                        row = 8 * t if i == 0 else 8 * SUB + 8 * (t - 8)
                        wk[row:row + 8, :] = w * kp[i]
                dq_blk = _seg_sum(seg0, wk[0:8 * SUB, :])
                if SUB > 8:
                    dq_blk = dq_blk + _seg_sum(seg1, wk[8 * SUB:12 * SUB, :])
                dq_blocks.append(dq_blk)
                dk_pieces += dkp
                at_pieces += atp
                yield
            dst1 = carried[0]
            st0 = st_ref[0, 0, c]
            dst1b = dst1.astype(bf16)
            eb = jnp.exp(b)
            b_end = b_s[CHUNK - 1:CHUNK, :]
            edec = jnp.exp(b_end - b)
            e_end = jnp.exp(b_end)
            kdec = (k * edec).astype(bf16)
            qdec = (q * eb).astype(bf16)
            st1 = e_end * st0 + _dot(vb16, kdec, TN)
            rterm = jnp.sum(dst1 * st1, axis=0, keepdims=True)
            carried[0] = e_end * dst1 + _dot(dob, qdec, TN)
            d_q = d_q + _dot(dob, st0.astype(bf16)) * eb + jnp.concatenate(dq_blocks, axis=0)
            d_k = d_k + _dot(vb16, dst1b) * edec + jnp.concatenate(dk_pieces, axis=0)
            d_v = _dot(kdec, dst1b, NT) + _dot((at_all + jnp.concatenate(at_pieces, axis=0)).astype(bf16), dob)
            db = q * d_q - k * d_k + jnp.where(last_row, rterm, 0.0)
            dlt = _seg_sum(triu, db) - fg * d_k
            df_ref[rs, :] = (dlt * (1.0 - lb) * sg * nsg / fg).astype(bf16)
            dlb_ref[0] += jnp.sum(dlt * (ind - sg) / fg, axis=0, keepdims=True)
            dq_ref[rs, :] = (d_q * (sq * (1.0 + qr * (1.0 - sq)))).astype(bf16)
            di_ref[rs, :] = d_v.astype(bf16)

        carried = [ds_scr[...]]
        _interleaved([chunk(c, carried) for c in reversed(range(NC))], BWD_INTERLEAVE)
        ds_scr[...] = carried[0]

    def col(part):
        return pl.BlockSpec((TB, HD), lambda b, h, n: (b * nT + nT - 1 - n, part * HG_HEADS + h))

    hcol = pl.BlockSpec((TB, HD), lambda b, h, n: (b * nT + nT - 1 - n, h))
    return _call(
        body, (proj, proj, proj, proj, lb_param, g_head, o_pre, states, dcat),
        name=f"hgrn_bwd_l{layer}", grid=(B, HG_HEADS, nT),
        in_specs=[col(0), col(1), col(2), col(3),
                  pl.BlockSpec((DEPTH, HD), lambda b, h, n: (0, h)),
                  pl.BlockSpec((1, HD), lambda b, h, n: (0, 0)),
                  hcol,
                  pl.BlockSpec((1, 1, NC, HD, HD), lambda b, h, n: (b, h, nT - 1 - n, 0, 0)),
                  hcol],
        out_specs=[hcol, hcol, hcol, hcol,
                   pl.BlockSpec((1, 1, HD), lambda b, h, n: (b, 0, h)),
                   pl.BlockSpec((1, 1, 1, HD), lambda b, h, n: (b, h, 0, 0))],
        out_shape=[jax.ShapeDtypeStruct((T, HG_WIDTH), bf16)] * 4 + [
            jax.ShapeDtypeStruct((B, 1, HG_WIDTH), f32), jax.ShapeDtypeStruct((B, HG_HEADS, 1, HD), f32)],
        scratch_shapes=[pltpu.VMEM((HD, HD), f32)] + [pltpu.VMEM((NC, CHUNK, HD), f32)] * 3
        + [pltpu.VMEM((NC * CHUNK // SUB, 12 * SUB, HD), f32)],
        semantics=("parallel", "parallel", "arbitrary"), ride=ride)


def _rope_tables(S):
    half = ATT_HEAD_DIM // 2
    inv_freq = ROPE_THETA ** (-jnp.arange(half, dtype=f32) / half)
    ang = jnp.arange(S, dtype=f32)[:, None] * inv_freq[None, :]
    cos, sin = jnp.cos(ang), jnp.sin(ang)
    return jnp.tile(jnp.concatenate([cos, cos], axis=1), (1, 2)), jnp.tile(jnp.concatenate([-sin, sin], axis=1), (1, 2))


def _swap_halves(x, first_half):
    return jnp.where(first_half, pltpu.roll(x, LANES - ATT_HEAD_DIM // 2, 1), pltpu.roll(x, ATT_HEAD_DIM // 2, 1))


def _rope(x, cos, sin, first_half):
    return x * cos + _swap_halves(x, first_half) * sin


def _rope_bwd(dy, cos, sin, first_half):
    return dy * cos + _swap_halves(dy * sin, first_half)


def _attn_consts(n):
    lane = lax.broadcasted_iota(jnp.int32, (1, LANES), 1)
    low = lane < ATT_HEAD_DIM
    first_half = (lane % ATT_HEAD_DIM) < ATT_HEAD_DIM // 2
    top = lax.broadcasted_iota(jnp.int32, (LANES, 1), 0) < ATT_HEAD_DIM
    s = lax.broadcasted_iota(jnp.int32, (2 * ATT_BLOCK, ATT_BLOCK), 0)
    t = lax.broadcasted_iota(jnp.int32, (2 * ATT_BLOCK, ATT_BLOCK), 1)
    mask = (s > t) & (s <= t + ATT_BLOCK) & ((s >= ATT_BLOCK) | (n > 0))
    return low, first_half, top, mask


def _dup_kv(x, low):
    rolled = pltpu.roll(x, ATT_HEAD_DIM, 1)
    return [jnp.where(low, x, rolled), jnp.where(low, rolled, x)]


def _attn_head(qtm, kd, vdt, sink, mask):
    s = jnp.where(mask, _dot(kd, qtm) * ATT_SCALE, NEG_INF)
    m = jnp.maximum(jnp.max(s, axis=0, keepdims=True), sink)
    p = jnp.exp(s - m)
    psink = jnp.exp(sink - m)
    inv = 1.0 / (jnp.sum(p, axis=0, keepdims=True) + psink)
    pn = p * inv
    return pn, psink * inv, _dot(vdt, pn.astype(bf16))


def _swa_fwd(proj, sink_b, cos, sin, *, B, S, ride=None):
    T = B * S
    L = ATT_BLOCK
    nB = S // L

    def body(q_ref, z_ref, kvc_ref, kvp_ref, sk_ref, cc_ref, sc_ref, cp_ref, sp_ref, cat_ref):
        n = pl.program_id(1)
        low, first_half, top, mask = _attn_consts(n)
        cc, sc = cc_ref[...], sc_ref[...]
        kc = _rope(kvc_ref[:, 0:LANES], cc, sc, first_half)
        kp = _rope(kvp_ref[:, 0:LANES], cp_ref[...], sp_ref[...], first_half)
        kd = [x.astype(bf16) for x in _dup_kv(jnp.concatenate([kp, kc], axis=0), low)]
        vdt = [x.T.astype(bf16) for x in _dup_kv(jnp.concatenate([kvp_ref[:, LANES:2 * LANES], kvc_ref[:, LANES:2 * LANES]], axis=0), low)]
        def head_pair(pair):
            cols = slice(pair * LANES, (pair + 1) * LANES)
            j = (2 * pair) // ATT_GROUP
            qt = _rope(q_ref[:, cols], cc, sc, first_half).T
            yield
            outs = []
            for hh in range(2):
                h = 2 * pair + hh
                qtm = jnp.where(top if hh == 0 else ~top, qt, 0.0).astype(bf16)
                outs.append(_attn_head(qtm, kd[j], vdt[j], sk_ref[h:h + 1, 0:1], mask)[2])
                yield
            zp = z_ref[:, cols]
            cat_ref[:, cols] = (jnp.where(top, outs[0], outs[1]).T * (zp * _sigmoid(zp))).astype(bf16)

        _interleaved([head_pair(p) for p in range(ATT_HEADS // 2)], SWA_INTERLEAVE)

    cur = lambda b, n: (b * nB + n, 0)
    return _call(
        body, (proj, proj, proj, proj, sink_b, cos, sin, cos, sin), name="swa_fwd", grid=(B, nB),
        in_specs=[pl.BlockSpec((L, ATT_WIDTH), lambda b, n: (b * nB + n, QA_BLK)),
                  pl.BlockSpec((L, ATT_WIDTH), lambda b, n: (b * nB + n, ZA_BLK)),
                  pl.BlockSpec((L, 2 * KV_WIDTH), lambda b, n: (b * nB + n, KV_BLK)),
                  pl.BlockSpec((L, 2 * KV_WIDTH), lambda b, n: (b * nB + jnp.maximum(n - 1, 0), KV_BLK)),
                  pl.BlockSpec((ATT_HEADS, LANES), lambda b, n: (0, 0)),
                  pl.BlockSpec((L, LANES), lambda b, n: (n, 0)), pl.BlockSpec((L, LANES), lambda b, n: (n, 0)),
                  pl.BlockSpec((L, LANES), lambda b, n: (jnp.maximum(n - 1, 0), 0)),
                  pl.BlockSpec((L, LANES), lambda b, n: (jnp.maximum(n - 1, 0), 0))],
        out_specs=[pl.BlockSpec((L, ATT_WIDTH), cur)],
        out_shape=[jax.ShapeDtypeStruct((T, ATT_WIDTH), bf16)],
        semantics=("parallel", "parallel"), ride=ride)


def _swa_bwd(proj, sink_b, cos, sin, dcat, *, B, S, ride=None):
    T = B * S
    L = ATT_BLOCK
    nB = S // L

    def body(q_ref, z_ref, kvc_ref, kvp_ref, sk_ref, cc_ref, sc_ref, cp_ref, sp_ref, dc_ref,
             dq_ref, dz_ref, dkv_ref, dsk_ref, carry, ds_st, pn_st, q_st, do_st):
        step = pl.program_id(1)
        n = nB - 1 - step

        @pl.when((pl.program_id(0) == 0) & (step == 0))
        def _():
            dsk_ref[...] = jnp.zeros_like(dsk_ref)

        @pl.when(step == 0)
        def _():
            carry[...] = jnp.zeros_like(carry)
        low, first_half, top, mask = _attn_consts(n)
        cc, sc, cp, sp = cc_ref[...], sc_ref[...], cp_ref[...], sp_ref[...]
        kc = _rope(kvc_ref[:, 0:LANES], cc, sc, first_half)
        kp = _rope(kvp_ref[:, 0:LANES], cp, sp, first_half)
        kdf = _dup_kv(jnp.concatenate([kp, kc], axis=0), low)
        vdf = _dup_kv(jnp.concatenate([kvp_ref[:, LANES:2 * LANES], kvc_ref[:, LANES:2 * LANES]], axis=0), low)
        kd = [x.astype(bf16) for x in kdf]
        vd = [x.astype(bf16) for x in vdf]
        kdt = [x.T.astype(bf16) for x in kdf]
        vdt = [x.T.astype(bf16) for x in vdf]
        dkd, dvd = [], []
        def head_pair(pair):
            cols = slice(pair * LANES, (pair + 1) * LANES)
            j = (2 * pair) // ATT_GROUP
            qp = _rope(q_ref[:, cols], cc, sc, first_half)
            qt = qp.T
            zp = z_ref[:, cols]
            dc = dc_ref[:, cols]
            sz = _sigmoid(zp)
            d_o = dc * (zp * sz)
            dot_ = d_o.T
            yield
            res = []
            for hh in range(2):
                rsel = top if hh == 0 else ~top
                qtm = jnp.where(rsel, qt, 0.0).astype(bf16)
                pn, psn, o = _attn_head(qtm, kd[j], vdt[j], sk_ref[2 * pair + hh:2 * pair + hh + 1, 0:1], mask)
                res.append((rsel, pn, psn, o))
                yield
            ot = jnp.where(top, res[0][3], res[1][3])
            dz_ref[:, cols] = (dc * ot.T * (sz * (1.0 + zp * (1.0 - sz)))).astype(bf16)
            dqts = []
            for hh in range(2):
                h = 2 * pair + hh
                rsel, pn, psn, _ = res[hh]
                lsel = low if hh == 0 else ~low
                dotm = jnp.where(rsel, dot_, 0.0)
                delta = jnp.sum(dotm * ot, axis=0, keepdims=True)
                dst = (pn * (_dot(vd[j], dotm.astype(bf16)) - delta) * ATT_SCALE).astype(bf16)
                dsk_ref[h:h + 1, :] += jnp.zeros((1, LANES), f32) - jnp.sum(psn * delta)
                dqts.append(_dot(kdt[j], dst))
                g = h % ATT_GROUP
                ds_st[:, g * LANES:(g + 1) * LANES] = dst
                pn_st[:, g * LANES:(g + 1) * LANES] = pn.astype(bf16)
                q_st[g * LANES:(g + 1) * LANES, :] = jnp.where(lsel, qp, 0.0).astype(bf16)
                do_st[g * LANES:(g + 1) * LANES, :] = jnp.where(lsel, d_o, 0.0).astype(bf16)
                yield
            dq_ref[:, cols] = _rope_bwd(jnp.where(top, dqts[0], dqts[1]).T, cc, sc, first_half).astype(bf16)

        pairs_per_group = ATT_GROUP // 2
        for grp in range(ATT_HEADS // ATT_GROUP):
            _interleaved([head_pair(grp * pairs_per_group + p) for p in range(pairs_per_group)], SWA_INTERLEAVE)
            dkd.append(_dot(ds_st[...], q_st[...]))
            dvd.append(_dot(pn_st[...], do_st[...]))
        dk = [x + pltpu.roll(x, ATT_HEAD_DIM, 1) for x in dkd]
        dv = [x + pltpu.roll(x, ATT_HEAD_DIM, 1) for x in dvd]
        dk = jnp.where(low, dk[0], dk[1])
        dv = jnp.where(low, dv[0], dv[1])
        dkv_ref[:, 0:LANES] = (_rope_bwd(dk[L:2 * L], cc, sc, first_half) + carry[:, 0:LANES]).astype(bf16)
        dkv_ref[:, LANES:2 * LANES] = (dv[L:2 * L] + carry[:, LANES:2 * LANES]).astype(bf16)
        carry[:, 0:LANES] = _rope_bwd(dk[0:L], cp, sp, first_half)
        carry[:, LANES:2 * LANES] = dv[0:L]

    rev = lambda b, s: b * nB + nB - 1 - s
    revp = lambda b, s: b * nB + jnp.maximum(nB - 2 - s, 0)
    wide = lambda blk: pl.BlockSpec((L, ATT_WIDTH), lambda b, s: (rev(b, s), blk))
    tab = pl.BlockSpec((L, LANES), lambda b, s: (nB - 1 - s, 0))
    tabp = pl.BlockSpec((L, LANES), lambda b, s: (jnp.maximum(nB - 2 - s, 0), 0))
    return _call(
        body, (proj, proj, proj, proj, sink_b, cos, sin, cos, sin, dcat), name="swa_bwd", grid=(B, nB),
        in_specs=[wide(QA_BLK), wide(ZA_BLK),
                  pl.BlockSpec((L, 2 * KV_WIDTH), lambda b, s: (rev(b, s), KV_BLK)),
                  pl.BlockSpec((L, 2 * KV_WIDTH), lambda b, s: (revp(b, s), KV_BLK)),
                  pl.BlockSpec((ATT_HEADS, LANES), lambda b, s: (0, 0)),
                  tab, tab, tabp, tabp, wide(0)],
        out_specs=[wide(0), wide(0), pl.BlockSpec((L, 2 * KV_WIDTH), lambda b, s: (rev(b, s), 0)),
                   pl.BlockSpec((ATT_HEADS, LANES), lambda b, s: (0, 0))],
        out_shape=[jax.ShapeDtypeStruct((T, ATT_WIDTH), bf16), jax.ShapeDtypeStruct((T, ATT_WIDTH), bf16),
                   jax.ShapeDtypeStruct((T, 2 * KV_WIDTH), bf16), jax.ShapeDtypeStruct((ATT_HEADS, LANES), f32)],
        scratch_shapes=[pltpu.VMEM((L, 2 * KV_WIDTH), f32),
                        pltpu.VMEM((2 * L, ATT_GROUP * LANES), bf16), pltpu.VMEM((2 * L, ATT_GROUP * LANES), bf16),
                        pltpu.VMEM((ATT_GROUP * LANES, LANES), bf16), pltpu.VMEM((ATT_GROUP * LANES, LANES), bf16)],
        semantics=("arbitrary", "arbitrary"), ride=ride)


def _train_step(x, target, bufs, g_pre, g_post, lb_param, g_head, sinks, *, B, S, exchange):
    L = DEPTH
    T = x.shape[0]
    ri, ro = IN_WIDTH // 8, MIX_WIDTH // 8
    cos, sin = _rope_tables(S)
    full = [list(b) for b in bufs]
    if exchange:
        full[0][0] = _run_exchange(_gather_d2d(_run_exchange(_gather_ici(bufs[0][:1]))))[0]
    saved = []
    for l in range(L):
        wt = full[l][0].reshape(1, IN_WIDTH, D_MODEL)
        tail = jnp.concatenate([wt[:, 5376:6400], wt[:, 5120:5376]], axis=1)
        first = exchange and l == 0
        ahead = exchange and l + 1 < L
        (proj, h), wo_landed = _in_proj(x, g_pre[l:l + 1], wt, tail, 0,
                                       ride=_gather_ici(bufs[0][1:], "gather_ici_wo") if first else None)
        (ch, o_pre, states), landed = _hgrn_fwd(
            proj, lb_param, g_head[l:l + 1], B=B, S=S, layer=l,
            ride=_merge(_gather_ici(bufs[l + 1]) if ahead else None,
                        _gather_d2d(wo_landed, "gather_d2d_wo") if first else None))
        if first:
            full[0][1] = landed[-1]
            landed = landed[:-1]
        wo = full[l][1].reshape(1, MIX_WIDTH, D_MODEL)
        sink_b = jnp.broadcast_to(sinks[l][:, None], (ATT_HEADS, LANES))
        (ca,), passed = _swa_fwd(proj, sink_b, cos, sin, B=B, S=S, ride=_gather_d2d(landed) if ahead else None)
        if ahead:
            full[l + 1] = list(passed)
        if l + 1 < L:
            xn, y = _out_proj(ch, ca, wo, 0, x, g_post[l:l + 1])
        else:
            dx, y, loss = _out_proj_loss(ch, ca, wo, 0, x, g_post[l:l + 1], target)
        saved.append((x, proj, h, ch, o_pre, states, sink_b, ca, y, wt, tail, wo))
        x = xn if l + 1 < L else None

    def reduce_tail(sums, recv):
        return _run_exchange(_pair_share([_chip_sum(s, r) for s, r in zip(sums, recv)]))

    grads = [None] * L
    waiting = None
    gg_pre, gg_post, g_lb, gg_head, g_sinks = [], [], [], [], []
    for l in reversed(range(L)):
        x_in, proj, h, ch, o_pre, states, sink_b, ca, y, wt, tail, wo = saved[l]
        (dch, dca, dwo, dgpost), got = _out_proj_bwd(dx, y, g_post[l:l + 1], wo, 0, ch, ca,
                                                     ride=_pair_exchange(waiting) if waiting else None)
        sums = [_pair_add(p, r) for p, r in zip(waiting, got)] if waiting else None
        (dq, df, di, dz, dlb, dgh), recv = _hgrn_bwd(proj, lb_param, g_head[l:l + 1], o_pre, states, dch, B=B, S=S,
                                                     layer=l, ride=_chip_exchange(sums) if waiting else None)
        at_end = exchange and l == 0
        part_o = [dwo.reshape(1, 4, 2, ro, D_MODEL)]
        halves = [_chip_sum(s, r) for s, r in zip(sums, recv)] if waiting else None
        (dqa, dza, dkv, dsk), rode = _swa_bwd(proj, sink_b, cos, sin, dca, B=B, S=S,
                                             ride=_merge(_pair_exchange(part_o) if at_end else None,
                                                         _pair_share(halves) if waiting else None))
        got_o = rode[:1]
        if waiting:
            grads[l + 1] = list(rode[-len(halves):])
        pieces = [dq, df, di, dz, dqa, dkv, dza]
        sums_o = [_pair_add(part_o[0], got_o[0])] if at_end else None
        (gwt,), recv_o = _grad_w_in(h, pieces, ride=_chip_exchange(sums_o) if at_end else None)
        part_t = [gwt.reshape(1, 4, 2, ri, D_MODEL)]
        if at_end:
            tm = min(512, T // 2)
            nb = T // tm
            na = max(1, nb // 4)
            (dx_a, dg_a), got_t = _in_proj_bwd(pieces, wt, 0, x_in, g_pre[l:l + 1], dx, tm=tm, blocks=(0, na),
                                               ride=_pair_exchange(part_t))
            sums_t = [_pair_add(part_t[0], got_t[0])]
            (dx, dg_b), recv_t = _in_proj_bwd(pieces, wt, 0, x_in, g_pre[l:l + 1], dx, tm=tm, blocks=(na, nb - na),
                                              dx_into=dx_a, ride=_chip_exchange(sums_t))
            dgpre = dg_a + dg_b
            grads[0] = reduce_tail(sums_t + sums_o, recv_t + recv_o)
        else:
            (dx, dgpre), _ = _in_proj_bwd(pieces, wt, 0, x_in, g_pre[l:l + 1], dx)
            if exchange:
                waiting = part_t + part_o
            else:
                grads[l] = [gwt, dwo]
        gg_pre.append(dgpre[0])
        gg_post.append(dgpost[0])
        g_lb.append(jnp.sum(dlb, axis=(0, 1)))
        gg_head.append(jnp.sum(dgh, axis=(0, 1, 2)))
        g_sinks.append(dsk[:, 0])
    rev = lambda xs: jnp.stack(xs[::-1])
    return loss[0, 0], dx, grads, rev(gg_pre), rev(gg_post), rev(g_lb), rev(gg_head), rev(g_sinks)


MESH = pl.DeviceIdType.MESH
ANY = pl.BlockSpec(memory_space=pl.ANY)


def _place():
    x, y, c = lax.axis_index("x"), lax.axis_index("y"), lax.axis_index("c")
    return x, y, c, [(1 - x, y), (x, 1 - y), (1 - x, 1 - y)]


def _rcopy(src, dst, send, recv, k, to):
    return pltpu.make_async_remote_copy(src_ref=src, dst_ref=dst, send_sem=send.at[k], recv_sem=recv.at[k],
                                        device_id=to, device_id_type=MESH)


class _Exchange:
    def __init__(self, name, inputs, out_shapes, n_sems, plan, in_place=False):
        self.name, self.inputs, self.out_shapes = name, list(inputs), list(out_shapes)
        self.n_sems, self.plan = n_sems, plan
        self.aliases = {a: a for a in range(len(inputs))} if in_place else {}

    def start(self, ins, outs, send, recv):
        for cp in self.plan(ins, outs, send, recv)[0]:
            cp.start()

    def finish(self, ins, outs, send, recv):
        sent, arriving = self.plan(ins, outs, send, recv)
        for cp in arriving:
            cp.wait_recv()
        for cp in sent:
            cp.wait_send()

    def sems(self):
        return [pltpu.SemaphoreType.DMA((self.n_sems,)), pltpu.SemaphoreType.DMA((self.n_sems,))]


class _SemView:
    def __init__(self, sems, offset):
        self.sems, self.offset = sems, offset

    @property
    def at(self):
        return self

    def __getitem__(self, k):
        return self.sems.at[self.offset + k]


def _both(a, b):
    ai, ao = len(a.inputs), len(a.out_shapes)

    def plan(ins, outs, send, recv):
        sa, ra = a.plan(ins[:ai], outs[:ao], send, recv)
        sb, rb = b.plan(ins[ai:], outs[ao:], _SemView(send, a.n_sems), _SemView(recv, a.n_sems))
        return sa + sb, ra + rb

    ex = _Exchange(a.name + "_" + b.name, a.inputs + b.inputs, a.out_shapes + b.out_shapes, a.n_sems + b.n_sems, plan)
    ex.aliases = {**a.aliases, **{ai + i: ao + o for i, o in b.aliases.items()}}
    return ex


def _merge(*rides):
    rides = [r for r in rides if r is not None]
    return functools.reduce(_both, rides) if rides else None


def _run_exchange(ex):
    n_in, n_out = len(ex.inputs), len(ex.out_shapes)

    def body(*refs):
        ins, outs = refs[:n_in], refs[n_in:n_in + n_out]
        send, recv = refs[n_in + n_out:]
        ex.start(ins, outs, send, recv)
        ex.finish(ins, outs, send, recv)

    return pl.pallas_call(
        body, name=ex.name, in_specs=[ANY] * n_in, out_specs=[ANY] * n_out, out_shape=ex.out_shapes,
        input_output_aliases=ex.aliases, scratch_shapes=ex.sems(),
    )(*ex.inputs)


def _call(body, operands, *, name, grid, in_specs, out_specs, out_shape, scratch_shapes=(), semantics, ride=None,
          aliases=None):
    aliases = dict(aliases or {})
    if ride is None:
        outs = pl.pallas_call(body, name=name, grid=grid, in_specs=in_specs, out_specs=out_specs, out_shape=out_shape,
                              input_output_aliases=aliases, scratch_shapes=list(scratch_shapes),
                              compiler_params=_params(*semantics))(*operands)
        return outs, []
    n_in, n_out, n_scr = len(in_specs), len(out_specs), len(scratch_shapes)
    r_in, r_out = len(ride.inputs), len(ride.out_shapes)

    def riding(*refs):
        refs = list(refs)
        ins, rins = refs[:n_in], refs[n_in:n_in + r_in]
        o0 = n_in + r_in
        outs, routs = refs[o0:o0 + n_out], refs[o0 + n_out:o0 + n_out + r_out]
        scr = refs[o0 + n_out + r_out:o0 + n_out + r_out + n_scr]
        send, recv = refs[-2:]
        ids = [pl.program_id(d) for d in range(len(grid))]
        first = functools.reduce(jnp.logical_and, [i == 0 for i in ids])
        last = functools.reduce(jnp.logical_and, [i == g - 1 for i, g in zip(ids, grid)])
        pl.when(first)(lambda: ride.start(rins, routs, send, recv))
        body(*ins, *outs, *scr)
        pl.when(last)(lambda: ride.finish(rins, routs, send, recv))

    res = pl.pallas_call(
        riding, name=name + "_" + ride.name, grid=grid,
        in_specs=list(in_specs) + [ANY] * r_in, out_specs=list(out_specs) + [ANY] * r_out,
        out_shape=list(out_shape) + list(ride.out_shapes),
        input_output_aliases={**aliases, **{n_in + a: n_out + b for a, b in ride.aliases.items()}},
        scratch_shapes=list(scratch_shapes) + ride.sems(),
        compiler_params=_params(*(["arbitrary"] * len(grid))),
    )(*operands, *ride.inputs)
    return res[:n_out], res[n_out:]


def _gather_ici(bufs, name="gather_ici"):
    n = len(bufs)

    def plan(ins, outs, send, recv):
        x, y, c, chips = _place()
        me = 2 * x + y
        sent, arriving = [], []
        for j, (px, py) in enumerate(chips):
            for a in range(n):
                mine, theirs = outs[a].at[:, me, c], outs[a].at[:, 2 * px + py, c]
                sent.append(_rcopy(mine, mine, send, recv, j * n + a, (px, py, c)))
                arriving.append(_rcopy(theirs, theirs, send, recv, j * n + a, (px, py, c)))
        return sent, arriving

    return _Exchange(name, bufs, [jax.ShapeDtypeStruct(b.shape, b.dtype) for b in bufs], 3 * n, plan, in_place=True)


def _gather_d2d(bufs, name="gather_d2d"):
    n = len(bufs)

    def plan(ins, outs, send, recv):
        x, y, c, chips = _place()
        sib = (x, y, 1 - c)
        sent, arriving = [], []
        for j, (px, py) in enumerate(chips):
            for a in range(n):
                got, theirs = outs[a].at[:, 2 * px + py, c], outs[a].at[:, 2 * px + py, 1 - c]
                sent.append(_rcopy(got, got, send, recv, j * n + a, sib))
                arriving.append(_rcopy(theirs, theirs, send, recv, j * n + a, sib))
        return sent, arriving

    return _Exchange(name, bufs, [jax.ShapeDtypeStruct(b.shape, b.dtype) for b in bufs], 3 * n, plan, in_place=True)


def _pair_exchange(parts):
    n = len(parts)

    def plan(ins, outs, send, recv):
        x, y, c, _ = _place()
        cps = [_rcopy(ins[a].at[:, :, 1 - c], outs[a], send, recv, a, (x, y, 1 - c)) for a in range(n)]
        return cps, cps

    return _Exchange("pair_exchange", parts,
                     [jax.ShapeDtypeStruct(p.shape[:2] + p.shape[3:], p.dtype) for p in parts], n, plan)


def _block_rows(r):
    return r if r <= 512 else r // 2


def _pair_add(part, got):
    L, K, _, r, C = part.shape
    rows = _block_rows(r)

    def body(c_ref, a_ref, b_ref, o_ref):
        o_ref[0, 0] = (a_ref[0, 0, 0] + b_ref[0, 0]).astype(bf16)

    blk = (1, 1, rows, C)
    return pl.pallas_call(
        body, name="pair_add",
        grid_spec=pltpu.PrefetchScalarGridSpec(
            num_scalar_prefetch=1, grid=(L, K, r // rows),
            in_specs=[pl.BlockSpec((1, 1, 1, rows, C), lambda l, k, i, c: (l, k, c[0], i, 0)),
                      pl.BlockSpec(blk, lambda l, k, i, c: (l, k, i, 0))],
            out_specs=pl.BlockSpec(blk, lambda l, k, i, c: (l, k, i, 0))),
        out_shape=jax.ShapeDtypeStruct((L, K, r, C), bf16),
        compiler_params=_params("parallel", "parallel", "parallel"),
    )(jnp.reshape(lax.axis_index("c"), (1,)).astype(jnp.int32), part, got)


def _chip_exchange(sums):
    n = len(sums)

    def plan(ins, outs, send, recv):
        x, y, c, chips = _place()
        cps = []
        for j, (px, py) in enumerate(chips):
            for a in range(n):
                cps.append(_rcopy(ins[a].at[:, 2 * px + py], outs[a].at[j], send, recv, j * n + a, (px, py, c)))
        return cps, cps

    return _Exchange("chip_exchange", sums,
                     [jax.ShapeDtypeStruct((3, s.shape[0]) + s.shape[2:], s.dtype) for s in sums], 3 * n, plan)


def _chip_sum(mine, got):
    L, K, r, C = mine.shape
    rows = _block_rows(r)

    def body(p_ref, a_ref, b_ref, o_ref):
        o_ref[0, 0] = (a_ref[0, 0].astype(f32) + b_ref[0, 0].astype(f32)) + (b_ref[1, 0].astype(f32) + b_ref[2, 0].astype(f32))

    place = jnp.stack([2 * lax.axis_index("x") + lax.axis_index("y"), lax.axis_index("c")]).astype(jnp.int32)
    return pl.pallas_call(
        body, name="chip_sum",
        grid_spec=pltpu.PrefetchScalarGridSpec(
            num_scalar_prefetch=1, grid=(L, r // rows),
            in_specs=[pl.BlockSpec((1, 1, rows, C), lambda l, i, p: (l, p[0], i, 0)),
                      pl.BlockSpec((3, 1, rows, C), lambda l, i, p: (0, l, i, 0))],
            out_specs=pl.BlockSpec((1, 1, rows, C), lambda l, i, p: (l, p[1], i, 0))),
        out_shape=jax.ShapeDtypeStruct((L, 2, r, C), f32),
        compiler_params=_params("parallel", "parallel"),
    )(place, mine, got)


def _pair_share(bufs):
    n = len(bufs)

    def plan(ins, outs, send, recv):
        x, y, c, _ = _place()
        sib = (x, y, 1 - c)
        sent = [_rcopy(outs[a].at[:, c], outs[a].at[:, c], send, recv, a, sib) for a in range(n)]
        arriving = [_rcopy(outs[a].at[:, 1 - c], outs[a].at[:, 1 - c], send, recv, a, sib) for a in range(n)]
        return sent, arriving

    return _Exchange("pair_share", bufs, [jax.ShapeDtypeStruct(b.shape, b.dtype) for b in bufs], n, plan, in_place=True)


def _all_sum_small(v):
    def body(v_ref, o_ref, buf, send, recv):
        x, y, c, _ = _place()
        me = 4 * x + 2 * y + c
        buf[me] = v_ref[...]
        cps = []
        for m in range(1, 8):
            to = (x ^ (m >> 2), y ^ ((m >> 1) & 1), c ^ (m & 1))
            cps.append(_rcopy(v_ref, buf.at[me], send, recv, m - 1, to))
        for cp in cps:
            cp.start()
        for cp in cps:
            cp.wait()
        acc = buf[0]
        for d in range(1, 8):
            acc = acc + buf[d]
        o_ref[...] = acc

    vm = pl.BlockSpec(memory_space=pltpu.VMEM)
    return pl.pallas_call(
        body, name="all_sum_small", in_specs=[vm], out_specs=vm,
        out_shape=jax.ShapeDtypeStruct(v.shape, v.dtype),
        scratch_shapes=[pltpu.VMEM((8,) + v.shape, v.dtype), pltpu.SemaphoreType.DMA((7,)), pltpu.SemaphoreType.DMA((7,))],
    )(v)


def _adamw_math(w, g, m, v):
    m = ADAM_B1 * m + (1.0 - ADAM_B1) * g
    v = ADAM_B2 * v + (1.0 - ADAM_B2) * (g * g)
    m_hat = m / (1.0 - ADAM_B1 ** ADAM_STEP)
    v_hat = v / (1.0 - ADAM_B2 ** ADAM_STEP)
    return -ADAM_LR * (m_hat / (jnp.sqrt(v_hat) + ADAM_EPS) + ADAM_WD * w), m, v


def _adamw(w, g, m, v):
    L, R, C = w.shape
    rows = R // 4

    def body(w_ref, g_ref, m_ref, v_ref, d_ref, mo_ref, vo_ref):
        d_ref[...], mo_ref[...], vo_ref[...] = _adamw_math(w_ref[...], g_ref[...], m_ref[...], v_ref[...])

    blk = pl.BlockSpec((1, rows, C), lambda l, i: (l, i, 0))
    return pl.pallas_call(
        body, name="adamw", grid=(L, R // rows), in_specs=[blk] * 4, out_specs=[blk] * 3,
        out_shape=[jax.ShapeDtypeStruct(w.shape, f32)] * 3,
        compiler_params=_params("parallel", "parallel"),
    )(w, g, m, v)


def _chip_index():
    return jnp.reshape(2 * lax.axis_index("x") + lax.axis_index("y"), (1,)).astype(jnp.int32)


def _shard_placed(w, l):
    _, R, C = w.shape
    rows = R // 4

    def body(k_ref, w_ref, o_ref):
        o_ref[0, 0] = w_ref[0].astype(bf16)

    return pl.pallas_call(
        body, name="shard_placed",
        grid_spec=pltpu.PrefetchScalarGridSpec(
            num_scalar_prefetch=1, grid=(R // rows,),
            in_specs=[pl.BlockSpec((1, rows, C), lambda i, k: (l, i, 0))],
            out_specs=pl.BlockSpec((1, 1, rows, C), lambda i, k: (0, k[0], i, 0))),
        out_shape=jax.ShapeDtypeStruct((1, 4, R, C), bf16),
        compiler_params=_params("parallel"),
    )(_chip_index(), w)


def _pack_small(g_pre, g_post, lb, g_head, sinks, loss=None):
    rows = []
    for l in range(DEPTH):
        tail = [g_head[l], sinks[l]]
        if loss is not None and l == 0:
            tail.append(jnp.reshape(loss, (1,)))
        tail = jnp.concatenate(tail)
        rows += [g_pre[l], g_post[l], lb[l], jnp.pad(tail, (0, D_MODEL - tail.shape[0]))]
    return jnp.stack(rows)


def _unpack_small(p):
    g_pre = jnp.stack([p[4 * l] for l in range(DEPTH)])
    g_post = jnp.stack([p[4 * l + 1] for l in range(DEPTH)])
    lb = jnp.stack([p[4 * l + 2] for l in range(DEPTH)])
    g_head = jnp.stack([p[4 * l + 3, :HG_HEAD_DIM] for l in range(DEPTH)])
    sinks = jnp.stack([p[4 * l + 3, HG_HEAD_DIM:HG_HEAD_DIM + ATT_HEADS] for l in range(DEPTH)])
    return g_pre, g_post, lb, g_head, sinks


def _small_update(gsum, w, m, v):
    def body(g_ref, w_ref, m_ref, v_ref, go_ref, d_ref, mo_ref, vo_ref):
        g = g_ref[...]
        w = w_ref[...]
        lbp = [w[4 * l + 2:4 * l + 3] for l in range(DEPTH)]
        mx = functools.reduce(jnp.maximum, lbp)
        e = [jnp.exp(t - mx) for t in lbp]
        tot = functools.reduce(jnp.add, e)
        p = [t / tot for t in e]
        glb = [g[4 * l + 2:4 * l + 3] for l in range(DEPTH)]
        row = lax.broadcasted_iota(jnp.int32, g.shape, 0)
        for j in range(DEPTH):
            gj = jnp.zeros_like(p[0])
            for l in range(DEPTH):
                for i in range(1, l + 1):
                    gj = gj + glb[l] * p[i] * ((1.0 if i == j else 0.0) - p[j])
            g = jnp.where(row == 4 * j + 2, gj, g)
        go_ref[...] = g
        d_ref[...], mo_ref[...], vo_ref[...] = _adamw_math(w, g, m_ref[...], v_ref[...])

    vm = pl.BlockSpec(memory_space=pltpu.VMEM)
    return pl.pallas_call(
        body, name="small_update", in_specs=[vm] * 4, out_specs=[vm] * 4,
        out_shape=[jax.ShapeDtypeStruct(gsum.shape, f32)] * 4,
    )(gsum, w, m, v)


def kernel(x, w_in, w_out, g_pre, g_post, lb_param, g_head, sinks, loss_target, m_w_in, m_w_out, m_g_pre, m_g_post, m_lb_param, m_g_head, m_sinks, v_w_in, v_w_out, v_g_pre, v_g_post, v_lb_param, v_g_head, v_sinks):
    B, S, _ = x.shape
    T = B * S
    L = DEPTH
    ri, ro = IN_WIDTH // 8, MIX_WIDTH // 8
    tr = lambda a: jnp.transpose(a, (0, 2, 1))
    wt, mt, vt = tr(w_in), tr(m_w_in), tr(v_w_in)
    bufs = [[_shard_placed(wt, l).reshape(1, 4, 2, ri, D_MODEL), _shard_placed(w_out, l).reshape(1, 4, 2, ro, D_MODEL)]
            for l in range(L)]
    loss, dx, grads, ggpre, ggpost, glb, gghead, gsinks = _train_step(
        x.reshape(T, D_MODEL), loss_target.reshape(T, D_MODEL), bufs, g_pre, g_post, lb_param, g_head, sinks,
        B=B, S=S, exchange=True)
    gwt_mine = jnp.concatenate([g[0] for g in grads], axis=0).reshape(L, 2 * ri, D_MODEL)
    grad_w_out = jnp.concatenate([g[1] for g in grads], axis=0).reshape(L, 2 * ro, D_MODEL)

    d_wt, nm_wt, nv_wt = _adamw(wt, gwt_mine, mt, vt)
    grad_w_in, d_w_in, nm_w_in, nv_w_in = tr(gwt_mine), tr(d_wt), tr(nm_wt), tr(nv_wt)
    d_w_out, nm_w_out, nv_w_out = _adamw(w_out, grad_w_out, m_w_out, v_w_out)

    gsum = _all_sum_small(_pack_small(ggpre, ggpost, glb, gghead, gsinks, loss))
    gs, ds, ms, vs = _small_update(
        gsum, _pack_small(g_pre, g_post, lb_param, g_head, sinks),
        _pack_small(m_g_pre, m_g_post, m_lb_param, m_g_head, m_sinks),
        _pack_small(v_g_pre, v_g_post, v_lb_param, v_g_head, v_sinks))
    loss_all = gsum[3, HG_HEAD_DIM + ATT_HEADS]
    return (loss_all, dx.reshape(B, S, D_MODEL), grad_w_in, grad_w_out, *_unpack_small(gs),
            d_w_in, d_w_out, *_unpack_small(ds), nm_w_in, nm_w_out, *_unpack_small(ms),
            nv_w_in, nv_w_out, *_unpack_small(vs))
```

```python
import functools
import math

import jax
import jax.numpy as jnp
from jax import lax
from jax.experimental import pallas as pl
from jax.experimental.pallas import tpu as pltpu

f32 = jnp.float32
bf16 = jnp.bfloat16

D_MODEL = 1024
DEPTH = 2
HG_WIDTH = 1024
HG_HEAD_DIM = 128
HG_HEADS = 8
CHUNK = 64
SUB = 16
ATT_WIDTH = 1024
ATT_HEAD_DIM = 64
ATT_HEADS = 16
ATT_GROUP = 8
KV_WIDTH = 128
ATT_BLOCK = 128
ATT_SCALE = 1.0 / math.sqrt(ATT_HEAD_DIM)
ROPE_THETA = 10000.0
IN_WIDTH = 6400
MIX_WIDTH = 2048
NORM_EPS = 1e-6
NEG_INF = -1e30
LB_FLOOR = 1e-20
LANES = 128
VMEM_LIMIT = 48 * 1024 * 1024

ADAM_LR = 0.001
ADAM_B1 = 0.9
ADAM_B2 = 0.999
ADAM_EPS = 1e-08
ADAM_WD = 0.01
ADAM_STEP = 10

QA_BLK, ZA_BLK, KV_BLK = 4, 5, 24

NT = (((1,), (1,)), ((), ()))
TN = (((0,), (0,)), ((), ()))


def _dot(a, b, dims=None):
    if dims is None:
        return jnp.dot(a, b, preferred_element_type=f32)
    return lax.dot_general(a, b, dims, preferred_element_type=f32)


def _sigmoid(x):
    return 1.0 / (1.0 + jnp.exp(-x))


def _params(*sem):
    return pltpu.CompilerParams(dimension_semantics=sem, vmem_limit_bytes=VMEM_LIMIT)


TAIL = IN_WIDTH - 5120


def _in_proj(x, g, wt, tail, l, *, tm=1024, ride=None):
    T = x.shape[0]
    tm = min(tm, T)
    nmain = 5120 // TAIL

    def body(x_ref, g_ref, w_ref, t_ref, p_ref, h_ref, hs):
        j = pl.program_id(1)

        @pl.when(j == 0)
        def _():
            xv = x_ref[...]
            r = lax.rsqrt(jnp.mean(xv * xv, axis=-1, keepdims=True) + NORM_EPS)
            hv = (xv * r * g_ref[...]).astype(bf16)
            hs[...] = hv
            h_ref[...] = hv

        @pl.when(j < nmain)
        def _():
            p_ref[...] = _dot(hs[...], w_ref[pl.ds(pl.multiple_of(j * TAIL, TAIL), TAIL), :], NT)

        @pl.when(j == nmain)
        def _():
            p_ref[...] = _dot(hs[...], t_ref[...], NT)

    resident = pl.Buffered(1)
    return _call(
        body, (x, g, wt, tail), name="in_proj", grid=(T // tm, nmain + 1),
        in_specs=[pl.BlockSpec((tm, D_MODEL), lambda i, j: (i, 0)),
                  pl.BlockSpec((1, D_MODEL), lambda i, j: (0, 0)),
                  pl.BlockSpec((None, nmain * TAIL, D_MODEL), lambda i, j: (l, 0, 0), pipeline_mode=resident),
                  pl.BlockSpec((None, TAIL, D_MODEL), lambda i, j: (l, 0, 0), pipeline_mode=resident)],
        out_specs=[pl.BlockSpec((tm, TAIL), lambda i, j: (i, j)),
                   pl.BlockSpec((tm, D_MODEL), lambda i, j: (i, 0))],
        out_shape=[jax.ShapeDtypeStruct((T, IN_WIDTH), f32), jax.ShapeDtypeStruct((T, D_MODEL), bf16)],
        scratch_shapes=[pltpu.VMEM((tm, D_MODEL), bf16)],
        semantics=("parallel", "arbitrary"), ride=ride)


def _out_proj(ch, ca, wo, l, x, g, *, tm=512):
    T = x.shape[0]
    tm = min(tm, T)
    half = MIX_WIDTH // 2

    def body(ch_ref, ca_ref, wo_ref, x_ref, g_ref, xn_ref, y_ref):
        y = _dot(ch_ref[...], wo_ref[0:half, :]) + _dot(ca_ref[...], wo_ref[half:MIX_WIDTH, :])
        r = lax.rsqrt(jnp.mean(y * y, axis=-1, keepdims=True) + NORM_EPS)
        y_ref[...] = y
        xn_ref[...] = x_ref[...] + y * r * g_ref[...]

    row = lambda i: (i, 0)
    fixed = lambda i: (0, 0)
    return pl.pallas_call(
        body, name="out_proj", grid=(T // tm,),
        in_specs=[pl.BlockSpec((tm, half), row), pl.BlockSpec((tm, half), row),
                  pl.BlockSpec((None, MIX_WIDTH, D_MODEL), lambda i: (l, 0, 0)), pl.BlockSpec((tm, D_MODEL), row),
                  pl.BlockSpec((1, D_MODEL), fixed)],
        out_specs=[pl.BlockSpec((tm, D_MODEL), row), pl.BlockSpec((tm, D_MODEL), row)],
        out_shape=[jax.ShapeDtypeStruct((T, D_MODEL), f32)] * 2,
        compiler_params=_params("parallel"),
    )(ch, ca, wo, x, g)


def _out_proj_loss(ch, ca, wo, l, x, g, target, *, tm=512):
    T = x.shape[0]
    tm = min(tm, T)
    half = MIX_WIDTH // 2

    def body(ch_ref, ca_ref, wo_ref, x_ref, g_ref, t_ref, d_ref, y_ref, l_ref):
        @pl.when(pl.program_id(0) == 0)
        def _():
            l_ref[...] = jnp.zeros_like(l_ref)
        y = _dot(ch_ref[...], wo_ref[0:half, :]) + _dot(ca_ref[...], wo_ref[half:MIX_WIDTH, :])
        r = lax.rsqrt(jnp.mean(y * y, axis=-1, keepdims=True) + NORM_EPS)
        y_ref[...] = y
        err = (x_ref[...] + y * r * g_ref[...]) - t_ref[...]
        d_ref[...] = err * (1.0 / D_MODEL)
        l_ref[...] += jnp.sum(err * err) * (0.5 / D_MODEL)

    row = lambda i: (i, 0)
    fixed = lambda i: (0, 0)
    return pl.pallas_call(
        body, name="out_proj_loss", grid=(T // tm,),
        in_specs=[pl.BlockSpec((tm, half), row), pl.BlockSpec((tm, half), row),
                  pl.BlockSpec((None, MIX_WIDTH, D_MODEL), lambda i: (l, 0, 0)), pl.BlockSpec((tm, D_MODEL), row),
                  pl.BlockSpec((1, D_MODEL), fixed), pl.BlockSpec((tm, D_MODEL), row)],
        out_specs=[pl.BlockSpec((tm, D_MODEL), row), pl.BlockSpec((tm, D_MODEL), row),
                   pl.BlockSpec((8, LANES), fixed)],
        out_shape=[jax.ShapeDtypeStruct((T, D_MODEL), f32)] * 2 + [jax.ShapeDtypeStruct((8, LANES), f32)],
        compiler_params=_params("arbitrary"),
    )(ch, ca, wo, x, g, target)


def _out_proj_bwd(dxn, y, g, wo, l, ch, ca, *, tm=512, ride=None):
    T = y.shape[0]
    tm = min(tm, T)
    half = MIX_WIDTH // 2

    def body(dx_ref, y_ref, g_ref, wo_ref, ch_ref, ca_ref, dch_ref, dca_ref, dwo_ref, dg_ref):
        @pl.when(pl.program_id(0) == 0)
        def _():
            dwo_ref[...] = jnp.zeros_like(dwo_ref)
            dg_ref[...] = jnp.zeros_like(dg_ref)
        y = y_ref[...]
        dx = dx_ref[...]
        r = lax.rsqrt(jnp.mean(y * y, axis=-1, keepdims=True) + NORM_EPS)
        gy = dx * g_ref[...]
        dy = r * gy - y * (r * r * r) * jnp.mean(gy * y, axis=-1, keepdims=True)
        dg_ref[...] += jnp.sum(dx * y * r, axis=0, keepdims=True)
        dyb = dy.astype(bf16)
        dch_ref[...] = _dot(dyb, wo_ref[0:half, :], NT)
        dca_ref[...] = _dot(dyb, wo_ref[half:MIX_WIDTH, :], NT)
        dwo_ref[0:half, :] += _dot(ch_ref[...], dyb, TN)
        dwo_ref[half:MIX_WIDTH, :] += _dot(ca_ref[...], dyb, TN)

    row = lambda i: (i, 0)
    fixed = lambda i: (0, 0)
    return _call(
        body, (dxn, y, g, wo, ch, ca), name="out_proj_bwd", grid=(T // tm,),
        in_specs=[pl.BlockSpec((tm, D_MODEL), row), pl.BlockSpec((tm, D_MODEL), row),
                  pl.BlockSpec((1, D_MODEL), fixed), pl.BlockSpec((None, MIX_WIDTH, D_MODEL), lambda i: (l, 0, 0)),
                  pl.BlockSpec((tm, half), row), pl.BlockSpec((tm, half), row)],
        out_specs=[pl.BlockSpec((tm, half), row), pl.BlockSpec((tm, half), row),
                   pl.BlockSpec((MIX_WIDTH, D_MODEL), fixed), pl.BlockSpec((1, D_MODEL), fixed)],
        out_shape=[jax.ShapeDtypeStruct((T, half), f32), jax.ShapeDtypeStruct((T, half), f32),
                   jax.ShapeDtypeStruct((MIX_WIDTH, D_MODEL), f32), jax.ShapeDtypeStruct((1, D_MODEL), f32)],
        semantics=("arbitrary",), ride=ride)


TILE = 256
PIECE_TILES = (4, 4, 4, 4, 4, 1, 4)
PIECE_START = tuple(sum(PIECE_TILES[:p]) for p in range(len(PIECE_TILES)))
N_TILES = sum(PIECE_TILES)


def _piece_specs(rows, index):
    def spec(s, n):
        def index_map(*g):
            r, t = index(*g)
            return r, jnp.clip(t - s, 0, n - 1)
        return pl.BlockSpec((rows, TILE), index_map)
    return [spec(s, n) for s, n in zip(PIECE_START, PIECE_TILES)]


def _for_piece(t, fn):
    for p, (s, n) in enumerate(zip(PIECE_START, PIECE_TILES)):
        pl.when((t >= s) & (t < s + n))(functools.partial(fn, p))


def _in_proj_bwd(pieces, wt, l, x, g, dxn, *, tm=512, blocks=None, dx_into=None, ride=None):
    T = x.shape[0]
    tm = min(tm, T)
    first, count = blocks or (0, T // tm)
    npc = len(pieces)
    starts = [sum(p.shape[1] for p in pieces[:i]) for i in range(npc)]
    extra = [] if dx_into is None else [dx_into]

    def body(*refs):
        dp_refs = refs[:npc]
        w_ref, x_ref, g_ref, dxn_ref = refs[npc:npc + 4]
        dx_ref, dg_ref = refs[npc + 4 + len(extra):]

        @pl.when(pl.program_id(0) == 0)
        def _():
            dg_ref[...] = jnp.zeros_like(dg_ref)
        dh = None
        for p in range(npc):
            term = _dot(dp_refs[p][...], w_ref[starts[p]:starts[p] + pieces[p].shape[1], :])
            dh = term if dh is None else dh + term
        xv = x_ref[...]
        r = lax.rsqrt(jnp.mean(xv * xv, axis=-1, keepdims=True) + NORM_EPS)
        gy = dh * g_ref[...]
        dx_ref[...] = dxn_ref[...] + r * gy - xv * (r * r * r) * jnp.mean(gy * xv, axis=-1, keepdims=True)
        dg_ref[...] += jnp.sum(dh * xv * r, axis=0, keepdims=True)

    rows = lambda i: (first + i, 0)
    return _call(
        body, (*pieces, wt, x, g, dxn, *extra), name="in_proj_bwd", grid=(count,),
        in_specs=[pl.BlockSpec((tm, p.shape[1]), rows) for p in pieces] + [
            pl.BlockSpec((None, IN_WIDTH, D_MODEL), lambda i: (l, 0, 0), pipeline_mode=pl.Buffered(1)),
            pl.BlockSpec((tm, D_MODEL), rows), pl.BlockSpec((1, D_MODEL), lambda i: (0, 0)),
            pl.BlockSpec((tm, D_MODEL), rows)] + [ANY] * len(extra),
        out_specs=[pl.BlockSpec((tm, D_MODEL), rows), pl.BlockSpec((1, D_MODEL), lambda i: (0, 0))],
        out_shape=[jax.ShapeDtypeStruct((T, D_MODEL), f32), jax.ShapeDtypeStruct((1, D_MODEL), f32)],
        semantics=("arbitrary",), ride=ride, aliases={npc + 4: 0} if extra else None)


def _grad_w_in(h, pieces, *, ride=None):
    T = h.shape[0]
    npc = len(pieces)

    def body(*refs):
        h_ref, dp_refs, o_ref = refs[0], refs[1:1 + npc], refs[1 + npc]

        def put(p):
            o_ref[...] = _dot(dp_refs[p][...], h_ref[...], TN)
        _for_piece(pl.program_id(0), put)

    return _call(
        body, (h, *pieces), name="grad_w_in", grid=(N_TILES,),
        in_specs=[pl.BlockSpec((T, D_MODEL), lambda j: (0, 0), pipeline_mode=pl.Buffered(1))]
        + _piece_specs(T, lambda j: (0, j)),
        out_specs=[pl.BlockSpec((TILE, D_MODEL), lambda j: (j, 0))],
        out_shape=[jax.ShapeDtypeStruct((IN_WIDTH, D_MODEL), f32)],
        semantics=("parallel",), ride=ride)


def _lower_bound(lbp, layer):
    m = jnp.max(lbp, axis=0, keepdims=True)
    e = jnp.exp(lbp - m)
    p = e / jnp.sum(e, axis=0, keepdims=True)
    acc = p[0:1]
    for i in range(1, layer + 1):
        acc = acc + p[i:i + 1]
    return acc - p[0:1]


def _gate_parts(qr, fr, lb, lbf):
    sq = _sigmoid(qr)
    e = jnp.exp(-jnp.abs(fr))
    inv = 1.0 / (1.0 + e)
    pos = fr >= 0
    sg = jnp.where(pos, inv, e * inv)
    nsg = jnp.where(pos, e * inv, inv)
    fg = lbf + (1.0 - lb) * sg
    return qr * sq, sq, sg, nsg, fg, jnp.log(fg), (1.0 - lb) * nsg


LEVELS = tuple(SUB << j for j in range((CHUNK // SUB).bit_length() - 1))


def _level_masks(transposed=False):
    t = lax.broadcasted_iota(jnp.int32, (CHUNK, CHUNK), 1 if transposed else 0)
    s = lax.broadcasted_iota(jnp.int32, (CHUNK, CHUNK), 0 if transposed else 1)
    return [(t % (2 * m) >= m) & (s % (2 * m) < m) & (t // (2 * m) == s // (2 * m)) for m in LEVELS]


def _level_anchor(b_s, row, m):
    beta = b_s[m - 1:m, :]
    for g in range(1, CHUNK // (2 * m)):
        beta = jnp.where(row >= g * 2 * m, b_s[g * 2 * m + m - 1:g * 2 * m + m, :], beta)
    return beta


FWD_INTERLEAVE = 16
BWD_INTERLEAVE = 8
SWA_INTERLEAVE = 4


def _interleaved(chunks, width):
    for g0 in range(0, len(chunks), width):
        live = chunks[g0:g0 + width]
        while live:
            for gen in list(live):
                try:
                    next(gen)
                except StopIteration:
                    live.remove(gen)


def _seg_sum(seg, x):
    hi = x.astype(bf16)
    return _dot(seg, hi) + _dot(seg, (x - hi.astype(f32)).astype(bf16))


def _hgrn_fwd(proj, lb_param, g_head, *, B, S, layer, ride=None):
    T = B * S
    TB = min(1024, S)
    nT, NC = S // TB, TB // CHUNK
    nC = S // CHUNK
    HD = HG_HEAD_DIM

    def body(q_ref, f_ref, i_ref, z_ref, lb_ref, gh_ref, cat_ref, op_ref, st_ref,
             s_scr, b_scr, k_scr):
        @pl.when(pl.program_id(2) == 0)
        def _():
            s_scr[...] = jnp.zeros_like(s_scr)
        lb = _lower_bound(lb_ref[...], layer)
        lbf = jnp.maximum(lb, LB_FLOOR)
        gh = gh_ref[...]
        r_i = lax.broadcasted_iota(jnp.int32, (CHUNK, CHUNK), 0)
        c_i = lax.broadcasted_iota(jnp.int32, (CHUNK, CHUNK), 1)
        tril = (r_i >= c_i).astype(bf16)
        rows8 = lax.broadcasted_iota(jnp.int32, (8, HD), 0)
        row_c = lax.broadcasted_iota(jnp.int32, (CHUNK, HD), 0)
        lane_c = lax.broadcasted_iota(jnp.int32, (8, CHUNK), 1)
        masks = _level_masks()

        def chunk(c, carried):
            rs = slice(c * CHUNK, (c + 1) * CHUNK)
            b_s, k_s = b_scr.at[c], k_scr.at[c]
            q, _, _, _, _, logf, k = _gate_parts(q_ref[rs, :], f_ref[rs, :], lb, lbf)
            v = i_ref[rs, :]
            b = _seg_sum(tril, logf)
            b_s[...] = b
            k_s[...] = k
            yield
            pieces = []
            for blk in range(CHUNK // SUB):
                r0 = blk * SUB
                bp = [b[r0 + 8 * i:r0 + 8 * i + 8] for i in range(SUB // 8)]
                qp = [q[r0 + 8 * i:r0 + 8 * i + 8] for i in range(SUB // 8)]
                ap = [jnp.zeros((8, CHUNK), f32) for _ in range(SUB // 8)]
                for s in range(SUB):
                    bs = b_s[r0 + s:r0 + s + 1, :]
                    ks = k_s[r0 + s:r0 + s + 1, :]
                    for i in range(s // 8, SUB // 8):
                        diff = bp[i] - bs
                        if i == s // 8:
                            diff = jnp.where(rows8 >= s - 8 * i, diff, NEG_INF)
                        col = jnp.sum(jnp.exp(diff) * qp[i] * ks, axis=1, keepdims=True)
                        ap[i] = jnp.where(lane_c == r0 + s, col, ap[i])
                pieces += ap
                yield
            a_all = jnp.concatenate(pieces, axis=0)
            for m, mk in zip(LEVELS, masks):
                beta = _level_anchor(b_s, row_c, m)
                qh = (q * jnp.exp(jnp.minimum(b - beta, 0.0))).astype(bf16)
                kh = (k * jnp.exp(jnp.minimum(beta - b, 0.0))).astype(bf16)
                a_all = a_all + jnp.where(mk, _dot(qh, kh, NT), 0.0)
            yield
            st = carried[0]
            st_ref[0, 0, c] = st
            vb16 = v.astype(bf16)
            o = _dot(a_all.astype(bf16), vb16) + _dot((q * jnp.exp(b)).astype(bf16), st.astype(bf16), NT)
            b_end = b_s[CHUNK - 1:CHUNK, :]
            kdec = (k * jnp.exp(b_end - b)).astype(bf16)
            carried[0] = jnp.exp(b_end) * st + _dot(vb16, kdec, TN)
            rr = lax.rsqrt(jnp.mean(o * o, axis=-1, keepdims=True) + NORM_EPS)
            zr = z_ref[rs, :]
            cat_ref[rs, :] = (o * rr * gh * (zr * _sigmoid(zr))).astype(bf16)
            op_ref[rs, :] = o

        carried = [s_scr[...]]
        _interleaved([chunk(c, carried) for c in range(NC)], FWD_INTERLEAVE)
        s_scr[...] = carried[0]

    def col(part):
        return pl.BlockSpec((TB, HD), lambda b, h, n: (b * nT + n, part * HG_HEADS + h))

    out_col = pl.BlockSpec((TB, HD), lambda b, h, n: (b * nT + n, h))
    return _call(
        body, (proj, proj, proj, proj, lb_param, g_head),
        name=f"hgrn_fwd_l{layer}", grid=(B, HG_HEADS, nT),
        in_specs=[col(0), col(1), col(2), col(3),
                  pl.BlockSpec((DEPTH, HD), lambda b, h, n: (0, h)),
                  pl.BlockSpec((1, HD), lambda b, h, n: (0, 0))],
        out_specs=[out_col, out_col,
                   pl.BlockSpec((1, 1, NC, HD, HD), lambda b, h, n: (b, h, n, 0, 0))],
        out_shape=[jax.ShapeDtypeStruct((T, HG_WIDTH), bf16), jax.ShapeDtypeStruct((T, HG_WIDTH), f32),
                   jax.ShapeDtypeStruct((B, HG_HEADS, nC, HD, HD), f32)],
        scratch_shapes=[pltpu.VMEM((HD, HD), f32), pltpu.VMEM((NC, CHUNK, HD), f32), pltpu.VMEM((NC, CHUNK, HD), f32)],
        semantics=("parallel", "parallel", "arbitrary"), ride=ride)


def _hgrn_bwd(proj, lb_param, g_head, o_pre, states, dcat, *, B, S, layer, ride=None):
    T = B * S
    TB = min(1024, S)
    nT, NC = S // TB, TB // CHUNK
    HD = HG_HEAD_DIM

    def body(q_ref, f_ref, i_ref, z_ref, lb_ref, gh_ref, op_ref, st_ref, dc_ref,
             dq_ref, df_ref, di_ref, dz_ref, dlb_ref, dgh_ref,
             ds_scr, b_scr, q_scr, do_scr, wk_scr):
        @pl.when(pl.program_id(2) == 0)
        def _():
            ds_scr[...] = jnp.zeros_like(ds_scr)
            dlb_ref[...] = jnp.zeros_like(dlb_ref)
            dgh_ref[...] = jnp.zeros_like(dgh_ref)
        lb = _lower_bound(lb_ref[...], layer)
        lbf = jnp.maximum(lb, LB_FLOOR)
        ind = (lb > LB_FLOOR).astype(f32)
        gh = gh_ref[...]
        r_i = lax.broadcasted_iota(jnp.int32, (CHUNK, CHUNK), 0)
        c_i = lax.broadcasted_iota(jnp.int32, (CHUNK, CHUNK), 1)
        tril = (r_i >= c_i).astype(bf16)
        triu = (c_i >= r_i).astype(bf16)
        rows8 = lax.broadcasted_iota(jnp.int32, (8, HD), 0)
        row_c = lax.broadcasted_iota(jnp.int32, (CHUNK, HD), 0)
        lane_c = lax.broadcasted_iota(jnp.int32, (8, CHUNK), 1)
        last_row = row_c == CHUNK - 1
        masks = _level_masks()
        masks_t = _level_masks(transposed=True)
        seg_t = lax.broadcasted_iota(jnp.int32, (SUB, 8 * SUB), 0)
        seg_r = lax.broadcasted_iota(jnp.int32, (SUB, 8 * SUB), 1) // 8
        seg0 = (seg_r == seg_t).astype(bf16)
        seg1 = (seg_r[:, 0:4 * SUB] + 8 == seg_t[:, 0:4 * SUB]).astype(bf16)

        def chunk(c, carried):
            rs = slice(c * CHUNK, (c + 1) * CHUNK)
            b_s, q_s, do_s = b_scr.at[c], q_scr.at[c], do_scr.at[c]
            qr, fr = q_ref[rs, :], f_ref[rs, :]
            q, sq, sg, nsg, fg, logf, k = _gate_parts(qr, fr, lb, lbf)
            v = i_ref[rs, :]
            b = _seg_sum(tril, logf)
            o = op_ref[rs, :]
            dc = dc_ref[rs, :]
            zr = z_ref[rs, :]
            sz = _sigmoid(zr)
            rr = lax.rsqrt(jnp.mean(o * o, axis=-1, keepdims=True) + NORM_EPS)
            dz_ref[rs, :] = (dc * (o * rr * gh) * (sz * (1.0 + zr * (1.0 - sz)))).astype(bf16)
            dn = dc * (zr * sz)
            dgh_ref[0, 0] += jnp.sum(dn * o * rr, axis=0, keepdims=True)
            gdn = dn * gh
            d_o = rr * gdn - o * (rr * rr * rr) * jnp.mean(gdn * o, axis=-1, keepdims=True)
            b_s[...] = b
            q_s[...] = q
            do_s[...] = d_o
            dob = d_o.astype(bf16)
            vb16 = v.astype(bf16)
            d_a = _dot(dob, vb16, NT)
            yield
            d_q = jnp.zeros((CHUNK, HD), f32)
            d_k = jnp.zeros((CHUNK, HD), f32)
            at_all = jnp.zeros((CHUNK, CHUNK), f32)
            for m, mk, mkt in zip(LEVELS, masks, masks_t):
                beta = _level_anchor(b_s, row_c, m)
                eq = jnp.exp(jnp.minimum(b - beta, 0.0))
                ek = jnp.exp(jnp.minimum(beta - b, 0.0))
                qh = (q * eq).astype(bf16)
                kh = (k * ek).astype(bf16)
                at_all = at_all + jnp.where(mkt, _dot(kh, qh, NT), 0.0)
                d_aa = jnp.where(mk, d_a, 0.0).astype(bf16)
                d_q = d_q + _dot(d_aa, kh) * eq
                d_k = d_k + _dot(d_aa, qh, TN) * ek
            yield
            dq_blocks, dk_pieces, at_pieces = [], [], []
            for blk in range(CHUNK // SUB):
                r0 = blk * SUB
                wk = wk_scr.at[c * (CHUNK // SUB) + blk]
                bp = [b[r0 + 8 * i:r0 + 8 * i + 8] for i in range(SUB // 8)]
                kp = [k[r0 + 8 * i:r0 + 8 * i + 8] for i in range(SUB // 8)]
                vp = [v[r0 + 8 * i:r0 + 8 * i + 8] for i in range(SUB // 8)]
                dkp = [jnp.zeros((8, HD), f32) for _ in range(SUB // 8)]
                atp = [jnp.zeros((8, CHUNK), f32) for _ in range(SUB // 8)]
                for t in range(SUB):
                    bt = b_s[r0 + t:r0 + t + 1, :]
                    qt = q_s[r0 + t:r0 + t + 1, :]
                    dot_ = do_s[r0 + t:r0 + t + 1, :]
                    for i in range(t // 8 + 1):
                        diff = bt - bp[i]
                        if i == t // 8:
                            diff = jnp.where(rows8 <= t - 8 * i, diff, NEG_INF)
                        e = jnp.exp(diff)
                        a = jnp.sum(e * kp[i] * qt, axis=1, keepdims=True)
                        atp[i] = jnp.where(lane_c == r0 + t, a, atp[i])
                        w = jnp.sum(vp[i] * dot_, axis=1, keepdims=True) * e
                        dkp[i] = dkp[i] + w * qt
                        row = 8 * t if i == 0 else 8 * SUB + 8 * (t - 8)
                        wk[row:row + 8, :] = w * kp[i]
                dq_blk = _seg_sum(seg0, wk[0:8 * SUB, :])
                if SUB > 8:
                    dq_blk = dq_blk + _seg_sum(seg1, wk[8 * SUB:12 * SUB, :])
                dq_blocks.append(dq_blk)
                dk_pieces += dkp
                at_pieces += atp
                yield
            dst1 = carried[0]
            st0 = st_ref[0, 0, c]
            dst1b = dst1.astype(bf16)
            eb = jnp.exp(b)
            b_end = b_s[CHUNK - 1:CHUNK, :]
            edec = jnp.exp(b_end - b)
            e_end = jnp.exp(b_end)
            kdec = (k * edec).astype(bf16)
            qdec = (q * eb).astype(bf16)
            st1 = e_end * st0 + _dot(vb16, kdec, TN)
            rterm = jnp.sum(dst1 * st1, axis=0, keepdims=True)
            carried[0] = e_end * dst1 + _dot(dob, qdec, TN)
            d_q = d_q + _dot(dob, st0.astype(bf16)) * eb + jnp.concatenate(dq_blocks, axis=0)
            d_k = d_k + _dot(vb16, dst1b) * edec + jnp.concatenate(dk_pieces, axis=0)
            d_v = _dot(kdec, dst1b, NT) + _dot((at_all + jnp.concatenate(at_pieces, axis=0)).astype(bf16), dob)
            db = q * d_q - k * d_k + jnp.where(last_row, rterm, 0.0)
            dlt = _seg_sum(triu, db) - fg * d_k
            df_ref[rs, :] = (dlt * (1.0 - lb) * sg * nsg / fg).astype(bf16)
            dlb_ref[0] += jnp.sum(dlt * (ind - sg) / fg, axis=0, keepdims=True)
            dq_ref[rs, :] = (d_q * (sq * (1.0 + qr * (1.0 - sq)))).astype(bf16)
            di_ref[rs, :] = d_v.astype(bf16)

        carried = [ds_scr[...]]
        _interleaved([chunk(c, carried) for c in reversed(range(NC))], BWD_INTERLEAVE)
        ds_scr[...] = carried[0]

    def col(part):
        return pl.BlockSpec((TB, HD), lambda b, h, n: (b * nT + nT - 1 - n, part * HG_HEADS + h))

    hcol = pl.BlockSpec((TB, HD), lambda b, h, n: (b * nT + nT - 1 - n, h))
    return _call(
        body, (proj, proj, proj, proj, lb_param, g_head, o_pre, states, dcat),
        name=f"hgrn_bwd_l{layer}", grid=(B, HG_HEADS, nT),
        in_specs=[col(0), col(1), col(2), col(3),
                  pl.BlockSpec((DEPTH, HD), lambda b, h, n: (0, h)),
                  pl.BlockSpec((1, HD), lambda b, h, n: (0, 0)),
                  hcol,
                  pl.BlockSpec((1, 1, NC, HD, HD), lambda b, h, n: (b, h, nT - 1 - n, 0, 0)),
                  hcol],
        out_specs=[hcol, hcol, hcol, hcol,
                   pl.BlockSpec((1, 1, HD), lambda b, h, n: (b, 0, h)),
                   pl.BlockSpec((1, 1, 1, HD), lambda b, h, n: (b, h, 0, 0))],
        out_shape=[jax.ShapeDtypeStruct((T, HG_WIDTH), bf16)] * 4 + [
            jax.ShapeDtypeStruct((B, 1, HG_WIDTH), f32), jax.ShapeDtypeStruct((B, HG_HEADS, 1, HD), f32)],
        scratch_shapes=[pltpu.VMEM((HD, HD), f32)] + [pltpu.VMEM((NC, CHUNK, HD), f32)] * 3
        + [pltpu.VMEM((NC * CHUNK // SUB, 12 * SUB, HD), f32)],
        semantics=("parallel", "parallel", "arbitrary"), ride=ride)


def _rope_tables(S):
    half = ATT_HEAD_DIM // 2
    inv_freq = ROPE_THETA ** (-jnp.arange(half, dtype=f32) / half)
    ang = jnp.arange(S, dtype=f32)[:, None] * inv_freq[None, :]
    cos, sin = jnp.cos(ang), jnp.sin(ang)
    return jnp.tile(jnp.concatenate([cos, cos], axis=1), (1, 2)), jnp.tile(jnp.concatenate([-sin, sin], axis=1), (1, 2))


def _swap_halves(x, first_half):
    return jnp.where(first_half, pltpu.roll(x, LANES - ATT_HEAD_DIM // 2, 1), pltpu.roll(x, ATT_HEAD_DIM // 2, 1))


def _rope(x, cos, sin, first_half):
    return x * cos + _swap_halves(x, first_half) * sin


def _rope_bwd(dy, cos, sin, first_half):
    return dy * cos + _swap_halves(dy * sin, first_half)


def _attn_consts(n):
    lane = lax.broadcasted_iota(jnp.int32, (1, LANES), 1)
    low = lane < ATT_HEAD_DIM
    first_half = (lane % ATT_HEAD_DIM) < ATT_HEAD_DIM // 2
    top = lax.broadcasted_iota(jnp.int32, (LANES, 1), 0) < ATT_HEAD_DIM
    s = lax.broadcasted_iota(jnp.int32, (2 * ATT_BLOCK, ATT_BLOCK), 0)
    t = lax.broadcasted_iota(jnp.int32, (2 * ATT_BLOCK, ATT_BLOCK), 1)
    mask = (s > t) & (s <= t + ATT_BLOCK) & ((s >= ATT_BLOCK) | (n > 0))
    return low, first_half, top, mask


def _dup_kv(x, low):
    rolled = pltpu.roll(x, ATT_HEAD_DIM, 1)
    return [jnp.where(low, x, rolled), jnp.where(low, rolled, x)]


def _attn_head(qtm, kd, vdt, sink, mask):
    s = jnp.where(mask, _dot(kd, qtm) * ATT_SCALE, NEG_INF)
    m = jnp.maximum(jnp.max(s, axis=0, keepdims=True), sink)
    p = jnp.exp(s - m)
    psink = jnp.exp(sink - m)
    inv = 1.0 / (jnp.sum(p, axis=0, keepdims=True) + psink)
    pn = p * inv
    return pn, psink * inv, _dot(vdt, pn.astype(bf16))


def _swa_fwd(proj, sink_b, cos, sin, *, B, S, ride=None):
    T = B * S
    L = ATT_BLOCK
    nB = S // L

    def body(q_ref, z_ref, kvc_ref, kvp_ref, sk_ref, cc_ref, sc_ref, cp_ref, sp_ref, cat_ref):
        n = pl.program_id(1)
        low, first_half, top, mask = _attn_consts(n)
        cc, sc = cc_ref[...], sc_ref[...]
        kc = _rope(kvc_ref[:, 0:LANES], cc, sc, first_half)
        kp = _rope(kvp_ref[:, 0:LANES], cp_ref[...], sp_ref[...], first_half)
        kd = [x.astype(bf16) for x in _dup_kv(jnp.concatenate([kp, kc], axis=0), low)]
        vdt = [x.T.astype(bf16) for x in _dup_kv(jnp.concatenate([kvp_ref[:, LANES:2 * LANES], kvc_ref[:, LANES:2 * LANES]], axis=0), low)]
        def head_pair(pair):
            cols = slice(pair * LANES, (pair + 1) * LANES)
            j = (2 * pair) // ATT_GROUP
            qt = _rope(q_ref[:, cols], cc, sc, first_half).T
            yield
            outs = []
            for hh in range(2):
                h = 2 * pair + hh
                qtm = jnp.where(top if hh == 0 else ~top, qt, 0.0).astype(bf16)
                outs.append(_attn_head(qtm, kd[j], vdt[j], sk_ref[h:h + 1, 0:1], mask)[2])
                yield
            zp = z_ref[:, cols]
            cat_ref[:, cols] = (jnp.where(top, outs[0], outs[1]).T * (zp * _sigmoid(zp))).astype(bf16)

        _interleaved([head_pair(p) for p in range(ATT_HEADS // 2)], SWA_INTERLEAVE)

    cur = lambda b, n: (b * nB + n, 0)
    return _call(
        body, (proj, proj, proj, proj, sink_b, cos, sin, cos, sin), name="swa_fwd", grid=(B, nB),
        in_specs=[pl.BlockSpec((L, ATT_WIDTH), lambda b, n: (b * nB + n, QA_BLK)),
                  pl.BlockSpec((L, ATT_WIDTH), lambda b, n: (b * nB + n, ZA_BLK)),
                  pl.BlockSpec((L, 2 * KV_WIDTH), lambda b, n: (b * nB + n, KV_BLK)),
                  pl.BlockSpec((L, 2 * KV_WIDTH), lambda b, n: (b * nB + jnp.maximum(n - 1, 0), KV_BLK)),
                  pl.BlockSpec((ATT_HEADS, LANES), lambda b, n: (0, 0)),
                  pl.BlockSpec((L, LANES), lambda b, n: (n, 0)), pl.BlockSpec((L, LANES), lambda b, n: (n, 0)),
                  pl.BlockSpec((L, LANES), lambda b, n: (jnp.maximum(n - 1, 0), 0)),
                  pl.BlockSpec((L, LANES), lambda b, n: (jnp.maximum(n - 1, 0), 0))],
        out_specs=[pl.BlockSpec((L, ATT_WIDTH), cur)],
        out_shape=[jax.ShapeDtypeStruct((T, ATT_WIDTH), bf16)],
        semantics=("parallel", "parallel"), ride=ride)


def _swa_bwd(proj, sink_b, cos, sin, dcat, *, B, S, ride=None):
    T = B * S
    L = ATT_BLOCK
    nB = S // L

    def body(q_ref, z_ref, kvc_ref, kvp_ref, sk_ref, cc_ref, sc_ref, cp_ref, sp_ref, dc_ref,
             dq_ref, dz_ref, dkv_ref, dsk_ref, carry, ds_st, pn_st, q_st, do_st):
        step = pl.program_id(1)
        n = nB - 1 - step

        @pl.when((pl.program_id(0) == 0) & (step == 0))
        def _():
            dsk_ref[...] = jnp.zeros_like(dsk_ref)

        @pl.when(step == 0)
        def _():
            carry[...] = jnp.zeros_like(carry)
        low, first_half, top, mask = _attn_consts(n)
        cc, sc, cp, sp = cc_ref[...], sc_ref[...], cp_ref[...], sp_ref[...]
        kc = _rope(kvc_ref[:, 0:LANES], cc, sc, first_half)
        kp = _rope(kvp_ref[:, 0:LANES], cp, sp, first_half)
        kdf = _dup_kv(jnp.concatenate([kp, kc], axis=0), low)
        vdf = _dup_kv(jnp.concatenate([kvp_ref[:, LANES:2 * LANES], kvc_ref[:, LANES:2 * LANES]], axis=0), low)
        kd = [x.astype(bf16) for x in kdf]
        vd = [x.astype(bf16) for x in vdf]
        kdt = [x.T.astype(bf16) for x in kdf]
        vdt = [x.T.astype(bf16) for x in vdf]
        dkd, dvd = [], []
        def head_pair(pair):
            cols = slice(pair * LANES, (pair + 1) * LANES)
            j = (2 * pair) // ATT_GROUP
            qp = _rope(q_ref[:, cols], cc, sc, first_half)
            qt = qp.T
            zp = z_ref[:, cols]
            dc = dc_ref[:, cols]
            sz = _sigmoid(zp)
            d_o = dc * (zp * sz)
            dot_ = d_o.T
            yield
            res = []
            for hh in range(2):
                rsel = top if hh == 0 else ~top
                qtm = jnp.where(rsel, qt, 0.0).astype(bf16)
                pn, psn, o = _attn_head(qtm, kd[j], vdt[j], sk_ref[2 * pair + hh:2 * pair + hh + 1, 0:1], mask)
                res.append((rsel, pn, psn, o))
                yield
            ot = jnp.where(top, res[0][3], res[1][3])
            dz_ref[:, cols] = (dc * ot.T * (sz * (1.0 + zp * (1.0 - sz)))).astype(bf16)
            dqts = []
            for hh in range(2):
                h = 2 * pair + hh
                rsel, pn, psn, _ = res[hh]
                lsel = low if hh == 0 else ~low
                dotm = jnp.where(rsel, dot_, 0.0)
                delta = jnp.sum(dotm * ot, axis=0, keepdims=True)
                dst = (pn * (_dot(vd[j], dotm.astype(bf16)) - delta) * ATT_SCALE).astype(bf16)
                dsk_ref[h:h + 1, :] += jnp.zeros((1, LANES), f32) - jnp.sum(psn * delta)
                dqts.append(_dot(kdt[j], dst))
                g = h % ATT_GROUP
                ds_st[:, g * LANES:(g + 1) * LANES] = dst
                pn_st[:, g * LANES:(g + 1) * LANES] = pn.astype(bf16)
                q_st[g * LANES:(g + 1) * LANES, :] = jnp.where(lsel, qp, 0.0).astype(bf16)
                do_st[g * LANES:(g + 1) * LANES, :] = jnp.where(lsel, d_o, 0.0).astype(bf16)
                yield
            dq_ref[:, cols] = _rope_bwd(jnp.where(top, dqts[0], dqts[1]).T, cc, sc, first_half).astype(bf16)

        pairs_per_group = ATT_GROUP // 2
        for grp in range(ATT_HEADS // ATT_GROUP):
            _interleaved([head_pair(grp * pairs_per_group + p) for p in range(pairs_per_group)], SWA_INTERLEAVE)
            dkd.append(_dot(ds_st[...], q_st[...]))
            dvd.append(_dot(pn_st[...], do_st[...]))
        dk = [x + pltpu.roll(x, ATT_HEAD_DIM, 1) for x in dkd]
        dv = [x + pltpu.roll(x, ATT_HEAD_DIM, 1) for x in dvd]
        dk = jnp.where(low, dk[0], dk[1])
        dv = jnp.where(low, dv[0], dv[1])
        dkv_ref[:, 0:LANES] = (_rope_bwd(dk[L:2 * L], cc, sc, first_half) + carry[:, 0:LANES]).astype(bf16)
        dkv_ref[:, LANES:2 * LANES] = (dv[L:2 * L] + carry[:, LANES:2 * LANES]).astype(bf16)
        carry[:, 0:LANES] = _rope_bwd(dk[0:L], cp, sp, first_half)
        carry[:, LANES:2 * LANES] = dv[0:L]

    rev = lambda b, s: b * nB + nB - 1 - s
    revp = lambda b, s: b * nB + jnp.maximum(nB - 2 - s, 0)
    wide = lambda blk: pl.BlockSpec((L, ATT_WIDTH), lambda b, s: (rev(b, s), blk))
    tab = pl.BlockSpec((L, LANES), lambda b, s: (nB - 1 - s, 0))
    tabp = pl.BlockSpec((L, LANES), lambda b, s: (jnp.maximum(nB - 2 - s, 0), 0))
    return _call(
        body, (proj, proj, proj, proj, sink_b, cos, sin, cos, sin, dcat), name="swa_bwd", grid=(B, nB),
        in_specs=[wide(QA_BLK), wide(ZA_BLK),
                  pl.BlockSpec((L, 2 * KV_WIDTH), lambda b, s: (rev(b, s), KV_BLK)),
                  pl.BlockSpec((L, 2 * KV_WIDTH), lambda b, s: (revp(b, s), KV_BLK)),
                  pl.BlockSpec((ATT_HEADS, LANES), lambda b, s: (0, 0)),
                  tab, tab, tabp, tabp, wide(0)],
        out_specs=[wide(0), wide(0), pl.BlockSpec((L, 2 * KV_WIDTH), lambda b, s: (rev(b, s), 0)),
                   pl.BlockSpec((ATT_HEADS, LANES), lambda b, s: (0, 0))],
        out_shape=[jax.ShapeDtypeStruct((T, ATT_WIDTH), bf16), jax.ShapeDtypeStruct((T, ATT_WIDTH), bf16),
                   jax.ShapeDtypeStruct((T, 2 * KV_WIDTH), bf16), jax.ShapeDtypeStruct((ATT_HEADS, LANES), f32)],
        scratch_shapes=[pltpu.VMEM((L, 2 * KV_WIDTH), f32),
                        pltpu.VMEM((2 * L, ATT_GROUP * LANES), bf16), pltpu.VMEM((2 * L, ATT_GROUP * LANES), bf16),
                        pltpu.VMEM((ATT_GROUP * LANES, LANES), bf16), pltpu.VMEM((ATT_GROUP * LANES, LANES), bf16)],
        semantics=("arbitrary", "arbitrary"), ride=ride)


def _train_step(x, target, bufs, g_pre, g_post, lb_param, g_head, sinks, *, B, S, exchange):
    L = DEPTH
    T = x.shape[0]
    ri, ro = IN_WIDTH // 8, MIX_WIDTH // 8
    cos, sin = _rope_tables(S)
    full = [list(b) for b in bufs]
    if exchange:
        full[0][0] = _run_exchange(_gather_d2d(_run_exchange(_gather_ici(bufs[0][:1]))))[0]
    saved = []
    for l in range(L):
        wt = full[l][0].reshape(1, IN_WIDTH, D_MODEL)
        tail = jnp.concatenate([wt[:, 5376:6400], wt[:, 5120:5376]], axis=1)
        first = exchange and l == 0
        ahead = exchange and l + 1 < L
        (proj, h), wo_landed = _in_proj(x, g_pre[l:l + 1], wt, tail, 0,
                                       ride=_gather_ici(bufs[0][1:], "gather_ici_wo") if first else None)
        (ch, o_pre, states), landed = _hgrn_fwd(
            proj, lb_param, g_head[l:l + 1], B=B, S=S, layer=l,
            ride=_merge(_gather_ici(bufs[l + 1]) if ahead else None,
                        _gather_d2d(wo_landed, "gather_d2d_wo") if first else None))
        if first:
            full[0][1] = landed[-1]
            landed = landed[:-1]
        wo = full[l][1].reshape(1, MIX_WIDTH, D_MODEL)
        sink_b = jnp.broadcast_to(sinks[l][:, None], (ATT_HEADS, LANES))
        (ca,), passed = _swa_fwd(proj, sink_b, cos, sin, B=B, S=S, ride=_gather_d2d(landed) if ahead else None)
        if ahead:
            full[l + 1] = list(passed)
        if l + 1 < L:
            xn, y = _out_proj(ch, ca, wo, 0, x, g_post[l:l + 1])
        else:
            dx, y, loss = _out_proj_loss(ch, ca, wo, 0, x, g_post[l:l + 1], target)
        saved.append((x, proj, h, ch, o_pre, states, sink_b, ca, y, wt, tail, wo))
        x = xn if l + 1 < L else None

    def reduce_tail(sums, recv):
        return _run_exchange(_pair_share([_chip_sum(s, r) for s, r in zip(sums, recv)]))

    grads = [None] * L
    waiting = None
    gg_pre, gg_post, g_lb, gg_head, g_sinks = [], [], [], [], []
    for l in reversed(range(L)):
        x_in, proj, h, ch, o_pre, states, sink_b, ca, y, wt, tail, wo = saved[l]
        (dch, dca, dwo, dgpost), got = _out_proj_bwd(dx, y, g_post[l:l + 1], wo, 0, ch, ca,
                                                     ride=_pair_exchange(waiting) if waiting else None)
        sums = [_pair_add(p, r) for p, r in zip(waiting, got)] if waiting else None
        (dq, df, di, dz, dlb, dgh), recv = _hgrn_bwd(proj, lb_param, g_head[l:l + 1], o_pre, states, dch, B=B, S=S,
                                                     layer=l, ride=_chip_exchange(sums) if waiting else None)
        at_end = exchange and l == 0
        part_o = [dwo.reshape(1, 4, 2, ro, D_MODEL)]
        halves = [_chip_sum(s, r) for s, r in zip(sums, recv)] if waiting else None
        (dqa, dza, dkv, dsk), rode = _swa_bwd(proj, sink_b, cos, sin, dca, B=B, S=S,
                                             ride=_merge(_pair_exchange(part_o) if at_end else None,
                                                         _pair_share(halves) if waiting else None))
        got_o = rode[:1]
        if waiting:
            grads[l + 1] = list(rode[-len(halves):])
        pieces = [dq, df, di, dz, dqa, dkv, dza]
        sums_o = [_pair_add(part_o[0], got_o[0])] if at_end else None
        (gwt,), recv_o = _grad_w_in(h, pieces, ride=_chip_exchange(sums_o) if at_end else None)
        part_t = [gwt.reshape(1, 4, 2, ri, D_MODEL)]
        if at_end:
            tm = min(512, T // 2)
            nb = T // tm
            na = max(1, nb // 4)
            (dx_a, dg_a), got_t = _in_proj_bwd(pieces, wt, 0, x_in, g_pre[l:l + 1], dx, tm=tm, blocks=(0, na),
                                               ride=_pair_exchange(part_t))
            sums_t = [_pair_add(part_t[0], got_t[0])]
            (dx, dg_b), recv_t = _in_proj_bwd(pieces, wt, 0, x_in, g_pre[l:l + 1], dx, tm=tm, blocks=(na, nb - na),
                                              dx_into=dx_a, ride=_chip_exchange(sums_t))
            dgpre = dg_a + dg_b
            grads[0] = reduce_tail(sums_t + sums_o, recv_t + recv_o)
        else:
            (dx, dgpre), _ = _in_proj_bwd(pieces, wt, 0, x_in, g_pre[l:l + 1], dx)
            if exchange:
                waiting = part_t + part_o
            else:
                grads[l] = [gwt, dwo]
        gg_pre.append(dgpre[0])
        gg_post.append(dgpost[0])
        g_lb.append(jnp.sum(dlb, axis=(0, 1)))
        gg_head.append(jnp.sum(dgh, axis=(0, 1, 2)))
        g_sinks.append(dsk[:, 0])
    rev = lambda xs: jnp.stack(xs[::-1])
    return loss[0, 0], dx, grads, rev(gg_pre), rev(gg_post), rev(g_lb), rev(gg_head), rev(g_sinks)


MESH = pl.DeviceIdType.MESH
ANY = pl.BlockSpec(memory_space=pl.ANY)


def _place():
    x, y, c = lax.axis_index("x"), lax.axis_index("y"), lax.axis_index("c")
    return x, y, c, [(1 - x, y), (x, 1 - y), (1 - x, 1 - y)]


def _rcopy(src, dst, send, recv, k, to):
    return pltpu.make_async_remote_copy(src_ref=src, dst_ref=dst, send_sem=send.at[k], recv_sem=recv.at[k],
                                        device_id=to, device_id_type=MESH)


class _Exchange:
    def __init__(self, name, inputs, out_shapes, n_sems, plan, in_place=False):
        self.name, self.inputs, self.out_shapes = name, list(inputs), list(out_shapes)
        self.n_sems, self.plan = n_sems, plan
        self.aliases = {a: a for a in range(len(inputs))} if in_place else {}

    def start(self, ins, outs, send, recv):
        for cp in self.plan(ins, outs, send, recv)[0]:
            cp.start()

    def finish(self, ins, outs, send, recv):
        sent, arriving = self.plan(ins, outs, send, recv)
        for cp in arriving:
            cp.wait_recv()
        for cp in sent:
            cp.wait_send()

    def sems(self):
        return [pltpu.SemaphoreType.DMA((self.n_sems,)), pltpu.SemaphoreType.DMA((self.n_sems,))]


class _SemView:
    def __init__(self, sems, offset):
        self.sems, self.offset = sems, offset

    @property
    def at(self):
        return self

    def __getitem__(self, k):
        return self.sems.at[self.offset + k]


def _both(a, b):
    ai, ao = len(a.inputs), len(a.out_shapes)

    def plan(ins, outs, send, recv):
        sa, ra = a.plan(ins[:ai], outs[:ao], send, recv)
        sb, rb = b.plan(ins[ai:], outs[ao:], _SemView(send, a.n_sems), _SemView(recv, a.n_sems))
        return sa + sb, ra + rb

    ex = _Exchange(a.name + "_" + b.name, a.inputs + b.inputs, a.out_shapes + b.out_shapes, a.n_sems + b.n_sems, plan)
    ex.aliases = {**a.aliases, **{ai + i: ao + o for i, o in b.aliases.items()}}
    return ex


def _merge(*rides):
    rides = [r for r in rides if r is not None]
    return functools.reduce(_both, rides) if rides else None


def _run_exchange(ex):
    n_in, n_out = len(ex.inputs), len(ex.out_shapes)

    def body(*refs):
        ins, outs = refs[:n_in], refs[n_in:n_in + n_out]
        send, recv = refs[n_in + n_out:]
        ex.start(ins, outs, send, recv)
        ex.finish(ins, outs, send, recv)

    return pl.pallas_call(
        body, name=ex.name, in_specs=[ANY] * n_in, out_specs=[ANY] * n_out, out_shape=ex.out_shapes,
        input_output_aliases=ex.aliases, scratch_shapes=ex.sems(),
    )(*ex.inputs)


def _call(body, operands, *, name, grid, in_specs, out_specs, out_shape, scratch_shapes=(), semantics, ride=None,
          aliases=None):
    aliases = dict(aliases or {})
    if ride is None:
        outs = pl.pallas_call(body, name=name, grid=grid, in_specs=in_specs, out_specs=out_specs, out_shape=out_shape,
                              input_output_aliases=aliases, scratch_shapes=list(scratch_shapes),
                              compiler_params=_params(*semantics))(*operands)
        return outs, []
    n_in, n_out, n_scr = len(in_specs), len(out_specs), len(scratch_shapes)
    r_in, r_out = len(ride.inputs), len(ride.out_shapes)

    def riding(*refs):
        refs = list(refs)
        ins, rins = refs[:n_in], refs[n_in:n_in + r_in]
        o0 = n_in + r_in
        outs, routs = refs[o0:o0 + n_out], refs[o0 + n_out:o0 + n_out + r_out]
        scr = refs[o0 + n_out + r_out:o0 + n_out + r_out + n_scr]
        send, recv = refs[-2:]
        ids = [pl.program_id(d) for d in range(len(grid))]
        first = functools.reduce(jnp.logical_and, [i == 0 for i in ids])
        last = functools.reduce(jnp.logical_and, [i == g - 1 for i, g in zip(ids, grid)])
        pl.when(first)(lambda: ride.start(rins, routs, send, recv))
        body(*ins, *outs, *scr)
        pl.when(last)(lambda: ride.finish(rins, routs, send, recv))

    res = pl.pallas_call(
        riding, name=name + "_" + ride.name, grid=grid,
        in_specs=list(in_specs) + [ANY] * r_in, out_specs=list(out_specs) + [ANY] * r_out,
        out_shape=list(out_shape) + list(ride.out_shapes),
        input_output_aliases={**aliases, **{n_in + a: n_out + b for a, b in ride.aliases.items()}},
        scratch_shapes=list(scratch_shapes) + ride.sems(),
        compiler_params=_params(*(["arbitrary"] * len(grid))),
    )(*operands, *ride.inputs)
    return res[:n_out], res[n_out:]


def _gather_ici(bufs, name="gather_ici"):
    n = len(bufs)

    def plan(ins, outs, send, recv):
        x, y, c, chips = _place()
        me = 2 * x + y
        sent, arriving = [], []
        for j, (px, py) in enumerate(chips):
            for a in range(n):
                mine, theirs = outs[a].at[:, me, c], outs[a].at[:, 2 * px + py, c]
                sent.append(_rcopy(mine, mine, send, recv, j * n + a, (px, py, c)))
                arriving.append(_rcopy(theirs, theirs, send, recv, j * n + a, (px, py, c)))
        return sent, arriving

    return _Exchange(name, bufs, [jax.ShapeDtypeStruct(b.shape, b.dtype) for b in bufs], 3 * n, plan, in_place=True)


def _gather_d2d(bufs, name="gather_d2d"):
    n = len(bufs)

    def plan(ins, outs, send, recv):
        x, y, c, chips = _place()
        sib = (x, y, 1 - c)
        sent, arriving = [], []
        for j, (px, py) in enumerate(chips):
            for a in range(n):
                got, theirs = outs[a].at[:, 2 * px + py, c], outs[a].at[:, 2 * px + py, 1 - c]
                sent.append(_rcopy(got, got, send, recv, j * n + a, sib))
                arriving.append(_rcopy(theirs, theirs, send, recv, j * n + a, sib))
        return sent, arriving

    return _Exchange(name, bufs, [jax.ShapeDtypeStruct(b.shape, b.dtype) for b in bufs], 3 * n, plan, in_place=True)


def _pair_exchange(parts):
    n = len(parts)

    def plan(ins, outs, send, recv):
        x, y, c, _ = _place()
        cps = [_rcopy(ins[a].at[:, :, 1 - c], outs[a], send, recv, a, (x, y, 1 - c)) for a in range(n)]
        return cps, cps

    return _Exchange("pair_exchange", parts,
                     [jax.ShapeDtypeStruct(p.shape[:2] + p.shape[3:], p.dtype) for p in parts], n, plan)


def _block_rows(r):
    return r if r <= 512 else r // 2


def _pair_add(part, got):
    L, K, _, r, C = part.shape
    rows = _block_rows(r)

    def body(c_ref, a_ref, b_ref, o_ref):
        o_ref[0, 0] = (a_ref[0, 0, 0] + b_ref[0, 0]).astype(bf16)

    blk = (1, 1, rows, C)
    return pl.pallas_call(
        body, name="pair_add",
        grid_spec=pltpu.PrefetchScalarGridSpec(
            num_scalar_prefetch=1, grid=(L, K, r // rows),
            in_specs=[pl.BlockSpec((1, 1, 1, rows, C), lambda l, k, i, c: (l, k, c[0], i, 0)),
                      pl.BlockSpec(blk, lambda l, k, i, c: (l, k, i, 0))],
            out_specs=pl.BlockSpec(blk, lambda l, k, i, c: (l, k, i, 0))),
        out_shape=jax.ShapeDtypeStruct((L, K, r, C), bf16),
        compiler_params=_params("parallel", "parallel", "parallel"),
    )(jnp.reshape(lax.axis_index("c"), (1,)).astype(jnp.int32), part, got)


def _chip_exchange(sums):
    n = len(sums)

    def plan(ins, outs, send, recv):
        x, y, c, chips = _place()
        cps = []
        for j, (px, py) in enumerate(chips):
            for a in range(n):
                cps.append(_rcopy(ins[a].at[:, 2 * px + py], outs[a].at[j], send, recv, j * n + a, (px, py, c)))
        return cps, cps

    return _Exchange("chip_exchange", sums,
                     [jax.ShapeDtypeStruct((3, s.shape[0]) + s.shape[2:], s.dtype) for s in sums], 3 * n, plan)


def _chip_sum(mine, got):
    L, K, r, C = mine.shape
    rows = _block_rows(r)

    def body(p_ref, a_ref, b_ref, o_ref):
        o_ref[0, 0] = (a_ref[0, 0].astype(f32) + b_ref[0, 0].astype(f32)) + (b_ref[1, 0].astype(f32) + b_ref[2, 0].astype(f32))

    place = jnp.stack([2 * lax.axis_index("x") + lax.axis_index("y"), lax.axis_index("c")]).astype(jnp.int32)
    return pl.pallas_call(
        body, name="chip_sum",
        grid_spec=pltpu.PrefetchScalarGridSpec(
            num_scalar_prefetch=1, grid=(L, r // rows),
            in_specs=[pl.BlockSpec((1, 1, rows, C), lambda l, i, p: (l, p[0], i, 0)),
                      pl.BlockSpec((3, 1, rows, C), lambda l, i, p: (0, l, i, 0))],
            out_specs=pl.BlockSpec((1, 1, rows, C), lambda l, i, p: (l, p[1], i, 0))),
        out_shape=jax.ShapeDtypeStruct((L, 2, r, C), f32),
        compiler_params=_params("parallel", "parallel"),
    )(place, mine, got)


def _pair_share(bufs):
    n = len(bufs)

    def plan(ins, outs, send, recv):
        x, y, c, _ = _place()
        sib = (x, y, 1 - c)
        sent = [_rcopy(outs[a].at[:, c], outs[a].at[:, c], send, recv, a, sib) for a in range(n)]
        arriving = [_rcopy(outs[a].at[:, 1 - c], outs[a].at[:, 1 - c], send, recv, a, sib) for a in range(n)]
        return sent, arriving

    return _Exchange("pair_share", bufs, [jax.ShapeDtypeStruct(b.shape, b.dtype) for b in bufs], n, plan, in_place=True)


def _all_sum_small(v):
    def body(v_ref, o_ref, buf, send, recv):
        x, y, c, _ = _place()
        me = 4 * x + 2 * y + c
        buf[me] = v_ref[...]
        cps = []
        for m in range(1, 8):
            to = (x ^ (m >> 2), y ^ ((m >> 1) & 1), c ^ (m & 1))
            cps.append(_rcopy(v_ref, buf.at[me], send, recv, m - 1, to))
        for cp in cps:
            cp.start()
        for cp in cps:
            cp.wait()
        acc = buf[0]
        for d in range(1, 8):
            acc = acc + buf[d]
        o_ref[...] = acc

    vm = pl.BlockSpec(memory_space=pltpu.VMEM)
    return pl.pallas_call(
        body, name="all_sum_small", in_specs=[vm], out_specs=vm,
        out_shape=jax.ShapeDtypeStruct(v.shape, v.dtype),
        scratch_shapes=[pltpu.VMEM((8,) + v.shape, v.dtype), pltpu.SemaphoreType.DMA((7,)), pltpu.SemaphoreType.DMA((7,))],
    )(v)


def _adamw_math(w, g, m, v):
    m = ADAM_B1 * m + (1.0 - ADAM_B1) * g
    v = ADAM_B2 * v + (1.0 - ADAM_B2) * (g * g)
    m_hat = m / (1.0 - ADAM_B1 ** ADAM_STEP)
    v_hat = v / (1.0 - ADAM_B2 ** ADAM_STEP)
    return -ADAM_LR * (m_hat / (jnp.sqrt(v_hat) + ADAM_EPS) + ADAM_WD * w), m, v


def _adamw(w, g, m, v):
    L, R, C = w.shape
    rows = R // 4

    def body(w_ref, g_ref, m_ref, v_ref, d_ref, mo_ref, vo_ref):
        d_ref[...], mo_ref[...], vo_ref[...] = _adamw_math(w_ref[...], g_ref[...], m_ref[...], v_ref[...])

    blk = pl.BlockSpec((1, rows, C), lambda l, i: (l, i, 0))
    return pl.pallas_call(
        body, name="adamw", grid=(L, R // rows), in_specs=[blk] * 4, out_specs=[blk] * 3,
        out_shape=[jax.ShapeDtypeStruct(w.shape, f32)] * 3,
        compiler_params=_params("parallel", "parallel"),
    )(w, g, m, v)


def _chip_index():
    return jnp.reshape(2 * lax.axis_index("x") + lax.axis_index("y"), (1,)).astype(jnp.int32)


def _shard_placed(w, l):
    _, R, C = w.shape
    rows = R // 4

    def body(k_ref, w_ref, o_ref):
        o_ref[0, 0] = w_ref[0].astype(bf16)

    return pl.pallas_call(
        body, name="shard_placed",
        grid_spec=pltpu.PrefetchScalarGridSpec(
            num_scalar_prefetch=1, grid=(R // rows,),
            in_specs=[pl.BlockSpec((1, rows, C), lambda i, k: (l, i, 0))],
            out_specs=pl.BlockSpec((1, 1, rows, C), lambda i, k: (0, k[0], i, 0))),
        out_shape=jax.ShapeDtypeStruct((1, 4, R, C), bf16),
        compiler_params=_params("parallel"),
    )(_chip_index(), w)


def _pack_small(g_pre, g_post, lb, g_head, sinks, loss=None):
    rows = []
    for l in range(DEPTH):
        tail = [g_head[l], sinks[l]]
        if loss is not None and l == 0:
            tail.append(jnp.reshape(loss, (1,)))
        tail = jnp.concatenate(tail)
        rows += [g_pre[l], g_post[l], lb[l], jnp.pad(tail, (0, D_MODEL - tail.shape[0]))]
    return jnp.stack(rows)


def _unpack_small(p):
    g_pre = jnp.stack([p[4 * l] for l in range(DEPTH)])
    g_post = jnp.stack([p[4 * l + 1] for l in range(DEPTH)])
    lb = jnp.stack([p[4 * l + 2] for l in range(DEPTH)])
    g_head = jnp.stack([p[4 * l + 3, :HG_HEAD_DIM] for l in range(DEPTH)])
    sinks = jnp.stack([p[4 * l + 3, HG_HEAD_DIM:HG_HEAD_DIM + ATT_HEADS] for l in range(DEPTH)])
    return g_pre, g_post, lb, g_head, sinks


def _small_update(gsum, w, m, v):
    def body(g_ref, w_ref, m_ref, v_ref, go_ref, d_ref, mo_ref, vo_ref):
        g = g_ref[...]
        w = w_ref[...]
        lbp = [w[4 * l + 2:4 * l + 3] for l in range(DEPTH)]
        mx = functools.reduce(jnp.maximum, lbp)
        e = [jnp.exp(t - mx) for t in lbp]
        tot = functools.reduce(jnp.add, e)
        p = [t / tot for t in e]
        glb = [g[4 * l + 2:4 * l + 3] for l in range(DEPTH)]
        row = lax.broadcasted_iota(jnp.int32, g.shape, 0)
        for j in range(DEPTH):
            gj = jnp.zeros_like(p[0])
            for l in range(DEPTH):
                for i in range(1, l + 1):
                    gj = gj + glb[l] * p[i] * ((1.0 if i == j else 0.0) - p[j])
            g = jnp.where(row == 4 * j + 2, gj, g)
        go_ref[...] = g
        d_ref[...], mo_ref[...], vo_ref[...] = _adamw_math(w, g, m_ref[...], v_ref[...])

    vm = pl.BlockSpec(memory_space=pltpu.VMEM)
    return pl.pallas_call(
        body, name="small_update", in_specs=[vm] * 4, out_specs=[vm] * 4,
        out_shape=[jax.ShapeDtypeStruct(gsum.shape, f32)] * 4,
    )(gsum, w, m, v)


def kernel(x, w_in, w_out, g_pre, g_post, lb_param, g_head, sinks, loss_target, m_w_in, m_w_out, m_g_pre, m_g_post, m_lb_param, m_g_head, m_sinks, v_w_in, v_w_out, v_g_pre, v_g_post, v_lb_param, v_g_head, v_sinks):
    B, S, _ = x.shape
    T = B * S
    L = DEPTH
    ri, ro = IN_WIDTH // 8, MIX_WIDTH // 8
    tr = lambda a: jnp.transpose(a, (0, 2, 1))
    wt, mt, vt = tr(w_in), tr(m_w_in), tr(v_w_in)
    bufs = [[_shard_placed(wt, l).reshape(1, 4, 2, ri, D_MODEL), _shard_placed(w_out, l).reshape(1, 4, 2, ro, D_MODEL)]
            for l in range(L)]
    loss, dx, grads, ggpre, ggpost, glb, gghead, gsinks = _train_step(
        x.reshape(T, D_MODEL), loss_target.reshape(T, D_MODEL), bufs, g_pre, g_post, lb_param, g_head, sinks,
        B=B, S=S, exchange=True)
    gwt_mine = jnp.concatenate([g[0] for g in grads], axis=0).reshape(L, 2 * ri, D_MODEL)
    grad_w_out = jnp.concatenate([g[1] for g in grads], axis=0).reshape(L, 2 * ro, D_MODEL)

    d_wt, nm_wt, nv_wt = _adamw(wt, gwt_mine, mt, vt)
    grad_w_in, d_w_in, nm_w_in, nv_w_in = tr(gwt_mine), tr(d_wt), tr(nm_wt), tr(nv_wt)
    d_w_out, nm_w_out, nv_w_out = _adamw(w_out, grad_w_out, m_w_out, v_w_out)

    gsum = _all_sum_small(_pack_small(ggpre, ggpost, glb, gghead, gsinks, loss))
    gs, ds, ms, vs = _small_update(
        gsum, _pack_small(g_pre, g_post, lb_param, g_head, sinks),
        _pack_small(m_g_pre, m_g_post, m_lb_param, m_g_head, m_sinks),
        _pack_small(v_g_pre, v_g_post, v_lb_param, v_g_head, v_sinks))
    loss_all = gsum[3, HG_HEAD_DIM + ATT_HEADS]
    return (loss_all, dx.reshape(B, S, D_MODEL), grad_w_in, grad_w_out, *_unpack_small(gs),
            d_w_in, d_w_out, *_unpack_small(ds), nm_w_in, nm_w_out, *_unpack_small(ms),
            nv_w_in, nv_w_out, *_unpack_small(vs))
```

```python
import functools
import math

import jax
import jax.numpy as jnp
from jax import lax
from jax.experimental import pallas as pl
from jax.experimental.pallas import tpu as pltpu

f32 = jnp.float32
bf16 = jnp.bfloat16

D_MODEL = 1024
DEPTH = 2
HG_WIDTH = 1024
HG_HEAD_DIM = 128
HG_HEADS = 8
CHUNK = 64
SUB = 16
ATT_WIDTH = 1024
ATT_HEAD_DIM = 64
ATT_HEADS = 16
ATT_GROUP = 8
KV_WIDTH = 128
ATT_BLOCK = 128
ATT_SCALE = 1.0 / math.sqrt(ATT_HEAD_DIM)
ROPE_THETA = 10000.0
IN_WIDTH = 6400
MIX_WIDTH = 2048
NORM_EPS = 1e-6
NEG_INF = -1e30
LB_FLOOR = 1e-20
LANES = 128
VMEM_LIMIT = 48 * 1024 * 1024

ADAM_LR = 0.001
ADAM_B1 = 0.9
ADAM_B2 = 0.999
ADAM_EPS = 1e-08
ADAM_WD = 0.01
ADAM_STEP = 10

QA_BLK, ZA_BLK, KV_BLK = 4, 5, 24

NT = (((1,), (1,)), ((), ()))
TN = (((0,), (0,)), ((), ()))


def _dot(a, b, dims=None):
    if dims is None:
        return jnp.dot(a, b, preferred_element_type=f32)
    return lax.dot_general(a, b, dims, preferred_element_type=f32)


def _sigmoid(x):
    return 1.0 / (1.0 + jnp.exp(-x))


def _params(*sem):
    return pltpu.CompilerParams(dimension_semantics=sem, vmem_limit_bytes=VMEM_LIMIT)


TAIL = IN_WIDTH - 5120


def _in_proj(x, g, wt, tail, l, *, tm=1024, ride=None):
    T = x.shape[0]
    tm = min(tm, T)
    nmain = 5120 // TAIL

    def body(x_ref, g_ref, w_ref, t_ref, p_ref, h_ref, hs):
        j = pl.program_id(1)

        @pl.when(j == 0)
        def _():
            xv = x_ref[...]
            r = lax.rsqrt(jnp.mean(xv * xv, axis=-1, keepdims=True) + NORM_EPS)
            hv = (xv * r * g_ref[...]).astype(bf16)
            hs[...] = hv
            h_ref[...] = hv

        @pl.when(j < nmain)
        def _():
            p_ref[...] = _dot(hs[...], w_ref[pl.ds(pl.multiple_of(j * TAIL, TAIL), TAIL), :], NT)

        @pl.when(j == nmain)
        def _():
            p_ref[...] = _dot(hs[...], t_ref[...], NT)

    resident = pl.Buffered(1)
    return _call(
        body, (x, g, wt, tail), name="in_proj", grid=(T // tm, nmain + 1),
        in_specs=[pl.BlockSpec((tm, D_MODEL), lambda i, j: (i, 0)),
                  pl.BlockSpec((1, D_MODEL), lambda i, j: (0, 0)),
                  pl.BlockSpec((None, nmain * TAIL, D_MODEL), lambda i, j: (l, 0, 0), pipeline_mode=resident),
                  pl.BlockSpec((None, TAIL, D_MODEL), lambda i, j: (l, 0, 0), pipeline_mode=resident)],
        out_specs=[pl.BlockSpec((tm, TAIL), lambda i, j: (i, j)),
                   pl.BlockSpec((tm, D_MODEL), lambda i, j: (i, 0))],
        out_shape=[jax.ShapeDtypeStruct((T, IN_WIDTH), f32), jax.ShapeDtypeStruct((T, D_MODEL), bf16)],
        scratch_shapes=[pltpu.VMEM((tm, D_MODEL), bf16)],
        semantics=("parallel", "arbitrary"), ride=ride)


def _out_proj(ch, ca, wo, l, x, g, *, tm=512):
    T = x.shape[0]
    tm = min(tm, T)
    half = MIX_WIDTH // 2

    def body(ch_ref, ca_ref, wo_ref, x_ref, g_ref, xn_ref, y_ref):
        y = _dot(ch_ref[...], wo_ref[0:half, :]) + _dot(ca_ref[...], wo_ref[half:MIX_WIDTH, :])
        r = lax.rsqrt(jnp.mean(y * y, axis=-1, keepdims=True) + NORM_EPS)
        y_ref[...] = y
        xn_ref[...] = x_ref[...] + y * r * g_ref[...]

    row = lambda i: (i, 0)
    fixed = lambda i: (0, 0)
    return pl.pallas_call(
        body, name="out_proj", grid=(T // tm,),
        in_specs=[pl.BlockSpec((tm, half), row), pl.BlockSpec((tm, half), row),
                  pl.BlockSpec((None, MIX_WIDTH, D_MODEL), lambda i: (l, 0, 0)), pl.BlockSpec((tm, D_MODEL), row),
                  pl.BlockSpec((1, D_MODEL), fixed)],
        out_specs=[pl.BlockSpec((tm, D_MODEL), row), pl.BlockSpec((tm, D_MODEL), row)],
        out_shape=[jax.ShapeDtypeStruct((T, D_MODEL), f32)] * 2,
        compiler_params=_params("parallel"),
    )(ch, ca, wo, x, g)


def _out_proj_loss(ch, ca, wo, l, x, g, target, *, tm=512):
    T = x.shape[0]
    tm = min(tm, T)
    half = MIX_WIDTH // 2

    def body(ch_ref, ca_ref, wo_ref, x_ref, g_ref, t_ref, d_ref, y_ref, l_ref):
        @pl.when(pl.program_id(0) == 0)
        def _():
            l_ref[...] = jnp.zeros_like(l_ref)
        y = _dot(ch_ref[...], wo_ref[0:half, :]) + _dot(ca_ref[...], wo_ref[half:MIX_WIDTH, :])
        r = lax.rsqrt(jnp.mean(y * y, axis=-1, keepdims=True) + NORM_EPS)
        y_ref[...] = y
        err = (x_ref[...] + y * r * g_ref[...]) - t_ref[...]
        d_ref[...] = err * (1.0 / D_MODEL)
        l_ref[...] += jnp.sum(err * err) * (0.5 / D_MODEL)

    row = lambda i: (i, 0)
    fixed = lambda i: (0, 0)
    return pl.pallas_call(
        body, name="out_proj_loss", grid=(T // tm,),
        in_specs=[pl.BlockSpec((tm, half), row), pl.BlockSpec((tm, half), row),
                  pl.BlockSpec((None, MIX_WIDTH, D_MODEL), lambda i: (l, 0, 0)), pl.BlockSpec((tm, D_MODEL), row),
                  pl.BlockSpec((1, D_MODEL), fixed), pl.BlockSpec((tm, D_MODEL), row)],
        out_specs=[pl.BlockSpec((tm, D_MODEL), row), pl.BlockSpec((tm, D_MODEL), row),
                   pl.BlockSpec((8, LANES), fixed)],
        out_shape=[jax.ShapeDtypeStruct((T, D_MODEL), f32)] * 2 + [jax.ShapeDtypeStruct((8, LANES), f32)],
        compiler_params=_params("arbitrary"),
    )(ch, ca, wo, x, g, target)


def _out_proj_bwd(dxn, y, g, wo, l, ch, ca, *, tm=512, ride=None):
    T = y.shape[0]
    tm = min(tm, T)
    half = MIX_WIDTH // 2

    def body(dx_ref, y_ref, g_ref, wo_ref, ch_ref, ca_ref, dch_ref, dca_ref, dwo_ref, dg_ref):
        @pl.when(pl.program_id(0) == 0)
        def _():
            dwo_ref[...] = jnp.zeros_like(dwo_ref)
            dg_ref[...] = jnp.zeros_like(dg_ref)
        y = y_ref[...]
        dx = dx_ref[...]
        r = lax.rsqrt(jnp.mean(y * y, axis=-1, keepdims=True) + NORM_EPS)
        gy = dx * g_ref[...]
        dy = r * gy - y * (r * r * r) * jnp.mean(gy * y, axis=-1, keepdims=True)
        dg_ref[...] += jnp.sum(dx * y * r, axis=0, keepdims=True)
        dyb = dy.astype(bf16)
        dch_ref[...] = _dot(dyb, wo_ref[0:half, :], NT)
        dca_ref[...] = _dot(dyb, wo_ref[half:MIX_WIDTH, :], NT)
        dwo_ref[0:half, :] += _dot(ch_ref[...], dyb, TN)
        dwo_ref[half:MIX_WIDTH, :] += _dot(ca_ref[...], dyb, TN)

    row = lambda i: (i, 0)
    fixed = lambda i: (0, 0)
    return _call(
        body, (dxn, y, g, wo, ch, ca), name="out_proj_bwd", grid=(T // tm,),
        in_specs=[pl.BlockSpec((tm, D_MODEL), row), pl.BlockSpec((tm, D_MODEL), row),
                  pl.BlockSpec((1, D_MODEL), fixed), pl.BlockSpec((None, MIX_WIDTH, D_MODEL), lambda i: (l, 0, 0)),
                  pl.BlockSpec((tm, half), row), pl.BlockSpec((tm, half), row)],
        out_specs=[pl.BlockSpec((tm, half), row), pl.BlockSpec((tm, half), row),
                   pl.BlockSpec((MIX_WIDTH, D_MODEL), fixed), pl.BlockSpec((1, D_MODEL), fixed)],
        out_shape=[jax.ShapeDtypeStruct((T, half), f32), jax.ShapeDtypeStruct((T, half), f32),
                   jax.ShapeDtypeStruct((MIX_WIDTH, D_MODEL), f32), jax.ShapeDtypeStruct((1, D_MODEL), f32)],
        semantics=("arbitrary",), ride=ride)


TILE = 256
PIECE_TILES = (4, 4, 4, 4, 4, 1, 4)
PIECE_START = tuple(sum(PIECE_TILES[:p]) for p in range(len(PIECE_TILES)))
N_TILES = sum(PIECE_TILES)


def _piece_specs(rows, index):
    def spec(s, n):
        def index_map(*g):
            r, t = index(*g)
            return r, jnp.clip(t - s, 0, n - 1)
        return pl.BlockSpec((rows, TILE), index_map)
    return [spec(s, n) for s, n in zip(PIECE_START, PIECE_TILES)]


def _for_piece(t, fn):
    for p, (s, n) in enumerate(zip(PIECE_START, PIECE_TILES)):
        pl.when((t >= s) & (t < s + n))(functools.partial(fn, p))


def _in_proj_bwd(pieces, wt, l, x, g, dxn, *, tm=512, blocks=None, dx_into=None, ride=None):
    T = x.shape[0]
    tm = min(tm, T)
    first, count = blocks or (0, T // tm)
    npc = len(pieces)
    starts = [sum(p.shape[1] for p in pieces[:i]) for i in range(npc)]
    extra = [] if dx_into is None else [dx_into]

    def body(*refs):
        dp_refs = refs[:npc]
        w_ref, x_ref, g_ref, dxn_ref = refs[npc:npc + 4]
        dx_ref, dg_ref = refs[npc + 4 + len(extra):]

        @pl.when(pl.program_id(0) == 0)
        def _():
            dg_ref[...] = jnp.zeros_like(dg_ref)
        dh = None
        for p in range(npc):
            term = _dot(dp_refs[p][...], w_ref[starts[p]:starts[p] + pieces[p].shape[1], :])
            dh = term if dh is None else dh + term
        xv = x_ref[...]
        r = lax.rsqrt(jnp.mean(xv * xv, axis=-1, keepdims=True) + NORM_EPS)
        gy = dh * g_ref[...]
        dx_ref[...] = dxn_ref[...] + r * gy - xv * (r * r * r) * jnp.mean(gy * xv, axis=-1, keepdims=True)
        dg_ref[...] += jnp.sum(dh * xv * r, axis=0, keepdims=True)

    rows = lambda i: (first + i, 0)
    return _call(
        body, (*pieces, wt, x, g, dxn, *extra), name="in_proj_bwd", grid=(count,),
        in_specs=[pl.BlockSpec((tm, p.shape[1]), rows) for p in pieces] + [
            pl.BlockSpec((None, IN_WIDTH, D_MODEL), lambda i: (l, 0, 0), pipeline_mode=pl.Buffered(1)),
            pl.BlockSpec((tm, D_MODEL), rows), pl.BlockSpec((1, D_MODEL), lambda i: (0, 0)),
            pl.BlockSpec((tm, D_MODEL), rows)] + [ANY] * len(extra),
        out_specs=[pl.BlockSpec((tm, D_MODEL), rows), pl.BlockSpec((1, D_MODEL), lambda i: (0, 0))],
        out_shape=[jax.ShapeDtypeStruct((T, D_MODEL), f32), jax.ShapeDtypeStruct((1, D_MODEL), f32)],
        semantics=("arbitrary",), ride=ride, aliases={npc + 4: 0} if extra else None)


def _grad_w_in(h, pieces, *, ride=None):
    T = h.shape[0]
    npc = len(pieces)

    def body(*refs):
        h_ref, dp_refs, o_ref = refs[0], refs[1:1 + npc], refs[1 + npc]

        def put(p):
            o_ref[...] = _dot(dp_refs[p][...], h_ref[...], TN)
        _for_piece(pl.program_id(0), put)

    return _call(
        body, (h, *pieces), name="grad_w_in", grid=(N_TILES,),
        in_specs=[pl.BlockSpec((T, D_MODEL), lambda j: (0, 0), pipeline_mode=pl.Buffered(1))]
        + _piece_specs(T, lambda j: (0, j)),
        out_specs=[pl.BlockSpec((TILE, D_MODEL), lambda j: (j, 0))],
        out_shape=[jax.ShapeDtypeStruct((IN_WIDTH, D_MODEL), f32)],
        semantics=("parallel",), ride=ride)


def _lower_bound(lbp, layer):
    m = jnp.max(lbp, axis=0, keepdims=True)
    e = jnp.exp(lbp - m)
    p = e / jnp.sum(e, axis=0, keepdims=True)
    acc = p[0:1]
    for i in range(1, layer + 1):
        acc = acc + p[i:i + 1]
    return acc - p[0:1]


def _gate_parts(qr, fr, lb, lbf):
    sq = _sigmoid(qr)
    e = jnp.exp(-jnp.abs(fr))
    inv = 1.0 / (1.0 + e)
    pos = fr >= 0
    sg = jnp.where(pos, inv, e * inv)
    nsg = jnp.where(pos, e * inv, inv)
    fg = lbf + (1.0 - lb) * sg
    return qr * sq, sq, sg, nsg, fg, jnp.log(fg), (1.0 - lb) * nsg


LEVELS = tuple(SUB << j for j in range((CHUNK // SUB).bit_length() - 1))


def _level_masks(transposed=False):
    t = lax.broadcasted_iota(jnp.int32, (CHUNK, CHUNK), 1 if transposed else 0)
    s = lax.broadcasted_iota(jnp.int32, (CHUNK, CHUNK), 0 if transposed else 1)
    return [(t % (2 * m) >= m) & (s % (2 * m) < m) & (t // (2 * m) == s // (2 * m)) for m in LEVELS]


def _level_anchor(b_s, row, m):
    beta = b_s[m - 1:m, :]
    for g in range(1, CHUNK // (2 * m)):
        beta = jnp.where(row >= g * 2 * m, b_s[g * 2 * m + m - 1:g * 2 * m + m, :], beta)
    return beta


FWD_INTERLEAVE = 16
BWD_INTERLEAVE = 8
SWA_INTERLEAVE = 4


def _interleaved(chunks, width):
    for g0 in range(0, len(chunks), width):
        live = chunks[g0:g0 + width]
        while live:
            for gen in list(live):
                try:
                    next(gen)
                except StopIteration:
                    live.remove(gen)


def _seg_sum(seg, x):
    hi = x.astype(bf16)
    return _dot(seg, hi) + _dot(seg, (x - hi.astype(f32)).astype(bf16))


def _hgrn_fwd(proj, lb_param, g_head, *, B, S, layer, ride=None):
    T = B * S
    TB = min(1024, S)
    nT, NC = S // TB, TB // CHUNK
    nC = S // CHUNK
    HD = HG_HEAD_DIM

    def body(q_ref, f_ref, i_ref, z_ref, lb_ref, gh_ref, cat_ref, op_ref, st_ref,
             s_scr, b_scr, k_scr):
        @pl.when(pl.program_id(2) == 0)
        def _():
            s_scr[...] = jnp.zeros_like(s_scr)
        lb = _lower_bound(lb_ref[...], layer)
        lbf = jnp.maximum(lb, LB_FLOOR)
        gh = gh_ref[...]
        r_i = lax.broadcasted_iota(jnp.int32, (CHUNK, CHUNK), 0)
        c_i = lax.broadcasted_iota(jnp.int32, (CHUNK, CHUNK), 1)
        tril = (r_i >= c_i).astype(bf16)
        rows8 = lax.broadcasted_iota(jnp.int32, (8, HD), 0)
        row_c = lax.broadcasted_iota(jnp.int32, (CHUNK, HD), 0)
        lane_c = lax.broadcasted_iota(jnp.int32, (8, CHUNK), 1)
        masks = _level_masks()

        def chunk(c, carried):
            rs = slice(c * CHUNK, (c + 1) * CHUNK)
            b_s, k_s = b_scr.at[c], k_scr.at[c]
            q, _, _, _, _, logf, k = _gate_parts(q_ref[rs, :], f_ref[rs, :], lb, lbf)
            v = i_ref[rs, :]
            b = _seg_sum(tril, logf)
            b_s[...] = b
            k_s[...] = k
            yield
            pieces = []
            for blk in range(CHUNK // SUB):
                r0 = blk * SUB
                bp = [b[r0 + 8 * i:r0 + 8 * i + 8] for i in range(SUB // 8)]
                qp = [q[r0 + 8 * i:r0 + 8 * i + 8] for i in range(SUB // 8)]
                ap = [jnp.zeros((8, CHUNK), f32) for _ in range(SUB // 8)]
                for s in range(SUB):
                    bs = b_s[r0 + s:r0 + s + 1, :]
                    ks = k_s[r0 + s:r0 + s + 1, :]
                    for i in range(s // 8, SUB // 8):
                        diff = bp[i] - bs
                        if i == s // 8:
                            diff = jnp.where(rows8 >= s - 8 * i, diff, NEG_INF)
                        col = jnp.sum(jnp.exp(diff) * qp[i] * ks, axis=1, keepdims=True)
                        ap[i] = jnp.where(lane_c == r0 + s, col, ap[i])
                pieces += ap
                yield
            a_all = jnp.concatenate(pieces, axis=0)
            for m, mk in zip(LEVELS, masks):
                beta = _level_anchor(b_s, row_c, m)
                qh = (q * jnp.exp(jnp.minimum(b - beta, 0.0))).astype(bf16)
                kh = (k * jnp.exp(jnp.minimum(beta - b, 0.0))).astype(bf16)
                a_all = a_all + jnp.where(mk, _dot(qh, kh, NT), 0.0)
            yield
            st = carried[0]
            st_ref[0, 0, c] = st
            vb16 = v.astype(bf16)
            o = _dot(a_all.astype(bf16), vb16) + _dot((q * jnp.exp(b)).astype(bf16), st.astype(bf16), NT)
            b_end = b_s[CHUNK - 1:CHUNK, :]
            kdec = (k * jnp.exp(b_end - b)).astype(bf16)
            carried[0] = jnp.exp(b_end) * st + _dot(vb16, kdec, TN)
            yield
            rr = lax.rsqrt(jnp.mean(o * o, axis=-1, keepdims=True) + NORM_EPS)
            zr = z_ref[rs, :]
            cat_ref[rs, :] = (o * rr * gh * (zr * _sigmoid(zr))).astype(bf16)
            op_ref[rs, :] = o

        carried = [s_scr[...]]
        _interleaved([chunk(c, carried) for c in range(NC)], FWD_INTERLEAVE)
        s_scr[...] = carried[0]

    def col(part):
        return pl.BlockSpec((TB, HD), lambda b, h, n: (b * nT + n, part * HG_HEADS + h))

    out_col = pl.BlockSpec((TB, HD), lambda b, h, n: (b * nT + n, h))
    return _call(
        body, (proj, proj, proj, proj, lb_param, g_head),
        name=f"hgrn_fwd_l{layer}", grid=(B, HG_HEADS, nT),
        in_specs=[col(0), col(1), col(2), col(3),
                  pl.BlockSpec((DEPTH, HD), lambda b, h, n: (0, h)),
                  pl.BlockSpec((1, HD), lambda b, h, n: (0, 0))],
        out_specs=[out_col, out_col,
                   pl.BlockSpec((1, 1, NC, HD, HD), lambda b, h, n: (b, h, n, 0, 0))],
        out_shape=[jax.ShapeDtypeStruct((T, HG_WIDTH), bf16), jax.ShapeDtypeStruct((T, HG_WIDTH), f32),
                   jax.ShapeDtypeStruct((B, HG_HEADS, nC, HD, HD), f32)],
        scratch_shapes=[pltpu.VMEM((HD, HD), f32), pltpu.VMEM((NC, CHUNK, HD), f32), pltpu.VMEM((NC, CHUNK, HD), f32)],
        semantics=("parallel", "parallel", "arbitrary"), ride=ride)


def _hgrn_bwd(proj, lb_param, g_head, o_pre, states, dcat, *, B, S, layer, ride=None):
    T = B * S
    TB = min(1024, S)
    nT, NC = S // TB, TB // CHUNK
    HD = HG_HEAD_DIM

    def body(q_ref, f_ref, i_ref, z_ref, lb_ref, gh_ref, op_ref, st_ref, dc_ref,
             dq_ref, df_ref, di_ref, dz_ref, dlb_ref, dgh_ref,
             ds_scr, b_scr, q_scr, do_scr, wk_scr):
        @pl.when(pl.program_id(2) == 0)
        def _():
            ds_scr[...] = jnp.zeros_like(ds_scr)
            dlb_ref[...] = jnp.zeros_like(dlb_ref)
            dgh_ref[...] = jnp.zeros_like(dgh_ref)
        lb = _lower_bound(lb_ref[...], layer)
        lbf = jnp.maximum(lb, LB_FLOOR)
        ind = (lb > LB_FLOOR).astype(f32)
        gh = gh_ref[...]
        r_i = lax.broadcasted_iota(jnp.int32, (CHUNK, CHUNK), 0)
        c_i = lax.broadcasted_iota(jnp.int32, (CHUNK, CHUNK), 1)
        tril = (r_i >= c_i).astype(bf16)
        triu = (c_i >= r_i).astype(bf16)
        rows8 = lax.broadcasted_iota(jnp.int32, (8, HD), 0)
        row_c = lax.broadcasted_iota(jnp.int32, (CHUNK, HD), 0)
        lane_c = lax.broadcasted_iota(jnp.int32, (8, CHUNK), 1)
        last_row = row_c == CHUNK - 1
        masks = _level_masks()
        masks_t = _level_masks(transposed=True)
        seg_t = lax.broadcasted_iota(jnp.int32, (SUB, 8 * SUB), 0)
        seg_r = lax.broadcasted_iota(jnp.int32, (SUB, 8 * SUB), 1) // 8
        seg0 = (seg_r == seg_t).astype(bf16)
        seg1 = (seg_r[:, 0:4 * SUB] + 8 == seg_t[:, 0:4 * SUB]).astype(bf16)

        def chunk(c, carried):
            rs = slice(c * CHUNK, (c + 1) * CHUNK)
            b_s, q_s, do_s = b_scr.at[c], q_scr.at[c], do_scr.at[c]
            qr, fr = q_ref[rs, :], f_ref[rs, :]
            q, sq, sg, nsg, fg, logf, k = _gate_parts(qr, fr, lb, lbf)
            v = i_ref[rs, :]
            b = _seg_sum(tril, logf)
            o = op_ref[rs, :]
            dc = dc_ref[rs, :]
            zr = z_ref[rs, :]
            sz = _sigmoid(zr)
            rr = lax.rsqrt(jnp.mean(o * o, axis=-1, keepdims=True) + NORM_EPS)
            dz_ref[rs, :] = (dc * (o * rr * gh) * (sz * (1.0 + zr * (1.0 - sz)))).astype(bf16)
            dn = dc * (zr * sz)
            dgh_ref[0, 0] += jnp.sum(dn * o * rr, axis=0, keepdims=True)
            gdn = dn * gh
            d_o = rr * gdn - o * (rr * rr * rr) * jnp.mean(gdn * o, axis=-1, keepdims=True)
            b_s[...] = b
            q_s[...] = q
            do_s[...] = d_o
            dob = d_o.astype(bf16)
            vb16 = v.astype(bf16)
            d_a = _dot(dob, vb16, NT)
            yield
            d_q = jnp.zeros((CHUNK, HD), f32)
            d_k = jnp.zeros((CHUNK, HD), f32)
            at_all = jnp.zeros((CHUNK, CHUNK), f32)
            for m, mk, mkt in zip(LEVELS, masks, masks_t):
                beta = _level_anchor(b_s, row_c, m)
                eq = jnp.exp(jnp.minimum(b - beta, 0.0))
                ek = jnp.exp(jnp.minimum(beta - b, 0.0))
                qh = (q * eq).astype(bf16)
                kh = (k * ek).astype(bf16)
                at_all = at_all + jnp.where(mkt, _dot(kh, qh, NT), 0.0)
                d_aa = jnp.where(mk, d_a, 0.0).astype(bf16)
                d_q = d_q + _dot(d_aa, kh) * eq
                d_k = d_k + _dot(d_aa, qh, TN) * ek
            yield
            dq_blocks, dk_pieces, at_pieces = [], [], []
            for blk in range(CHUNK // SUB):
                r0 = blk * SUB
                wk = wk_scr.at[c * (CHUNK // SUB) + blk]
                bp = [b[r0 + 8 * i:r0 + 8 * i + 8] for i in range(SUB // 8)]
                kp = [k[r0 + 8 * i:r0 + 8 * i + 8] for i in range(SUB // 8)]
                vp = [v[r0 + 8 * i:r0 + 8 * i + 8] for i in range(SUB // 8)]
                dkp = [jnp.zeros((8, HD), f32) for _ in range(SUB // 8)]
                atp = [jnp.zeros((8, CHUNK), f32) for _ in range(SUB // 8)]
                for t in range(SUB):
                    bt = b_s[r0 + t:r0 + t + 1, :]
                    qt = q_s[r0 + t:r0 + t + 1, :]
                    dot_ = do_s[r0 + t:r0 + t + 1, :]
                    for i in range(t // 8 + 1):
                        diff = bt - bp[i]
                        if i == t // 8:
                            diff = jnp.where(rows8 <= t - 8 * i, diff, NEG_INF)
                        e = jnp.exp(diff)
                        a = jnp.sum(e * kp[i] * qt, axis=1, keepdims=True)
                        atp[i] = jnp.where(lane_c == r0 + t, a, atp[i])
                        w = jnp.sum(vp[i] * dot_, axis=1, keepdims=True) * e
                        dkp[i] = dkp[i] + w * qt
                        row = 8 * t if i == 0 else 8 * SUB + 8 * (t - 8)
                        wk[row:row + 8, :] = w * kp[i]
                dq_blk = _seg_sum(seg0, wk[0:8 * SUB, :])
                if SUB > 8:
                    dq_blk = dq_blk + _seg_sum(seg1, wk[8 * SUB:12 * SUB, :])
                dq_blocks.append(dq_blk)
                dk_pieces += dkp
                at_pieces += atp
                yield
            dst1 = carried[0]
            st0 = st_ref[0, 0, c]
            dst1b = dst1.astype(bf16)
            eb = jnp.exp(b)
            b_end = b_s[CHUNK - 1:CHUNK, :]
            edec = jnp.exp(b_end - b)
            e_end = jnp.exp(b_end)
            kdec = (k * edec).astype(bf16)
            qdec = (q * eb).astype(bf16)
            st1 = e_end * st0 + _dot(vb16, kdec, TN)
            rterm = jnp.sum(dst1 * st1, axis=0, keepdims=True)
            carried[0] = e_end * dst1 + _dot(dob, qdec, TN)
            d_q = d_q + _dot(dob, st0.astype(bf16)) * eb + jnp.concatenate(dq_blocks, axis=0)
            d_k = d_k + _dot(vb16, dst1b) * edec + jnp.concatenate(dk_pieces, axis=0)
            d_v = _dot(kdec, dst1b, NT) + _dot((at_all + jnp.concatenate(at_pieces, axis=0)).astype(bf16), dob)
            yield
            db = q * d_q - k * d_k + jnp.where(last_row, rterm, 0.0)
            dlt = _seg_sum(triu, db) - fg * d_k
            df_ref[rs, :] = (dlt * (1.0 - lb) * sg * nsg / fg).astype(bf16)
            dlb_ref[0] += jnp.sum(dlt * (ind - sg) / fg, axis=0, keepdims=True)
            dq_ref[rs, :] = (d_q * (sq * (1.0 + qr * (1.0 - sq)))).astype(bf16)
            di_ref[rs, :] = d_v.astype(bf16)

        carried = [ds_scr[...]]
        _interleaved([chunk(c, carried) for c in reversed(range(NC))], BWD_INTERLEAVE)
        ds_scr[...] = carried[0]

    def col(part):
        return pl.BlockSpec((TB, HD), lambda b, h, n: (b * nT + nT - 1 - n, part * HG_HEADS + h))

    hcol = pl.BlockSpec((TB, HD), lambda b, h, n: (b * nT + nT - 1 - n, h))
    return _call(
        body, (proj, proj, proj, proj, lb_param, g_head, o_pre, states, dcat),
        name=f"hgrn_bwd_l{layer}", grid=(B, HG_HEADS, nT),
        in_specs=[col(0), col(1), col(2), col(3),
                  pl.BlockSpec((DEPTH, HD), lambda b, h, n: (0, h)),
                  pl.BlockSpec((1, HD), lambda b, h, n: (0, 0)),
                  hcol,
                  pl.BlockSpec((1, 1, NC, HD, HD), lambda b, h, n: (b, h, nT - 1 - n, 0, 0)),
                  hcol],
        out_specs=[hcol, hcol, hcol, hcol,
                   pl.BlockSpec((1, 1, HD), lambda b, h, n: (b, 0, h)),
                   pl.BlockSpec((1, 1, 1, HD), lambda b, h, n: (b, h, 0, 0))],
        out_shape=[jax.ShapeDtypeStruct((T, HG_WIDTH), bf16)] * 4 + [
            jax.ShapeDtypeStruct((B, 1, HG_WIDTH), f32), jax.ShapeDtypeStruct((B, HG_HEADS, 1, HD), f32)],
        scratch_shapes=[pltpu.VMEM((HD, HD), f32)] + [pltpu.VMEM((NC, CHUNK, HD), f32)] * 3
        + [pltpu.VMEM((NC * CHUNK // SUB, 12 * SUB, HD), f32)],
        semantics=("parallel", "parallel", "arbitrary"), ride=ride)


def _rope_tables(S):
    half = ATT_HEAD_DIM // 2
    inv_freq = ROPE_THETA ** (-jnp.arange(half, dtype=f32) / half)
    ang = jnp.arange(S, dtype=f32)[:, None] * inv_freq[None, :]
    cos, sin = jnp.cos(ang), jnp.sin(ang)
    return jnp.tile(jnp.concatenate([cos, cos], axis=1), (1, 2)), jnp.tile(jnp.concatenate([-sin, sin], axis=1), (1, 2))


def _swap_halves(x, first_half):
    return jnp.where(first_half, pltpu.roll(x, LANES - ATT_HEAD_DIM // 2, 1), pltpu.roll(x, ATT_HEAD_DIM // 2, 1))


def _rope(x, cos, sin, first_half):
    return x * cos + _swap_halves(x, first_half) * sin


def _rope_bwd(dy, cos, sin, first_half):
    return dy * cos + _swap_halves(dy * sin, first_half)


def _attn_consts(n):
    lane = lax.broadcasted_iota(jnp.int32, (1, LANES), 1)
    low = lane < ATT_HEAD_DIM
    first_half = (lane % ATT_HEAD_DIM) < ATT_HEAD_DIM // 2
    top = lax.broadcasted_iota(jnp.int32, (LANES, 1), 0) < ATT_HEAD_DIM
    s = lax.broadcasted_iota(jnp.int32, (2 * ATT_BLOCK, ATT_BLOCK), 0)
    t = lax.broadcasted_iota(jnp.int32, (2 * ATT_BLOCK, ATT_BLOCK), 1)
    mask = (s > t) & (s <= t + ATT_BLOCK) & ((s >= ATT_BLOCK) | (n > 0))
    return low, first_half, top, mask


def _dup_kv(x, low):
    rolled = pltpu.roll(x, ATT_HEAD_DIM, 1)
    return [jnp.where(low, x, rolled), jnp.where(low, rolled, x)]


def _attn_head(qtm, kd, vdt, sink, mask):
    s = jnp.where(mask, _dot(kd, qtm) * ATT_SCALE, NEG_INF)
    m = jnp.maximum(jnp.max(s, axis=0, keepdims=True), sink)
    p = jnp.exp(s - m)
    psink = jnp.exp(sink - m)
    inv = 1.0 / (jnp.sum(p, axis=0, keepdims=True) + psink)
    pn = p * inv
    return pn, psink * inv, _dot(vdt, pn.astype(bf16))


def _swa_fwd(proj, sink_b, cos, sin, *, B, S, ride=None):
    T = B * S
    L = ATT_BLOCK
    nB = S // L

    def body(q_ref, z_ref, kvc_ref, kvp_ref, sk_ref, cc_ref, sc_ref, cp_ref, sp_ref, cat_ref):
        n = pl.program_id(1)
        low, first_half, top, mask = _attn_consts(n)
        cc, sc = cc_ref[...], sc_ref[...]
        kc = _rope(kvc_ref[:, 0:LANES], cc, sc, first_half)
        kp = _rope(kvp_ref[:, 0:LANES], cp_ref[...], sp_ref[...], first_half)
        kd = [x.astype(bf16) for x in _dup_kv(jnp.concatenate([kp, kc], axis=0), low)]
        vdt = [x.T.astype(bf16) for x in _dup_kv(jnp.concatenate([kvp_ref[:, LANES:2 * LANES], kvc_ref[:, LANES:2 * LANES]], axis=0), low)]
        def head_pair(pair):
            cols = slice(pair * LANES, (pair + 1) * LANES)
            j = (2 * pair) // ATT_GROUP
            qt = _rope(q_ref[:, cols], cc, sc, first_half).T
            yield
            outs = []
            for hh in range(2):
                h = 2 * pair + hh
                qtm = jnp.where(top if hh == 0 else ~top, qt, 0.0).astype(bf16)
                outs.append(_attn_head(qtm, kd[j], vdt[j], sk_ref[h:h + 1, 0:1], mask)[2])
                yield
            zp = z_ref[:, cols]
            cat_ref[:, cols] = (jnp.where(top, outs[0], outs[1]).T * (zp * _sigmoid(zp))).astype(bf16)

        _interleaved([head_pair(p) for p in range(ATT_HEADS // 2)], SWA_INTERLEAVE)

    cur = lambda b, n: (b * nB + n, 0)
    return _call(
        body, (proj, proj, proj, proj, sink_b, cos, sin, cos, sin), name="swa_fwd", grid=(B, nB),
        in_specs=[pl.BlockSpec((L, ATT_WIDTH), lambda b, n: (b * nB + n, QA_BLK)),
                  pl.BlockSpec((L, ATT_WIDTH), lambda b, n: (b * nB + n, ZA_BLK)),
                  pl.BlockSpec((L, 2 * KV_WIDTH), lambda b, n: (b * nB + n, KV_BLK)),
                  pl.BlockSpec((L, 2 * KV_WIDTH), lambda b, n: (b * nB + jnp.maximum(n - 1, 0), KV_BLK)),
                  pl.BlockSpec((ATT_HEADS, LANES), lambda b, n: (0, 0)),
                  pl.BlockSpec((L, LANES), lambda b, n: (n, 0)), pl.BlockSpec((L, LANES), lambda b, n: (n, 0)),
                  pl.BlockSpec((L, LANES), lambda b, n: (jnp.maximum(n - 1, 0), 0)),
                  pl.BlockSpec((L, LANES), lambda b, n: (jnp.maximum(n - 1, 0), 0))],
        out_specs=[pl.BlockSpec((L, ATT_WIDTH), cur)],
        out_shape=[jax.ShapeDtypeStruct((T, ATT_WIDTH), bf16)],
        semantics=("parallel", "parallel"), ride=ride)


def _swa_bwd(proj, sink_b, cos, sin, dcat, *, B, S, ride=None):
    T = B * S
    L = ATT_BLOCK
    nB = S // L

    def body(q_ref, z_ref, kvc_ref, kvp_ref, sk_ref, cc_ref, sc_ref, cp_ref, sp_ref, dc_ref,
             dq_ref, dz_ref, dkv_ref, dsk_ref, carry, ds_st, pn_st, q_st, do_st):
        step = pl.program_id(1)
        n = nB - 1 - step

        @pl.when((pl.program_id(0) == 0) & (step == 0))
        def _():
            dsk_ref[...] = jnp.zeros_like(dsk_ref)

        @pl.when(step == 0)
        def _():
            carry[...] = jnp.zeros_like(carry)
        low, first_half, top, mask = _attn_consts(n)
        cc, sc, cp, sp = cc_ref[...], sc_ref[...], cp_ref[...], sp_ref[...]
        kc = _rope(kvc_ref[:, 0:LANES], cc, sc, first_half)
        kp = _rope(kvp_ref[:, 0:LANES], cp, sp, first_half)
        kdf = _dup_kv(jnp.concatenate([kp, kc], axis=0), low)
        vdf = _dup_kv(jnp.concatenate([kvp_ref[:, LANES:2 * LANES], kvc_ref[:, LANES:2 * LANES]], axis=0), low)
        kd = [x.astype(bf16) for x in kdf]
        vd = [x.astype(bf16) for x in vdf]
        kdt = [x.T.astype(bf16) for x in kdf]
        vdt = [x.T.astype(bf16) for x in vdf]
        dkd, dvd = [], []
        def head_pair(pair):
            cols = slice(pair * LANES, (pair + 1) * LANES)
            j = (2 * pair) // ATT_GROUP
            qp = _rope(q_ref[:, cols], cc, sc, first_half)
            qt = qp.T
            zp = z_ref[:, cols]
            dc = dc_ref[:, cols]
            sz = _sigmoid(zp)
            d_o = dc * (zp * sz)
            dot_ = d_o.T
            yield
            res = []
            for hh in range(2):
                rsel = top if hh == 0 else ~top
                qtm = jnp.where(rsel, qt, 0.0).astype(bf16)
                pn, psn, o = _attn_head(qtm, kd[j], vdt[j], sk_ref[2 * pair + hh:2 * pair + hh + 1, 0:1], mask)
                res.append((rsel, pn, psn, o))
                yield
            ot = jnp.where(top, res[0][3], res[1][3])
            dz_ref[:, cols] = (dc * ot.T * (sz * (1.0 + zp * (1.0 - sz)))).astype(bf16)
            dqts = []
            for hh in range(2):
                h = 2 * pair + hh
                rsel, pn, psn, _ = res[hh]
                lsel = low if hh == 0 else ~low
                dotm = jnp.where(rsel, dot_, 0.0)
                delta = jnp.sum(dotm * ot, axis=0, keepdims=True)
                dst = (pn * (_dot(vd[j], dotm.astype(bf16)) - delta) * ATT_SCALE).astype(bf16)
                dsk_ref[h:h + 1, :] += jnp.zeros((1, LANES), f32) - jnp.sum(psn * delta)
                dqts.append(_dot(kdt[j], dst))
                g = h % ATT_GROUP
                ds_st[:, g * LANES:(g + 1) * LANES] = dst
                pn_st[:, g * LANES:(g + 1) * LANES] = pn.astype(bf16)
                q_st[g * LANES:(g + 1) * LANES, :] = jnp.where(lsel, qp, 0.0).astype(bf16)
                do_st[g * LANES:(g + 1) * LANES, :] = jnp.where(lsel, d_o, 0.0).astype(bf16)
                yield
            dq_ref[:, cols] = _rope_bwd(jnp.where(top, dqts[0], dqts[1]).T, cc, sc, first_half).astype(bf16)

        pairs_per_group = ATT_GROUP // 2
        for grp in range(ATT_HEADS // ATT_GROUP):
            _interleaved([head_pair(grp * pairs_per_group + p) for p in range(pairs_per_group)], SWA_INTERLEAVE)
            dkd.append(_dot(ds_st[...], q_st[...]))
            dvd.append(_dot(pn_st[...], do_st[...]))
        dk = [x + pltpu.roll(x, ATT_HEAD_DIM, 1) for x in dkd]
        dv = [x + pltpu.roll(x, ATT_HEAD_DIM, 1) for x in dvd]
        dk = jnp.where(low, dk[0], dk[1])
        dv = jnp.where(low, dv[0], dv[1])
        dkv_ref[:, 0:LANES] = (_rope_bwd(dk[L:2 * L], cc, sc, first_half) + carry[:, 0:LANES]).astype(bf16)
        dkv_ref[:, LANES:2 * LANES] = (dv[L:2 * L] + carry[:, LANES:2 * LANES]).astype(bf16)
        carry[:, 0:LANES] = _rope_bwd(dk[0:L], cp, sp, first_half)
        carry[:, LANES:2 * LANES] = dv[0:L]

    rev = lambda b, s: b * nB + nB - 1 - s
    revp = lambda b, s: b * nB + jnp.maximum(nB - 2 - s, 0)
    wide = lambda blk: pl.BlockSpec((L, ATT_WIDTH), lambda b, s: (rev(b, s), blk))
    tab = pl.BlockSpec((L, LANES), lambda b, s: (nB - 1 - s, 0))
    tabp = pl.BlockSpec((L, LANES), lambda b, s: (jnp.maximum(nB - 2 - s, 0), 0))
    return _call(
        body, (proj, proj, proj, proj, sink_b, cos, sin, cos, sin, dcat), name="swa_bwd", grid=(B, nB),
        in_specs=[wide(QA_BLK), wide(ZA_BLK),
                  pl.BlockSpec((L, 2 * KV_WIDTH), lambda b, s: (rev(b, s), KV_BLK)),
                  pl.BlockSpec((L, 2 * KV_WIDTH), lambda b, s: (revp(b, s), KV_BLK)),
                  pl.BlockSpec((ATT_HEADS, LANES), lambda b, s: (0, 0)),
                  tab, tab, tabp, tabp, wide(0)],
        out_specs=[wide(0), wide(0), pl.BlockSpec((L, 2 * KV_WIDTH), lambda b, s: (rev(b, s), 0)),
                   pl.BlockSpec((ATT_HEADS, LANES), lambda b, s: (0, 0))],
        out_shape=[jax.ShapeDtypeStruct((T, ATT_WIDTH), bf16), jax.ShapeDtypeStruct((T, ATT_WIDTH), bf16),
                   jax.ShapeDtypeStruct((T, 2 * KV_WIDTH), bf16), jax.ShapeDtypeStruct((ATT_HEADS, LANES), f32)],
        scratch_shapes=[pltpu.VMEM((L, 2 * KV_WIDTH), f32),
                        pltpu.VMEM((2 * L, ATT_GROUP * LANES), bf16), pltpu.VMEM((2 * L, ATT_GROUP * LANES), bf16),
                        pltpu.VMEM((ATT_GROUP * LANES, LANES), bf16), pltpu.VMEM((ATT_GROUP * LANES, LANES), bf16)],
        semantics=("arbitrary", "arbitrary"), ride=ride)


def _train_step(x, target, bufs, g_pre, g_post, lb_param, g_head, sinks, *, B, S, exchange):
    L = DEPTH
    T = x.shape[0]
    ri, ro = IN_WIDTH // 8, MIX_WIDTH // 8
    cos, sin = _rope_tables(S)
    full = [list(b) for b in bufs]
    if exchange:
        full[0][0] = _run_exchange(_gather_d2d(_run_exchange(_gather_ici(bufs[0][:1]))))[0]
    saved = []
    for l in range(L):
        wt = full[l][0].reshape(1, IN_WIDTH, D_MODEL)
        tail = jnp.concatenate([wt[:, 5376:6400], wt[:, 5120:5376]], axis=1)
        first = exchange and l == 0
        ahead = exchange and l + 1 < L
        (proj, h), wo_landed = _in_proj(x, g_pre[l:l + 1], wt, tail, 0,
                                       ride=_gather_ici(bufs[0][1:], "gather_ici_wo") if first else None)
        (ch, o_pre, states), landed = _hgrn_fwd(
            proj, lb_param, g_head[l:l + 1], B=B, S=S, layer=l,
            ride=_merge(_gather_ici(bufs[l + 1]) if ahead else None,
                        _gather_d2d(wo_landed, "gather_d2d_wo") if first else None))
        if first:
            full[0][1] = landed[-1]
            landed = landed[:-1]
        wo = full[l][1].reshape(1, MIX_WIDTH, D_MODEL)
        sink_b = jnp.broadcast_to(sinks[l][:, None], (ATT_HEADS, LANES))
        (ca,), passed = _swa_fwd(proj, sink_b, cos, sin, B=B, S=S, ride=_gather_d2d(landed) if ahead else None)
        if ahead:
            full[l + 1] = list(passed)
        if l + 1 < L:
            xn, y = _out_proj(ch, ca, wo, 0, x, g_post[l:l + 1])
        else:
            dx, y, loss = _out_proj_loss(ch, ca, wo, 0, x, g_post[l:l + 1], target)
        saved.append((x, proj, h, ch, o_pre, states, sink_b, ca, y, wt, tail, wo))
        x = xn if l + 1 < L else None

    def reduce_tail(sums, recv):
        return _run_exchange(_pair_share([_chip_sum(s, r) for s, r in zip(sums, recv)]))

    grads = [None] * L
    waiting = None
    gg_pre, gg_post, g_lb, gg_head, g_sinks = [], [], [], [], []
    for l in reversed(range(L)):
        x_in, proj, h, ch, o_pre, states, sink_b, ca, y, wt, tail, wo = saved[l]
        (dch, dca, dwo, dgpost), got = _out_proj_bwd(dx, y, g_post[l:l + 1], wo, 0, ch, ca,
                                                     ride=_pair_exchange(waiting) if waiting else None)
        sums = [_pair_add(p, r) for p, r in zip(waiting, got)] if waiting else None
        (dq, df, di, dz, dlb, dgh), recv = _hgrn_bwd(proj, lb_param, g_head[l:l + 1], o_pre, states, dch, B=B, S=S,
                                                     layer=l, ride=_chip_exchange(sums) if waiting else None)
        at_end = exchange and l == 0
        part_o = [dwo.reshape(1, 4, 2, ro, D_MODEL)]
        halves = [_chip_sum(s, r) for s, r in zip(sums, recv)] if waiting else None
        (dqa, dza, dkv, dsk), rode = _swa_bwd(proj, sink_b, cos, sin, dca, B=B, S=S,
                                             ride=_merge(_pair_exchange(part_o) if at_end else None,
                                                         _pair_share(halves) if waiting else None))
        got_o = rode[:1]
        if waiting:
            grads[l + 1] = list(rode[-len(halves):])
        pieces = [dq, df, di, dz, dqa, dkv, dza]
        sums_o = [_pair_add(part_o[0], got_o[0])] if at_end else None
        (gwt,), recv_o = _grad_w_in(h, pieces, ride=_chip_exchange(sums_o) if at_end else None)
        part_t = [gwt.reshape(1, 4, 2, ri, D_MODEL)]
        if at_end:
            tm = min(512, T // 2)
            nb = T // tm
            na = max(1, nb // 4)
            (dx_a, dg_a), got_t = _in_proj_bwd(pieces, wt, 0, x_in, g_pre[l:l + 1], dx, tm=tm, blocks=(0, na),
                                               ride=_pair_exchange(part_t))
            sums_t = [_pair_add(part_t[0], got_t[0])]
            (dx, dg_b), recv_t = _in_proj_bwd(pieces, wt, 0, x_in, g_pre[l:l + 1], dx, tm=tm, blocks=(na, nb - na),
                                              dx_into=dx_a, ride=_chip_exchange(sums_t))
            dgpre = dg_a + dg_b
            grads[0] = reduce_tail(sums_t + sums_o, recv_t + recv_o)
        else:
            (dx, dgpre), _ = _in_proj_bwd(pieces, wt, 0, x_in, g_pre[l:l + 1], dx)
            if exchange:
                waiting = part_t + part_o
            else:
                grads[l] = [gwt, dwo]
        gg_pre.append(dgpre[0])
        gg_post.append(dgpost[0])
        g_lb.append(jnp.sum(dlb, axis=(0, 1)))
        gg_head.append(jnp.sum(dgh, axis=(0, 1, 2)))
        g_sinks.append(dsk[:, 0])
    rev = lambda xs: jnp.stack(xs[::-1])
    return loss[0, 0], dx, grads, rev(gg_pre), rev(gg_post), rev(g_lb), rev(gg_head), rev(g_sinks)


MESH = pl.DeviceIdType.MESH
ANY = pl.BlockSpec(memory_space=pl.ANY)


def _place():
    x, y, c = lax.axis_index("x"), lax.axis_index("y"), lax.axis_index("c")
    return x, y, c, [(1 - x, y), (x, 1 - y), (1 - x, 1 - y)]


def _rcopy(src, dst, send, recv, k, to):
    return pltpu.make_async_remote_copy(src_ref=src, dst_ref=dst, send_sem=send.at[k], recv_sem=recv.at[k],
                                        device_id=to, device_id_type=MESH)


class _Exchange:
    def __init__(self, name, inputs, out_shapes, n_sems, plan, in_place=False):
        self.name, self.inputs, self.out_shapes = name, list(inputs), list(out_shapes)
        self.n_sems, self.plan = n_sems, plan
        self.aliases = {a: a for a in range(len(inputs))} if in_place else {}

    def start(self, ins, outs, send, recv):
        for cp in self.plan(ins, outs, send, recv)[0]:
            cp.start()

    def finish(self, ins, outs, send, recv):
        sent, arriving = self.plan(ins, outs, send, recv)
        for cp in arriving:
            cp.wait_recv()
        for cp in sent:
            cp.wait_send()

    def sems(self):
        return [pltpu.SemaphoreType.DMA((self.n_sems,)), pltpu.SemaphoreType.DMA((self.n_sems,))]


class _SemView:
    def __init__(self, sems, offset):
        self.sems, self.offset = sems, offset

    @property
    def at(self):
        return self

    def __getitem__(self, k):
        return self.sems.at[self.offset + k]


def _both(a, b):
    ai, ao = len(a.inputs), len(a.out_shapes)

    def plan(ins, outs, send, recv):
        sa, ra = a.plan(ins[:ai], outs[:ao], send, recv)
        sb, rb = b.plan(ins[ai:], outs[ao:], _SemView(send, a.n_sems), _SemView(recv, a.n_sems))
        return sa + sb, ra + rb

    ex = _Exchange(a.name + "_" + b.name, a.inputs + b.inputs, a.out_shapes + b.out_shapes, a.n_sems + b.n_sems, plan)
    ex.aliases = {**a.aliases, **{ai + i: ao + o for i, o in b.aliases.items()}}
    return ex


def _merge(*rides):
    rides = [r for r in rides if r is not None]
    return functools.reduce(_both, rides) if rides else None


def _run_exchange(ex):
    n_in, n_out = len(ex.inputs), len(ex.out_shapes)

    def body(*refs):
        ins, outs = refs[:n_in], refs[n_in:n_in + n_out]
        send, recv = refs[n_in + n_out:]
        ex.start(ins, outs, send, recv)
        ex.finish(ins, outs, send, recv)

    return pl.pallas_call(
        body, name=ex.name, in_specs=[ANY] * n_in, out_specs=[ANY] * n_out, out_shape=ex.out_shapes,
        input_output_aliases=ex.aliases, scratch_shapes=ex.sems(),
    )(*ex.inputs)


def _call(body, operands, *, name, grid, in_specs, out_specs, out_shape, scratch_shapes=(), semantics, ride=None,
          aliases=None):
    aliases = dict(aliases or {})
    if ride is None:
        outs = pl.pallas_call(body, name=name, grid=grid, in_specs=in_specs, out_specs=out_specs, out_shape=out_shape,
                              input_output_aliases=aliases, scratch_shapes=list(scratch_shapes),
                              compiler_params=_params(*semantics))(*operands)
        return outs, []
    n_in, n_out, n_scr = len(in_specs), len(out_specs), len(scratch_shapes)
    r_in, r_out = len(ride.inputs), len(ride.out_shapes)

    def riding(*refs):
        refs = list(refs)
        ins, rins = refs[:n_in], refs[n_in:n_in + r_in]
        o0 = n_in + r_in
        outs, routs = refs[o0:o0 + n_out], refs[o0 + n_out:o0 + n_out + r_out]
        scr = refs[o0 + n_out + r_out:o0 + n_out + r_out + n_scr]
        send, recv = refs[-2:]
        ids = [pl.program_id(d) for d in range(len(grid))]
        first = functools.reduce(jnp.logical_and, [i == 0 for i in ids])
        last = functools.reduce(jnp.logical_and, [i == g - 1 for i, g in zip(ids, grid)])
        pl.when(first)(lambda: ride.start(rins, routs, send, recv))
        body(*ins, *outs, *scr)
        pl.when(last)(lambda: ride.finish(rins, routs, send, recv))

    res = pl.pallas_call(
        riding, name=name + "_" + ride.name, grid=grid,
        in_specs=list(in_specs) + [ANY] * r_in, out_specs=list(out_specs) + [ANY] * r_out,
        out_shape=list(out_shape) + list(ride.out_shapes),
        input_output_aliases={**aliases, **{n_in + a: n_out + b for a, b in ride.aliases.items()}},
        scratch_shapes=list(scratch_shapes) + ride.sems(),
        compiler_params=_params(*(["arbitrary"] * len(grid))),
    )(*operands, *ride.inputs)
    return res[:n_out], res[n_out:]


def _gather_ici(bufs, name="gather_ici"):
    n = len(bufs)

    def plan(ins, outs, send, recv):
        x, y, c, chips = _place()
        me = 2 * x + y
        sent, arriving = [], []
        for j, (px, py) in enumerate(chips):
            for a in range(n):
                mine, theirs = outs[a].at[:, me, c], outs[a].at[:, 2 * px + py, c]
                sent.append(_rcopy(mine, mine, send, recv, j * n + a, (px, py, c)))
                arriving.append(_rcopy(theirs, theirs, send, recv, j * n + a, (px, py, c)))
        return sent, arriving

    return _Exchange(name, bufs, [jax.ShapeDtypeStruct(b.shape, b.dtype) for b in bufs], 3 * n, plan, in_place=True)


def _gather_d2d(bufs, name="gather_d2d"):
    n = len(bufs)

    def plan(ins, outs, send, recv):
        x, y, c, chips = _place()
        sib = (x, y, 1 - c)
        sent, arriving = [], []
        for j, (px, py) in enumerate(chips):
            for a in range(n):
                got, theirs = outs[a].at[:, 2 * px + py, c], outs[a].at[:, 2 * px + py, 1 - c]
                sent.append(_rcopy(got, got, send, recv, j * n + a, sib))
                arriving.append(_rcopy(theirs, theirs, send, recv, j * n + a, sib))
        return sent, arriving

    return _Exchange(name, bufs, [jax.ShapeDtypeStruct(b.shape, b.dtype) for b in bufs], 3 * n, plan, in_place=True)


def _pair_exchange(parts):
    n = len(parts)

    def plan(ins, outs, send, recv):
        x, y, c, _ = _place()
        cps = [_rcopy(ins[a].at[:, :, 1 - c], outs[a], send, recv, a, (x, y, 1 - c)) for a in range(n)]
        return cps, cps

    return _Exchange("pair_exchange", parts,
                     [jax.ShapeDtypeStruct(p.shape[:2] + p.shape[3:], p.dtype) for p in parts], n, plan)


def _block_rows(r):
    return r if r <= 512 else r // 2


def _pair_add(part, got):
    L, K, _, r, C = part.shape
    rows = _block_rows(r)

    def body(c_ref, a_ref, b_ref, o_ref):
        o_ref[0, 0] = (a_ref[0, 0, 0] + b_ref[0, 0]).astype(bf16)

    blk = (1, 1, rows, C)
    return pl.pallas_call(
        body, name="pair_add",
        grid_spec=pltpu.PrefetchScalarGridSpec(
            num_scalar_prefetch=1, grid=(L, K, r // rows),
            in_specs=[pl.BlockSpec((1, 1, 1, rows, C), lambda l, k, i, c: (l, k, c[0], i, 0)),
                      pl.BlockSpec(blk, lambda l, k, i, c: (l, k, i, 0))],
            out_specs=pl.BlockSpec(blk, lambda l, k, i, c: (l, k, i, 0))),
        out_shape=jax.ShapeDtypeStruct((L, K, r, C), bf16),
        compiler_params=_params("parallel", "parallel", "parallel"),
    )(jnp.reshape(lax.axis_index("c"), (1,)).astype(jnp.int32), part, got)


def _chip_exchange(sums):
    n = len(sums)

    def plan(ins, outs, send, recv):
        x, y, c, chips = _place()
        cps = []
        for j, (px, py) in enumerate(chips):
            for a in range(n):
                cps.append(_rcopy(ins[a].at[:, 2 * px + py], outs[a].at[j], send, recv, j * n + a, (px, py, c)))
        return cps, cps

    return _Exchange("chip_exchange", sums,
                     [jax.ShapeDtypeStruct((3, s.shape[0]) + s.shape[2:], s.dtype) for s in sums], 3 * n, plan)


def _chip_sum(mine, got):
    L, K, r, C = mine.shape
    rows = _block_rows(r)

    def body(p_ref, a_ref, b_ref, o_ref):
        o_ref[0, 0] = (a_ref[0, 0].astype(f32) + b_ref[0, 0].astype(f32)) + (b_ref[1, 0].astype(f32) + b_ref[2, 0].astype(f32))

    place = jnp.stack([2 * lax.axis_index("x") + lax.axis_index("y"), lax.axis_index("c")]).astype(jnp.int32)
    return pl.pallas_call(
        body, name="chip_sum",
        grid_spec=pltpu.PrefetchScalarGridSpec(
            num_scalar_prefetch=1, grid=(L, r // rows),
            in_specs=[pl.BlockSpec((1, 1, rows, C), lambda l, i, p: (l, p[0], i, 0)),
                      pl.BlockSpec((3, 1, rows, C), lambda l, i, p: (0, l, i, 0))],
            out_specs=pl.BlockSpec((1, 1, rows, C), lambda l, i, p: (l, p[1], i, 0))),
        out_shape=jax.ShapeDtypeStruct((L, 2, r, C), f32),
        compiler_params=_params("parallel", "parallel"),
    )(place, mine, got)


def _pair_share(bufs):
    n = len(bufs)

    def plan(ins, outs, send, recv):
        x, y, c, _ = _place()
        sib = (x, y, 1 - c)
        sent = [_rcopy(outs[a].at[:, c], outs[a].at[:, c], send, recv, a, sib) for a in range(n)]
        arriving = [_rcopy(outs[a].at[:, 1 - c], outs[a].at[:, 1 - c], send, recv, a, sib) for a in range(n)]
        return sent, arriving

    return _Exchange("pair_share", bufs, [jax.ShapeDtypeStruct(b.shape, b.dtype) for b in bufs], n, plan, in_place=True)


def _all_sum_small(v):
    def body(v_ref, o_ref, buf, send, recv):
        x, y, c, _ = _place()
        me = 4 * x + 2 * y + c
        buf[me] = v_ref[...]
        cps = []
        for m in range(1, 8):
            to = (x ^ (m >> 2), y ^ ((m >> 1) & 1), c ^ (m & 1))
            cps.append(_rcopy(v_ref, buf.at[me], send, recv, m - 1, to))
        for cp in cps:
            cp.start()
        for cp in cps:
            cp.wait()
        acc = buf[0]
        for d in range(1, 8):
            acc = acc + buf[d]
        o_ref[...] = acc

    vm = pl.BlockSpec(memory_space=pltpu.VMEM)
    return pl.pallas_call(
        body, name="all_sum_small", in_specs=[vm], out_specs=vm,
        out_shape=jax.ShapeDtypeStruct(v.shape, v.dtype),
        scratch_shapes=[pltpu.VMEM((8,) + v.shape, v.dtype), pltpu.SemaphoreType.DMA((7,)), pltpu.SemaphoreType.DMA((7,))],
    )(v)


def _adamw_math(w, g, m, v):
    m = ADAM_B1 * m + (1.0 - ADAM_B1) * g
    v = ADAM_B2 * v + (1.0 - ADAM_B2) * (g * g)
    m_hat = m / (1.0 - ADAM_B1 ** ADAM_STEP)
    v_hat = v / (1.0 - ADAM_B2 ** ADAM_STEP)
    return -ADAM_LR * (m_hat / (jnp.sqrt(v_hat) + ADAM_EPS) + ADAM_WD * w), m, v


def _adamw(w, g, m, v):
    L, R, C = w.shape
    rows = R // 4

    def body(w_ref, g_ref, m_ref, v_ref, d_ref, mo_ref, vo_ref):
        d_ref[...], mo_ref[...], vo_ref[...] = _adamw_math(w_ref[...], g_ref[...], m_ref[...], v_ref[...])

    blk = pl.BlockSpec((1, rows, C), lambda l, i: (l, i, 0))
    return pl.pallas_call(
        body, name="adamw", grid=(L, R // rows), in_specs=[blk] * 4, out_specs=[blk] * 3,
        out_shape=[jax.ShapeDtypeStruct(w.shape, f32)] * 3,
        compiler_params=_params("parallel", "parallel"),
    )(w, g, m, v)


def _chip_index():
    return jnp.reshape(2 * lax.axis_index("x") + lax.axis_index("y"), (1,)).astype(jnp.int32)


def _shard_placed(w, l):
    _, R, C = w.shape
    rows = R // 4

    def body(k_ref, w_ref, o_ref):
        o_ref[0, 0] = w_ref[0].astype(bf16)

    return pl.pallas_call(
        body, name="shard_placed",
        grid_spec=pltpu.PrefetchScalarGridSpec(
            num_scalar_prefetch=1, grid=(R // rows,),
            in_specs=[pl.BlockSpec((1, rows, C), lambda i, k: (l, i, 0))],
            out_specs=pl.BlockSpec((1, 1, rows, C), lambda i, k: (0, k[0], i, 0))),
        out_shape=jax.ShapeDtypeStruct((1, 4, R, C), bf16),
        compiler_params=_params("parallel"),
    )(_chip_index(), w)


def _pack_small(g_pre, g_post, lb, g_head, sinks, loss=None):
    rows = []
    for l in range(DEPTH):
        tail = [g_head[l], sinks[l]]
        if loss is not None and l == 0:
            tail.append(jnp.reshape(loss, (1,)))
        tail = jnp.concatenate(tail)
        rows += [g_pre[l], g_post[l], lb[l], jnp.pad(tail, (0, D_MODEL - tail.shape[0]))]
    return jnp.stack(rows)


def _unpack_small(p):
    g_pre = jnp.stack([p[4 * l] for l in range(DEPTH)])
    g_post = jnp.stack([p[4 * l + 1] for l in range(DEPTH)])
    lb = jnp.stack([p[4 * l + 2] for l in range(DEPTH)])
    g_head = jnp.stack([p[4 * l + 3, :HG_HEAD_DIM] for l in range(DEPTH)])
    sinks = jnp.stack([p[4 * l + 3, HG_HEAD_DIM:HG_HEAD_DIM + ATT_HEADS] for l in range(DEPTH)])
    return g_pre, g_post, lb, g_head, sinks


def _small_update(gsum, w, m, v):
    def body(g_ref, w_ref, m_ref, v_ref, go_ref, d_ref, mo_ref, vo_ref):
        g = g_ref[...]
        w = w_ref[...]
        lbp = [w[4 * l + 2:4 * l + 3] for l in range(DEPTH)]
        mx = functools.reduce(jnp.maximum, lbp)
        e = [jnp.exp(t - mx) for t in lbp]
        tot = functools.reduce(jnp.add, e)
        p = [t / tot for t in e]
        glb = [g[4 * l + 2:4 * l + 3] for l in range(DEPTH)]
        row = lax.broadcasted_iota(jnp.int32, g.shape, 0)
        for j in range(DEPTH):
            gj = jnp.zeros_like(p[0])
            for l in range(DEPTH):
                for i in range(1, l + 1):
                    gj = gj + glb[l] * p[i] * ((1.0 if i == j else 0.0) - p[j])
            g = jnp.where(row == 4 * j + 2, gj, g)
        go_ref[...] = g
        d_ref[...], mo_ref[...], vo_ref[...] = _adamw_math(w, g, m_ref[...], v_ref[...])

    vm = pl.BlockSpec(memory_space=pltpu.VMEM)
    return pl.pallas_call(
        body, name="small_update", in_specs=[vm] * 4, out_specs=[vm] * 4,
        out_shape=[jax.ShapeDtypeStruct(gsum.shape, f32)] * 4,
    )(gsum, w, m, v)


def kernel(x, w_in, w_out, g_pre, g_post, lb_param, g_head, sinks, loss_target, m_w_in, m_w_out, m_g_pre, m_g_post, m_lb_param, m_g_head, m_sinks, v_w_in, v_w_out, v_g_pre, v_g_post, v_lb_param, v_g_head, v_sinks):
    B, S, _ = x.shape
    T = B * S
    L = DEPTH
    ri, ro = IN_WIDTH // 8, MIX_WIDTH // 8
    tr = lambda a: jnp.transpose(a, (0, 2, 1))
    wt, mt, vt = tr(w_in), tr(m_w_in), tr(v_w_in)
    bufs = [[_shard_placed(wt, l).reshape(1, 4, 2, ri, D_MODEL), _shard_placed(w_out, l).reshape(1, 4, 2, ro, D_MODEL)]
            for l in range(L)]
    loss, dx, grads, ggpre, ggpost, glb, gghead, gsinks = _train_step(
        x.reshape(T, D_MODEL), loss_target.reshape(T, D_MODEL), bufs, g_pre, g_post, lb_param, g_head, sinks,
        B=B, S=S, exchange=True)
    gwt_mine = jnp.concatenate([g[0] for g in grads], axis=0).reshape(L, 2 * ri, D_MODEL)
    grad_w_out = jnp.concatenate([g[1] for g in grads], axis=0).reshape(L, 2 * ro, D_MODEL)

    d_wt, nm_wt, nv_wt = _adamw(wt, gwt_mine, mt, vt)
    grad_w_in, d_w_in, nm_w_in, nv_w_in = tr(gwt_mine), tr(d_wt), tr(nm_wt), tr(nv_wt)
    d_w_out, nm_w_out, nv_w_out = _adamw(w_out, grad_w_out, m_w_out, v_w_out)

    gsum = _all_sum_small(_pack_small(ggpre, ggpost, glb, gghead, gsinks, loss))
    gs, ds, ms, vs = _small_update(
        gsum, _pack_small(g_pre, g_post, lb_param, g_head, sinks),
        _pack_small(m_g_pre, m_g_post, m_lb_param, m_g_head, m_sinks),
        _pack_small(v_g_pre, v_g_post, v_lb_param, v_g_head, v_sinks))
    loss_all = gsum[3, HG_HEAD_DIM + ATT_HEADS]
    return (loss_all, dx.reshape(B, S, D_MODEL), grad_w_in, grad_w_out, *_unpack_small(gs),
            d_w_in, d_w_out, *_unpack_small(ds), nm_w_in, nm_w_out, *_unpack_small(ms),
            nv_w_in, nv_w_out, *_unpack_small(vs))
```

```python
import functools
import math

import jax
import jax.numpy as jnp
from jax import lax
from jax.experimental import pallas as pl
from jax.experimental.pallas import tpu as pltpu

f32 = jnp.float32
bf16 = jnp.bfloat16

D_MODEL = 1024
DEPTH = 2
HG_WIDTH = 1024
HG_HEAD_DIM = 128
HG_HEADS = 8
CHUNK = 64
SUB = 16
ATT_WIDTH = 1024
ATT_HEAD_DIM = 64
ATT_HEADS = 16
ATT_GROUP = 8
KV_WIDTH = 128
ATT_BLOCK = 128
ATT_SCALE = 1.0 / math.sqrt(ATT_HEAD_DIM)
ROPE_THETA = 10000.0
IN_WIDTH = 6400
MIX_WIDTH = 2048
NORM_EPS = 1e-6
NEG_INF = -1e30
LB_FLOOR = 1e-20
LANES = 128
VMEM_LIMIT = 48 * 1024 * 1024

ADAM_LR = 0.001
ADAM_B1 = 0.9
ADAM_B2 = 0.999
ADAM_EPS = 1e-08
ADAM_WD = 0.01
ADAM_STEP = 10

QA_BLK, ZA_BLK, KV_BLK = 4, 5, 24

NT = (((1,), (1,)), ((), ()))
TN = (((0,), (0,)), ((), ()))


def _dot(a, b, dims=None):
    if dims is None:
        return jnp.dot(a, b, preferred_element_type=f32)
    return lax.dot_general(a, b, dims, preferred_element_type=f32)


def _sigmoid(x):
    return 1.0 / (1.0 + jnp.exp(-x))


def _params(*sem):
    return pltpu.CompilerParams(dimension_semantics=sem, vmem_limit_bytes=VMEM_LIMIT)


TAIL = IN_WIDTH - 5120


def _in_proj(x, g, wt, tail, l, *, tm=1024, ride=None):
    T = x.shape[0]
    tm = min(tm, T)
    nmain = 5120 // TAIL

    def body(x_ref, g_ref, w_ref, t_ref, p_ref, h_ref, hs):
        j = pl.program_id(1)

        @pl.when(j == 0)
        def _():
            xv = x_ref[...]
            r = lax.rsqrt(jnp.mean(xv * xv, axis=-1, keepdims=True) + NORM_EPS)
            hv = (xv * r * g_ref[...]).astype(bf16)
            hs[...] = hv
            h_ref[...] = hv

        @pl.when(j < nmain)
        def _():
            p_ref[...] = _dot(hs[...], w_ref[pl.ds(pl.multiple_of(j * TAIL, TAIL), TAIL), :], NT)

        @pl.when(j == nmain)
        def _():
            p_ref[...] = _dot(hs[...], t_ref[...], NT)

    resident = pl.Buffered(1)
    return _call(
        body, (x, g, wt, tail), name="in_proj", grid=(T // tm, nmain + 1),
        in_specs=[pl.BlockSpec((tm, D_MODEL), lambda i, j: (i, 0)),
                  pl.BlockSpec((1, D_MODEL), lambda i, j: (0, 0)),
                  pl.BlockSpec((None, nmain * TAIL, D_MODEL), lambda i, j: (l, 0, 0), pipeline_mode=resident),
                  pl.BlockSpec((None, TAIL, D_MODEL), lambda i, j: (l, 0, 0), pipeline_mode=resident)],
        out_specs=[pl.BlockSpec((tm, TAIL), lambda i, j: (i, j)),
                   pl.BlockSpec((tm, D_MODEL), lambda i, j: (i, 0))],
        out_shape=[jax.ShapeDtypeStruct((T, IN_WIDTH), f32), jax.ShapeDtypeStruct((T, D_MODEL), bf16)],
        scratch_shapes=[pltpu.VMEM((tm, D_MODEL), bf16)],
        semantics=("parallel", "arbitrary"), ride=ride)


def _out_proj(ch, ca, wo, l, x, g, *, tm=512):
    T = x.shape[0]
    tm = min(tm, T)
    half = MIX_WIDTH // 2

    def body(ch_ref, ca_ref, wo_ref, x_ref, g_ref, xn_ref, y_ref):
        y = _dot(ch_ref[...], wo_ref[0:half, :]) + _dot(ca_ref[...], wo_ref[half:MIX_WIDTH, :])
        r = lax.rsqrt(jnp.mean(y * y, axis=-1, keepdims=True) + NORM_EPS)
        y_ref[...] = y
        xn_ref[...] = x_ref[...] + y * r * g_ref[...]

    row = lambda i: (i, 0)
    fixed = lambda i: (0, 0)
    return pl.pallas_call(
        body, name="out_proj", grid=(T // tm,),
        in_specs=[pl.BlockSpec((tm, half), row), pl.BlockSpec((tm, half), row),
                  pl.BlockSpec((None, MIX_WIDTH, D_MODEL), lambda i: (l, 0, 0)), pl.BlockSpec((tm, D_MODEL), row),
                  pl.BlockSpec((1, D_MODEL), fixed)],
        out_specs=[pl.BlockSpec((tm, D_MODEL), row), pl.BlockSpec((tm, D_MODEL), row)],
        out_shape=[jax.ShapeDtypeStruct((T, D_MODEL), f32)] * 2,
        compiler_params=_params("parallel"),
    )(ch, ca, wo, x, g)


def _out_proj_loss(ch, ca, wo, l, x, g, target, *, tm=512):
    T = x.shape[0]
    tm = min(tm, T)
    half = MIX_WIDTH // 2

    def body(ch_ref, ca_ref, wo_ref, x_ref, g_ref, t_ref, d_ref, y_ref, l_ref):
        @pl.when(pl.program_id(0) == 0)
        def _():
            l_ref[...] = jnp.zeros_like(l_ref)
        y = _dot(ch_ref[...], wo_ref[0:half, :]) + _dot(ca_ref[...], wo_ref[half:MIX_WIDTH, :])
        r = lax.rsqrt(jnp.mean(y * y, axis=-1, keepdims=True) + NORM_EPS)
        y_ref[...] = y
        err = (x_ref[...] + y * r * g_ref[...]) - t_ref[...]
        d_ref[...] = err * (1.0 / D_MODEL)
        l_ref[...] += jnp.sum(err * err) * (0.5 / D_MODEL)

    row = lambda i: (i, 0)
    fixed = lambda i: (0, 0)
    return pl.pallas_call(
        body, name="out_proj_loss", grid=(T // tm,),
        in_specs=[pl.BlockSpec((tm, half), row), pl.BlockSpec((tm, half), row),
                  pl.BlockSpec((None, MIX_WIDTH, D_MODEL), lambda i: (l, 0, 0)), pl.BlockSpec((tm, D_MODEL), row),
                  pl.BlockSpec((1, D_MODEL), fixed), pl.BlockSpec((tm, D_MODEL), row)],
        out_specs=[pl.BlockSpec((tm, D_MODEL), row), pl.BlockSpec((tm, D_MODEL), row),
                   pl.BlockSpec((8, LANES), fixed)],
        out_shape=[jax.ShapeDtypeStruct((T, D_MODEL), f32)] * 2 + [jax.ShapeDtypeStruct((8, LANES), f32)],
        compiler_params=_params("arbitrary"),
    )(ch, ca, wo, x, g, target)


def _out_proj_bwd(dxn, y, g, wo, l, ch, ca, *, tm=512, ride=None):
    T = y.shape[0]
    tm = min(tm, T)
    half = MIX_WIDTH // 2

    def body(dx_ref, y_ref, g_ref, wo_ref, ch_ref, ca_ref, dch_ref, dca_ref, dwo_ref, dg_ref):
        @pl.when(pl.program_id(0) == 0)
        def _():
            dwo_ref[...] = jnp.zeros_like(dwo_ref)
            dg_ref[...] = jnp.zeros_like(dg_ref)
        y = y_ref[...]
        dx = dx_ref[...]
        r = lax.rsqrt(jnp.mean(y * y, axis=-1, keepdims=True) + NORM_EPS)
        gy = dx * g_ref[...]
        dy = r * gy - y * (r * r * r) * jnp.mean(gy * y, axis=-1, keepdims=True)
        dg_ref[...] += jnp.sum(dx * y * r, axis=0, keepdims=True)
        dyb = dy.astype(bf16)
        dch_ref[...] = _dot(dyb, wo_ref[0:half, :], NT)
        dca_ref[...] = _dot(dyb, wo_ref[half:MIX_WIDTH, :], NT)
        dwo_ref[0:half, :] += _dot(ch_ref[...], dyb, TN)
        dwo_ref[half:MIX_WIDTH, :] += _dot(ca_ref[...], dyb, TN)

    row = lambda i: (i, 0)
    fixed = lambda i: (0, 0)
    return _call(
        body, (dxn, y, g, wo, ch, ca), name="out_proj_bwd", grid=(T // tm,),
        in_specs=[pl.BlockSpec((tm, D_MODEL), row), pl.BlockSpec((tm, D_MODEL), row),
                  pl.BlockSpec((1, D_MODEL), fixed), pl.BlockSpec((None, MIX_WIDTH, D_MODEL), lambda i: (l, 0, 0)),
                  pl.BlockSpec((tm, half), row), pl.BlockSpec((tm, half), row)],
        out_specs=[pl.BlockSpec((tm, half), row), pl.BlockSpec((tm, half), row),
                   pl.BlockSpec((MIX_WIDTH, D_MODEL), fixed), pl.BlockSpec((1, D_MODEL), fixed)],
        out_shape=[jax.ShapeDtypeStruct((T, half), f32), jax.ShapeDtypeStruct((T, half), f32),
                   jax.ShapeDtypeStruct((MIX_WIDTH, D_MODEL), f32), jax.ShapeDtypeStruct((1, D_MODEL), f32)],
        semantics=("arbitrary",), ride=ride)


TILE = 256
PIECE_TILES = (4, 4, 4, 4, 4, 1, 4)
PIECE_START = tuple(sum(PIECE_TILES[:p]) for p in range(len(PIECE_TILES)))
N_TILES = sum(PIECE_TILES)


def _piece_specs(rows, index):
    def spec(s, n):
        def index_map(*g):
            r, t = index(*g)
            return r, jnp.clip(t - s, 0, n - 1)
        return pl.BlockSpec((rows, TILE), index_map)
    return [spec(s, n) for s, n in zip(PIECE_START, PIECE_TILES)]


def _for_piece(t, fn):
    for p, (s, n) in enumerate(zip(PIECE_START, PIECE_TILES)):
        pl.when((t >= s) & (t < s + n))(functools.partial(fn, p))


def _in_proj_bwd(pieces, wt, l, x, g, dxn, *, tm=512, blocks=None, dx_into=None, ride=None):
    T = x.shape[0]
    tm = min(tm, T)
    first, count = blocks or (0, T // tm)
    npc = len(pieces)
    starts = [sum(p.shape[1] for p in pieces[:i]) for i in range(npc)]
    extra = [] if dx_into is None else [dx_into]

    def body(*refs):
        dp_refs = refs[:npc]
        w_ref, x_ref, g_ref, dxn_ref = refs[npc:npc + 4]
        dx_ref, dg_ref = refs[npc + 4 + len(extra):]

        @pl.when(pl.program_id(0) == 0)
        def _():
            dg_ref[...] = jnp.zeros_like(dg_ref)
        dh = None
        for p in range(npc):
            term = _dot(dp_refs[p][...], w_ref[starts[p]:starts[p] + pieces[p].shape[1], :])
            dh = term if dh is None else dh + term
        xv = x_ref[...]
        r = lax.rsqrt(jnp.mean(xv * xv, axis=-1, keepdims=True) + NORM_EPS)
        gy = dh * g_ref[...]
        dx_ref[...] = dxn_ref[...] + r * gy - xv * (r * r * r) * jnp.mean(gy * xv, axis=-1, keepdims=True)
        dg_ref[...] += jnp.sum(dh * xv * r, axis=0, keepdims=True)

    rows = lambda i: (first + i, 0)
    return _call(
        body, (*pieces, wt, x, g, dxn, *extra), name="in_proj_bwd", grid=(count,),
        in_specs=[pl.BlockSpec((tm, p.shape[1]), rows) for p in pieces] + [
            pl.BlockSpec((None, IN_WIDTH, D_MODEL), lambda i: (l, 0, 0), pipeline_mode=pl.Buffered(1)),
            pl.BlockSpec((tm, D_MODEL), rows), pl.BlockSpec((1, D_MODEL), lambda i: (0, 0)),
            pl.BlockSpec((tm, D_MODEL), rows)] + [ANY] * len(extra),
        out_specs=[pl.BlockSpec((tm, D_MODEL), rows), pl.BlockSpec((1, D_MODEL), lambda i: (0, 0))],
        out_shape=[jax.ShapeDtypeStruct((T, D_MODEL), f32), jax.ShapeDtypeStruct((1, D_MODEL), f32)],
        semantics=("arbitrary",), ride=ride, aliases={npc + 4: 0} if extra else None)


def _grad_w_in(h, pieces, *, ride=None):
    T = h.shape[0]
    npc = len(pieces)

    def body(*refs):
        h_ref, dp_refs, o_ref = refs[0], refs[1:1 + npc], refs[1 + npc]

        def put(p):
            o_ref[...] = _dot(dp_refs[p][...], h_ref[...], TN)
        _for_piece(pl.program_id(0), put)

    return _call(
        body, (h, *pieces), name="grad_w_in", grid=(N_TILES,),
        in_specs=[pl.BlockSpec((T, D_MODEL), lambda j: (0, 0), pipeline_mode=pl.Buffered(1))]
        + _piece_specs(T, lambda j: (0, j)),
        out_specs=[pl.BlockSpec((TILE, D_MODEL), lambda j: (j, 0))],
        out_shape=[jax.ShapeDtypeStruct((IN_WIDTH, D_MODEL), f32)],
        semantics=("parallel",), ride=ride)


def _lower_bound(lbp, layer):
    m = jnp.max(lbp, axis=0, keepdims=True)
    e = jnp.exp(lbp - m)
    p = e / jnp.sum(e, axis=0, keepdims=True)
    acc = p[0:1]
    for i in range(1, layer + 1):
        acc = acc + p[i:i + 1]
    return acc - p[0:1]


def _gate_parts(qr, fr, lb, lbf):
    sq = _sigmoid(qr)
    e = jnp.exp(-jnp.abs(fr))
    inv = 1.0 / (1.0 + e)
    pos = fr >= 0
    sg = jnp.where(pos, inv, e * inv)
    nsg = jnp.where(pos, e * inv, inv)
    fg = lbf + (1.0 - lb) * sg
    return qr * sq, sq, sg, nsg, fg, jnp.log(fg), (1.0 - lb) * nsg


LEVELS = tuple(SUB << j for j in range((CHUNK // SUB).bit_length() - 1))


def _level_masks(transposed=False):
    t = lax.broadcasted_iota(jnp.int32, (CHUNK, CHUNK), 1 if transposed else 0)
    s = lax.broadcasted_iota(jnp.int32, (CHUNK, CHUNK), 0 if transposed else 1)
    return [(t % (2 * m) >= m) & (s % (2 * m) < m) & (t // (2 * m) == s // (2 * m)) for m in LEVELS]


def _level_anchor(b_s, row, m):
    beta = b_s[m - 1:m, :]
    for g in range(1, CHUNK // (2 * m)):
        beta = jnp.where(row >= g * 2 * m, b_s[g * 2 * m + m - 1:g * 2 * m + m, :], beta)
    return beta


FWD_INTERLEAVE = 16
BWD_INTERLEAVE = 8
SWA_INTERLEAVE = 4


def _interleaved(chunks, width):
    for g0 in range(0, len(chunks), width):
        live = chunks[g0:g0 + width]
        while live:
            for gen in list(live):
                try:
                    next(gen)
                except StopIteration:
                    live.remove(gen)


def _seg_sum(seg, x):
    hi = x.astype(bf16)
    return _dot(seg, hi) + _dot(seg, (x - hi.astype(f32)).astype(bf16))


def _hgrn_fwd(proj, lb_param, g_head, *, B, S, layer, ride=None):
    T = B * S
    TB = min(2048, S)
    nT, NC = S // TB, TB // CHUNK
    nC = S // CHUNK
    HD = HG_HEAD_DIM

    def body(q_ref, f_ref, i_ref, z_ref, lb_ref, gh_ref, cat_ref, op_ref, st_ref,
             s_scr, b_scr, k_scr):
        @pl.when(pl.program_id(2) == 0)
        def _():
            s_scr[...] = jnp.zeros_like(s_scr)
        lb = _lower_bound(lb_ref[...], layer)
        lbf = jnp.maximum(lb, LB_FLOOR)
        gh = gh_ref[...]
        r_i = lax.broadcasted_iota(jnp.int32, (CHUNK, CHUNK), 0)
        c_i = lax.broadcasted_iota(jnp.int32, (CHUNK, CHUNK), 1)
        tril = (r_i >= c_i).astype(bf16)
        rows8 = lax.broadcasted_iota(jnp.int32, (8, HD), 0)
        row_c = lax.broadcasted_iota(jnp.int32, (CHUNK, HD), 0)
        lane_c = lax.broadcasted_iota(jnp.int32, (8, CHUNK), 1)
        masks = _level_masks()

        def chunk(c, carried):
            rs = slice(c * CHUNK, (c + 1) * CHUNK)
            b_s, k_s = b_scr.at[c], k_scr.at[c]
            q, _, _, _, _, logf, k = _gate_parts(q_ref[rs, :], f_ref[rs, :], lb, lbf)
            v = i_ref[rs, :]
            b = _seg_sum(tril, logf)
            b_s[...] = b
            k_s[...] = k
            yield
            pieces = []
            for blk in range(CHUNK // SUB):
                r0 = blk * SUB
                bp = [b[r0 + 8 * i:r0 + 8 * i + 8] for i in range(SUB // 8)]
                qp = [q[r0 + 8 * i:r0 + 8 * i + 8] for i in range(SUB // 8)]
                ap = [jnp.zeros((8, CHUNK), f32) for _ in range(SUB // 8)]
                for s in range(SUB):
                    bs = b_s[r0 + s:r0 + s + 1, :]
                    ks = k_s[r0 + s:r0 + s + 1, :]
                    for i in range(s // 8, SUB // 8):
                        diff = bp[i] - bs
                        if i == s // 8:
                            diff = jnp.where(rows8 >= s - 8 * i, diff, NEG_INF)
                        col = jnp.sum(jnp.exp(diff) * qp[i] * ks, axis=1, keepdims=True)
                        ap[i] = jnp.where(lane_c == r0 + s, col, ap[i])
                pieces += ap
                yield
            a_all = jnp.concatenate(pieces, axis=0)
            for m, mk in zip(LEVELS, masks):
                beta = _level_anchor(b_s, row_c, m)
                qh = (q * jnp.exp(jnp.minimum(b - beta, 0.0))).astype(bf16)
                kh = (k * jnp.exp(jnp.minimum(beta - b, 0.0))).astype(bf16)
                a_all = a_all + jnp.where(mk, _dot(qh, kh, NT), 0.0)
            yield
            st = carried[0]
            st_ref[0, 0, c] = st
            vb16 = v.astype(bf16)
            o = _dot(a_all.astype(bf16), vb16) + _dot((q * jnp.exp(b)).astype(bf16), st.astype(bf16), NT)
            b_end = b_s[CHUNK - 1:CHUNK, :]
            kdec = (k * jnp.exp(b_end - b)).astype(bf16)
            carried[0] = jnp.exp(b_end) * st + _dot(vb16, kdec, TN)
            yield
            rr = lax.rsqrt(jnp.mean(o * o, axis=-1, keepdims=True) + NORM_EPS)
            zr = z_ref[rs, :]
            cat_ref[rs, :] = (o * rr * gh * (zr * _sigmoid(zr))).astype(bf16)
            op_ref[rs, :] = o

        carried = [s_scr[...]]
        _interleaved([chunk(c, carried) for c in range(NC)], FWD_INTERLEAVE)
        s_scr[...] = carried[0]

    def col(part):
        return pl.BlockSpec((TB, HD), lambda b, h, n: (b * nT + n, part * HG_HEADS + h))

    out_col = pl.BlockSpec((TB, HD), lambda b, h, n: (b * nT + n, h))
    return _call(
        body, (proj, proj, proj, proj, lb_param, g_head),
        name=f"hgrn_fwd_l{layer}", grid=(B, HG_HEADS, nT),
        in_specs=[col(0), col(1), col(2), col(3),
                  pl.BlockSpec((DEPTH, HD), lambda b, h, n: (0, h)),
                  pl.BlockSpec((1, HD), lambda b, h, n: (0, 0))],
        out_specs=[out_col, out_col,
                   pl.BlockSpec((1, 1, NC, HD, HD), lambda b, h, n: (b, h, n, 0, 0))],
        out_shape=[jax.ShapeDtypeStruct((T, HG_WIDTH), bf16), jax.ShapeDtypeStruct((T, HG_WIDTH), f32),
                   jax.ShapeDtypeStruct((B, HG_HEADS, nC, HD, HD), f32)],
        scratch_shapes=[pltpu.VMEM((HD, HD), f32), pltpu.VMEM((NC, CHUNK, HD), f32), pltpu.VMEM((NC, CHUNK, HD), f32)],
        semantics=("parallel", "parallel", "arbitrary"), ride=ride)


def _hgrn_bwd(proj, lb_param, g_head, o_pre, states, dcat, *, B, S, layer, ride=None):
    T = B * S
    TB = min(2048, S)
    nT, NC = S // TB, TB // CHUNK
    HD = HG_HEAD_DIM

    def body(q_ref, f_ref, i_ref, z_ref, lb_ref, gh_ref, op_ref, st_ref, dc_ref,
             dq_ref, df_ref, di_ref, dz_ref, dlb_ref, dgh_ref,
             ds_scr, b_scr, q_scr, do_scr, wk_scr):
        @pl.when(pl.program_id(2) == 0)
        def _():
            ds_scr[...] = jnp.zeros_like(ds_scr)
            dlb_ref[...] = jnp.zeros_like(dlb_ref)
            dgh_ref[...] = jnp.zeros_like(dgh_ref)
        lb = _lower_bound(lb_ref[...], layer)
        lbf = jnp.maximum(lb, LB_FLOOR)
        ind = (lb > LB_FLOOR).astype(f32)
        gh = gh_ref[...]
        r_i = lax.broadcasted_iota(jnp.int32, (CHUNK, CHUNK), 0)
        c_i = lax.broadcasted_iota(jnp.int32, (CHUNK, CHUNK), 1)
        tril = (r_i >= c_i).astype(bf16)
        triu = (c_i >= r_i).astype(bf16)
        rows8 = lax.broadcasted_iota(jnp.int32, (8, HD), 0)
        row_c = lax.broadcasted_iota(jnp.int32, (CHUNK, HD), 0)
        lane_c = lax.broadcasted_iota(jnp.int32, (8, CHUNK), 1)
        last_row = row_c == CHUNK - 1
        masks = _level_masks()
        masks_t = _level_masks(transposed=True)
        seg_t = lax.broadcasted_iota(jnp.int32, (SUB, 8 * SUB), 0)
        seg_r = lax.broadcasted_iota(jnp.int32, (SUB, 8 * SUB), 1) // 8
        seg0 = (seg_r == seg_t).astype(bf16)
        seg1 = (seg_r[:, 0:4 * SUB] + 8 == seg_t[:, 0:4 * SUB]).astype(bf16)

        def chunk(c, carried):
            rs = slice(c * CHUNK, (c + 1) * CHUNK)
            b_s, q_s, do_s = b_scr.at[c], q_scr.at[c], do_scr.at[c]
            qr, fr = q_ref[rs, :], f_ref[rs, :]
            q, sq, sg, nsg, fg, logf, k = _gate_parts(qr, fr, lb, lbf)
            v = i_ref[rs, :]
            b = _seg_sum(tril, logf)
            o = op_ref[rs, :]
            dc = dc_ref[rs, :]
            zr = z_ref[rs, :]
            sz = _sigmoid(zr)
            rr = lax.rsqrt(jnp.mean(o * o, axis=-1, keepdims=True) + NORM_EPS)
            dz_ref[rs, :] = (dc * (o * rr * gh) * (sz * (1.0 + zr * (1.0 - sz)))).astype(bf16)
            dn = dc * (zr * sz)
            dgh_ref[0, 0] += jnp.sum(dn * o * rr, axis=0, keepdims=True)
            gdn = dn * gh
            d_o = rr * gdn - o * (rr * rr * rr) * jnp.mean(gdn * o, axis=-1, keepdims=True)
            b_s[...] = b
            q_s[...] = q
            do_s[...] = d_o
            dob = d_o.astype(bf16)
            vb16 = v.astype(bf16)
            d_a = _dot(dob, vb16, NT)
            yield
            d_q = jnp.zeros((CHUNK, HD), f32)
            d_k = jnp.zeros((CHUNK, HD), f32)
            at_all = jnp.zeros((CHUNK, CHUNK), f32)
            for m, mk, mkt in zip(LEVELS, masks, masks_t):
                beta = _level_anchor(b_s, row_c, m)
                eq = jnp.exp(jnp.minimum(b - beta, 0.0))
                ek = jnp.exp(jnp.minimum(beta - b, 0.0))
                qh = (q * eq).astype(bf16)
                kh = (k * ek).astype(bf16)
                at_all = at_all + jnp.where(mkt, _dot(kh, qh, NT), 0.0)
                d_aa = jnp.where(mk, d_a, 0.0).astype(bf16)
                d_q = d_q + _dot(d_aa, kh) * eq
                d_k = d_k + _dot(d_aa, qh, TN) * ek
            yield
            dq_blocks, dk_pieces, at_pieces = [], [], []
            for blk in range(CHUNK // SUB):
                r0 = blk * SUB
                wk = wk_scr.at[c * (CHUNK // SUB) + blk]
                bp = [b[r0 + 8 * i:r0 + 8 * i + 8] for i in range(SUB // 8)]
                kp = [k[r0 + 8 * i:r0 + 8 * i + 8] for i in range(SUB // 8)]
                vp = [v[r0 + 8 * i:r0 + 8 * i + 8] for i in range(SUB // 8)]
                dkp = [jnp.zeros((8, HD), f32) for _ in range(SUB // 8)]
                atp = [jnp.zeros((8, CHUNK), f32) for _ in range(SUB // 8)]
                for t in range(SUB):
                    bt = b_s[r0 + t:r0 + t + 1, :]
                    qt = q_s[r0 + t:r0 + t + 1, :]
                    dot_ = do_s[r0 + t:r0 + t + 1, :]
                    for i in range(t // 8 + 1):
                        diff = bt - bp[i]
                        if i == t // 8:
                            diff = jnp.where(rows8 <= t - 8 * i, diff, NEG_INF)
                        e = jnp.exp(diff)
                        a = jnp.sum(e * kp[i] * qt, axis=1, keepdims=True)
                        atp[i] = jnp.where(lane_c == r0 + t, a, atp[i])
                        w = jnp.sum(vp[i] * dot_, axis=1, keepdims=True) * e
                        dkp[i] = dkp[i] + w * qt
                        row = 8 * t if i == 0 else 8 * SUB + 8 * (t - 8)
                        wk[row:row + 8, :] = w * kp[i]
                dq_blk = _seg_sum(seg0, wk[0:8 * SUB, :])
                if SUB > 8:
                    dq_blk = dq_blk + _seg_sum(seg1, wk[8 * SUB:12 * SUB, :])
                dq_blocks.append(dq_blk)
                dk_pieces += dkp
                at_pieces += atp
                yield
            dst1 = carried[0]
            st0 = st_ref[0, 0, c]
            dst1b = dst1.astype(bf16)
            eb = jnp.exp(b)
            b_end = b_s[CHUNK - 1:CHUNK, :]
            edec = jnp.exp(b_end - b)
            e_end = jnp.exp(b_end)
            kdec = (k * edec).astype(bf16)
            qdec = (q * eb).astype(bf16)
            st1 = e_end * st0 + _dot(vb16, kdec, TN)
            rterm = jnp.sum(dst1 * st1, axis=0, keepdims=True)
            carried[0] = e_end * dst1 + _dot(dob, qdec, TN)
            d_q = d_q + _dot(dob, st0.astype(bf16)) * eb + jnp.concatenate(dq_blocks, axis=0)
            d_k = d_k + _dot(vb16, dst1b) * edec + jnp.concatenate(dk_pieces, axis=0)
            d_v = _dot(kdec, dst1b, NT) + _dot((at_all + jnp.concatenate(at_pieces, axis=0)).astype(bf16), dob)
            yield
            db = q * d_q - k * d_k + jnp.where(last_row, rterm, 0.0)
            dlt = _seg_sum(triu, db) - fg * d_k
            df_ref[rs, :] = (dlt * (1.0 - lb) * sg * nsg / fg).astype(bf16)
            dlb_ref[0] += jnp.sum(dlt * (ind - sg) / fg, axis=0, keepdims=True)
            dq_ref[rs, :] = (d_q * (sq * (1.0 + qr * (1.0 - sq)))).astype(bf16)
            di_ref[rs, :] = d_v.astype(bf16)

        carried = [ds_scr[...]]
        _interleaved([chunk(c, carried) for c in reversed(range(NC))], BWD_INTERLEAVE)
        ds_scr[...] = carried[0]

    def col(part):
        return pl.BlockSpec((TB, HD), lambda b, h, n: (b * nT + nT - 1 - n, part * HG_HEADS + h))

    hcol = pl.BlockSpec((TB, HD), lambda b, h, n: (b * nT + nT - 1 - n, h))
    return _call(
        body, (proj, proj, proj, proj, lb_param, g_head, o_pre, states, dcat),
        name=f"hgrn_bwd_l{layer}", grid=(B, HG_HEADS, nT),
        in_specs=[col(0), col(1), col(2), col(3),
                  pl.BlockSpec((DEPTH, HD), lambda b, h, n: (0, h)),
                  pl.BlockSpec((1, HD), lambda b, h, n: (0, 0)),
                  hcol,
                  pl.BlockSpec((1, 1, NC, HD, HD), lambda b, h, n: (b, h, nT - 1 - n, 0, 0)),
                  hcol],
        out_specs=[hcol, hcol, hcol, hcol,
                   pl.BlockSpec((1, 1, HD), lambda b, h, n: (b, 0, h)),
                   pl.BlockSpec((1, 1, 1, HD), lambda b, h, n: (b, h, 0, 0))],
        out_shape=[jax.ShapeDtypeStruct((T, HG_WIDTH), bf16)] * 4 + [
            jax.ShapeDtypeStruct((B, 1, HG_WIDTH), f32), jax.ShapeDtypeStruct((B, HG_HEADS, 1, HD), f32)],
        scratch_shapes=[pltpu.VMEM((HD, HD), f32)] + [pltpu.VMEM((NC, CHUNK, HD), f32)] * 3
        + [pltpu.VMEM((NC * CHUNK // SUB, 12 * SUB, HD), f32)],
        semantics=("parallel", "parallel", "arbitrary"), ride=ride)


def _rope_tables(S):
    half = ATT_HEAD_DIM // 2
    inv_freq = ROPE_THETA ** (-jnp.arange(half, dtype=f32) / half)
    ang = jnp.arange(S, dtype=f32)[:, None] * inv_freq[None, :]
    cos, sin = jnp.cos(ang), jnp.sin(ang)
    return jnp.tile(jnp.concatenate([cos, cos], axis=1), (1, 2)), jnp.tile(jnp.concatenate([-sin, sin], axis=1), (1, 2))


def _swap_halves(x, first_half):
    return jnp.where(first_half, pltpu.roll(x, LANES - ATT_HEAD_DIM // 2, 1), pltpu.roll(x, ATT_HEAD_DIM // 2, 1))


def _rope(x, cos, sin, first_half):
    return x * cos + _swap_halves(x, first_half) * sin


def _rope_bwd(dy, cos, sin, first_half):
    return dy * cos + _swap_halves(dy * sin, first_half)


def _attn_consts(n):
    lane = lax.broadcasted_iota(jnp.int32, (1, LANES), 1)
    low = lane < ATT_HEAD_DIM
    first_half = (lane % ATT_HEAD_DIM) < ATT_HEAD_DIM // 2
    top = lax.broadcasted_iota(jnp.int32, (LANES, 1), 0) < ATT_HEAD_DIM
    s = lax.broadcasted_iota(jnp.int32, (2 * ATT_BLOCK, ATT_BLOCK), 0)
    t = lax.broadcasted_iota(jnp.int32, (2 * ATT_BLOCK, ATT_BLOCK), 1)
    mask = (s > t) & (s <= t + ATT_BLOCK) & ((s >= ATT_BLOCK) | (n > 0))
    return low, first_half, top, mask


def _dup_kv(x, low):
    rolled = pltpu.roll(x, ATT_HEAD_DIM, 1)
    return [jnp.where(low, x, rolled), jnp.where(low, rolled, x)]


def _attn_head(qtm, kd, vdt, sink, mask):
    s = jnp.where(mask, _dot(kd, qtm) * ATT_SCALE, NEG_INF)
    m = jnp.maximum(jnp.max(s, axis=0, keepdims=True), sink)
    p = jnp.exp(s - m)
    psink = jnp.exp(sink - m)
    inv = 1.0 / (jnp.sum(p, axis=0, keepdims=True) + psink)
    pn = p * inv
    return pn, psink * inv, _dot(vdt, pn.astype(bf16))


def _swa_fwd(proj, sink_b, cos, sin, *, B, S, ride=None):
    T = B * S
    L = ATT_BLOCK
    nB = S // L

    def body(q_ref, z_ref, kvc_ref, kvp_ref, sk_ref, cc_ref, sc_ref, cp_ref, sp_ref, cat_ref):
        n = pl.program_id(1)
        low, first_half, top, mask = _attn_consts(n)
        cc, sc = cc_ref[...], sc_ref[...]
        kc = _rope(kvc_ref[:, 0:LANES], cc, sc, first_half)
        kp = _rope(kvp_ref[:, 0:LANES], cp_ref[...], sp_ref[...], first_half)
        kd = [x.astype(bf16) for x in _dup_kv(jnp.concatenate([kp, kc], axis=0), low)]
        vdt = [x.T.astype(bf16) for x in _dup_kv(jnp.concatenate([kvp_ref[:, LANES:2 * LANES], kvc_ref[:, LANES:2 * LANES]], axis=0), low)]
        def head_pair(pair):
            cols = slice(pair * LANES, (pair + 1) * LANES)
            j = (2 * pair) // ATT_GROUP
            qt = _rope(q_ref[:, cols], cc, sc, first_half).T
            yield
            outs = []
            for hh in range(2):
                h = 2 * pair + hh
                qtm = jnp.where(top if hh == 0 else ~top, qt, 0.0).astype(bf16)
                outs.append(_attn_head(qtm, kd[j], vdt[j], sk_ref[h:h + 1, 0:1], mask)[2])
                yield
            zp = z_ref[:, cols]
            cat_ref[:, cols] = (jnp.where(top, outs[0], outs[1]).T * (zp * _sigmoid(zp))).astype(bf16)

        _interleaved([head_pair(p) for p in range(ATT_HEADS // 2)], SWA_INTERLEAVE)

    cur = lambda b, n: (b * nB + n, 0)
    return _call(
        body, (proj, proj, proj, proj, sink_b, cos, sin, cos, sin), name="swa_fwd", grid=(B, nB),
        in_specs=[pl.BlockSpec((L, ATT_WIDTH), lambda b, n: (b * nB + n, QA_BLK)),
                  pl.BlockSpec((L, ATT_WIDTH), lambda b, n: (b * nB + n, ZA_BLK)),
                  pl.BlockSpec((L, 2 * KV_WIDTH), lambda b, n: (b * nB + n, KV_BLK)),
                  pl.BlockSpec((L, 2 * KV_WIDTH), lambda b, n: (b * nB + jnp.maximum(n - 1, 0), KV_BLK)),
                  pl.BlockSpec((ATT_HEADS, LANES), lambda b, n: (0, 0)),
                  pl.BlockSpec((L, LANES), lambda b, n: (n, 0)), pl.BlockSpec((L, LANES), lambda b, n: (n, 0)),
                  pl.BlockSpec((L, LANES), lambda b, n: (jnp.maximum(n - 1, 0), 0)),
                  pl.BlockSpec((L, LANES), lambda b, n: (jnp.maximum(n - 1, 0), 0))],
        out_specs=[pl.BlockSpec((L, ATT_WIDTH), cur)],
        out_shape=[jax.ShapeDtypeStruct((T, ATT_WIDTH), bf16)],
        semantics=("parallel", "parallel"), ride=ride)


def _swa_bwd(proj, sink_b, cos, sin, dcat, *, B, S, ride=None):
    T = B * S
    L = ATT_BLOCK
    nB = S // L

    def body(q_ref, z_ref, kvc_ref, kvp_ref, sk_ref, cc_ref, sc_ref, cp_ref, sp_ref, dc_ref,
             dq_ref, dz_ref, dkv_ref, dsk_ref, carry, ds_st, pn_st, q_st, do_st):
        step = pl.program_id(1)
        n = nB - 1 - step

        @pl.when((pl.program_id(0) == 0) & (step == 0))
        def _():
            dsk_ref[...] = jnp.zeros_like(dsk_ref)

        @pl.when(step == 0)
        def _():
            carry[...] = jnp.zeros_like(carry)
        low, first_half, top, mask = _attn_consts(n)
        cc, sc, cp, sp = cc_ref[...], sc_ref[...], cp_ref[...], sp_ref[...]
        kc = _rope(kvc_ref[:, 0:LANES], cc, sc, first_half)
        kp = _rope(kvp_ref[:, 0:LANES], cp, sp, first_half)
        kdf = _dup_kv(jnp.concatenate([kp, kc], axis=0), low)
        vdf = _dup_kv(jnp.concatenate([kvp_ref[:, LANES:2 * LANES], kvc_ref[:, LANES:2 * LANES]], axis=0), low)
        kd = [x.astype(bf16) for x in kdf]
        vd = [x.astype(bf16) for x in vdf]
        kdt = [x.T.astype(bf16) for x in kdf]
        vdt = [x.T.astype(bf16) for x in vdf]
        dkd, dvd = [], []
        def head_pair(pair):
            cols = slice(pair * LANES, (pair + 1) * LANES)
            j = (2 * pair) // ATT_GROUP
            qp = _rope(q_ref[:, cols], cc, sc, first_half)
            qt = qp.T
            zp = z_ref[:, cols]
            dc = dc_ref[:, cols]
            sz = _sigmoid(zp)
            d_o = dc * (zp * sz)
            dot_ = d_o.T
            yield
            res = []
            for hh in range(2):
                rsel = top if hh == 0 else ~top
                qtm = jnp.where(rsel, qt, 0.0).astype(bf16)
                pn, psn, o = _attn_head(qtm, kd[j], vdt[j], sk_ref[2 * pair + hh:2 * pair + hh + 1, 0:1], mask)
                res.append((rsel, pn, psn, o))
                yield
            ot = jnp.where(top, res[0][3], res[1][3])
            dz_ref[:, cols] = (dc * ot.T * (sz * (1.0 + zp * (1.0 - sz)))).astype(bf16)
            dqts = []
            for hh in range(2):
                h = 2 * pair + hh
                rsel, pn, psn, _ = res[hh]
                lsel = low if hh == 0 else ~low
                dotm = jnp.where(rsel, dot_, 0.0)
                delta = jnp.sum(dotm * ot, axis=0, keepdims=True)
                dst = (pn * (_dot(vd[j], dotm.astype(bf16)) - delta) * ATT_SCALE).astype(bf16)
                dsk_ref[h:h + 1, :] += jnp.zeros((1, LANES), f32) - jnp.sum(psn * delta)
                dqts.append(_dot(kdt[j], dst))
                g = h % ATT_GROUP
                ds_st[:, g * LANES:(g + 1) * LANES] = dst
                pn_st[:, g * LANES:(g + 1) * LANES] = pn.astype(bf16)
                q_st[g * LANES:(g + 1) * LANES, :] = jnp.where(lsel, qp, 0.0).astype(bf16)
                do_st[g * LANES:(g + 1) * LANES, :] = jnp.where(lsel, d_o, 0.0).astype(bf16)
                yield
            dq_ref[:, cols] = _rope_bwd(jnp.where(top, dqts[0], dqts[1]).T, cc, sc, first_half).astype(bf16)

        pairs_per_group = ATT_GROUP // 2
        for grp in range(ATT_HEADS // ATT_GROUP):
            _interleaved([head_pair(grp * pairs_per_group + p) for p in range(pairs_per_group)], SWA_INTERLEAVE)
            dkd.append(_dot(ds_st[...], q_st[...]))
            dvd.append(_dot(pn_st[...], do_st[...]))
        dk = [x + pltpu.roll(x, ATT_HEAD_DIM, 1) for x in dkd]
        dv = [x + pltpu.roll(x, ATT_HEAD_DIM, 1) for x in dvd]
        dk = jnp.where(low, dk[0], dk[1])
        dv = jnp.where(low, dv[0], dv[1])
        dkv_ref[:, 0:LANES] = (_rope_bwd(dk[L:2 * L], cc, sc, first_half) + carry[:, 0:LANES]).astype(bf16)
        dkv_ref[:, LANES:2 * LANES] = (dv[L:2 * L] + carry[:, LANES:2 * LANES]).astype(bf16)
        carry[:, 0:LANES] = _rope_bwd(dk[0:L], cp, sp, first_half)
        carry[:, LANES:2 * LANES] = dv[0:L]

    rev = lambda b, s: b * nB + nB - 1 - s
    revp = lambda b, s: b * nB + jnp.maximum(nB - 2 - s, 0)
    wide = lambda blk: pl.BlockSpec((L, ATT_WIDTH), lambda b, s: (rev(b, s), blk))
    tab = pl.BlockSpec((L, LANES), lambda b, s: (nB - 1 - s, 0))
    tabp = pl.BlockSpec((L, LANES), lambda b, s: (jnp.maximum(nB - 2 - s, 0), 0))
    return _call(
        body, (proj, proj, proj, proj, sink_b, cos, sin, cos, sin, dcat), name="swa_bwd", grid=(B, nB),
        in_specs=[wide(QA_BLK), wide(ZA_BLK),
                  pl.BlockSpec((L, 2 * KV_WIDTH), lambda b, s: (rev(b, s), KV_BLK)),
                  pl.BlockSpec((L, 2 * KV_WIDTH), lambda b, s: (revp(b, s), KV_BLK)),
                  pl.BlockSpec((ATT_HEADS, LANES), lambda b, s: (0, 0)),
                  tab, tab, tabp, tabp, wide(0)],
        out_specs=[wide(0), wide(0), pl.BlockSpec((L, 2 * KV_WIDTH), lambda b, s: (rev(b, s), 0)),
                   pl.BlockSpec((ATT_HEADS, LANES), lambda b, s: (0, 0))],
        out_shape=[jax.ShapeDtypeStruct((T, ATT_WIDTH), bf16), jax.ShapeDtypeStruct((T, ATT_WIDTH), bf16),
                   jax.ShapeDtypeStruct((T, 2 * KV_WIDTH), bf16), jax.ShapeDtypeStruct((ATT_HEADS, LANES), f32)],
        scratch_shapes=[pltpu.VMEM((L, 2 * KV_WIDTH), f32),
                        pltpu.VMEM((2 * L, ATT_GROUP * LANES), bf16), pltpu.VMEM((2 * L, ATT_GROUP * LANES), bf16),
                        pltpu.VMEM((ATT_GROUP * LANES, LANES), bf16), pltpu.VMEM((ATT_GROUP * LANES, LANES), bf16)],
        semantics=("arbitrary", "arbitrary"), ride=ride)


def _train_step(x, target, bufs, g_pre, g_post, lb_param, g_head, sinks, *, B, S, exchange):
    L = DEPTH
    T = x.shape[0]
    ri, ro = IN_WIDTH // 8, MIX_WIDTH // 8
    cos, sin = _rope_tables(S)
    full = [list(b) for b in bufs]
    if exchange:
        full[0][0] = _run_exchange(_gather_d2d(_run_exchange(_gather_ici(bufs[0][:1]))))[0]
    saved = []
    for l in range(L):
        wt = full[l][0].reshape(1, IN_WIDTH, D_MODEL)
        tail = jnp.concatenate([wt[:, 5376:6400], wt[:, 5120:5376]], axis=1)
        first = exchange and l == 0
        ahead = exchange and l + 1 < L
        (proj, h), wo_landed = _in_proj(x, g_pre[l:l + 1], wt, tail, 0,
                                       ride=_gather_ici(bufs[0][1:], "gather_ici_wo") if first else None)
        (ch, o_pre, states), landed = _hgrn_fwd(
            proj, lb_param, g_head[l:l + 1], B=B, S=S, layer=l,
            ride=_merge(_gather_ici(bufs[l + 1]) if ahead else None,
                        _gather_d2d(wo_landed, "gather_d2d_wo") if first else None))
        if first:
            full[0][1] = landed[-1]
            landed = landed[:-1]
        wo = full[l][1].reshape(1, MIX_WIDTH, D_MODEL)
        sink_b = jnp.broadcast_to(sinks[l][:, None], (ATT_HEADS, LANES))
        (ca,), passed = _swa_fwd(proj, sink_b, cos, sin, B=B, S=S, ride=_gather_d2d(landed) if ahead else None)
        if ahead:
            full[l + 1] = list(passed)
        if l + 1 < L:
            xn, y = _out_proj(ch, ca, wo, 0, x, g_post[l:l + 1])
        else:
            dx, y, loss = _out_proj_loss(ch, ca, wo, 0, x, g_post[l:l + 1], target)
        saved.append((x, proj, h, ch, o_pre, states, sink_b, ca, y, wt, tail, wo))
        x = xn if l + 1 < L else None

    def reduce_tail(sums, recv):
        return _run_exchange(_pair_share([_chip_sum(s, r) for s, r in zip(sums, recv)]))

    grads = [None] * L
    waiting = None
    gg_pre, gg_post, g_lb, gg_head, g_sinks = [], [], [], [], []
    for l in reversed(range(L)):
        x_in, proj, h, ch, o_pre, states, sink_b, ca, y, wt, tail, wo = saved[l]
        (dch, dca, dwo, dgpost), got = _out_proj_bwd(dx, y, g_post[l:l + 1], wo, 0, ch, ca,
                                                     ride=_pair_exchange(waiting) if waiting else None)
        sums = [_pair_add(p, r) for p, r in zip(waiting, got)] if waiting else None
        (dq, df, di, dz, dlb, dgh), recv = _hgrn_bwd(proj, lb_param, g_head[l:l + 1], o_pre, states, dch, B=B, S=S,
                                                     layer=l, ride=_chip_exchange(sums) if waiting else None)
        at_end = exchange and l == 0
        part_o = [dwo.reshape(1, 4, 2, ro, D_MODEL)]
        halves = [_chip_sum(s, r) for s, r in zip(sums, recv)] if waiting else None
        (dqa, dza, dkv, dsk), rode = _swa_bwd(proj, sink_b, cos, sin, dca, B=B, S=S,
                                             ride=_merge(_pair_exchange(part_o) if at_end else None,
                                                         _pair_share(halves) if waiting else None))
        got_o = rode[:1]
        if waiting:
            grads[l + 1] = list(rode[-len(halves):])
        pieces = [dq, df, di, dz, dqa, dkv, dza]
        sums_o = [_pair_add(part_o[0], got_o[0])] if at_end else None
        (gwt,), recv_o = _grad_w_in(h, pieces, ride=_chip_exchange(sums_o) if at_end else None)
        part_t = [gwt.reshape(1, 4, 2, ri, D_MODEL)]
        if at_end:
            tm = min(512, T // 2)
            nb = T // tm
            na = max(1, nb // 4)
            (dx_a, dg_a), got_t = _in_proj_bwd(pieces, wt, 0, x_in, g_pre[l:l + 1], dx, tm=tm, blocks=(0, na),
                                               ride=_pair_exchange(part_t))
            sums_t = [_pair_add(part_t[0], got_t[0])]
            (dx, dg_b), recv_t = _in_proj_bwd(pieces, wt, 0, x_in, g_pre[l:l + 1], dx, tm=tm, blocks=(na, nb - na),
                                              dx_into=dx_a, ride=_chip_exchange(sums_t))
            dgpre = dg_a + dg_b
            grads[0] = reduce_tail(sums_t + sums_o, recv_t + recv_o)
        else:
            (dx, dgpre), _ = _in_proj_bwd(pieces, wt, 0, x_in, g_pre[l:l + 1], dx)
            if exchange:
                waiting = part_t + part_o
            else:
                grads[l] = [gwt, dwo]
        gg_pre.append(dgpre[0])
        gg_post.append(dgpost[0])
        g_lb.append(jnp.sum(dlb, axis=(0, 1)))
        gg_head.append(jnp.sum(dgh, axis=(0, 1, 2)))
        g_sinks.append(dsk[:, 0])
    rev = lambda xs: jnp.stack(xs[::-1])
    return loss[0, 0], dx, grads, rev(gg_pre), rev(gg_post), rev(g_lb), rev(gg_head), rev(g_sinks)


MESH = pl.DeviceIdType.MESH
ANY = pl.BlockSpec(memory_space=pl.ANY)


def _place():
    x, y, c = lax.axis_index("x"), lax.axis_index("y"), lax.axis_index("c")
    return x, y, c, [(1 - x, y), (x, 1 - y), (1 - x, 1 - y)]


def _rcopy(src, dst, send, recv, k, to):
    return pltpu.make_async_remote_copy(src_ref=src, dst_ref=dst, send_sem=send.at[k], recv_sem=recv.at[k],
                                        device_id=to, device_id_type=MESH)


class _Exchange:
    def __init__(self, name, inputs, out_shapes, n_sems, plan, in_place=False):
        self.name, self.inputs, self.out_shapes = name, list(inputs), list(out_shapes)
        self.n_sems, self.plan = n_sems, plan
        self.aliases = {a: a for a in range(len(inputs))} if in_place else {}

    def start(self, ins, outs, send, recv):
        for cp in self.plan(ins, outs, send, recv)[0]:
            cp.start()

    def finish(self, ins, outs, send, recv):
        sent, arriving = self.plan(ins, outs, send, recv)
        for cp in arriving:
            cp.wait_recv()
        for cp in sent:
            cp.wait_send()

    def sems(self):
        return [pltpu.SemaphoreType.DMA((self.n_sems,)), pltpu.SemaphoreType.DMA((self.n_sems,))]


class _SemView:
    def __init__(self, sems, offset):
        self.sems, self.offset = sems, offset

    @property
    def at(self):
        return self

    def __getitem__(self, k):
        return self.sems.at[self.offset + k]


def _both(a, b):
    ai, ao = len(a.inputs), len(a.out_shapes)

    def plan(ins, outs, send, recv):
        sa, ra = a.plan(ins[:ai], outs[:ao], send, recv)
        sb, rb = b.plan(ins[ai:], outs[ao:], _SemView(send, a.n_sems), _SemView(recv, a.n_sems))
        return sa + sb, ra + rb

    ex = _Exchange(a.name + "_" + b.name, a.inputs + b.inputs, a.out_shapes + b.out_shapes, a.n_sems + b.n_sems, plan)
    ex.aliases = {**a.aliases, **{ai + i: ao + o for i, o in b.aliases.items()}}
    return ex


def _merge(*rides):
    rides = [r for r in rides if r is not None]
    return functools.reduce(_both, rides) if rides else None


def _run_exchange(ex):
    n_in, n_out = len(ex.inputs), len(ex.out_shapes)

    def body(*refs):
        ins, outs = refs[:n_in], refs[n_in:n_in + n_out]
        send, recv = refs[n_in + n_out:]
        ex.start(ins, outs, send, recv)
        ex.finish(ins, outs, send, recv)

    return pl.pallas_call(
        body, name=ex.name, in_specs=[ANY] * n_in, out_specs=[ANY] * n_out, out_shape=ex.out_shapes,
        input_output_aliases=ex.aliases, scratch_shapes=ex.sems(),
    )(*ex.inputs)


def _call(body, operands, *, name, grid, in_specs, out_specs, out_shape, scratch_shapes=(), semantics, ride=None,
          aliases=None):
    aliases = dict(aliases or {})
    if ride is None:
        outs = pl.pallas_call(body, name=name, grid=grid, in_specs=in_specs, out_specs=out_specs, out_shape=out_shape,
                              input_output_aliases=aliases, scratch_shapes=list(scratch_shapes),
                              compiler_params=_params(*semantics))(*operands)
        return outs, []
    n_in, n_out, n_scr = len(in_specs), len(out_specs), len(scratch_shapes)
    r_in, r_out = len(ride.inputs), len(ride.out_shapes)

    def riding(*refs):
        refs = list(refs)
        ins, rins = refs[:n_in], refs[n_in:n_in + r_in]
        o0 = n_in + r_in
        outs, routs = refs[o0:o0 + n_out], refs[o0 + n_out:o0 + n_out + r_out]
        scr = refs[o0 + n_out + r_out:o0 + n_out + r_out + n_scr]
        send, recv = refs[-2:]
        ids = [pl.program_id(d) for d in range(len(grid))]
        first = functools.reduce(jnp.logical_and, [i == 0 for i in ids])
        last = functools.reduce(jnp.logical_and, [i == g - 1 for i, g in zip(ids, grid)])
        pl.when(first)(lambda: ride.start(rins, routs, send, recv))
        body(*ins, *outs, *scr)
        pl.when(last)(lambda: ride.finish(rins, routs, send, recv))

    res = pl.pallas_call(
        riding, name=name + "_" + ride.name, grid=grid,
        in_specs=list(in_specs) + [ANY] * r_in, out_specs=list(out_specs) + [ANY] * r_out,
        out_shape=list(out_shape) + list(ride.out_shapes),
        input_output_aliases={**aliases, **{n_in + a: n_out + b for a, b in ride.aliases.items()}},
        scratch_shapes=list(scratch_shapes) + ride.sems(),
        compiler_params=_params(*(["arbitrary"] * len(grid))),
    )(*operands, *ride.inputs)
    return res[:n_out], res[n_out:]


def _gather_ici(bufs, name="gather_ici"):
    n = len(bufs)

    def plan(ins, outs, send, recv):
        x, y, c, chips = _place()
        me = 2 * x + y
        sent, arriving = [], []
        for j, (px, py) in enumerate(chips):
            for a in range(n):
                mine, theirs = outs[a].at[:, me, c], outs[a].at[:, 2 * px + py, c]
                sent.append(_rcopy(mine, mine, send, recv, j * n + a, (px, py, c)))
                arriving.append(_rcopy(theirs, theirs, send, recv, j * n + a, (px, py, c)))
        return sent, arriving

    return _Exchange(name, bufs, [jax.ShapeDtypeStruct(b.shape, b.dtype) for b in bufs], 3 * n, plan, in_place=True)


def _gather_d2d(bufs, name="gather_d2d"):
    n = len(bufs)

    def plan(ins, outs, send, recv):
        x, y, c, chips = _place()
        sib = (x, y, 1 - c)
        sent, arriving = [], []
        for j, (px, py) in enumerate(chips):
            for a in range(n):
                got, theirs = outs[a].at[:, 2 * px + py, c], outs[a].at[:, 2 * px + py, 1 - c]
                sent.append(_rcopy(got, got, send, recv, j * n + a, sib))
                arriving.append(_rcopy(theirs, theirs, send, recv, j * n + a, sib))
        return sent, arriving

    return _Exchange(name, bufs, [jax.ShapeDtypeStruct(b.shape, b.dtype) for b in bufs], 3 * n, plan, in_place=True)


def _pair_exchange(parts):
    n = len(parts)

    def plan(ins, outs, send, recv):
        x, y, c, _ = _place()
        cps = [_rcopy(ins[a].at[:, :, 1 - c], outs[a], send, recv, a, (x, y, 1 - c)) for a in range(n)]
        return cps, cps

    return _Exchange("pair_exchange", parts,
                     [jax.ShapeDtypeStruct(p.shape[:2] + p.shape[3:], p.dtype) for p in parts], n, plan)


def _block_rows(r):
    return r if r <= 512 else r // 2


def _pair_add(part, got):
    L, K, _, r, C = part.shape
    rows = _block_rows(r)

    def body(c_ref, a_ref, b_ref, o_ref):
        o_ref[0, 0] = (a_ref[0, 0, 0] + b_ref[0, 0]).astype(bf16)

    blk = (1, 1, rows, C)
    return pl.pallas_call(
        body, name="pair_add",
        grid_spec=pltpu.PrefetchScalarGridSpec(
            num_scalar_prefetch=1, grid=(L, K, r // rows),
            in_specs=[pl.BlockSpec((1, 1, 1, rows, C), lambda l, k, i, c: (l, k, c[0], i, 0)),
                      pl.BlockSpec(blk, lambda l, k, i, c: (l, k, i, 0))],
            out_specs=pl.BlockSpec(blk, lambda l, k, i, c: (l, k, i, 0))),
        out_shape=jax.ShapeDtypeStruct((L, K, r, C), bf16),
        compiler_params=_params("parallel", "parallel", "parallel"),
    )(jnp.reshape(lax.axis_index("c"), (1,)).astype(jnp.int32), part, got)


def _chip_exchange(sums):
    n = len(sums)

    def plan(ins, outs, send, recv):
        x, y, c, chips = _place()
        cps = []
        for j, (px, py) in enumerate(chips):
            for a in range(n):
                cps.append(_rcopy(ins[a].at[:, 2 * px + py], outs[a].at[j], send, recv, j * n + a, (px, py, c)))
        return cps, cps

    return _Exchange("chip_exchange", sums,
                     [jax.ShapeDtypeStruct((3, s.shape[0]) + s.shape[2:], s.dtype) for s in sums], 3 * n, plan)


def _chip_sum(mine, got):
    L, K, r, C = mine.shape
    rows = _block_rows(r)

    def body(p_ref, a_ref, b_ref, o_ref):
        o_ref[0, 0] = (a_ref[0, 0].astype(f32) + b_ref[0, 0].astype(f32)) + (b_ref[1, 0].astype(f32) + b_ref[2, 0].astype(f32))

    place = jnp.stack([2 * lax.axis_index("x") + lax.axis_index("y"), lax.axis_index("c")]).astype(jnp.int32)
    return pl.pallas_call(
        body, name="chip_sum",
        grid_spec=pltpu.PrefetchScalarGridSpec(
            num_scalar_prefetch=1, grid=(L, r // rows),
            in_specs=[pl.BlockSpec((1, 1, rows, C), lambda l, i, p: (l, p[0], i, 0)),
                      pl.BlockSpec((3, 1, rows, C), lambda l, i, p: (0, l, i, 0))],
            out_specs=pl.BlockSpec((1, 1, rows, C), lambda l, i, p: (l, p[1], i, 0))),
        out_shape=jax.ShapeDtypeStruct((L, 2, r, C), f32),
        compiler_params=_params("parallel", "parallel"),
    )(place, mine, got)


def _pair_share(bufs):
    n = len(bufs)

    def plan(ins, outs, send, recv):
        x, y, c, _ = _place()
        sib = (x, y, 1 - c)
        sent = [_rcopy(outs[a].at[:, c], outs[a].at[:, c], send, recv, a, sib) for a in range(n)]
        arriving = [_rcopy(outs[a].at[:, 1 - c], outs[a].at[:, 1 - c], send, recv, a, sib) for a in range(n)]
        return sent, arriving

    return _Exchange("pair_share", bufs, [jax.ShapeDtypeStruct(b.shape, b.dtype) for b in bufs], n, plan, in_place=True)


def _all_sum_small(v):
    def body(v_ref, o_ref, buf, send, recv):
        x, y, c, _ = _place()
        me = 4 * x + 2 * y + c
        buf[me] = v_ref[...]
        cps = []
        for m in range(1, 8):
            to = (x ^ (m >> 2), y ^ ((m >> 1) & 1), c ^ (m & 1))
            cps.append(_rcopy(v_ref, buf.at[me], send, recv, m - 1, to))
        for cp in cps:
            cp.start()
        for cp in cps:
            cp.wait()
        acc = buf[0]
        for d in range(1, 8):
            acc = acc + buf[d]
        o_ref[...] = acc

    vm = pl.BlockSpec(memory_space=pltpu.VMEM)
    return pl.pallas_call(
        body, name="all_sum_small", in_specs=[vm], out_specs=vm,
        out_shape=jax.ShapeDtypeStruct(v.shape, v.dtype),
        scratch_shapes=[pltpu.VMEM((8,) + v.shape, v.dtype), pltpu.SemaphoreType.DMA((7,)), pltpu.SemaphoreType.DMA((7,))],
    )(v)


def _adamw_math(w, g, m, v):
    m = ADAM_B1 * m + (1.0 - ADAM_B1) * g
    v = ADAM_B2 * v + (1.0 - ADAM_B2) * (g * g)
    m_hat = m / (1.0 - ADAM_B1 ** ADAM_STEP)
    v_hat = v / (1.0 - ADAM_B2 ** ADAM_STEP)
    return -ADAM_LR * (m_hat / (jnp.sqrt(v_hat) + ADAM_EPS) + ADAM_WD * w), m, v


def _adamw(w, g, m, v):
    L, R, C = w.shape
    rows = R // 4

    def body(w_ref, g_ref, m_ref, v_ref, d_ref, mo_ref, vo_ref):
        d_ref[...], mo_ref[...], vo_ref[...] = _adamw_math(w_ref[...], g_ref[...], m_ref[...], v_ref[...])

    blk = pl.BlockSpec((1, rows, C), lambda l, i: (l, i, 0))
    return pl.pallas_call(
        body, name="adamw", grid=(L, R // rows), in_specs=[blk] * 4, out_specs=[blk] * 3,
        out_shape=[jax.ShapeDtypeStruct(w.shape, f32)] * 3,
        compiler_params=_params("parallel", "parallel"),
    )(w, g, m, v)


def _chip_index():
    return jnp.reshape(2 * lax.axis_index("x") + lax.axis_index("y"), (1,)).astype(jnp.int32)


def _shard_placed(w, l):
    _, R, C = w.shape
    rows = R // 4

    def body(k_ref, w_ref, o_ref):
        o_ref[0, 0] = w_ref[0].astype(bf16)

    return pl.pallas_call(
        body, name="shard_placed",
        grid_spec=pltpu.PrefetchScalarGridSpec(
            num_scalar_prefetch=1, grid=(R // rows,),
            in_specs=[pl.BlockSpec((1, rows, C), lambda i, k: (l, i, 0))],
            out_specs=pl.BlockSpec((1, 1, rows, C), lambda i, k: (0, k[0], i, 0))),
        out_shape=jax.ShapeDtypeStruct((1, 4, R, C), bf16),
        compiler_params=_params("parallel"),
    )(_chip_index(), w)


def _pack_small(g_pre, g_post, lb, g_head, sinks, loss=None):
    rows = []
    for l in range(DEPTH):
        tail = [g_head[l], sinks[l]]
        if loss is not None and l == 0:
            tail.append(jnp.reshape(loss, (1,)))
        tail = jnp.concatenate(tail)
        rows += [g_pre[l], g_post[l], lb[l], jnp.pad(tail, (0, D_MODEL - tail.shape[0]))]
    return jnp.stack(rows)


def _unpack_small(p):
    g_pre = jnp.stack([p[4 * l] for l in range(DEPTH)])
    g_post = jnp.stack([p[4 * l + 1] for l in range(DEPTH)])
    lb = jnp.stack([p[4 * l + 2] for l in range(DEPTH)])
    g_head = jnp.stack([p[4 * l + 3, :HG_HEAD_DIM] for l in range(DEPTH)])
    sinks = jnp.stack([p[4 * l + 3, HG_HEAD_DIM:HG_HEAD_DIM + ATT_HEADS] for l in range(DEPTH)])
    return g_pre, g_post, lb, g_head, sinks


def _small_update(gsum, w, m, v):
    def body(g_ref, w_ref, m_ref, v_ref, go_ref, d_ref, mo_ref, vo_ref):
        g = g_ref[...]
        w = w_ref[...]
        lbp = [w[4 * l + 2:4 * l + 3] for l in range(DEPTH)]
        mx = functools.reduce(jnp.maximum, lbp)
        e = [jnp.exp(t - mx) for t in lbp]
        tot = functools.reduce(jnp.add, e)
        p = [t / tot for t in e]
        glb = [g[4 * l + 2:4 * l + 3] for l in range(DEPTH)]
        row = lax.broadcasted_iota(jnp.int32, g.shape, 0)
        for j in range(DEPTH):
            gj = jnp.zeros_like(p[0])
            for l in range(DEPTH):
                for i in range(1, l + 1):
                    gj = gj + glb[l] * p[i] * ((1.0 if i == j else 0.0) - p[j])
            g = jnp.where(row == 4 * j + 2, gj, g)
        go_ref[...] = g
        d_ref[...], mo_ref[...], vo_ref[...] = _adamw_math(w, g, m_ref[...], v_ref[...])

    vm = pl.BlockSpec(memory_space=pltpu.VMEM)
    return pl.pallas_call(
        body, name="small_update", in_specs=[vm] * 4, out_specs=[vm] * 4,
        out_shape=[jax.ShapeDtypeStruct(gsum.shape, f32)] * 4,
    )(gsum, w, m, v)


def kernel(x, w_in, w_out, g_pre, g_post, lb_param, g_head, sinks, loss_target, m_w_in, m_w_out, m_g_pre, m_g_post, m_lb_param, m_g_head, m_sinks, v_w_in, v_w_out, v_g_pre, v_g_post, v_lb_param, v_g_head, v_sinks):
    B, S, _ = x.shape
    T = B * S
    L = DEPTH
    ri, ro = IN_WIDTH // 8, MIX_WIDTH // 8
    tr = lambda a: jnp.transpose(a, (0, 2, 1))
    wt, mt, vt = tr(w_in), tr(m_w_in), tr(v_w_in)
    bufs = [[_shard_placed(wt, l).reshape(1, 4, 2, ri, D_MODEL), _shard_placed(w_out, l).reshape(1, 4, 2, ro, D_MODEL)]
            for l in range(L)]
    loss, dx, grads, ggpre, ggpost, glb, gghead, gsinks = _train_step(
        x.reshape(T, D_MODEL), loss_target.reshape(T, D_MODEL), bufs, g_pre, g_post, lb_param, g_head, sinks,
        B=B, S=S, exchange=True)
    gwt_mine = jnp.concatenate([g[0] for g in grads], axis=0).reshape(L, 2 * ri, D_MODEL)
    grad_w_out = jnp.concatenate([g[1] for g in grads], axis=0).reshape(L, 2 * ro, D_MODEL)

    d_wt, nm_wt, nv_wt = _adamw(wt, gwt_mine, mt, vt)
    grad_w_in, d_w_in, nm_w_in, nv_w_in = tr(gwt_mine), tr(d_wt), tr(nm_wt), tr(nv_wt)
    d_w_out, nm_w_out, nv_w_out = _adamw(w_out, grad_w_out, m_w_out, v_w_out)

    gsum = _all_sum_small(_pack_small(ggpre, ggpost, glb, gghead, gsinks, loss))
    gs, ds, ms, vs = _small_update(
        gsum, _pack_small(g_pre, g_post, lb_param, g_head, sinks),
        _pack_small(m_g_pre, m_g_post, m_lb_param, m_g_head, m_sinks),
        _pack_small(v_g_pre, v_g_post, v_lb_param, v_g_head, v_sinks))
    loss_all = gsum[3, HG_HEAD_DIM + ATT_HEADS]
    return (loss_all, dx.reshape(B, S, D_MODEL), grad_w_in, grad_w_out, *_unpack_small(gs),
            d_w_in, d_w_out, *_unpack_small(ds), nm_w_in, nm_w_out, *_unpack_small(ms),
            nv_w_in, nv_w_out, *_unpack_small(vs))
```

```python
import functools
import math

import jax
import jax.numpy as jnp
from jax import lax
from jax.experimental import pallas as pl
from jax.experimental.pallas import tpu as pltpu

f32 = jnp.float32
bf16 = jnp.bfloat16

D_MODEL = 1024
DEPTH = 2
HG_WIDTH = 1024
HG_HEAD_DIM = 128
HG_HEADS = 8
CHUNK = 64
SUB = 16
ATT_WIDTH = 1024
ATT_HEAD_DIM = 64
ATT_HEADS = 16
ATT_GROUP = 8
KV_WIDTH = 128
ATT_BLOCK = 128
ATT_SCALE = 1.0 / math.sqrt(ATT_HEAD_DIM)
ROPE_THETA = 10000.0
IN_WIDTH = 6400
MIX_WIDTH = 2048
NORM_EPS = 1e-6
NEG_INF = -1e30
LB_FLOOR = 1e-20
LANES = 128
VMEM_LIMIT = 48 * 1024 * 1024

ADAM_LR = 0.001
ADAM_B1 = 0.9
ADAM_B2 = 0.999
ADAM_EPS = 1e-08
ADAM_WD = 0.01
ADAM_STEP = 10

QA_BLK, ZA_BLK, KV_BLK = 4, 5, 24

NT = (((1,), (1,)), ((), ()))
TN = (((0,), (0,)), ((), ()))


def _dot(a, b, dims=None):
    if dims is None:
        return jnp.dot(a, b, preferred_element_type=f32)
    return lax.dot_general(a, b, dims, preferred_element_type=f32)


def _sigmoid(x):
    return 1.0 / (1.0 + jnp.exp(-x))


def _params(*sem):
    return pltpu.CompilerParams(dimension_semantics=sem, vmem_limit_bytes=VMEM_LIMIT)


TAIL = IN_WIDTH - 5120


def _in_proj(x, g, wt, tail, l, *, tm=1024, ride=None):
    T = x.shape[0]
    tm = min(tm, T)
    nmain = 5120 // TAIL

    def body(x_ref, g_ref, w_ref, t_ref, p_ref, h_ref, hs):
        j = pl.program_id(1)

        @pl.when(j == 0)
        def _():
            xv = x_ref[...]
            r = lax.rsqrt(jnp.mean(xv * xv, axis=-1, keepdims=True) + NORM_EPS)
            hv = (xv * r * g_ref[...]).astype(bf16)
            hs[...] = hv
            h_ref[...] = hv

        @pl.when(j < nmain)
        def _():
            p_ref[...] = _dot(hs[...], w_ref[pl.ds(pl.multiple_of(j * TAIL, TAIL), TAIL), :], NT)

        @pl.when(j == nmain)
        def _():
            p_ref[...] = _dot(hs[...], t_ref[...], NT)

    resident = pl.Buffered(1)
    return _call(
        body, (x, g, wt, tail), name="in_proj", grid=(T // tm, nmain + 1),
        in_specs=[pl.BlockSpec((tm, D_MODEL), lambda i, j: (i, 0)),
                  pl.BlockSpec((1, D_MODEL), lambda i, j: (0, 0)),
                  pl.BlockSpec((None, nmain * TAIL, D_MODEL), lambda i, j: (l, 0, 0), pipeline_mode=resident),
                  pl.BlockSpec((None, TAIL, D_MODEL), lambda i, j: (l, 0, 0), pipeline_mode=resident)],
        out_specs=[pl.BlockSpec((tm, TAIL), lambda i, j: (i, j)),
                   pl.BlockSpec((tm, D_MODEL), lambda i, j: (i, 0))],
        out_shape=[jax.ShapeDtypeStruct((T, IN_WIDTH), f32), jax.ShapeDtypeStruct((T, D_MODEL), bf16)],
        scratch_shapes=[pltpu.VMEM((tm, D_MODEL), bf16)],
        semantics=("parallel", "arbitrary"), ride=ride)


def _out_proj(ch, ca, wo, l, x, g, *, tm=512):
    T = x.shape[0]
    tm = min(tm, T)
    half = MIX_WIDTH // 2

    def body(ch_ref, ca_ref, wo_ref, x_ref, g_ref, xn_ref, y_ref):
        y = _dot(ch_ref[...], wo_ref[0:half, :]) + _dot(ca_ref[...], wo_ref[half:MIX_WIDTH, :])
        r = lax.rsqrt(jnp.mean(y * y, axis=-1, keepdims=True) + NORM_EPS)
        y_ref[...] = y
        xn_ref[...] = x_ref[...] + y * r * g_ref[...]

    row = lambda i: (i, 0)
    fixed = lambda i: (0, 0)
    return pl.pallas_call(
        body, name="out_proj", grid=(T // tm,),
        in_specs=[pl.BlockSpec((tm, half), row), pl.BlockSpec((tm, half), row),
                  pl.BlockSpec((None, MIX_WIDTH, D_MODEL), lambda i: (l, 0, 0)), pl.BlockSpec((tm, D_MODEL), row),
                  pl.BlockSpec((1, D_MODEL), fixed)],
        out_specs=[pl.BlockSpec((tm, D_MODEL), row), pl.BlockSpec((tm, D_MODEL), row)],
        out_shape=[jax.ShapeDtypeStruct((T, D_MODEL), f32)] * 2,
        compiler_params=_params("parallel"),
    )(ch, ca, wo, x, g)


def _out_proj_loss(ch, ca, wo, l, x, g, target, *, tm=512):
    T = x.shape[0]
    tm = min(tm, T)
    half = MIX_WIDTH // 2

    def body(ch_ref, ca_ref, wo_ref, x_ref, g_ref, t_ref, d_ref, y_ref, l_ref):
        @pl.when(pl.program_id(0) == 0)
        def _():
            l_ref[...] = jnp.zeros_like(l_ref)
        y = _dot(ch_ref[...], wo_ref[0:half, :]) + _dot(ca_ref[...], wo_ref[half:MIX_WIDTH, :])
        r = lax.rsqrt(jnp.mean(y * y, axis=-1, keepdims=True) + NORM_EPS)
        y_ref[...] = y
        err = (x_ref[...] + y * r * g_ref[...]) - t_ref[...]
        d_ref[...] = err * (1.0 / D_MODEL)
        l_ref[...] += jnp.sum(err * err) * (0.5 / D_MODEL)

    row = lambda i: (i, 0)
    fixed = lambda i: (0, 0)
    return pl.pallas_call(
        body, name="out_proj_loss", grid=(T // tm,),
        in_specs=[pl.BlockSpec((tm, half), row), pl.BlockSpec((tm, half), row),
                  pl.BlockSpec((None, MIX_WIDTH, D_MODEL), lambda i: (l, 0, 0)), pl.BlockSpec((tm, D_MODEL), row),
                  pl.BlockSpec((1, D_MODEL), fixed), pl.BlockSpec((tm, D_MODEL), row)],
        out_specs=[pl.BlockSpec((tm, D_MODEL), row), pl.BlockSpec((tm, D_MODEL), row),
                   pl.BlockSpec((8, LANES), fixed)],
        out_shape=[jax.ShapeDtypeStruct((T, D_MODEL), f32)] * 2 + [jax.ShapeDtypeStruct((8, LANES), f32)],
        compiler_params=_params("arbitrary"),
    )(ch, ca, wo, x, g, target)


def _out_proj_bwd(dxn, y, g, wo, l, ch, ca, *, tm=512, ride=None):
    T = y.shape[0]
    tm = min(tm, T)
    half = MIX_WIDTH // 2

    def body(dx_ref, y_ref, g_ref, wo_ref, ch_ref, ca_ref, dch_ref, dca_ref, dwo_ref, dg_ref):
        @pl.when(pl.program_id(0) == 0)
        def _():
            dwo_ref[...] = jnp.zeros_like(dwo_ref)
            dg_ref[...] = jnp.zeros_like(dg_ref)
        y = y_ref[...]
        dx = dx_ref[...]
        r = lax.rsqrt(jnp.mean(y * y, axis=-1, keepdims=True) + NORM_EPS)
        gy = dx * g_ref[...]
        dy = r * gy - y * (r * r * r) * jnp.mean(gy * y, axis=-1, keepdims=True)
        dg_ref[...] += jnp.sum(dx * y * r, axis=0, keepdims=True)
        dyb = dy.astype(bf16)
        dch_ref[...] = _dot(dyb, wo_ref[0:half, :], NT)
        dca_ref[...] = _dot(dyb, wo_ref[half:MIX_WIDTH, :], NT)
        dwo_ref[0:half, :] += _dot(ch_ref[...], dyb, TN)
        dwo_ref[half:MIX_WIDTH, :] += _dot(ca_ref[...], dyb, TN)

    row = lambda i: (i, 0)
    fixed = lambda i: (0, 0)
    return _call(
        body, (dxn, y, g, wo, ch, ca), name="out_proj_bwd", grid=(T // tm,),
        in_specs=[pl.BlockSpec((tm, D_MODEL), row), pl.BlockSpec((tm, D_MODEL), row),
                  pl.BlockSpec((1, D_MODEL), fixed), pl.BlockSpec((None, MIX_WIDTH, D_MODEL), lambda i: (l, 0, 0)),
                  pl.BlockSpec((tm, half), row), pl.BlockSpec((tm, half), row)],
        out_specs=[pl.BlockSpec((tm, half), row), pl.BlockSpec((tm, half), row),
                   pl.BlockSpec((MIX_WIDTH, D_MODEL), fixed), pl.BlockSpec((1, D_MODEL), fixed)],
        out_shape=[jax.ShapeDtypeStruct((T, half), f32), jax.ShapeDtypeStruct((T, half), f32),
                   jax.ShapeDtypeStruct((MIX_WIDTH, D_MODEL), f32), jax.ShapeDtypeStruct((1, D_MODEL), f32)],
        semantics=("arbitrary",), ride=ride)


TILE = 256
PIECE_TILES = (4, 4, 4, 4, 4, 1, 4)
PIECE_START = tuple(sum(PIECE_TILES[:p]) for p in range(len(PIECE_TILES)))
N_TILES = sum(PIECE_TILES)


def _piece_specs(rows, index):
    def spec(s, n):
        def index_map(*g):
            r, t = index(*g)
            return r, jnp.clip(t - s, 0, n - 1)
        return pl.BlockSpec((rows, TILE), index_map)
    return [spec(s, n) for s, n in zip(PIECE_START, PIECE_TILES)]


def _for_piece(t, fn):
    for p, (s, n) in enumerate(zip(PIECE_START, PIECE_TILES)):
        pl.when((t >= s) & (t < s + n))(functools.partial(fn, p))


def _in_proj_bwd(pieces, wt, l, x, g, dxn, *, tm=512, blocks=None, dx_into=None, ride=None):
    T = x.shape[0]
    tm = min(tm, T)
    first, count = blocks or (0, T // tm)
    npc = len(pieces)
    starts = [sum(p.shape[1] for p in pieces[:i]) for i in range(npc)]
    extra = [] if dx_into is None else [dx_into]

    def body(*refs):
        dp_refs = refs[:npc]
        w_ref, x_ref, g_ref, dxn_ref = refs[npc:npc + 4]
        dx_ref, dg_ref = refs[npc + 4 + len(extra):]

        @pl.when(pl.program_id(0) == 0)
        def _():
            dg_ref[...] = jnp.zeros_like(dg_ref)
        dh = None
        for p in range(npc):
            term = _dot(dp_refs[p][...], w_ref[starts[p]:starts[p] + pieces[p].shape[1], :])
            dh = term if dh is None else dh + term
        xv = x_ref[...]
        r = lax.rsqrt(jnp.mean(xv * xv, axis=-1, keepdims=True) + NORM_EPS)
        gy = dh * g_ref[...]
        dx_ref[...] = dxn_ref[...] + r * gy - xv * (r * r * r) * jnp.mean(gy * xv, axis=-1, keepdims=True)
        dg_ref[...] += jnp.sum(dh * xv * r, axis=0, keepdims=True)

    rows = lambda i: (first + i, 0)
    return _call(
        body, (*pieces, wt, x, g, dxn, *extra), name="in_proj_bwd", grid=(count,),
        in_specs=[pl.BlockSpec((tm, p.shape[1]), rows) for p in pieces] + [
            pl.BlockSpec((None, IN_WIDTH, D_MODEL), lambda i: (l, 0, 0), pipeline_mode=pl.Buffered(1)),
            pl.BlockSpec((tm, D_MODEL), rows), pl.BlockSpec((1, D_MODEL), lambda i: (0, 0)),
            pl.BlockSpec((tm, D_MODEL), rows)] + [ANY] * len(extra),
        out_specs=[pl.BlockSpec((tm, D_MODEL), rows), pl.BlockSpec((1, D_MODEL), lambda i: (0, 0))],
        out_shape=[jax.ShapeDtypeStruct((T, D_MODEL), f32), jax.ShapeDtypeStruct((1, D_MODEL), f32)],
        semantics=("arbitrary",), ride=ride, aliases={npc + 4: 0} if extra else None)


def _grad_w_in(h, pieces, *, ride=None):
    T = h.shape[0]
    npc = len(pieces)

    def body(*refs):
        h_ref, dp_refs, o_ref = refs[0], refs[1:1 + npc], refs[1 + npc]

        def put(p):
            o_ref[...] = _dot(dp_refs[p][...], h_ref[...], TN)
        _for_piece(pl.program_id(0), put)

    return _call(
        body, (h, *pieces), name="grad_w_in", grid=(N_TILES,),
        in_specs=[pl.BlockSpec((T, D_MODEL), lambda j: (0, 0), pipeline_mode=pl.Buffered(1))]
        + _piece_specs(T, lambda j: (0, j)),
        out_specs=[pl.BlockSpec((TILE, D_MODEL), lambda j: (j, 0))],
        out_shape=[jax.ShapeDtypeStruct((IN_WIDTH, D_MODEL), f32)],
        semantics=("parallel",), ride=ride)


def _lower_bound(lbp, layer):
    m = jnp.max(lbp, axis=0, keepdims=True)
    e = jnp.exp(lbp - m)
    p = e / jnp.sum(e, axis=0, keepdims=True)
    acc = p[0:1]
    for i in range(1, layer + 1):
        acc = acc + p[i:i + 1]
    return acc - p[0:1]


def _gate_parts(qr, fr, lb, lbf):
    sq = _sigmoid(qr)
    e = jnp.exp(-jnp.abs(fr))
    inv = 1.0 / (1.0 + e)
    pos = fr >= 0
    sg = jnp.where(pos, inv, e * inv)
    nsg = jnp.where(pos, e * inv, inv)
    fg = lbf + (1.0 - lb) * sg
    return qr * sq, sq, sg, nsg, fg, jnp.log(fg), (1.0 - lb) * nsg


LEVELS = tuple(SUB << j for j in range((CHUNK // SUB).bit_length() - 1))


def _level_masks(transposed=False):
    t = lax.broadcasted_iota(jnp.int32, (CHUNK, CHUNK), 1 if transposed else 0)
    s = lax.broadcasted_iota(jnp.int32, (CHUNK, CHUNK), 0 if transposed else 1)
    return [(t % (2 * m) >= m) & (s % (2 * m) < m) & (t // (2 * m) == s // (2 * m)) for m in LEVELS]


def _level_anchor(b_s, row, m):
    beta = b_s[m - 1:m, :]
    for g in range(1, CHUNK // (2 * m)):
        beta = jnp.where(row >= g * 2 * m, b_s[g * 2 * m + m - 1:g * 2 * m + m, :], beta)
    return beta


FWD_INTERLEAVE = 16
BWD_INTERLEAVE = 8
SWA_INTERLEAVE = 4


def _interleaved(chunks, width):
    for g0 in range(0, len(chunks), width):
        live = chunks[g0:g0 + width]
        while live:
            for gen in list(live):
                try:
                    next(gen)
                except StopIteration:
                    live.remove(gen)


def _seg_sum(seg, x):
    hi = x.astype(bf16)
    return _dot(seg, hi) + _dot(seg, (x - hi.astype(f32)).astype(bf16))


def _hgrn_fwd(proj, lb_param, g_head, *, B, S, layer, ride=None):
    T = B * S
    TB = min(2048, S)
    nT, NC = S // TB, TB // CHUNK
    nC = S // CHUNK
    HD = HG_HEAD_DIM

    def body(q_ref, f_ref, i_ref, z_ref, lb_ref, gh_ref, cat_ref, op_ref, st_ref,
             s_scr, b_scr, k_scr):
        @pl.when(pl.program_id(2) == 0)
        def _():
            s_scr[...] = jnp.zeros_like(s_scr)
        lb = _lower_bound(lb_ref[...], layer)
        lbf = jnp.maximum(lb, LB_FLOOR)
        gh = gh_ref[...]
        r_i = lax.broadcasted_iota(jnp.int32, (CHUNK, CHUNK), 0)
        c_i = lax.broadcasted_iota(jnp.int32, (CHUNK, CHUNK), 1)
        tril = (r_i >= c_i).astype(bf16)
        rows8 = lax.broadcasted_iota(jnp.int32, (8, HD), 0)
        row_c = lax.broadcasted_iota(jnp.int32, (CHUNK, HD), 0)
        lane_c = lax.broadcasted_iota(jnp.int32, (8, CHUNK), 1)
        masks = _level_masks()

        def chunk(c, carried):
            rs = slice(c * CHUNK, (c + 1) * CHUNK)
            b_s, k_s = b_scr.at[c], k_scr.at[c]
            q, _, _, _, _, logf, k = _gate_parts(q_ref[rs, :], f_ref[rs, :], lb, lbf)
            v = i_ref[rs, :]
            b = _seg_sum(tril, logf)
            b_s[...] = b
            k_s[...] = k
            yield
            pieces = []
            for blk in range(CHUNK // SUB):
                r0 = blk * SUB
                bp = [b[r0 + 8 * i:r0 + 8 * i + 8] for i in range(SUB // 8)]
                qp = [q[r0 + 8 * i:r0 + 8 * i + 8] for i in range(SUB // 8)]
                ap = [jnp.zeros((8, CHUNK), f32) for _ in range(SUB // 8)]
                for s in range(SUB):
                    bs = b_s[r0 + s:r0 + s + 1, :]
                    ks = k_s[r0 + s:r0 + s + 1, :]
                    for i in range(s // 8, SUB // 8):
                        diff = bp[i] - bs
                        if i == s // 8:
                            diff = jnp.where(rows8 >= s - 8 * i, diff, NEG_INF)
                        col = jnp.sum(jnp.exp(diff) * qp[i] * ks, axis=1, keepdims=True)
                        ap[i] = jnp.where(lane_c == r0 + s, col, ap[i])
                pieces += ap
                yield
            a_all = jnp.concatenate(pieces, axis=0)
            for m, mk in zip(LEVELS, masks):
                beta = _level_anchor(b_s, row_c, m)
                qh = (q * jnp.exp(jnp.minimum(b - beta, 0.0))).astype(bf16)
                kh = (k * jnp.exp(jnp.minimum(beta - b, 0.0))).astype(bf16)
                a_all = a_all + jnp.where(mk, _dot(qh, kh, NT), 0.0)
            yield
            st = carried[0]
            st_ref[0, 0, c] = st
            vb16 = v.astype(bf16)
            o = _dot(a_all.astype(bf16), vb16) + _dot((q * jnp.exp(b)).astype(bf16), st.astype(bf16), NT)
            b_end = b_s[CHUNK - 1:CHUNK, :]
            kdec = (k * jnp.exp(b_end - b)).astype(bf16)
            carried[0] = jnp.exp(b_end) * st + _dot(vb16, kdec, TN)
            yield
            rr = lax.rsqrt(jnp.mean(o * o, axis=-1, keepdims=True) + NORM_EPS)
            zr = z_ref[rs, :]
            cat_ref[rs, :] = (o * rr * gh * (zr * _sigmoid(zr))).astype(bf16)
            op_ref[rs, :] = o

        carried = [s_scr[...]]
        _interleaved([chunk(c, carried) for c in range(NC)], FWD_INTERLEAVE)
        s_scr[...] = carried[0]

    def col(part):
        return pl.BlockSpec((TB, HD), lambda b, h, n: (b * nT + n, part * HG_HEADS + h))

    out_col = pl.BlockSpec((TB, HD), lambda b, h, n: (b * nT + n, h))
    return _call(
        body, (proj, proj, proj, proj, lb_param, g_head),
        name=f"hgrn_fwd_l{layer}", grid=(B, HG_HEADS, nT),
        in_specs=[col(0), col(1), col(2), col(3),
                  pl.BlockSpec((DEPTH, HD), lambda b, h, n: (0, h)),
                  pl.BlockSpec((1, HD), lambda b, h, n: (0, 0))],
        out_specs=[out_col, out_col,
                   pl.BlockSpec((1, 1, NC, HD, HD), lambda b, h, n: (b, h, n, 0, 0))],
        out_shape=[jax.ShapeDtypeStruct((T, HG_WIDTH), bf16), jax.ShapeDtypeStruct((T, HG_WIDTH), f32),
                   jax.ShapeDtypeStruct((B, HG_HEADS, nC, HD, HD), f32)],
        scratch_shapes=[pltpu.VMEM((HD, HD), f32), pltpu.VMEM((NC, CHUNK, HD), f32), pltpu.VMEM((NC, CHUNK, HD), f32)],
        semantics=("parallel", "parallel", "arbitrary"), ride=ride)


def _hgrn_bwd(proj, lb_param, g_head, o_pre, states, dcat, *, B, S, layer, ride=None):
    T = B * S
    TB = min(2048, S)
    nT, NC = S // TB, TB // CHUNK
    HD = HG_HEAD_DIM

    def body(q_ref, f_ref, i_ref, z_ref, lb_ref, gh_ref, op_ref, st_ref, dc_ref,
             dq_ref, df_ref, di_ref, dz_ref, dlb_ref, dgh_ref,
             ds_scr, b_scr, q_scr, do_scr, wk_scr):
        @pl.when(pl.program_id(2) == 0)
        def _():
            ds_scr[...] = jnp.zeros_like(ds_scr)
            dlb_ref[...] = jnp.zeros_like(dlb_ref)
            dgh_ref[...] = jnp.zeros_like(dgh_ref)
        lb = _lower_bound(lb_ref[...], layer)
        lbf = jnp.maximum(lb, LB_FLOOR)
        ind = (lb > LB_FLOOR).astype(f32)
        gh = gh_ref[...]
        r_i = lax.broadcasted_iota(jnp.int32, (CHUNK, CHUNK), 0)
        c_i = lax.broadcasted_iota(jnp.int32, (CHUNK, CHUNK), 1)
        tril = (r_i >= c_i).astype(bf16)
        triu = (c_i >= r_i).astype(bf16)
        rows8 = lax.broadcasted_iota(jnp.int32, (8, HD), 0)
        row_c = lax.broadcasted_iota(jnp.int32, (CHUNK, HD), 0)
        lane_c = lax.broadcasted_iota(jnp.int32, (8, CHUNK), 1)
        last_row = row_c == CHUNK - 1
        masks = _level_masks()
        masks_t = _level_masks(transposed=True)
        seg_t = lax.broadcasted_iota(jnp.int32, (SUB, 8 * SUB), 0)
        seg_r = lax.broadcasted_iota(jnp.int32, (SUB, 8 * SUB), 1) // 8
        seg0 = (seg_r == seg_t).astype(bf16)
        seg1 = (seg_r[:, 0:4 * SUB] + 8 == seg_t[:, 0:4 * SUB]).astype(bf16)

        def chunk(c, carried):
            rs = slice(c * CHUNK, (c + 1) * CHUNK)
            b_s, q_s, do_s = b_scr.at[c], q_scr.at[c], do_scr.at[c]
            qr, fr = q_ref[rs, :], f_ref[rs, :]
            q, sq, sg, nsg, fg, logf, k = _gate_parts(qr, fr, lb, lbf)
            v = i_ref[rs, :]
            b = _seg_sum(tril, logf)
            o = op_ref[rs, :]
            dc = dc_ref[rs, :]
            zr = z_ref[rs, :]
            sz = _sigmoid(zr)
            rr = lax.rsqrt(jnp.mean(o * o, axis=-1, keepdims=True) + NORM_EPS)
            dz_ref[rs, :] = (dc * (o * rr * gh) * (sz * (1.0 + zr * (1.0 - sz)))).astype(bf16)
            dn = dc * (zr * sz)
            dgh_ref[0, 0] += jnp.sum(dn * o * rr, axis=0, keepdims=True)
            gdn = dn * gh
            d_o = rr * gdn - o * (rr * rr * rr) * jnp.mean(gdn * o, axis=-1, keepdims=True)
            b_s[...] = b
            q_s[...] = q
            do_s[...] = d_o
            dob = d_o.astype(bf16)
            vb16 = v.astype(bf16)
            d_a = _dot(dob, vb16, NT)
            yield
            d_q = jnp.zeros((CHUNK, HD), f32)
            d_k = jnp.zeros((CHUNK, HD), f32)
            at_all = jnp.zeros((CHUNK, CHUNK), f32)
            for m, mk, mkt in zip(LEVELS, masks, masks_t):
                beta = _level_anchor(b_s, row_c, m)
                eq = jnp.exp(jnp.minimum(b - beta, 0.0))
                ek = jnp.exp(jnp.minimum(beta - b, 0.0))
                qh = (q * eq).astype(bf16)
                kh = (k * ek).astype(bf16)
                at_all = at_all + jnp.where(mkt, _dot(kh, qh, NT), 0.0)
                d_aa = jnp.where(mk, d_a, 0.0).astype(bf16)
                d_q = d_q + _dot(d_aa, kh) * eq
                d_k = d_k + _dot(d_aa, qh, TN) * ek
            yield
            dq_blocks, dk_pieces, at_pieces = [], [], []
            for blk in range(CHUNK // SUB):
                r0 = blk * SUB
                wk = wk_scr.at[c * (CHUNK // SUB) + blk]
                bp = [b[r0 + 8 * i:r0 + 8 * i + 8] for i in range(SUB // 8)]
                kp = [k[r0 + 8 * i:r0 + 8 * i + 8] for i in range(SUB // 8)]
                vp = [v[r0 + 8 * i:r0 + 8 * i + 8] for i in range(SUB // 8)]
                dkp = [jnp.zeros((8, HD), f32) for _ in range(SUB // 8)]
                atp = [jnp.zeros((8, CHUNK), f32) for _ in range(SUB // 8)]
                for t in range(SUB):
                    bt = b_s[r0 + t:r0 + t + 1, :]
                    qt = q_s[r0 + t:r0 + t + 1, :]
                    dot_ = do_s[r0 + t:r0 + t + 1, :]
                    for i in range(t // 8 + 1):
                        diff = bt - bp[i]
                        if i == t // 8:
                            diff = jnp.where(rows8 <= t - 8 * i, diff, NEG_INF)
                        e = jnp.exp(diff)
                        a = jnp.sum(e * kp[i] * qt, axis=1, keepdims=True)
                        atp[i] = jnp.where(lane_c == r0 + t, a, atp[i])
                        w = jnp.sum(vp[i] * dot_, axis=1, keepdims=True) * e
                        dkp[i] = dkp[i] + w * qt
                        row = 8 * t if i == 0 else 8 * SUB + 8 * (t - 8)
                        wk[row:row + 8, :] = w * kp[i]
                dq_blk = _seg_sum(seg0, wk[0:8 * SUB, :])
                if SUB > 8:
                    dq_blk = dq_blk + _seg_sum(seg1, wk[8 * SUB:12 * SUB, :])
                dq_blocks.append(dq_blk)
                dk_pieces += dkp
                at_pieces += atp
                yield
            dst1 = carried[0]
            st0 = st_ref[0, 0, c]
            dst1b = dst1.astype(bf16)
            eb = jnp.exp(b)
            b_end = b_s[CHUNK - 1:CHUNK, :]
            edec = jnp.exp(b_end - b)
            e_end = jnp.exp(b_end)
            kdec = (k * edec).astype(bf16)
            qdec = (q * eb).astype(bf16)
            st1 = e_end * st0 + _dot(vb16, kdec, TN)
            rterm = jnp.sum(dst1 * st1, axis=0, keepdims=True)
            carried[0] = e_end * dst1 + _dot(dob, qdec, TN)
            d_q = d_q + _dot(dob, st0.astype(bf16)) * eb + jnp.concatenate(dq_blocks, axis=0)
            d_k = d_k + _dot(vb16, dst1b) * edec + jnp.concatenate(dk_pieces, axis=0)
            d_v = _dot(kdec, dst1b, NT) + _dot((at_all + jnp.concatenate(at_pieces, axis=0)).astype(bf16), dob)
            yield
            db = q * d_q - k * d_k + jnp.where(last_row, rterm, 0.0)
            dlt = _seg_sum(triu, db) - fg * d_k
            df_ref[rs, :] = (dlt * (1.0 - lb) * sg * nsg / fg).astype(bf16)
            dlb_ref[0] += jnp.sum(dlt * (ind - sg) / fg, axis=0, keepdims=True)
            dq_ref[rs, :] = (d_q * (sq * (1.0 + qr * (1.0 - sq)))).astype(bf16)
            di_ref[rs, :] = d_v.astype(bf16)

        carried = [ds_scr[...]]
        _interleaved([chunk(c, carried) for c in reversed(range(NC))], BWD_INTERLEAVE)
        ds_scr[...] = carried[0]

    def col(part):
        return pl.BlockSpec((TB, HD), lambda b, h, n: (b * nT + nT - 1 - n, part * HG_HEADS + h))

    hcol = pl.BlockSpec((TB, HD), lambda b, h, n: (b * nT + nT - 1 - n, h))
    return _call(
        body, (proj, proj, proj, proj, lb_param, g_head, o_pre, states, dcat),
        name=f"hgrn_bwd_l{layer}", grid=(B, HG_HEADS, nT),
        in_specs=[col(0), col(1), col(2), col(3),
                  pl.BlockSpec((DEPTH, HD), lambda b, h, n: (0, h)),
                  pl.BlockSpec((1, HD), lambda b, h, n: (0, 0)),
                  hcol,
                  pl.BlockSpec((1, 1, NC, HD, HD), lambda b, h, n: (b, h, nT - 1 - n, 0, 0)),
                  hcol],
        out_specs=[hcol, hcol, hcol, hcol,
                   pl.BlockSpec((1, 1, HD), lambda b, h, n: (b, 0, h)),
                   pl.BlockSpec((1, 1, 1, HD), lambda b, h, n: (b, h, 0, 0))],
        out_shape=[jax.ShapeDtypeStruct((T, HG_WIDTH), bf16)] * 4 + [
            jax.ShapeDtypeStruct((B, 1, HG_WIDTH), f32), jax.ShapeDtypeStruct((B, HG_HEADS, 1, HD), f32)],
        scratch_shapes=[pltpu.VMEM((HD, HD), f32)] + [pltpu.VMEM((NC, CHUNK, HD), f32)] * 3
        + [pltpu.VMEM((NC * CHUNK // SUB, 12 * SUB, HD), f32)],
        semantics=("parallel", "parallel", "arbitrary"), ride=ride)


def _rope_tables(S):
    half = ATT_HEAD_DIM // 2
    inv_freq = ROPE_THETA ** (-jnp.arange(half, dtype=f32) / half)
    ang = jnp.arange(S, dtype=f32)[:, None] * inv_freq[None, :]
    cos, sin = jnp.cos(ang), jnp.sin(ang)
    return jnp.tile(jnp.concatenate([cos, cos], axis=1), (1, 2)), jnp.tile(jnp.concatenate([-sin, sin], axis=1), (1, 2))


def _swap_halves(x, first_half):
    return jnp.where(first_half, pltpu.roll(x, LANES - ATT_HEAD_DIM // 2, 1), pltpu.roll(x, ATT_HEAD_DIM // 2, 1))


def _rope(x, cos, sin, first_half):
    return x * cos + _swap_halves(x, first_half) * sin


def _rope_bwd(dy, cos, sin, first_half):
    return dy * cos + _swap_halves(dy * sin, first_half)


def _attn_consts(n):
    lane = lax.broadcasted_iota(jnp.int32, (1, LANES), 1)
    low = lane < ATT_HEAD_DIM
    first_half = (lane % ATT_HEAD_DIM) < ATT_HEAD_DIM // 2
    top = lax.broadcasted_iota(jnp.int32, (LANES, 1), 0) < ATT_HEAD_DIM
    s = lax.broadcasted_iota(jnp.int32, (2 * ATT_BLOCK, ATT_BLOCK), 0)
    t = lax.broadcasted_iota(jnp.int32, (2 * ATT_BLOCK, ATT_BLOCK), 1)
    mask = (s > t) & (s <= t + ATT_BLOCK) & ((s >= ATT_BLOCK) | (n > 0))
    return low, first_half, top, mask


def _dup_kv(x, low):
    rolled = pltpu.roll(x, ATT_HEAD_DIM, 1)
    return [jnp.where(low, x, rolled), jnp.where(low, rolled, x)]


def _attn_head(qtm, kd, vdt, sink, mask):
    s = jnp.where(mask, _dot(kd, qtm) * ATT_SCALE, NEG_INF)
    m = jnp.maximum(jnp.max(s, axis=0, keepdims=True), sink)
    p = jnp.exp(s - m)
    psink = jnp.exp(sink - m)
    inv = 1.0 / (jnp.sum(p, axis=0, keepdims=True) + psink)
    pn = p * inv
    return pn, psink * inv, _dot(vdt, pn.astype(bf16))


def _swa_fwd(proj, sink_b, cos, sin, *, B, S, ride=None):
    T = B * S
    L = ATT_BLOCK
    nB = S // L

    def body(q_ref, z_ref, kvc_ref, kvp_ref, sk_ref, cc_ref, sc_ref, cp_ref, sp_ref, cat_ref):
        n = pl.program_id(1)
        low, first_half, top, mask = _attn_consts(n)
        cc, sc = cc_ref[...], sc_ref[...]
        kc = _rope(kvc_ref[:, 0:LANES], cc, sc, first_half)
        kp = _rope(kvp_ref[:, 0:LANES], cp_ref[...], sp_ref[...], first_half)
        kd = [x.astype(bf16) for x in _dup_kv(jnp.concatenate([kp, kc], axis=0), low)]
        vdt = [x.T.astype(bf16) for x in _dup_kv(jnp.concatenate([kvp_ref[:, LANES:2 * LANES], kvc_ref[:, LANES:2 * LANES]], axis=0), low)]
        def head_pair(pair):
            cols = slice(pair * LANES, (pair + 1) * LANES)
            j = (2 * pair) // ATT_GROUP
            qt = _rope(q_ref[:, cols], cc, sc, first_half).T
            yield
            outs = []
            for hh in range(2):
                h = 2 * pair + hh
                qtm = jnp.where(top if hh == 0 else ~top, qt, 0.0).astype(bf16)
                outs.append(_attn_head(qtm, kd[j], vdt[j], sk_ref[h:h + 1, 0:1], mask)[2])
                yield
            zp = z_ref[:, cols]
            cat_ref[:, cols] = (jnp.where(top, outs[0], outs[1]).T * (zp * _sigmoid(zp))).astype(bf16)

        _interleaved([head_pair(p) for p in range(ATT_HEADS // 2)], SWA_INTERLEAVE)

    cur = lambda b, n: (b * nB + n, 0)
    return _call(
        body, (proj, proj, proj, proj, sink_b, cos, sin, cos, sin), name="swa_fwd", grid=(B, nB),
        in_specs=[pl.BlockSpec((L, ATT_WIDTH), lambda b, n: (b * nB + n, QA_BLK)),
                  pl.BlockSpec((L, ATT_WIDTH), lambda b, n: (b * nB + n, ZA_BLK)),
                  pl.BlockSpec((L, 2 * KV_WIDTH), lambda b, n: (b * nB + n, KV_BLK)),
                  pl.BlockSpec((L, 2 * KV_WIDTH), lambda b, n: (b * nB + jnp.maximum(n - 1, 0), KV_BLK)),
                  pl.BlockSpec((ATT_HEADS, LANES), lambda b, n: (0, 0)),
                  pl.BlockSpec((L, LANES), lambda b, n: (n, 0)), pl.BlockSpec((L, LANES), lambda b, n: (n, 0)),
                  pl.BlockSpec((L, LANES), lambda b, n: (jnp.maximum(n - 1, 0), 0)),
                  pl.BlockSpec((L, LANES), lambda b, n: (jnp.maximum(n - 1, 0), 0))],
        out_specs=[pl.BlockSpec((L, ATT_WIDTH), cur)],
        out_shape=[jax.ShapeDtypeStruct((T, ATT_WIDTH), bf16)],
        semantics=("parallel", "parallel"), ride=ride)


def _swa_bwd(proj, sink_b, cos, sin, dcat, *, B, S, ride=None):
    T = B * S
    L = ATT_BLOCK
    nB = S // L

    def body(q_ref, z_ref, kvc_ref, kvp_ref, sk_ref, cc_ref, sc_ref, cp_ref, sp_ref, dc_ref,
             dq_ref, dz_ref, dkv_ref, dsk_ref, carry, ds_st, pn_st, q_st, do_st):
        step = pl.program_id(1)
        n = nB - 1 - step

        @pl.when((pl.program_id(0) == 0) & (step == 0))
        def _():
            dsk_ref[...] = jnp.zeros_like(dsk_ref)

        @pl.when(step == 0)
        def _():
            carry[...] = jnp.zeros_like(carry)
        low, first_half, top, mask = _attn_consts(n)
        cc, sc, cp, sp = cc_ref[...], sc_ref[...], cp_ref[...], sp_ref[...]
        kc = _rope(kvc_ref[:, 0:LANES], cc, sc, first_half)
        kp = _rope(kvp_ref[:, 0:LANES], cp, sp, first_half)
        kdf = _dup_kv(jnp.concatenate([kp, kc], axis=0), low)
        vdf = _dup_kv(jnp.concatenate([kvp_ref[:, LANES:2 * LANES], kvc_ref[:, LANES:2 * LANES]], axis=0), low)
        kd = [x.astype(bf16) for x in kdf]
        vd = [x.astype(bf16) for x in vdf]
        kdt = [x.T.astype(bf16) for x in kdf]
        vdt = [x.T.astype(bf16) for x in vdf]
        dkd, dvd = [], []
        def head_pair(pair):
            cols = slice(pair * LANES, (pair + 1) * LANES)
            j = (2 * pair) // ATT_GROUP
            qp = _rope(q_ref[:, cols], cc, sc, first_half)
            qt = qp.T
            zp = z_ref[:, cols]
            dc = dc_ref[:, cols]
            sz = _sigmoid(zp)
            d_o = dc * (zp * sz)
            dot_ = d_o.T
            yield
            res = []
            for hh in range(2):
                rsel = top if hh == 0 else ~top
                qtm = jnp.where(rsel, qt, 0.0).astype(bf16)
                pn, psn, o = _attn_head(qtm, kd[j], vdt[j], sk_ref[2 * pair + hh:2 * pair + hh + 1, 0:1], mask)
                res.append((rsel, pn, psn, o))
                yield
            ot = jnp.where(top, res[0][3], res[1][3])
            dz_ref[:, cols] = (dc * ot.T * (sz * (1.0 + zp * (1.0 - sz)))).astype(bf16)
            dqts = []
            for hh in range(2):
                h = 2 * pair + hh
                rsel, pn, psn, _ = res[hh]
                lsel = low if hh == 0 else ~low
                dotm = jnp.where(rsel, dot_, 0.0)
                delta = jnp.sum(dotm * ot, axis=0, keepdims=True)
                dst = (pn * (_dot(vd[j], dotm.astype(bf16)) - delta) * ATT_SCALE).astype(bf16)
                dsk_ref[h:h + 1, :] += jnp.zeros((1, LANES), f32) - jnp.sum(psn * delta)
                dqts.append(_dot(kdt[j], dst))
                g = h % ATT_GROUP
                ds_st[:, g * LANES:(g + 1) * LANES] = dst
                pn_st[:, g * LANES:(g + 1) * LANES] = pn.astype(bf16)
                q_st[g * LANES:(g + 1) * LANES, :] = jnp.where(lsel, qp, 0.0).astype(bf16)
                do_st[g * LANES:(g + 1) * LANES, :] = jnp.where(lsel, d_o, 0.0).astype(bf16)
                yield
            dq_ref[:, cols] = _rope_bwd(jnp.where(top, dqts[0], dqts[1]).T, cc, sc, first_half).astype(bf16)

        pairs_per_group = ATT_GROUP // 2
        for grp in range(ATT_HEADS // ATT_GROUP):
            _interleaved([head_pair(grp * pairs_per_group + p) for p in range(pairs_per_group)], SWA_INTERLEAVE)
            dkd.append(_dot(ds_st[...], q_st[...]))
            dvd.append(_dot(pn_st[...], do_st[...]))
        dk = [x + pltpu.roll(x, ATT_HEAD_DIM, 1) for x in dkd]
        dv = [x + pltpu.roll(x, ATT_HEAD_DIM, 1) for x in dvd]
        dk = jnp.where(low, dk[0], dk[1])
        dv = jnp.where(low, dv[0], dv[1])
        dkv_ref[:, 0:LANES] = (_rope_bwd(dk[L:2 * L], cc, sc, first_half) + carry[:, 0:LANES]).astype(bf16)
        dkv_ref[:, LANES:2 * LANES] = (dv[L:2 * L] + carry[:, LANES:2 * LANES]).astype(bf16)
        carry[:, 0:LANES] = _rope_bwd(dk[0:L], cp, sp, first_half)
        carry[:, LANES:2 * LANES] = dv[0:L]

    rev = lambda b, s: b * nB + nB - 1 - s
    revp = lambda b, s: b * nB + jnp.maximum(nB - 2 - s, 0)
    wide = lambda blk: pl.BlockSpec((L, ATT_WIDTH), lambda b, s: (rev(b, s), blk))
    tab = pl.BlockSpec((L, LANES), lambda b, s: (nB - 1 - s, 0))
    tabp = pl.BlockSpec((L, LANES), lambda b, s: (jnp.maximum(nB - 2 - s, 0), 0))
    return _call(
        body, (proj, proj, proj, proj, sink_b, cos, sin, cos, sin, dcat), name="swa_bwd", grid=(B, nB),
        in_specs=[wide(QA_BLK), wide(ZA_BLK),
                  pl.BlockSpec((L, 2 * KV_WIDTH), lambda b, s: (rev(b, s), KV_BLK)),
                  pl.BlockSpec((L, 2 * KV_WIDTH), lambda b, s: (revp(b, s), KV_BLK)),
                  pl.BlockSpec((ATT_HEADS, LANES), lambda b, s: (0, 0)),
                  tab, tab, tabp, tabp, wide(0)],
        out_specs=[wide(0), wide(0), pl.BlockSpec((L, 2 * KV_WIDTH), lambda b, s: (rev(b, s), 0)),
                   pl.BlockSpec((ATT_HEADS, LANES), lambda b, s: (0, 0))],
        out_shape=[jax.ShapeDtypeStruct((T, ATT_WIDTH), bf16), jax.ShapeDtypeStruct((T, ATT_WIDTH), bf16),
                   jax.ShapeDtypeStruct((T, 2 * KV_WIDTH), bf16), jax.ShapeDtypeStruct((ATT_HEADS, LANES), f32)],
        scratch_shapes=[pltpu.VMEM((L, 2 * KV_WIDTH), f32),
                        pltpu.VMEM((2 * L, ATT_GROUP * LANES), bf16), pltpu.VMEM((2 * L, ATT_GROUP * LANES), bf16),
                        pltpu.VMEM((ATT_GROUP * LANES, LANES), bf16), pltpu.VMEM((ATT_GROUP * LANES, LANES), bf16)],
        semantics=("arbitrary", "arbitrary"), ride=ride)


def _train_step(x, target, bufs, g_pre, g_post, lb_param, g_head, sinks, *, B, S, exchange):
    L = DEPTH
    T = x.shape[0]
    ri, ro = IN_WIDTH // 8, MIX_WIDTH // 8
    cos, sin = _rope_tables(S)
    full = [list(b) for b in bufs]
    if exchange:
        full[0][0] = _run_exchange(_gather_d2d(_run_exchange(_gather_ici(bufs[0][:1]))))[0]
    saved = []
    for l in range(L):
        wt = full[l][0].reshape(1, IN_WIDTH, D_MODEL)
        tail = jnp.concatenate([wt[:, 5376:6400], wt[:, 5120:5376]], axis=1)
        first = exchange and l == 0
        ahead = exchange and l + 1 < L
        (proj, h), wo_landed = _in_proj(x, g_pre[l:l + 1], wt, tail, 0,
                                       ride=_gather_ici(bufs[0][1:], "gather_ici_wo") if first else None)
        (ch, o_pre, states), landed = _hgrn_fwd(
            proj, lb_param, g_head[l:l + 1], B=B, S=S, layer=l,
            ride=_merge(_gather_ici(bufs[l + 1]) if ahead else None,
                        _gather_d2d(wo_landed, "gather_d2d_wo") if first else None))
        if first:
            full[0][1] = landed[-1]
            landed = landed[:-1]
        wo = full[l][1].reshape(1, MIX_WIDTH, D_MODEL)
        sink_b = jnp.broadcast_to(sinks[l][:, None], (ATT_HEADS, LANES))
        (ca,), passed = _swa_fwd(proj, sink_b, cos, sin, B=B, S=S, ride=_gather_d2d(landed) if ahead else None)
        if ahead:
            full[l + 1] = list(passed)
        if l + 1 < L:
            xn, y = _out_proj(ch, ca, wo, 0, x, g_post[l:l + 1])
        else:
            dx, y, loss = _out_proj_loss(ch, ca, wo, 0, x, g_post[l:l + 1], target)
        saved.append((x, proj, h, ch, o_pre, states, sink_b, ca, y, wt, tail, wo))
        x = xn if l + 1 < L else None

    def reduce_tail(sums, recv):
        return _run_exchange(_pair_share([_chip_sum(s, r) for s, r in zip(sums, recv)]))

    grads = [None] * L
    waiting = None
    gg_pre, gg_post, g_lb, gg_head, g_sinks = [], [], [], [], []
    for l in reversed(range(L)):
        x_in, proj, h, ch, o_pre, states, sink_b, ca, y, wt, tail, wo = saved[l]
        (dch, dca, dwo, dgpost), got = _out_proj_bwd(dx, y, g_post[l:l + 1], wo, 0, ch, ca,
                                                     ride=_pair_exchange(waiting) if waiting else None)
        sums = [_pair_add(p, r) for p, r in zip(waiting, got)] if waiting else None
        (dq, df, di, dz, dlb, dgh), recv = _hgrn_bwd(proj, lb_param, g_head[l:l + 1], o_pre, states, dch, B=B, S=S,
                                                     layer=l, ride=_chip_exchange(sums) if waiting else None)
        at_end = exchange and l == 0
        part_o = [dwo.reshape(1, 4, 2, ro, D_MODEL)]
        halves = [_chip_sum(s, r) for s, r in zip(sums, recv)] if waiting else None
        (dqa, dza, dkv, dsk), rode = _swa_bwd(proj, sink_b, cos, sin, dca, B=B, S=S,
                                             ride=_merge(_pair_exchange(part_o) if at_end else None,
                                                         _pair_share(halves) if waiting else None))
        got_o = rode[:1]
        if waiting:
            grads[l + 1] = list(rode[-len(halves):])
        pieces = [dq, df, di, dz, dqa, dkv, dza]
        sums_o = [_pair_add(part_o[0], got_o[0])] if at_end else None
        (gwt,), recv_o = _grad_w_in(h, pieces, ride=_chip_exchange(sums_o) if at_end else None)
        part_t = [gwt.reshape(1, 4, 2, ri, D_MODEL)]
        if at_end:
            got_t = _run_exchange(_pair_exchange(part_t))
            sums_t = [_pair_add(part_t[0], got_t[0])]
            (dx, dgpre), recv_t = _in_proj_bwd(pieces, wt, 0, x_in, g_pre[l:l + 1], dx, ride=_chip_exchange(sums_t))
            grads[0] = reduce_tail(sums_t + sums_o, recv_t + recv_o)
        else:
            (dx, dgpre), _ = _in_proj_bwd(pieces, wt, 0, x_in, g_pre[l:l + 1], dx)
            if exchange:
                waiting = part_t + part_o
            else:
                grads[l] = [gwt, dwo]
        gg_pre.append(dgpre[0])
        gg_post.append(dgpost[0])
        g_lb.append(jnp.sum(dlb, axis=(0, 1)))
        gg_head.append(jnp.sum(dgh, axis=(0, 1, 2)))
        g_sinks.append(dsk[:, 0])
    rev = lambda xs: jnp.stack(xs[::-1])
    return loss[0, 0], dx, grads, rev(gg_pre), rev(gg_post), rev(g_lb), rev(gg_head), rev(g_sinks)


MESH = pl.DeviceIdType.MESH
ANY = pl.BlockSpec(memory_space=pl.ANY)


def _place():
    x, y, c = lax.axis_index("x"), lax.axis_index("y"), lax.axis_index("c")
    return x, y, c, [(1 - x, y), (x, 1 - y), (1 - x, 1 - y)]


def _rcopy(src, dst, send, recv, k, to):
    return pltpu.make_async_remote_copy(src_ref=src, dst_ref=dst, send_sem=send.at[k], recv_sem=recv.at[k],
                                        device_id=to, device_id_type=MESH)


class _Exchange:
    def __init__(self, name, inputs, out_shapes, n_sems, plan, in_place=False):
        self.name, self.inputs, self.out_shapes = name, list(inputs), list(out_shapes)
        self.n_sems, self.plan = n_sems, plan
        self.aliases = {a: a for a in range(len(inputs))} if in_place else {}

    def start(self, ins, outs, send, recv):
        for cp in self.plan(ins, outs, send, recv)[0]:
            cp.start()

    def finish(self, ins, outs, send, recv):
        sent, arriving = self.plan(ins, outs, send, recv)
        for cp in arriving:
            cp.wait_recv()
        for cp in sent:
            cp.wait_send()

    def sems(self):
        return [pltpu.SemaphoreType.DMA((self.n_sems,)), pltpu.SemaphoreType.DMA((self.n_sems,))]


class _SemView:
    def __init__(self, sems, offset):
        self.sems, self.offset = sems, offset

    @property
    def at(self):
        return self

    def __getitem__(self, k):
        return self.sems.at[self.offset + k]


def _both(a, b):
    ai, ao = len(a.inputs), len(a.out_shapes)

    def plan(ins, outs, send, recv):
        sa, ra = a.plan(ins[:ai], outs[:ao], send, recv)
        sb, rb = b.plan(ins[ai:], outs[ao:], _SemView(send, a.n_sems), _SemView(recv, a.n_sems))
        return sa + sb, ra + rb

    ex = _Exchange(a.name + "_" + b.name, a.inputs + b.inputs, a.out_shapes + b.out_shapes, a.n_sems + b.n_sems, plan)
    ex.aliases = {**a.aliases, **{ai + i: ao + o for i, o in b.aliases.items()}}
    return ex


def _merge(*rides):
    rides = [r for r in rides if r is not None]
    return functools.reduce(_both, rides) if rides else None


def _run_exchange(ex):
    n_in, n_out = len(ex.inputs), len(ex.out_shapes)

    def body(*refs):
        ins, outs = refs[:n_in], refs[n_in:n_in + n_out]
        send, recv = refs[n_in + n_out:]
        ex.start(ins, outs, send, recv)
        ex.finish(ins, outs, send, recv)

    return pl.pallas_call(
        body, name=ex.name, in_specs=[ANY] * n_in, out_specs=[ANY] * n_out, out_shape=ex.out_shapes,
        input_output_aliases=ex.aliases, scratch_shapes=ex.sems(),
    )(*ex.inputs)


def _call(body, operands, *, name, grid, in_specs, out_specs, out_shape, scratch_shapes=(), semantics, ride=None,
          aliases=None):
    aliases = dict(aliases or {})
    if ride is None:
        outs = pl.pallas_call(body, name=name, grid=grid, in_specs=in_specs, out_specs=out_specs, out_shape=out_shape,
                              input_output_aliases=aliases, scratch_shapes=list(scratch_shapes),
                              compiler_params=_params(*semantics))(*operands)
        return outs, []
    n_in, n_out, n_scr = len(in_specs), len(out_specs), len(scratch_shapes)
    r_in, r_out = len(ride.inputs), len(ride.out_shapes)

    def riding(*refs):
        refs = list(refs)
        ins, rins = refs[:n_in], refs[n_in:n_in + r_in]
        o0 = n_in + r_in
        outs, routs = refs[o0:o0 + n_out], refs[o0 + n_out:o0 + n_out + r_out]
        scr = refs[o0 + n_out + r_out:o0 + n_out + r_out + n_scr]
        send, recv = refs[-2:]
        ids = [pl.program_id(d) for d in range(len(grid))]
        first = functools.reduce(jnp.logical_and, [i == 0 for i in ids])
        last = functools.reduce(jnp.logical_and, [i == g - 1 for i, g in zip(ids, grid)])
        pl.when(first)(lambda: ride.start(rins, routs, send, recv))
        body(*ins, *outs, *scr)
        pl.when(last)(lambda: ride.finish(rins, routs, send, recv))

    res = pl.pallas_call(
        riding, name=name + "_" + ride.name, grid=grid,
        in_specs=list(in_specs) + [ANY] * r_in, out_specs=list(out_specs) + [ANY] * r_out,
        out_shape=list(out_shape) + list(ride.out_shapes),
        input_output_aliases={**aliases, **{n_in + a: n_out + b for a, b in ride.aliases.items()}},
        scratch_shapes=list(scratch_shapes) + ride.sems(),
        compiler_params=_params(*(["arbitrary"] * len(grid))),
    )(*operands, *ride.inputs)
    return res[:n_out], res[n_out:]


def _gather_ici(bufs, name="gather_ici"):
    n = len(bufs)

    def plan(ins, outs, send, recv):
        x, y, c, chips = _place()
        me = 2 * x + y
        sent, arriving = [], []
        for j, (px, py) in enumerate(chips):
            for a in range(n):
                mine, theirs = outs[a].at[:, me, c], outs[a].at[:, 2 * px + py, c]
                sent.append(_rcopy(mine, mine, send, recv, j * n + a, (px, py, c)))
                arriving.append(_rcopy(theirs, theirs, send, recv, j * n + a, (px, py, c)))
        return sent, arriving

    return _Exchange(name, bufs, [jax.ShapeDtypeStruct(b.shape, b.dtype) for b in bufs], 3 * n, plan, in_place=True)


def _gather_d2d(bufs, name="gather_d2d"):
    n = len(bufs)

    def plan(ins, outs, send, recv):
        x, y, c, chips = _place()
        sib = (x, y, 1 - c)
        sent, arriving = [], []
        for j, (px, py) in enumerate(chips):
            for a in range(n):
                got, theirs = outs[a].at[:, 2 * px + py, c], outs[a].at[:, 2 * px + py, 1 - c]
                sent.append(_rcopy(got, got, send, recv, j * n + a, sib))
                arriving.append(_rcopy(theirs, theirs, send, recv, j * n + a, sib))
        return sent, arriving

    return _Exchange(name, bufs, [jax.ShapeDtypeStruct(b.shape, b.dtype) for b in bufs], 3 * n, plan, in_place=True)


def _pair_exchange(parts):
    n = len(parts)

    def plan(ins, outs, send, recv):
        x, y, c, _ = _place()
        cps = [_rcopy(ins[a].at[:, :, 1 - c], outs[a], send, recv, a, (x, y, 1 - c)) for a in range(n)]
        return cps, cps

    return _Exchange("pair_exchange", parts,
                     [jax.ShapeDtypeStruct(p.shape[:2] + p.shape[3:], p.dtype) for p in parts], n, plan)


def _block_rows(r):
    return r if r <= 512 else r // 2


def _pair_add(part, got):
    L, K, _, r, C = part.shape
    rows = _block_rows(r)

    def body(c_ref, a_ref, b_ref, o_ref):
        o_ref[0, 0] = (a_ref[0, 0, 0] + b_ref[0, 0]).astype(bf16)

    blk = (1, 1, rows, C)
    return pl.pallas_call(
        body, name="pair_add",
        grid_spec=pltpu.PrefetchScalarGridSpec(
            num_scalar_prefetch=1, grid=(L, K, r // rows),
            in_specs=[pl.BlockSpec((1, 1, 1, rows, C), lambda l, k, i, c: (l, k, c[0], i, 0)),
                      pl.BlockSpec(blk, lambda l, k, i, c: (l, k, i, 0))],
            out_specs=pl.BlockSpec(blk, lambda l, k, i, c: (l, k, i, 0))),
        out_shape=jax.ShapeDtypeStruct((L, K, r, C), bf16),
        compiler_params=_params("parallel", "parallel", "parallel"),
    )(jnp.reshape(lax.axis_index("c"), (1,)).astype(jnp.int32), part, got)


def _chip_exchange(sums):
    n = len(sums)

    def plan(ins, outs, send, recv):
        x, y, c, chips = _place()
        cps = []
        for j, (px, py) in enumerate(chips):
            for a in range(n):
                cps.append(_rcopy(ins[a].at[:, 2 * px + py], outs[a].at[j], send, recv, j * n + a, (px, py, c)))
        return cps, cps

    return _Exchange("chip_exchange", sums,
                     [jax.ShapeDtypeStruct((3, s.shape[0]) + s.shape[2:], s.dtype) for s in sums], 3 * n, plan)


def _chip_sum(mine, got):
    L, K, r, C = mine.shape
    rows = _block_rows(r)

    def body(p_ref, a_ref, b_ref, o_ref):
        o_ref[0, 0] = (a_ref[0, 0].astype(f32) + b_ref[0, 0].astype(f32)) + (b_ref[1, 0].astype(f32) + b_ref[2, 0].astype(f32))

    place = jnp.stack([2 * lax.axis_index("x") + lax.axis_index("y"), lax.axis_index("c")]).astype(jnp.int32)
    return pl.pallas_call(
        body, name="chip_sum",
        grid_spec=pltpu.PrefetchScalarGridSpec(
            num_scalar_prefetch=1, grid=(L, r // rows),
            in_specs=[pl.BlockSpec((1, 1, rows, C), lambda l, i, p: (l, p[0], i, 0)),
                      pl.BlockSpec((3, 1, rows, C), lambda l, i, p: (0, l, i, 0))],
            out_specs=pl.BlockSpec((1, 1, rows, C), lambda l, i, p: (l, p[1], i, 0))),
        out_shape=jax.ShapeDtypeStruct((L, 2, r, C), f32),
        compiler_params=_params("parallel", "parallel"),
    )(place, mine, got)


def _pair_share(bufs):
    n = len(bufs)

    def plan(ins, outs, send, recv):
        x, y, c, _ = _place()
        sib = (x, y, 1 - c)
        sent = [_rcopy(outs[a].at[:, c], outs[a].at[:, c], send, recv, a, sib) for a in range(n)]
        arriving = [_rcopy(outs[a].at[:, 1 - c], outs[a].at[:, 1 - c], send, recv, a, sib) for a in range(n)]
        return sent, arriving

    return _Exchange("pair_share", bufs, [jax.ShapeDtypeStruct(b.shape, b.dtype) for b in bufs], n, plan, in_place=True)


def _all_sum_small(v):
    def body(v_ref, o_ref, buf, send, recv):
        x, y, c, _ = _place()
        me = 4 * x + 2 * y + c
        buf[me] = v_ref[...]
        cps = []
        for m in range(1, 8):
            to = (x ^ (m >> 2), y ^ ((m >> 1) & 1), c ^ (m & 1))
            cps.append(_rcopy(v_ref, buf.at[me], send, recv, m - 1, to))
        for cp in cps:
            cp.start()
        for cp in cps:
            cp.wait()
        acc = buf[0]
        for d in range(1, 8):
            acc = acc + buf[d]
        o_ref[...] = acc

    vm = pl.BlockSpec(memory_space=pltpu.VMEM)
    return pl.pallas_call(
        body, name="all_sum_small", in_specs=[vm], out_specs=vm,
        out_shape=jax.ShapeDtypeStruct(v.shape, v.dtype),
        scratch_shapes=[pltpu.VMEM((8,) + v.shape, v.dtype), pltpu.SemaphoreType.DMA((7,)), pltpu.SemaphoreType.DMA((7,))],
    )(v)


def _adamw_math(w, g, m, v):
    m = ADAM_B1 * m + (1.0 - ADAM_B1) * g
    v = ADAM_B2 * v + (1.0 - ADAM_B2) * (g * g)
    m_hat = m / (1.0 - ADAM_B1 ** ADAM_STEP)
    v_hat = v / (1.0 - ADAM_B2 ** ADAM_STEP)
    return -ADAM_LR * (m_hat / (jnp.sqrt(v_hat) + ADAM_EPS) + ADAM_WD * w), m, v


def _adamw(w, g, m, v):
    L, R, C = w.shape
    rows = R // 4

    def body(w_ref, g_ref, m_ref, v_ref, d_ref, mo_ref, vo_ref):
        d_ref[...], mo_ref[...], vo_ref[...] = _adamw_math(w_ref[...], g_ref[...], m_ref[...], v_ref[...])

    blk = pl.BlockSpec((1, rows, C), lambda l, i: (l, i, 0))
    return pl.pallas_call(
        body, name="adamw", grid=(L, R // rows), in_specs=[blk] * 4, out_specs=[blk] * 3,
        out_shape=[jax.ShapeDtypeStruct(w.shape, f32)] * 3,
        compiler_params=_params("parallel", "parallel"),
    )(w, g, m, v)


def _chip_index():
    return jnp.reshape(2 * lax.axis_index("x") + lax.axis_index("y"), (1,)).astype(jnp.int32)


def _shard_placed(w, l):
    _, R, C = w.shape
    rows = R // 4

    def body(k_ref, w_ref, o_ref):
        o_ref[0, 0] = w_ref[0].astype(bf16)

    return pl.pallas_call(
        body, name="shard_placed",
        grid_spec=pltpu.PrefetchScalarGridSpec(
            num_scalar_prefetch=1, grid=(R // rows,),
            in_specs=[pl.BlockSpec((1, rows, C), lambda i, k: (l, i, 0))],
            out_specs=pl.BlockSpec((1, 1, rows, C), lambda i, k: (0, k[0], i, 0))),
        out_shape=jax.ShapeDtypeStruct((1, 4, R, C), bf16),
        compiler_params=_params("parallel"),
    )(_chip_index(), w)


def _pack_small(g_pre, g_post, lb, g_head, sinks, loss=None):
    rows = []
    for l in range(DEPTH):
        tail = [g_head[l], sinks[l]]
        if loss is not None and l == 0:
            tail.append(jnp.reshape(loss, (1,)))
        tail = jnp.concatenate(tail)
        rows += [g_pre[l], g_post[l], lb[l], jnp.pad(tail, (0, D_MODEL - tail.shape[0]))]
    return jnp.stack(rows)


def _unpack_small(p):
    g_pre = jnp.stack([p[4 * l] for l in range(DEPTH)])
    g_post = jnp.stack([p[4 * l + 1] for l in range(DEPTH)])
    lb = jnp.stack([p[4 * l + 2] for l in range(DEPTH)])
    g_head = jnp.stack([p[4 * l + 3, :HG_HEAD_DIM] for l in range(DEPTH)])
    sinks = jnp.stack([p[4 * l + 3, HG_HEAD_DIM:HG_HEAD_DIM + ATT_HEADS] for l in range(DEPTH)])
    return g_pre, g_post, lb, g_head, sinks


def _small_update(gsum, w, m, v):
    def body(g_ref, w_ref, m_ref, v_ref, go_ref, d_ref, mo_ref, vo_ref):
        g = g_ref[...]
        w = w_ref[...]
        lbp = [w[4 * l + 2:4 * l + 3] for l in range(DEPTH)]
        mx = functools.reduce(jnp.maximum, lbp)
        e = [jnp.exp(t - mx) for t in lbp]
        tot = functools.reduce(jnp.add, e)
        p = [t / tot for t in e]
        glb = [g[4 * l + 2:4 * l + 3] for l in range(DEPTH)]
        row = lax.broadcasted_iota(jnp.int32, g.shape, 0)
        for j in range(DEPTH):
            gj = jnp.zeros_like(p[0])
            for l in range(DEPTH):
                for i in range(1, l + 1):
                    gj = gj + glb[l] * p[i] * ((1.0 if i == j else 0.0) - p[j])
            g = jnp.where(row == 4 * j + 2, gj, g)
        go_ref[...] = g
        d_ref[...], mo_ref[...], vo_ref[...] = _adamw_math(w, g, m_ref[...], v_ref[...])

    vm = pl.BlockSpec(memory_space=pltpu.VMEM)
    return pl.pallas_call(
        body, name="small_update", in_specs=[vm] * 4, out_specs=[vm] * 4,
        out_shape=[jax.ShapeDtypeStruct(gsum.shape, f32)] * 4,
    )(gsum, w, m, v)


def kernel(x, w_in, w_out, g_pre, g_post, lb_param, g_head, sinks, loss_target, m_w_in, m_w_out, m_g_pre, m_g_post, m_lb_param, m_g_head, m_sinks, v_w_in, v_w_out, v_g_pre, v_g_post, v_lb_param, v_g_head, v_sinks):
    B, S, _ = x.shape
    T = B * S
    L = DEPTH
    ri, ro = IN_WIDTH // 8, MIX_WIDTH // 8
    tr = lambda a: jnp.transpose(a, (0, 2, 1))
    wt, mt, vt = tr(w_in), tr(m_w_in), tr(v_w_in)
    bufs = [[_shard_placed(wt, l).reshape(1, 4, 2, ri, D_MODEL), _shard_placed(w_out, l).reshape(1, 4, 2, ro, D_MODEL)]
            for l in range(L)]
    loss, dx, grads, ggpre, ggpost, glb, gghead, gsinks = _train_step(
        x.reshape(T, D_MODEL), loss_target.reshape(T, D_MODEL), bufs, g_pre, g_post, lb_param, g_head, sinks,
        B=B, S=S, exchange=True)
    gwt_mine = jnp.concatenate([g[0] for g in grads], axis=0).reshape(L, 2 * ri, D_MODEL)
    grad_w_out = jnp.concatenate([g[1] for g in grads], axis=0).reshape(L, 2 * ro, D_MODEL)

    d_wt, nm_wt, nv_wt = _adamw(wt, gwt_mine, mt, vt)
    grad_w_in, d_w_in, nm_w_in, nv_w_in = tr(gwt_mine), tr(d_wt), tr(nm_wt), tr(nv_wt)
    d_w_out, nm_w_out, nv_w_out = _adamw(w_out, grad_w_out, m_w_out, v_w_out)

    gsum = _all_sum_small(_pack_small(ggpre, ggpost, glb, gghead, gsinks, loss))
    gs, ds, ms, vs = _small_update(
        gsum, _pack_small(g_pre, g_post, lb_param, g_head, sinks),
        _pack_small(m_g_pre, m_g_post, m_lb_param, m_g_head, m_sinks),
        _pack_small(v_g_pre, v_g_post, v_lb_param, v_g_head, v_sinks))
    loss_all = gsum[3, HG_HEAD_DIM + ATT_HEADS]
    return (loss_all, dx.reshape(B, S, D_MODEL), grad_w_in, grad_w_out, *_unpack_small(gs),
            d_w_in, d_w_out, *_unpack_small(ds), nm_w_in, nm_w_out, *_unpack_small(ms),
            nv_w_in, nv_w_out, *_unpack_small(vs))
```
